```python
import math
import jax
import jax.numpy as jnp
from jax import lax
import numpy as np

D_MODEL = 1024
BATCH = 8
SEQ = 4096
DEPTH = 4

CHUNK = 64
N_META = 16
N_MIXERS = 4
EPS = 1e-6
NEG_INF = -1e30
ROPE_BASE = 10000.0
Q_BLOCK = 128

MLA_HEADS = 8
MLA_NOPE = 128
MLA_ROPE = 64
MLA_V = 128
MLA_QK = MLA_NOPE + MLA_ROPE
MLA_Q_LORA = 384
MLA_KV_LORA = 256

HGRN_EXPAND = 128
HGRN_HEADS = D_MODEL // HGRN_EXPAND
HGRN_DK = HGRN_EXPAND
HGRN_DV = D_MODEL // HGRN_HEADS
HGRN_CHUNK = 16

S5_GROUP = 16
S5_GROUPS = D_MODEL // S5_GROUP
S5_STATE = 64
S5_DT_MIN = 1e-3
S5_DT_MAX = 1e-1

RET_HEADS = 4
RET_DK = D_MODEL // RET_HEADS
RET_DV = 2 * RET_DK

FFN_HIDDEN = 2816
CONV_WIDTH = 3

N_MLA_L = (DEPTH + 3) // 4
N_HGRN_L = (DEPTH + 2) // 4
N_S5_L = (DEPTH + 1) // 4
N_RET_L = DEPTH // 4

kernel_name = "chunk_causal_hybrid_trunk"

F32 = jnp.float32


def rms_norm(x, g):
    xf = x.astype(F32)
    y = xf * lax.rsqrt(jnp.mean(xf * xf, axis=-1, keepdims=True) + EPS)
    return (y * g.astype(F32)).astype(x.dtype)


def rope_tables(n_pos, dim):
    inv_freq = 1.0 / (ROPE_BASE ** (jnp.arange(0, dim, 2, dtype=F32) / dim))
    ang = jnp.arange(n_pos, dtype=F32)[:, None] * inv_freq[None, :]
    return jnp.cos(ang), jnp.sin(ang)


def apply_rope(x, cos, sin):
    half = x.shape[-1] // 2
    x1, x2 = x[..., :half], x[..., half:]
    c = cos[None, :, None, :].astype(x.dtype)
    s = sin[None, :, None, :].astype(x.dtype)
    return jnp.concatenate([x1 * c - x2 * s, x1 * s + x2 * c], axis=-1)


def pad_front(t, n):
    return jnp.pad(t, [(0, 0), (n, 0)] + [(0, 0)] * (t.ndim - 2))


def chunk_ids(n_pos):
    real = 1 + jnp.arange(n_pos - N_META, dtype=jnp.int32) // CHUNK
    return jnp.concatenate([jnp.zeros((N_META,), jnp.int32), real])


def mla_mixer(a, w_down, g_cq, g_ckv, w_uq, w_ukv, g_qhead, g_khead, w_o, cos, sin, cid):
    B, L, _ = a.shape
    H = MLA_HEADS
    down = a @ w_down
    c_q = rms_norm(down[..., :MLA_Q_LORA], g_cq)
    c_kv = rms_norm(down[..., MLA_Q_LORA:MLA_Q_LORA + MLA_KV_LORA], g_ckv)
    k_pe = down[..., MLA_Q_LORA + MLA_KV_LORA:]
    q = (c_q @ w_uq).reshape(B, L, H, MLA_QK)
    kv = (c_kv @ w_ukv).reshape(B, L, H, MLA_NOPE + MLA_V)
    k = jnp.concatenate(
        [kv[..., :MLA_NOPE], jnp.broadcast_to(k_pe[:, :, None, :], (B, L, H, MLA_ROPE))], axis=-1)
    v = kv[..., MLA_NOPE:]
    q = rms_norm(q, g_qhead)
    k = rms_norm(k, g_khead)
    q = jnp.concatenate([q[..., :MLA_NOPE], apply_rope(q[..., MLA_NOPE:], cos, sin)], axis=-1)
    k = jnp.concatenate([k[..., :MLA_NOPE], apply_rope(k[..., MLA_NOPE:], cos, sin)], axis=-1)
    scale = MLA_QK ** -0.5

    def attend(qb, kb, vb, mask):
        s = jnp.einsum("bqhd,bkhd->bhqk", qb, kb).astype(F32) * scale
        s = jnp.where(mask[None, None], s, NEG_INF)
        p = jax.nn.softmax(s, axis=-1).astype(vb.dtype)
        return jnp.einsum("bhqk,bkhd->bqhd", p, vb)

    o_meta = attend(q[:, :N_META], k[:, :N_META], v[:, :N_META], jnp.ones((N_META, N_META), bool))
    n_real = L - N_META
    n_blk = n_real // Q_BLOCK
    q_blk = q[:, N_META:].reshape(B, n_blk, Q_BLOCK, H, MLA_QK).transpose(1, 0, 2, 3, 4)
    cid_blk = cid[N_META:].reshape(n_blk, Q_BLOCK)
    o_real = lax.map(lambda args: attend(args[0], k, v, cid[None, :] <= args[1][:, None]),
                     (q_blk, cid_blk))
    o_real = o_real.transpose(1, 0, 2, 3, 4).reshape(B, n_real, H, MLA_V)
    o = jnp.concatenate([o_meta, o_real], axis=1).reshape(B, L, H * MLA_V)
    return o @ w_o


def gla_chunkwise(q, k, v, log_f, chunk):
    B, L, H, DK = q.shape
    DV = v.shape[-1]
    n_pad = (-L) % chunk
    NC = (L + n_pad) // chunk

    def to_chunks(t):
        return pad_front(t, n_pad).reshape(B, NC, chunk, H, t.shape[-1]).transpose(1, 0, 3, 2, 4)

    qc, kc, vc, gc = map(to_chunks, (q, k, v, log_f))
    G = jnp.cumsum(gc, axis=3)
    G_last = G[..., -1:, :]
    q_dec = qc * jnp.exp(G)
    k_inv = kc * jnp.exp(-G)
    k_tail = kc * jnp.exp(G_last - G)
    causal = jnp.tril(jnp.ones((chunk, chunk), bool))
    attn = jnp.where(causal, jnp.einsum("nbhid,nbhjd->nbhij", q_dec, k_inv), 0.0)
    o_intra = jnp.einsum("nbhij,nbhjv->nbhiv", attn, vc)

    def step(state, xs):
        qd, kt, vv, gl = xs
        o_inter = jnp.einsum("bhid,bhdv->bhiv", qd, state)
        state = state * jnp.exp(gl)[:, :, 0, :, None] + jnp.einsum("bhjd,bhjv->bhdv", kt, vv)
        return state, o_inter

    _, o_inter = lax.scan(step, jnp.zeros((B, H, DK, DV), F32), (q_dec, k_tail, vc, G_last))
    o = (o_intra + o_inter).transpose(1, 0, 3, 2, 4).reshape(B, NC * chunk, H, DV)
    return o[:, n_pad:]


def hgrn2_mixer(a, w_in, lb, g_o, w_o):
    B, L, _ = a.shape
    H = HGRN_HEADS
    q, f, i_in, g = jnp.split(a @ w_in, 4, axis=-1)
    q = jax.nn.silu(q.astype(F32))
    forget = lb + (1.0 - lb) * jax.nn.sigmoid(f.astype(F32))
    log_f = jnp.log(forget)
    k = 1.0 - forget
    shp = (B, L, H, HGRN_DK)
    o = gla_chunkwise(q.reshape(shp), k.reshape(shp),
                      i_in.astype(F32).reshape(B, L, H, HGRN_DV), log_f.reshape(shp), HGRN_CHUNK)
    o = rms_norm(o, g_o).reshape(B, L, H * HGRN_DV) * jax.nn.silu(g.astype(F32))
    return o.astype(a.dtype) @ w_o


def _complex_affine_combine(e1, e2):
    a1r, a1i, b1r, b1i = e1
    a2r, a2i, b2r, b2i = e2
    return (a2r * a1r - a2i * a1i,
            a2r * a1i + a2i * a1r,
            a2r * b1r - a2i * b1i + b2r,
            a2r * b1i + a2i * b1r + b2i)


def s5_mixer(a, lam_re, lam_im, log_dt, b_re, b_im, c_re, c_im, d_skip, w_glu):
    B, L, D = a.shape
    G, P, K, C = S5_GROUPS, S5_STATE, S5_GROUP, CHUNK
    lam_re, lam_im = lam_re.astype(F32), lam_im.astype(F32)
    b_re, b_im = b_re.astype(F32), b_im.astype(F32)
    c_re, c_im = c_re.astype(F32), c_im.astype(F32)
    dt = jnp.exp(log_dt.astype(F32))[:, None]
    mag = jnp.exp(lam_re * dt)
    abar_re = mag * jnp.cos(lam_im * dt)
    abar_im = mag * jnp.sin(lam_im * dt)
    den = lam_re * lam_re + lam_im * lam_im
    zoh_re = ((abar_re - 1.0) * lam_re + abar_im * lam_im) / den
    zoh_im = (abar_im * lam_re - (abar_re - 1.0) * lam_im) / den
    bbar_re = zoh_re[..., None] * b_re - zoh_im[..., None] * b_im
    bbar_im = zoh_re[..., None] * b_im + zoh_im[..., None] * b_re
    steps = (jnp.arange(C, dtype=F32) + 1.0)[:, None, None] * dt[None]
    pmag = jnp.exp(lam_re[None] * steps)
    pow_re = pmag * jnp.cos(lam_im[None] * steps)
    pow_im = pmag * jnp.sin(lam_im[None] * steps)

    n_pad = (-L) % C
    NC = (L + n_pad) // C
    u = pad_front(a.astype(F32), n_pad).reshape(B, NC, C, G, K).transpose(1, 0, 2, 3, 4)

    def step(carry, uc):
        xr0, xi0 = carry
        bu_re = jnp.einsum("bcgk,gpk->bcgp", uc, bbar_re)
        bu_im = jnp.einsum("bcgk,gpk->bcgp", uc, bbar_im)
        ar = jnp.broadcast_to(abar_re, bu_re.shape)
        ai = jnp.broadcast_to(abar_im, bu_im.shape)
        _, _, xr, xi = lax.associative_scan(_complex_affine_combine, (ar, ai, bu_re, bu_im), axis=1)
        xr = xr + pow_re * xr0[:, None] - pow_im * xi0[:, None]
        xi = xi + pow_re * xi0[:, None] + pow_im * xr0[:, None]
        y = jnp.einsum("bcgp,gkp->bcgk", xr, c_re) - jnp.einsum("bcgp,gkp->bcgk", xi, c_im)
        return (xr[:, -1], xi[:, -1]), y

    zeros = jnp.zeros((B, G, P), F32)
    _, y = lax.scan(step, (zeros, zeros), u)
    y = y.transpose(1, 0, 2, 3, 4).reshape(B, NC * C, D)[:, n_pad:]
    y = jax.nn.gelu(y + d_skip.astype(F32) * a.astype(F32))
    val, gate = jnp.split(y.astype(a.dtype) @ w_glu, 2, axis=-1)
    return val * jax.nn.sigmoid(gate)


def retention_mixer(a, w_in, gn_g, w_o, cos, sin):
    B, L, _ = a.shape
    H, DK, DV, C = RET_HEADS, RET_DK, RET_DV, CHUNK
    qk_w = H * DK
    q, k, v, g = jnp.split(a @ w_in, [qk_w, 2 * qk_w, 2 * qk_w + H * DV], axis=-1)
    q = apply_rope(q.reshape(B, L, H, DK), cos, sin).astype(F32)
    k = apply_rope(k.reshape(B, L, H, DK), cos, sin).astype(F32) * (DK ** -0.5)
    v = v.reshape(B, L, H, DV).astype(F32)
    n_pad = (-L) % C
    NC = (L + n_pad) // C

    def to_chunks(t):
        return pad_front(t, n_pad).reshape(B, NC, C, H, t.shape[-1]).transpose(1, 0, 3, 2, 4)

    qc, kc, vc = map(to_chunks, (q, k, v))
    log_gamma = jnp.log(1.0 - jnp.exp2(-5.0 - jnp.arange(H, dtype=F32)))
    pos = jnp.arange(C, dtype=F32)
    diff = pos[:, None] - pos[None, :]
    decay = jnp.where(diff >= 0, jnp.exp(diff[None] * log_gamma[:, None, None]), 0.0)
    scores = jnp.einsum("nbhid,nbhjd->nbhij", qc, kc) * decay
    o_intra = jnp.einsum("nbhij,nbhjv->nbhiv", scores, vc)
    q_decay = jnp.exp((pos[None, :] + 1.0) * log_gamma[:, None])[..., None]
    k_decay = jnp.exp((C - 1.0 - pos[None, :]) * log_gamma[:, None])[..., None]
    chunk_decay = jnp.exp(C * log_gamma)[:, None, None]

    def step(state, xs):
        qi, ki, vi = xs
        o_inter = jnp.einsum("bhid,bhdv->bhiv", qi * q_decay, state)
        state = state * chunk_decay + jnp.einsum("bhjd,bhjv->bhdv", ki * k_decay, vi)
        return state, o_inter

    _, o_inter = lax.scan(step, jnp.zeros((B, H, DK, DV), F32), (qc, kc, vc))
    o = (o_intra + o_inter).transpose(1, 0, 3, 2, 4).reshape(B, NC * C, H, DV)[:, n_pad:]
    mu = jnp.mean(o, axis=-1, keepdims=True)
    var = jnp.mean(jnp.square(o - mu), axis=-1, keepdims=True)
    o = ((o - mu) * lax.rsqrt(var + EPS)).reshape(B, L, H * DV)
    o = o * gn_g.astype(F32) * jax.nn.silu(g.astype(F32))
    return o.astype(a.dtype) @ w_o


def conv_glu_ffn(a, w_up, conv_w, conv_b, w_down):
    u = a @ w_up
    u = lax.conv_general_dilated(
        u, conv_w[:, None, :].astype(u.dtype), window_strides=(1,),
        padding=[(CONV_WIDTH - 1, 0)], dimension_numbers=("NWC", "WIO", "NWC"),
        feature_group_count=u.shape[-1]) + conv_b.astype(u.dtype)
    gate, val = jnp.split(u, 2, axis=-1)
    return (jax.nn.silu(gate) * val) @ w_down


def _fwd_setup_inputs(seed: int = 0) -> dict:
    key = jax.random.key(seed)
    ks = iter(jax.random.split(key, 40))
    D = D_MODEL

    def nrm(shape, scale):
        return jax.random.normal(next(ks), shape, F32) * scale

    def gain(shape):
        return 1.0 + 0.02 * jax.random.normal(next(ks), shape, F32)

    n_a, n_b, n_c, n_d = N_MLA_L, N_HGRN_L, N_S5_L, N_RET_L
    return {
        "x": nrm((BATCH, SEQ, D), 1.0),
        "meta_tokens": nrm((N_META, D), 1.0),
        "norm_mix_g": gain((DEPTH, D)),
        "norm_ffn_g": gain((DEPTH, D)),
        "mla_w_down": nrm((n_a, D, MLA_Q_LORA + MLA_KV_LORA + MLA_ROPE), D ** -0.5),
        "mla_cq_norm_g": gain((n_a, MLA_Q_LORA)),
        "mla_ckv_norm_g": gain((n_a, MLA_KV_LORA)),
        "mla_w_uq": nrm((n_a, MLA_Q_LORA, MLA_HEADS * MLA_QK), MLA_Q_LORA ** -0.5),
        "mla_w_ukv": nrm((n_a, MLA_KV_LORA, MLA_HEADS * (MLA_NOPE + MLA_V)), MLA_KV_LORA ** -0.5),
        "mla_q_head_g": gain((n_a, MLA_QK)),
        "mla_k_head_g": gain((n_a, MLA_QK)),
        "mla_w_o": nrm((n_a, MLA_HEADS * MLA_V, D), (MLA_HEADS * MLA_V) ** -0.5),
        "hgrn_w_in": nrm((n_b, D, 4 * D), D ** -0.5),
        "hgrn_lb_logits": nrm((DEPTH, HGRN_HEADS * HGRN_DK), 0.1),
        "hgrn_o_norm_g": gain((n_b, HGRN_DV)),
        "hgrn_w_o": nrm((n_b, D, D), D ** -0.5),
        "s5_lam_re": -0.5 + nrm((n_c, S5_GROUPS, S5_STATE), 0.01),
        "s5_lam_im": jnp.pi * jnp.arange(S5_STATE, dtype=F32)[None, None, :] + nrm((n_c, S5_GROUPS, S5_STATE), 0.01),
        "s5_log_dt": jax.random.uniform(next(ks), (n_c, S5_GROUPS), F32,
                                        minval=math.log(S5_DT_MIN), maxval=math.log(S5_DT_MAX)),
        "s5_b_re": nrm((n_c, S5_GROUPS, S5_STATE, S5_GROUP), (2 * S5_GROUP) ** -0.5),
        "s5_b_im": nrm((n_c, S5_GROUPS, S5_STATE, S5_GROUP), (2 * S5_GROUP) ** -0.5),
        "s5_c_re": nrm((n_c, S5_GROUPS, S5_GROUP, S5_STATE), (2 * S5_STATE) ** -0.5),
        "s5_c_im": nrm((n_c, S5_GROUPS, S5_GROUP, S5_STATE), (2 * S5_STATE) ** -0.5),
        "s5_d": nrm((n_c, D), 1.0),
        "s5_w_glu": nrm((n_c, D, 2 * D), D ** -0.5),
        "ret_w_in": nrm((n_d, D, 2 * RET_HEADS * RET_DK + 2 * RET_HEADS * RET_DV), D ** -0.5),
        "ret_gn_g": gain((n_d, RET_HEADS * RET_DV)),
        "ret_w_o": nrm((n_d, RET_HEADS * RET_DV, D), (RET_HEADS * RET_DV) ** -0.5),
        "ffn_w_up": nrm((DEPTH, D, 2 * FFN_HIDDEN), D ** -0.5),
        "ffn_conv_w": nrm((DEPTH, CONV_WIDTH, 2 * FFN_HIDDEN), CONV_WIDTH ** -0.5),
        "ffn_conv_b": nrm((DEPTH, 2 * FFN_HIDDEN), 0.01),
        "ffn_w_down": nrm((DEPTH, FFN_HIDDEN, D), FFN_HIDDEN ** -0.5),
    }


def _fwd_reference(x, meta_tokens, norm_mix_g, norm_ffn_g,
              mla_w_down, mla_cq_norm_g, mla_ckv_norm_g, mla_w_uq, mla_w_ukv, mla_q_head_g, mla_k_head_g, mla_w_o,
              hgrn_w_in, hgrn_lb_logits, hgrn_o_norm_g, hgrn_w_o,
              s5_lam_re, s5_lam_im, s5_log_dt, s5_b_re, s5_b_im, s5_c_re, s5_c_im, s5_d, s5_w_glu,
              ret_w_in, ret_gn_g, ret_w_o,
              ffn_w_up, ffn_conv_w, ffn_conv_b, ffn_w_down):
    B = x.shape[0]
    L = x.shape[1] + N_META
    h = jnp.concatenate(
        [jnp.broadcast_to(meta_tokens[None].astype(x.dtype), (B, N_META, D_MODEL)), x], axis=1)
    cid = chunk_ids(L)
    cos_a, sin_a = rope_tables(L, MLA_ROPE)
    cos_d, sin_d = rope_tables(L, RET_DK)
    lb_cum = jnp.cumsum(jax.nn.softmax(hgrn_lb_logits.astype(F32), axis=0), axis=0)
    lb_all = lb_cum - lb_cum[0:1]

    for i in range(DEPTH):
        m, j = i % N_MIXERS, i // N_MIXERS
        a = rms_norm(h, norm_mix_g[i])
        if m == 0:
            y = mla_mixer(a, mla_w_down[j], mla_cq_norm_g[j], mla_ckv_norm_g[j], mla_w_uq[j], mla_w_ukv[j],
                          mla_q_head_g[j], mla_k_head_g[j], mla_w_o[j], cos_a, sin_a, cid)
        elif m == 1:
            y = hgrn2_mixer(a, hgrn_w_in[j], lb_all[i], hgrn_o_norm_g[j], hgrn_w_o[j])
        elif m == 2:
            y = s5_mixer(a, s5_lam_re[j], s5_lam_im[j], s5_log_dt[j], s5_b_re[j], s5_b_im[j],
                         s5_c_re[j], s5_c_im[j], s5_d[j], s5_w_glu[j])
        else:
            y = retention_mixer(a, ret_w_in[j], ret_gn_g[j], ret_w_o[j], cos_d, sin_d)
        h = h + y.astype(h.dtype)
        f = conv_glu_ffn(rms_norm(h, norm_ffn_g[i]), ffn_w_up[i], ffn_conv_w[i], ffn_conv_b[i], ffn_w_down[i])
        h = h + f.astype(h.dtype)
    return h[:, N_META:]


import jax as _jax
import jax.numpy as _jnp

TWIN_FORMAT = 'train_step'
FWD_PARAMS = ['x', 'meta_tokens', 'norm_mix_g', 'norm_ffn_g', 'mla_w_down', 'mla_cq_norm_g', 'mla_ckv_norm_g', 'mla_w_uq', 'mla_w_ukv', 'mla_q_head_g', 'mla_k_head_g', 'mla_w_o', 'hgrn_w_in', 'hgrn_lb_logits', 'hgrn_o_norm_g', 'hgrn_w_o', 's5_lam_re', 's5_lam_im', 's5_log_dt', 's5_b_re', 's5_b_im', 's5_c_re', 's5_c_im', 's5_d', 's5_w_glu', 'ret_w_in', 'ret_gn_g', 'ret_w_o', 'ffn_w_up', 'ffn_conv_w', 'ffn_conv_b', 'ffn_w_down']
TWIN_WEIGHTS = ['meta_tokens', 'norm_mix_g', 'norm_ffn_g', 'mla_w_down', 'mla_cq_norm_g', 'mla_ckv_norm_g', 'mla_w_uq', 'mla_w_ukv', 'mla_q_head_g', 'mla_k_head_g', 'mla_w_o', 'hgrn_w_in', 'hgrn_lb_logits', 'hgrn_o_norm_g', 'hgrn_w_o', 's5_lam_re', 's5_lam_im', 's5_log_dt', 's5_b_re', 's5_b_im', 's5_c_re', 's5_c_im', 's5_d', 's5_w_glu', 'ret_w_in', 'ret_gn_g', 'ret_w_o', 'ffn_w_up', 'ffn_conv_w', 'ffn_conv_b', 'ffn_w_down']
TWIN_DIFF_INPUT = 'x'
TWIN_INPUTS = ['x', 'meta_tokens', 'norm_mix_g', 'norm_ffn_g', 'mla_w_down', 'mla_cq_norm_g', 'mla_ckv_norm_g', 'mla_w_uq', 'mla_w_ukv', 'mla_q_head_g', 'mla_k_head_g', 'mla_w_o', 'hgrn_w_in', 'hgrn_lb_logits', 'hgrn_o_norm_g', 'hgrn_w_o', 's5_lam_re', 's5_lam_im', 's5_log_dt', 's5_b_re', 's5_b_im', 's5_c_re', 's5_c_im', 's5_d', 's5_w_glu', 'ret_w_in', 'ret_gn_g', 'ret_w_o', 'ffn_w_up', 'ffn_conv_w', 'ffn_conv_b', 'ffn_w_down', 'loss_target', 'm_meta_tokens', 'm_norm_mix_g', 'm_norm_ffn_g', 'm_mla_w_down', 'm_mla_cq_norm_g', 'm_mla_ckv_norm_g', 'm_mla_w_uq', 'm_mla_w_ukv', 'm_mla_q_head_g', 'm_mla_k_head_g', 'm_mla_w_o', 'm_hgrn_w_in', 'm_hgrn_lb_logits', 'm_hgrn_o_norm_g', 'm_hgrn_w_o', 'm_s5_lam_re', 'm_s5_lam_im', 'm_s5_log_dt', 'm_s5_b_re', 'm_s5_b_im', 'm_s5_c_re', 'm_s5_c_im', 'm_s5_d', 'm_s5_w_glu', 'm_ret_w_in', 'm_ret_gn_g', 'm_ret_w_o', 'm_ffn_w_up', 'm_ffn_conv_w', 'm_ffn_conv_b', 'm_ffn_w_down', 'v_meta_tokens', 'v_norm_mix_g', 'v_norm_ffn_g', 'v_mla_w_down', 'v_mla_cq_norm_g', 'v_mla_ckv_norm_g', 'v_mla_w_uq', 'v_mla_w_ukv', 'v_mla_q_head_g', 'v_mla_k_head_g', 'v_mla_w_o', 'v_hgrn_w_in', 'v_hgrn_lb_logits', 'v_hgrn_o_norm_g', 'v_hgrn_w_o', 'v_s5_lam_re', 'v_s5_lam_im', 'v_s5_log_dt', 'v_s5_b_re', 'v_s5_b_im', 'v_s5_c_re', 'v_s5_c_im', 'v_s5_d', 'v_s5_w_glu', 'v_ret_w_in', 'v_ret_gn_g', 'v_ret_w_o', 'v_ffn_w_up', 'v_ffn_conv_w', 'v_ffn_conv_b', 'v_ffn_w_down']
TWIN_OUTPUTS = ['loss', 'grad_x', 'grad_meta_tokens', 'grad_norm_mix_g', 'grad_norm_ffn_g', 'grad_mla_w_down', 'grad_mla_cq_norm_g', 'grad_mla_ckv_norm_g', 'grad_mla_w_uq', 'grad_mla_w_ukv', 'grad_mla_q_head_g', 'grad_mla_k_head_g', 'grad_mla_w_o', 'grad_hgrn_w_in', 'grad_hgrn_lb_logits', 'grad_hgrn_o_norm_g', 'grad_hgrn_w_o', 'grad_s5_lam_re', 'grad_s5_lam_im', 'grad_s5_log_dt', 'grad_s5_b_re', 'grad_s5_b_im', 'grad_s5_c_re', 'grad_s5_c_im', 'grad_s5_d', 'grad_s5_w_glu', 'grad_ret_w_in', 'grad_ret_gn_g', 'grad_ret_w_o', 'grad_ffn_w_up', 'grad_ffn_conv_w', 'grad_ffn_conv_b', 'grad_ffn_w_down', 'delta_meta_tokens', 'delta_norm_mix_g', 'delta_norm_ffn_g', 'delta_mla_w_down', 'delta_mla_cq_norm_g', 'delta_mla_ckv_norm_g', 'delta_mla_w_uq', 'delta_mla_w_ukv', 'delta_mla_q_head_g', 'delta_mla_k_head_g', 'delta_mla_w_o', 'delta_hgrn_w_in', 'delta_hgrn_lb_logits', 'delta_hgrn_o_norm_g', 'delta_hgrn_w_o', 'delta_s5_lam_re', 'delta_s5_lam_im', 'delta_s5_log_dt', 'delta_s5_b_re', 'delta_s5_b_im', 'delta_s5_c_re', 'delta_s5_c_im', 'delta_s5_d', 'delta_s5_w_glu', 'delta_ret_w_in', 'delta_ret_gn_g', 'delta_ret_w_o', 'delta_ffn_w_up', 'delta_ffn_conv_w', 'delta_ffn_conv_b', 'delta_ffn_w_down', 'new_m_meta_tokens', 'new_m_norm_mix_g', 'new_m_norm_ffn_g', 'new_m_mla_w_down', 'new_m_mla_cq_norm_g', 'new_m_mla_ckv_norm_g', 'new_m_mla_w_uq', 'new_m_mla_w_ukv', 'new_m_mla_q_head_g', 'new_m_mla_k_head_g', 'new_m_mla_w_o', 'new_m_hgrn_w_in', 'new_m_hgrn_lb_logits', 'new_m_hgrn_o_norm_g', 'new_m_hgrn_w_o', 'new_m_s5_lam_re', 'new_m_s5_lam_im', 'new_m_s5_log_dt', 'new_m_s5_b_re', 'new_m_s5_b_im', 'new_m_s5_c_re', 'new_m_s5_c_im', 'new_m_s5_d', 'new_m_s5_w_glu', 'new_m_ret_w_in', 'new_m_ret_gn_g', 'new_m_ret_w_o', 'new_m_ffn_w_up', 'new_m_ffn_conv_w', 'new_m_ffn_conv_b', 'new_m_ffn_w_down', 'new_v_meta_tokens', 'new_v_norm_mix_g', 'new_v_norm_ffn_g', 'new_v_mla_w_down', 'new_v_mla_cq_norm_g', 'new_v_mla_ckv_norm_g', 'new_v_mla_w_uq', 'new_v_mla_w_ukv', 'new_v_mla_q_head_g', 'new_v_mla_k_head_g', 'new_v_mla_w_o', 'new_v_hgrn_w_in', 'new_v_hgrn_lb_logits', 'new_v_hgrn_o_norm_g', 'new_v_hgrn_w_o', 'new_v_s5_lam_re', 'new_v_s5_lam_im', 'new_v_s5_log_dt', 'new_v_s5_b_re', 'new_v_s5_b_im', 'new_v_s5_c_re', 'new_v_s5_c_im', 'new_v_s5_d', 'new_v_s5_w_glu', 'new_v_ret_w_in', 'new_v_ret_gn_g', 'new_v_ret_w_o', 'new_v_ffn_w_up', 'new_v_ffn_conv_w', 'new_v_ffn_conv_b', 'new_v_ffn_w_down']
TWIN_LEAF_KINDS = {'loss': 'loss', 'grad_x': 'grad_x', 'grad_meta_tokens': 'grad_w', 'grad_norm_mix_g': 'grad_w', 'grad_norm_ffn_g': 'grad_w', 'grad_mla_w_down': 'grad_w', 'grad_mla_cq_norm_g': 'grad_w', 'grad_mla_ckv_norm_g': 'grad_w', 'grad_mla_w_uq': 'grad_w', 'grad_mla_w_ukv': 'grad_w', 'grad_mla_q_head_g': 'grad_w', 'grad_mla_k_head_g': 'grad_w', 'grad_mla_w_o': 'grad_w', 'grad_hgrn_w_in': 'grad_w', 'grad_hgrn_lb_logits': 'grad_w', 'grad_hgrn_o_norm_g': 'grad_w', 'grad_hgrn_w_o': 'grad_w', 'grad_s5_lam_re': 'grad_w', 'grad_s5_lam_im': 'grad_w', 'grad_s5_log_dt': 'grad_w', 'grad_s5_b_re': 'grad_w', 'grad_s5_b_im': 'grad_w', 'grad_s5_c_re': 'grad_w', 'grad_s5_c_im': 'grad_w', 'grad_s5_d': 'grad_w', 'grad_s5_w_glu': 'grad_w', 'grad_ret_w_in': 'grad_w', 'grad_ret_gn_g': 'grad_w', 'grad_ret_w_o': 'grad_w', 'grad_ffn_w_up': 'grad_w', 'grad_ffn_conv_w': 'grad_w', 'grad_ffn_conv_b': 'grad_w', 'grad_ffn_w_down': 'grad_w', 'delta_meta_tokens': 'delta_w', 'delta_norm_mix_g': 'delta_w', 'delta_norm_ffn_g': 'delta_w', 'delta_mla_w_down': 'delta_w', 'delta_mla_cq_norm_g': 'delta_w', 'delta_mla_ckv_norm_g': 'delta_w', 'delta_mla_w_uq': 'delta_w', 'delta_mla_w_ukv': 'delta_w', 'delta_mla_q_head_g': 'delta_w', 'delta_mla_k_head_g': 'delta_w', 'delta_mla_w_o': 'delta_w', 'delta_hgrn_w_in': 'delta_w', 'delta_hgrn_lb_logits': 'delta_w', 'delta_hgrn_o_norm_g': 'delta_w', 'delta_hgrn_w_o': 'delta_w', 'delta_s5_lam_re': 'delta_w', 'delta_s5_lam_im': 'delta_w', 'delta_s5_log_dt': 'delta_w', 'delta_s5_b_re': 'delta_w', 'delta_s5_b_im': 'delta_w', 'delta_s5_c_re': 'delta_w', 'delta_s5_c_im': 'delta_w', 'delta_s5_d': 'delta_w', 'delta_s5_w_glu': 'delta_w', 'delta_ret_w_in': 'delta_w', 'delta_ret_gn_g': 'delta_w', 'delta_ret_w_o': 'delta_w', 'delta_ffn_w_up': 'delta_w', 'delta_ffn_conv_w': 'delta_w', 'delta_ffn_conv_b': 'delta_w', 'delta_ffn_w_down': 'delta_w', 'new_m_meta_tokens': 'new_m', 'new_m_norm_mix_g': 'new_m', 'new_m_norm_ffn_g': 'new_m', 'new_m_mla_w_down': 'new_m', 'new_m_mla_cq_norm_g': 'new_m', 'new_m_mla_ckv_norm_g': 'new_m', 'new_m_mla_w_uq': 'new_m', 'new_m_mla_w_ukv': 'new_m', 'new_m_mla_q_head_g': 'new_m', 'new_m_mla_k_head_g': 'new_m', 'new_m_mla_w_o': 'new_m', 'new_m_hgrn_w_in': 'new_m', 'new_m_hgrn_lb_logits': 'new_m', 'new_m_hgrn_o_norm_g': 'new_m', 'new_m_hgrn_w_o': 'new_m', 'new_m_s5_lam_re': 'new_m', 'new_m_s5_lam_im': 'new_m', 'new_m_s5_log_dt': 'new_m', 'new_m_s5_b_re': 'new_m', 'new_m_s5_b_im': 'new_m', 'new_m_s5_c_re': 'new_m', 'new_m_s5_c_im': 'new_m', 'new_m_s5_d': 'new_m', 'new_m_s5_w_glu': 'new_m', 'new_m_ret_w_in': 'new_m', 'new_m_ret_gn_g': 'new_m', 'new_m_ret_w_o': 'new_m', 'new_m_ffn_w_up': 'new_m', 'new_m_ffn_conv_w': 'new_m', 'new_m_ffn_conv_b': 'new_m', 'new_m_ffn_w_down': 'new_m', 'new_v_meta_tokens': 'new_v', 'new_v_norm_mix_g': 'new_v', 'new_v_norm_ffn_g': 'new_v', 'new_v_mla_w_down': 'new_v', 'new_v_mla_cq_norm_g': 'new_v', 'new_v_mla_ckv_norm_g': 'new_v', 'new_v_mla_w_uq': 'new_v', 'new_v_mla_w_ukv': 'new_v', 'new_v_mla_q_head_g': 'new_v', 'new_v_mla_k_head_g': 'new_v', 'new_v_mla_w_o': 'new_v', 'new_v_hgrn_w_in': 'new_v', 'new_v_hgrn_lb_logits': 'new_v', 'new_v_hgrn_o_norm_g': 'new_v', 'new_v_hgrn_w_o': 'new_v', 'new_v_s5_lam_re': 'new_v', 'new_v_s5_lam_im': 'new_v', 'new_v_s5_log_dt': 'new_v', 'new_v_s5_b_re': 'new_v', 'new_v_s5_b_im': 'new_v', 'new_v_s5_c_re': 'new_v', 'new_v_s5_c_im': 'new_v', 'new_v_s5_d': 'new_v', 'new_v_s5_w_glu': 'new_v', 'new_v_ret_w_in': 'new_v', 'new_v_ret_gn_g': 'new_v', 'new_v_ret_w_o': 'new_v', 'new_v_ffn_w_up': 'new_v', 'new_v_ffn_conv_w': 'new_v', 'new_v_ffn_conv_b': 'new_v', 'new_v_ffn_w_down': 'new_v'}


def _forward(args):
    return _fwd_reference(*[args[k] for k in FWD_PARAMS])


def _output_shape():
    def fwd():
        inp = _fwd_setup_inputs(0)
        return _fwd_reference(*[inp[k] for k in FWD_PARAMS])
    out = _jax.eval_shape(fwd)
    return out.shape, out.dtype

N_MICROBATCH = 1
ADAM_LR = 0.001
ADAM_B1 = 0.9
ADAM_B2 = 0.999
ADAM_EPS = 1e-08
ADAM_WD = 0.01
ADAM_STEP = 10
PER_EXAMPLE_BATCH_AXIS = {'x': 0, 'loss_target': 0}
SHARED_INPUTS = []
_WEIGHT_DTYPES = {'meta_tokens': _jnp.float32, 'norm_mix_g': _jnp.float32, 'norm_ffn_g': _jnp.float32, 'mla_w_down': _jnp.float32, 'mla_cq_norm_g': _jnp.float32, 'mla_ckv_norm_g': _jnp.float32, 'mla_w_uq': _jnp.float32, 'mla_w_ukv': _jnp.float32, 'mla_q_head_g': _jnp.float32, 'mla_k_head_g': _jnp.float32, 'mla_w_o': _jnp.float32, 'hgrn_w_in': _jnp.float32, 'hgrn_lb_logits': _jnp.float32, 'hgrn_o_norm_g': _jnp.float32, 'hgrn_w_o': _jnp.float32, 's5_lam_re': _jnp.float32, 's5_lam_im': _jnp.float32, 's5_log_dt': _jnp.float32, 's5_b_re': _jnp.float32, 's5_b_im': _jnp.float32, 's5_c_re': _jnp.float32, 's5_c_im': _jnp.float32, 's5_d': _jnp.float32, 's5_w_glu': _jnp.float32, 'ret_w_in': _jnp.float32, 'ret_gn_g': _jnp.float32, 'ret_w_o': _jnp.float32, 'ffn_w_up': _jnp.float32, 'ffn_conv_w': _jnp.float32, 'ffn_conv_b': _jnp.float32, 'ffn_w_down': _jnp.float32}
MOMENT_SCALE = {'meta_tokens': 7.859117e-02, 'norm_mix_g': 1.021242e+01, 'norm_ffn_g': 2.619780e+01, 'mla_w_down': 5.203163e-01, 'mla_cq_norm_g': 3.692749e-01, 'mla_ckv_norm_g': 1.026905e+00, 'mla_w_uq': 1.810073e-01, 'mla_w_ukv': 2.615691e-01, 'mla_q_head_g': 1.043058e+00, 'mla_k_head_g': 1.050462e+00, 'mla_w_o': 3.154162e-01, 'hgrn_w_in': 4.501803e-01, 'hgrn_lb_logits': 1.651876e-02, 'hgrn_o_norm_g': 9.162901e+01, 'hgrn_w_o': 6.433908e-01, 's5_lam_re': 1.769623e-02, 's5_lam_im': 1.596378e-02, 's5_log_dt': 8.882622e+00, 's5_b_re': 1.202038e-02, 's5_b_im': 1.215866e-02, 's5_c_re': 2.392626e-02, 's5_c_im': 2.543833e-02, 's5_d': 4.915603e+00, 's5_w_glu': 1.379608e+00, 'ret_w_in': 3.273791e-01, 'ret_gn_g': 5.760505e+00, 'ret_w_o': 3.401847e-01, 'ffn_w_up': 4.464529e-01, 'ffn_conv_w': 3.594693e+00, 'ffn_conv_b': 3.380577e+00, 'ffn_w_down': 5.925293e-01}


def _to_microbatches(a, axis):
    t = _jnp.moveaxis(a, axis, 0)
    t = t.reshape((N_MICROBATCH, t.shape[0] // N_MICROBATCH) + t.shape[1:])
    return _jnp.moveaxis(t, 1, axis + 1)


def setup_inputs(seed: int = 0) -> dict:
    inp = _fwd_setup_inputs(seed)
    key = _jax.random.fold_in(_jax.random.key(seed), 7919)
    shape, _ = _output_shape()
    out = dict(inp)
    out["loss_target"] = _jax.random.normal(_jax.random.fold_in(key, 0), shape, _jnp.float32)
    for i, name in enumerate(TWIN_WEIGHTS):
        w = inp[name].astype(_jnp.float32)
        if MOMENT_SCALE is None:
            s = _jnp.sqrt(_jnp.mean(_jnp.square(w)) + 1e-30)
        else:
            s = MOMENT_SCALE[name]
        km, kv = _jax.random.split(_jax.random.fold_in(key, i + 1))
        out[name] = w
        out["m_" + name] = s * _jax.random.normal(km, w.shape, _jnp.float32)
        out["v_" + name] = (s * s) * _jax.random.uniform(kv, w.shape, _jnp.float32, 0.5, 1.5)
    if N_MICROBATCH > 1:
        for name, axis in PER_EXAMPLE_BATCH_AXIS.items():
            out[name] = _to_microbatches(out[name], axis)
    return {'x': out['x'], 'meta_tokens': out['meta_tokens'], 'norm_mix_g': out['norm_mix_g'], 'norm_ffn_g': out['norm_ffn_g'], 'mla_w_down': out['mla_w_down'], 'mla_cq_norm_g': out['mla_cq_norm_g'], 'mla_ckv_norm_g': out['mla_ckv_norm_g'], 'mla_w_uq': out['mla_w_uq'], 'mla_w_ukv': out['mla_w_ukv'], 'mla_q_head_g': out['mla_q_head_g'], 'mla_k_head_g': out['mla_k_head_g'], 'mla_w_o': out['mla_w_o'], 'hgrn_w_in': out['hgrn_w_in'], 'hgrn_lb_logits': out['hgrn_lb_logits'], 'hgrn_o_norm_g': out['hgrn_o_norm_g'], 'hgrn_w_o': out['hgrn_w_o'], 's5_lam_re': out['s5_lam_re'], 's5_lam_im': out['s5_lam_im'], 's5_log_dt': out['s5_log_dt'], 's5_b_re': out['s5_b_re'], 's5_b_im': out['s5_b_im'], 's5_c_re': out['s5_c_re'], 's5_c_im': out['s5_c_im'], 's5_d': out['s5_d'], 's5_w_glu': out['s5_w_glu'], 'ret_w_in': out['ret_w_in'], 'ret_gn_g': out['ret_gn_g'], 'ret_w_o': out['ret_w_o'], 'ffn_w_up': out['ffn_w_up'], 'ffn_conv_w': out['ffn_conv_w'], 'ffn_conv_b': out['ffn_conv_b'], 'ffn_w_down': out['ffn_w_down'], 'loss_target': out['loss_target'], 'm_meta_tokens': out['m_meta_tokens'], 'm_norm_mix_g': out['m_norm_mix_g'], 'm_norm_ffn_g': out['m_norm_ffn_g'], 'm_mla_w_down': out['m_mla_w_down'], 'm_mla_cq_norm_g': out['m_mla_cq_norm_g'], 'm_mla_ckv_norm_g': out['m_mla_ckv_norm_g'], 'm_mla_w_uq': out['m_mla_w_uq'], 'm_mla_w_ukv': out['m_mla_w_ukv'], 'm_mla_q_head_g': out['m_mla_q_head_g'], 'm_mla_k_head_g': out['m_mla_k_head_g'], 'm_mla_w_o': out['m_mla_w_o'], 'm_hgrn_w_in': out['m_hgrn_w_in'], 'm_hgrn_lb_logits': out['m_hgrn_lb_logits'], 'm_hgrn_o_norm_g': out['m_hgrn_o_norm_g'], 'm_hgrn_w_o': out['m_hgrn_w_o'], 'm_s5_lam_re': out['m_s5_lam_re'], 'm_s5_lam_im': out['m_s5_lam_im'], 'm_s5_log_dt': out['m_s5_log_dt'], 'm_s5_b_re': out['m_s5_b_re'], 'm_s5_b_im': out['m_s5_b_im'], 'm_s5_c_re': out['m_s5_c_re'], 'm_s5_c_im': out['m_s5_c_im'], 'm_s5_d': out['m_s5_d'], 'm_s5_w_glu': out['m_s5_w_glu'], 'm_ret_w_in': out['m_ret_w_in'], 'm_ret_gn_g': out['m_ret_gn_g'], 'm_ret_w_o': out['m_ret_w_o'], 'm_ffn_w_up': out['m_ffn_w_up'], 'm_ffn_conv_w': out['m_ffn_conv_w'], 'm_ffn_conv_b': out['m_ffn_conv_b'], 'm_ffn_w_down': out['m_ffn_w_down'], 'v_meta_tokens': out['v_meta_tokens'], 'v_norm_mix_g': out['v_norm_mix_g'], 'v_norm_ffn_g': out['v_norm_ffn_g'], 'v_mla_w_down': out['v_mla_w_down'], 'v_mla_cq_norm_g': out['v_mla_cq_norm_g'], 'v_mla_ckv_norm_g': out['v_mla_ckv_norm_g'], 'v_mla_w_uq': out['v_mla_w_uq'], 'v_mla_w_ukv': out['v_mla_w_ukv'], 'v_mla_q_head_g': out['v_mla_q_head_g'], 'v_mla_k_head_g': out['v_mla_k_head_g'], 'v_mla_w_o': out['v_mla_w_o'], 'v_hgrn_w_in': out['v_hgrn_w_in'], 'v_hgrn_lb_logits': out['v_hgrn_lb_logits'], 'v_hgrn_o_norm_g': out['v_hgrn_o_norm_g'], 'v_hgrn_w_o': out['v_hgrn_w_o'], 'v_s5_lam_re': out['v_s5_lam_re'], 'v_s5_lam_im': out['v_s5_lam_im'], 'v_s5_log_dt': out['v_s5_log_dt'], 'v_s5_b_re': out['v_s5_b_re'], 'v_s5_b_im': out['v_s5_b_im'], 'v_s5_c_re': out['v_s5_c_re'], 'v_s5_c_im': out['v_s5_c_im'], 'v_s5_d': out['v_s5_d'], 'v_s5_w_glu': out['v_s5_w_glu'], 'v_ret_w_in': out['v_ret_w_in'], 'v_ret_gn_g': out['v_ret_gn_g'], 'v_ret_w_o': out['v_ret_w_o'], 'v_ffn_w_up': out['v_ffn_w_up'], 'v_ffn_conv_w': out['v_ffn_conv_w'], 'v_ffn_conv_b': out['v_ffn_conv_b'], 'v_ffn_w_down': out['v_ffn_w_down']}


def _loss(weights, diff, rest, loss_target):
    with _jax.named_scope("forward"):
        args = {**rest, TWIN_DIFF_INPUT: diff, **{k: w.astype(_WEIGHT_DTYPES[k]) for k, w in weights.items()}}
        y = _forward(args)
    with _jax.named_scope("loss_head"):
        err = _jnp.square(y.astype(_jnp.float32) - loss_target)
        return 0.5 * _jnp.sum(_jnp.mean(err, axis=-1)) if err.ndim else 0.5 * err


def _adamw(w, g, m, v):
    m = ADAM_B1 * m + (1.0 - ADAM_B1) * g
    v = ADAM_B2 * v + (1.0 - ADAM_B2) * _jnp.square(g)
    m_hat = m / (1.0 - ADAM_B1 ** ADAM_STEP)
    v_hat = v / (1.0 - ADAM_B2 ** ADAM_STEP)
    delta = -ADAM_LR * (m_hat / (_jnp.sqrt(v_hat) + ADAM_EPS) + ADAM_WD * w)
    return delta, m, v


def reference(x, meta_tokens, norm_mix_g, norm_ffn_g, mla_w_down, mla_cq_norm_g, mla_ckv_norm_g, mla_w_uq, mla_w_ukv, mla_q_head_g, mla_k_head_g, mla_w_o, hgrn_w_in, hgrn_lb_logits, hgrn_o_norm_g, hgrn_w_o, s5_lam_re, s5_lam_im, s5_log_dt, s5_b_re, s5_b_im, s5_c_re, s5_c_im, s5_d, s5_w_glu, ret_w_in, ret_gn_g, ret_w_o, ffn_w_up, ffn_conv_w, ffn_conv_b, ffn_w_down, loss_target, m_meta_tokens, m_norm_mix_g, m_norm_ffn_g, m_mla_w_down, m_mla_cq_norm_g, m_mla_ckv_norm_g, m_mla_w_uq, m_mla_w_ukv, m_mla_q_head_g, m_mla_k_head_g, m_mla_w_o, m_hgrn_w_in, m_hgrn_lb_logits, m_hgrn_o_norm_g, m_hgrn_w_o, m_s5_lam_re, m_s5_lam_im, m_s5_log_dt, m_s5_b_re, m_s5_b_im, m_s5_c_re, m_s5_c_im, m_s5_d, m_s5_w_glu, m_ret_w_in, m_ret_gn_g, m_ret_w_o, m_ffn_w_up, m_ffn_conv_w, m_ffn_conv_b, m_ffn_w_down, v_meta_tokens, v_norm_mix_g, v_norm_ffn_g, v_mla_w_down, v_mla_cq_norm_g, v_mla_ckv_norm_g, v_mla_w_uq, v_mla_w_ukv, v_mla_q_head_g, v_mla_k_head_g, v_mla_w_o, v_hgrn_w_in, v_hgrn_lb_logits, v_hgrn_o_norm_g, v_hgrn_w_o, v_s5_lam_re, v_s5_lam_im, v_s5_log_dt, v_s5_b_re, v_s5_b_im, v_s5_c_re, v_s5_c_im, v_s5_d, v_s5_w_glu, v_ret_w_in, v_ret_gn_g, v_ret_w_o, v_ffn_w_up, v_ffn_conv_w, v_ffn_conv_b, v_ffn_w_down):
    given = dict(x=x, meta_tokens=meta_tokens, norm_mix_g=norm_mix_g, norm_ffn_g=norm_ffn_g, mla_w_down=mla_w_down, mla_cq_norm_g=mla_cq_norm_g, mla_ckv_norm_g=mla_ckv_norm_g, mla_w_uq=mla_w_uq, mla_w_ukv=mla_w_ukv, mla_q_head_g=mla_q_head_g, mla_k_head_g=mla_k_head_g, mla_w_o=mla_w_o, hgrn_w_in=hgrn_w_in, hgrn_lb_logits=hgrn_lb_logits, hgrn_o_norm_g=hgrn_o_norm_g, hgrn_w_o=hgrn_w_o, s5_lam_re=s5_lam_re, s5_lam_im=s5_lam_im, s5_log_dt=s5_log_dt, s5_b_re=s5_b_re, s5_b_im=s5_b_im, s5_c_re=s5_c_re, s5_c_im=s5_c_im, s5_d=s5_d, s5_w_glu=s5_w_glu, ret_w_in=ret_w_in, ret_gn_g=ret_gn_g, ret_w_o=ret_w_o, ffn_w_up=ffn_w_up, ffn_conv_w=ffn_conv_w, ffn_conv_b=ffn_conv_b, ffn_w_down=ffn_w_down, loss_target=loss_target, m_meta_tokens=m_meta_tokens, m_norm_mix_g=m_norm_mix_g, m_norm_ffn_g=m_norm_ffn_g, m_mla_w_down=m_mla_w_down, m_mla_cq_norm_g=m_mla_cq_norm_g, m_mla_ckv_norm_g=m_mla_ckv_norm_g, m_mla_w_uq=m_mla_w_uq, m_mla_w_ukv=m_mla_w_ukv, m_mla_q_head_g=m_mla_q_head_g, m_mla_k_head_g=m_mla_k_head_g, m_mla_w_o=m_mla_w_o, m_hgrn_w_in=m_hgrn_w_in, m_hgrn_lb_logits=m_hgrn_lb_logits, m_hgrn_o_norm_g=m_hgrn_o_norm_g, m_hgrn_w_o=m_hgrn_w_o, m_s5_lam_re=m_s5_lam_re, m_s5_lam_im=m_s5_lam_im, m_s5_log_dt=m_s5_log_dt, m_s5_b_re=m_s5_b_re, m_s5_b_im=m_s5_b_im, m_s5_c_re=m_s5_c_re, m_s5_c_im=m_s5_c_im, m_s5_d=m_s5_d, m_s5_w_glu=m_s5_w_glu, m_ret_w_in=m_ret_w_in, m_ret_gn_g=m_ret_gn_g, m_ret_w_o=m_ret_w_o, m_ffn_w_up=m_ffn_w_up, m_ffn_conv_w=m_ffn_conv_w, m_ffn_conv_b=m_ffn_conv_b, m_ffn_w_down=m_ffn_w_down, v_meta_tokens=v_meta_tokens, v_norm_mix_g=v_norm_mix_g, v_norm_ffn_g=v_norm_ffn_g, v_mla_w_down=v_mla_w_down, v_mla_cq_norm_g=v_mla_cq_norm_g, v_mla_ckv_norm_g=v_mla_ckv_norm_g, v_mla_w_uq=v_mla_w_uq, v_mla_w_ukv=v_mla_w_ukv, v_mla_q_head_g=v_mla_q_head_g, v_mla_k_head_g=v_mla_k_head_g, v_mla_w_o=v_mla_w_o, v_hgrn_w_in=v_hgrn_w_in, v_hgrn_lb_logits=v_hgrn_lb_logits, v_hgrn_o_norm_g=v_hgrn_o_norm_g, v_hgrn_w_o=v_hgrn_w_o, v_s5_lam_re=v_s5_lam_re, v_s5_lam_im=v_s5_lam_im, v_s5_log_dt=v_s5_log_dt, v_s5_b_re=v_s5_b_re, v_s5_b_im=v_s5_b_im, v_s5_c_re=v_s5_c_re, v_s5_c_im=v_s5_c_im, v_s5_d=v_s5_d, v_s5_w_glu=v_s5_w_glu, v_ret_w_in=v_ret_w_in, v_ret_gn_g=v_ret_gn_g, v_ret_w_o=v_ret_w_o, v_ffn_w_up=v_ffn_w_up, v_ffn_conv_w=v_ffn_conv_w, v_ffn_conv_b=v_ffn_conv_b, v_ffn_w_down=v_ffn_w_down)
    weights = {n: given[n] for n in TWIN_WEIGHTS}
    shared = {n: given[n] for n in SHARED_INPUTS}
    per_example = {n: given[n] for n in ['x']}
    grad_fn = _jax.value_and_grad(_loss, argnums=(0, 1))

    def one_microbatch(ex, loss_target):
        ex = dict(ex)
        diff = ex.pop(TWIN_DIFF_INPUT)
        return grad_fn(weights, diff, {**shared, **ex}, loss_target)

    if N_MICROBATCH == 1:
        loss, (grad_w, grad_x) = one_microbatch(per_example, given["loss_target"])
    else:
        def body(carry, xs):
            loss_sum, grad_sum = carry
            l_k, (gw_k, gx_k) = one_microbatch(xs[0], xs[1])
            with _jax.named_scope("update"):
                return (loss_sum + l_k, _jax.tree.map(_jnp.add, grad_sum, gw_k)), gx_k

        init = (_jnp.zeros((), _jnp.float32), _jax.tree.map(_jnp.zeros_like, weights))
        (loss, grad_w), grad_x = _jax.lax.scan(body, init, (per_example, given["loss_target"]))
    with _jax.named_scope("update"):
        delta_w, new_m, new_v = {}, {}, {}
        for n in TWIN_WEIGHTS:
            delta_w[n], new_m[n], new_v[n] = _adamw(weights[n], grad_w[n], given["m_" + n], given["v_" + n])
    return (loss, grad_x, *[grad_w[n] for n in TWIN_WEIGHTS], *[delta_w[n] for n in TWIN_WEIGHTS],
            *[new_m[n] for n in TWIN_WEIGHTS], *[new_v[n] for n in TWIN_WEIGHTS])
```

```python
import functools
import math

import jax
import jax.numpy as jnp
from jax import lax
from jax.experimental import pallas as pl
from jax.experimental.pallas import tpu as pltpu

F32 = jnp.float32
BF = jnp.bfloat16
SDS = jax.ShapeDtypeStruct

N_DEV = 8
D = 1024
N_META = 16
PAD = 48
OFF = PAD + N_META
CH = 64
EPS = 1e-6
NEG = -1e30
ROPE_BASE = 10000.0

MLA_H, MLA_NOPE, MLA_ROPE, MLA_V = 8, 128, 64, 128
MLA_QK = MLA_NOPE + MLA_ROPE
MLA_QL, MLA_KVL = 384, 256
HG_H, HG_D, HG_C = 8, 128, 16
S5_G, S5_P, S5_K = 64, 64, 16
S5_SG = 8
RET_H, RET_DK, RET_DV = 4, 256, 512
FFN_F = 2816
FFN_B = 704

ADAM_LR, ADAM_B1, ADAM_B2, ADAM_EPS, ADAM_WD, ADAM_STEP = 0.001, 0.9, 0.999, 1e-08, 0.01, 10

VMEM_LIMIT = 56 * 1024 * 1024
ARB = "arbitrary"


def _cp(n):
    return pltpu.CompilerParams(dimension_semantics=(ARB,) * n, vmem_limit_bytes=VMEM_LIMIT)


def _bdot(a, b, ca, cb):
    return lax.dot_general(a.astype(BF), b.astype(BF), (((ca,), (cb,)), ((), ())), preferred_element_type=F32)


@jax.custom_vjp
def mm(a, b):
    return _bdot(a, b, 1, 0)


@jax.custom_vjp
def mm_nt(a, b):
    return _bdot(a, b, 1, 1)


@jax.custom_vjp
def mm_tn(a, b):
    return _bdot(a, b, 0, 0)


mm.defvjp(lambda a, b: (mm(a, b), (a, b)),
          lambda r, g: (mm_nt(g, r[1]).astype(r[0].dtype), mm_tn(r[0], g).astype(r[1].dtype)))
mm_nt.defvjp(lambda a, b: (mm_nt(a, b), (a, b)),
             lambda r, g: (mm(g, r[1]).astype(r[0].dtype), mm_tn(g, r[0]).astype(r[1].dtype)))
mm_tn.defvjp(lambda a, b: (mm_tn(a, b), (a, b)),
             lambda r, g: (mm_nt(r[1], g).astype(r[0].dtype), mm(r[0], g).astype(r[1].dtype)))


def _xdot(a, b, ca, cb):
    return lax.dot_general(a, b, (((ca,), (cb,)), ((), ())), preferred_element_type=F32,
                           precision=lax.Precision.HIGHEST)


@jax.custom_vjp
def cleft(t, x):
    return _xdot(t, x, 1, 0)


cleft.defvjp(lambda t, x: (cleft(t, x), t), lambda t, g: (jnp.zeros_like(t), _xdot(t, g, 0, 0)))


@jax.custom_vjp
def cright(x, r):
    return _xdot(x, r, 1, 0)


cright.defvjp(lambda x, r: (cright(x, r), r), lambda r, g: (_xdot(g, r, 1, 1), jnp.zeros_like(r)))


def _shift_raw(x, s):
    n = x.shape[0]
    r = lax.broadcasted_iota(jnp.int32, x.shape, 0)
    y = pltpu.roll(x, s % n, 0)
    return jnp.where((r >= s) & (r < n + s), y, 0.0)


@functools.partial(jax.custom_vjp, nondiff_argnums=(1,))
def shift_rows(x, s):
    return _shift_raw(x, s)


shift_rows.defvjp(lambda x, s: (_shift_raw(x, s), None), lambda s, _, g: (_shift_raw(g, -s),))


def _rms(x, g):
    return x * lax.rsqrt(jnp.mean(x * x, axis=-1, keepdims=True) + EPS) * g


def _silu(x):
    return x * jax.nn.sigmoid(x)


def _mm_call(name, a, b, *, grid, a_spec, b_spec, o_shape, o_spec, dims, acc_shape, res=None, res_spec=None,
             mask_tm=None):
    nk = grid[2]

    def body(*refs):
        if res is None:
            a_ref, b_ref, o_ref, acc = refs
        else:
            a_ref, b_ref, r_ref, o_ref, acc = refs
        k = pl.program_id(2)

        @pl.when(k == 0)
        def _():
            acc[...] = jnp.zeros_like(acc)

        acc[...] += lax.dot_general(a_ref[...].astype(BF), b_ref[...].astype(BF), dims, preferred_element_type=F32)

        @pl.when(k == nk - 1)
        def _():
            v = acc[...]
            if res is not None:
                v = v + r_ref[...].astype(F32)
                rows = pl.program_id(0) * mask_tm + lax.broadcasted_iota(jnp.int32, v.shape, 0)
                v = jnp.where(rows >= PAD, v, 0.0)
            o_ref[...] = v.astype(o_ref.dtype)

    ins = [a, b] + ([res] if res is not None else [])
    specs = [a_spec, b_spec] + ([res_spec] if res is not None else [])
    return pl.pallas_call(body, name=name, grid=grid, in_specs=specs, out_specs=o_spec, out_shape=o_shape,
                          scratch_shapes=[pltpu.VMEM(acc_shape, F32)], compiler_params=_cp(3))(*ins)


NN = (((1,), (0,)), ((), ()))
NT = (((1,), (1,)), ((), ()))
TN = (((0,), (0,)), ((), ()))


def _row_tile(lp):
    for t in (832, 640, 320, 64):
        if lp % t == 0:
            return t
    raise ValueError(lp)


def _col_tile(n):
    for t in (1024, 768, 512, 384, 256, 128):
        if n % t == 0:
            return t
    return n


def lin(name, a, w, out_dtype=F32, res=None):
    m, k = a.shape
    n = w.shape[1]
    tm, tn, tc = _row_tile(m), _col_tile(n), _col_tile(k)
    return _mm_call(name, a, w, grid=(m // tm, n // tn, k // tc),
                    a_spec=pl.BlockSpec((tm, tc), lambda i, j, kk: (i, kk)),
                    b_spec=pl.BlockSpec((tc, tn), lambda i, j, kk: (kk, j)),
                    o_shape=SDS((m, n), out_dtype), o_spec=pl.BlockSpec((tm, tn), lambda i, j, kk: (i, j)),
                    dims=NN, acc_shape=(tm, tn), res=res,
                    res_spec=pl.BlockSpec((tm, tn), lambda i, j, kk: (i, j)), mask_tm=tm)


def lin_bo(name, a, wb, out_dtype=F32):
    m, k = a.shape
    nb, _, n = wb.shape
    tm = _row_tile(m)
    return _mm_call(name, a, wb, grid=(m // tm, nb, 1),
                    a_spec=pl.BlockSpec((tm, k), lambda i, j, kk: (i, 0)),
                    b_spec=pl.BlockSpec((None, k, n), lambda i, j, kk: (j, 0, 0)),
                    o_shape=SDS((nb, m, n), out_dtype), o_spec=pl.BlockSpec((None, tm, n), lambda i, j, kk: (j, i, 0)),
                    dims=NN, acc_shape=(tm, n))


def lin_bi(name, ab, wb, out_dtype=F32, res=None):
    nb, m, k = ab.shape
    n = wb.shape[2]
    tm, tn = _row_tile(m), _col_tile(n)
    return _mm_call(name, ab, wb, grid=(m // tm, n // tn, nb),
                    a_spec=pl.BlockSpec((None, tm, k), lambda i, j, kk: (kk, i, 0)),
                    b_spec=pl.BlockSpec((None, k, tn), lambda i, j, kk: (kk, 0, j)),
                    o_shape=SDS((m, n), out_dtype), o_spec=pl.BlockSpec((tm, tn), lambda i, j, kk: (i, j)),
                    dims=NN, acc_shape=(tm, tn), res=res,
                    res_spec=pl.BlockSpec((tm, tn), lambda i, j, kk: (i, j)), mask_tm=tm)


def lin_t(name, g, w, out_dtype=F32):
    m, n = g.shape
    k = w.shape[0]
    tm, tk, tc = _row_tile(m), _col_tile(k), _col_tile(n)
    return _mm_call(name, g, w, grid=(m // tm, k // tk, n // tc),
                    a_spec=pl.BlockSpec((tm, tc), lambda i, j, kk: (i, kk)),
                    b_spec=pl.BlockSpec((tk, tc), lambda i, j, kk: (j, kk)),
                    o_shape=SDS((m, k), out_dtype), o_spec=pl.BlockSpec((tm, tk), lambda i, j, kk: (i, j)),
                    dims=NT, acc_shape=(tm, tk))


def lin_t_bi(name, gb, wb, out_dtype=F32):
    nb, m, n = gb.shape
    k = wb.shape[1]
    tm, tk = _row_tile(m), _col_tile(k)
    return _mm_call(name, gb, wb, grid=(m // tm, k // tk, nb),
                    a_spec=pl.BlockSpec((None, tm, n), lambda i, j, kk: (kk, i, 0)),
                    b_spec=pl.BlockSpec((None, tk, n), lambda i, j, kk: (kk, j, 0)),
                    o_shape=SDS((m, k), out_dtype), o_spec=pl.BlockSpec((tm, tk), lambda i, j, kk: (i, j)),
                    dims=NT, acc_shape=(tm, tk))


def lin_t_bo(name, g, wb, out_dtype=F32):
    m, n = g.shape
    nb, k, _ = wb.shape
    tm = _row_tile(m)
    return _mm_call(name, g, wb, grid=(m // tm, nb, 1),
                    a_spec=pl.BlockSpec((tm, n), lambda i, j, kk: (i, 0)),
                    b_spec=pl.BlockSpec((None, k, n), lambda i, j, kk: (j, 0, 0)),
                    o_shape=SDS((nb, m, k), out_dtype), o_spec=pl.BlockSpec((None, tm, k), lambda i, j, kk: (j, i, 0)),
                    dims=NT, acc_shape=(tm, k))


def wgrad(name, a, g):
    m, k = a.shape
    n = g.shape[1]
    tm, tn = _row_tile(m), _col_tile(n)
    return _mm_call(name, a, g, grid=(1, n // tn, m // tm),
                    a_spec=pl.BlockSpec((tm, k), lambda i, j, kk: (kk, 0)),
                    b_spec=pl.BlockSpec((tm, tn), lambda i, j, kk: (kk, j)),
                    o_shape=SDS((k, n), F32), o_spec=pl.BlockSpec((k, tn), lambda i, j, kk: (0, j)),
                    dims=TN, acc_shape=(k, tn))


def wgrad_bo(name, a, gb):
    m, k = a.shape
    nb, _, n = gb.shape
    tm = _row_tile(m)
    return _mm_call(name, a, gb, grid=(nb, 1, m // tm),
                    a_spec=pl.BlockSpec((tm, k), lambda i, j, kk: (kk, 0)),
                    b_spec=pl.BlockSpec((None, tm, n), lambda i, j, kk: (i, kk, 0)),
                    o_shape=SDS((nb, k, n), F32), o_spec=pl.BlockSpec((None, k, n), lambda i, j, kk: (i, 0, 0)),
                    dims=TN, acc_shape=(k, n))


def wgrad_bi(name, zb, g):
    nb, m, k = zb.shape
    n = g.shape[1]
    tm, tn = _row_tile(m), _col_tile(n)
    return _mm_call(name, zb, g, grid=(nb, n // tn, m // tm),
                    a_spec=pl.BlockSpec((None, tm, k), lambda i, j, kk: (i, kk, 0)),
                    b_spec=pl.BlockSpec((tm, tn), lambda i, j, kk: (kk, j)),
                    o_shape=SDS((nb, k, n), F32), o_spec=pl.BlockSpec((None, k, tn), lambda i, j, kk: (i, 0, j)),
                    dims=TN, acc_shape=(k, tn))


class Arg:
    def __init__(self, arr, block, imap, shared=False, acc=False):
        self.arr, self.block, self.imap = arr, block, imap
        self.shared = shared
        self.acc = acc

    @property
    def spec(self):
        return pl.BlockSpec(self.block, self.imap)

    def vshape(self):
        return tuple(b for b in self.block if b is not None)


def _rev(arg, nt):
    return pl.BlockSpec(arg.block, lambda o, t, _f=arg.imap: _f(o, nt - 1 - t))


def seq_fwd(name, fn, grid, params, consts, xs, outs, carries=()):
    no, nt = grid
    n_p, n_c, n_x, n_y, n_k = len(params), len(consts), len(xs), len(outs), len(carries)

    def body(*refs):
        p_refs = refs[:n_p]
        c_refs = refs[n_p:n_p + n_c]
        x_refs = refs[n_p + n_c:n_p + n_c + n_x]
        r = n_p + n_c + n_x
        y_refs = refs[r:r + n_y]
        s_refs = refs[r + n_y:r + n_y + n_k]
        k_refs = refs[r + n_y + n_k:]
        t = pl.program_id(1)

        if n_k:
            @pl.when(t == 0)
            def _():
                for k in k_refs:
                    k[...] = jnp.zeros_like(k)

        carry = tuple(k[...] for k in k_refs)
        for s, c in zip(s_refs, carry):
            s[...] = c
        new_carry, ys = fn(tuple(p[...] for p in p_refs), tuple(c[...] for c in c_refs), carry,
                           tuple(x[...] for x in x_refs))
        for k, c in zip(k_refs, new_carry):
            k[...] = c
        for y_ref, y in zip(y_refs, ys):
            y_ref[...] = y.astype(y_ref.dtype)

    out_shape = [SDS(s, d) for (s, d, _, _) in outs]
    out_specs = [pl.BlockSpec(b, im) for (_, _, b, im) in outs]
    for cs in carries:
        out_shape.append(SDS((no, nt) + cs, F32))
        out_specs.append(pl.BlockSpec((None, None) + cs, lambda o, t, _n=len(cs): (o, t) + (0,) * _n))
    res = pl.pallas_call(
        body, name=name, grid=grid, in_specs=[a.spec for a in list(params) + list(consts) + list(xs)],
        out_specs=out_specs, out_shape=out_shape, scratch_shapes=[pltpu.VMEM(cs, F32) for cs in carries],
        compiler_params=_cp(2))(*[a.arr for a in list(params) + list(consts) + list(xs)])
    return res[:n_y], res[n_y:]


def seq_bwd(name, fn, grid, params, consts, xs, dys, saved=(), carries=()):
    no, nt = grid
    n_p, n_c, n_x, n_y, n_k = len(params), len(consts), len(xs), len(dys), len(carries)

    def body(*refs):
        p_refs = refs[:n_p]
        c_refs = refs[n_p:n_p + n_c]
        x_refs = refs[n_p + n_c:n_p + n_c + n_x]
        r = n_p + n_c + n_x
        g_refs = refs[r:r + n_y]
        s_refs = refs[r + n_y:r + n_y + n_k]
        r = r + n_y + n_k
        dx_refs = refs[r:r + n_x]
        dp_refs = refs[r + n_x:r + n_x + n_p]
        k_refs = refs[r + n_x + n_p:]
        o = pl.program_id(0)
        t = pl.program_id(1)

        if n_k:
            @pl.when(t == 0)
            def _():
                for k in k_refs:
                    k[...] = jnp.zeros_like(k)

        for a, dp in zip(params, dp_refs):
            @pl.when((t == 0) & (o == 0) if a.shared else (t == 0))
            def _(dp=dp):
                dp[...] = jnp.zeros_like(dp)

        for a, dx in zip(xs, dx_refs):
            if a.acc:
                @pl.when(t == 0)
                def _(dx=dx):
                    dx[...] = jnp.zeros_like(dx)

        consts_v = tuple(c[...] for c in c_refs)

        def f(pv, cv, xv):
            return fn(pv, consts_v, cv, xv)

        pv = tuple(p[...] for p in p_refs)
        cv = tuple(s[...] for s in s_refs)
        xv = tuple(x[...] for x in x_refs)
        (new_carry, ys), vjp = jax.vjp(f, pv, cv, xv)
        cot = (tuple(k[...] for k in k_refs), tuple(g[...].astype(y.dtype) for g, y in zip(g_refs, ys)))
        dpv, dcv, dxv = vjp(cot)
        for k, c in zip(k_refs, dcv):
            k[...] = c
        for dp, v in zip(dp_refs, dpv):
            dp[...] += v
        for a, dx, v in zip(xs, dx_refs, dxv):
            if a.acc:
                dx[...] += v
            else:
                dx[...] = v.astype(dx.dtype)

    in_specs = ([_rev(a, nt) for a in list(params) + list(consts) + list(xs) + list(dys)]
                + [pl.BlockSpec((None, None) + cs, lambda o, t, _n=len(cs): (o, nt - 1 - t) + (0,) * _n) for cs in carries])
    out_shape = [SDS(a.arr.shape, F32) for a in xs] + [SDS(a.arr.shape, F32) for a in params]
    out_specs = [_rev(a, nt) for a in list(xs) + list(params)]
    res = pl.pallas_call(
        body, name=name, grid=grid, in_specs=in_specs, out_specs=out_specs, out_shape=out_shape,
        scratch_shapes=[pltpu.VMEM(cs, F32) for cs in carries], compiler_params=_cp(2))(
            *[a.arr for a in list(params) + list(consts) + list(xs) + list(dys)], *saved)
    return res[:n_x], res[n_x:]


def _rowmask(lp):
    return (jnp.arange(lp) >= PAD).astype(F32)[:, None]


def _norm_fn(p, c, k, x):
    return (), (_rms(x[0] * c[0], p[0]),)


def _norm_b_fn(p, c, k, x):
    h = x[0] * c[0]
    return (), (_rms(h, p[0]), h)


def norm_fwd(name, h, g, mask, out_dtype=BF):
    lp, d = h.shape
    tr = _row_tile(lp)
    row = lambda o, t: (t, 0)
    (a,), _ = seq_fwd(name, _norm_fn, (1, lp // tr), [Arg(g, (1, d), lambda o, t: (0, 0), shared=True)],
                      [Arg(mask, (tr, 1), row)], [Arg(h, (tr, d), row)], [((lp, d), out_dtype, (tr, d), row)])
    return a


def norm_bwd(name, h, g, mask, da, dskip):
    lp, d = h.shape
    tr = _row_tile(lp)
    row = lambda o, t: (t, 0)
    (dh,), (dg,) = seq_bwd(name, _norm_b_fn, (1, lp // tr), [Arg(g, (1, d), lambda o, t: (0, 0), shared=True)],
                           [Arg(mask, (tr, 1), row)], [Arg(h, (tr, d), row)],
                           [Arg(da, (tr, d), row), Arg(dskip, (tr, d), row)])
    return dh, dg


def _ffn_tile(lp):
    return 320 if (lp % 320 == 0 and lp > 320) else 64


def _conv_rows(ext, w, b, n):
    u2 = ext[8:8 + n]
    u1 = pltpu.roll(ext, 1, 0)[8:8 + n]
    u0 = pltpu.roll(ext, 2, 0)[8:8 + n]
    return w[2] * u2 + w[1] * u1 + w[0] * u0 + b, (u0, u1, u2)


def ffn_core_fwd(name, u, cw, cb):
    _, nj, lp, fb = u.shape
    tr = _ffn_tile(lp)
    nt = lp // tr

    def body(u_ref, up_ref, w_ref, b_ref, z_ref):
        i = pl.program_id(1)
        prev = jnp.where(i > 0, up_ref[...], 0.0)
        cs = []
        for s in range(2):
            ext = jnp.concatenate([prev[s], u_ref[s]], axis=0)
            c, _ = _conv_rows(ext, w_ref[s], b_ref[s], tr)
            cs.append(c)
        z_ref[...] = (_silu(cs[0]) * cs[1]).astype(z_ref.dtype)

    return pl.pallas_call(
        body, name=name, grid=(nj, nt),
        in_specs=[pl.BlockSpec((2, None, tr, fb), lambda j, i: (0, j, i, 0)),
                  pl.BlockSpec((2, None, 8, fb), lambda j, i: (0, j, jnp.maximum(i * (tr // 8) - 1, 0), 0)),
                  pl.BlockSpec((2, None, 3, 1, fb), lambda j, i: (0, j, 0, 0, 0)),
                  pl.BlockSpec((2, None, 1, fb), lambda j, i: (0, j, 0, 0))],
        out_specs=pl.BlockSpec((None, tr, fb), lambda j, i: (j, i, 0)),
        out_shape=SDS((nj, lp, fb), BF), compiler_params=_cp(2))(u, u, cw, cb)


def ffn_core_bwd(name, u, dz, cw, cb):
    _, nj, lp, fb = u.shape
    tr = _ffn_tile(lp)
    nt = lp // tr
    nb8 = lp // 8

    def body(u_ref, up_ref, un_ref, dz_ref, dzn_ref, w_ref, b_ref, du_ref, dw_ref, db_ref):
        i = pl.program_id(1)

        @pl.when(i == 0)
        def _():
            dw_ref[...] = jnp.zeros_like(dw_ref)
            db_ref[...] = jnp.zeros_like(db_ref)

        prev = jnp.where(i > 0, up_ref[...], 0.0)
        nxt = jnp.where(i < nt - 1, un_ref[...], 0.0)
        dz_e = jnp.concatenate([dz_ref[...], jnp.where(i < nt - 1, dzn_ref[...], 0.0)], axis=0)
        n = tr + 8
        cs, taps = [], []
        for s in range(2):
            ext = jnp.concatenate([prev[s], u_ref[s], nxt[s]], axis=0)
            c, tp = _conv_rows(ext, w_ref[s], b_ref[s], n)
            cs.append(c)
            taps.append(tp)
        sg = jax.nn.sigmoid(cs[0])
        dcs = [dz_e * cs[1] * sg * (1.0 + cs[0] * (1.0 - sg)), dz_e * cs[0] * sg]
        for s in range(2):
            dc = dcs[s]
            w = w_ref[s]
            d1 = pltpu.roll(dc, n - 1, 0)[:tr]
            d2 = pltpu.roll(dc, n - 2, 0)[:tr]
            dcm = dc[:tr]
            du_ref[s] = w[2] * dcm + w[1] * d1 + w[0] * d2
            for k in range(3):
                dw_ref[s, k] += jnp.sum(dcm * taps[s][k][:tr], axis=0, keepdims=True)
            db_ref[s] += jnp.sum(dcm, axis=0, keepdims=True)

    return pl.pallas_call(
        body, name=name, grid=(nj, nt),
        in_specs=[pl.BlockSpec((2, None, tr, fb), lambda j, i: (0, j, i, 0)),
                  pl.BlockSpec((2, None, 8, fb), lambda j, i: (0, j, jnp.maximum(i * (tr // 8) - 1, 0), 0)),
                  pl.BlockSpec((2, None, 8, fb), lambda j, i: (0, j, jnp.minimum((i + 1) * (tr // 8), nb8 - 1), 0)),
                  pl.BlockSpec((None, tr, fb), lambda j, i: (j, i, 0)),
                  pl.BlockSpec((None, 8, fb), lambda j, i: (j, jnp.minimum((i + 1) * (tr // 8), nb8 - 1), 0)),
                  pl.BlockSpec((2, None, 3, 1, fb), lambda j, i: (0, j, 0, 0, 0)),
                  pl.BlockSpec((2, None, 1, fb), lambda j, i: (0, j, 0, 0))],
        out_specs=[pl.BlockSpec((2, None, tr, fb), lambda j, i: (0, j, i, 0)),
                   pl.BlockSpec((2, None, 3, 1, fb), lambda j, i: (0, j, 0, 0, 0)),
                   pl.BlockSpec((2, None, 1, fb), lambda j, i: (0, j, 0, 0))],
        out_shape=[SDS(u.shape, F32), SDS(cw.shape, F32), SDS(cb.shape, F32)],
        compiler_params=_cp(2))(u, u, u, dz, dz, cw, cb)


def ffn_fwd(i, h, mask, w):
    a = norm_fwd(f"ffn{i}_norm", h, w["ng"], mask)
    u = lin_bo(f"ffn{i}_up", a, w["up"])
    lp = h.shape[0]
    u = u.reshape(2, 4, lp, FFN_B)
    z = ffn_core_fwd(f"ffn{i}_core", u, w["cw"], w["cb"])
    h2 = lin_bi(f"ffn{i}_down", z, w["down"], res=h)
    return h2, (h, a, u, z)


def ffn_bwd(i, dh2, mask, w, saved):
    h, a, u, z = saved
    lp = h.shape[0]
    g = {}
    g["down"] = wgrad_bi(f"ffn{i}_dwdown", z, dh2)
    dz = lin_t_bo(f"ffn{i}_dz", dh2, w["down"])
    du, g["cw"], g["cb"] = ffn_core_bwd(f"ffn{i}_core_b", u, dz, w["cw"], w["cb"])
    du = du.reshape(8, lp, FFN_B)
    g["up"] = wgrad_bo(f"ffn{i}_dwup", a, du)
    da = lin_t_bi(f"ffn{i}_da", du, w["up"])
    dh, g["ng"] = norm_bwd(f"ffn{i}_norm_b", h, w["ng"], mask, da, dh2)
    return dh, g


def _hgrn_fn(p, c, k, x):
    lb, go = p
    q, f, iv, g = x
    (st,) = k
    qs = _silu(q)
    forget = lb + (1.0 - lb) * jax.nn.sigmoid(f)
    logf = jnp.log(forget)
    kk = 1.0 - forget
    r = lax.broadcasted_iota(jnp.int32, (HG_C, HG_C), 0)
    cc = lax.broadcasted_iota(jnp.int32, (HG_C, HG_C), 1)
    tri = (r >= cc).astype(F32)
    outs = []
    for s in range(CH // HG_C):
        sl = slice(HG_C * s, HG_C * (s + 1))
        lf = logf[sl]
        gc = cleft(tri, lf)
        gl = jnp.sum(lf, axis=0, keepdims=True)
        qd = qs[sl] * jnp.exp(gc)
        ki = kk[sl] * jnp.exp(-gc)
        kt = kk[sl] * jnp.exp(gl - gc)
        attn = jnp.where(r >= cc, mm_nt(qd, ki), 0.0)
        outs.append(mm(attn, iv[sl]) + mm_nt(qd, st))
        st = st * jnp.exp(gl) + mm_tn(iv[sl], kt)
    o = jnp.concatenate(outs, axis=0)
    return (st,), (_rms(o, go) * _silu(g),)


def _hgrn_args(u4, lb, go):
    lp = u4[0].shape[1]
    xs = [Arg(t, (None, CH, HG_D), lambda o, t: (o // 4, t, o % 4)) for t in u4]
    ps = [Arg(lb, (1, HG_D), lambda o, t: (0, o)), Arg(go, (1, HG_D), lambda o, t: (0, 0), shared=True)]
    return (HG_H, lp // CH), ps, xs


def hgrn_fwd(h, mask, w):
    lp = h.shape[0]
    a = norm_fwd("hgrn_norm", h, w["ng"], mask)
    u = lin_bo("hgrn_in", a, w["win"])
    u4 = [u[2 * s:2 * s + 2] for s in range(4)]
    grid, ps, xs = _hgrn_args(u4, w["lb"], w["go"])
    (z,), (st,) = seq_fwd("hgrn_core", _hgrn_fn, grid, ps, [], xs,
                          [((lp, D), BF, (CH, HG_D), lambda o, t: (t, o))], carries=[(HG_D, HG_D)])
    h2 = lin("hgrn_out", z, w["wo"], res=h)
    return h2, (h, a, u4, z, st)


def hgrn_bwd(dh2, mask, w, saved):
    h, a, u4, z, st = saved
    lp = h.shape[0]
    g = {}
    g["wo"] = wgrad("hgrn_dwo", z, dh2)
    dz = lin_t("hgrn_dz", dh2, w["wo"])
    grid, ps, xs = _hgrn_args(u4, w["lb"], w["go"])
    du4, (g["lb"], g["go"]) = seq_bwd("hgrn_core_b", _hgrn_fn, grid, ps, [], xs,
                                      [Arg(dz, (CH, HG_D), lambda o, t: (t, o))], saved=[st],
                                      carries=[(HG_D, HG_D)])
    du = jnp.concatenate(du4, axis=0)
    g["win"] = wgrad_bo("hgrn_dwin", a, du)
    da = lin_t_bi("hgrn_da", du, w["win"])
    dh, g["ng"] = norm_bwd("hgrn_norm_b", h, w["ng"], mask, da, dh2)
    return dh, g


S5_W = S5_SG * S5_P


def s5_tables(lam_re, lam_im, log_dt, b_re, b_im, c_re, c_im):
    dt = jnp.exp(log_dt)[:, None]
    mag = jnp.exp(lam_re * dt)
    abar_re = mag * jnp.cos(lam_im * dt)
    abar_im = mag * jnp.sin(lam_im * dt)
    den = lam_re * lam_re + lam_im * lam_im
    zoh_re = ((abar_re - 1.0) * lam_re + abar_im * lam_im) / den
    zoh_im = (abar_im * lam_re - (abar_re - 1.0) * lam_im) / den
    bbar_re = zoh_re[..., None] * b_re - zoh_im[..., None] * b_im
    bbar_im = zoh_re[..., None] * b_im + zoh_im[..., None] * b_re
    eye = jnp.eye(S5_SG, dtype=F32)

    def blockdiag_in(b):
        t = b.reshape(N_DEV, S5_SG, S5_P, S5_K).transpose(0, 1, 3, 2)
        return jnp.einsum("jakp,ab->jakbp", t, eye).reshape(N_DEV, S5_SG * S5_K, S5_W)

    def blockdiag_out(c):
        t = c.reshape(N_DEV, S5_SG, S5_K, S5_P).transpose(0, 1, 3, 2)
        return jnp.einsum("japk,ab->japbk", t, eye).reshape(N_DEV, S5_W, S5_SG * S5_K)

    wb = jnp.concatenate([blockdiag_in(bbar_re), blockdiag_in(bbar_im)], axis=2)
    wc = jnp.concatenate([blockdiag_out(c_re), -blockdiag_out(c_im)], axis=1)

    def powers(n):
        steps = n[:, None, None] * dt[None]
        pm = jnp.exp(lam_re[None] * steps)
        pr = (pm * jnp.cos(lam_im[None] * steps)).reshape(-1, N_DEV, S5_W).transpose(1, 0, 2)
        pi = (pm * jnp.sin(lam_im[None] * steps)).reshape(-1, N_DEV, S5_W).transpose(1, 0, 2)
        return jnp.concatenate([pr, pi], axis=2)

    apow = powers(2.0 ** jnp.arange(6, dtype=F32))[:, :, None, :]
    ptab = powers(jnp.arange(CH, dtype=F32) + 1.0)
    return wb, wc, apow, ptab


def _cmul(ar, ai, xr, xi):
    return ar * xr - ai * xi, ar * xi + ai * xr


def _s5_fn(p, c, k, x):
    wb, wc, apow, ptab, dsk = p
    (a,) = x
    (x0,) = k
    bu = mm(a, wb)
    xr, xi = bu[:, :S5_W], bu[:, S5_W:]
    for s in range(6):
        asr, asi = apow[s][:, :S5_W], apow[s][:, S5_W:]
        dr, di = _cmul(asr, asi, shift_rows(xr, 1 << s), shift_rows(xi, 1 << s))
        xr, xi = xr + dr, xi + di
    dr, di = _cmul(ptab[:, :S5_W], ptab[:, S5_W:], x0[:, :S5_W], x0[:, S5_W:])
    xr, xi = xr + dr, xi + di
    xx = jnp.concatenate([xr, xi], axis=1)
    last = lax.broadcasted_iota(jnp.int32, xx.shape, 0) == CH - 1
    x0n = jnp.sum(jnp.where(last, xx, 0.0), axis=0, keepdims=True)
    y = mm(xx, wc)
    return (x0n,), (jax.nn.gelu(y + dsk * a),)


def _s5_args(a, tb, dsk):
    lp = a.shape[0]
    wb, wc, apow, ptab = tb
    ps = [Arg(wb, (None, 128, 2 * S5_W), lambda o, t: (o, 0, 0)), Arg(wc, (None, 2 * S5_W, 128), lambda o, t: (o, 0, 0)),
          Arg(apow, (None, 6, 1, 2 * S5_W), lambda o, t: (o, 0, 0, 0)), Arg(ptab, (None, CH, 2 * S5_W), lambda o, t: (o, 0, 0)),
          Arg(dsk, (1, 128), lambda o, t: (0, o))]
    xs = [Arg(a, (CH, 128), lambda o, t: (t, o))]
    return (N_DEV, lp // CH), ps, xs


def _glu_res_fn(p, c, k, x):
    h, vg = x
    return (), ((h + vg[:, :D] * jax.nn.sigmoid(vg[:, D:])) * c[0],)


def _glu_args(h, vg, mask):
    lp = h.shape[0]
    tr = _row_tile(lp)
    row = lambda o, t: (t, 0)
    return (1, lp // tr), [Arg(mask, (tr, 1), row)], [Arg(h, (tr, D), row), Arg(vg, (tr, 2 * D), row)], tr


def s5_fwd(h, mask, w):
    lp = h.shape[0]
    a = norm_fwd("s5_norm", h, w["ng"], mask, out_dtype=F32)
    grid, ps, xs = _s5_args(a, w["tb"], w["dsk"])
    (z,), (st,) = seq_fwd("s5_core", _s5_fn, grid, ps, [], xs,
                          [((lp, D), BF, (CH, 128), lambda o, t: (t, o))], carries=[(1, 2 * S5_W)])
    vg = lin("s5_glu", z, w["wglu"])
    grid2, cs, xs2, tr = _glu_args(h, vg, mask)
    (h2,), _ = seq_fwd("s5_res", _glu_res_fn, grid2, [], cs, xs2, [((lp, D), F32, (tr, D), lambda o, t: (t, 0))])
    return h2, (h, a, z, vg, st)


def s5_bwd(dh2, mask, w, saved):
    h, a, z, vg, st = saved
    g = {}
    grid2, cs, xs2, tr = _glu_args(h, vg, mask)
    (dskip, dvg), _ = seq_bwd("s5_res_b", _glu_res_fn, grid2, [], cs, xs2, [Arg(dh2, (tr, D), lambda o, t: (t, 0))])
    g["wglu"] = wgrad("s5_dwglu", z, dvg)
    dz = lin_t("s5_dz", dvg, w["wglu"])
    grid, ps, xs = _s5_args(a, w["tb"], w["dsk"])
    (da,), dps = seq_bwd("s5_core_b", _s5_fn, grid, ps, [], xs, [Arg(dz, (CH, 128), lambda o, t: (t, o))],
                         saved=[st], carries=[(1, 2 * S5_W)])
    g["tb"] = tuple(dps[:4])
    g["dsk"] = dps[4]
    dh, g["ng"] = norm_bwd("s5_norm_b", h, w["ng"], mask, da, dskip)
    return dh, g


def ret_consts(lp):
    pos = jnp.maximum(jnp.arange(lp, dtype=F32) - PAD, 0.0)
    inv = 1.0 / (ROPE_BASE ** (jnp.arange(0, RET_DK, 2, dtype=F32) / RET_DK))
    ang = pos[:, None] * inv[None, :]
    lg = jnp.log(1.0 - jnp.exp2(-5.0 - jnp.arange(RET_H, dtype=F32)))
    p = jnp.arange(CH, dtype=F32)
    diff = p[:, None] - p[None, :]
    decay = jnp.where(diff >= 0, jnp.exp(diff[None] * lg[:, None, None]), 0.0)
    qd = jnp.exp((p[None, :] + 1.0) * lg[:, None])[..., None]
    kd = jnp.exp((CH - 1.0 - p[None, :]) * lg[:, None])[..., None]
    cd = jnp.exp(CH * lg)[:, None, None]
    return jnp.cos(ang), jnp.sin(ang), decay, qd, kd, cd


def _ret_fn(p, c, k, x):
    (gn,) = p
    cos, sin, decay, qd, kd, cd = c
    (st,) = k
    q, kk, v, g = x
    hd = RET_DK // 2

    def rope(t):
        t1, t2 = t[:, :hd], t[:, hd:]
        return jnp.concatenate([t1 * cos - t2 * sin, t1 * sin + t2 * cos], axis=1)

    qr = rope(q)
    kr = rope(kk) * (RET_DK ** -0.5)
    o = mm(mm_nt(qr, kr) * decay, v) + mm(qr * qd, st)
    st = st * cd + mm_tn(kr * kd, v)
    mu = jnp.mean(o, axis=-1, keepdims=True)
    var = jnp.mean(jnp.square(o - mu), axis=-1, keepdims=True)
    return (st,), ((o - mu) * lax.rsqrt(var + EPS) * gn * _silu(g),)


def _ret_args(u4, gn, rc):
    lp = u4[0].shape[0]
    cos, sin, decay, qd, kd, cd = rc
    ps = [Arg(gn, (1, RET_DV), lambda o, t: (0, o))]
    cs = [Arg(cos, (CH, RET_DK // 2), lambda o, t: (t, 0)), Arg(sin, (CH, RET_DK // 2), lambda o, t: (t, 0)),
          Arg(decay, (None, CH, CH), lambda o, t: (o, 0, 0)), Arg(qd, (None, CH, 1), lambda o, t: (o, 0, 0)),
          Arg(kd, (None, CH, 1), lambda o, t: (o, 0, 0)), Arg(cd, (None, 1, 1), lambda o, t: (o, 0, 0))]
    xs = [Arg(u4[0], (CH, RET_DK), lambda o, t: (t, o)), Arg(u4[1], (CH, RET_DK), lambda o, t: (t, o)),
          Arg(u4[2], (CH, RET_DV), lambda o, t: (t, o)), Arg(u4[3], (CH, RET_DV), lambda o, t: (t, o))]
    return (RET_H, lp // CH), ps, cs, xs


def ret_fwd(h, mask, w):
    lp = h.shape[0]
    a = norm_fwd("ret_norm", h, w["ng"], mask)
    u = lin("ret_in", a, w["win"])
    u4 = [u[:, :D], u[:, D:2 * D], u[:, 2 * D:4 * D], u[:, 4 * D:]]
    grid, ps, cs, xs = _ret_args(u4, w["gn"], w["rc"])
    (z,), (st,) = seq_fwd("ret_core", _ret_fn, grid, ps, cs, xs,
                          [((lp, 2 * D), BF, (CH, RET_DV), lambda o, t: (t, o))], carries=[(RET_DK, RET_DV)])
    h2 = lin("ret_out", z, w["wo"], res=h)
    return h2, (h, a, u4, z, st)


def ret_bwd(dh2, mask, w, saved):
    h, a, u4, z, st = saved
    g = {}
    g["wo"] = wgrad("ret_dwo", z, dh2)
    dz = lin_t("ret_dz", dh2, w["wo"])
    grid, ps, cs, xs = _ret_args(u4, w["gn"], w["rc"])
    du4, (g["gn"],) = seq_bwd("ret_core_b", _ret_fn, grid, ps, cs, xs,
                              [Arg(dz, (CH, RET_DV), lambda o, t: (t, o))], saved=[st], carries=[(RET_DK, RET_DV)])
    du = jnp.concatenate(du4, axis=1)
    g["win"] = wgrad("ret_dwin", a, du)
    da = lin_t("ret_da", du, w["win"])
    dh, g["ng"] = norm_bwd("ret_norm_b", h, w["ng"], mask, da, dh2)
    return dh, g


def mla_consts(lp):
    pos = jnp.maximum(jnp.arange(lp, dtype=F32) - PAD, 0.0)
    inv = 1.0 / (ROPE_BASE ** (jnp.arange(0, MLA_ROPE, 2, dtype=F32) / MLA_ROPE))
    ang = pos[:, None] * inv[None, :]
    cos = jnp.concatenate([jnp.cos(ang), jnp.cos(ang)], axis=1)
    sin = jnp.concatenate([jnp.sin(ang), jnp.sin(ang)], axis=1)
    hd = MLA_ROPE // 2
    i = jnp.arange(hd)
    rot = jnp.zeros((MLA_ROPE, MLA_ROPE), F32).at[hd + i, i].set(-1.0).at[i, hd + i].set(1.0)
    return cos, sin, rot


def _mla_prep1_fn(p, c, k, x):
    gq, gkv = p
    (down,) = x
    return (), (_rms(down[:, :MLA_QL], gq), _rms(down[:, MLA_QL:MLA_QL + MLA_KVL], gkv), down[:, MLA_QL + MLA_KVL:])


def _mla_prep2(p, c, x):
    gq, gk = p
    cos, sin, rot = c
    q, kv, kpe = x
    qn = _rms(q, gq)
    qn_n, qn_r = qn[:, :MLA_NOPE], qn[:, MLA_NOPE:]
    qo = jnp.concatenate([qn_n, qn_r * cos + cright(qn_r, rot) * sin], axis=1)
    kn = kv[:, :MLA_NOPE]
    ms = (jnp.sum(kn * kn, axis=-1, keepdims=True) + jnp.sum(kpe * kpe, axis=-1, keepdims=True)) / MLA_QK
    r = lax.rsqrt(ms + EPS)
    kr = kpe * r * gk[:, MLA_NOPE:]
    ko = jnp.concatenate([kn * r * gk[:, :MLA_NOPE], kr * cos + cright(kr, rot) * sin], axis=1)
    return qo, ko, kv[:, MLA_NOPE:]


def _mla_prep2_fn(p, c, k, x):
    return (), _mla_prep2(p, c, x)[:2]


def _mla_prep2_b_fn(p, c, k, x):
    return (), _mla_prep2(p, c, x)


def _prep1_args(down, gq, gkv):
    lp = down.shape[0]
    tr = _row_tile(lp)
    ps = [Arg(gq, (1, MLA_QL), lambda o, t: (0, 0), shared=True), Arg(gkv, (1, MLA_KVL), lambda o, t: (0, 0), shared=True)]
    return (1, lp // tr), ps, [Arg(down, (tr, down.shape[1]), lambda o, t: (t, 0))], tr


def _prep2_args(qraw, kvraw, kpe, gq, gk, mc):
    lp = kpe.shape[0]
    tr = _row_tile(lp)
    cos, sin, rot = mc
    ps = [Arg(gq, (1, MLA_QK), lambda o, t: (0, 0), shared=True), Arg(gk, (1, MLA_QK), lambda o, t: (0, 0), shared=True)]
    cs = [Arg(cos, (tr, MLA_ROPE), lambda o, t: (o, 0)), Arg(sin, (tr, MLA_ROPE), lambda o, t: (o, 0)),
          Arg(rot, (MLA_ROPE, MLA_ROPE), lambda o, t: (0, 0))]
    xs = [Arg(qraw, (None, tr, MLA_QK), lambda o, t: (t, o, 0)), Arg(kvraw, (None, tr, MLA_NOPE + MLA_V), lambda o, t: (t, o, 0)),
          Arg(kpe, (tr, MLA_ROPE), lambda o, t: (o, 0), acc=True)]
    return (lp // tr, MLA_H), ps, cs, xs, tr


def _attn_tile(lp):
    return 320 if (lp % 320 == 0 and lp > 320) else 64


def _attn_mask(qi, ki, ta):
    rows = qi * ta + lax.broadcasted_iota(jnp.int32, (ta, ta), 0)
    cols = ki * ta + lax.broadcasted_iota(jnp.int32, (ta, ta), 1)
    return (cols >= PAD) & ((cols // CH) <= (rows // CH))


def attn_fwd(q, k, kv):
    nh, lp, dq = q.shape
    ta = _attn_tile(lp)
    nb = lp // ta
    scale = MLA_QK ** -0.5

    def body(q_ref, k_ref, v_ref, o_ref, lse_ref, m_s, l_s, acc_s):
        qi, ki = pl.program_id(1), pl.program_id(2)

        @pl.when(ki == 0)
        def _():
            m_s[...] = jnp.full_like(m_s, NEG)
            l_s[...] = jnp.zeros_like(l_s)
            acc_s[...] = jnp.zeros_like(acc_s)

        @pl.when(ki <= qi)
        def _():
            s = _bdot(q_ref[...], k_ref[...], 1, 1) * scale
            s = jnp.where(_attn_mask(qi, ki, ta), s, NEG)
            m_new = jnp.maximum(m_s[...], jnp.max(s, axis=-1, keepdims=True))
            p = jnp.exp(s - m_new)
            alpha = jnp.exp(m_s[...] - m_new)
            l_s[...] = alpha * l_s[...] + jnp.sum(p, axis=-1, keepdims=True)
            acc_s[...] = alpha * acc_s[...] + _bdot(p, v_ref[...], 1, 0)
            m_s[...] = m_new

        @pl.when(ki == nb - 1)
        def _():
            o_ref[...] = (acc_s[...] / l_s[...]).astype(o_ref.dtype)
            lse_ref[...] = m_s[...] + jnp.log(l_s[...])

    return pl.pallas_call(
        body, name="mla_attn", grid=(nh, nb, nb),
        in_specs=[pl.BlockSpec((None, ta, dq), lambda h, qi, ki: (h, qi, 0)),
                  pl.BlockSpec((None, ta, dq), lambda h, qi, ki: (h, jnp.minimum(ki, qi), 0)),
                  pl.BlockSpec((None, ta, MLA_V), lambda h, qi, ki: (h, jnp.minimum(ki, qi), 1))],
        out_specs=[pl.BlockSpec((ta, MLA_V), lambda h, qi, ki: (qi, h)),
                   pl.BlockSpec((None, ta, 1), lambda h, qi, ki: (h, qi, 0))],
        out_shape=[SDS((lp, nh * MLA_V), BF), SDS((nh, lp, 1), F32)],
        scratch_shapes=[pltpu.VMEM((ta, 1), F32), pltpu.VMEM((ta, 1), F32), pltpu.VMEM((ta, MLA_V), F32)],
        compiler_params=_cp(3))(q, k, kv)


def attn_bwd(q, k, kv, o, do, lse):
    nh, lp, dq = q.shape
    ta = _attn_tile(lp)
    nb = lp // ta
    scale = MLA_QK ** -0.5

    def body(q_ref, k_ref, v_ref, o_ref, do_ref, lse_ref, dq_ref, dk_ref, dv_ref, dk_s, dv_s):
        ki, qi = pl.program_id(1), pl.program_id(2)

        @pl.when((ki == 0) & (qi == 0))
        def _():
            dq_ref[...] = jnp.zeros_like(dq_ref)

        @pl.when(qi == 0)
        def _():
            dk_s[...] = jnp.zeros_like(dk_s)
            dv_s[...] = jnp.zeros_like(dv_s)

        @pl.when(qi >= ki)
        def _():
            dov = do_ref[...]
            s = _bdot(q_ref[...], k_ref[...], 1, 1) * scale
            s = jnp.where(_attn_mask(qi, ki, ta), s, NEG)
            p = jnp.exp(s - lse_ref[...])
            delta = jnp.sum(dov * o_ref[...].astype(F32), axis=-1, keepdims=True)
            dv_s[...] += _bdot(p, dov, 0, 0)
            dp = _bdot(dov, v_ref[...], 1, 1)
            ds = p * (dp - delta) * scale
            rows = pl.ds(pl.multiple_of(qi * ta, ta), ta)
            dq_ref[rows, :] += _bdot(ds, k_ref[...], 1, 0)
            dk_s[...] += _bdot(ds, q_ref[...], 0, 0)

        @pl.when(qi == nb - 1)
        def _():
            dk_ref[...] = dk_s[...]
            dv_ref[...] = dv_s[...]

    qmap = lambda h, ki, qi: (h, jnp.maximum(qi, ki), 0)
    return pl.pallas_call(
        body, name="mla_attn_b", grid=(nh, nb, nb),
        in_specs=[pl.BlockSpec((None, ta, dq), qmap),
                  pl.BlockSpec((None, ta, dq), lambda h, ki, qi: (h, ki, 0)),
                  pl.BlockSpec((None, ta, MLA_V), lambda h, ki, qi: (h, ki, 1)),
                  pl.BlockSpec((ta, MLA_V), lambda h, ki, qi: (jnp.maximum(qi, ki), h)),
                  pl.BlockSpec((ta, MLA_V), lambda h, ki, qi: (jnp.maximum(qi, ki), h)),
                  pl.BlockSpec((None, ta, 1), qmap)],
        out_specs=[pl.BlockSpec((None, lp, dq), lambda h, ki, qi: (h, 0, 0)),
                   pl.BlockSpec((None, ta, dq), lambda h, ki, qi: (h, ki, 0)),
                   pl.BlockSpec((None, ta, MLA_V), lambda h, ki, qi: (h, ki, 0))],
        out_shape=[SDS((nh, lp, dq), F32), SDS((nh, lp, dq), F32), SDS((nh, lp, MLA_V), F32)],
        scratch_shapes=[pltpu.VMEM((ta, dq), F32), pltpu.VMEM((ta, MLA_V), F32)],
        compiler_params=_cp(3))(q, k, kv, o, do, lse)


def mla_fwd(h, mask, w):
    lp = h.shape[0]
    a = norm_fwd("mla_norm", h, w["ng"], mask)
    down = lin("mla_down", a, w["wdown"])
    grid, ps, xs, tr = _prep1_args(down, w["gcq"], w["gckv"])
    row = lambda o, t: (t, 0)
    (cq, ckv, kpe), _ = seq_fwd("mla_prep1", _mla_prep1_fn, grid, ps, [], xs,
                                [((lp, MLA_QL), BF, (tr, MLA_QL), row), ((lp, MLA_KVL), BF, (tr, MLA_KVL), row),
                                 ((lp, MLA_ROPE), F32, (tr, MLA_ROPE), row)])
    qraw = lin_bo("mla_uq", cq, w["wuq"])
    kvraw = lin_bo("mla_ukv", ckv, w["wukv"])
    grid, ps, cs, xs, tr = _prep2_args(qraw, kvraw, kpe, w["gq"], w["gk"], w["mc"])
    hm = lambda o, t: (t, o, 0)
    (q, k), _ = seq_fwd("mla_prep2", _mla_prep2_fn, grid, ps, cs, xs,
                        [((MLA_H, lp, MLA_QK), BF, (None, tr, MLA_QK), hm), ((MLA_H, lp, MLA_QK), BF, (None, tr, MLA_QK), hm)])
    o, lse = attn_fwd(q, k, kvraw)
    h2 = lin("mla_out", o, w["wo"], res=h)
    return h2, (h, a, down, cq, ckv, kpe, qraw, kvraw, q, k, o, lse)


def mla_bwd(dh2, mask, w, saved):
    h, a, down, cq, ckv, kpe, qraw, kvraw, q, k, o, lse = saved
    lp = h.shape[0]
    g = {}
    g["wo"] = wgrad("mla_dwo", o, dh2)
    do = lin_t("mla_do", dh2, w["wo"])
    dq, dk, dv = attn_bwd(q, k, kvraw, o, do, lse)
    grid, ps, cs, xs, tr = _prep2_args(qraw, kvraw, kpe, w["gq"], w["gk"], w["mc"])
    hm = lambda o, t: (t, o, 0)
    (dqraw, dkvraw, dkpe), (g["gq"], g["gk"]) = seq_bwd(
        "mla_prep2_b", _mla_prep2_b_fn, grid, ps, cs, xs,
        [Arg(dq, (None, tr, MLA_QK), hm), Arg(dk, (None, tr, MLA_QK), hm), Arg(dv, (None, tr, MLA_V), hm)])
    g["wuq"] = wgrad_bo("mla_dwuq", cq, dqraw)
    dcq = lin_t_bi("mla_dcq", dqraw, w["wuq"])
    g["wukv"] = wgrad_bo("mla_dwukv", ckv, dkvraw)
    dckv = lin_t_bi("mla_dckv", dkvraw, w["wukv"])
    grid, ps, xs, tr = _prep1_args(down, w["gcq"], w["gckv"])
    row = lambda o, t: (t, 0)
    (ddown,), (g["gcq"], g["gckv"]) = seq_bwd(
        "mla_prep1_b", _mla_prep1_fn, grid, ps, [], xs,
        [Arg(dcq, (tr, MLA_QL), row), Arg(dckv, (tr, MLA_KVL), row), Arg(dkpe, (tr, MLA_ROPE), row)])
    g["wdown"] = wgrad("mla_dwdown", a, ddown)
    da = lin_t("mla_da", ddown, w["wdown"])
    dh, g["ng"] = norm_bwd("mla_norm_b", h, w["ng"], mask, da, dh2)
    return dh, g


def loss_head(h, target):
    lp, d = h.shape
    assert OFF == CH

    def body(h_ref, t_ref, loss_ref, dh_ref):
        i = pl.program_id(0)

        @pl.when(i == 0)
        def _():
            loss_ref[...] = jnp.zeros_like(loss_ref)

        e = jnp.where(i > 0, h_ref[...] - t_ref[...], 0.0)
        loss_ref[...] += jnp.sum(e * e) * (0.5 / d)
        dh_ref[...] = e * (1.0 / d)

    return pl.pallas_call(
        body, name="loss_head", grid=(lp // CH,),
        in_specs=[pl.BlockSpec((CH, d), lambda i: (i, 0)), pl.BlockSpec((CH, d), lambda i: (jnp.maximum(i - 1, 0), 0))],
        out_specs=[pl.BlockSpec((8, 128), lambda i: (0, 0)), pl.BlockSpec((CH, d), lambda i: (i, 0))],
        out_shape=[SDS((8, 128), F32), SDS((lp, d), F32)], compiler_params=_cp(1))(h, target)


def _adam_tile(r):
    if r % 8:
        return r
    best = 8
    for t in range(8, min(r, 512) + 1, 8):
        if r % t == 0:
            best = t
    return best


def adamw(name, land, w, m, v):
    r, c = w.shape
    tr = _adam_tile(r)
    c1 = 1.0 / (1.0 - ADAM_B1 ** ADAM_STEP)
    c2 = 1.0 / (1.0 - ADAM_B2 ** ADAM_STEP)

    def body(l_ref, w_ref, m_ref, v_ref, g_ref, d_ref, nm_ref, nv_ref):
        g = l_ref[0]
        for i in range(1, N_DEV):
            g = g + l_ref[i]
        nm = ADAM_B1 * m_ref[...] + (1.0 - ADAM_B1) * g
        nv = ADAM_B2 * v_ref[...] + (1.0 - ADAM_B2) * (g * g)
        g_ref[...] = g
        nm_ref[...] = nm
        nv_ref[...] = nv
        d_ref[...] = -ADAM_LR * ((nm * c1) / (jnp.sqrt(nv * c2) + ADAM_EPS) + ADAM_WD * w_ref[...])

    blk = pl.BlockSpec((tr, c), lambda i: (i, 0))
    return pl.pallas_call(
        body, name=name, grid=(r // tr,),
        in_specs=[pl.BlockSpec((N_DEV, tr, c), lambda i: (0, i, 0)), blk, blk, blk],
        out_specs=[blk, blk, blk, blk], out_shape=[SDS((r, c), F32)] * 4, compiler_params=_cp(1))(land, w, m, v)


ANY = pl.BlockSpec(memory_space=pl.ANY)
MESH = pl.DeviceIdType.MESH


def _me():
    return lax.axis_index("x"), lax.axis_index("y"), lax.axis_index("c")


def _peers():
    x, y, c = _me()
    out = []
    for k in range(1, N_DEV):
        px = 1 - x if k & 4 else x
        py = 1 - y if k & 2 else y
        pc = 1 - c if k & 1 else c
        out.append(((px, py, pc), 4 * px + 2 * py + pc))
    return out


def all_gather(name, xs):
    n = len(xs)

    def body(*refs):
        x_refs, o_refs = refs[:n], refs[n:2 * n]
        send, recv, loc = refs[2 * n:]
        x, y, c = _me()
        me = 4 * x + 2 * y + c
        peers = _peers()
        local = [pltpu.make_async_copy(x_refs[i], o_refs[i].at[me], loc.at[i]) for i in range(n)]
        for cp in local:
            cp.start()
        sends = []
        for k, (pid, _) in enumerate(peers):
            for i in range(n):
                cp = pltpu.make_async_remote_copy(src_ref=x_refs[i], dst_ref=o_refs[i].at[me], send_sem=send.at[i, k],
                                                  recv_sem=recv.at[i, k], device_id=pid, device_id_type=MESH)
                cp.start()
                sends.append(cp)
        for k, (pid, pidx) in enumerate(peers):
            for i in range(n):
                pltpu.make_async_remote_copy(src_ref=x_refs[i], dst_ref=o_refs[i].at[pidx], send_sem=send.at[i, k],
                                             recv_sem=recv.at[i, k], device_id=pid, device_id_type=MESH).wait_recv()
        for cp in sends:
            cp.wait_send()
        for cp in local:
            cp.wait()

    return pl.pallas_call(
        body, name=name, in_specs=[ANY] * n, out_specs=[ANY] * n,
        out_shape=[SDS((N_DEV,) + x.shape, x.dtype) for x in xs],
        scratch_shapes=[pltpu.SemaphoreType.DMA((n, N_DEV - 1)), pltpu.SemaphoreType.DMA((n, N_DEV - 1)),
                        pltpu.SemaphoreType.DMA((n,))])(*xs)


def all_to_all(name, groups):
    flat = [(gi, l, a) for gi, grp in enumerate(groups) for l, a in enumerate(grp)]
    n, ng = len(flat), len(groups)

    def body(*refs):
        x_refs, o_refs = refs[:n], refs[n:n + ng]
        send, recv, loc = refs[n + ng:]
        x, y, c = _me()
        me = 4 * x + 2 * y + c
        peers = _peers()
        local = [pltpu.make_async_copy(x_refs[i].at[me], o_refs[gi].at[me, l], loc.at[i]) for i, (gi, l, _) in enumerate(flat)]
        for cp in local:
            cp.start()
        sends = []
        for k, (pid, pidx) in enumerate(peers):
            for i, (gi, l, _) in enumerate(flat):
                cp = pltpu.make_async_remote_copy(src_ref=x_refs[i].at[pidx], dst_ref=o_refs[gi].at[me, l],
                                                  send_sem=send.at[i, k], recv_sem=recv.at[i, k], device_id=pid,
                                                  device_id_type=MESH)
                cp.start()
                sends.append(cp)
        for k, (pid, pidx) in enumerate(peers):
            for i, (gi, l, _) in enumerate(flat):
                pltpu.make_async_remote_copy(src_ref=x_refs[i].at[pidx], dst_ref=o_refs[gi].at[pidx, l],
                                             send_sem=send.at[i, k], recv_sem=recv.at[i, k], device_id=pid,
                                             device_id_type=MESH).wait_recv()
        for cp in sends:
            cp.wait_send()
        for cp in local:
            cp.wait()

    return pl.pallas_call(
        body, name=name, in_specs=[ANY] * n, out_specs=[ANY] * ng,
        out_shape=[SDS((N_DEV, len(grp)) + grp[0].shape[1:], grp[0].dtype) for grp in groups],
        scratch_shapes=[pltpu.SemaphoreType.DMA((n, N_DEV - 1)), pltpu.SemaphoreType.DMA((n, N_DEV - 1)),
                        pltpu.SemaphoreType.DMA((n,))])(*[a for _, _, a in flat])


WEIGHTS = ['meta_tokens', 'norm_mix_g', 'norm_ffn_g', 'mla_w_down', 'mla_cq_norm_g', 'mla_ckv_norm_g', 'mla_w_uq',
           'mla_w_ukv', 'mla_q_head_g', 'mla_k_head_g', 'mla_w_o', 'hgrn_w_in', 'hgrn_lb_logits', 'hgrn_o_norm_g',
           'hgrn_w_o', 's5_lam_re', 's5_lam_im', 's5_log_dt', 's5_b_re', 's5_b_im', 's5_c_re', 's5_c_im', 's5_d',
           's5_w_glu', 'ret_w_in', 'ret_gn_g', 'ret_w_o', 'ffn_w_up', 'ffn_conv_w', 'ffn_conv_b', 'ffn_w_down']
BIG = ['mla_w_down', 'mla_w_uq', 'mla_w_ukv', 'mla_w_o', 'hgrn_w_in', 'hgrn_w_o', 's5_w_glu', 'ret_w_in', 'ret_w_o',
       'ffn_w_up', 'ffn_w_down']
SMALL_SH = ['meta_tokens', 's5_d', 'ret_gn_g', 'ffn_conv_w']
SMALL_REP = ['norm_mix_g', 'norm_ffn_g', 'mla_cq_norm_g', 'mla_ckv_norm_g', 'mla_q_head_g', 'mla_k_head_g',
             'hgrn_lb_logits', 'hgrn_o_norm_g', 's5_lam_re', 's5_lam_im', 's5_log_dt', 's5_b_re', 's5_b_im',
             's5_c_re', 's5_c_im', 'ffn_conv_b']
LANE = 128


def _flat(arrs, mult):
    v = jnp.concatenate([a.reshape(-1) for a in arrs])
    pad = (-v.shape[0]) % mult
    return jnp.pad(v, (0, pad)).reshape(-1, LANE)


def _unflat(flat2d, like):
    v = flat2d.reshape(-1)
    out, o = [], 0
    for a in like:
        out.append(v[o:o + a.size].reshape(a.shape))
        o += a.size
    return out


def _lb_of(logits):
    cum = jnp.cumsum(jax.nn.softmax(logits, axis=0), axis=0)
    return (cum - cum[0:1])[1:2]


def _cols_to_blocks(g):
    k, n = g.shape
    return g.reshape(k, N_DEV, n // N_DEV).transpose(1, 0, 2)


def _blocks_to_cols(wb):
    nb, k, n = wb.shape
    return wb.transpose(1, 0, 2).reshape(k, nb * n)


def _local_step(x2, tgt, meta, wg, rep):
    lr = x2.shape[0]
    lp = lr + OFF
    mask = _rowmask(lp)
    h = jnp.concatenate([jnp.zeros((PAD, D), F32), meta, x2], axis=0)
    lb, lb_vjp = jax.vjp(_lb_of, rep['hgrn_lb_logits'])
    s5p = [rep[n][0] for n in ('s5_lam_re', 's5_lam_im', 's5_log_dt', 's5_b_re', 's5_b_im', 's5_c_re', 's5_c_im')]
    tb, tb_vjp = jax.vjp(s5_tables, *s5p)
    ngm, ngf = rep['norm_mix_g'], rep['norm_ffn_g']
    w_mla = dict(ng=ngm[0:1], wdown=wg['mla_w_down'], gcq=rep['mla_cq_norm_g'], gckv=rep['mla_ckv_norm_g'],
                 wuq=wg['mla_w_uq'], wukv=wg['mla_w_ukv'], gq=rep['mla_q_head_g'], gk=rep['mla_k_head_g'],
                 wo=wg['mla_w_o'], mc=mla_consts(lp))
    w_hg = dict(ng=ngm[1:2], win=wg['hgrn_w_in'], lb=lb, go=rep['hgrn_o_norm_g'], wo=wg['hgrn_w_o'])
    w_s5 = dict(ng=ngm[2:3], tb=tb, dsk=wg['s5_d'], wglu=wg['s5_w_glu'])
    w_ret = dict(ng=ngm[3:4], win=wg['ret_w_in'], gn=wg['ret_gn_g'], wo=wg['ret_w_o'], rc=ret_consts(lp))
    w_ffn = [dict(ng=ngf[i:i + 1], up=wg['ffn_w_up'][i], cw=wg['ffn_conv_w'][i],
                  cb=rep['ffn_conv_b'][i].reshape(2, 4, 1, FFN_B), down=wg['ffn_w_down'][i]) for i in range(4)]
    mix_f = [(mla_fwd, mla_bwd, w_mla), (hgrn_fwd, hgrn_bwd, w_hg), (s5_fwd, s5_bwd, w_s5), (ret_fwd, ret_bwd, w_ret)]

    saved = []
    for i in range(4):
        h, sv = mix_f[i][0](h, mask, mix_f[i][2])
        saved.append(sv)
        h, sv = ffn_fwd(i, h, mask, w_ffn[i])
        saved.append(sv)
    loss, dh = loss_head(h, tgt)
    gm, gf = [None] * 4, [None] * 4
    for i in reversed(range(4)):
        dh, gf[i] = ffn_bwd(i, dh, mask, w_ffn[i], saved[2 * i + 1])
        dh, gm[i] = mix_f[i][1](dh, mask, mix_f[i][2], saved[2 * i])

    g_mla, g_hg, g_s5, g_ret = gm
    ds5 = tb_vjp(g_s5['tb'])
    g = {
        'norm_mix_g': jnp.concatenate([t['ng'] for t in gm], axis=0),
        'norm_ffn_g': jnp.concatenate([t['ng'] for t in gf], axis=0),
        'mla_w_down': g_mla['wdown'].reshape(N_DEV, D // N_DEV, -1), 'mla_cq_norm_g': g_mla['gcq'],
        'mla_ckv_norm_g': g_mla['gckv'], 'mla_w_uq': g_mla['wuq'], 'mla_w_ukv': g_mla['wukv'],
        'mla_q_head_g': g_mla['gq'], 'mla_k_head_g': g_mla['gk'], 'mla_w_o': g_mla['wo'].reshape(N_DEV, D // N_DEV, D),
        'hgrn_w_in': g_hg['win'], 'hgrn_lb_logits': lb_vjp(g_hg['lb'])[0], 'hgrn_o_norm_g': g_hg['go'],
        'hgrn_w_o': g_hg['wo'].reshape(N_DEV, D // N_DEV, D),
        's5_lam_re': ds5[0][None], 's5_lam_im': ds5[1][None], 's5_log_dt': ds5[2][None], 's5_b_re': ds5[3][None],
        's5_b_im': ds5[4][None], 's5_c_re': ds5[5][None], 's5_c_im': ds5[6][None],
        's5_d': g_s5['dsk'].reshape(N_DEV, 1, -1), 's5_w_glu': _cols_to_blocks(g_s5['wglu']),
        'ret_w_in': _cols_to_blocks(g_ret['win']), 'ret_gn_g': g_ret['gn'].reshape(N_DEV, 1, -1),
        'ret_w_o': g_ret['wo'].reshape(N_DEV, 2 * D // N_DEV, D),
        'ffn_w_up': [t['up'] for t in gf],
        'ffn_conv_w': jnp.stack([t['cw'].reshape(N_DEV, 3, FFN_B) for t in gf], axis=1),
        'ffn_conv_b': jnp.stack([t['cb'].reshape(-1) for t in gf], axis=0),
        'ffn_w_down': [t['down'].reshape(N_DEV, FFN_F // N_DEV, D) for t in gf],
    }
    dmeta = dh[PAD:OFF].reshape(N_META, N_DEV, D // N_DEV).transpose(1, 0, 2)
    return loss, dh[OFF:], dmeta, g


def _step(args):
    w = {n: args[n] for n in WEIGHTS}
    x2, tgt = args['x'][0], args['loss_target'][0]

    big_items, big_slots = [], {}
    for n in BIG:
        t = w[n].astype(BF)
        parts = [t[l] for l in range(t.shape[0])]
        big_slots[n] = (len(big_items), len(parts))
        big_items += parts
    small_sh = _flat([w[n] for n in SMALL_SH], LANE)
    gathered = all_gather("gather_weights", big_items + [small_sh])

    def got(n, l=0):
        return gathered[big_slots[n][0] + l]

    sm = gathered[-1].reshape(N_DEV, -1)
    o = 0
    sm_parts = {}
    for n in SMALL_SH:
        sm_parts[n] = sm[:, o:o + w[n].size].reshape((N_DEV,) + w[n].shape)
        o += w[n].size
    meta = sm_parts['meta_tokens'].transpose(1, 0, 2).reshape(N_META, D)
    cw = sm_parts['ffn_conv_w']
    wg = {
        'mla_w_down': got('mla_w_down').reshape(D, -1), 'mla_w_uq': got('mla_w_uq'), 'mla_w_ukv': got('mla_w_ukv'),
        'mla_w_o': got('mla_w_o').reshape(D, D), 'hgrn_w_in': got('hgrn_w_in'), 'hgrn_w_o': got('hgrn_w_o').reshape(D, D),
        's5_w_glu': _blocks_to_cols(got('s5_w_glu')), 's5_d': sm_parts['s5_d'].reshape(1, D),
        'ret_w_in': _blocks_to_cols(got('ret_w_in')), 'ret_gn_g': sm_parts['ret_gn_g'].reshape(1, 2 * D),
        'ret_w_o': got('ret_w_o').reshape(2 * D, D),
        'ffn_w_up': [got('ffn_w_up', l) for l in range(4)],
        'ffn_conv_w': [cw[:, l].reshape(2, 4, 3, 1, FFN_B) for l in range(4)],
        'ffn_w_down': [got('ffn_w_down', l).reshape(4, FFN_B, D) for l in range(4)],
    }
    rep = {n: w[n] for n in SMALL_REP}

    loss, grad_x, dmeta, g = _local_step(x2, tgt, meta, wg, rep)
    g['meta_tokens'] = dmeta

    groups = [g[n] if isinstance(g[n], list) else [g[n]] for n in BIG]
    gsm = jnp.concatenate([g[n].reshape(N_DEV, -1) for n in SMALL_SH], axis=1)
    pad = (-gsm.shape[1]) % LANE
    gsm = jnp.pad(gsm, ((0, 0), (0, pad))).reshape(N_DEV, -1, LANE)
    lands = all_to_all("scatter_grads", groups + [[gsm]])
    grep = _flat([g[n] for n in SMALL_REP], 8 * LANE)
    (rep_land,) = all_gather("gather_small_grads", [grep])

    res = {}
    for n, land in zip(BIG, lands[:-1]):
        shp = w[n].shape
        c = shp[-1]
        out = adamw("adam_" + n, land.reshape(N_DEV, -1, c), w[n].reshape(-1, c), args['m_' + n].reshape(-1, c),
                    args['v_' + n].reshape(-1, c))
        res[n] = [t.reshape(shp) for t in out]
    sh_like = [w[n] for n in SMALL_SH]
    out = adamw("adam_small_sharded", lands[-1].reshape(N_DEV, -1, LANE), _flat(sh_like, LANE),
                _flat([args['m_' + n] for n in SMALL_SH], LANE), _flat([args['v_' + n] for n in SMALL_SH], LANE))
    for n, parts in zip(SMALL_SH, zip(*[_unflat(t, sh_like) for t in out])):
        res[n] = list(parts)
    rep_like = [w[n] for n in SMALL_REP]
    out = adamw("adam_small_replicated", rep_land, _flat(rep_like, 8 * LANE),
                _flat([args['m_' + n] for n in SMALL_REP], 8 * LANE), _flat([args['v_' + n] for n in SMALL_REP], 8 * LANE))
    for n, parts in zip(SMALL_REP, zip(*[_unflat(t, rep_like) for t in out])):
        res[n] = list(parts)

    total = lax.psum(loss[0, 0], ("x", "y", "c"))
    outs = [total, grad_x[None]]
    for k in range(4):
        outs += [res[n][k] for n in WEIGHTS]
    return tuple(outs)


def kernel(x, meta_tokens, norm_mix_g, norm_ffn_g, mla_w_down, mla_cq_norm_g, mla_ckv_norm_g, mla_w_uq, mla_w_ukv, mla_q_head_g, mla_k_head_g, mla_w_o, hgrn_w_in, hgrn_lb_logits, hgrn_o_norm_g, hgrn_w_o, s5_lam_re, s5_lam_im, s5_log_dt, s5_b_re, s5_b_im, s5_c_re, s5_c_im, s5_d, s5_w_glu, ret_w_in, ret_gn_g, ret_w_o, ffn_w_up, ffn_conv_w, ffn_conv_b, ffn_w_down, loss_target, m_meta_tokens, m_norm_mix_g, m_norm_ffn_g, m_mla_w_down, m_mla_cq_norm_g, m_mla_ckv_norm_g, m_mla_w_uq, m_mla_w_ukv, m_mla_q_head_g, m_mla_k_head_g, m_mla_w_o, m_hgrn_w_in, m_hgrn_lb_logits, m_hgrn_o_norm_g, m_hgrn_w_o, m_s5_lam_re, m_s5_lam_im, m_s5_log_dt, m_s5_b_re, m_s5_b_im, m_s5_c_re, m_s5_c_im, m_s5_d, m_s5_w_glu, m_ret_w_in, m_ret_gn_g, m_ret_w_o, m_ffn_w_up, m_ffn_conv_w, m_ffn_conv_b, m_ffn_w_down, v_meta_tokens, v_norm_mix_g, v_norm_ffn_g, v_mla_w_down, v_mla_cq_norm_g, v_mla_ckv_norm_g, v_mla_w_uq, v_mla_w_ukv, v_mla_q_head_g, v_mla_k_head_g, v_mla_w_o, v_hgrn_w_in, v_hgrn_lb_logits, v_hgrn_o_norm_g, v_hgrn_w_o, v_s5_lam_re, v_s5_lam_im, v_s5_log_dt, v_s5_b_re, v_s5_b_im, v_s5_c_re, v_s5_c_im, v_s5_d, v_s5_w_glu, v_ret_w_in, v_ret_gn_g, v_ret_w_o, v_ffn_w_up, v_ffn_conv_w, v_ffn_conv_b, v_ffn_w_down):
    return _step(dict(locals()))
```

```python
import functools
import math

import jax
import jax.numpy as jnp
from jax import lax
from jax.experimental import pallas as pl
from jax.experimental.pallas import tpu as pltpu

F32 = jnp.float32
BF = jnp.bfloat16
SDS = jax.ShapeDtypeStruct

N_DEV = 8
D = 1024
N_META = 16
PAD = 48
OFF = PAD + N_META
CH = 64
EPS = 1e-6
NEG = -1e30
ROPE_BASE = 10000.0

MLA_H, MLA_NOPE, MLA_ROPE, MLA_V = 8, 128, 64, 128
MLA_QK = MLA_NOPE + MLA_ROPE
MLA_QL, MLA_KVL = 384, 256
HG_H, HG_D, HG_C = 8, 128, 16
S5_G, S5_P, S5_K = 64, 64, 16
S5_SG = 8
RET_H, RET_DK, RET_DV = 4, 256, 512
FFN_F = 2816
FFN_B = 704

ADAM_LR, ADAM_B1, ADAM_B2, ADAM_EPS, ADAM_WD, ADAM_STEP = 0.001, 0.9, 0.999, 1e-08, 0.01, 10

VMEM_LIMIT = 56 * 1024 * 1024
ARB = "arbitrary"


def _cp(n):
    return pltpu.CompilerParams(dimension_semantics=(ARB,) * n, vmem_limit_bytes=VMEM_LIMIT)


def _bdot(a, b, ca, cb):
    return lax.dot_general(a.astype(BF), b.astype(BF), (((ca,), (cb,)), ((), ())), preferred_element_type=F32)


@jax.custom_vjp
def mm(a, b):
    return _bdot(a, b, 1, 0)


@jax.custom_vjp
def mm_nt(a, b):
    return _bdot(a, b, 1, 1)


@jax.custom_vjp
def mm_tn(a, b):
    return _bdot(a, b, 0, 0)


mm.defvjp(lambda a, b: (mm(a, b), (a, b)),
          lambda r, g: (mm_nt(g, r[1]).astype(r[0].dtype), mm_tn(r[0], g).astype(r[1].dtype)))
mm_nt.defvjp(lambda a, b: (mm_nt(a, b), (a, b)),
             lambda r, g: (mm(g, r[1]).astype(r[0].dtype), mm_tn(g, r[0]).astype(r[1].dtype)))
mm_tn.defvjp(lambda a, b: (mm_tn(a, b), (a, b)),
             lambda r, g: (mm_nt(r[1], g).astype(r[0].dtype), mm(r[0], g).astype(r[1].dtype)))


def _xdot(a, b, ca, cb):
    return lax.dot_general(a, b, (((ca,), (cb,)), ((), ())), preferred_element_type=F32,
                           precision=lax.Precision.HIGHEST)


@jax.custom_vjp
def cleft(t, x):
    return _xdot(t, x, 1, 0)


cleft.defvjp(lambda t, x: (cleft(t, x), t), lambda t, g: (jnp.zeros_like(t), _xdot(t, g, 0, 0)))


@jax.custom_vjp
def cright(x, r):
    return _xdot(x, r, 1, 0)


cright.defvjp(lambda x, r: (cright(x, r), r), lambda r, g: (_xdot(g, r, 1, 1), jnp.zeros_like(r)))


def _shift_raw(x, s):
    n = x.shape[0]
    r = lax.broadcasted_iota(jnp.int32, x.shape, 0)
    y = pltpu.roll(x, s % n, 0)
    return jnp.where((r >= s) & (r < n + s), y, 0.0)


@functools.partial(jax.custom_vjp, nondiff_argnums=(1,))
def shift_rows(x, s):
    return _shift_raw(x, s)


shift_rows.defvjp(lambda x, s: (_shift_raw(x, s), None), lambda s, _, g: (_shift_raw(g, -s),))


def _rms(x, g):
    return x * lax.rsqrt(jnp.mean(x * x, axis=-1, keepdims=True) + EPS) * g


def _silu(x):
    return x * jax.nn.sigmoid(x)


def _mm_call(name, a, b, *, grid, a_spec, b_spec, o_shape, o_spec, dims, acc_shape, res=None, res_spec=None,
             mask_tm=None):
    nk = grid[2]

    def body(*refs):
        if res is None:
            a_ref, b_ref, o_ref, acc = refs
        else:
            a_ref, b_ref, r_ref, o_ref, acc = refs
        k = pl.program_id(2)

        @pl.when(k == 0)
        def _():
            acc[...] = jnp.zeros_like(acc)

        acc[...] += lax.dot_general(a_ref[...].astype(BF), b_ref[...].astype(BF), dims, preferred_element_type=F32)

        @pl.when(k == nk - 1)
        def _():
            v = acc[...]
            if res is not None:
                v = v + r_ref[...].astype(F32)
                rows = pl.program_id(0) * mask_tm + lax.broadcasted_iota(jnp.int32, v.shape, 0)
                v = jnp.where(rows >= PAD, v, 0.0)
            o_ref[...] = v.astype(o_ref.dtype)

    ins = [a, b] + ([res] if res is not None else [])
    specs = [a_spec, b_spec] + ([res_spec] if res is not None else [])
    return pl.pallas_call(body, name=name, grid=grid, in_specs=specs, out_specs=o_spec, out_shape=o_shape,
                          scratch_shapes=[pltpu.VMEM(acc_shape, F32)], compiler_params=_cp(3))(*ins)


NN = (((1,), (0,)), ((), ()))
NT = (((1,), (1,)), ((), ()))
TN = (((0,), (0,)), ((), ()))


def _row_tile(lp):
    for t in (832, 640, 320, 64):
        if lp % t == 0:
            return t
    raise ValueError(lp)


def _col_tile(n):
    for t in (1024, 768, 512, 384, 256, 128):
        if n % t == 0:
            return t
    return n


def lin(name, a, w, out_dtype=F32, res=None):
    m, k = a.shape
    n = w.shape[1]
    tm, tn, tc = _row_tile(m), _col_tile(n), _col_tile(k)
    return _mm_call(name, a, w, grid=(m // tm, n // tn, k // tc),
                    a_spec=pl.BlockSpec((tm, tc), lambda i, j, kk: (i, kk)),
                    b_spec=pl.BlockSpec((tc, tn), lambda i, j, kk: (kk, j)),
                    o_shape=SDS((m, n), out_dtype), o_spec=pl.BlockSpec((tm, tn), lambda i, j, kk: (i, j)),
                    dims=NN, acc_shape=(tm, tn), res=res,
                    res_spec=pl.BlockSpec((tm, tn), lambda i, j, kk: (i, j)), mask_tm=tm)


def lin_bo(name, a, wb, out_dtype=F32):
    m, k = a.shape
    nb, _, n = wb.shape
    tm = _row_tile(m)
    return _mm_call(name, a, wb, grid=(m // tm, nb, 1),
                    a_spec=pl.BlockSpec((tm, k), lambda i, j, kk: (i, 0)),
                    b_spec=pl.BlockSpec((None, k, n), lambda i, j, kk: (j, 0, 0)),
                    o_shape=SDS((nb, m, n), out_dtype), o_spec=pl.BlockSpec((None, tm, n), lambda i, j, kk: (j, i, 0)),
                    dims=NN, acc_shape=(tm, n))


def lin_bi(name, ab, wb, out_dtype=F32, res=None):
    nb, m, k = ab.shape
    n = wb.shape[2]
    tm, tn = _row_tile(m), _col_tile(n)
    return _mm_call(name, ab, wb, grid=(m // tm, n // tn, nb),
                    a_spec=pl.BlockSpec((None, tm, k), lambda i, j, kk: (kk, i, 0)),
                    b_spec=pl.BlockSpec((None, k, tn), lambda i, j, kk: (kk, 0, j)),
                    o_shape=SDS((m, n), out_dtype), o_spec=pl.BlockSpec((tm, tn), lambda i, j, kk: (i, j)),
                    dims=NN, acc_shape=(tm, tn), res=res,
                    res_spec=pl.BlockSpec((tm, tn), lambda i, j, kk: (i, j)), mask_tm=tm)


def lin_t(name, g, w, out_dtype=F32):
    m, n = g.shape
    k = w.shape[0]
    tm, tk, tc = _row_tile(m), _col_tile(k), _col_tile(n)
    return _mm_call(name, g, w, grid=(m // tm, k // tk, n // tc),
                    a_spec=pl.BlockSpec((tm, tc), lambda i, j, kk: (i, kk)),
                    b_spec=pl.BlockSpec((tk, tc), lambda i, j, kk: (j, kk)),
                    o_shape=SDS((m, k), out_dtype), o_spec=pl.BlockSpec((tm, tk), lambda i, j, kk: (i, j)),
                    dims=NT, acc_shape=(tm, tk))


def lin_t_bi(name, gb, wb, out_dtype=F32):
    nb, m, n = gb.shape
    k = wb.shape[1]
    tm, tk = _row_tile(m), _col_tile(k)
    return _mm_call(name, gb, wb, grid=(m // tm, k // tk, nb),
                    a_spec=pl.BlockSpec((None, tm, n), lambda i, j, kk: (kk, i, 0)),
                    b_spec=pl.BlockSpec((None, tk, n), lambda i, j, kk: (kk, j, 0)),
                    o_shape=SDS((m, k), out_dtype), o_spec=pl.BlockSpec((tm, tk), lambda i, j, kk: (i, j)),
                    dims=NT, acc_shape=(tm, tk))


def lin_t_bo(name, g, wb, out_dtype=F32):
    m, n = g.shape
    nb, k, _ = wb.shape
    tm = _row_tile(m)
    return _mm_call(name, g, wb, grid=(m // tm, nb, 1),
                    a_spec=pl.BlockSpec((tm, n), lambda i, j, kk: (i, 0)),
                    b_spec=pl.BlockSpec((None, k, n), lambda i, j, kk: (j, 0, 0)),
                    o_shape=SDS((nb, m, k), out_dtype), o_spec=pl.BlockSpec((None, tm, k), lambda i, j, kk: (j, i, 0)),
                    dims=NT, acc_shape=(tm, k))


def wgrad(name, a, g):
    m, k = a.shape
    n = g.shape[1]
    tm, tn = _row_tile(m), _col_tile(n)
    return _mm_call(name, a, g, grid=(1, n // tn, m // tm),
                    a_spec=pl.BlockSpec((tm, k), lambda i, j, kk: (kk, 0)),
                    b_spec=pl.BlockSpec((tm, tn), lambda i, j, kk: (kk, j)),
                    o_shape=SDS((k, n), F32), o_spec=pl.BlockSpec((k, tn), lambda i, j, kk: (0, j)),
                    dims=TN, acc_shape=(k, tn))


def wgrad_bo(name, a, gb):
    m, k = a.shape
    nb, _, n = gb.shape
    tm = _row_tile(m)
    return _mm_call(name, a, gb, grid=(nb, 1, m // tm),
                    a_spec=pl.BlockSpec((tm, k), lambda i, j, kk: (kk, 0)),
                    b_spec=pl.BlockSpec((None, tm, n), lambda i, j, kk: (i, kk, 0)),
                    o_shape=SDS((nb, k, n), F32), o_spec=pl.BlockSpec((None, k, n), lambda i, j, kk: (i, 0, 0)),
                    dims=TN, acc_shape=(k, n))


def wgrad_bi(name, zb, g):
    nb, m, k = zb.shape
    n = g.shape[1]
    tm, tn = _row_tile(m), _col_tile(n)
    return _mm_call(name, zb, g, grid=(nb, n // tn, m // tm),
                    a_spec=pl.BlockSpec((None, tm, k), lambda i, j, kk: (i, kk, 0)),
                    b_spec=pl.BlockSpec((tm, tn), lambda i, j, kk: (kk, j)),
                    o_shape=SDS((nb, k, n), F32), o_spec=pl.BlockSpec((None, k, tn), lambda i, j, kk: (i, 0, j)),
                    dims=TN, acc_shape=(k, tn))


class Arg:
    def __init__(self, arr, block, imap, shared=False, acc=False):
        self.arr, self.block, self.imap = arr, block, imap
        self.shared = shared
        self.acc = acc

    @property
    def spec(self):
        return pl.BlockSpec(self.block, self.imap)

    def vshape(self):
        return tuple(b for b in self.block if b is not None)


def _rev(arg, nt):
    return pl.BlockSpec(arg.block, lambda o, t, _f=arg.imap: _f(o, nt - 1 - t))


def seq_fwd(name, fn, grid, params, consts, xs, outs, carries=()):
    no, nt = grid
    n_p, n_c, n_x, n_y, n_k = len(params), len(consts), len(xs), len(outs), len(carries)

    def body(*refs):
        p_refs = refs[:n_p]
        c_refs = refs[n_p:n_p + n_c]
        x_refs = refs[n_p + n_c:n_p + n_c + n_x]
        r = n_p + n_c + n_x
        y_refs = refs[r:r + n_y]
        s_refs = refs[r + n_y:r + n_y + n_k]
        k_refs = refs[r + n_y + n_k:]
        t = pl.program_id(1)

        if n_k:
            @pl.when(t == 0)
            def _():
                for k in k_refs:
                    k[...] = jnp.zeros_like(k)

        carry = tuple(k[...] for k in k_refs)
        for s, c in zip(s_refs, carry):
            s[...] = c
        new_carry, ys = fn(tuple(p[...] for p in p_refs), tuple(c[...] for c in c_refs), carry,
                           tuple(x[...] for x in x_refs))
        for k, c in zip(k_refs, new_carry):
            k[...] = c
        for y_ref, y in zip(y_refs, ys):
            y_ref[...] = y.astype(y_ref.dtype)

    out_shape = [SDS(s, d) for (s, d, _, _) in outs]
    out_specs = [pl.BlockSpec(b, im) for (_, _, b, im) in outs]
    for cs in carries:
        out_shape.append(SDS((no, nt) + cs, F32))
        out_specs.append(pl.BlockSpec((None, None) + cs, lambda o, t, _n=len(cs): (o, t) + (0,) * _n))
    res = pl.pallas_call(
        body, name=name, grid=grid, in_specs=[a.spec for a in list(params) + list(consts) + list(xs)],
        out_specs=out_specs, out_shape=out_shape, scratch_shapes=[pltpu.VMEM(cs, F32) for cs in carries],
        compiler_params=_cp(2))(*[a.arr for a in list(params) + list(consts) + list(xs)])
    return res[:n_y], res[n_y:]


def seq_bwd(name, fn, grid, params, consts, xs, dys, saved=(), carries=()):
    no, nt = grid
    n_p, n_c, n_x, n_y, n_k = len(params), len(consts), len(xs), len(dys), len(carries)

    def body(*refs):
        p_refs = refs[:n_p]
        c_refs = refs[n_p:n_p + n_c]
        x_refs = refs[n_p + n_c:n_p + n_c + n_x]
        r = n_p + n_c + n_x
        g_refs = refs[r:r + n_y]
        s_refs = refs[r + n_y:r + n_y + n_k]
        r = r + n_y + n_k
        dx_refs = refs[r:r + n_x]
        dp_refs = refs[r + n_x:r + n_x + n_p]
        k_refs = refs[r + n_x + n_p:]
        o = pl.program_id(0)
        t = pl.program_id(1)

        if n_k:
            @pl.when(t == 0)
            def _():
                for k in k_refs:
                    k[...] = jnp.zeros_like(k)

        for a, dp in zip(params, dp_refs):
            @pl.when((t == 0) & (o == 0) if a.shared else (t == 0))
            def _(dp=dp):
                dp[...] = jnp.zeros_like(dp)

        for a, dx in zip(xs, dx_refs):
            if a.acc:
                @pl.when(t == 0)
                def _(dx=dx):
                    dx[...] = jnp.zeros_like(dx)

        consts_v = tuple(c[...] for c in c_refs)

        def f(pv, cv, xv):
            return fn(pv, consts_v, cv, xv)

        pv = tuple(p[...] for p in p_refs)
        cv = tuple(s[...] for s in s_refs)
        xv = tuple(x[...] for x in x_refs)
        (new_carry, ys), vjp = jax.vjp(f, pv, cv, xv)
        cot = (tuple(k[...] for k in k_refs), tuple(g[...].astype(y.dtype) for g, y in zip(g_refs, ys)))
        dpv, dcv, dxv = vjp(cot)
        for k, c in zip(k_refs, dcv):
            k[...] = c
        for dp, v in zip(dp_refs, dpv):
            dp[...] += v
        for a, dx, v in zip(xs, dx_refs, dxv):
            if a.acc:
                dx[...] += v
            else:
                dx[...] = v.astype(dx.dtype)

    in_specs = ([_rev(a, nt) for a in list(params) + list(consts) + list(xs) + list(dys)]
                + [pl.BlockSpec((None, None) + cs, lambda o, t, _n=len(cs): (o, nt - 1 - t) + (0,) * _n) for cs in carries])
    out_shape = [SDS(a.arr.shape, F32) for a in xs] + [SDS(a.arr.shape, F32) for a in params]
    out_specs = [_rev(a, nt) for a in list(xs) + list(params)]
    res = pl.pallas_call(
        body, name=name, grid=grid, in_specs=in_specs, out_specs=out_specs, out_shape=out_shape,
        scratch_shapes=[pltpu.VMEM(cs, F32) for cs in carries], compiler_params=_cp(2))(
            *[a.arr for a in list(params) + list(consts) + list(xs) + list(dys)], *saved)
    return res[:n_x], res[n_x:]


def _rowmask(lp):
    return (jnp.arange(lp) >= PAD).astype(F32)[:, None]


def _norm_fn(p, c, k, x):
    return (), (_rms(x[0] * c[0], p[0]),)


def _norm_b_fn(p, c, k, x):
    h = x[0] * c[0]
    return (), (_rms(h, p[0]), h)


def norm_fwd(name, h, g, mask, out_dtype=BF):
    lp, d = h.shape
    tr = _row_tile(lp)
    row = lambda o, t: (t, 0)
    (a,), _ = seq_fwd(name, _norm_fn, (1, lp // tr), [Arg(g, (1, d), lambda o, t: (0, 0), shared=True)],
                      [Arg(mask, (tr, 1), row)], [Arg(h, (tr, d), row)], [((lp, d), out_dtype, (tr, d), row)])
    return a


def norm_bwd(name, h, g, mask, da, dskip):
    lp, d = h.shape
    tr = _row_tile(lp)
    row = lambda o, t: (t, 0)
    (dh,), (dg,) = seq_bwd(name, _norm_b_fn, (1, lp // tr), [Arg(g, (1, d), lambda o, t: (0, 0), shared=True)],
                           [Arg(mask, (tr, 1), row)], [Arg(h, (tr, d), row)],
                           [Arg(da, (tr, d), row), Arg(dskip, (tr, d), row)])
    return dh, dg


def _ffn_tile(lp):
    return 320 if (lp % 320 == 0 and lp > 320) else 64


def _conv_rows(ext, w, b, n):
    u2 = ext[8:8 + n]
    u1 = pltpu.roll(ext, 1, 0)[8:8 + n]
    u0 = pltpu.roll(ext, 2, 0)[8:8 + n]
    return w[2] * u2 + w[1] * u1 + w[0] * u0 + b, (u0, u1, u2)


def ffn_core_fwd(name, u, cw, cb):
    _, nj, lp, fb = u.shape
    tr = _ffn_tile(lp)
    nt = lp // tr

    def body(u_ref, up_ref, w_ref, b_ref, z_ref):
        i = pl.program_id(1)
        prev = jnp.where(i > 0, up_ref[...], 0.0)
        cs = []
        for s in range(2):
            ext = jnp.concatenate([prev[s], u_ref[s]], axis=0)
            c, _ = _conv_rows(ext, w_ref[s], b_ref[s], tr)
            cs.append(c)
        z_ref[...] = (_silu(cs[0]) * cs[1]).astype(z_ref.dtype)

    return pl.pallas_call(
        body, name=name, grid=(nj, nt),
        in_specs=[pl.BlockSpec((2, None, tr, fb), lambda j, i: (0, j, i, 0)),
                  pl.BlockSpec((2, None, 8, fb), lambda j, i: (0, j, jnp.maximum(i * (tr // 8) - 1, 0), 0)),
                  pl.BlockSpec((2, None, 3, 1, fb), lambda j, i: (0, j, 0, 0, 0)),
                  pl.BlockSpec((2, None, 1, fb), lambda j, i: (0, j, 0, 0))],
        out_specs=pl.BlockSpec((None, tr, fb), lambda j, i: (j, i, 0)),
        out_shape=SDS((nj, lp, fb), BF), compiler_params=_cp(2))(u, u, cw, cb)


def ffn_core_bwd(name, u, dz, cw, cb):
    _, nj, lp, fb = u.shape
    tr = _ffn_tile(lp)
    nt = lp // tr
    nb8 = lp // 8

    def body(u_ref, up_ref, un_ref, dz_ref, dzn_ref, w_ref, b_ref, du_ref, dw_ref, db_ref):
        i = pl.program_id(1)

        @pl.when(i == 0)
        def _():
            dw_ref[...] = jnp.zeros_like(dw_ref)
            db_ref[...] = jnp.zeros_like(db_ref)

        prev = jnp.where(i > 0, up_ref[...], 0.0)
        nxt = jnp.where(i < nt - 1, un_ref[...], 0.0)
        dz_e = jnp.concatenate([dz_ref[...], jnp.where(i < nt - 1, dzn_ref[...], 0.0)], axis=0)
        n = tr + 8
        cs, taps = [], []
        for s in range(2):
            ext = jnp.concatenate([prev[s], u_ref[s], nxt[s]], axis=0)
            c, tp = _conv_rows(ext, w_ref[s], b_ref[s], n)
            cs.append(c)
            taps.append(tp)
        sg = jax.nn.sigmoid(cs[0])
        dcs = [dz_e * cs[1] * sg * (1.0 + cs[0] * (1.0 - sg)), dz_e * cs[0] * sg]
        for s in range(2):
            dc = dcs[s]
            w = w_ref[s]
            d1 = pltpu.roll(dc, n - 1, 0)[:tr]
            d2 = pltpu.roll(dc, n - 2, 0)[:tr]
            dcm = dc[:tr]
            du_ref[s] = w[2] * dcm + w[1] * d1 + w[0] * d2
            for k in range(3):
                dw_ref[s, k] += jnp.sum(dcm * taps[s][k][:tr], axis=0, keepdims=True)
            db_ref[s] += jnp.sum(dcm, axis=0, keepdims=True)

    return pl.pallas_call(
        body, name=name, grid=(nj, nt),
        in_specs=[pl.BlockSpec((2, None, tr, fb), lambda j, i: (0, j, i, 0)),
                  pl.BlockSpec((2, None, 8, fb), lambda j, i: (0, j, jnp.maximum(i * (tr // 8) - 1, 0), 0)),
                  pl.BlockSpec((2, None, 8, fb), lambda j, i: (0, j, jnp.minimum((i + 1) * (tr // 8), nb8 - 1), 0)),
                  pl.BlockSpec((None, tr, fb), lambda j, i: (j, i, 0)),
                  pl.BlockSpec((None, 8, fb), lambda j, i: (j, jnp.minimum((i + 1) * (tr // 8), nb8 - 1), 0)),
                  pl.BlockSpec((2, None, 3, 1, fb), lambda j, i: (0, j, 0, 0, 0)),
                  pl.BlockSpec((2, None, 1, fb), lambda j, i: (0, j, 0, 0))],
        out_specs=[pl.BlockSpec((2, None, tr, fb), lambda j, i: (0, j, i, 0)),
                   pl.BlockSpec((2, None, 3, 1, fb), lambda j, i: (0, j, 0, 0, 0)),
                   pl.BlockSpec((2, None, 1, fb), lambda j, i: (0, j, 0, 0))],
        out_shape=[SDS(u.shape, F32), SDS(cw.shape, F32), SDS(cb.shape, F32)],
        compiler_params=_cp(2))(u, u, u, dz, dz, cw, cb)


def ffn_fwd(i, h, mask, w):
    a = norm_fwd(f"ffn{i}_norm", h, w["ng"], mask)
    u = lin_bo(f"ffn{i}_up", a, w["up"])
    lp = h.shape[0]
    u = u.reshape(2, 4, lp, FFN_B)
    z = ffn_core_fwd(f"ffn{i}_core", u, w["cw"], w["cb"])
    h2 = lin_bi(f"ffn{i}_down", z, w["down"], res=h)
    return h2, (h, a, u, z)


def ffn_bwd(i, dh2, mask, w, saved):
    h, a, u, z = saved
    lp = h.shape[0]
    g = {}
    g["down"] = wgrad_bi(f"ffn{i}_dwdown", z, dh2)
    dz = lin_t_bo(f"ffn{i}_dz", dh2, w["down"])
    du, g["cw"], g["cb"] = ffn_core_bwd(f"ffn{i}_core_b", u, dz, w["cw"], w["cb"])
    du = du.reshape(8, lp, FFN_B)
    g["up"] = wgrad_bo(f"ffn{i}_dwup", a, du)
    da = lin_t_bi(f"ffn{i}_da", du, w["up"])
    dh, g["ng"] = norm_bwd(f"ffn{i}_norm_b", h, w["ng"], mask, da, dh2)
    return dh, g


def _hgrn_fn(p, c, k, x):
    lb, go = p
    q, f, iv, g = x
    (st,) = k
    qs = _silu(q)
    forget = lb + (1.0 - lb) * jax.nn.sigmoid(f)
    logf = jnp.log(forget)
    kk = 1.0 - forget
    r = lax.broadcasted_iota(jnp.int32, (HG_C, HG_C), 0)
    cc = lax.broadcasted_iota(jnp.int32, (HG_C, HG_C), 1)
    tri = (r >= cc).astype(F32)
    outs = []
    for s in range(CH // HG_C):
        sl = slice(HG_C * s, HG_C * (s + 1))
        lf = logf[sl]
        gc = cleft(tri, lf)
        gl = jnp.sum(lf, axis=0, keepdims=True)
        qd = qs[sl] * jnp.exp(gc)
        ki = kk[sl] * jnp.exp(-gc)
        kt = kk[sl] * jnp.exp(gl - gc)
        attn = jnp.where(r >= cc, mm_nt(qd, ki), 0.0)
        outs.append(mm(attn, iv[sl]) + mm_nt(qd, st))
        st = st * jnp.exp(gl) + mm_tn(iv[sl], kt)
    o = jnp.concatenate(outs, axis=0)
    return (st,), (_rms(o, go) * _silu(g),)


def _hgrn_args(u4, lb, go):
    lp = u4[0].shape[1]
    xs = [Arg(t, (None, CH, HG_D), lambda o, t: (o // 4, t, o % 4)) for t in u4]
    ps = [Arg(lb, (1, HG_D), lambda o, t: (0, o)), Arg(go, (1, HG_D), lambda o, t: (0, 0), shared=True)]
    return (HG_H, lp // CH), ps, xs


def hgrn_fwd(h, mask, w):
    lp = h.shape[0]
    a = norm_fwd("hgrn_norm", h, w["ng"], mask)
    u = lin_bo("hgrn_in", a, w["win"])
    u4 = [u[2 * s:2 * s + 2] for s in range(4)]
    grid, ps, xs = _hgrn_args(u4, w["lb"], w["go"])
    (z,), (st,) = seq_fwd("hgrn_core", _hgrn_fn, grid, ps, [], xs,
                          [((lp, D), BF, (CH, HG_D), lambda o, t: (t, o))], carries=[(HG_D, HG_D)])
    h2 = lin("hgrn_out", z, w["wo"], res=h)
    return h2, (h, a, u4, z, st)


def hgrn_bwd(dh2, mask, w, saved):
    h, a, u4, z, st = saved
    lp = h.shape[0]
    g = {}
    g["wo"] = wgrad("hgrn_dwo", z, dh2)
    dz = lin_t("hgrn_dz", dh2, w["wo"])
    grid, ps, xs = _hgrn_args(u4, w["lb"], w["go"])
    du4, (g["lb"], g["go"]) = seq_bwd("hgrn_core_b", _hgrn_fn, grid, ps, [], xs,
                                      [Arg(dz, (CH, HG_D), lambda o, t: (t, o))], saved=[st],
                                      carries=[(HG_D, HG_D)])
    du = jnp.concatenate(du4, axis=0)
    g["win"] = wgrad_bo("hgrn_dwin", a, du)
    da = lin_t_bi("hgrn_da", du, w["win"])
    dh, g["ng"] = norm_bwd("hgrn_norm_b", h, w["ng"], mask, da, dh2)
    return dh, g


S5_W = S5_SG * S5_P


def s5_tables(lam_re, lam_im, log_dt, b_re, b_im, c_re, c_im):
    dt = jnp.exp(log_dt)[:, None]
    mag = jnp.exp(lam_re * dt)
    abar_re = mag * jnp.cos(lam_im * dt)
    abar_im = mag * jnp.sin(lam_im * dt)
    den = lam_re * lam_re + lam_im * lam_im
    zoh_re = ((abar_re - 1.0) * lam_re + abar_im * lam_im) / den
    zoh_im = (abar_im * lam_re - (abar_re - 1.0) * lam_im) / den
    bbar_re = zoh_re[..., None] * b_re - zoh_im[..., None] * b_im
    bbar_im = zoh_re[..., None] * b_im + zoh_im[..., None] * b_re
    eye = jnp.eye(S5_SG, dtype=F32)

    def blockdiag_in(b):
        t = b.reshape(N_DEV, S5_SG, S5_P, S5_K).transpose(0, 1, 3, 2)
        return jnp.einsum("jakp,ab->jakbp", t, eye).reshape(N_DEV, S5_SG * S5_K, S5_W)

    def blockdiag_out(c):
        t = c.reshape(N_DEV, S5_SG, S5_K, S5_P).transpose(0, 1, 3, 2)
        return jnp.einsum("japk,ab->japbk", t, eye).reshape(N_DEV, S5_W, S5_SG * S5_K)

    wb = jnp.concatenate([blockdiag_in(bbar_re), blockdiag_in(bbar_im)], axis=2)
    wc = jnp.concatenate([blockdiag_out(c_re), -blockdiag_out(c_im)], axis=1)

    def powers(n):
        steps = n[:, None, None] * dt[None]
        pm = jnp.exp(lam_re[None] * steps)
        pr = (pm * jnp.cos(lam_im[None] * steps)).reshape(-1, N_DEV, S5_W).transpose(1, 0, 2)
        pi = (pm * jnp.sin(lam_im[None] * steps)).reshape(-1, N_DEV, S5_W).transpose(1, 0, 2)
        return jnp.concatenate([pr, pi], axis=2)

    apow = powers(2.0 ** jnp.arange(6, dtype=F32))[:, :, None, :]
    ptab = powers(jnp.arange(CH, dtype=F32) + 1.0)
    return wb, wc, apow, ptab


def _cmul(ar, ai, xr, xi):
    return ar * xr - ai * xi, ar * xi + ai * xr


def _s5_fn(p, c, k, x):
    wb, wc, apow, ptab, dsk = p
    (a,) = x
    (x0,) = k
    bu = mm(a, wb)
    xr, xi = bu[:, :S5_W], bu[:, S5_W:]
    for s in range(6):
        asr, asi = apow[s][:, :S5_W], apow[s][:, S5_W:]
        dr, di = _cmul(asr, asi, shift_rows(xr, 1 << s), shift_rows(xi, 1 << s))
        xr, xi = xr + dr, xi + di
    dr, di = _cmul(ptab[:, :S5_W], ptab[:, S5_W:], x0[:, :S5_W], x0[:, S5_W:])
    xr, xi = xr + dr, xi + di
    xx = jnp.concatenate([xr, xi], axis=1)
    last = lax.broadcasted_iota(jnp.int32, xx.shape, 0) == CH - 1
    x0n = jnp.sum(jnp.where(last, xx, 0.0), axis=0, keepdims=True)
    y = mm(xx, wc)
    return (x0n,), (jax.nn.gelu(y + dsk * a),)


def _s5_args(a, tb, dsk):
    lp = a.shape[0]
    wb, wc, apow, ptab = tb
    ps = [Arg(wb, (None, 128, 2 * S5_W), lambda o, t: (o, 0, 0)), Arg(wc, (None, 2 * S5_W, 128), lambda o, t: (o, 0, 0)),
          Arg(apow, (None, 6, 1, 2 * S5_W), lambda o, t: (o, 0, 0, 0)), Arg(ptab, (None, CH, 2 * S5_W), lambda o, t: (o, 0, 0)),
          Arg(dsk, (1, 128), lambda o, t: (0, o))]
    xs = [Arg(a, (CH, 128), lambda o, t: (t, o))]
    return (N_DEV, lp // CH), ps, xs


def _glu_res_fn(p, c, k, x):
    h, vg = x
    return (), ((h + vg[:, :D] * jax.nn.sigmoid(vg[:, D:])) * c[0],)


def _glu_args(h, vg, mask):
    lp = h.shape[0]
    tr = _row_tile(lp)
    row = lambda o, t: (t, 0)
    return (1, lp // tr), [Arg(mask, (tr, 1), row)], [Arg(h, (tr, D), row), Arg(vg, (tr, 2 * D), row)], tr


def s5_fwd(h, mask, w):
    lp = h.shape[0]
    a = norm_fwd("s5_norm", h, w["ng"], mask, out_dtype=F32)
    grid, ps, xs = _s5_args(a, w["tb"], w["dsk"])
    (z,), (st,) = seq_fwd("s5_core", _s5_fn, grid, ps, [], xs,
                          [((lp, D), BF, (CH, 128), lambda o, t: (t, o))], carries=[(1, 2 * S5_W)])
    vg = lin("s5_glu", z, w["wglu"])
    grid2, cs, xs2, tr = _glu_args(h, vg, mask)
    (h2,), _ = seq_fwd("s5_res", _glu_res_fn, grid2, [], cs, xs2, [((lp, D), F32, (tr, D), lambda o, t: (t, 0))])
    return h2, (h, a, z, vg, st)


def s5_bwd(dh2, mask, w, saved):
    h, a, z, vg, st = saved
    g = {}
    grid2, cs, xs2, tr = _glu_args(h, vg, mask)
    (dskip, dvg), _ = seq_bwd("s5_res_b", _glu_res_fn, grid2, [], cs, xs2, [Arg(dh2, (tr, D), lambda o, t: (t, 0))])
    g["wglu"] = wgrad("s5_dwglu", z, dvg)
    dz = lin_t("s5_dz", dvg, w["wglu"])
    grid, ps, xs = _s5_args(a, w["tb"], w["dsk"])
    (da,), dps = seq_bwd("s5_core_b", _s5_fn, grid, ps, [], xs, [Arg(dz, (CH, 128), lambda o, t: (t, o))],
                         saved=[st], carries=[(1, 2 * S5_W)])
    g["tb"] = tuple(dps[:4])
    g["dsk"] = dps[4]
    dh, g["ng"] = norm_bwd("s5_norm_b", h, w["ng"], mask, da, dskip)
    return dh, g


def ret_consts(lp):
    pos = jnp.maximum(jnp.arange(lp, dtype=F32) - PAD, 0.0)
    inv = 1.0 / (ROPE_BASE ** (jnp.arange(0, RET_DK, 2, dtype=F32) / RET_DK))
    ang = pos[:, None] * inv[None, :]
    lg = jnp.log(1.0 - jnp.exp2(-5.0 - jnp.arange(RET_H, dtype=F32)))
    p = jnp.arange(CH, dtype=F32)
    diff = p[:, None] - p[None, :]
    decay = jnp.where(diff >= 0, jnp.exp(diff[None] * lg[:, None, None]), 0.0)
    qd = jnp.exp((p[None, :] + 1.0) * lg[:, None])[..., None]
    kd = jnp.exp((CH - 1.0 - p[None, :]) * lg[:, None])[..., None]
    cd = jnp.exp(CH * lg)[:, None, None]
    return jnp.cos(ang), jnp.sin(ang), decay, qd, kd, cd


def _ret_fn(p, c, k, x):
    (gn,) = p
    cos, sin, decay, qd, kd, cd = c
    (st,) = k
    q, kk, v, g = x
    hd = RET_DK // 2

    def rope(t):
        t1, t2 = t[:, :hd], t[:, hd:]
        return jnp.concatenate([t1 * cos - t2 * sin, t1 * sin + t2 * cos], axis=1)

    qr = rope(q)
    kr = rope(kk) * (RET_DK ** -0.5)
    o = mm(mm_nt(qr, kr) * decay, v) + mm(qr * qd, st)
    st = st * cd + mm_tn(kr * kd, v)
    mu = jnp.mean(o, axis=-1, keepdims=True)
    var = jnp.mean(jnp.square(o - mu), axis=-1, keepdims=True)
    return (st,), ((o - mu) * lax.rsqrt(var + EPS) * gn * _silu(g),)


def _ret_args(u4, gn, rc):
    lp = u4[0].shape[0]
    cos, sin, decay, qd, kd, cd = rc
    ps = [Arg(gn, (1, RET_DV), lambda o, t: (0, o))]
    cs = [Arg(cos, (CH, RET_DK // 2), lambda o, t: (t, 0)), Arg(sin, (CH, RET_DK // 2), lambda o, t: (t, 0)),
          Arg(decay, (None, CH, CH), lambda o, t: (o, 0, 0)), Arg(qd, (None, CH, 1), lambda o, t: (o, 0, 0)),
          Arg(kd, (None, CH, 1), lambda o, t: (o, 0, 0)), Arg(cd, (None, 1, 1), lambda o, t: (o, 0, 0))]
    xs = [Arg(u4[0], (CH, RET_DK), lambda o, t: (t, o)), Arg(u4[1], (CH, RET_DK), lambda o, t: (t, o)),
          Arg(u4[2], (CH, RET_DV), lambda o, t: (t, o)), Arg(u4[3], (CH, RET_DV), lambda o, t: (t, o))]
    return (RET_H, lp // CH), ps, cs, xs


def ret_fwd(h, mask, w):
    lp = h.shape[0]
    a = norm_fwd("ret_norm", h, w["ng"], mask)
    u = lin("ret_in", a, w["win"])
    u4 = [u[:, :D], u[:, D:2 * D], u[:, 2 * D:4 * D], u[:, 4 * D:]]
    grid, ps, cs, xs = _ret_args(u4, w["gn"], w["rc"])
    (z,), (st,) = seq_fwd("ret_core", _ret_fn, grid, ps, cs, xs,
                          [((lp, 2 * D), BF, (CH, RET_DV), lambda o, t: (t, o))], carries=[(RET_DK, RET_DV)])
    h2 = lin("ret_out", z, w["wo"], res=h)
    return h2, (h, a, u4, z, st)


def ret_bwd(dh2, mask, w, saved):
    h, a, u4, z, st = saved
    g = {}
    g["wo"] = wgrad("ret_dwo", z, dh2)
    dz = lin_t("ret_dz", dh2, w["wo"])
    grid, ps, cs, xs = _ret_args(u4, w["gn"], w["rc"])
    du4, (g["gn"],) = seq_bwd("ret_core_b", _ret_fn, grid, ps, cs, xs,
                              [Arg(dz, (CH, RET_DV), lambda o, t: (t, o))], saved=[st], carries=[(RET_DK, RET_DV)])
    du = jnp.concatenate(du4, axis=1)
    g["win"] = wgrad("ret_dwin", a, du)
    da = lin_t("ret_da", du, w["win"])
    dh, g["ng"] = norm_bwd("ret_norm_b", h, w["ng"], mask, da, dh2)
    return dh, g


def mla_consts(lp):
    pos = jnp.maximum(jnp.arange(lp, dtype=F32) - PAD, 0.0)
    inv = 1.0 / (ROPE_BASE ** (jnp.arange(0, MLA_ROPE, 2, dtype=F32) / MLA_ROPE))
    ang = pos[:, None] * inv[None, :]
    cos = jnp.concatenate([jnp.cos(ang), jnp.cos(ang)], axis=1)
    sin = jnp.concatenate([jnp.sin(ang), jnp.sin(ang)], axis=1)
    hd = MLA_ROPE // 2
    r = lax.broadcasted_iota(jnp.int32, (MLA_ROPE, MLA_ROPE), 0)
    c = lax.broadcasted_iota(jnp.int32, (MLA_ROPE, MLA_ROPE), 1)
    rot = jnp.where(r == c + hd, -1.0, jnp.where(c == r + hd, 1.0, 0.0)).astype(F32)
    return cos, sin, rot


def _mla_prep1_fn(p, c, k, x):
    gq, gkv = p
    (down,) = x
    return (), (_rms(down[:, :MLA_QL], gq), _rms(down[:, MLA_QL:MLA_QL + MLA_KVL], gkv), down[:, MLA_QL + MLA_KVL:])


def _mla_prep2(p, c, x):
    gq, gk = p
    cos, sin, rot = c
    q, kv, kpe = x
    qn = _rms(q, gq)
    qn_n, qn_r = qn[:, :MLA_NOPE], qn[:, MLA_NOPE:]
    qo = jnp.concatenate([qn_n, qn_r * cos + cright(qn_r, rot) * sin], axis=1)
    kn = kv[:, :MLA_NOPE]
    ms = (jnp.sum(kn * kn, axis=-1, keepdims=True) + jnp.sum(kpe * kpe, axis=-1, keepdims=True)) / MLA_QK
    r = lax.rsqrt(ms + EPS)
    kr = kpe * r * gk[:, MLA_NOPE:]
    ko = jnp.concatenate([kn * r * gk[:, :MLA_NOPE], kr * cos + cright(kr, rot) * sin], axis=1)
    return qo, ko, kv[:, MLA_NOPE:]


def _mla_prep2_fn(p, c, k, x):
    return (), _mla_prep2(p, c, x)[:2]


def _mla_prep2_b_fn(p, c, k, x):
    return (), _mla_prep2(p, c, x)


def _prep1_args(down, gq, gkv):
    lp = down.shape[0]
    tr = _row_tile(lp)
    ps = [Arg(gq, (1, MLA_QL), lambda o, t: (0, 0), shared=True), Arg(gkv, (1, MLA_KVL), lambda o, t: (0, 0), shared=True)]
    return (1, lp // tr), ps, [Arg(down, (tr, down.shape[1]), lambda o, t: (t, 0))], tr


def _prep2_args(qraw, kvraw, kpe, gq, gk, mc):
    lp = kpe.shape[0]
    tr = _row_tile(lp)
    cos, sin, rot = mc
    ps = [Arg(gq, (1, MLA_QK), lambda o, t: (0, 0), shared=True), Arg(gk, (1, MLA_QK), lambda o, t: (0, 0), shared=True)]
    cs = [Arg(cos, (tr, MLA_ROPE), lambda o, t: (o, 0)), Arg(sin, (tr, MLA_ROPE), lambda o, t: (o, 0)),
          Arg(rot, (MLA_ROPE, MLA_ROPE), lambda o, t: (0, 0))]
    xs = [Arg(qraw, (None, tr, MLA_QK), lambda o, t: (t, o, 0)), Arg(kvraw, (None, tr, MLA_NOPE + MLA_V), lambda o, t: (t, o, 0)),
          Arg(kpe, (tr, MLA_ROPE), lambda o, t: (o, 0), acc=True)]
    return (lp // tr, MLA_H), ps, cs, xs, tr


def _attn_tile(lp):
    return 832 if (lp % 832 == 0 and lp > 832) else 64


def _attn_mask(qi, ki, ta):
    rows = qi * ta + lax.broadcasted_iota(jnp.int32, (ta, ta), 0)
    cols = ki * ta + lax.broadcasted_iota(jnp.int32, (ta, ta), 1)
    return (cols >= PAD) & ((cols // CH) <= (rows // CH))


def attn_fwd(q, k, kv):
    nh, lp, dq = q.shape
    ta = _attn_tile(lp)
    nb = lp // ta
    scale = MLA_QK ** -0.5

    def body(q_ref, k_ref, v_ref, o_ref, lse_ref, m_s, l_s, acc_s):
        qi, ki = pl.program_id(1), pl.program_id(2)

        @pl.when(ki == 0)
        def _():
            m_s[...] = jnp.full_like(m_s, NEG)
            l_s[...] = jnp.zeros_like(l_s)
            acc_s[...] = jnp.zeros_like(acc_s)

        def step(masked):
            s = _bdot(q_ref[...], k_ref[...], 1, 1) * scale
            if masked:
                s = jnp.where(_attn_mask(qi, ki, ta), s, NEG)
            m_new = jnp.maximum(m_s[...], jnp.max(s, axis=-1, keepdims=True))
            p = jnp.exp(s - m_new)
            alpha = jnp.exp(m_s[...] - m_new)
            l_s[...] = alpha * l_s[...] + jnp.sum(p, axis=-1, keepdims=True)
            acc_s[...] = alpha * acc_s[...] + _bdot(p, v_ref[...], 1, 0)
            m_s[...] = m_new

        pl.when((ki == qi) | (ki == 0))(functools.partial(step, True))
        pl.when((ki < qi) & (ki > 0))(functools.partial(step, False))

        @pl.when(ki == nb - 1)
        def _():
            o_ref[...] = (acc_s[...] / l_s[...]).astype(o_ref.dtype)
            lse_ref[...] = m_s[...] + jnp.log(l_s[...])

    return pl.pallas_call(
        body, name="mla_attn", grid=(nh, nb, nb),
        in_specs=[pl.BlockSpec((None, ta, dq), lambda h, qi, ki: (h, qi, 0)),
                  pl.BlockSpec((None, ta, dq), lambda h, qi, ki: (h, jnp.minimum(ki, qi), 0)),
                  pl.BlockSpec((None, ta, MLA_V), lambda h, qi, ki: (h, jnp.minimum(ki, qi), 1))],
        out_specs=[pl.BlockSpec((ta, MLA_V), lambda h, qi, ki: (qi, h)),
                   pl.BlockSpec((None, ta, 1), lambda h, qi, ki: (h, qi, 0))],
        out_shape=[SDS((lp, nh * MLA_V), BF), SDS((nh, lp, 1), F32)],
        scratch_shapes=[pltpu.VMEM((ta, 1), F32), pltpu.VMEM((ta, 1), F32), pltpu.VMEM((ta, MLA_V), F32)],
        compiler_params=_cp(3))(q, k, kv)


def attn_bwd(q, k, kv, o, do, lse):
    nh, lp, dq = q.shape
    ta = _attn_tile(lp)
    nb = lp // ta
    scale = MLA_QK ** -0.5

    def body(q_ref, k_ref, v_ref, o_ref, do_ref, lse_ref, dq_ref, dk_ref, dv_ref, dk_s, dv_s):
        ki, qi = pl.program_id(1), pl.program_id(2)

        @pl.when((ki == 0) & (qi == 0))
        def _():
            dq_ref[...] = jnp.zeros_like(dq_ref)

        @pl.when(qi == 0)
        def _():
            dk_s[...] = jnp.zeros_like(dk_s)
            dv_s[...] = jnp.zeros_like(dv_s)

        def step(masked):
            dov = do_ref[...]
            s = _bdot(q_ref[...], k_ref[...], 1, 1) * scale
            if masked:
                s = jnp.where(_attn_mask(qi, ki, ta), s, NEG)
            p = jnp.exp(s - lse_ref[...])
            delta = jnp.sum(dov * o_ref[...].astype(F32), axis=-1, keepdims=True)
            dv_s[...] += _bdot(p, dov, 0, 0)
            dp = _bdot(dov, v_ref[...], 1, 1)
            ds = p * (dp - delta) * scale
            rows = pl.ds(pl.multiple_of(qi * ta, ta), ta)
            dq_ref[rows, :] += _bdot(ds, k_ref[...], 1, 0)
            dk_s[...] += _bdot(ds, q_ref[...], 0, 0)

        pl.when((ki == qi) | (ki == 0))(functools.partial(step, True))
        pl.when((ki < qi) & (ki > 0))(functools.partial(step, False))

        @pl.when(qi == nb - 1)
        def _():
            dk_ref[...] = dk_s[...]
            dv_ref[...] = dv_s[...]

    qmap = lambda h, ki, qi: (h, jnp.maximum(qi, ki), 0)
    return pl.pallas_call(
        body, name="mla_attn_b", grid=(nh, nb, nb),
        in_specs=[pl.BlockSpec((None, ta, dq), qmap),
                  pl.BlockSpec((None, ta, dq), lambda h, ki, qi: (h, ki, 0)),
                  pl.BlockSpec((None, ta, MLA_V), lambda h, ki, qi: (h, ki, 1)),
                  pl.BlockSpec((ta, MLA_V), lambda h, ki, qi: (jnp.maximum(qi, ki), h)),
                  pl.BlockSpec((ta, MLA_V), lambda h, ki, qi: (jnp.maximum(qi, ki), h)),
                  pl.BlockSpec((None, ta, 1), qmap)],
        out_specs=[pl.BlockSpec((None, lp, dq), lambda h, ki, qi: (h, 0, 0)),
                   pl.BlockSpec((None, ta, dq), lambda h, ki, qi: (h, ki, 0)),
                   pl.BlockSpec((None, ta, MLA_V), lambda h, ki, qi: (h, ki, 0))],
        out_shape=[SDS((nh, lp, dq), F32), SDS((nh, lp, dq), F32), SDS((nh, lp, MLA_V), F32)],
        scratch_shapes=[pltpu.VMEM((ta, dq), F32), pltpu.VMEM((ta, MLA_V), F32)],
        compiler_params=_cp(3))(q, k, kv, o, do, lse)


def mla_fwd(h, mask, w):
    lp = h.shape[0]
    a = norm_fwd("mla_norm", h, w["ng"], mask)
    down = lin("mla_down", a, w["wdown"])
    grid, ps, xs, tr = _prep1_args(down, w["gcq"], w["gckv"])
    row = lambda o, t: (t, 0)
    (cq, ckv, kpe), _ = seq_fwd("mla_prep1", _mla_prep1_fn, grid, ps, [], xs,
                                [((lp, MLA_QL), BF, (tr, MLA_QL), row), ((lp, MLA_KVL), BF, (tr, MLA_KVL), row),
                                 ((lp, MLA_ROPE), F32, (tr, MLA_ROPE), row)])
    qraw = lin_bo("mla_uq", cq, w["wuq"])
    kvraw = lin_bo("mla_ukv", ckv, w["wukv"])
    grid, ps, cs, xs, tr = _prep2_args(qraw, kvraw, kpe, w["gq"], w["gk"], w["mc"])
    hm = lambda o, t: (t, o, 0)
    (q, k), _ = seq_fwd("mla_prep2", _mla_prep2_fn, grid, ps, cs, xs,
                        [((MLA_H, lp, MLA_QK), BF, (None, tr, MLA_QK), hm), ((MLA_H, lp, MLA_QK), BF, (None, tr, MLA_QK), hm)])
    o, lse = attn_fwd(q, k, kvraw)
    h2 = lin("mla_out", o, w["wo"], res=h)
    return h2, (h, a, down, cq, ckv, kpe, qraw, kvraw, q, k, o, lse)


def mla_bwd(dh2, mask, w, saved):
    h, a, down, cq, ckv, kpe, qraw, kvraw, q, k, o, lse = saved
    lp = h.shape[0]
    g = {}
    g["wo"] = wgrad("mla_dwo", o, dh2)
    do = lin_t("mla_do", dh2, w["wo"])
    dq, dk, dv = attn_bwd(q, k, kvraw, o, do, lse)
    grid, ps, cs, xs, tr = _prep2_args(qraw, kvraw, kpe, w["gq"], w["gk"], w["mc"])
    hm = lambda o, t: (t, o, 0)
    (dqraw, dkvraw, dkpe), (g["gq"], g["gk"]) = seq_bwd(
        "mla_prep2_b", _mla_prep2_b_fn, grid, ps, cs, xs,
        [Arg(dq, (None, tr, MLA_QK), hm), Arg(dk, (None, tr, MLA_QK), hm), Arg(dv, (None, tr, MLA_V), hm)])
    g["wuq"] = wgrad_bo("mla_dwuq", cq, dqraw)
    dcq = lin_t_bi("mla_dcq", dqraw, w["wuq"])
    g["wukv"] = wgrad_bo("mla_dwukv", ckv, dkvraw)
    dckv = lin_t_bi("mla_dckv", dkvraw, w["wukv"])
    grid, ps, xs, tr = _prep1_args(down, w["gcq"], w["gckv"])
    row = lambda o, t: (t, 0)
    (ddown,), (g["gcq"], g["gckv"]) = seq_bwd(
        "mla_prep1_b", _mla_prep1_fn, grid, ps, [], xs,
        [Arg(dcq, (tr, MLA_QL), row), Arg(dckv, (tr, MLA_KVL), row), Arg(dkpe, (tr, MLA_ROPE), row)])
    g["wdown"] = wgrad("mla_dwdown", a, ddown)
    da = lin_t("mla_da", ddown, w["wdown"])
    dh, g["ng"] = norm_bwd("mla_norm_b", h, w["ng"], mask, da, dh2)
    return dh, g


def loss_head(h, target):
    lp, d = h.shape
    assert OFF == CH

    def body(h_ref, t_ref, loss_ref, dh_ref):
        i = pl.program_id(0)

        @pl.when(i == 0)
        def _():
            loss_ref[...] = jnp.zeros_like(loss_ref)

        e = jnp.where(i > 0, h_ref[...] - t_ref[...], 0.0)
        loss_ref[...] += jnp.sum(e * e) * (0.5 / d)
        dh_ref[...] = e * (1.0 / d)

    return pl.pallas_call(
        body, name="loss_head", grid=(lp // CH,),
        in_specs=[pl.BlockSpec((CH, d), lambda i: (i, 0)), pl.BlockSpec((CH, d), lambda i: (jnp.maximum(i - 1, 0), 0))],
        out_specs=[pl.BlockSpec((8, 128), lambda i: (0, 0)), pl.BlockSpec((CH, d), lambda i: (i, 0))],
        out_shape=[SDS((8, 128), F32), SDS((lp, d), F32)], compiler_params=_cp(1))(h, target)


def _adam_tile(r):
    if r % 8:
        return r
    best = 8
    for t in range(8, min(r, 512) + 1, 8):
        if r % t == 0:
            best = t
    return best


def adamw(name, land, w, m, v):
    r, c = w.shape
    tr = _adam_tile(r)
    c1 = 1.0 / (1.0 - ADAM_B1 ** ADAM_STEP)
    c2 = 1.0 / (1.0 - ADAM_B2 ** ADAM_STEP)

    def body(l_ref, w_ref, m_ref, v_ref, g_ref, d_ref, nm_ref, nv_ref):
        g = l_ref[0]
        for i in range(1, N_DEV):
            g = g + l_ref[i]
        nm = ADAM_B1 * m_ref[...] + (1.0 - ADAM_B1) * g
        nv = ADAM_B2 * v_ref[...] + (1.0 - ADAM_B2) * (g * g)
        g_ref[...] = g
        nm_ref[...] = nm
        nv_ref[...] = nv
        d_ref[...] = -ADAM_LR * ((nm * c1) / (jnp.sqrt(nv * c2) + ADAM_EPS) + ADAM_WD * w_ref[...])

    blk = pl.BlockSpec((tr, c), lambda i: (i, 0))
    return pl.pallas_call(
        body, name=name, grid=(r // tr,),
        in_specs=[pl.BlockSpec((N_DEV, tr, c), lambda i: (0, i, 0)), blk, blk, blk],
        out_specs=[blk, blk, blk, blk], out_shape=[SDS((r, c), F32)] * 4, compiler_params=_cp(1))(land, w, m, v)


ANY = pl.BlockSpec(memory_space=pl.ANY)
MESH = pl.DeviceIdType.MESH


def _me():
    return lax.axis_index("x"), lax.axis_index("y"), lax.axis_index("c")


def _peers():
    x, y, c = _me()
    out = []
    for k in range(1, N_DEV):
        px = 1 - x if k & 4 else x
        py = 1 - y if k & 2 else y
        pc = 1 - c if k & 1 else c
        out.append(((px, py, pc), 4 * px + 2 * py + pc))
    return out


def all_gather(name, xs):
    n = len(xs)

    def body(*refs):
        x_refs, o_refs = refs[:n], refs[n:2 * n]
        send, recv, loc = refs[2 * n:]
        x, y, c = _me()
        me = 4 * x + 2 * y + c
        peers = _peers()
        local = [pltpu.make_async_copy(x_refs[i], o_refs[i].at[me], loc.at[i]) for i in range(n)]
        for cp in local:
            cp.start()
        sends = []
        for k, (pid, _) in enumerate(peers):
            for i in range(n):
                cp = pltpu.make_async_remote_copy(src_ref=x_refs[i], dst_ref=o_refs[i].at[me], send_sem=send.at[i, k],
                                                  recv_sem=recv.at[i, k], device_id=pid, device_id_type=MESH)
                cp.start()
                sends.append(cp)
        for k, (pid, pidx) in enumerate(peers):
            for i in range(n):
                pltpu.make_async_remote_copy(src_ref=x_refs[i], dst_ref=o_refs[i].at[pidx], send_sem=send.at[i, k],
                                             recv_sem=recv.at[i, k], device_id=pid, device_id_type=MESH).wait_recv()
        for cp in sends:
            cp.wait_send()
        for cp in local:
            cp.wait()

    return pl.pallas_call(
        body, name=name, in_specs=[ANY] * n, out_specs=[ANY] * n,
        out_shape=[SDS((N_DEV,) + x.shape, x.dtype) for x in xs],
        scratch_shapes=[pltpu.SemaphoreType.DMA((n, N_DEV - 1)), pltpu.SemaphoreType.DMA((n, N_DEV - 1)),
                        pltpu.SemaphoreType.DMA((n,))])(*xs)


HBM_SPEC = pl.BlockSpec(memory_space=pltpu.HBM)
SEM_SPEC = pl.BlockSpec(memory_space=pltpu.SEMAPHORE)
DATAFLOW = pltpu.SideEffectType.DATAFLOW_SIDE_EFFECTING


def _my_index():
    return 4 * lax.axis_index("x") + 2 * lax.axis_index("y") + lax.axis_index("c")


def _hbm(a):
    return pltpu.with_memory_space_constraint(a, pltpu.HBM)


NP = N_DEV - 1


def _push_copy(x_ref, land_ref, send, recv, pid, src_idx, dst_idx, scatter):
    src = x_ref.at[src_idx] if scatter else x_ref
    return pltpu.make_async_remote_copy(src_ref=src, dst_ref=land_ref.at[dst_idx], send_sem=send, recv_sem=recv,
                                        device_id=pid, device_id_type=MESH)


def push_start(name, xs, me, scatter):
    n = len(xs)
    lands = []
    for a in xs:
        own = lax.dynamic_index_in_dim(a, me, 0, keepdims=True) if scatter else a[None]
        z = lax.empty((N_DEV,) + own.shape[1:], a.dtype)
        lands.append(lax.dynamic_update_slice(z, own, (me,) + (0,) * (own.ndim - 1)))
    ns = 2 * NP * n

    def body(*refs):
        x_refs, land_refs = refs[:n], refs[n:2 * n]
        sems = refs[2 * n:2 * n + ns]
        token = refs[-1]
        x, y, c = _me()
        mine = 4 * x + 2 * y + c
        for i in range(n):
            for k, (pid, pidx) in enumerate(_peers()):
                s = 2 * (NP * i + k)
                _push_copy(x_refs[i], land_refs[i], sems[s], sems[s + 1], pid, pidx, mine, scatter).start()
        token[...] = jnp.zeros_like(token)

    out_shape = ([pltpu.SemaphoreType.DMA(())] * ns + [pltpu.HBM(a.shape, a.dtype) for a in xs + lands]
                 + [SDS((8, 128), F32)])
    res = pl.pallas_call(
        body, name=name, out_shape=out_shape, in_specs=[HBM_SPEC] * (2 * n),
        out_specs=[SEM_SPEC] * ns + [HBM_SPEC] * (2 * n) + [pl.BlockSpec(memory_space=pltpu.VMEM)],
        input_output_aliases={i: ns + i for i in range(2 * n)},
        compiler_params=pltpu.CompilerParams(has_side_effects=DATAFLOW))(*[_hbm(a) for a in xs + lands])
    sems, thru, token = res[:ns], res[ns:-1], res[-1]
    return [dict(x=thru[i], land=thru[n + i], sems=list(sems[2 * NP * i:2 * NP * (i + 1)]), token=token)
            for i in range(n)]


def push_wait(name, hd, after, scatter):
    def body(*refs):
        x_ref, land_ref = refs[0], refs[1]
        sems = refs[2:2 + 2 * NP]
        for k, (pid, pidx) in enumerate(_peers()):
            cp = _push_copy(x_ref, land_ref, sems[2 * k], sems[2 * k + 1], pid, pidx, pidx, scatter)
            cp.wait_send()
            cp.wait_recv()

    x, land = hd["x"], hd["land"]
    res = pl.pallas_call(
        body, name=name, out_shape=[pltpu.HBM(x.shape, x.dtype), pltpu.HBM(land.shape, land.dtype)],
        in_specs=[HBM_SPEC, HBM_SPEC] + [SEM_SPEC] * (2 * NP) + [ANY], out_specs=[HBM_SPEC, HBM_SPEC],
        input_output_aliases={0: 0, 1: 1},
        compiler_params=pltpu.CompilerParams(has_side_effects=DATAFLOW))(x, land, *hd["sems"], after)
    return res[1]


WEIGHTS = ['meta_tokens', 'norm_mix_g', 'norm_ffn_g', 'mla_w_down', 'mla_cq_norm_g', 'mla_ckv_norm_g', 'mla_w_uq',
           'mla_w_ukv', 'mla_q_head_g', 'mla_k_head_g', 'mla_w_o', 'hgrn_w_in', 'hgrn_lb_logits', 'hgrn_o_norm_g',
           'hgrn_w_o', 's5_lam_re', 's5_lam_im', 's5_log_dt', 's5_b_re', 's5_b_im', 's5_c_re', 's5_c_im', 's5_d',
           's5_w_glu', 'ret_w_in', 'ret_gn_g', 'ret_w_o', 'ffn_w_up', 'ffn_conv_w', 'ffn_conv_b', 'ffn_w_down']
BIG = ['mla_w_down', 'mla_w_uq', 'mla_w_ukv', 'mla_w_o', 'hgrn_w_in', 'hgrn_w_o', 's5_w_glu', 'ret_w_in', 'ret_w_o',
       'ffn_w_up', 'ffn_w_down']
SMALL_SH = ['meta_tokens', 's5_d', 'ret_gn_g', 'ffn_conv_w']
SMALL_REP = ['norm_mix_g', 'norm_ffn_g', 'mla_cq_norm_g', 'mla_ckv_norm_g', 'mla_q_head_g', 'mla_k_head_g',
             'hgrn_lb_logits', 'hgrn_o_norm_g', 's5_lam_re', 's5_lam_im', 's5_log_dt', 's5_b_re', 's5_b_im',
             's5_c_re', 's5_c_im', 'ffn_conv_b']
LANE = 128


def _flat(arrs, mult):
    v = jnp.concatenate([a.reshape(-1) for a in arrs])
    pad = (-v.shape[0]) % mult
    return jnp.pad(v, (0, pad)).reshape(-1, LANE)


def _unflat(flat2d, like):
    v = flat2d.reshape(-1)
    out, o = [], 0
    for a in like:
        out.append(v[o:o + a.size].reshape(a.shape))
        o += a.size
    return out


def _lb_of(logits):
    cum = jnp.cumsum(jax.nn.softmax(logits, axis=0), axis=0)
    return (cum - cum[0:1])[1:2]


def _cols_to_blocks(g):
    k, n = g.shape
    return g.reshape(k, N_DEV, n // N_DEV).transpose(1, 0, 2)


def _blocks_to_cols(wb):
    nb, k, n = wb.shape
    return wb.transpose(1, 0, 2).reshape(k, nb * n)


SUBS = ['mla', 'ffn0', 'hgrn', 'ffn1', 's5', 'ffn2', 'ret', 'ffn3']
GROUPS = [[('mla_w_down', 0), ('mla_w_uq', 0), ('mla_w_ukv', 0), ('mla_w_o', 0)],
          [('ffn_w_up', 0), ('ffn_w_down', 0)],
          [('hgrn_w_in', 0), ('hgrn_w_o', 0)],
          [('ffn_w_up', 1), ('ffn_w_down', 1)],
          [('s5_w_glu', 0)],
          [('ffn_w_up', 2), ('ffn_w_down', 2)],
          [('ret_w_in', 0), ('ret_w_o', 0)],
          [('ffn_w_up', 3), ('ffn_w_down', 3)]]


def _pack(parts, mult):
    v = jnp.concatenate(parts)
    return jnp.pad(v, (0, (-v.shape[0]) % mult)).reshape(-1, LANE)


def _pack8(parts, mult):
    v = jnp.concatenate(parts, axis=1)
    return jnp.pad(v, ((0, 0), (0, (-v.shape[1]) % mult))).reshape(N_DEV, -1, LANE)


def _sub_weights(k, got, rep, tabs, lp):
    ngm, ngf = rep['norm_mix_g'], rep['norm_ffn_g']
    if k == 0:
        return dict(ng=ngm[0:1], wdown=got[0].reshape(D, -1), gcq=rep['mla_cq_norm_g'], gckv=rep['mla_ckv_norm_g'],
                    wuq=got[1], wukv=got[2], gq=rep['mla_q_head_g'], gk=rep['mla_k_head_g'], wo=got[3].reshape(D, D),
                    mc=mla_consts(lp))
    if k == 2:
        return dict(ng=ngm[1:2], win=got[0], lb=tabs['lb'], go=rep['hgrn_o_norm_g'], wo=got[1].reshape(D, D))
    if k == 4:
        return dict(ng=ngm[2:3], tb=tabs['tb'], dsk=tabs['s5_d'], wglu=_blocks_to_cols(got[0]))
    if k == 6:
        return dict(ng=ngm[3:4], win=_blocks_to_cols(got[0]), gn=tabs['ret_gn_g'], wo=got[1].reshape(2 * D, D),
                    rc=ret_consts(lp))
    i = k // 2
    return dict(ng=ngf[i:i + 1], up=got[0], cw=tabs['conv_w'][:, i].reshape(2, 4, 3, 1, FFN_B),
                cb=rep['ffn_conv_b'][i].reshape(2, 4, 1, FFN_B), down=got[1].reshape(4, FFN_B, D))


def _sub_grad_blocks(k, g):
    if k == 0:
        parts = [g['wdown'], g['wuq'], g['wukv'], g['wo']]
    elif k == 2:
        parts = [g['win'], g['wo']]
    elif k == 4:
        parts = [_cols_to_blocks(g['wglu'])]
    elif k == 6:
        parts = [_cols_to_blocks(g['win']), g['wo']]
    else:
        parts = [g['up'], g['down']]
    return [p.reshape(N_DEV, -1) for p in parts]


_FWD = [mla_fwd, None, hgrn_fwd, None, s5_fwd, None, ret_fwd, None]
_BWD = [mla_bwd, None, hgrn_bwd, None, s5_bwd, None, ret_bwd, None]


def _step(args):
    w = {n: args[n] for n in WEIGHTS}
    x2, tgt = args['x'][0], args['loss_target'][0]

    lp = x2.shape[0] + OFF
    me = _my_index()
    mask = _rowmask(lp)
    rep = {n: w[n] for n in SMALL_REP}
    n_small = sum(w[n].size for n in SMALL_SH)

    packs = []
    for gi, grp in enumerate(GROUPS):
        parts = [w[n][l].astype(BF).reshape(-1) for n, l in grp]
        if gi == 0:
            small = jnp.concatenate([w[n].reshape(-1) for n in SMALL_SH])
            parts.append(lax.bitcast_convert_type(small, BF).reshape(-1))
        packs.append(_pack(parts, 16 * LANE))
    gh = push_start("gather_start", packs, me, scatter=False)

    def fetch(gi, after):
        land = push_wait("gather_wait_" + SUBS[gi], gh[gi], after, scatter=False).reshape(N_DEV, -1)
        out, o = [], 0
        for n, _ in GROUPS[gi]:
            shp = w[n].shape[1:]
            sz = math.prod(shp)
            out.append(land[:, o:o + sz].reshape((N_DEV,) + shp))
            o += sz
        if gi == 0:
            out.append(lax.bitcast_convert_type(land[:, o:o + 2 * n_small].reshape(N_DEV, n_small, 2), F32))
        return out

    got = fetch(0, x2)
    sm, o, smp = got[-1], 0, {}
    for n in SMALL_SH:
        smp[n] = sm[:, o:o + w[n].size].reshape((N_DEV,) + w[n].shape)
        o += w[n].size
    meta = smp['meta_tokens'].transpose(1, 0, 2).reshape(N_META, D)
    lb, lb_vjp = jax.vjp(_lb_of, rep['hgrn_lb_logits'])
    s5p = [rep[n][0] for n in ('s5_lam_re', 's5_lam_im', 's5_log_dt', 's5_b_re', 's5_b_im', 's5_c_re', 's5_c_im')]
    tb, tb_vjp = jax.vjp(s5_tables, *s5p)
    tabs = dict(lb=lb, tb=tb, s5_d=smp['s5_d'].reshape(1, D), ret_gn_g=smp['ret_gn_g'].reshape(1, 2 * D),
                conv_w=smp['ffn_conv_w'])
    h = jnp.concatenate([jnp.zeros((PAD, D), F32), meta, x2], axis=0)
    ws, saved = [], []
    for k in range(8):
        if k > 0:
            got = fetch(k, h)
        ws.append(_sub_weights(k, got, rep, tabs, lp))
        if k % 2:
            h, sv = ffn_fwd(k // 2, h, mask, ws[k])
        else:
            h, sv = _FWD[k](h, mask, ws[k])
        saved.append(sv)
    loss, dh = loss_head(h, tgt)

    gs, sh = [None] * 8, [None] * 8
    mk = mask
    for k in reversed(range(8)):
        if k % 2:
            dh, gs[k] = ffn_bwd(k // 2, dh, mk, ws[k], saved[k])
        else:
            dh, gs[k] = _BWD[k](dh, mk, ws[k], saved[k])
        blocks = _sub_grad_blocks(k, gs[k])
        if k == 0:
            dmeta = dh[PAD:OFF].reshape(N_META, N_DEV, D // N_DEV).transpose(1, 0, 2)
            dcw = jnp.stack([gs[2 * i + 1]['cw'].reshape(N_DEV, 3, FFN_B) for i in range(4)], axis=1)
            blocks += [t.reshape(N_DEV, -1) for t in (dmeta, gs[4]['dsk'], gs[6]['gn'], dcw)]
        sh[k] = push_start("scatter_start_" + SUBS[k], [_pack8(blocks, 8 * LANE)], me, scatter=True)[0]
        mk = mask + sh[k]["token"][0, 0]
    grad_x = dh[OFF:]

    ds5 = tb_vjp(gs[4]['tb'])
    g_rep = {
        'norm_mix_g': jnp.concatenate([gs[k]['ng'] for k in (0, 2, 4, 6)], axis=0),
        'norm_ffn_g': jnp.concatenate([gs[k]['ng'] for k in (1, 3, 5, 7)], axis=0),
        'mla_cq_norm_g': gs[0]['gcq'], 'mla_ckv_norm_g': gs[0]['gckv'], 'mla_q_head_g': gs[0]['gq'],
        'mla_k_head_g': gs[0]['gk'], 'hgrn_lb_logits': lb_vjp(gs[2]['lb'])[0], 'hgrn_o_norm_g': gs[2]['go'],
        's5_lam_re': ds5[0], 's5_lam_im': ds5[1], 's5_log_dt': ds5[2], 's5_b_re': ds5[3], 's5_b_im': ds5[4],
        's5_c_re': ds5[5], 's5_c_im': ds5[6],
        'ffn_conv_b': jnp.stack([gs[k]['cb'].reshape(-1) for k in (1, 3, 5, 7)], axis=0),
    }
    grep = _flat([g_rep[n] for n in SMALL_REP], 8 * LANE)
    (rep_land,) = all_gather("gather_small_grads", [grep])

    res = {n: [[None] * w[n].shape[0] for _ in range(4)] for n in BIG}
    for k in reversed(range(8)):
        land = push_wait("scatter_wait_" + SUBS[k], sh[k], rep_land, scatter=True)
        like = [w[n][l] for n, l in GROUPS[k]] + ([w[n] for n in SMALL_SH] if k == 0 else [])

        def packed(prefix):
            return _pack([args[prefix + n][l].reshape(-1) for n, l in GROUPS[k]]
                         + ([args[prefix + n].reshape(-1) for n in SMALL_SH] if k == 0 else []), 8 * LANE)

        out = adamw("adam_" + SUBS[k], land, packed(''), packed('m_'), packed('v_'))
        for q in range(4):
            parts = _unflat(out[q], like)
            for (n, l), p in zip(GROUPS[k], parts):
                res[n][q][l] = p
            if k == 0:
                for n, p in zip(SMALL_SH, parts[len(GROUPS[k]):]):
                    res[n] = res.get(n, [None] * 4)
                    res[n][q] = p
    for n in BIG:
        res[n] = [jnp.stack(res[n][q], axis=0) for q in range(4)]
    rep_like = [w[n] for n in SMALL_REP]
    out = adamw("adam_small_replicated", rep_land, _flat(rep_like, 8 * LANE),
                _flat([args['m_' + n] for n in SMALL_REP], 8 * LANE), _flat([args['v_' + n] for n in SMALL_REP], 8 * LANE))
    for n, parts in zip(SMALL_REP, zip(*[_unflat(t, rep_like) for t in out])):
        res[n] = list(parts)

    total = lax.psum(loss[0, 0], ("x", "y", "c"))
    outs = [total, grad_x[None]]
    for k in range(4):
        outs += [res[n][k] for n in WEIGHTS]
    return tuple(outs)


def kernel(x, meta_tokens, norm_mix_g, norm_ffn_g, mla_w_down, mla_cq_norm_g, mla_ckv_norm_g, mla_w_uq, mla_w_ukv, mla_q_head_g, mla_k_head_g, mla_w_o, hgrn_w_in, hgrn_lb_logits, hgrn_o_norm_g, hgrn_w_o, s5_lam_re, s5_lam_im, s5_log_dt, s5_b_re, s5_b_im, s5_c_re, s5_c_im, s5_d, s5_w_glu, ret_w_in, ret_gn_g, ret_w_o, ffn_w_up, ffn_conv_w, ffn_conv_b, ffn_w_down, loss_target, m_meta_tokens, m_norm_mix_g, m_norm_ffn_g, m_mla_w_down, m_mla_cq_norm_g, m_mla_ckv_norm_g, m_mla_w_uq, m_mla_w_ukv, m_mla_q_head_g, m_mla_k_head_g, m_mla_w_o, m_hgrn_w_in, m_hgrn_lb_logits, m_hgrn_o_norm_g, m_hgrn_w_o, m_s5_lam_re, m_s5_lam_im, m_s5_log_dt, m_s5_b_re, m_s5_b_im, m_s5_c_re, m_s5_c_im, m_s5_d, m_s5_w_glu, m_ret_w_in, m_ret_gn_g, m_ret_w_o, m_ffn_w_up, m_ffn_conv_w, m_ffn_conv_b, m_ffn_w_down, v_meta_tokens, v_norm_mix_g, v_norm_ffn_g, v_mla_w_down, v_mla_cq_norm_g, v_mla_ckv_norm_g, v_mla_w_uq, v_mla_w_ukv, v_mla_q_head_g, v_mla_k_head_g, v_mla_w_o, v_hgrn_w_in, v_hgrn_lb_logits, v_hgrn_o_norm_g, v_hgrn_w_o, v_s5_lam_re, v_s5_lam_im, v_s5_log_dt, v_s5_b_re, v_s5_b_im, v_s5_c_re, v_s5_c_im, v_s5_d, v_s5_w_glu, v_ret_w_in, v_ret_gn_g, v_ret_w_o, v_ffn_w_up, v_ffn_conv_w, v_ffn_conv_b, v_ffn_w_down):
    return _step(dict(locals()))
```

```python
import functools
import math

import jax
import jax.numpy as jnp
from jax import lax
from jax.experimental import pallas as pl
from jax.experimental.pallas import tpu as pltpu

F32 = jnp.float32
BF = jnp.bfloat16
SDS = jax.ShapeDtypeStruct

N_DEV = 8
D = 1024
N_META = 16
PAD = 48
OFF = PAD + N_META
CH = 64
EPS = 1e-6
NEG = -1e30
ROPE_BASE = 10000.0

MLA_H, MLA_NOPE, MLA_ROPE, MLA_V = 8, 128, 64, 128
MLA_QK = MLA_NOPE + MLA_ROPE
MLA_QL, MLA_KVL = 384, 256
HG_H, HG_D, HG_C = 8, 128, 16
S5_G, S5_P, S5_K = 64, 64, 16
S5_SG = 8
RET_H, RET_DK, RET_DV = 4, 256, 512
FFN_F = 2816
FFN_B = 704

ADAM_LR, ADAM_B1, ADAM_B2, ADAM_EPS, ADAM_WD, ADAM_STEP = 0.001, 0.9, 0.999, 1e-08, 0.01, 10

VMEM_LIMIT = 56 * 1024 * 1024
ARB = "arbitrary"


def _cp(n):
    return pltpu.CompilerParams(dimension_semantics=(ARB,) * n, vmem_limit_bytes=VMEM_LIMIT)


def _bdot(a, b, ca, cb):
    return lax.dot_general(a.astype(BF), b.astype(BF), (((ca,), (cb,)), ((), ())), preferred_element_type=F32)


@jax.custom_vjp
def mm(a, b):
    return _bdot(a, b, 1, 0)


@jax.custom_vjp
def mm_nt(a, b):
    return _bdot(a, b, 1, 1)


@jax.custom_vjp
def mm_tn(a, b):
    return _bdot(a, b, 0, 0)


mm.defvjp(lambda a, b: (mm(a, b), (a, b)),
          lambda r, g: (mm_nt(g, r[1]).astype(r[0].dtype), mm_tn(r[0], g).astype(r[1].dtype)))
mm_nt.defvjp(lambda a, b: (mm_nt(a, b), (a, b)),
             lambda r, g: (mm(g, r[1]).astype(r[0].dtype), mm_tn(g, r[0]).astype(r[1].dtype)))
mm_tn.defvjp(lambda a, b: (mm_tn(a, b), (a, b)),
             lambda r, g: (mm_nt(r[1], g).astype(r[0].dtype), mm(r[0], g).astype(r[1].dtype)))


def _xdot(a, b, ca, cb):
    return lax.dot_general(a, b, (((ca,), (cb,)), ((), ())), preferred_element_type=F32,
                           precision=lax.Precision.HIGHEST)


@jax.custom_vjp
def cleft(t, x):
    return _xdot(t, x, 1, 0)


cleft.defvjp(lambda t, x: (cleft(t, x), t), lambda t, g: (jnp.zeros_like(t), _xdot(t, g, 0, 0)))


@jax.custom_vjp
def cright(x, r):
    return _xdot(x, r, 1, 0)


cright.defvjp(lambda x, r: (cright(x, r), r), lambda r, g: (_xdot(g, r, 1, 1), jnp.zeros_like(r)))


def _shift_raw(x, s):
    n = x.shape[0]
    r = lax.broadcasted_iota(jnp.int32, x.shape, 0)
    y = pltpu.roll(x, s % n, 0)
    return jnp.where((r >= s) & (r < n + s), y, 0.0)


@functools.partial(jax.custom_vjp, nondiff_argnums=(1,))
def shift_rows(x, s):
    return _shift_raw(x, s)


shift_rows.defvjp(lambda x, s: (_shift_raw(x, s), None), lambda s, _, g: (_shift_raw(g, -s),))


def _rms(x, g):
    return x * lax.rsqrt(jnp.mean(x * x, axis=-1, keepdims=True) + EPS) * g


def _silu(x):
    return x * jax.nn.sigmoid(x)


def _mm_call(name, a, b, *, grid, a_spec, b_spec, o_shape, o_spec, dims, acc_shape, res=None, res_spec=None,
             mask_tm=None):
    nk = grid[2]

    def body(*refs):
        if res is None:
            a_ref, b_ref, o_ref, acc = refs
        else:
            a_ref, b_ref, r_ref, o_ref, acc = refs
        k = pl.program_id(2)

        @pl.when(k == 0)
        def _():
            acc[...] = jnp.zeros_like(acc)

        acc[...] += lax.dot_general(a_ref[...].astype(BF), b_ref[...].astype(BF), dims, preferred_element_type=F32)

        @pl.when(k == nk - 1)
        def _():
            v = acc[...]
            if res is not None:
                v = v + r_ref[...].astype(F32)
                rows = pl.program_id(0) * mask_tm + lax.broadcasted_iota(jnp.int32, v.shape, 0)
                v = jnp.where(rows >= PAD, v, 0.0)
            o_ref[...] = v.astype(o_ref.dtype)

    ins = [a, b] + ([res] if res is not None else [])
    specs = [a_spec, b_spec] + ([res_spec] if res is not None else [])
    return pl.pallas_call(body, name=name, grid=grid, in_specs=specs, out_specs=o_spec, out_shape=o_shape,
                          scratch_shapes=[pltpu.VMEM(acc_shape, F32)], compiler_params=_cp(3))(*ins)


NN = (((1,), (0,)), ((), ()))
NT = (((1,), (1,)), ((), ()))
TN = (((0,), (0,)), ((), ()))


def _row_tile(lp):
    for t in (832, 640, 320, 64):
        if lp % t == 0:
            return t
    raise ValueError(lp)


def _col_tile(n):
    for t in (1024, 768, 512, 384, 256, 128):
        if n % t == 0:
            return t
    return n


def lin(name, a, w, out_dtype=F32, res=None):
    m, k = a.shape
    n = w.shape[1]
    tm, tn, tc = _row_tile(m), _col_tile(n), _col_tile(k)
    return _mm_call(name, a, w, grid=(m // tm, n // tn, k // tc),
                    a_spec=pl.BlockSpec((tm, tc), lambda i, j, kk: (i, kk)),
                    b_spec=pl.BlockSpec((tc, tn), lambda i, j, kk: (kk, j)),
                    o_shape=SDS((m, n), out_dtype), o_spec=pl.BlockSpec((tm, tn), lambda i, j, kk: (i, j)),
                    dims=NN, acc_shape=(tm, tn), res=res,
                    res_spec=pl.BlockSpec((tm, tn), lambda i, j, kk: (i, j)), mask_tm=tm)


def lin_bo(name, a, wb, out_dtype=F32):
    m, k = a.shape
    nb, _, n = wb.shape
    tm = _row_tile(m)
    return _mm_call(name, a, wb, grid=(m // tm, nb, 1),
                    a_spec=pl.BlockSpec((tm, k), lambda i, j, kk: (i, 0)),
                    b_spec=pl.BlockSpec((None, k, n), lambda i, j, kk: (j, 0, 0)),
                    o_shape=SDS((nb, m, n), out_dtype), o_spec=pl.BlockSpec((None, tm, n), lambda i, j, kk: (j, i, 0)),
                    dims=NN, acc_shape=(tm, n))


def lin_bi(name, ab, wb, out_dtype=F32, res=None):
    nb, m, k = ab.shape
    n = wb.shape[2]
    tm, tn = _row_tile(m), _col_tile(n)
    return _mm_call(name, ab, wb, grid=(m // tm, n // tn, nb),
                    a_spec=pl.BlockSpec((None, tm, k), lambda i, j, kk: (kk, i, 0)),
                    b_spec=pl.BlockSpec((None, k, tn), lambda i, j, kk: (kk, 0, j)),
                    o_shape=SDS((m, n), out_dtype), o_spec=pl.BlockSpec((tm, tn), lambda i, j, kk: (i, j)),
                    dims=NN, acc_shape=(tm, tn), res=res,
                    res_spec=pl.BlockSpec((tm, tn), lambda i, j, kk: (i, j)), mask_tm=tm)


def lin_t(name, g, w, out_dtype=F32):
    m, n = g.shape
    k = w.shape[0]
    tm, tk, tc = _row_tile(m), _col_tile(k), _col_tile(n)
    return _mm_call(name, g, w, grid=(m // tm, k // tk, n // tc),
                    a_spec=pl.BlockSpec((tm, tc), lambda i, j, kk: (i, kk)),
                    b_spec=pl.BlockSpec((tk, tc), lambda i, j, kk: (j, kk)),
                    o_shape=SDS((m, k), out_dtype), o_spec=pl.BlockSpec((tm, tk), lambda i, j, kk: (i, j)),
                    dims=NT, acc_shape=(tm, tk))


def lin_t_bi(name, gb, wb, out_dtype=F32):
    nb, m, n = gb.shape
    k = wb.shape[1]
    tm, tk = _row_tile(m), _col_tile(k)
    return _mm_call(name, gb, wb, grid=(m // tm, k // tk, nb),
                    a_spec=pl.BlockSpec((None, tm, n), lambda i, j, kk: (kk, i, 0)),
                    b_spec=pl.BlockSpec((None, tk, n), lambda i, j, kk: (kk, j, 0)),
                    o_shape=SDS((m, k), out_dtype), o_spec=pl.BlockSpec((tm, tk), lambda i, j, kk: (i, j)),
                    dims=NT, acc_shape=(tm, tk))


def lin_t_bo(name, g, wb, out_dtype=F32):
    m, n = g.shape
    nb, k, _ = wb.shape
    tm = _row_tile(m)
    return _mm_call(name, g, wb, grid=(m // tm, nb, 1),
                    a_spec=pl.BlockSpec((tm, n), lambda i, j, kk: (i, 0)),
                    b_spec=pl.BlockSpec((None, k, n), lambda i, j, kk: (j, 0, 0)),
                    o_shape=SDS((nb, m, k), out_dtype), o_spec=pl.BlockSpec((None, tm, k), lambda i, j, kk: (j, i, 0)),
                    dims=NT, acc_shape=(tm, k))


def wgrad(name, a, g):
    m, k = a.shape
    n = g.shape[1]
    tm, tn = _row_tile(m), _col_tile(n)
    return _mm_call(name, a, g, grid=(1, n // tn, m // tm),
                    a_spec=pl.BlockSpec((tm, k), lambda i, j, kk: (kk, 0)),
                    b_spec=pl.BlockSpec((tm, tn), lambda i, j, kk: (kk, j)),
                    o_shape=SDS((k, n), F32), o_spec=pl.BlockSpec((k, tn), lambda i, j, kk: (0, j)),
                    dims=TN, acc_shape=(k, tn))


def wgrad_bo(name, a, gb):
    m, k = a.shape
    nb, _, n = gb.shape
    tm = _row_tile(m)
    return _mm_call(name, a, gb, grid=(nb, 1, m // tm),
                    a_spec=pl.BlockSpec((tm, k), lambda i, j, kk: (kk, 0)),
                    b_spec=pl.BlockSpec((None, tm, n), lambda i, j, kk: (i, kk, 0)),
                    o_shape=SDS((nb, k, n), F32), o_spec=pl.BlockSpec((None, k, n), lambda i, j, kk: (i, 0, 0)),
                    dims=TN, acc_shape=(k, n))


def wgrad_bi(name, zb, g):
    nb, m, k = zb.shape
    n = g.shape[1]
    tm, tn = _row_tile(m), _col_tile(n)
    return _mm_call(name, zb, g, grid=(nb, n // tn, m // tm),
                    a_spec=pl.BlockSpec((None, tm, k), lambda i, j, kk: (i, kk, 0)),
                    b_spec=pl.BlockSpec((tm, tn), lambda i, j, kk: (kk, j)),
                    o_shape=SDS((nb, k, n), F32), o_spec=pl.BlockSpec((None, k, tn), lambda i, j, kk: (i, 0, j)),
                    dims=TN, acc_shape=(k, tn))


class Arg:
    def __init__(self, arr, block, imap, shared=False, acc=False):
        self.arr, self.block, self.imap = arr, block, imap
        self.shared = shared
        self.acc = acc

    @property
    def spec(self):
        return pl.BlockSpec(self.block, self.imap)

    def vshape(self):
        return tuple(b for b in self.block if b is not None)


def _rev(arg, nt):
    return pl.BlockSpec(arg.block, lambda o, t, _f=arg.imap: _f(o, nt - 1 - t))


def seq_fwd(name, fn, grid, params, consts, xs, outs, carries=()):
    no, nt = grid
    n_p, n_c, n_x, n_y, n_k = len(params), len(consts), len(xs), len(outs), len(carries)

    def body(*refs):
        p_refs = refs[:n_p]
        c_refs = refs[n_p:n_p + n_c]
        x_refs = refs[n_p + n_c:n_p + n_c + n_x]
        r = n_p + n_c + n_x
        y_refs = refs[r:r + n_y]
        s_refs = refs[r + n_y:r + n_y + n_k]
        k_refs = refs[r + n_y + n_k:]
        t = pl.program_id(1)

        if n_k:
            @pl.when(t == 0)
            def _():
                for k in k_refs:
                    k[...] = jnp.zeros_like(k)

        carry = tuple(k[...] for k in k_refs)
        for s, c in zip(s_refs, carry):
            s[...] = c
        new_carry, ys = fn(tuple(p[...] for p in p_refs), tuple(c[...] for c in c_refs), carry,
                           tuple(x[...] for x in x_refs))
        for k, c in zip(k_refs, new_carry):
            k[...] = c
        for y_ref, y in zip(y_refs, ys):
            y_ref[...] = y.astype(y_ref.dtype)

    out_shape = [SDS(s, d) for (s, d, _, _) in outs]
    out_specs = [pl.BlockSpec(b, im) for (_, _, b, im) in outs]
    for cs in carries:
        out_shape.append(SDS((no, nt) + cs, F32))
        out_specs.append(pl.BlockSpec((None, None) + cs, lambda o, t, _n=len(cs): (o, t) + (0,) * _n))
    res = pl.pallas_call(
        body, name=name, grid=grid, in_specs=[a.spec for a in list(params) + list(consts) + list(xs)],
        out_specs=out_specs, out_shape=out_shape, scratch_shapes=[pltpu.VMEM(cs, F32) for cs in carries],
        compiler_params=_cp(2))(*[a.arr for a in list(params) + list(consts) + list(xs)])
    return res[:n_y], res[n_y:]


def seq_bwd(name, fn, grid, params, consts, xs, dys, saved=(), carries=()):
    no, nt = grid
    n_p, n_c, n_x, n_y, n_k = len(params), len(consts), len(xs), len(dys), len(carries)

    def body(*refs):
        p_refs = refs[:n_p]
        c_refs = refs[n_p:n_p + n_c]
        x_refs = refs[n_p + n_c:n_p + n_c + n_x]
        r = n_p + n_c + n_x
        g_refs = refs[r:r + n_y]
        s_refs = refs[r + n_y:r + n_y + n_k]
        r = r + n_y + n_k
        dx_refs = refs[r:r + n_x]
        dp_refs = refs[r + n_x:r + n_x + n_p]
        k_refs = refs[r + n_x + n_p:]
        o = pl.program_id(0)
        t = pl.program_id(1)

        if n_k:
            @pl.when(t == 0)
            def _():
                for k in k_refs:
                    k[...] = jnp.zeros_like(k)

        for a, dp in zip(params, dp_refs):
            @pl.when((t == 0) & (o == 0) if a.shared else (t == 0))
            def _(dp=dp):
                dp[...] = jnp.zeros_like(dp)

        for a, dx in zip(xs, dx_refs):
            if a.acc:
                @pl.when(t == 0)
                def _(dx=dx):
                    dx[...] = jnp.zeros_like(dx)

        consts_v = tuple(c[...] for c in c_refs)

        def f(pv, cv, xv):
            return fn(pv, consts_v, cv, xv)

        pv = tuple(p[...] for p in p_refs)
        cv = tuple(s[...] for s in s_refs)
        xv = tuple(x[...] for x in x_refs)
        (new_carry, ys), vjp = jax.vjp(f, pv, cv, xv)
        cot = (tuple(k[...] for k in k_refs), tuple(g[...].astype(y.dtype) for g, y in zip(g_refs, ys)))
        dpv, dcv, dxv = vjp(cot)
        for k, c in zip(k_refs, dcv):
            k[...] = c
        for dp, v in zip(dp_refs, dpv):
            dp[...] += v
        for a, dx, v in zip(xs, dx_refs, dxv):
            if a.acc:
                dx[...] += v
            else:
                dx[...] = v.astype(dx.dtype)

    in_specs = ([_rev(a, nt) for a in list(params) + list(consts) + list(xs) + list(dys)]
                + [pl.BlockSpec((None, None) + cs, lambda o, t, _n=len(cs): (o, nt - 1 - t) + (0,) * _n) for cs in carries])
    out_shape = [SDS(a.arr.shape, F32) for a in xs] + [SDS(a.arr.shape, F32) for a in params]
    out_specs = [_rev(a, nt) for a in list(xs) + list(params)]
    res = pl.pallas_call(
        body, name=name, grid=grid, in_specs=in_specs, out_specs=out_specs, out_shape=out_shape,
        scratch_shapes=[pltpu.VMEM(cs, F32) for cs in carries], compiler_params=_cp(2))(
            *[a.arr for a in list(params) + list(consts) + list(xs) + list(dys)], *saved)
    return res[:n_x], res[n_x:]


def _rowmask(lp):
    return (jnp.arange(lp) >= PAD).astype(F32)[:, None]


def _norm_fn(p, c, k, x):
    return (), (_rms(x[0] * c[0], p[0]),)


def _norm_b_fn(p, c, k, x):
    h = x[0] * c[0]
    return (), (_rms(h, p[0]), h)


def norm_fwd(name, h, g, mask, out_dtype=BF):
    lp, d = h.shape
    tr = _row_tile(lp)
    row = lambda o, t: (t, 0)
    (a,), _ = seq_fwd(name, _norm_fn, (1, lp // tr), [Arg(g, (1, d), lambda o, t: (0, 0), shared=True)],
                      [Arg(mask, (tr, 1), row)], [Arg(h, (tr, d), row)], [((lp, d), out_dtype, (tr, d), row)])
    return a


def norm_bwd(name, h, g, mask, da, dskip):
    lp, d = h.shape
    tr = _row_tile(lp)
    row = lambda o, t: (t, 0)
    (dh,), (dg,) = seq_bwd(name, _norm_b_fn, (1, lp // tr), [Arg(g, (1, d), lambda o, t: (0, 0), shared=True)],
                           [Arg(mask, (tr, 1), row)], [Arg(h, (tr, d), row)],
                           [Arg(da, (tr, d), row), Arg(dskip, (tr, d), row)])
    return dh, dg


def _ffn_tile(lp):
    return 320 if (lp % 320 == 0 and lp > 320) else 64


def _conv_rows(ext, w, b, n):
    u2 = ext[8:8 + n]
    u1 = pltpu.roll(ext, 1, 0)[8:8 + n]
    u0 = pltpu.roll(ext, 2, 0)[8:8 + n]
    return w[2] * u2 + w[1] * u1 + w[0] * u0 + b, (u0, u1, u2)


def ffn_core_fwd(name, u, cw, cb):
    _, nj, lp, fb = u.shape
    tr = _ffn_tile(lp)
    nt = lp // tr

    def body(u_ref, up_ref, w_ref, b_ref, z_ref):
        i = pl.program_id(1)
        prev = jnp.where(i > 0, up_ref[...], 0.0)
        cs = []
        for s in range(2):
            ext = jnp.concatenate([prev[s], u_ref[s]], axis=0)
            c, _ = _conv_rows(ext, w_ref[s], b_ref[s], tr)
            cs.append(c)
        z_ref[...] = (_silu(cs[0]) * cs[1]).astype(z_ref.dtype)

    return pl.pallas_call(
        body, name=name, grid=(nj, nt),
        in_specs=[pl.BlockSpec((2, None, tr, fb), lambda j, i: (0, j, i, 0)),
                  pl.BlockSpec((2, None, 8, fb), lambda j, i: (0, j, jnp.maximum(i * (tr // 8) - 1, 0), 0)),
                  pl.BlockSpec((2, None, 3, 1, fb), lambda j, i: (0, j, 0, 0, 0)),
                  pl.BlockSpec((2, None, 1, fb), lambda j, i: (0, j, 0, 0))],
        out_specs=pl.BlockSpec((None, tr, fb), lambda j, i: (j, i, 0)),
        out_shape=SDS((nj, lp, fb), BF), compiler_params=_cp(2))(u, u, cw, cb)


def ffn_core_bwd(name, u, dz, cw, cb):
    _, nj, lp, fb = u.shape
    tr = _ffn_tile(lp)
    nt = lp // tr
    nb8 = lp // 8

    def body(u_ref, up_ref, un_ref, dz_ref, dzn_ref, w_ref, b_ref, du_ref, dw_ref, db_ref):
        i = pl.program_id(1)

        @pl.when(i == 0)
        def _():
            dw_ref[...] = jnp.zeros_like(dw_ref)
            db_ref[...] = jnp.zeros_like(db_ref)

        prev = jnp.where(i > 0, up_ref[...], 0.0)
        nxt = jnp.where(i < nt - 1, un_ref[...], 0.0)
        dz_e = jnp.concatenate([dz_ref[...], jnp.where(i < nt - 1, dzn_ref[...], 0.0)], axis=0)
        n = tr + 8
        cs, taps = [], []
        for s in range(2):
            ext = jnp.concatenate([prev[s], u_ref[s], nxt[s]], axis=0)
            c, tp = _conv_rows(ext, w_ref[s], b_ref[s], n)
            cs.append(c)
            taps.append(tp)
        sg = jax.nn.sigmoid(cs[0])
        dcs = [dz_e * cs[1] * sg * (1.0 + cs[0] * (1.0 - sg)), dz_e * cs[0] * sg]
        for s in range(2):
            dc = dcs[s]
            w = w_ref[s]
            d1 = pltpu.roll(dc, n - 1, 0)[:tr]
            d2 = pltpu.roll(dc, n - 2, 0)[:tr]
            dcm = dc[:tr]
            du_ref[s] = w[2] * dcm + w[1] * d1 + w[0] * d2
            for k in range(3):
                dw_ref[s, k] += jnp.sum(dcm * taps[s][k][:tr], axis=0, keepdims=True)
            db_ref[s] += jnp.sum(dcm, axis=0, keepdims=True)

    return pl.pallas_call(
        body, name=name, grid=(nj, nt),
        in_specs=[pl.BlockSpec((2, None, tr, fb), lambda j, i: (0, j, i, 0)),
                  pl.BlockSpec((2, None, 8, fb), lambda j, i: (0, j, jnp.maximum(i * (tr // 8) - 1, 0), 0)),
                  pl.BlockSpec((2, None, 8, fb), lambda j, i: (0, j, jnp.minimum((i + 1) * (tr // 8), nb8 - 1), 0)),
                  pl.BlockSpec((None, tr, fb), lambda j, i: (j, i, 0)),
                  pl.BlockSpec((None, 8, fb), lambda j, i: (j, jnp.minimum((i + 1) * (tr // 8), nb8 - 1), 0)),
                  pl.BlockSpec((2, None, 3, 1, fb), lambda j, i: (0, j, 0, 0, 0)),
                  pl.BlockSpec((2, None, 1, fb), lambda j, i: (0, j, 0, 0))],
        out_specs=[pl.BlockSpec((2, None, tr, fb), lambda j, i: (0, j, i, 0)),
                   pl.BlockSpec((2, None, 3, 1, fb), lambda j, i: (0, j, 0, 0, 0)),
                   pl.BlockSpec((2, None, 1, fb), lambda j, i: (0, j, 0, 0))],
        out_shape=[SDS(u.shape, F32), SDS(cw.shape, F32), SDS(cb.shape, F32)],
        compiler_params=_cp(2))(u, u, u, dz, dz, cw, cb)


def ffn_fwd(i, h, mask, w):
    a = norm_fwd(f"ffn{i}_norm", h, w["ng"], mask)
    u = lin_bo(f"ffn{i}_up", a, w["up"])
    lp = h.shape[0]
    u = u.reshape(2, 4, lp, FFN_B)
    z = ffn_core_fwd(f"ffn{i}_core", u, w["cw"], w["cb"])
    h2 = lin_bi(f"ffn{i}_down", z, w["down"], res=h)
    return h2, (h, a, u, z)


def ffn_bwd(i, dh2, mask, w, saved):
    h, a, u, z = saved
    lp = h.shape[0]
    g = {}
    g["down"] = wgrad_bi(f"ffn{i}_dwdown", z, dh2)
    dz = lin_t_bo(f"ffn{i}_dz", dh2, w["down"])
    du, g["cw"], g["cb"] = ffn_core_bwd(f"ffn{i}_core_b", u, dz, w["cw"], w["cb"])
    du = du.reshape(8, lp, FFN_B)
    g["up"] = wgrad_bo(f"ffn{i}_dwup", a, du)
    da = lin_t_bi(f"ffn{i}_da", du, w["up"])
    dh, g["ng"] = norm_bwd(f"ffn{i}_norm_b", h, w["ng"], mask, da, dh2)
    return dh, g


def _hgrn_fn(p, c, k, x):
    lb, go = p
    q, f, iv, g = x
    (st,) = k
    qs = _silu(q)
    forget = lb + (1.0 - lb) * jax.nn.sigmoid(f)
    logf = jnp.log(forget)
    kk = 1.0 - forget
    r = lax.broadcasted_iota(jnp.int32, (HG_C, HG_C), 0)
    cc = lax.broadcasted_iota(jnp.int32, (HG_C, HG_C), 1)
    tri = (r >= cc).astype(F32)
    outs = []
    for s in range(CH // HG_C):
        sl = slice(HG_C * s, HG_C * (s + 1))
        lf = logf[sl]
        gc = cleft(tri, lf)
        gl = jnp.sum(lf, axis=0, keepdims=True)
        qd = qs[sl] * jnp.exp(gc)
        ki = kk[sl] * jnp.exp(-gc)
        kt = kk[sl] * jnp.exp(gl - gc)
        attn = jnp.where(r >= cc, mm_nt(qd, ki), 0.0)
        outs.append(mm(attn, iv[sl]) + mm_nt(qd, st))
        st = st * jnp.exp(gl) + mm_tn(iv[sl], kt)
    o = jnp.concatenate(outs, axis=0)
    return (st,), (_rms(o, go) * _silu(g),)


def _hgrn_args(u4, lb, go):
    lp = u4[0].shape[1]
    xs = [Arg(t, (None, CH, HG_D), lambda o, t: (o // 4, t, o % 4)) for t in u4]
    ps = [Arg(lb, (1, HG_D), lambda o, t: (0, o)), Arg(go, (1, HG_D), lambda o, t: (0, 0), shared=True)]
    return (HG_H, lp // CH), ps, xs


def hgrn_fwd(h, mask, w):
    lp = h.shape[0]
    a = norm_fwd("hgrn_norm", h, w["ng"], mask)
    u = lin_bo("hgrn_in", a, w["win"])
    u4 = [u[2 * s:2 * s + 2] for s in range(4)]
    grid, ps, xs = _hgrn_args(u4, w["lb"], w["go"])
    (z,), (st,) = seq_fwd("hgrn_core", _hgrn_fn, grid, ps, [], xs,
                          [((lp, D), BF, (CH, HG_D), lambda o, t: (t, o))], carries=[(HG_D, HG_D)])
    h2 = lin("hgrn_out", z, w["wo"], res=h)
    return h2, (h, a, u4, z, st)


def hgrn_bwd(dh2, mask, w, saved):
    h, a, u4, z, st = saved
    lp = h.shape[0]
    g = {}
    g["wo"] = wgrad("hgrn_dwo", z, dh2)
    dz = lin_t("hgrn_dz", dh2, w["wo"])
    grid, ps, xs = _hgrn_args(u4, w["lb"], w["go"])
    du4, (g["lb"], g["go"]) = seq_bwd("hgrn_core_b", _hgrn_fn, grid, ps, [], xs,
                                      [Arg(dz, (CH, HG_D), lambda o, t: (t, o))], saved=[st],
                                      carries=[(HG_D, HG_D)])
    du = jnp.concatenate(du4, axis=0)
    g["win"] = wgrad_bo("hgrn_dwin", a, du)
    da = lin_t_bi("hgrn_da", du, w["win"])
    dh, g["ng"] = norm_bwd("hgrn_norm_b", h, w["ng"], mask, da, dh2)
    return dh, g


S5_W = S5_SG * S5_P


def s5_tables(lam_re, lam_im, log_dt, b_re, b_im, c_re, c_im):
    dt = jnp.exp(log_dt)[:, None]
    mag = jnp.exp(lam_re * dt)
    abar_re = mag * jnp.cos(lam_im * dt)
    abar_im = mag * jnp.sin(lam_im * dt)
    den = lam_re * lam_re + lam_im * lam_im
    zoh_re = ((abar_re - 1.0) * lam_re + abar_im * lam_im) / den
    zoh_im = (abar_im * lam_re - (abar_re - 1.0) * lam_im) / den
    bbar_re = zoh_re[..., None] * b_re - zoh_im[..., None] * b_im
    bbar_im = zoh_re[..., None] * b_im + zoh_im[..., None] * b_re
    eye = jnp.eye(S5_SG, dtype=F32)

    def blockdiag_in(b):
        t = b.reshape(N_DEV, S5_SG, S5_P, S5_K).transpose(0, 1, 3, 2)
        return jnp.einsum("jakp,ab->jakbp", t, eye).reshape(N_DEV, S5_SG * S5_K, S5_W)

    def blockdiag_out(c):
        t = c.reshape(N_DEV, S5_SG, S5_K, S5_P).transpose(0, 1, 3, 2)
        return jnp.einsum("japk,ab->japbk", t, eye).reshape(N_DEV, S5_W, S5_SG * S5_K)

    wb = jnp.concatenate([blockdiag_in(bbar_re), blockdiag_in(bbar_im)], axis=2)
    wc = jnp.concatenate([blockdiag_out(c_re), -blockdiag_out(c_im)], axis=1)

    def powers(n):
        steps = n[:, None, None] * dt[None]
        pm = jnp.exp(lam_re[None] * steps)
        pr = (pm * jnp.cos(lam_im[None] * steps)).reshape(-1, N_DEV, S5_W).transpose(1, 0, 2)
        pi = (pm * jnp.sin(lam_im[None] * steps)).reshape(-1, N_DEV, S5_W).transpose(1, 0, 2)
        return jnp.concatenate([pr, pi], axis=2)

    apow = powers(2.0 ** jnp.arange(6, dtype=F32))[:, :, None, :]
    ptab = powers(jnp.arange(CH, dtype=F32) + 1.0)
    return wb, wc, apow, ptab


def _cmul(ar, ai, xr, xi):
    return ar * xr - ai * xi, ar * xi + ai * xr


def _s5_fn(p, c, k, x):
    wb, wc, apow, ptab, dsk = p
    (a,) = x
    (x0,) = k
    bu = mm(a, wb)
    xr, xi = bu[:, :S5_W], bu[:, S5_W:]
    for s in range(6):
        asr, asi = apow[s][:, :S5_W], apow[s][:, S5_W:]
        dr, di = _cmul(asr, asi, shift_rows(xr, 1 << s), shift_rows(xi, 1 << s))
        xr, xi = xr + dr, xi + di
    dr, di = _cmul(ptab[:, :S5_W], ptab[:, S5_W:], x0[:, :S5_W], x0[:, S5_W:])
    xr, xi = xr + dr, xi + di
    xx = jnp.concatenate([xr, xi], axis=1)
    last = lax.broadcasted_iota(jnp.int32, xx.shape, 0) == CH - 1
    x0n = jnp.sum(jnp.where(last, xx, 0.0), axis=0, keepdims=True)
    y = mm(xx, wc)
    return (x0n,), (jax.nn.gelu(y + dsk * a),)


def _s5_args(a, tb, dsk):
    lp = a.shape[0]
    wb, wc, apow, ptab = tb
    ps = [Arg(wb, (None, 128, 2 * S5_W), lambda o, t: (o, 0, 0)), Arg(wc, (None, 2 * S5_W, 128), lambda o, t: (o, 0, 0)),
          Arg(apow, (None, 6, 1, 2 * S5_W), lambda o, t: (o, 0, 0, 0)), Arg(ptab, (None, CH, 2 * S5_W), lambda o, t: (o, 0, 0)),
          Arg(dsk, (1, 128), lambda o, t: (0, o))]
    xs = [Arg(a, (CH, 128), lambda o, t: (t, o))]
    return (N_DEV, lp // CH), ps, xs


def _glu_res_fn(p, c, k, x):
    h, vg = x
    return (), ((h + vg[:, :D] * jax.nn.sigmoid(vg[:, D:])) * c[0],)


def _glu_args(h, vg, mask):
    lp = h.shape[0]
    tr = _row_tile(lp)
    row = lambda o, t: (t, 0)
    return (1, lp // tr), [Arg(mask, (tr, 1), row)], [Arg(h, (tr, D), row), Arg(vg, (tr, 2 * D), row)], tr


def s5_fwd(h, mask, w):
    lp = h.shape[0]
    a = norm_fwd("s5_norm", h, w["ng"], mask, out_dtype=F32)
    grid, ps, xs = _s5_args(a, w["tb"], w["dsk"])
    (z,), (st,) = seq_fwd("s5_core", _s5_fn, grid, ps, [], xs,
                          [((lp, D), BF, (CH, 128), lambda o, t: (t, o))], carries=[(1, 2 * S5_W)])
    vg = lin("s5_glu", z, w["wglu"])
    grid2, cs, xs2, tr = _glu_args(h, vg, mask)
    (h2,), _ = seq_fwd("s5_res", _glu_res_fn, grid2, [], cs, xs2, [((lp, D), F32, (tr, D), lambda o, t: (t, 0))])
    return h2, (h, a, z, vg, st)


def s5_bwd(dh2, mask, w, saved):
    h, a, z, vg, st = saved
    g = {}
    grid2, cs, xs2, tr = _glu_args(h, vg, mask)
    (dskip, dvg), _ = seq_bwd("s5_res_b", _glu_res_fn, grid2, [], cs, xs2, [Arg(dh2, (tr, D), lambda o, t: (t, 0))])
    g["wglu"] = wgrad("s5_dwglu", z, dvg)
    dz = lin_t("s5_dz", dvg, w["wglu"])
    grid, ps, xs = _s5_args(a, w["tb"], w["dsk"])
    (da,), dps = seq_bwd("s5_core_b", _s5_fn, grid, ps, [], xs, [Arg(dz, (CH, 128), lambda o, t: (t, o))],
                         saved=[st], carries=[(1, 2 * S5_W)])
    g["tb"] = tuple(dps[:4])
    g["dsk"] = dps[4]
    dh, g["ng"] = norm_bwd("s5_norm_b", h, w["ng"], mask, da, dskip)
    return dh, g


def ret_consts(lp):
    pos = jnp.maximum(jnp.arange(lp, dtype=F32) - PAD, 0.0)
    inv = 1.0 / (ROPE_BASE ** (jnp.arange(0, RET_DK, 2, dtype=F32) / RET_DK))
    ang = pos[:, None] * inv[None, :]
    lg = jnp.log(1.0 - jnp.exp2(-5.0 - jnp.arange(RET_H, dtype=F32)))
    p = jnp.arange(CH, dtype=F32)
    diff = p[:, None] - p[None, :]
    decay = jnp.where(diff >= 0, jnp.exp(diff[None] * lg[:, None, None]), 0.0)
    qd = jnp.exp((p[None, :] + 1.0) * lg[:, None])[..., None]
    kd = jnp.exp((CH - 1.0 - p[None, :]) * lg[:, None])[..., None]
    cd = jnp.exp(CH * lg)[:, None, None]
    return jnp.cos(ang), jnp.sin(ang), decay, qd, kd, cd


def _ret_fn(p, c, k, x):
    (gn,) = p
    cos, sin, decay, qd, kd, cd = c
    (st,) = k
    q, kk, v, g = x
    hd = RET_DK // 2

    def rope(t):
        t1, t2 = t[:, :hd], t[:, hd:]
        return jnp.concatenate([t1 * cos - t2 * sin, t1 * sin + t2 * cos], axis=1)

    qr = rope(q)
    kr = rope(kk) * (RET_DK ** -0.5)
    o = mm(mm_nt(qr, kr) * decay, v) + mm(qr * qd, st)
    st = st * cd + mm_tn(kr * kd, v)
    mu = jnp.mean(o, axis=-1, keepdims=True)
    var = jnp.mean(jnp.square(o - mu), axis=-1, keepdims=True)
    return (st,), ((o - mu) * lax.rsqrt(var + EPS) * gn * _silu(g),)


def _ret_args(u4, gn, rc):
    lp = u4[0].shape[0]
    cos, sin, decay, qd, kd, cd = rc
    ps = [Arg(gn, (1, RET_DV), lambda o, t: (0, o))]
    cs = [Arg(cos, (CH, RET_DK // 2), lambda o, t: (t, 0)), Arg(sin, (CH, RET_DK // 2), lambda o, t: (t, 0)),
          Arg(decay, (None, CH, CH), lambda o, t: (o, 0, 0)), Arg(qd, (None, CH, 1), lambda o, t: (o, 0, 0)),
          Arg(kd, (None, CH, 1), lambda o, t: (o, 0, 0)), Arg(cd, (None, 1, 1), lambda o, t: (o, 0, 0))]
    xs = [Arg(u4[0], (CH, RET_DK), lambda o, t: (t, o)), Arg(u4[1], (CH, RET_DK), lambda o, t: (t, o)),
          Arg(u4[2], (CH, RET_DV), lambda o, t: (t, o)), Arg(u4[3], (CH, RET_DV), lambda o, t: (t, o))]
    return (RET_H, lp // CH), ps, cs, xs


def ret_fwd(h, mask, w):
    lp = h.shape[0]
    a = norm_fwd("ret_norm", h, w["ng"], mask)
    u = lin("ret_in", a, w["win"])
    u4 = [u[:, :D], u[:, D:2 * D], u[:, 2 * D:4 * D], u[:, 4 * D:]]
    grid, ps, cs, xs = _ret_args(u4, w["gn"], w["rc"])
    (z,), (st,) = seq_fwd("ret_core", _ret_fn, grid, ps, cs, xs,
                          [((lp, 2 * D), BF, (CH, RET_DV), lambda o, t: (t, o))], carries=[(RET_DK, RET_DV)])
    h2 = lin("ret_out", z, w["wo"], res=h)
    return h2, (h, a, u4, z, st)


def ret_bwd(dh2, mask, w, saved):
    h, a, u4, z, st = saved
    g = {}
    g["wo"] = wgrad("ret_dwo", z, dh2)
    dz = lin_t("ret_dz", dh2, w["wo"])
    grid, ps, cs, xs = _ret_args(u4, w["gn"], w["rc"])
    du4, (g["gn"],) = seq_bwd("ret_core_b", _ret_fn, grid, ps, cs, xs,
                              [Arg(dz, (CH, RET_DV), lambda o, t: (t, o))], saved=[st], carries=[(RET_DK, RET_DV)])
    du = jnp.concatenate(du4, axis=1)
    g["win"] = wgrad("ret_dwin", a, du)
    da = lin_t("ret_da", du, w["win"])
    dh, g["ng"] = norm_bwd("ret_norm_b", h, w["ng"], mask, da, dh2)
    return dh, g


def mla_consts(lp):
    pos = jnp.maximum(jnp.arange(lp, dtype=F32) - PAD, 0.0)
    inv = 1.0 / (ROPE_BASE ** (jnp.arange(0, MLA_ROPE, 2, dtype=F32) / MLA_ROPE))
    ang = pos[:, None] * inv[None, :]
    cos = jnp.concatenate([jnp.cos(ang), jnp.cos(ang)], axis=1)
    sin = jnp.concatenate([jnp.sin(ang), jnp.sin(ang)], axis=1)
    hd = MLA_ROPE // 2
    r = lax.broadcasted_iota(jnp.int32, (MLA_ROPE, MLA_ROPE), 0)
    c = lax.broadcasted_iota(jnp.int32, (MLA_ROPE, MLA_ROPE), 1)
    rot = jnp.where(r == c + hd, -1.0, jnp.where(c == r + hd, 1.0, 0.0)).astype(F32)
    return cos, sin, rot


def _mla_prep1_fn(p, c, k, x):
    gq, gkv = p
    (down,) = x
    return (), (_rms(down[:, :MLA_QL], gq), _rms(down[:, MLA_QL:MLA_QL + MLA_KVL], gkv), down[:, MLA_QL + MLA_KVL:])


def _mla_prep2(p, c, x):
    gq, gk = p
    cos, sin, rot = c
    q, kv, kpe = x
    qn = _rms(q, gq)
    qn_n, qn_r = qn[:, :MLA_NOPE], qn[:, MLA_NOPE:]
    qo = jnp.concatenate([qn_n, qn_r * cos + cright(qn_r, rot) * sin], axis=1)
    kn = kv[:, :MLA_NOPE]
    ms = (jnp.sum(kn * kn, axis=-1, keepdims=True) + jnp.sum(kpe * kpe, axis=-1, keepdims=True)) / MLA_QK
    r = lax.rsqrt(ms + EPS)
    kr = kpe * r * gk[:, MLA_NOPE:]
    ko = jnp.concatenate([kn * r * gk[:, :MLA_NOPE], kr * cos + cright(kr, rot) * sin], axis=1)
    return qo, ko, kv[:, MLA_NOPE:]


def _mla_prep2_fn(p, c, k, x):
    return (), _mla_prep2(p, c, x)[:2]


def _mla_prep2_b_fn(p, c, k, x):
    return (), _mla_prep2(p, c, x)


def _prep1_args(down, gq, gkv):
    lp = down.shape[0]
    tr = _row_tile(lp)
    ps = [Arg(gq, (1, MLA_QL), lambda o, t: (0, 0), shared=True), Arg(gkv, (1, MLA_KVL), lambda o, t: (0, 0), shared=True)]
    return (1, lp // tr), ps, [Arg(down, (tr, down.shape[1]), lambda o, t: (t, 0))], tr


def _prep2_args(qraw, kvraw, kpe, gq, gk, mc):
    lp = kpe.shape[0]
    tr = _row_tile(lp)
    cos, sin, rot = mc
    ps = [Arg(gq, (1, MLA_QK), lambda o, t: (0, 0), shared=True), Arg(gk, (1, MLA_QK), lambda o, t: (0, 0), shared=True)]
    cs = [Arg(cos, (tr, MLA_ROPE), lambda o, t: (o, 0)), Arg(sin, (tr, MLA_ROPE), lambda o, t: (o, 0)),
          Arg(rot, (MLA_ROPE, MLA_ROPE), lambda o, t: (0, 0))]
    xs = [Arg(qraw, (None, tr, MLA_QK), lambda o, t: (t, o, 0)), Arg(kvraw, (None, tr, MLA_NOPE + MLA_V), lambda o, t: (t, o, 0)),
          Arg(kpe, (tr, MLA_ROPE), lambda o, t: (o, 0), acc=True)]
    return (lp // tr, MLA_H), ps, cs, xs, tr


def _attn_tile(lp):
    return 832 if (lp % 832 == 0 and lp > 832) else 64


def _attn_mask(qi, ki, ta):
    rows = qi * ta + lax.broadcasted_iota(jnp.int32, (ta, ta), 0)
    cols = ki * ta + lax.broadcasted_iota(jnp.int32, (ta, ta), 1)
    return (cols >= PAD) & ((cols // CH) <= (rows // CH))


def attn_fwd(q, k, kv):
    nh, lp, dq = q.shape
    ta = _attn_tile(lp)
    nb = lp // ta
    scale = MLA_QK ** -0.5

    def body(q_ref, k_ref, v_ref, o_ref, lse_ref, m_s, l_s, acc_s):
        qi, ki = pl.program_id(1), pl.program_id(2)

        @pl.when(ki == 0)
        def _():
            m_s[...] = jnp.full_like(m_s, NEG)
            l_s[...] = jnp.zeros_like(l_s)
            acc_s[...] = jnp.zeros_like(acc_s)

        def step(masked):
            s = _bdot(q_ref[...], k_ref[...], 1, 1) * scale
            if masked:
                s = jnp.where(_attn_mask(qi, ki, ta), s, NEG)
            m_new = jnp.maximum(m_s[...], jnp.max(s, axis=-1, keepdims=True))
            p = jnp.exp(s - m_new)
            alpha = jnp.exp(m_s[...] - m_new)
            l_s[...] = alpha * l_s[...] + jnp.sum(p, axis=-1, keepdims=True)
            acc_s[...] = alpha * acc_s[...] + _bdot(p, v_ref[...], 1, 0)
            m_s[...] = m_new

        pl.when((ki == qi) | (ki == 0))(functools.partial(step, True))
        pl.when((ki < qi) & (ki > 0))(functools.partial(step, False))

        @pl.when(ki == nb - 1)
        def _():
            o_ref[...] = (acc_s[...] / l_s[...]).astype(o_ref.dtype)
            lse_ref[...] = m_s[...] + jnp.log(l_s[...])

    return pl.pallas_call(
        body, name="mla_attn", grid=(nh, nb, nb),
        in_specs=[pl.BlockSpec((None, ta, dq), lambda h, qi, ki: (h, qi, 0)),
                  pl.BlockSpec((None, ta, dq), lambda h, qi, ki: (h, jnp.minimum(ki, qi), 0)),
                  pl.BlockSpec((None, ta, MLA_V), lambda h, qi, ki: (h, jnp.minimum(ki, qi), 1))],
        out_specs=[pl.BlockSpec((ta, MLA_V), lambda h, qi, ki: (qi, h)),
                   pl.BlockSpec((None, ta, 1), lambda h, qi, ki: (h, qi, 0))],
        out_shape=[SDS((lp, nh * MLA_V), BF), SDS((nh, lp, 1), F32)],
        scratch_shapes=[pltpu.VMEM((ta, 1), F32), pltpu.VMEM((ta, 1), F32), pltpu.VMEM((ta, MLA_V), F32)],
        compiler_params=_cp(3))(q, k, kv)


def attn_bwd(q, k, kv, o, do, lse):
    nh, lp, dq = q.shape
    ta = _attn_tile(lp)
    nb = lp // ta
    scale = MLA_QK ** -0.5

    def body(q_ref, k_ref, v_ref, o_ref, do_ref, lse_ref, dq_ref, dk_ref, dv_ref, dk_s, dv_s):
        ki, qi = pl.program_id(1), pl.program_id(2)

        @pl.when((ki == 0) & (qi == 0))
        def _():
            dq_ref[...] = jnp.zeros_like(dq_ref)

        @pl.when(qi == 0)
        def _():
            dk_s[...] = jnp.zeros_like(dk_s)
            dv_s[...] = jnp.zeros_like(dv_s)

        def step(masked):
            dov = do_ref[...]
            s = _bdot(q_ref[...], k_ref[...], 1, 1) * scale
            if masked:
                s = jnp.where(_attn_mask(qi, ki, ta), s, NEG)
            p = jnp.exp(s - lse_ref[...])
            delta = jnp.sum(dov * o_ref[...].astype(F32), axis=-1, keepdims=True)
            dv_s[...] += _bdot(p, dov, 0, 0)
            dp = _bdot(dov, v_ref[...], 1, 1)
            ds = p * (dp - delta) * scale
            rows = pl.ds(pl.multiple_of(qi * ta, ta), ta)
            dq_ref[rows, :] += _bdot(ds, k_ref[...], 1, 0)
            dk_s[...] += _bdot(ds, q_ref[...], 0, 0)

        pl.when((ki == qi) | (ki == 0))(functools.partial(step, True))
        pl.when((ki < qi) & (ki > 0))(functools.partial(step, False))

        @pl.when(qi == nb - 1)
        def _():
            dk_ref[...] = dk_s[...]
            dv_ref[...] = dv_s[...]

    qmap = lambda h, ki, qi: (h, jnp.maximum(qi, ki), 0)
    return pl.pallas_call(
        body, name="mla_attn_b", grid=(nh, nb, nb),
        in_specs=[pl.BlockSpec((None, ta, dq), qmap),
                  pl.BlockSpec((None, ta, dq), lambda h, ki, qi: (h, ki, 0)),
                  pl.BlockSpec((None, ta, MLA_V), lambda h, ki, qi: (h, ki, 1)),
                  pl.BlockSpec((ta, MLA_V), lambda h, ki, qi: (jnp.maximum(qi, ki), h)),
                  pl.BlockSpec((ta, MLA_V), lambda h, ki, qi: (jnp.maximum(qi, ki), h)),
                  pl.BlockSpec((None, ta, 1), qmap)],
        out_specs=[pl.BlockSpec((None, lp, dq), lambda h, ki, qi: (h, 0, 0)),
                   pl.BlockSpec((None, ta, dq), lambda h, ki, qi: (h, ki, 0)),
                   pl.BlockSpec((None, ta, MLA_V), lambda h, ki, qi: (h, ki, 0))],
        out_shape=[SDS((nh, lp, dq), F32), SDS((nh, lp, dq), F32), SDS((nh, lp, MLA_V), F32)],
        scratch_shapes=[pltpu.VMEM((ta, dq), F32), pltpu.VMEM((ta, MLA_V), F32)],
        compiler_params=_cp(3))(q, k, kv, o, do, lse)


def mla_fwd(h, mask, w):
    lp = h.shape[0]
    a = norm_fwd("mla_norm", h, w["ng"], mask)
    down = lin("mla_down", a, w["wdown"])
    grid, ps, xs, tr = _prep1_args(down, w["gcq"], w["gckv"])
    row = lambda o, t: (t, 0)
    (cq, ckv, kpe), _ = seq_fwd("mla_prep1", _mla_prep1_fn, grid, ps, [], xs,
                                [((lp, MLA_QL), BF, (tr, MLA_QL), row), ((lp, MLA_KVL), BF, (tr, MLA_KVL), row),
                                 ((lp, MLA_ROPE), F32, (tr, MLA_ROPE), row)])
    qraw = lin_bo("mla_uq", cq, w["wuq"])
    kvraw = lin_bo("mla_ukv", ckv, w["wukv"])
    grid, ps, cs, xs, tr = _prep2_args(qraw, kvraw, kpe, w["gq"], w["gk"], w["mc"])
    hm = lambda o, t: (t, o, 0)
    (q, k), _ = seq_fwd("mla_prep2", _mla_prep2_fn, grid, ps, cs, xs,
                        [((MLA_H, lp, MLA_QK), BF, (None, tr, MLA_QK), hm), ((MLA_H, lp, MLA_QK), BF, (None, tr, MLA_QK), hm)])
    o, lse = attn_fwd(q, k, kvraw)
    h2 = lin("mla_out", o, w["wo"], res=h)
    return h2, (h, a, down, cq, ckv, kpe, qraw, kvraw, q, k, o, lse)


def mla_bwd(dh2, mask, w, saved):
    h, a, down, cq, ckv, kpe, qraw, kvraw, q, k, o, lse = saved
    lp = h.shape[0]
    g = {}
    g["wo"] = wgrad("mla_dwo", o, dh2)
    do = lin_t("mla_do", dh2, w["wo"])
    dq, dk, dv = attn_bwd(q, k, kvraw, o, do, lse)
    grid, ps, cs, xs, tr = _prep2_args(qraw, kvraw, kpe, w["gq"], w["gk"], w["mc"])
    hm = lambda o, t: (t, o, 0)
    (dqraw, dkvraw, dkpe), (g["gq"], g["gk"]) = seq_bwd(
        "mla_prep2_b", _mla_prep2_b_fn, grid, ps, cs, xs,
        [Arg(dq, (None, tr, MLA_QK), hm), Arg(dk, (None, tr, MLA_QK), hm), Arg(dv, (None, tr, MLA_V), hm)])
    g["wuq"] = wgrad_bo("mla_dwuq", cq, dqraw)
    dcq = lin_t_bi("mla_dcq", dqraw, w["wuq"])
    g["wukv"] = wgrad_bo("mla_dwukv", ckv, dkvraw)
    dckv = lin_t_bi("mla_dckv", dkvraw, w["wukv"])
    grid, ps, xs, tr = _prep1_args(down, w["gcq"], w["gckv"])
    row = lambda o, t: (t, 0)
    (ddown,), (g["gcq"], g["gckv"]) = seq_bwd(
        "mla_prep1_b", _mla_prep1_fn, grid, ps, [], xs,
        [Arg(dcq, (tr, MLA_QL), row), Arg(dckv, (tr, MLA_KVL), row), Arg(dkpe, (tr, MLA_ROPE), row)])
    g["wdown"] = wgrad("mla_dwdown", a, ddown)
    da = lin_t("mla_da", ddown, w["wdown"])
    dh, g["ng"] = norm_bwd("mla_norm_b", h, w["ng"], mask, da, dh2)
    return dh, g


def loss_head(h, target):
    lp, d = h.shape
    assert OFF == CH

    def body(h_ref, t_ref, loss_ref, dh_ref):
        i = pl.program_id(0)

        @pl.when(i == 0)
        def _():
            loss_ref[...] = jnp.zeros_like(loss_ref)

        e = jnp.where(i > 0, h_ref[...] - t_ref[...], 0.0)
        loss_ref[...] += jnp.sum(e * e) * (0.5 / d)
        dh_ref[...] = e * (1.0 / d)

    return pl.pallas_call(
        body, name="loss_head", grid=(lp // CH,),
        in_specs=[pl.BlockSpec((CH, d), lambda i: (i, 0)), pl.BlockSpec((CH, d), lambda i: (jnp.maximum(i - 1, 0), 0))],
        out_specs=[pl.BlockSpec((8, 128), lambda i: (0, 0)), pl.BlockSpec((CH, d), lambda i: (i, 0))],
        out_shape=[SDS((8, 128), F32), SDS((lp, d), F32)], compiler_params=_cp(1))(h, target)


ADAM_LAND_BYTES = 20 * 1024 * 1024


def _adam_tile(r, c, nl):
    if r % 8:
        return r
    best = 8
    for t in range(8, r + 1, 8):
        if r % t == 0 and N_DEV * t * c * 4 * 2 * nl <= ADAM_LAND_BYTES:
            best = t
    return best


def adamw(name, lands, w, m, v):
    nl, r, c = w.shape
    tr = _adam_tile(r, c, nl)
    c1 = 1.0 / (1.0 - ADAM_B1 ** ADAM_STEP)
    c2 = 1.0 / (1.0 - ADAM_B2 ** ADAM_STEP)

    def body(*refs):
        l_refs = refs[:nl]
        w_ref, m_ref, v_ref, g_ref, d_ref, nm_ref, nv_ref = refs[nl:]
        layer = pl.program_id(0)
        for j in range(nl):
            @pl.when(layer == j)
            def _(j=j):
                g = l_refs[j][0]
                for i in range(1, N_DEV):
                    g = g + l_refs[j][i]
                g_ref[...] = g

        g = g_ref[...]
        nm = ADAM_B1 * m_ref[...] + (1.0 - ADAM_B1) * g
        nv = ADAM_B2 * v_ref[...] + (1.0 - ADAM_B2) * (g * g)
        nm_ref[...] = nm
        nv_ref[...] = nv
        d_ref[...] = -ADAM_LR * ((nm * c1) / (jnp.sqrt(nv * c2) + ADAM_EPS) + ADAM_WD * w_ref[...])

    blk = pl.BlockSpec((None, tr, c), lambda l, i: (l, i, 0))
    land_specs = [pl.BlockSpec((N_DEV, tr, c), lambda l, i, j=j: (0, jnp.where(l == j, i, 0), 0)) for j in range(nl)]
    return pl.pallas_call(
        body, name=name, grid=(nl, r // tr), in_specs=land_specs + [blk, blk, blk],
        out_specs=[blk, blk, blk, blk], out_shape=[SDS((nl, r, c), F32)] * 4, compiler_params=_cp(2))(*lands, w, m, v)


ANY = pl.BlockSpec(memory_space=pl.ANY)
MESH = pl.DeviceIdType.MESH


def _me():
    return lax.axis_index("x"), lax.axis_index("y"), lax.axis_index("c")


def _peers():
    x, y, c = _me()
    out = []
    for k in range(1, N_DEV):
        px = 1 - x if k & 4 else x
        py = 1 - y if k & 2 else y
        pc = 1 - c if k & 1 else c
        out.append(((px, py, pc), 4 * px + 2 * py + pc))
    return out


def all_gather(name, xs):
    n = len(xs)

    def body(*refs):
        x_refs, o_refs = refs[:n], refs[n:2 * n]
        send, recv, loc = refs[2 * n:]
        x, y, c = _me()
        me = 4 * x + 2 * y + c
        peers = _peers()
        local = [pltpu.make_async_copy(x_refs[i], o_refs[i].at[me], loc.at[i]) for i in range(n)]
        for cp in local:
            cp.start()
        sends = []
        for k, (pid, _) in enumerate(peers):
            for i in range(n):
                cp = pltpu.make_async_remote_copy(src_ref=x_refs[i], dst_ref=o_refs[i].at[me], send_sem=send.at[i, k],
                                                  recv_sem=recv.at[i, k], device_id=pid, device_id_type=MESH)
                cp.start()
                sends.append(cp)
        for k, (pid, pidx) in enumerate(peers):
            for i in range(n):
                pltpu.make_async_remote_copy(src_ref=x_refs[i], dst_ref=o_refs[i].at[pidx], send_sem=send.at[i, k],
                                             recv_sem=recv.at[i, k], device_id=pid, device_id_type=MESH).wait_recv()
        for cp in sends:
            cp.wait_send()
        for cp in local:
            cp.wait()

    return pl.pallas_call(
        body, name=name, in_specs=[ANY] * n, out_specs=[ANY] * n,
        out_shape=[SDS((N_DEV,) + x.shape, x.dtype) for x in xs],
        scratch_shapes=[pltpu.SemaphoreType.DMA((n, N_DEV - 1)), pltpu.SemaphoreType.DMA((n, N_DEV - 1)),
                        pltpu.SemaphoreType.DMA((n,))])(*xs)


HBM_SPEC = pl.BlockSpec(memory_space=pltpu.HBM)
SEM_SPEC = pl.BlockSpec(memory_space=pltpu.SEMAPHORE)
DATAFLOW = pltpu.SideEffectType.DATAFLOW_SIDE_EFFECTING


def _my_index():
    return 4 * lax.axis_index("x") + 2 * lax.axis_index("y") + lax.axis_index("c")


def _hbm(a):
    return pltpu.with_memory_space_constraint(a, pltpu.HBM)


NP = N_DEV - 1


def _push_copy(x_ref, land_ref, send, recv, pid, src_idx, dst_idx, scatter):
    src = x_ref.at[src_idx] if scatter else x_ref
    return pltpu.make_async_remote_copy(src_ref=src, dst_ref=land_ref.at[dst_idx], send_sem=send, recv_sem=recv,
                                        device_id=pid, device_id_type=MESH)


def push_start(name, xs, me, scatter):
    n = len(xs)
    lands = []
    for a in xs:
        own = lax.dynamic_index_in_dim(a, me, 0, keepdims=True) if scatter else a[None]
        z = lax.empty((N_DEV,) + own.shape[1:], a.dtype)
        lands.append(lax.dynamic_update_slice(z, own, (me,) + (0,) * (own.ndim - 1)))
    ns = 2 * NP * n

    def body(*refs):
        x_refs, land_refs = refs[:n], refs[n:2 * n]
        sems = refs[2 * n:2 * n + ns]
        token = refs[-1]
        x, y, c = _me()
        mine = 4 * x + 2 * y + c
        for i in range(n):
            for k, (pid, pidx) in enumerate(_peers()):
                s = 2 * (NP * i + k)
                _push_copy(x_refs[i], land_refs[i], sems[s], sems[s + 1], pid, pidx, mine, scatter).start()
        token[...] = jnp.zeros_like(token)

    out_shape = ([pltpu.SemaphoreType.DMA(())] * ns + [pltpu.HBM(a.shape, a.dtype) for a in xs + lands]
                 + [SDS((8, 128), F32)])
    res = pl.pallas_call(
        body, name=name, out_shape=out_shape, in_specs=[HBM_SPEC] * (2 * n),
        out_specs=[SEM_SPEC] * ns + [HBM_SPEC] * (2 * n) + [pl.BlockSpec(memory_space=pltpu.VMEM)],
        input_output_aliases={i: ns + i for i in range(2 * n)},
        compiler_params=pltpu.CompilerParams(has_side_effects=DATAFLOW))(*[_hbm(a) for a in xs + lands])
    sems, thru, token = res[:ns], res[ns:-1], res[-1]
    return [dict(x=thru[i], land=thru[n + i], sems=list(sems[2 * NP * i:2 * NP * (i + 1)]), token=token)
            for i in range(n)]


def push_wait(name, hds, after, scatter):
    n = len(hds)
    ns = 2 * NP

    def body(*refs):
        x_refs, land_refs = refs[:n], refs[n:2 * n]
        sems = refs[2 * n:2 * n + ns * n]
        for i in range(n):
            for k, (pid, pidx) in enumerate(_peers()):
                cp = _push_copy(x_refs[i], land_refs[i], sems[ns * i + 2 * k], sems[ns * i + 2 * k + 1], pid, pidx, pidx,
                                scatter)
                cp.wait_send()
                cp.wait_recv()

    arrs = [hd["x"] for hd in hds] + [hd["land"] for hd in hds]
    sems = [s for hd in hds for s in hd["sems"]]
    res = pl.pallas_call(
        body, name=name, out_shape=[pltpu.HBM(a.shape, a.dtype) for a in arrs],
        in_specs=[HBM_SPEC] * (2 * n) + [SEM_SPEC] * (ns * n) + [ANY], out_specs=[HBM_SPEC] * (2 * n),
        input_output_aliases={i: i for i in range(2 * n)},
        compiler_params=pltpu.CompilerParams(has_side_effects=DATAFLOW))(*arrs, *sems, after)
    return list(res[n:])


WEIGHTS = ['meta_tokens', 'norm_mix_g', 'norm_ffn_g', 'mla_w_down', 'mla_cq_norm_g', 'mla_ckv_norm_g', 'mla_w_uq',
           'mla_w_ukv', 'mla_q_head_g', 'mla_k_head_g', 'mla_w_o', 'hgrn_w_in', 'hgrn_lb_logits', 'hgrn_o_norm_g',
           'hgrn_w_o', 's5_lam_re', 's5_lam_im', 's5_log_dt', 's5_b_re', 's5_b_im', 's5_c_re', 's5_c_im', 's5_d',
           's5_w_glu', 'ret_w_in', 'ret_gn_g', 'ret_w_o', 'ffn_w_up', 'ffn_conv_w', 'ffn_conv_b', 'ffn_w_down']
BIG = ['mla_w_down', 'mla_w_uq', 'mla_w_ukv', 'mla_w_o', 'hgrn_w_in', 'hgrn_w_o', 's5_w_glu', 'ret_w_in', 'ret_w_o',
       'ffn_w_up', 'ffn_w_down']
SMALL_SH = ['meta_tokens', 's5_d', 'ret_gn_g', 'ffn_conv_w']
SMALL_REP = ['norm_mix_g', 'norm_ffn_g', 'mla_cq_norm_g', 'mla_ckv_norm_g', 'mla_q_head_g', 'mla_k_head_g',
             'hgrn_lb_logits', 'hgrn_o_norm_g', 's5_lam_re', 's5_lam_im', 's5_log_dt', 's5_b_re', 's5_b_im',
             's5_c_re', 's5_c_im', 'ffn_conv_b']
LANE = 128


def _flat(arrs, mult):
    v = jnp.concatenate([a.reshape(-1) for a in arrs])
    pad = (-v.shape[0]) % mult
    return jnp.pad(v, (0, pad)).reshape(-1, LANE)


def _unflat(flat2d, like):
    v = flat2d.reshape(-1)
    out, o = [], 0
    for a in like:
        out.append(v[o:o + a.size].reshape(a.shape))
        o += a.size
    return out


def _lb_of(logits):
    cum = jnp.cumsum(jax.nn.softmax(logits, axis=0), axis=0)
    return (cum - cum[0:1])[1:2]


def _cols_to_blocks(g):
    k, n = g.shape
    return g.reshape(k, N_DEV, n // N_DEV).transpose(1, 0, 2)


def _blocks_to_cols(wb):
    nb, k, n = wb.shape
    return wb.transpose(1, 0, 2).reshape(k, nb * n)


SUBS = ['mla', 'ffn0', 'hgrn', 'ffn1', 's5', 'ffn2', 'ret', 'ffn3']
GROUPS = [[('mla_w_down', 0), ('mla_w_uq', 0), ('mla_w_ukv', 0), ('mla_w_o', 0)],
          [('ffn_w_up', 0), ('ffn_w_down', 0)],
          [('hgrn_w_in', 0), ('hgrn_w_o', 0)],
          [('ffn_w_up', 1), ('ffn_w_down', 1)],
          [('s5_w_glu', 0)],
          [('ffn_w_up', 2), ('ffn_w_down', 2)],
          [('ret_w_in', 0), ('ret_w_o', 0)],
          [('ffn_w_up', 3), ('ffn_w_down', 3)]]


def _pack8(parts, mult):
    v = jnp.concatenate(parts, axis=1)
    return jnp.pad(v, ((0, 0), (0, (-v.shape[1]) % mult))).reshape(N_DEV, -1, LANE)


def _sub_weights(k, got, rep, tabs, lp):
    ngm, ngf = rep['norm_mix_g'], rep['norm_ffn_g']
    if k == 0:
        return dict(ng=ngm[0:1], wdown=got[0].reshape(D, -1), gcq=rep['mla_cq_norm_g'], gckv=rep['mla_ckv_norm_g'],
                    wuq=got[1], wukv=got[2], gq=rep['mla_q_head_g'], gk=rep['mla_k_head_g'], wo=got[3].reshape(D, D),
                    mc=mla_consts(lp))
    if k == 2:
        return dict(ng=ngm[1:2], win=got[0], lb=tabs['lb'], go=rep['hgrn_o_norm_g'], wo=got[1].reshape(D, D))
    if k == 4:
        return dict(ng=ngm[2:3], tb=tabs['tb'], dsk=tabs['s5_d'], wglu=_blocks_to_cols(got[0]))
    if k == 6:
        return dict(ng=ngm[3:4], win=_blocks_to_cols(got[0]), gn=tabs['ret_gn_g'], wo=got[1].reshape(2 * D, D),
                    rc=ret_consts(lp))
    i = k // 2
    return dict(ng=ngf[i:i + 1], up=got[0], cw=tabs['conv_w'][:, i].reshape(2, 4, 3, 1, FFN_B),
                cb=rep['ffn_conv_b'][i].reshape(2, 4, 1, FFN_B), down=got[1].reshape(4, FFN_B, D))


def _sub_grad_blocks(k, g):
    if k == 0:
        parts = [g['wdown'], g['wuq'], g['wukv'], g['wo']]
    elif k == 2:
        parts = [g['win'], g['wo']]
    elif k == 4:
        parts = [_cols_to_blocks(g['wglu'])]
    elif k == 6:
        parts = [_cols_to_blocks(g['win']), g['wo']]
    else:
        parts = [g['up'], g['down']]
    return parts


_FWD = [mla_fwd, None, hgrn_fwd, None, s5_fwd, None, ret_fwd, None]
_BWD = [mla_bwd, None, hgrn_bwd, None, s5_bwd, None, ret_bwd, None]


def _step(args):
    w = {n: args[n] for n in WEIGHTS}
    x2, tgt = args['x'][0], args['loss_target'][0]

    lp = x2.shape[0] + OFF
    me = _my_index()
    mask = _rowmask(lp)
    rep = {n: w[n] for n in SMALL_REP}

    xs, slots = [], []
    for gi, grp in enumerate(GROUPS):
        items = [w[n][l].astype(BF) for n, l in grp] + ([_flat([w[n] for n in SMALL_SH], LANE)] if gi == 0 else [])
        slots.append((len(xs), len(items)))
        xs += items
    gh = push_start("gather_start", xs, me, scatter=False)

    def fetch(gi, after):
        s, cnt = slots[gi]
        return push_wait("gather_wait_" + SUBS[gi], gh[s:s + cnt], after, scatter=False)

    got = fetch(0, x2)
    sm, o, smp = got[-1].reshape(N_DEV, -1), 0, {}
    for n in SMALL_SH:
        smp[n] = sm[:, o:o + w[n].size].reshape((N_DEV,) + w[n].shape)
        o += w[n].size
    meta = smp['meta_tokens'].transpose(1, 0, 2).reshape(N_META, D)
    lb, lb_vjp = jax.vjp(_lb_of, rep['hgrn_lb_logits'])
    s5p = [rep[n][0] for n in ('s5_lam_re', 's5_lam_im', 's5_log_dt', 's5_b_re', 's5_b_im', 's5_c_re', 's5_c_im')]
    tb, tb_vjp = jax.vjp(s5_tables, *s5p)
    tabs = dict(lb=lb, tb=tb, s5_d=smp['s5_d'].reshape(1, D), ret_gn_g=smp['ret_gn_g'].reshape(1, 2 * D),
                conv_w=smp['ffn_conv_w'])
    h = jnp.concatenate([jnp.zeros((PAD, D), F32), meta, x2], axis=0)
    ws, saved = [], []
    for k in range(8):
        if k > 0:
            got = fetch(k, h)
        ws.append(_sub_weights(k, got, rep, tabs, lp))
        if k % 2:
            h, sv = ffn_fwd(k // 2, h, mask, ws[k])
        else:
            h, sv = _FWD[k](h, mask, ws[k])
        saved.append(sv)
    loss, dh = loss_head(h, tgt)

    gs, sh = [None] * 8, [None] * 8
    mk = mask
    for k in reversed(range(8)):
        if k % 2:
            dh, gs[k] = ffn_bwd(k // 2, dh, mk, ws[k], saved[k])
        else:
            dh, gs[k] = _BWD[k](dh, mk, ws[k], saved[k])
        blocks = [b.reshape((N_DEV,) + w[n].shape[1:]) for b, (n, _) in zip(_sub_grad_blocks(k, gs[k]), GROUPS[k])]
        if k == 0:
            dmeta = dh[PAD:OFF].reshape(N_META, N_DEV, D // N_DEV).transpose(1, 0, 2)
            dcw = jnp.stack([gs[2 * i + 1]['cw'].reshape(N_DEV, 3, FFN_B) for i in range(4)], axis=1)
            blocks.append(_pack8([t.reshape(N_DEV, -1) for t in (dmeta, gs[4]['dsk'], gs[6]['gn'], dcw)], LANE))
        sh[k] = push_start("scatter_start_" + SUBS[k], blocks, me, scatter=True)
        mk = mask + sh[k][0]["token"][0, 0]
    grad_x = dh[OFF:]

    ds5 = tb_vjp(gs[4]['tb'])
    g_rep = {
        'norm_mix_g': jnp.concatenate([gs[k]['ng'] for k in (0, 2, 4, 6)], axis=0),
        'norm_ffn_g': jnp.concatenate([gs[k]['ng'] for k in (1, 3, 5, 7)], axis=0),
        'mla_cq_norm_g': gs[0]['gcq'], 'mla_ckv_norm_g': gs[0]['gckv'], 'mla_q_head_g': gs[0]['gq'],
        'mla_k_head_g': gs[0]['gk'], 'hgrn_lb_logits': lb_vjp(gs[2]['lb'])[0], 'hgrn_o_norm_g': gs[2]['go'],
        's5_lam_re': ds5[0], 's5_lam_im': ds5[1], 's5_log_dt': ds5[2], 's5_b_re': ds5[3], 's5_b_im': ds5[4],
        's5_c_re': ds5[5], 's5_c_im': ds5[6],
        'ffn_conv_b': jnp.stack([gs[k]['cb'].reshape(-1) for k in (1, 3, 5, 7)], axis=0),
    }
    grep = _flat([g_rep[n] for n in SMALL_REP], 8 * LANE)
    (rep_land,) = all_gather("gather_small_grads", [grep])

    lands = {n: [None] * w[n].shape[0] for n in BIG}
    for k in reversed(range(8)):
        got = push_wait("scatter_wait_" + SUBS[k], sh[k], rep_land, scatter=True)
        for (n, l), t in zip(GROUPS[k], got):
            lands[n][l] = t
        if k == 0:
            small_land = got[-1]
    res = {}
    for n in BIG:
        res[n] = adamw("adam_" + n, lands[n], w[n], args['m_' + n], args['v_' + n])

    def flat_adam(name, land, names, mult):
        like = [w[n] for n in names]
        out = adamw(name, [land], _flat(like, mult)[None], _flat([args['m_' + n] for n in names], mult)[None],
                    _flat([args['v_' + n] for n in names], mult)[None])
        for n, parts in zip(names, zip(*[_unflat(t, like) for t in out])):
            res[n] = list(parts)

    flat_adam("adam_small_sharded", small_land, SMALL_SH, LANE)
    flat_adam("adam_small_replicated", rep_land, SMALL_REP, 8 * LANE)

    total = lax.psum(loss[0, 0], ("x", "y", "c"))
    outs = [total, grad_x[None]]
    for k in range(4):
        outs += [res[n][k] for n in WEIGHTS]
    return tuple(outs)


def kernel(x, meta_tokens, norm_mix_g, norm_ffn_g, mla_w_down, mla_cq_norm_g, mla_ckv_norm_g, mla_w_uq, mla_w_ukv, mla_q_head_g, mla_k_head_g, mla_w_o, hgrn_w_in, hgrn_lb_logits, hgrn_o_norm_g, hgrn_w_o, s5_lam_re, s5_lam_im, s5_log_dt, s5_b_re, s5_b_im, s5_c_re, s5_c_im, s5_d, s5_w_glu, ret_w_in, ret_gn_g, ret_w_o, ffn_w_up, ffn_conv_w, ffn_conv_b, ffn_w_down, loss_target, m_meta_tokens, m_norm_mix_g, m_norm_ffn_g, m_mla_w_down, m_mla_cq_norm_g, m_mla_ckv_norm_g, m_mla_w_uq, m_mla_w_ukv, m_mla_q_head_g, m_mla_k_head_g, m_mla_w_o, m_hgrn_w_in, m_hgrn_lb_logits, m_hgrn_o_norm_g, m_hgrn_w_o, m_s5_lam_re, m_s5_lam_im, m_s5_log_dt, m_s5_b_re, m_s5_b_im, m_s5_c_re, m_s5_c_im, m_s5_d, m_s5_w_glu, m_ret_w_in, m_ret_gn_g, m_ret_w_o, m_ffn_w_up, m_ffn_conv_w, m_ffn_conv_b, m_ffn_w_down, v_meta_tokens, v_norm_mix_g, v_norm_ffn_g, v_mla_w_down, v_mla_cq_norm_g, v_mla_ckv_norm_g, v_mla_w_uq, v_mla_w_ukv, v_mla_q_head_g, v_mla_k_head_g, v_mla_w_o, v_hgrn_w_in, v_hgrn_lb_logits, v_hgrn_o_norm_g, v_hgrn_w_o, v_s5_lam_re, v_s5_lam_im, v_s5_log_dt, v_s5_b_re, v_s5_b_im, v_s5_c_re, v_s5_c_im, v_s5_d, v_s5_w_glu, v_ret_w_in, v_ret_gn_g, v_ret_w_o, v_ffn_w_up, v_ffn_conv_w, v_ffn_conv_b, v_ffn_w_down):
    return _step(dict(locals()))
```

```python
import functools
import math

import jax
import jax.numpy as jnp
from jax import lax
from jax.experimental import pallas as pl
from jax.experimental.pallas import tpu as pltpu

F32 = jnp.float32
BF = jnp.bfloat16
SDS = jax.ShapeDtypeStruct

N_DEV = 8
D = 1024
N_META = 16
PAD = 48
OFF = PAD + N_META
CH = 64
EPS = 1e-6
NEG = -1e30
ROPE_BASE = 10000.0

MLA_H, MLA_NOPE, MLA_ROPE, MLA_V = 8, 128, 64, 128
MLA_QK = MLA_NOPE + MLA_ROPE
MLA_QL, MLA_KVL = 384, 256
HG_H, HG_D, HG_C = 8, 128, 16
S5_G, S5_P, S5_K = 64, 64, 16
S5_SG = 8
RET_H, RET_DK, RET_DV = 4, 256, 512
FFN_F = 2816
FFN_B = 704

ADAM_LR, ADAM_B1, ADAM_B2, ADAM_EPS, ADAM_WD, ADAM_STEP = 0.001, 0.9, 0.999, 1e-08, 0.01, 10

VMEM_LIMIT = 56 * 1024 * 1024
ARB = "arbitrary"


def _cp(n):
    return pltpu.CompilerParams(dimension_semantics=(ARB,) * n, vmem_limit_bytes=VMEM_LIMIT)


def _bdot(a, b, ca, cb):
    return lax.dot_general(a.astype(BF), b.astype(BF), (((ca,), (cb,)), ((), ())), preferred_element_type=F32)


@jax.custom_vjp
def mm(a, b):
    return _bdot(a, b, 1, 0)


@jax.custom_vjp
def mm_nt(a, b):
    return _bdot(a, b, 1, 1)


@jax.custom_vjp
def mm_tn(a, b):
    return _bdot(a, b, 0, 0)


mm.defvjp(lambda a, b: (mm(a, b), (a, b)),
          lambda r, g: (mm_nt(g, r[1]).astype(r[0].dtype), mm_tn(r[0], g).astype(r[1].dtype)))
mm_nt.defvjp(lambda a, b: (mm_nt(a, b), (a, b)),
             lambda r, g: (mm(g, r[1]).astype(r[0].dtype), mm_tn(g, r[0]).astype(r[1].dtype)))
mm_tn.defvjp(lambda a, b: (mm_tn(a, b), (a, b)),
             lambda r, g: (mm_nt(r[1], g).astype(r[0].dtype), mm(r[0], g).astype(r[1].dtype)))


def _xdot(a, b, ca, cb):
    return lax.dot_general(a, b, (((ca,), (cb,)), ((), ())), preferred_element_type=F32,
                           precision=lax.Precision.HIGHEST)


@jax.custom_vjp
def cright(x, r):
    return _xdot(x, r, 1, 0)


cright.defvjp(lambda x, r: (cright(x, r), r), lambda r, g: (_xdot(g, r, 1, 1), jnp.zeros_like(r)))


def _shift_raw(x, s):
    n = x.shape[0]
    r = lax.broadcasted_iota(jnp.int32, x.shape, 0)
    y = pltpu.roll(x, s % n, 0)
    return jnp.where((r >= s) & (r < n + s), y, 0.0)


@functools.partial(jax.custom_vjp, nondiff_argnums=(1,))
def shift_rows(x, s):
    return _shift_raw(x, s)


shift_rows.defvjp(lambda x, s: (_shift_raw(x, s), None), lambda s, _, g: (_shift_raw(g, -s),))


def _seg_shift_raw(x, s, seg, up):
    n = x.shape[0]
    r = lax.broadcasted_iota(jnp.int32, x.shape, 0) % seg
    if up:
        return jnp.where(r < seg - s, pltpu.roll(x, n - s, 0), 0.0)
    return jnp.where(r >= s, pltpu.roll(x, s, 0), 0.0)


@functools.partial(jax.custom_vjp, nondiff_argnums=(1, 2))
def seg_shift(x, s, seg):
    return _seg_shift_raw(x, s, seg, False)


seg_shift.defvjp(lambda x, s, seg: (_seg_shift_raw(x, s, seg, False), None),
                 lambda s, seg, _, g: (_seg_shift_raw(g, s, seg, True),))


def _seg_cumsum(x, seg):
    s = 1
    while s < seg:
        x = x + seg_shift(x, s, seg)
        s *= 2
    return x


def _rms(x, g):
    return x * lax.rsqrt(jnp.mean(x * x, axis=-1, keepdims=True) + EPS) * g


def _silu(x):
    return x * jax.nn.sigmoid(x)


def _mm_call(name, a, b, *, grid, a_spec, b_spec, o_shape, o_spec, dims, acc_shape, res=None, res_spec=None,
             mask_tm=None):
    nk = grid[2]

    def body(*refs):
        if res is None:
            a_ref, b_ref, o_ref = refs[:3]
        else:
            a_ref, b_ref, r_ref, o_ref = refs[:4]
        k = pl.program_id(2)

        def dot():
            return lax.dot_general(a_ref[...].astype(BF), b_ref[...].astype(BF), dims, preferred_element_type=F32)

        def finish(v):
            if res is not None:
                v = v + r_ref[...].astype(F32)
                rows = pl.program_id(0) * mask_tm + lax.broadcasted_iota(jnp.int32, v.shape, 0)
                v = jnp.where(rows >= PAD, v, 0.0)
            o_ref[...] = v.astype(o_ref.dtype)

        if nk == 1:
            finish(dot())
            return
        acc = refs[-1]

        @pl.when(k == 0)
        def _():
            acc[...] = dot()

        @pl.when((k > 0) & (k < nk - 1))
        def _():
            acc[...] += dot()

        @pl.when(k == nk - 1)
        def _():
            finish(acc[...] + dot())

    ins = [a, b] + ([res] if res is not None else [])
    specs = [a_spec, b_spec] + ([res_spec] if res is not None else [])
    scratch = [pltpu.VMEM(acc_shape, F32)] if nk > 1 else []
    return pl.pallas_call(body, name=name, grid=grid, in_specs=specs, out_specs=o_spec, out_shape=o_shape,
                          scratch_shapes=scratch, compiler_params=_cp(3))(*ins)


NN = (((1,), (0,)), ((), ()))
NT = (((1,), (1,)), ((), ()))
TN = (((0,), (0,)), ((), ()))


def _row_tile(lp):
    for t in (832, 640, 320, 64):
        if lp % t == 0:
            return t
    raise ValueError(lp)


def _col_tile(n):
    for t in (1024, 768, 512, 384, 256, 128):
        if n % t == 0:
            return t
    return n


def lin(name, a, w, out_dtype=F32, res=None):
    m, k = a.shape
    n = w.shape[1]
    tm, tn, tc = _row_tile(m), _col_tile(n), _col_tile(k)
    return _mm_call(name, a, w, grid=(m // tm, n // tn, k // tc),
                    a_spec=pl.BlockSpec((tm, tc), lambda i, j, kk: (i, kk)),
                    b_spec=pl.BlockSpec((tc, tn), lambda i, j, kk: (kk, j)),
                    o_shape=SDS((m, n), out_dtype), o_spec=pl.BlockSpec((tm, tn), lambda i, j, kk: (i, j)),
                    dims=NN, acc_shape=(tm, tn), res=res,
                    res_spec=pl.BlockSpec((tm, tn), lambda i, j, kk: (i, j)), mask_tm=tm)


def lin_bo(name, a, wb, out_dtype=F32):
    m, k = a.shape
    nb, _, n = wb.shape
    tm = _row_tile(m)
    return _mm_call(name, a, wb, grid=(m // tm, nb, 1),
                    a_spec=pl.BlockSpec((tm, k), lambda i, j, kk: (i, 0)),
                    b_spec=pl.BlockSpec((None, k, n), lambda i, j, kk: (j, 0, 0)),
                    o_shape=SDS((nb, m, n), out_dtype), o_spec=pl.BlockSpec((None, tm, n), lambda i, j, kk: (j, i, 0)),
                    dims=NN, acc_shape=(tm, n))


def lin_bi(name, ab, wb, out_dtype=F32, res=None):
    nb, m, k = ab.shape
    n = wb.shape[2]
    tm, tn = _row_tile(m), _col_tile(n)
    return _mm_call(name, ab, wb, grid=(m // tm, n // tn, nb),
                    a_spec=pl.BlockSpec((None, tm, k), lambda i, j, kk: (kk, i, 0)),
                    b_spec=pl.BlockSpec((None, k, tn), lambda i, j, kk: (kk, 0, j)),
                    o_shape=SDS((m, n), out_dtype), o_spec=pl.BlockSpec((tm, tn), lambda i, j, kk: (i, j)),
                    dims=NN, acc_shape=(tm, tn), res=res,
                    res_spec=pl.BlockSpec((tm, tn), lambda i, j, kk: (i, j)), mask_tm=tm)


def lin_t(name, g, w, out_dtype=F32):
    m, n = g.shape
    k = w.shape[0]
    tm, tk, tc = _row_tile(m), _col_tile(k), _col_tile(n)
    return _mm_call(name, g, w, grid=(m // tm, k // tk, n // tc),
                    a_spec=pl.BlockSpec((tm, tc), lambda i, j, kk: (i, kk)),
                    b_spec=pl.BlockSpec((tk, tc), lambda i, j, kk: (j, kk)),
                    o_shape=SDS((m, k), out_dtype), o_spec=pl.BlockSpec((tm, tk), lambda i, j, kk: (i, j)),
                    dims=NT, acc_shape=(tm, tk))


def lin_t_bi(name, gb, wb, out_dtype=F32):
    nb, m, n = gb.shape
    k = wb.shape[1]
    tm, tk = _row_tile(m), _col_tile(k)
    return _mm_call(name, gb, wb, grid=(m // tm, k // tk, nb),
                    a_spec=pl.BlockSpec((None, tm, n), lambda i, j, kk: (kk, i, 0)),
                    b_spec=pl.BlockSpec((None, tk, n), lambda i, j, kk: (kk, j, 0)),
                    o_shape=SDS((m, k), out_dtype), o_spec=pl.BlockSpec((tm, tk), lambda i, j, kk: (i, j)),
                    dims=NT, acc_shape=(tm, tk))


def lin_t_bo(name, g, wb, out_dtype=F32):
    m, n = g.shape
    nb, k, _ = wb.shape
    tm = _row_tile(m)
    return _mm_call(name, g, wb, grid=(m // tm, nb, 1),
                    a_spec=pl.BlockSpec((tm, n), lambda i, j, kk: (i, 0)),
                    b_spec=pl.BlockSpec((None, k, n), lambda i, j, kk: (j, 0, 0)),
                    o_shape=SDS((nb, m, k), out_dtype), o_spec=pl.BlockSpec((None, tm, k), lambda i, j, kk: (j, i, 0)),
                    dims=NT, acc_shape=(tm, k))


def wgrad(name, a, g):
    m, k = a.shape
    n = g.shape[1]
    tm, tn = _row_tile(m), _col_tile(n)
    return _mm_call(name, a, g, grid=(1, n // tn, m // tm),
                    a_spec=pl.BlockSpec((tm, k), lambda i, j, kk: (kk, 0)),
                    b_spec=pl.BlockSpec((tm, tn), lambda i, j, kk: (kk, j)),
                    o_shape=SDS((k, n), F32), o_spec=pl.BlockSpec((k, tn), lambda i, j, kk: (0, j)),
                    dims=TN, acc_shape=(k, tn))


def wgrad_bo(name, a, gb):
    m, k = a.shape
    nb, _, n = gb.shape
    tm = _row_tile(m)
    return _mm_call(name, a, gb, grid=(nb, 1, m // tm),
                    a_spec=pl.BlockSpec((tm, k), lambda i, j, kk: (kk, 0)),
                    b_spec=pl.BlockSpec((None, tm, n), lambda i, j, kk: (i, kk, 0)),
                    o_shape=SDS((nb, k, n), F32), o_spec=pl.BlockSpec((None, k, n), lambda i, j, kk: (i, 0, 0)),
                    dims=TN, acc_shape=(k, n))


def wgrad_bi(name, zb, g):
    nb, m, k = zb.shape
    n = g.shape[1]
    tm, tn = _row_tile(m), _col_tile(n)
    return _mm_call(name, zb, g, grid=(nb, n // tn, m // tm),
                    a_spec=pl.BlockSpec((None, tm, k), lambda i, j, kk: (i, kk, 0)),
                    b_spec=pl.BlockSpec((tm, tn), lambda i, j, kk: (kk, j)),
                    o_shape=SDS((nb, k, n), F32), o_spec=pl.BlockSpec((None, k, tn), lambda i, j, kk: (i, 0, j)),
                    dims=TN, acc_shape=(k, tn))


class Arg:
    def __init__(self, arr, block, imap, shared=False, acc=False):
        self.arr, self.block, self.imap = arr, block, imap
        self.shared = shared
        self.acc = acc

    @property
    def spec(self):
        return pl.BlockSpec(self.block, self.imap)

    def vshape(self):
        return tuple(b for b in self.block if b is not None)


def _rev(arg, nt):
    return pl.BlockSpec(arg.block, lambda o, t, _f=arg.imap: _f(o, nt - 1 - t))


def seq_fwd(name, fn, grid, params, consts, xs, outs, carries=()):
    no, nt = grid
    n_p, n_c, n_x, n_y, n_k = len(params), len(consts), len(xs), len(outs), len(carries)

    def body(*refs):
        p_refs = refs[:n_p]
        c_refs = refs[n_p:n_p + n_c]
        x_refs = refs[n_p + n_c:n_p + n_c + n_x]
        r = n_p + n_c + n_x
        y_refs = refs[r:r + n_y]
        s_refs = refs[r + n_y:r + n_y + n_k]
        k_refs = refs[r + n_y + n_k:]
        t = pl.program_id(1)

        if n_k:
            @pl.when(t == 0)
            def _():
                for k in k_refs:
                    k[...] = jnp.zeros_like(k)

        carry = tuple(k[...] for k in k_refs)
        for s, c in zip(s_refs, carry):
            s[...] = c
        new_carry, ys = fn(tuple(p[...] for p in p_refs), tuple(c[...] for c in c_refs), carry,
                           tuple(x[...] for x in x_refs))
        for k, c in zip(k_refs, new_carry):
            k[...] = c
        for y_ref, y in zip(y_refs, ys):
            y_ref[...] = y.astype(y_ref.dtype)

    out_shape = [SDS(s, d) for (s, d, _, _) in outs]
    out_specs = [pl.BlockSpec(b, im) for (_, _, b, im) in outs]
    for cs in carries:
        out_shape.append(SDS((no, nt) + cs, F32))
        out_specs.append(pl.BlockSpec((None, None) + cs, lambda o, t, _n=len(cs): (o, t) + (0,) * _n))
    res = pl.pallas_call(
        body, name=name, grid=grid, in_specs=[a.spec for a in list(params) + list(consts) + list(xs)],
        out_specs=out_specs, out_shape=out_shape, scratch_shapes=[pltpu.VMEM(cs, F32) for cs in carries],
        compiler_params=_cp(2))(*[a.arr for a in list(params) + list(consts) + list(xs)])
    return res[:n_y], res[n_y:]


def seq_bwd(name, fn, grid, params, consts, xs, dys, saved=(), carries=()):
    no, nt = grid
    n_p, n_c, n_x, n_y, n_k = len(params), len(consts), len(xs), len(dys), len(carries)

    def body(*refs):
        p_refs = refs[:n_p]
        c_refs = refs[n_p:n_p + n_c]
        x_refs = refs[n_p + n_c:n_p + n_c + n_x]
        r = n_p + n_c + n_x
        g_refs = refs[r:r + n_y]
        s_refs = refs[r + n_y:r + n_y + n_k]
        r = r + n_y + n_k
        dx_refs = refs[r:r + n_x]
        dp_refs = refs[r + n_x:r + n_x + n_p]
        k_refs = refs[r + n_x + n_p:]
        o = pl.program_id(0)
        t = pl.program_id(1)

        if n_k:
            @pl.when(t == 0)
            def _():
                for k in k_refs:
                    k[...] = jnp.zeros_like(k)

        for a, dp in zip(params, dp_refs):
            @pl.when((t == 0) & (o == 0) if a.shared else (t == 0))
            def _(dp=dp):
                dp[...] = jnp.zeros_like(dp)

        for a, dx in zip(xs, dx_refs):
            if a.acc:
                @pl.when(t == 0)
                def _(dx=dx):
                    dx[...] = jnp.zeros_like(dx)

        consts_v = tuple(c[...] for c in c_refs)

        def f(pv, cv, xv):
            return fn(pv, consts_v, cv, xv)

        pv = tuple(p[...] for p in p_refs)
        cv = tuple(s[...] for s in s_refs)
        xv = tuple(x[...] for x in x_refs)
        (new_carry, ys), vjp = jax.vjp(f, pv, cv, xv)
        cot = (tuple(k[...] for k in k_refs), tuple(g[...].astype(y.dtype) for g, y in zip(g_refs, ys)))
        dpv, dcv, dxv = vjp(cot)
        for k, c in zip(k_refs, dcv):
            k[...] = c
        for dp, v in zip(dp_refs, dpv):
            dp[...] += v
        for a, dx, v in zip(xs, dx_refs, dxv):
            if a.acc:
                dx[...] += v
            else:
                dx[...] = v.astype(dx.dtype)

    in_specs = ([_rev(a, nt) for a in list(params) + list(consts) + list(xs) + list(dys)]
                + [pl.BlockSpec((None, None) + cs, lambda o, t, _n=len(cs): (o, nt - 1 - t) + (0,) * _n) for cs in carries])
    out_shape = [SDS(a.arr.shape, F32) for a in xs] + [SDS(a.arr.shape, F32) for a in params]
    out_specs = [_rev(a, nt) for a in list(xs) + list(params)]
    res = pl.pallas_call(
        body, name=name, grid=grid, in_specs=in_specs, out_specs=out_specs, out_shape=out_shape,
        scratch_shapes=[pltpu.VMEM(cs, F32) for cs in carries], compiler_params=_cp(2))(
            *[a.arr for a in list(params) + list(consts) + list(xs) + list(dys)], *saved)
    return res[:n_x], res[n_x:]


def _rowmask(lp):
    return (jnp.arange(lp) >= PAD).astype(F32)[:, None]


def _norm_fn(p, c, k, x):
    return (), (_rms(x[0] * c[0], p[0]),)


def _norm_b_fn(p, c, k, x):
    h = x[0] * c[0]
    return (), (_rms(h, p[0]), h)


def norm_fwd(name, h, g, mask, out_dtype=BF):
    lp, d = h.shape
    tr = _row_tile(lp)
    row = lambda o, t: (t, 0)
    (a,), _ = seq_fwd(name, _norm_fn, (1, lp // tr), [Arg(g, (1, d), lambda o, t: (0, 0), shared=True)],
                      [Arg(mask, (tr, 1), row)], [Arg(h, (tr, d), row)], [((lp, d), out_dtype, (tr, d), row)])
    return a


def norm_bwd(name, h, g, mask, da, dskip):
    lp, d = h.shape
    tr = _row_tile(lp)
    row = lambda o, t: (t, 0)
    (dh,), (dg,) = seq_bwd(name, _norm_b_fn, (1, lp // tr), [Arg(g, (1, d), lambda o, t: (0, 0), shared=True)],
                           [Arg(mask, (tr, 1), row)], [Arg(h, (tr, d), row)],
                           [Arg(da, (tr, d), row), Arg(dskip, (tr, d), row)])
    return dh, dg


def _ffn_tile(lp):
    return 320 if (lp % 320 == 0 and lp > 320) else 64


def _conv_rows(ext, w, b, n):
    u2 = ext[8:8 + n]
    u1 = pltpu.roll(ext, 1, 0)[8:8 + n]
    u0 = pltpu.roll(ext, 2, 0)[8:8 + n]
    return w[2] * u2 + w[1] * u1 + w[0] * u0 + b, (u0, u1, u2)


def ffn_core_fwd(name, u, cw, cb):
    _, nj, lp, fb = u.shape
    tr = _ffn_tile(lp)
    nt = lp // tr

    def body(u_ref, up_ref, w_ref, b_ref, z_ref):
        i = pl.program_id(1)
        prev = jnp.where(i > 0, up_ref[...], 0.0)
        cs = []
        for s in range(2):
            ext = jnp.concatenate([prev[s], u_ref[s]], axis=0)
            c, _ = _conv_rows(ext, w_ref[s], b_ref[s], tr)
            cs.append(c)
        z_ref[...] = (_silu(cs[0]) * cs[1]).astype(z_ref.dtype)

    return pl.pallas_call(
        body, name=name, grid=(nj, nt),
        in_specs=[pl.BlockSpec((2, None, tr, fb), lambda j, i: (0, j, i, 0)),
                  pl.BlockSpec((2, None, 8, fb), lambda j, i: (0, j, jnp.maximum(i * (tr // 8) - 1, 0), 0)),
                  pl.BlockSpec((2, None, 3, 1, fb), lambda j, i: (0, j, 0, 0, 0)),
                  pl.BlockSpec((2, None, 1, fb), lambda j, i: (0, j, 0, 0))],
        out_specs=pl.BlockSpec((None, tr, fb), lambda j, i: (j, i, 0)),
        out_shape=SDS((nj, lp, fb), BF), compiler_params=_cp(2))(u, u, cw, cb)


def ffn_core_bwd(name, u, dz, cw, cb):
    _, nj, lp, fb = u.shape
    tr = _ffn_tile(lp)
    nt = lp // tr
    nb8 = lp // 8

    def body(u_ref, up_ref, un_ref, dz_ref, dzn_ref, w_ref, b_ref, du_ref, dw_ref, db_ref):
        i = pl.program_id(1)

        @pl.when(i == 0)
        def _():
            dw_ref[...] = jnp.zeros_like(dw_ref)
            db_ref[...] = jnp.zeros_like(db_ref)

        prev = jnp.where(i > 0, up_ref[...], 0.0)
        nxt = jnp.where(i < nt - 1, un_ref[...], 0.0)
        dz_e = jnp.concatenate([dz_ref[...], jnp.where(i < nt - 1, dzn_ref[...], 0.0)], axis=0)
        n = tr + 8
        cs, taps = [], []
        for s in range(2):
            ext = jnp.concatenate([prev[s], u_ref[s], nxt[s]], axis=0)
            c, tp = _conv_rows(ext, w_ref[s], b_ref[s], n)
            cs.append(c)
            taps.append(tp)
        sg = jax.nn.sigmoid(cs[0])
        dcs = [dz_e * cs[1] * sg * (1.0 + cs[0] * (1.0 - sg)), dz_e * cs[0] * sg]
        for s in range(2):
            dc = dcs[s]
            w = w_ref[s]
            d1 = pltpu.roll(dc, n - 1, 0)[:tr]
            d2 = pltpu.roll(dc, n - 2, 0)[:tr]
            dcm = dc[:tr]
            du_ref[s] = w[2] * dcm + w[1] * d1 + w[0] * d2
            for k in range(3):
                dw_ref[s, k] += jnp.sum(dcm * taps[s][k][:tr], axis=0, keepdims=True)
            db_ref[s] += jnp.sum(dcm, axis=0, keepdims=True)

    return pl.pallas_call(
        body, name=name, grid=(nj, nt),
        in_specs=[pl.BlockSpec((2, None, tr, fb), lambda j, i: (0, j, i, 0)),
                  pl.BlockSpec((2, None, 8, fb), lambda j, i: (0, j, jnp.maximum(i * (tr // 8) - 1, 0), 0)),
                  pl.BlockSpec((2, None, 8, fb), lambda j, i: (0, j, jnp.minimum((i + 1) * (tr // 8), nb8 - 1), 0)),
                  pl.BlockSpec((None, tr, fb), lambda j, i: (j, i, 0)),
                  pl.BlockSpec((None, 8, fb), lambda j, i: (j, jnp.minimum((i + 1) * (tr // 8), nb8 - 1), 0)),
                  pl.BlockSpec((2, None, 3, 1, fb), lambda j, i: (0, j, 0, 0, 0)),
                  pl.BlockSpec((2, None, 1, fb), lambda j, i: (0, j, 0, 0))],
        out_specs=[pl.BlockSpec((2, None, tr, fb), lambda j, i: (0, j, i, 0)),
                   pl.BlockSpec((2, None, 3, 1, fb), lambda j, i: (0, j, 0, 0, 0)),
                   pl.BlockSpec((2, None, 1, fb), lambda j, i: (0, j, 0, 0))],
        out_shape=[SDS(u.shape, F32), SDS(cw.shape, F32), SDS(cb.shape, F32)],
        compiler_params=_cp(2))(u, u, u, dz, dz, cw, cb)


def ffn_fwd(i, h, mask, w):
    a = norm_fwd(f"ffn{i}_norm", h, w["ng"], mask)
    u = lin_bo(f"ffn{i}_up", a, w["up"])
    lp = h.shape[0]
    u = u.reshape(2, 4, lp, FFN_B)
    z = ffn_core_fwd(f"ffn{i}_core", u, w["cw"], w["cb"])
    h2 = lin_bi(f"ffn{i}_down", z, w["down"], res=h)
    return h2, (h, a, u, z)


def ffn_bwd(i, dh2, mask, w, saved):
    h, a, u, z = saved
    lp = h.shape[0]
    g = {}
    g["down"] = wgrad_bi(f"ffn{i}_dwdown", z, dh2)
    dz = lin_t_bo(f"ffn{i}_dz", dh2, w["down"])
    du, g["cw"], g["cb"] = ffn_core_bwd(f"ffn{i}_core_b", u, dz, w["cw"], w["cb"])
    du = du.reshape(8, lp, FFN_B)
    g["up"] = wgrad_bo(f"ffn{i}_dwup", a, du)
    da = lin_t_bi(f"ffn{i}_da", du, w["up"])
    dh, g["ng"] = norm_bwd(f"ffn{i}_norm_b", h, w["ng"], mask, da, dh2)
    return dh, g


HG_HB = 4


def _hgrn_fn(p, c, k, x):
    lb, go = p
    q, f, iv, g = x
    (st_all,) = k
    qs = _silu(q)
    forget = lb + (1.0 - lb) * jax.nn.sigmoid(f)
    logf = jnp.log(forget)
    kk = 1.0 - forget
    gc_all = _seg_cumsum(logf, HG_C)
    r = lax.broadcasted_iota(jnp.int32, (HG_C, HG_C), 0)
    cc = lax.broadcasted_iota(jnp.int32, (HG_C, HG_C), 1)
    ns = CH // HG_C
    cells = [(j, s) for j in range(HG_HB) for s in range(ns)]

    def blk(t, j, s):
        return t[HG_C * s:HG_C * (s + 1), HG_D * j:HG_D * (j + 1)]

    gl = {c: jnp.sum(blk(logf, *c), axis=0, keepdims=True) for c in cells}
    qd = {c: blk(qs, *c) * jnp.exp(blk(gc_all, *c)) for c in cells}
    ki = {c: blk(kk, *c) * jnp.exp(-blk(gc_all, *c)) for c in cells}
    up = {c: mm_tn(blk(iv, *c), blk(kk, *c) * jnp.exp(gl[c] - blk(gc_all, *c))) for c in cells}
    st, sts = {}, []
    for j in range(HG_HB):
        cur = st_all[j]
        for s in range(ns):
            st[(j, s)] = cur
            cur = cur * jnp.exp(gl[(j, s)]) + up[(j, s)]
        sts.append(cur)
    both = {c: mm_nt(qd[c], jnp.concatenate([st[c], ki[c]], axis=0)) for c in cells}
    oc = {c: mm(jnp.where(r >= cc, both[c][:, HG_D:], 0.0), blk(iv, *c)) + both[c][:, :HG_D] for c in cells}
    zs = []
    for j in range(HG_HB):
        o = jnp.concatenate([oc[(j, s)] for s in range(ns)], axis=0)
        zs.append(_rms(o, go) * _silu(g[:, HG_D * j:HG_D * (j + 1)]))
    return (jnp.stack(sts, axis=0),), (jnp.concatenate(zs, axis=1),)


def _hgrn_args(u4, lb, go):
    lp = u4[0].shape[1]
    wb = HG_HB * HG_D
    xs = [Arg(t, (None, CH, wb), lambda o, t: (o, t, 0)) for t in u4]
    ps = [Arg(lb, (1, wb), lambda o, t: (0, o)), Arg(go, (1, HG_D), lambda o, t: (0, 0), shared=True)]
    return (HG_H // HG_HB, lp // CH), ps, xs


def hgrn_fwd(h, mask, w):
    lp = h.shape[0]
    a = norm_fwd("hgrn_norm", h, w["ng"], mask)
    u = lin_bo("hgrn_in", a, w["win"])
    u4 = [u[2 * s:2 * s + 2] for s in range(4)]
    grid, ps, xs = _hgrn_args(u4, w["lb"], w["go"])
    (z,), (st,) = seq_fwd("hgrn_core", _hgrn_fn, grid, ps, [], xs,
                          [((lp, D), BF, (CH, HG_HB * HG_D), lambda o, t: (t, o))], carries=[(HG_HB, HG_D, HG_D)])
    h2 = lin("hgrn_out", z, w["wo"], res=h)
    return h2, (h, a, u4, z, st)


def hgrn_bwd(dh2, mask, w, saved):
    h, a, u4, z, st = saved
    lp = h.shape[0]
    g = {}
    g["wo"] = wgrad("hgrn_dwo", z, dh2)
    dz = lin_t("hgrn_dz", dh2, w["wo"])
    grid, ps, xs = _hgrn_args(u4, w["lb"], w["go"])
    du4, (g["lb"], g["go"]) = seq_bwd("hgrn_core_b", _hgrn_fn, grid, ps, [], xs,
                                      [Arg(dz, (CH, HG_HB * HG_D), lambda o, t: (t, o))], saved=[st],
                                      carries=[(HG_HB, HG_D, HG_D)])
    du = jnp.concatenate(du4, axis=0)
    g["win"] = wgrad_bo("hgrn_dwin", a, du)
    da = lin_t_bi("hgrn_da", du, w["win"])
    dh, g["ng"] = norm_bwd("hgrn_norm_b", h, w["ng"], mask, da, dh2)
    return dh, g


S5_W = S5_SG * S5_P


def s5_tables(lam_re, lam_im, log_dt, b_re, b_im, c_re, c_im):
    dt = jnp.exp(log_dt)[:, None]
    mag = jnp.exp(lam_re * dt)
    abar_re = mag * jnp.cos(lam_im * dt)
    abar_im = mag * jnp.sin(lam_im * dt)
    den = lam_re * lam_re + lam_im * lam_im
    zoh_re = ((abar_re - 1.0) * lam_re + abar_im * lam_im) / den
    zoh_im = (abar_im * lam_re - (abar_re - 1.0) * lam_im) / den
    bbar_re = zoh_re[..., None] * b_re - zoh_im[..., None] * b_im
    bbar_im = zoh_re[..., None] * b_im + zoh_im[..., None] * b_re
    eye = jnp.eye(S5_SG, dtype=F32)

    def blockdiag_in(b):
        t = b.reshape(N_DEV, S5_SG, S5_P, S5_K).transpose(0, 1, 3, 2)
        return jnp.einsum("jakp,ab->jakbp", t, eye).reshape(N_DEV, S5_SG * S5_K, S5_W)

    def blockdiag_out(c):
        t = c.reshape(N_DEV, S5_SG, S5_K, S5_P).transpose(0, 1, 3, 2)
        return jnp.einsum("japk,ab->japbk", t, eye).reshape(N_DEV, S5_W, S5_SG * S5_K)

    wb = jnp.concatenate([blockdiag_in(bbar_re), blockdiag_in(bbar_im)], axis=2)
    wc = jnp.concatenate([blockdiag_out(c_re), -blockdiag_out(c_im)], axis=1)

    def powers(n):
        steps = n[:, None, None] * dt[None]
        pm = jnp.exp(lam_re[None] * steps)
        pr = (pm * jnp.cos(lam_im[None] * steps)).reshape(-1, N_DEV, S5_W).transpose(1, 0, 2)
        pi = (pm * jnp.sin(lam_im[None] * steps)).reshape(-1, N_DEV, S5_W).transpose(1, 0, 2)
        return jnp.concatenate([pr, pi], axis=2)

    apow = powers(2.0 ** jnp.arange(6, dtype=F32))[:, :, None, :]
    ptab = powers(jnp.arange(CH, dtype=F32) + 1.0)
    return wb, wc, apow, ptab


def _cmul(ar, ai, xr, xi):
    return ar * xr - ai * xi, ar * xi + ai * xr


def _s5_fn(p, c, k, x):
    wb, wc, apow, ptab, dsk = p
    (a,) = x
    (x0,) = k
    bu = mm(a, wb)
    xr, xi = bu[:, :S5_W], bu[:, S5_W:]
    for s in range(6):
        asr, asi = apow[s][:, :S5_W], apow[s][:, S5_W:]
        dr, di = _cmul(asr, asi, shift_rows(xr, 1 << s), shift_rows(xi, 1 << s))
        xr, xi = xr + dr, xi + di
    dr, di = _cmul(ptab[:, :S5_W], ptab[:, S5_W:], x0[:, :S5_W], x0[:, S5_W:])
    xr, xi = xr + dr, xi + di
    xx = jnp.concatenate([xr, xi], axis=1)
    last = lax.broadcasted_iota(jnp.int32, xx.shape, 0) == CH - 1
    x0n = jnp.sum(jnp.where(last, xx, 0.0), axis=0, keepdims=True)
    y = mm(xx, wc)
    return (x0n,), (jax.nn.gelu(y + dsk * a),)


def _s5_args(a, tb, dsk):
    lp = a.shape[0]
    wb, wc, apow, ptab = tb
    ps = [Arg(wb, (None, 128, 2 * S5_W), lambda o, t: (o, 0, 0)), Arg(wc, (None, 2 * S5_W, 128), lambda o, t: (o, 0, 0)),
          Arg(apow, (None, 6, 1, 2 * S5_W), lambda o, t: (o, 0, 0, 0)), Arg(ptab, (None, CH, 2 * S5_W), lambda o, t: (o, 0, 0)),
          Arg(dsk, (1, 128), lambda o, t: (0, o))]
    xs = [Arg(a, (CH, 128), lambda o, t: (t, o))]
    return (N_DEV, lp // CH), ps, xs


def _glu_res_fn(p, c, k, x):
    h, vg = x
    return (), ((h + vg[:, :D] * jax.nn.sigmoid(vg[:, D:])) * c[0],)


def _glu_args(h, vg, mask):
    lp = h.shape[0]
    tr = _row_tile(lp)
    row = lambda o, t: (t, 0)
    return (1, lp // tr), [Arg(mask, (tr, 1), row)], [Arg(h, (tr, D), row), Arg(vg, (tr, 2 * D), row)], tr


def s5_fwd(h, mask, w):
    lp = h.shape[0]
    a = norm_fwd("s5_norm", h, w["ng"], mask, out_dtype=F32)
    grid, ps, xs = _s5_args(a, w["tb"], w["dsk"])
    (z,), (st,) = seq_fwd("s5_core", _s5_fn, grid, ps, [], xs,
                          [((lp, D), BF, (CH, 128), lambda o, t: (t, o))], carries=[(1, 2 * S5_W)])
    vg = lin("s5_glu", z, w["wglu"])
    grid2, cs, xs2, tr = _glu_args(h, vg, mask)
    (h2,), _ = seq_fwd("s5_res", _glu_res_fn, grid2, [], cs, xs2, [((lp, D), F32, (tr, D), lambda o, t: (t, 0))])
    return h2, (h, a, z, vg, st)


def s5_bwd(dh2, mask, w, saved):
    h, a, z, vg, st = saved
    g = {}
    grid2, cs, xs2, tr = _glu_args(h, vg, mask)
    (dskip, dvg), _ = seq_bwd("s5_res_b", _glu_res_fn, grid2, [], cs, xs2, [Arg(dh2, (tr, D), lambda o, t: (t, 0))])
    g["wglu"] = wgrad("s5_dwglu", z, dvg)
    dz = lin_t("s5_dz", dvg, w["wglu"])
    grid, ps, xs = _s5_args(a, w["tb"], w["dsk"])
    (da,), dps = seq_bwd("s5_core_b", _s5_fn, grid, ps, [], xs, [Arg(dz, (CH, 128), lambda o, t: (t, o))],
                         saved=[st], carries=[(1, 2 * S5_W)])
    g["tb"] = tuple(dps[:4])
    g["dsk"] = dps[4]
    dh, g["ng"] = norm_bwd("s5_norm_b", h, w["ng"], mask, da, dskip)
    return dh, g


def ret_consts(lp):
    pos = jnp.maximum(jnp.arange(lp, dtype=F32) - PAD, 0.0)
    inv = 1.0 / (ROPE_BASE ** (jnp.arange(0, RET_DK, 2, dtype=F32) / RET_DK))
    ang = pos[:, None] * inv[None, :]
    lg = jnp.log(1.0 - jnp.exp2(-5.0 - jnp.arange(RET_H, dtype=F32)))
    p = jnp.arange(CH, dtype=F32)
    diff = p[:, None] - p[None, :]
    decay = jnp.where(diff >= 0, jnp.exp(diff[None] * lg[:, None, None]), 0.0)
    qd = jnp.exp((p[None, :] + 1.0) * lg[:, None])[..., None]
    kd = jnp.exp((CH - 1.0 - p[None, :]) * lg[:, None])[..., None]
    cd = jnp.exp(CH * lg)[:, None, None]
    return jnp.cos(ang), jnp.sin(ang), decay, qd, kd, cd


def _ret_fn(p, c, k, x):
    (gn,) = p
    cos, sin, decay, qd, kd, cd = c
    (st,) = k
    q, kk, v, g = x
    hd = RET_DK // 2

    def rope(t):
        t1, t2 = t[:, :hd], t[:, hd:]
        return jnp.concatenate([t1 * cos - t2 * sin, t1 * sin + t2 * cos], axis=1)

    qr = rope(q)
    kr = rope(kk) * (RET_DK ** -0.5)
    scores = mm_nt(qr, kr)
    inter = mm(qr * qd, st)
    st = st * cd + mm_tn(kr * kd, v)
    o = mm(scores * decay, v) + inter
    mu = jnp.mean(o, axis=-1, keepdims=True)
    var = jnp.mean(jnp.square(o - mu), axis=-1, keepdims=True)
    return (st,), ((o - mu) * lax.rsqrt(var + EPS) * gn * _silu(g),)


def _ret_args(u4, gn, rc):
    lp = u4[0].shape[0]
    cos, sin, decay, qd, kd, cd = rc
    ps = [Arg(gn, (1, RET_DV), lambda o, t: (0, o))]
    cs = [Arg(cos, (CH, RET_DK // 2), lambda o, t: (t, 0)), Arg(sin, (CH, RET_DK // 2), lambda o, t: (t, 0)),
          Arg(decay, (None, CH, CH), lambda o, t: (o, 0, 0)), Arg(qd, (None, CH, 1), lambda o, t: (o, 0, 0)),
          Arg(kd, (None, CH, 1), lambda o, t: (o, 0, 0)), Arg(cd, (None, 1, 1), lambda o, t: (o, 0, 0))]
    xs = [Arg(u4[0], (CH, RET_DK), lambda o, t: (t, o)), Arg(u4[1], (CH, RET_DK), lambda o, t: (t, o)),
          Arg(u4[2], (CH, RET_DV), lambda o, t: (t, o)), Arg(u4[3], (CH, RET_DV), lambda o, t: (t, o))]
    return (RET_H, lp // CH), ps, cs, xs


def ret_fwd(h, mask, w):
    lp = h.shape[0]
    a = norm_fwd("ret_norm", h, w["ng"], mask)
    u = lin("ret_in", a, w["win"])
    u4 = [u[:, :D], u[:, D:2 * D], u[:, 2 * D:4 * D], u[:, 4 * D:]]
    grid, ps, cs, xs = _ret_args(u4, w["gn"], w["rc"])
    (z,), (st,) = seq_fwd("ret_core", _ret_fn, grid, ps, cs, xs,
                          [((lp, 2 * D), BF, (CH, RET_DV), lambda o, t: (t, o))], carries=[(RET_DK, RET_DV)])
    h2 = lin("ret_out", z, w["wo"], res=h)
    return h2, (h, a, u4, z, st)


def ret_bwd(dh2, mask, w, saved):
    h, a, u4, z, st = saved
    g = {}
    g["wo"] = wgrad("ret_dwo", z, dh2)
    dz = lin_t("ret_dz", dh2, w["wo"])
    grid, ps, cs, xs = _ret_args(u4, w["gn"], w["rc"])
    du4, (g["gn"],) = seq_bwd("ret_core_b", _ret_fn, grid, ps, cs, xs,
                              [Arg(dz, (CH, RET_DV), lambda o, t: (t, o))], saved=[st], carries=[(RET_DK, RET_DV)])
    du = jnp.concatenate(du4, axis=1)
    g["win"] = wgrad("ret_dwin", a, du)
    da = lin_t("ret_da", du, w["win"])
    dh, g["ng"] = norm_bwd("ret_norm_b", h, w["ng"], mask, da, dh2)
    return dh, g


def mla_consts(lp):
    pos = jnp.maximum(jnp.arange(lp, dtype=F32) - PAD, 0.0)
    inv = 1.0 / (ROPE_BASE ** (jnp.arange(0, MLA_ROPE, 2, dtype=F32) / MLA_ROPE))
    ang = pos[:, None] * inv[None, :]
    cos = jnp.concatenate([jnp.cos(ang), jnp.cos(ang)], axis=1)
    sin = jnp.concatenate([jnp.sin(ang), jnp.sin(ang)], axis=1)
    hd = MLA_ROPE // 2
    r = lax.broadcasted_iota(jnp.int32, (MLA_ROPE, MLA_ROPE), 0)
    c = lax.broadcasted_iota(jnp.int32, (MLA_ROPE, MLA_ROPE), 1)
    rot = jnp.where(r == c + hd, -1.0, jnp.where(c == r + hd, 1.0, 0.0)).astype(F32)
    return cos, sin, rot


def _mla_prep1_fn(p, c, k, x):
    gq, gkv = p
    (down,) = x
    return (), (_rms(down[:, :MLA_QL], gq), _rms(down[:, MLA_QL:MLA_QL + MLA_KVL], gkv), down[:, MLA_QL + MLA_KVL:])


def _mla_prep2(p, c, x):
    gq, gk = p
    cos, sin, rot = c
    q, kv, kpe = x
    qn = _rms(q, gq)
    qn_n, qn_r = qn[:, :MLA_NOPE], qn[:, MLA_NOPE:]
    qo = jnp.concatenate([qn_n, qn_r * cos + cright(qn_r, rot) * sin], axis=1)
    kn = kv[:, :MLA_NOPE]
    ms = (jnp.sum(kn * kn, axis=-1, keepdims=True) + jnp.sum(kpe * kpe, axis=-1, keepdims=True)) / MLA_QK
    r = lax.rsqrt(ms + EPS)
    kr = kpe * r * gk[:, MLA_NOPE:]
    ko = jnp.concatenate([kn * r * gk[:, :MLA_NOPE], kr * cos + cright(kr, rot) * sin], axis=1)
    return qo, ko, kv[:, MLA_NOPE:]


def _mla_prep2_fn(p, c, k, x):
    return (), _mla_prep2(p, c, x)[:2]


def _mla_prep2_b_fn(p, c, k, x):
    return (), _mla_prep2(p, c, x)


def _prep1_args(down, gq, gkv):
    lp = down.shape[0]
    tr = _row_tile(lp)
    ps = [Arg(gq, (1, MLA_QL), lambda o, t: (0, 0), shared=True), Arg(gkv, (1, MLA_KVL), lambda o, t: (0, 0), shared=True)]
    return (1, lp // tr), ps, [Arg(down, (tr, down.shape[1]), lambda o, t: (t, 0))], tr


def _prep2_args(qraw, kvraw, kpe, gq, gk, mc):
    lp = kpe.shape[0]
    tr = _row_tile(lp)
    cos, sin, rot = mc
    ps = [Arg(gq, (1, MLA_QK), lambda o, t: (0, 0), shared=True), Arg(gk, (1, MLA_QK), lambda o, t: (0, 0), shared=True)]
    cs = [Arg(cos, (tr, MLA_ROPE), lambda o, t: (o, 0)), Arg(sin, (tr, MLA_ROPE), lambda o, t: (o, 0)),
          Arg(rot, (MLA_ROPE, MLA_ROPE), lambda o, t: (0, 0))]
    xs = [Arg(qraw, (None, tr, MLA_QK), lambda o, t: (t, o, 0)), Arg(kvraw, (None, tr, MLA_NOPE + MLA_V), lambda o, t: (t, o, 0)),
          Arg(kpe, (tr, MLA_ROPE), lambda o, t: (o, 0), acc=True)]
    return (lp // tr, MLA_H), ps, cs, xs, tr


def _attn_tile(lp):
    return 832 if (lp % 832 == 0 and lp > 832) else 64


def _attn_mask(qi, ki, ta):
    rows = qi * ta + lax.broadcasted_iota(jnp.int32, (ta, ta), 0)
    cols = ki * ta + lax.broadcasted_iota(jnp.int32, (ta, ta), 1)
    return (cols >= PAD) & ((cols // CH) <= (rows // CH))


def attn_fwd(q, k, kv):
    nh, lp, dq = q.shape
    ta = _attn_tile(lp)
    nb = lp // ta
    scale = MLA_QK ** -0.5

    def body(q_ref, k_ref, v_ref, o_ref, lse_ref, m_s, l_s, acc_s):
        qi, ki = pl.program_id(1), pl.program_id(2)

        @pl.when(ki == 0)
        def _():
            m_s[...] = jnp.full_like(m_s, NEG)
            l_s[...] = jnp.zeros_like(l_s)
            acc_s[...] = jnp.zeros_like(acc_s)

        def step(masked):
            s = _bdot(q_ref[...], k_ref[...], 1, 1) * scale
            if masked:
                s = jnp.where(_attn_mask(qi, ki, ta), s, NEG)
            m_new = jnp.maximum(m_s[...], jnp.max(s, axis=-1, keepdims=True))
            p = jnp.exp(s - m_new)
            alpha = jnp.exp(m_s[...] - m_new)
            l_s[...] = alpha * l_s[...] + jnp.sum(p, axis=-1, keepdims=True)
            acc_s[...] = alpha * acc_s[...] + _bdot(p, v_ref[...], 1, 0)
            m_s[...] = m_new

        pl.when((ki == qi) | (ki == 0))(functools.partial(step, True))
        pl.when((ki < qi) & (ki > 0))(functools.partial(step, False))

        @pl.when(ki == nb - 1)
        def _():
            o_ref[...] = (acc_s[...] / l_s[...]).astype(o_ref.dtype)
            lse_ref[...] = m_s[...] + jnp.log(l_s[...])

    return pl.pallas_call(
        body, name="mla_attn", grid=(nh, nb, nb),
        in_specs=[pl.BlockSpec((None, ta, dq), lambda h, qi, ki: (h, qi, 0)),
                  pl.BlockSpec((None, ta, dq), lambda h, qi, ki: (h, jnp.minimum(ki, qi), 0)),
                  pl.BlockSpec((None, ta, MLA_V), lambda h, qi, ki: (h, jnp.minimum(ki, qi), 1))],
        out_specs=[pl.BlockSpec((ta, MLA_V), lambda h, qi, ki: (qi, h)),
                   pl.BlockSpec((None, ta, 1), lambda h, qi, ki: (h, qi, 0))],
        out_shape=[SDS((lp, nh * MLA_V), BF), SDS((nh, lp, 1), F32)],
        scratch_shapes=[pltpu.VMEM((ta, 1), F32), pltpu.VMEM((ta, 1), F32), pltpu.VMEM((ta, MLA_V), F32)],
        compiler_params=_cp(3))(q, k, kv)


def attn_bwd(q, k, kv, o, do, lse):
    nh, lp, dq = q.shape
    ta = _attn_tile(lp)
    nb = lp // ta
    scale = MLA_QK ** -0.5

    def body(q_ref, k_ref, v_ref, o_ref, do_ref, lse_ref, dq_ref, dk_ref, dv_ref, dk_s, dv_s):
        ki, qi = pl.program_id(1), pl.program_id(2)

        @pl.when((ki == 0) & (qi == 0))
        def _():
            dq_ref[...] = jnp.zeros_like(dq_ref)

        @pl.when(qi == 0)
        def _():
            dk_s[...] = jnp.zeros_like(dk_s)
            dv_s[...] = jnp.zeros_like(dv_s)

        def step(masked):
            dov = do_ref[...]
            s = _bdot(q_ref[...], k_ref[...], 1, 1) * scale
            dp = _bdot(dov, v_ref[...], 1, 1)
            if masked:
                s = jnp.where(_attn_mask(qi, ki, ta), s, NEG)
            p = jnp.exp(s - lse_ref[...])
            delta = jnp.sum(dov * o_ref[...].astype(F32), axis=-1, keepdims=True)
            dv_s[...] += _bdot(p, dov, 0, 0)
            ds = p * (dp - delta) * scale
            rows = pl.ds(pl.multiple_of(qi * ta, ta), ta)
            dq_ref[rows, :] += _bdot(ds, k_ref[...], 1, 0)
            dk_s[...] += _bdot(ds, q_ref[...], 0, 0)

        pl.when((ki == qi) | (ki == 0))(functools.partial(step, True))
        pl.when((ki < qi) & (ki > 0))(functools.partial(step, False))

        @pl.when(qi == nb - 1)
        def _():
            dk_ref[...] = dk_s[...]
            dv_ref[...] = dv_s[...]

    qmap = lambda h, ki, qi: (h, jnp.maximum(qi, ki), 0)
    return pl.pallas_call(
        body, name="mla_attn_b", grid=(nh, nb, nb),
        in_specs=[pl.BlockSpec((None, ta, dq), qmap),
                  pl.BlockSpec((None, ta, dq), lambda h, ki, qi: (h, ki, 0)),
                  pl.BlockSpec((None, ta, MLA_V), lambda h, ki, qi: (h, ki, 1)),
                  pl.BlockSpec((ta, MLA_V), lambda h, ki, qi: (jnp.maximum(qi, ki), h)),
                  pl.BlockSpec((ta, MLA_V), lambda h, ki, qi: (jnp.maximum(qi, ki), h)),
                  pl.BlockSpec((None, ta, 1), qmap)],
        out_specs=[pl.BlockSpec((None, lp, dq), lambda h, ki, qi: (h, 0, 0)),
                   pl.BlockSpec((None, ta, dq), lambda h, ki, qi: (h, ki, 0)),
                   pl.BlockSpec((None, ta, MLA_V), lambda h, ki, qi: (h, ki, 0))],
        out_shape=[SDS((nh, lp, dq), F32), SDS((nh, lp, dq), F32), SDS((nh, lp, MLA_V), F32)],
        scratch_shapes=[pltpu.VMEM((ta, dq), F32), pltpu.VMEM((ta, MLA_V), F32)],
        compiler_params=_cp(3))(q, k, kv, o, do, lse)


def mla_fwd(h, mask, w):
    lp = h.shape[0]
    a = norm_fwd("mla_norm", h, w["ng"], mask)
    down = lin("mla_down", a, w["wdown"])
    grid, ps, xs, tr = _prep1_args(down, w["gcq"], w["gckv"])
    row = lambda o, t: (t, 0)
    (cq, ckv, kpe), _ = seq_fwd("mla_prep1", _mla_prep1_fn, grid, ps, [], xs,
                                [((lp, MLA_QL), BF, (tr, MLA_QL), row), ((lp, MLA_KVL), BF, (tr, MLA_KVL), row),
                                 ((lp, MLA_ROPE), F32, (tr, MLA_ROPE), row)])
    qraw = lin_bo("mla_uq", cq, w["wuq"])
    kvraw = lin_bo("mla_ukv", ckv, w["wukv"])
    grid, ps, cs, xs, tr = _prep2_args(qraw, kvraw, kpe, w["gq"], w["gk"], w["mc"])
    hm = lambda o, t: (t, o, 0)
    (q, k), _ = seq_fwd("mla_prep2", _mla_prep2_fn, grid, ps, cs, xs,
                        [((MLA_H, lp, MLA_QK), BF, (None, tr, MLA_QK), hm), ((MLA_H, lp, MLA_QK), BF, (None, tr, MLA_QK), hm)])
    o, lse = attn_fwd(q, k, kvraw)
    h2 = lin("mla_out", o, w["wo"], res=h)
    return h2, (h, a, down, cq, ckv, kpe, qraw, kvraw, q, k, o, lse)


def mla_bwd(dh2, mask, w, saved):
    h, a, down, cq, ckv, kpe, qraw, kvraw, q, k, o, lse = saved
    lp = h.shape[0]
    g = {}
    g["wo"] = wgrad("mla_dwo", o, dh2)
    do = lin_t("mla_do", dh2, w["wo"])
    dq, dk, dv = attn_bwd(q, k, kvraw, o, do, lse)
    grid, ps, cs, xs, tr = _prep2_args(qraw, kvraw, kpe, w["gq"], w["gk"], w["mc"])
    hm = lambda o, t: (t, o, 0)
    (dqraw, dkvraw, dkpe), (g["gq"], g["gk"]) = seq_bwd(
        "mla_prep2_b", _mla_prep2_b_fn, grid, ps, cs, xs,
        [Arg(dq, (None, tr, MLA_QK), hm), Arg(dk, (None, tr, MLA_QK), hm), Arg(dv, (None, tr, MLA_V), hm)])
    g["wuq"] = wgrad_bo("mla_dwuq", cq, dqraw)
    dcq = lin_t_bi("mla_dcq", dqraw, w["wuq"])
    g["wukv"] = wgrad_bo("mla_dwukv", ckv, dkvraw)
    dckv = lin_t_bi("mla_dckv", dkvraw, w["wukv"])
    grid, ps, xs, tr = _prep1_args(down, w["gcq"], w["gckv"])
    row = lambda o, t: (t, 0)
    (ddown,), (g["gcq"], g["gckv"]) = seq_bwd(
        "mla_prep1_b", _mla_prep1_fn, grid, ps, [], xs,
        [Arg(dcq, (tr, MLA_QL), row), Arg(dckv, (tr, MLA_KVL), row), Arg(dkpe, (tr, MLA_ROPE), row)])
    g["wdown"] = wgrad("mla_dwdown", a, ddown)
    da = lin_t("mla_da", ddown, w["wdown"])
    dh, g["ng"] = norm_bwd("mla_norm_b", h, w["ng"], mask, da, dh2)
    return dh, g


def loss_head(h, target):
    lp, d = h.shape
    assert OFF == CH

    def body(h_ref, t_ref, loss_ref, dh_ref):
        i = pl.program_id(0)

        @pl.when(i == 0)
        def _():
            loss_ref[...] = jnp.zeros_like(loss_ref)

        e = jnp.where(i > 0, h_ref[...] - t_ref[...], 0.0)
        loss_ref[...] += jnp.sum(e * e) * (0.5 / d)
        dh_ref[...] = e * (1.0 / d)

    return pl.pallas_call(
        body, name="loss_head", grid=(lp // CH,),
        in_specs=[pl.BlockSpec((CH, d), lambda i: (i, 0)), pl.BlockSpec((CH, d), lambda i: (jnp.maximum(i - 1, 0), 0))],
        out_specs=[pl.BlockSpec((8, 128), lambda i: (0, 0)), pl.BlockSpec((CH, d), lambda i: (i, 0))],
        out_shape=[SDS((8, 128), F32), SDS((lp, d), F32)], compiler_params=_cp(1))(h, target)


ADAM_LAND_BYTES = 20 * 1024 * 1024


def _adam_tile(r, c, nl):
    if r % 8:
        return r
    best = 8
    for t in range(8, r + 1, 8):
        if r % t == 0 and N_DEV * t * c * 4 * 2 * nl <= ADAM_LAND_BYTES:
            best = t
    return best


def adamw(name, lands, w, m, v):
    nl, r, c = w.shape
    tr = _adam_tile(r, c, nl)
    c1 = 1.0 / (1.0 - ADAM_B1 ** ADAM_STEP)
    c2 = 1.0 / (1.0 - ADAM_B2 ** ADAM_STEP)

    def body(*refs):
        l_refs = refs[:nl]
        w_ref, m_ref, v_ref, g_ref, d_ref, nm_ref, nv_ref = refs[nl:]
        layer = pl.program_id(0)
        for j in range(nl):
            @pl.when(layer == j)
            def _(j=j):
                g = l_refs[j][0]
                for i in range(1, N_DEV):
                    g = g + l_refs[j][i]
                g_ref[...] = g

        g = g_ref[...]
        nm = ADAM_B1 * m_ref[...] + (1.0 - ADAM_B1) * g
        nv = ADAM_B2 * v_ref[...] + (1.0 - ADAM_B2) * (g * g)
        nm_ref[...] = nm
        nv_ref[...] = nv
        d_ref[...] = -ADAM_LR * ((nm * c1) / (jnp.sqrt(nv * c2) + ADAM_EPS) + ADAM_WD * w_ref[...])

    blk = pl.BlockSpec((None, tr, c), lambda l, i: (l, i, 0))
    land_specs = [pl.BlockSpec((N_DEV, tr, c), lambda l, i, j=j: (0, jnp.where(l == j, i, 0), 0)) for j in range(nl)]
    return pl.pallas_call(
        body, name=name, grid=(nl, r // tr), in_specs=land_specs + [blk, blk, blk],
        out_specs=[blk, blk, blk, blk], out_shape=[SDS((nl, r, c), F32)] * 4, compiler_params=_cp(2))(*lands, w, m, v)


ANY = pl.BlockSpec(memory_space=pl.ANY)
MESH = pl.DeviceIdType.MESH


def _me():
    return lax.axis_index("x"), lax.axis_index("y"), lax.axis_index("c")


def _peers():
    x, y, c = _me()
    out = []
    for k in range(1, N_DEV):
        px = 1 - x if k & 4 else x
        py = 1 - y if k & 2 else y
        pc = 1 - c if k & 1 else c
        out.append(((px, py, pc), 4 * px + 2 * py + pc))
    return out


HBM_SPEC = pl.BlockSpec(memory_space=pltpu.HBM)
SEM_SPEC = pl.BlockSpec(memory_space=pltpu.SEMAPHORE)
DATAFLOW = pltpu.SideEffectType.DATAFLOW_SIDE_EFFECTING


def _my_index():
    return 4 * lax.axis_index("x") + 2 * lax.axis_index("y") + lax.axis_index("c")


def _hbm(a):
    return pltpu.with_memory_space_constraint(a, pltpu.HBM)


NP = N_DEV - 1


def _push_copy(x_ref, land_ref, send, recv, pid, src_idx, dst_idx, scatter):
    src = x_ref.at[src_idx] if scatter else x_ref
    return pltpu.make_async_remote_copy(src_ref=src, dst_ref=land_ref.at[dst_idx], send_sem=send, recv_sem=recv,
                                        device_id=pid, device_id_type=MESH)


def push_start(name, xs, me, scatter):
    n = len(xs)
    lands = []
    for a in xs:
        own = lax.dynamic_index_in_dim(a, me, 0, keepdims=True) if scatter else a[None]
        z = lax.empty((N_DEV,) + own.shape[1:], a.dtype)
        lands.append(lax.dynamic_update_slice(z, own, (me,) + (0,) * (own.ndim - 1)))
    ns = 2 * NP * n

    def body(*refs):
        x_refs, land_refs = refs[:n], refs[n:2 * n]
        sems = refs[2 * n:2 * n + ns]
        token = refs[-1]
        x, y, c = _me()
        mine = 4 * x + 2 * y + c
        for i in range(n):
            for k, (pid, pidx) in enumerate(_peers()):
                s = 2 * (NP * i + k)
                _push_copy(x_refs[i], land_refs[i], sems[s], sems[s + 1], pid, pidx, mine, scatter).start()
        token[...] = jnp.zeros_like(token)

    out_shape = ([pltpu.SemaphoreType.DMA(())] * ns + [pltpu.HBM(a.shape, a.dtype) for a in xs + lands]
                 + [SDS((8, 128), F32)])
    res = pl.pallas_call(
        body, name=name, out_shape=out_shape, in_specs=[HBM_SPEC] * (2 * n),
        out_specs=[SEM_SPEC] * ns + [HBM_SPEC] * (2 * n) + [pl.BlockSpec(memory_space=pltpu.VMEM)],
        input_output_aliases={i: ns + i for i in range(2 * n)},
        compiler_params=pltpu.CompilerParams(has_side_effects=DATAFLOW))(*[_hbm(a) for a in xs + lands])
    sems, thru, token = res[:ns], res[ns:-1], res[-1]
    return [dict(x=thru[i], land=thru[n + i], sems=list(sems[2 * NP * i:2 * NP * (i + 1)]), token=token)
            for i in range(n)]


def push_wait(name, hds, after, scatter):
    n = len(hds)
    ns = 2 * NP

    def body(*refs):
        x_refs, land_refs = refs[:n], refs[n:2 * n]
        sems = refs[2 * n:2 * n + ns * n]
        for i in range(n):
            for k, (pid, pidx) in enumerate(_peers()):
                cp = _push_copy(x_refs[i], land_refs[i], sems[ns * i + 2 * k], sems[ns * i + 2 * k + 1], pid, pidx, pidx,
                                scatter)
                cp.wait_send()
                cp.wait_recv()

    arrs = [hd["x"] for hd in hds] + [hd["land"] for hd in hds]
    sems = [s for hd in hds for s in hd["sems"]]
    res = pl.pallas_call(
        body, name=name, out_shape=[pltpu.HBM(a.shape, a.dtype) for a in arrs],
        in_specs=[HBM_SPEC] * (2 * n) + [SEM_SPEC] * (ns * n) + [ANY], out_specs=[HBM_SPEC] * (2 * n),
        input_output_aliases={i: i for i in range(2 * n)},
        compiler_params=pltpu.CompilerParams(has_side_effects=DATAFLOW))(*arrs, *sems, after)
    return list(res[n:])


WEIGHTS = ['meta_tokens', 'norm_mix_g', 'norm_ffn_g', 'mla_w_down', 'mla_cq_norm_g', 'mla_ckv_norm_g', 'mla_w_uq',
           'mla_w_ukv', 'mla_q_head_g', 'mla_k_head_g', 'mla_w_o', 'hgrn_w_in', 'hgrn_lb_logits', 'hgrn_o_norm_g',
           'hgrn_w_o', 's5_lam_re', 's5_lam_im', 's5_log_dt', 's5_b_re', 's5_b_im', 's5_c_re', 's5_c_im', 's5_d',
           's5_w_glu', 'ret_w_in', 'ret_gn_g', 'ret_w_o', 'ffn_w_up', 'ffn_conv_w', 'ffn_conv_b', 'ffn_w_down']
BIG = ['mla_w_down', 'mla_w_uq', 'mla_w_ukv', 'mla_w_o', 'hgrn_w_in', 'hgrn_w_o', 's5_w_glu', 'ret_w_in', 'ret_w_o',
       'ffn_w_up', 'ffn_w_down']
SMALL_SH = ['meta_tokens', 's5_d', 'ret_gn_g', 'ffn_conv_w']
SMALL_REP = ['norm_mix_g', 'norm_ffn_g', 'mla_cq_norm_g', 'mla_ckv_norm_g', 'mla_q_head_g', 'mla_k_head_g',
             'hgrn_lb_logits', 'hgrn_o_norm_g', 's5_lam_re', 's5_lam_im', 's5_log_dt', 's5_b_re', 's5_b_im',
             's5_c_re', 's5_c_im', 'ffn_conv_b']
LANE = 128


def _flat(arrs, mult):
    v = jnp.concatenate([a.reshape(-1) for a in arrs])
    pad = (-v.shape[0]) % mult
    return jnp.pad(v, (0, pad)).reshape(-1, LANE)


def _unflat(flat2d, like):
    v = flat2d.reshape(-1)
    out, o = [], 0
    for a in like:
        out.append(v[o:o + a.size].reshape(a.shape))
        o += a.size
    return out


def _lb_of(logits):
    cum = jnp.cumsum(jax.nn.softmax(logits, axis=0), axis=0)
    return (cum - cum[0:1])[1:2]


def _cols_to_blocks(g):
    k, n = g.shape
    return g.reshape(k, N_DEV, n // N_DEV).transpose(1, 0, 2)


def _blocks_to_cols(wb):
    nb, k, n = wb.shape
    return wb.transpose(1, 0, 2).reshape(k, nb * n)


SUBS = ['mla', 'ffn0', 'hgrn', 'ffn1', 's5', 'ffn2', 'ret', 'ffn3']
GROUPS = [[('mla_w_down', 0), ('mla_w_uq', 0), ('mla_w_ukv', 0), ('mla_w_o', 0)],
          [('ffn_w_up', 0), ('ffn_w_down', 0)],
          [('hgrn_w_in', 0), ('hgrn_w_o', 0)],
          [('ffn_w_up', 1), ('ffn_w_down', 1)],
          [('s5_w_glu', 0)],
          [('ffn_w_up', 2), ('ffn_w_down', 2)],
          [('ret_w_in', 0), ('ret_w_o', 0)],
          [('ffn_w_up', 3), ('ffn_w_down', 3)]]


def _pack8(parts, mult):
    v = jnp.concatenate(parts, axis=1)
    return jnp.pad(v, ((0, 0), (0, (-v.shape[1]) % mult))).reshape(N_DEV, -1, LANE)


def _sub_weights(k, got, rep, tabs, lp):
    ngm, ngf = rep['norm_mix_g'], rep['norm_ffn_g']
    if k == 0:
        return dict(ng=ngm[0:1], wdown=got[0].reshape(D, -1), gcq=rep['mla_cq_norm_g'], gckv=rep['mla_ckv_norm_g'],
                    wuq=got[1], wukv=got[2], gq=rep['mla_q_head_g'], gk=rep['mla_k_head_g'], wo=got[3].reshape(D, D),
                    mc=mla_consts(lp))
    if k == 2:
        return dict(ng=ngm[1:2], win=got[0], lb=tabs['lb'], go=rep['hgrn_o_norm_g'], wo=got[1].reshape(D, D))
    if k == 4:
        return dict(ng=ngm[2:3], tb=tabs['tb'], dsk=tabs['s5_d'], wglu=_blocks_to_cols(got[0]))
    if k == 6:
        return dict(ng=ngm[3:4], win=_blocks_to_cols(got[0]), gn=tabs['ret_gn_g'], wo=got[1].reshape(2 * D, D),
                    rc=ret_consts(lp))
    i = k // 2
    return dict(ng=ngf[i:i + 1], up=got[0], cw=tabs['conv_w'][:, i].reshape(2, 4, 3, 1, FFN_B),
                cb=rep['ffn_conv_b'][i].reshape(2, 4, 1, FFN_B), down=got[1].reshape(4, FFN_B, D))


def _sub_grad_blocks(k, g):
    if k == 0:
        parts = [g['wdown'], g['wuq'], g['wukv'], g['wo']]
    elif k == 2:
        parts = [g['win'], g['wo']]
    elif k == 4:
        parts = [_cols_to_blocks(g['wglu'])]
    elif k == 6:
        parts = [_cols_to_blocks(g['win']), g['wo']]
    else:
        parts = [g['up'], g['down']]
    return parts


_FWD = [mla_fwd, None, hgrn_fwd, None, s5_fwd, None, ret_fwd, None]
_BWD = [mla_bwd, None, hgrn_bwd, None, s5_bwd, None, ret_bwd, None]


def _step(args):
    w = {n: args[n] for n in WEIGHTS}
    x2, tgt = args['x'][0], args['loss_target'][0]

    lp = x2.shape[0] + OFF
    me = _my_index()
    mask = _rowmask(lp)
    rep = {n: w[n] for n in SMALL_REP}

    xs, slots = [], []
    for gi, grp in enumerate(GROUPS):
        items = [w[n][l].astype(BF) for n, l in grp] + ([_flat([w[n] for n in SMALL_SH], LANE)] if gi == 0 else [])
        slots.append((len(xs), len(items)))
        xs += items
    gh = push_start("gather_start", xs, me, scatter=False)

    def fetch(gi, after):
        s, cnt = slots[gi]
        return push_wait("gather_wait_" + SUBS[gi], gh[s:s + cnt], after, scatter=False)

    got = fetch(0, x2)
    sm, o, smp = got[-1].reshape(N_DEV, -1), 0, {}
    for n in SMALL_SH:
        smp[n] = sm[:, o:o + w[n].size].reshape((N_DEV,) + w[n].shape)
        o += w[n].size
    meta = smp['meta_tokens'].transpose(1, 0, 2).reshape(N_META, D)
    lb, lb_vjp = jax.vjp(_lb_of, rep['hgrn_lb_logits'])
    s5p = [rep[n][0] for n in ('s5_lam_re', 's5_lam_im', 's5_log_dt', 's5_b_re', 's5_b_im', 's5_c_re', 's5_c_im')]
    tb, tb_vjp = jax.vjp(s5_tables, *s5p)
    tabs = dict(lb=lb, tb=tb, s5_d=smp['s5_d'].reshape(1, D), ret_gn_g=smp['ret_gn_g'].reshape(1, 2 * D),
                conv_w=smp['ffn_conv_w'])
    h = jnp.concatenate([jnp.zeros((PAD, D), F32), meta, x2], axis=0)
    ws, saved = [], []
    for k in range(8):
        if k > 0:
            got = fetch(k, h)
        ws.append(_sub_weights(k, got, rep, tabs, lp))
        if k % 2:
            h, sv = ffn_fwd(k // 2, h, mask, ws[k])
        else:
            h, sv = _FWD[k](h, mask, ws[k])
        saved.append(sv)
    loss, dh = loss_head(h, tgt)

    gs, sh = [None] * 8, [None] * 8
    mk = mask
    for k in reversed(range(8)):
        if k % 2:
            dh, gs[k] = ffn_bwd(k // 2, dh, mk, ws[k], saved[k])
        else:
            dh, gs[k] = _BWD[k](dh, mk, ws[k], saved[k])
        blocks = [b.reshape((N_DEV,) + w[n].shape[1:]) for b, (n, _) in zip(_sub_grad_blocks(k, gs[k]), GROUPS[k])]
        if k == 0:
            dmeta = dh[PAD:OFF].reshape(N_META, N_DEV, D // N_DEV).transpose(1, 0, 2)
            dcw = jnp.stack([gs[2 * i + 1]['cw'].reshape(N_DEV, 3, FFN_B) for i in range(4)], axis=1)
            blocks.append(_pack8([t.reshape(N_DEV, -1) for t in (dmeta, gs[4]['dsk'], gs[6]['gn'], dcw)], LANE))
        sh[k] = push_start("scatter_start_" + SUBS[k], blocks, me, scatter=True)
        mk = mask + sh[k][0]["token"][0, 0]
    grad_x = dh[OFF:]

    ds5 = tb_vjp(gs[4]['tb'])
    g_rep = {
        'norm_mix_g': jnp.concatenate([gs[k]['ng'] for k in (0, 2, 4, 6)], axis=0),
        'norm_ffn_g': jnp.concatenate([gs[k]['ng'] for k in (1, 3, 5, 7)], axis=0),
        'mla_cq_norm_g': gs[0]['gcq'], 'mla_ckv_norm_g': gs[0]['gckv'], 'mla_q_head_g': gs[0]['gq'],
        'mla_k_head_g': gs[0]['gk'], 'hgrn_lb_logits': lb_vjp(gs[2]['lb'])[0], 'hgrn_o_norm_g': gs[2]['go'],
        's5_lam_re': ds5[0], 's5_lam_im': ds5[1], 's5_log_dt': ds5[2], 's5_b_re': ds5[3], 's5_b_im': ds5[4],
        's5_c_re': ds5[5], 's5_c_im': ds5[6],
        'ffn_conv_b': jnp.stack([gs[k]['cb'].reshape(-1) for k in (1, 3, 5, 7)], axis=0),
    }
    grep = _flat([g_rep[n] for n in SMALL_REP], 8 * LANE)
    rh = push_start("small_grads_start", [grep], me, scatter=False)

    lands = {n: [None] * w[n].shape[0] for n in BIG}
    res = {}
    late = [n for n, _ in GROUPS[0]]
    for k in reversed(range(1, 8)):
        got = push_wait("scatter_wait_" + SUBS[k], sh[k], grep, scatter=True)
        for (n, l), t in zip(GROUPS[k], got):
            lands[n][l] = t
    for n in BIG:
        if n not in late:
            res[n] = adamw("adam_" + n, lands[n], w[n], args['m_' + n], args['v_' + n])
    after = res['ffn_w_up'][1]
    got = push_wait("scatter_wait_" + SUBS[0], sh[0], after, scatter=True)
    small_land = got[-1]
    (rep_land,) = push_wait("small_grads_wait", rh, after, scatter=False)
    for (n, _), t in zip(GROUPS[0], got):
        res[n] = adamw("adam_" + n, [t], w[n], args['m_' + n], args['v_' + n])

    def flat_adam(name, land, names, mult):
        like = [w[n] for n in names]
        out = adamw(name, [land], _flat(like, mult)[None], _flat([args['m_' + n] for n in names], mult)[None],
                    _flat([args['v_' + n] for n in names], mult)[None])
        for n, parts in zip(names, zip(*[_unflat(t, like) for t in out])):
            res[n] = list(parts)

    flat_adam("adam_small_sharded", small_land, SMALL_SH, LANE)
    flat_adam("adam_small_replicated", rep_land, SMALL_REP, 8 * LANE)

    total = lax.psum(loss[0, 0], ("x", "y", "c"))
    outs = [total, grad_x[None]]
    for k in range(4):
        outs += [res[n][k] for n in WEIGHTS]
    return tuple(outs)


def kernel(x, meta_tokens, norm_mix_g, norm_ffn_g, mla_w_down, mla_cq_norm_g, mla_ckv_norm_g, mla_w_uq, mla_w_ukv, mla_q_head_g, mla_k_head_g, mla_w_o, hgrn_w_in, hgrn_lb_logits, hgrn_o_norm_g, hgrn_w_o, s5_lam_re, s5_lam_im, s5_log_dt, s5_b_re, s5_b_im, s5_c_re, s5_c_im, s5_d, s5_w_glu, ret_w_in, ret_gn_g, ret_w_o, ffn_w_up, ffn_conv_w, ffn_conv_b, ffn_w_down, loss_target, m_meta_tokens, m_norm_mix_g, m_norm_ffn_g, m_mla_w_down, m_mla_cq_norm_g, m_mla_ckv_norm_g, m_mla_w_uq, m_mla_w_ukv, m_mla_q_head_g, m_mla_k_head_g, m_mla_w_o, m_hgrn_w_in, m_hgrn_lb_logits, m_hgrn_o_norm_g, m_hgrn_w_o, m_s5_lam_re, m_s5_lam_im, m_s5_log_dt, m_s5_b_re, m_s5_b_im, m_s5_c_re, m_s5_c_im, m_s5_d, m_s5_w_glu, m_ret_w_in, m_ret_gn_g, m_ret_w_o, m_ffn_w_up, m_ffn_conv_w, m_ffn_conv_b, m_ffn_w_down, v_meta_tokens, v_norm_mix_g, v_norm_ffn_g, v_mla_w_down, v_mla_cq_norm_g, v_mla_ckv_norm_g, v_mla_w_uq, v_mla_w_ukv, v_mla_q_head_g, v_mla_k_head_g, v_mla_w_o, v_hgrn_w_in, v_hgrn_lb_logits, v_hgrn_o_norm_g, v_hgrn_w_o, v_s5_lam_re, v_s5_lam_im, v_s5_log_dt, v_s5_b_re, v_s5_b_im, v_s5_c_re, v_s5_c_im, v_s5_d, v_s5_w_glu, v_ret_w_in, v_ret_gn_g, v_ret_w_o, v_ffn_w_up, v_ffn_conv_w, v_ffn_conv_b, v_ffn_w_down):
    return _step(dict(locals()))
```

```python
import functools
import math

import jax
import jax.numpy as jnp
from jax import lax
from jax.experimental import pallas as pl
from jax.experimental.pallas import tpu as pltpu

F32 = jnp.float32
BF = jnp.bfloat16
SDS = jax.ShapeDtypeStruct

N_DEV = 8
D = 1024
N_META = 16
PAD = 48
OFF = PAD + N_META
CH = 64
EPS = 1e-6
NEG = -1e30
ROPE_BASE = 10000.0

MLA_H, MLA_NOPE, MLA_ROPE, MLA_V = 8, 128, 64, 128
MLA_QK = MLA_NOPE + MLA_ROPE
MLA_QL, MLA_KVL = 384, 256
HG_H, HG_D, HG_C = 8, 128, 16
S5_G, S5_P, S5_K = 64, 64, 16
S5_SG = 8
RET_H, RET_DK, RET_DV = 4, 256, 512
FFN_F = 2816
FFN_B = 704

ADAM_LR, ADAM_B1, ADAM_B2, ADAM_EPS, ADAM_WD, ADAM_STEP = 0.001, 0.9, 0.999, 1e-08, 0.01, 10

VMEM_LIMIT = 56 * 1024 * 1024
ARB = "arbitrary"


def _cp(n):
    return pltpu.CompilerParams(dimension_semantics=(ARB,) * n, vmem_limit_bytes=VMEM_LIMIT)


def _bdot(a, b, ca, cb):
    return lax.dot_general(a.astype(BF), b.astype(BF), (((ca,), (cb,)), ((), ())), preferred_element_type=F32)


@jax.custom_vjp
def mm(a, b):
    return _bdot(a, b, 1, 0)


@jax.custom_vjp
def mm_nt(a, b):
    return _bdot(a, b, 1, 1)


@jax.custom_vjp
def mm_tn(a, b):
    return _bdot(a, b, 0, 0)


mm.defvjp(lambda a, b: (mm(a, b), (a, b)),
          lambda r, g: (mm_nt(g, r[1]).astype(r[0].dtype), mm_tn(r[0], g).astype(r[1].dtype)))
mm_nt.defvjp(lambda a, b: (mm_nt(a, b), (a, b)),
             lambda r, g: (mm(g, r[1]).astype(r[0].dtype), mm_tn(g, r[0]).astype(r[1].dtype)))
mm_tn.defvjp(lambda a, b: (mm_tn(a, b), (a, b)),
             lambda r, g: (mm_nt(r[1], g).astype(r[0].dtype), mm(r[0], g).astype(r[1].dtype)))


def _xdot(a, b, ca, cb):
    return lax.dot_general(a, b, (((ca,), (cb,)), ((), ())), preferred_element_type=F32,
                           precision=lax.Precision.HIGHEST)


@jax.custom_vjp
def cright(x, r):
    return _xdot(x, r, 1, 0)


cright.defvjp(lambda x, r: (cright(x, r), r), lambda r, g: (_xdot(g, r, 1, 1), jnp.zeros_like(r)))


def _shift_raw(x, s):
    n = x.shape[0]
    r = lax.broadcasted_iota(jnp.int32, x.shape, 0)
    y = pltpu.roll(x, s % n, 0)
    return jnp.where((r >= s) & (r < n + s), y, 0.0)


@functools.partial(jax.custom_vjp, nondiff_argnums=(1,))
def shift_rows(x, s):
    return _shift_raw(x, s)


shift_rows.defvjp(lambda x, s: (_shift_raw(x, s), None), lambda s, _, g: (_shift_raw(g, -s),))


def _seg_shift_raw(x, s, seg, up):
    n = x.shape[0]
    r = lax.broadcasted_iota(jnp.int32, x.shape, 0) % seg
    if up:
        return jnp.where(r < seg - s, pltpu.roll(x, n - s, 0), 0.0)
    return jnp.where(r >= s, pltpu.roll(x, s, 0), 0.0)


@functools.partial(jax.custom_vjp, nondiff_argnums=(1, 2))
def seg_shift(x, s, seg):
    return _seg_shift_raw(x, s, seg, False)


seg_shift.defvjp(lambda x, s, seg: (_seg_shift_raw(x, s, seg, False), None),
                 lambda s, seg, _, g: (_seg_shift_raw(g, s, seg, True),))


def _seg_cumsum(x, seg):
    s = 1
    while s < seg:
        x = x + seg_shift(x, s, seg)
        s *= 2
    return x


def _rms(x, g):
    return x * lax.rsqrt(jnp.mean(x * x, axis=-1, keepdims=True) + EPS) * g


def _silu(x):
    return x * jax.nn.sigmoid(x)


def _mm_call(name, a, b, *, grid, a_spec, b_spec, o_shape, o_spec, dims, acc_shape, res=None, res_spec=None,
             mask_tm=None):
    nk = grid[2]

    def body(*refs):
        if res is None:
            a_ref, b_ref, o_ref = refs[:3]
        else:
            a_ref, b_ref, r_ref, o_ref = refs[:4]
        k = pl.program_id(2)

        def dot():
            return lax.dot_general(a_ref[...].astype(BF), b_ref[...].astype(BF), dims, preferred_element_type=F32)

        def finish(v):
            if res is not None:
                v = v + r_ref[...].astype(F32)
                rows = pl.program_id(0) * mask_tm + lax.broadcasted_iota(jnp.int32, v.shape, 0)
                v = jnp.where(rows >= PAD, v, 0.0)
            o_ref[...] = v.astype(o_ref.dtype)

        if nk == 1:
            finish(dot())
            return
        acc = refs[-1]

        @pl.when(k == 0)
        def _():
            acc[...] = dot()

        @pl.when((k > 0) & (k < nk - 1))
        def _():
            acc[...] += dot()

        @pl.when(k == nk - 1)
        def _():
            finish(acc[...] + dot())

    ins = [a, b] + ([res] if res is not None else [])
    specs = [a_spec, b_spec] + ([res_spec] if res is not None else [])
    scratch = [pltpu.VMEM(acc_shape, F32)] if nk > 1 else []
    return pl.pallas_call(body, name=name, grid=grid, in_specs=specs, out_specs=o_spec, out_shape=o_shape,
                          scratch_shapes=scratch, compiler_params=_cp(3))(*ins)


NN = (((1,), (0,)), ((), ()))
NT = (((1,), (1,)), ((), ()))
TN = (((0,), (0,)), ((), ()))


def _row_tile(lp):
    for t in (832, 640, 320, 64):
        if lp % t == 0:
            return t
    raise ValueError(lp)


def _col_tile(n):
    for t in (1024, 768, 512, 384, 256, 128):
        if n % t == 0:
            return t
    return n


def lin(name, a, w, out_dtype=F32, res=None):
    m, k = a.shape
    n = w.shape[1]
    tm, tn, tc = _row_tile(m), _col_tile(n), _col_tile(k)
    return _mm_call(name, a, w, grid=(m // tm, n // tn, k // tc),
                    a_spec=pl.BlockSpec((tm, tc), lambda i, j, kk: (i, kk)),
                    b_spec=pl.BlockSpec((tc, tn), lambda i, j, kk: (kk, j)),
                    o_shape=SDS((m, n), out_dtype), o_spec=pl.BlockSpec((tm, tn), lambda i, j, kk: (i, j)),
                    dims=NN, acc_shape=(tm, tn), res=res,
                    res_spec=pl.BlockSpec((tm, tn), lambda i, j, kk: (i, j)), mask_tm=tm)


def lin_bo(name, a, wb, out_dtype=F32):
    m, k = a.shape
    nb, _, n = wb.shape
    tm = _row_tile(m)
    return _mm_call(name, a, wb, grid=(m // tm, nb, 1),
                    a_spec=pl.BlockSpec((tm, k), lambda i, j, kk: (i, 0)),
                    b_spec=pl.BlockSpec((None, k, n), lambda i, j, kk: (j, 0, 0)),
                    o_shape=SDS((nb, m, n), out_dtype), o_spec=pl.BlockSpec((None, tm, n), lambda i, j, kk: (j, i, 0)),
                    dims=NN, acc_shape=(tm, n))


def lin_bi(name, ab, wb, out_dtype=F32, res=None):
    nb, m, k = ab.shape
    n = wb.shape[2]
    tm, tn = _row_tile(m), _col_tile(n)
    return _mm_call(name, ab, wb, grid=(m // tm, n // tn, nb),
                    a_spec=pl.BlockSpec((None, tm, k), lambda i, j, kk: (kk, i, 0)),
                    b_spec=pl.BlockSpec((None, k, tn), lambda i, j, kk: (kk, 0, j)),
                    o_shape=SDS((m, n), out_dtype), o_spec=pl.BlockSpec((tm, tn), lambda i, j, kk: (i, j)),
                    dims=NN, acc_shape=(tm, tn), res=res,
                    res_spec=pl.BlockSpec((tm, tn), lambda i, j, kk: (i, j)), mask_tm=tm)


def lin_t(name, g, w, out_dtype=F32):
    m, n = g.shape
    k = w.shape[0]
    tm, tk, tc = _row_tile(m), _col_tile(k), _col_tile(n)
    return _mm_call(name, g, w, grid=(m // tm, k // tk, n // tc),
                    a_spec=pl.BlockSpec((tm, tc), lambda i, j, kk: (i, kk)),
                    b_spec=pl.BlockSpec((tk, tc), lambda i, j, kk: (j, kk)),
                    o_shape=SDS((m, k), out_dtype), o_spec=pl.BlockSpec((tm, tk), lambda i, j, kk: (i, j)),
                    dims=NT, acc_shape=(tm, tk))


def lin_t_bi(name, gb, wb, out_dtype=F32):
    nb, m, n = gb.shape
    k = wb.shape[1]
    tm, tk = _row_tile(m), _col_tile(k)
    return _mm_call(name, gb, wb, grid=(m // tm, k // tk, nb),
                    a_spec=pl.BlockSpec((None, tm, n), lambda i, j, kk: (kk, i, 0)),
                    b_spec=pl.BlockSpec((None, tk, n), lambda i, j, kk: (kk, j, 0)),
                    o_shape=SDS((m, k), out_dtype), o_spec=pl.BlockSpec((tm, tk), lambda i, j, kk: (i, j)),
                    dims=NT, acc_shape=(tm, tk))


def lin_t_bo(name, g, wb, out_dtype=F32):
    m, n = g.shape
    nb, k, _ = wb.shape
    tm = _row_tile(m)
    return _mm_call(name, g, wb, grid=(m // tm, nb, 1),
                    a_spec=pl.BlockSpec((tm, n), lambda i, j, kk: (i, 0)),
                    b_spec=pl.BlockSpec((None, k, n), lambda i, j, kk: (j, 0, 0)),
                    o_shape=SDS((nb, m, k), out_dtype), o_spec=pl.BlockSpec((None, tm, k), lambda i, j, kk: (j, i, 0)),
                    dims=NT, acc_shape=(tm, k))


def wgrad(name, a, g):
    m, k = a.shape
    n = g.shape[1]
    tm, tn = _row_tile(m), _col_tile(n)
    return _mm_call(name, a, g, grid=(1, n // tn, m // tm),
                    a_spec=pl.BlockSpec((tm, k), lambda i, j, kk: (kk, 0)),
                    b_spec=pl.BlockSpec((tm, tn), lambda i, j, kk: (kk, j)),
                    o_shape=SDS((k, n), F32), o_spec=pl.BlockSpec((k, tn), lambda i, j, kk: (0, j)),
                    dims=TN, acc_shape=(k, tn))


def wgrad_bo(name, a, gb):
    m, k = a.shape
    nb, _, n = gb.shape
    tm = _row_tile(m)
    return _mm_call(name, a, gb, grid=(nb, 1, m // tm),
                    a_spec=pl.BlockSpec((tm, k), lambda i, j, kk: (kk, 0)),
                    b_spec=pl.BlockSpec((None, tm, n), lambda i, j, kk: (i, kk, 0)),
                    o_shape=SDS((nb, k, n), F32), o_spec=pl.BlockSpec((None, k, n), lambda i, j, kk: (i, 0, 0)),
                    dims=TN, acc_shape=(k, n))


def wgrad_bi(name, zb, g):
    nb, m, k = zb.shape
    n = g.shape[1]
    tm, tn = _row_tile(m), _col_tile(n)
    return _mm_call(name, zb, g, grid=(nb, n // tn, m // tm),
                    a_spec=pl.BlockSpec((None, tm, k), lambda i, j, kk: (i, kk, 0)),
                    b_spec=pl.BlockSpec((tm, tn), lambda i, j, kk: (kk, j)),
                    o_shape=SDS((nb, k, n), F32), o_spec=pl.BlockSpec((None, k, tn), lambda i, j, kk: (i, 0, j)),
                    dims=TN, acc_shape=(k, tn))


class Arg:
    def __init__(self, arr, block, imap, shared=False, acc=False):
        self.arr, self.block, self.imap = arr, block, imap
        self.shared = shared
        self.acc = acc

    @property
    def spec(self):
        return pl.BlockSpec(self.block, self.imap)

    def vshape(self):
        return tuple(b for b in self.block if b is not None)


def _rev(arg, nt):
    return pl.BlockSpec(arg.block, lambda o, t, _f=arg.imap: _f(o, nt - 1 - t))


def seq_fwd(name, fn, grid, params, consts, xs, outs, carries=(), save_dtype=F32):
    no, nt = grid
    n_p, n_c, n_x, n_y, n_k = len(params), len(consts), len(xs), len(outs), len(carries)

    def body(*refs):
        p_refs = refs[:n_p]
        c_refs = refs[n_p:n_p + n_c]
        x_refs = refs[n_p + n_c:n_p + n_c + n_x]
        r = n_p + n_c + n_x
        y_refs = refs[r:r + n_y]
        s_refs = refs[r + n_y:r + n_y + n_k]
        k_refs = refs[r + n_y + n_k:]
        t = pl.program_id(1)

        if n_k:
            @pl.when(t == 0)
            def _():
                for k in k_refs:
                    k[...] = jnp.zeros_like(k)

        carry = tuple(k[...] for k in k_refs)
        for s, c in zip(s_refs, carry):
            s[...] = c.astype(s.dtype)
        new_carry, ys = fn(tuple(p[...] for p in p_refs), tuple(c[...] for c in c_refs), carry,
                           tuple(x[...] for x in x_refs))
        for k, c in zip(k_refs, new_carry):
            k[...] = c
        for y_ref, y in zip(y_refs, ys):
            y_ref[...] = y.astype(y_ref.dtype)

    out_shape = [SDS(s, d) for (s, d, _, _) in outs]
    out_specs = [pl.BlockSpec(b, im) for (_, _, b, im) in outs]
    for cs in carries:
        out_shape.append(SDS((no, nt) + cs, save_dtype))
        out_specs.append(pl.BlockSpec((None, None) + cs, lambda o, t, _n=len(cs): (o, t) + (0,) * _n))
    res = pl.pallas_call(
        body, name=name, grid=grid, in_specs=[a.spec for a in list(params) + list(consts) + list(xs)],
        out_specs=out_specs, out_shape=out_shape, scratch_shapes=[pltpu.VMEM(cs, F32) for cs in carries],
        compiler_params=_cp(2))(*[a.arr for a in list(params) + list(consts) + list(xs)])
    return res[:n_y], res[n_y:]


def seq_bwd(name, fn, grid, params, consts, xs, dys, saved=(), carries=()):
    no, nt = grid
    n_p, n_c, n_x, n_y, n_k = len(params), len(consts), len(xs), len(dys), len(carries)

    def body(*refs):
        p_refs = refs[:n_p]
        c_refs = refs[n_p:n_p + n_c]
        x_refs = refs[n_p + n_c:n_p + n_c + n_x]
        r = n_p + n_c + n_x
        g_refs = refs[r:r + n_y]
        s_refs = refs[r + n_y:r + n_y + n_k]
        r = r + n_y + n_k
        dx_refs = refs[r:r + n_x]
        dp_refs = refs[r + n_x:r + n_x + n_p]
        k_refs = refs[r + n_x + n_p:]
        o = pl.program_id(0)
        t = pl.program_id(1)

        if n_k:
            @pl.when(t == 0)
            def _():
                for k in k_refs:
                    k[...] = jnp.zeros_like(k)

        for a, dp in zip(params, dp_refs):
            @pl.when((t == 0) & (o == 0) if a.shared else (t == 0))
            def _(dp=dp):
                dp[...] = jnp.zeros_like(dp)

        for a, dx in zip(xs, dx_refs):
            if a.acc:
                @pl.when(t == 0)
                def _(dx=dx):
                    dx[...] = jnp.zeros_like(dx)

        consts_v = tuple(c[...] for c in c_refs)

        def f(pv, cv, xv):
            return fn(pv, consts_v, cv, xv)

        pv = tuple(p[...] for p in p_refs)
        cv = tuple(s[...].astype(F32) for s in s_refs)
        xv = tuple(x[...] for x in x_refs)
        (new_carry, ys), vjp = jax.vjp(f, pv, cv, xv)
        cot = (tuple(k[...] for k in k_refs), tuple(g[...].astype(y.dtype) for g, y in zip(g_refs, ys)))
        dpv, dcv, dxv = vjp(cot)
        for k, c in zip(k_refs, dcv):
            k[...] = c
        for dp, v in zip(dp_refs, dpv):
            dp[...] += v
        for a, dx, v in zip(xs, dx_refs, dxv):
            if a.acc:
                dx[...] += v
            else:
                dx[...] = v.astype(dx.dtype)

    in_specs = ([_rev(a, nt) for a in list(params) + list(consts) + list(xs) + list(dys)]
                + [pl.BlockSpec((None, None) + cs, lambda o, t, _n=len(cs): (o, nt - 1 - t) + (0,) * _n) for cs in carries])
    out_shape = [SDS(a.arr.shape, F32) for a in xs] + [SDS(a.arr.shape, F32) for a in params]
    out_specs = [_rev(a, nt) for a in list(xs) + list(params)]
    res = pl.pallas_call(
        body, name=name, grid=grid, in_specs=in_specs, out_specs=out_specs, out_shape=out_shape,
        scratch_shapes=[pltpu.VMEM(cs, F32) for cs in carries], compiler_params=_cp(2))(
            *[a.arr for a in list(params) + list(consts) + list(xs) + list(dys)], *saved)
    return res[:n_x], res[n_x:]


def _rowmask(lp):
    return (jnp.arange(lp) >= PAD).astype(F32)[:, None]


def _norm_fn(p, c, k, x):
    return (), (_rms(x[0] * c[0], p[0]),)


def _norm_b_fn(p, c, k, x):
    h = x[0] * c[0]
    return (), (_rms(h, p[0]), h)


def norm_fwd(name, h, g, mask, out_dtype=BF):
    lp, d = h.shape
    tr = _row_tile(lp)
    row = lambda o, t: (t, 0)
    (a,), _ = seq_fwd(name, _norm_fn, (1, lp // tr), [Arg(g, (1, d), lambda o, t: (0, 0), shared=True)],
                      [Arg(mask, (tr, 1), row)], [Arg(h, (tr, d), row)], [((lp, d), out_dtype, (tr, d), row)])
    return a


def norm_bwd(name, h, g, mask, da, dskip):
    lp, d = h.shape
    tr = _row_tile(lp)
    row = lambda o, t: (t, 0)
    (dh,), (dg,) = seq_bwd(name, _norm_b_fn, (1, lp // tr), [Arg(g, (1, d), lambda o, t: (0, 0), shared=True)],
                           [Arg(mask, (tr, 1), row)], [Arg(h, (tr, d), row)],
                           [Arg(da, (tr, d), row), Arg(dskip, (tr, d), row)])
    return dh, dg


def _ffn_tile(lp):
    return 320 if (lp % 320 == 0 and lp > 320) else 64


def _conv_rows(ext, w, b, n):
    u2 = ext[8:8 + n]
    u1 = pltpu.roll(ext, 1, 0)[8:8 + n]
    u0 = pltpu.roll(ext, 2, 0)[8:8 + n]
    return w[2] * u2 + w[1] * u1 + w[0] * u0 + b, (u0, u1, u2)


def ffn_core_fwd(name, u, cw, cb):
    _, nj, lp, fb = u.shape
    tr = _ffn_tile(lp)
    nt = lp // tr

    def body(u_ref, up_ref, w_ref, b_ref, z_ref):
        i = pl.program_id(1)
        prev = jnp.where(i > 0, up_ref[...], 0.0)
        cs = []
        for s in range(2):
            ext = jnp.concatenate([prev[s], u_ref[s]], axis=0)
            c, _ = _conv_rows(ext, w_ref[s], b_ref[s], tr)
            cs.append(c)
        z_ref[...] = (_silu(cs[0]) * cs[1]).astype(z_ref.dtype)

    return pl.pallas_call(
        body, name=name, grid=(nj, nt),
        in_specs=[pl.BlockSpec((2, None, tr, fb), lambda j, i: (0, j, i, 0)),
                  pl.BlockSpec((2, None, 8, fb), lambda j, i: (0, j, jnp.maximum(i * (tr // 8) - 1, 0), 0)),
                  pl.BlockSpec((2, None, 3, 1, fb), lambda j, i: (0, j, 0, 0, 0)),
                  pl.BlockSpec((2, None, 1, fb), lambda j, i: (0, j, 0, 0))],
        out_specs=pl.BlockSpec((None, tr, fb), lambda j, i: (j, i, 0)),
        out_shape=SDS((nj, lp, fb), BF), compiler_params=_cp(2))(u, u, cw, cb)


def ffn_core_bwd(name, u, dz, cw, cb):
    _, nj, lp, fb = u.shape
    tr = _ffn_tile(lp)
    nt = lp // tr
    nb8 = lp // 8

    def body(u_ref, up_ref, un_ref, dz_ref, dzn_ref, w_ref, b_ref, du_ref, dw_ref, db_ref):
        i = pl.program_id(1)

        @pl.when(i == 0)
        def _():
            dw_ref[...] = jnp.zeros_like(dw_ref)
            db_ref[...] = jnp.zeros_like(db_ref)

        prev = jnp.where(i > 0, up_ref[...], 0.0)
        nxt = jnp.where(i < nt - 1, un_ref[...], 0.0)
        dz_e = jnp.concatenate([dz_ref[...], jnp.where(i < nt - 1, dzn_ref[...], 0.0)], axis=0)
        n = tr + 8
        cs, taps = [], []
        for s in range(2):
            ext = jnp.concatenate([prev[s], u_ref[s], nxt[s]], axis=0)
            c, tp = _conv_rows(ext, w_ref[s], b_ref[s], n)
            cs.append(c)
            taps.append(tp)
        sg = jax.nn.sigmoid(cs[0])
        dcs = [dz_e * cs[1] * sg * (1.0 + cs[0] * (1.0 - sg)), dz_e * cs[0] * sg]
        for s in range(2):
            dc = dcs[s]
            w = w_ref[s]
            d1 = pltpu.roll(dc, n - 1, 0)[:tr]
            d2 = pltpu.roll(dc, n - 2, 0)[:tr]
            dcm = dc[:tr]
            du_ref[s] = w[2] * dcm + w[1] * d1 + w[0] * d2
            for k in range(3):
                dw_ref[s, k] += jnp.sum(dcm * taps[s][k][:tr], axis=0, keepdims=True)
            db_ref[s] += jnp.sum(dcm, axis=0, keepdims=True)

    return pl.pallas_call(
        body, name=name, grid=(nj, nt),
        in_specs=[pl.BlockSpec((2, None, tr, fb), lambda j, i: (0, j, i, 0)),
                  pl.BlockSpec((2, None, 8, fb), lambda j, i: (0, j, jnp.maximum(i * (tr // 8) - 1, 0), 0)),
                  pl.BlockSpec((2, None, 8, fb), lambda j, i: (0, j, jnp.minimum((i + 1) * (tr // 8), nb8 - 1), 0)),
                  pl.BlockSpec((None, tr, fb), lambda j, i: (j, i, 0)),
                  pl.BlockSpec((None, 8, fb), lambda j, i: (j, jnp.minimum((i + 1) * (tr // 8), nb8 - 1), 0)),
                  pl.BlockSpec((2, None, 3, 1, fb), lambda j, i: (0, j, 0, 0, 0)),
                  pl.BlockSpec((2, None, 1, fb), lambda j, i: (0, j, 0, 0))],
        out_specs=[pl.BlockSpec((2, None, tr, fb), lambda j, i: (0, j, i, 0)),
                   pl.BlockSpec((2, None, 3, 1, fb), lambda j, i: (0, j, 0, 0, 0)),
                   pl.BlockSpec((2, None, 1, fb), lambda j, i: (0, j, 0, 0))],
        out_shape=[SDS(u.shape, F32), SDS(cw.shape, F32), SDS(cb.shape, F32)],
        compiler_params=_cp(2))(u, u, u, dz, dz, cw, cb)


def ffn_fwd(i, h, mask, w):
    a = norm_fwd(f"ffn{i}_norm", h, w["ng"], mask)
    u = lin_bo(f"ffn{i}_up", a, w["up"])
    lp = h.shape[0]
    u = u.reshape(2, 4, lp, FFN_B)
    z = ffn_core_fwd(f"ffn{i}_core", u, w["cw"], w["cb"])
    h2 = lin_bi(f"ffn{i}_down", z, w["down"], res=h)
    return h2, (h, a, u, z)


def ffn_bwd(i, dh2, mask, w, saved):
    h, a, u, z = saved
    lp = h.shape[0]
    g = {}
    g["down"] = wgrad_bi(f"ffn{i}_dwdown", z, dh2)
    dz = lin_t_bo(f"ffn{i}_dz", dh2, w["down"])
    du, g["cw"], g["cb"] = ffn_core_bwd(f"ffn{i}_core_b", u, dz, w["cw"], w["cb"])
    du = du.reshape(8, lp, FFN_B)
    g["up"] = wgrad_bo(f"ffn{i}_dwup", a, du)
    da = lin_t_bi(f"ffn{i}_da", du, w["up"])
    dh, g["ng"] = norm_bwd(f"ffn{i}_norm_b", h, w["ng"], mask, da, dh2)
    return dh, g


HG_HB = 4


def _hgrn_fn(p, c, k, x):
    lb, go = p
    q, f, iv, g = x[0][0], x[0][1], x[0][2], x[0][3]
    (st_all,) = k
    qs = _silu(q)
    forget = lb + (1.0 - lb) * jax.nn.sigmoid(f)
    logf = jnp.log(forget)
    kk = 1.0 - forget
    gc_all = _seg_cumsum(logf, HG_C)
    r = lax.broadcasted_iota(jnp.int32, (HG_C, HG_C), 0)
    cc = lax.broadcasted_iota(jnp.int32, (HG_C, HG_C), 1)
    ns = CH // HG_C
    cells = [(j, s) for j in range(HG_HB) for s in range(ns)]

    def blk(t, j, s):
        return t[HG_C * s:HG_C * (s + 1), HG_D * j:HG_D * (j + 1)]

    gl = {c: jnp.sum(blk(logf, *c), axis=0, keepdims=True) for c in cells}
    qd = {c: blk(qs, *c) * jnp.exp(blk(gc_all, *c)) for c in cells}
    ki = {c: blk(kk, *c) * jnp.exp(-blk(gc_all, *c)) for c in cells}
    up = {c: mm_tn(blk(iv, *c), blk(kk, *c) * jnp.exp(gl[c] - blk(gc_all, *c))) for c in cells}
    st, sts = {}, []
    for j in range(HG_HB):
        cur = st_all[j]
        for s in range(ns):
            st[(j, s)] = cur
            cur = cur * jnp.exp(gl[(j, s)]) + up[(j, s)]
        sts.append(cur)
    both = {c: mm_nt(qd[c], jnp.concatenate([st[c], ki[c]], axis=0)) for c in cells}
    oc = {c: mm(jnp.where(r >= cc, both[c][:, HG_D:], 0.0), blk(iv, *c)) + both[c][:, :HG_D] for c in cells}
    zs = []
    for j in range(HG_HB):
        o = jnp.concatenate([oc[(j, s)] for s in range(ns)], axis=0)
        zs.append(_rms(o, go) * _silu(g[:, HG_D * j:HG_D * (j + 1)]))
    return (jnp.stack(sts, axis=0),), (jnp.concatenate(zs, axis=1),)


def _hgrn_args(u4, lb, go):
    lp = u4.shape[2]
    wb = HG_HB * HG_D
    xs = [Arg(u4, (4, None, CH, wb), lambda o, t: (0, o, t, 0))]
    ps = [Arg(lb, (1, wb), lambda o, t: (0, o)), Arg(go, (1, HG_D), lambda o, t: (0, 0), shared=True)]
    return (HG_H // HG_HB, lp // CH), ps, xs


def hgrn_fwd(h, mask, w):
    lp = h.shape[0]
    a = norm_fwd("hgrn_norm", h, w["ng"], mask)
    u4 = lin_bo("hgrn_in", a, w["win"]).reshape(4, HG_H // HG_HB, lp, HG_HB * HG_D)
    grid, ps, xs = _hgrn_args(u4, w["lb"], w["go"])
    (z,), (st,) = seq_fwd("hgrn_core", _hgrn_fn, grid, ps, [], xs,
                          [((lp, D), BF, (CH, HG_HB * HG_D), lambda o, t: (t, o))], carries=[(HG_HB, HG_D, HG_D)])
    h2 = lin("hgrn_out", z, w["wo"], res=h)
    return h2, (h, a, u4, z, st)


def hgrn_bwd(dh2, mask, w, saved):
    h, a, u4, z, st = saved
    lp = h.shape[0]
    g = {}
    g["wo"] = wgrad("hgrn_dwo", z, dh2)
    dz = lin_t("hgrn_dz", dh2, w["wo"])
    grid, ps, xs = _hgrn_args(u4, w["lb"], w["go"])
    (du4,), (g["lb"], g["go"]) = seq_bwd("hgrn_core_b", _hgrn_fn, grid, ps, [], xs,
                                         [Arg(dz, (CH, HG_HB * HG_D), lambda o, t: (t, o))], saved=[st],
                                         carries=[(HG_HB, HG_D, HG_D)])
    du = du4.reshape(N_DEV, lp, HG_HB * HG_D)
    g["win"] = wgrad_bo("hgrn_dwin", a, du)
    da = lin_t_bi("hgrn_da", du, w["win"])
    dh, g["ng"] = norm_bwd("hgrn_norm_b", h, w["ng"], mask, da, dh2)
    return dh, g


S5_W = S5_SG * S5_P


def s5_tables(lam_re, lam_im, log_dt, b_re, b_im, c_re, c_im):
    dt = jnp.exp(log_dt)[:, None]
    mag = jnp.exp(lam_re * dt)
    abar_re = mag * jnp.cos(lam_im * dt)
    abar_im = mag * jnp.sin(lam_im * dt)
    den = lam_re * lam_re + lam_im * lam_im
    zoh_re = ((abar_re - 1.0) * lam_re + abar_im * lam_im) / den
    zoh_im = (abar_im * lam_re - (abar_re - 1.0) * lam_im) / den
    bbar_re = zoh_re[..., None] * b_re - zoh_im[..., None] * b_im
    bbar_im = zoh_re[..., None] * b_im + zoh_im[..., None] * b_re
    eye = jnp.eye(S5_SG, dtype=F32)

    def blockdiag_in(b):
        t = b.reshape(N_DEV, S5_SG, S5_P, S5_K).transpose(0, 1, 3, 2)
        return jnp.einsum("jakp,ab->jakbp", t, eye).reshape(N_DEV, S5_SG * S5_K, S5_W)

    def blockdiag_out(c):
        t = c.reshape(N_DEV, S5_SG, S5_K, S5_P).transpose(0, 1, 3, 2)
        return jnp.einsum("japk,ab->japbk", t, eye).reshape(N_DEV, S5_W, S5_SG * S5_K)

    wb = jnp.concatenate([blockdiag_in(bbar_re), blockdiag_in(bbar_im)], axis=2)
    wc = jnp.concatenate([blockdiag_out(c_re), -blockdiag_out(c_im)], axis=1)

    def powers(n):
        steps = n[:, None, None] * dt[None]
        pm = jnp.exp(lam_re[None] * steps)
        pr = (pm * jnp.cos(lam_im[None] * steps)).reshape(-1, N_DEV, S5_W).transpose(1, 0, 2)
        pi = (pm * jnp.sin(lam_im[None] * steps)).reshape(-1, N_DEV, S5_W).transpose(1, 0, 2)
        return jnp.concatenate([pr, pi], axis=2)

    apow = powers(2.0 ** jnp.arange(6, dtype=F32))[:, :, None, :]
    ptab = powers(jnp.arange(CH, dtype=F32) + 1.0)
    return wb, wc, apow, ptab


def _cmul(ar, ai, xr, xi):
    return ar * xr - ai * xi, ar * xi + ai * xr


S5_BB = 4


def _s5_fn(p, c, k, x):
    wb, wc, apow, ptab, dsk = p
    (a,) = x
    (x0,) = k
    blocks = range(S5_BB)
    aj = [a[:, 128 * j:128 * (j + 1)] for j in blocks]
    bu = [mm(aj[j], wb[j]) for j in blocks]
    xxs, x0n = [], []
    for j in blocks:
        xr, xi = bu[j][:, :S5_W], bu[j][:, S5_W:]
        for s in range(6):
            asr, asi = apow[j][s][:, :S5_W], apow[j][s][:, S5_W:]
            dr, di = _cmul(asr, asi, shift_rows(xr, 1 << s), shift_rows(xi, 1 << s))
            xr, xi = xr + dr, xi + di
        dr, di = _cmul(ptab[j][:, :S5_W], ptab[j][:, S5_W:], x0[j][:, :S5_W], x0[j][:, S5_W:])
        xx = jnp.concatenate([xr + dr, xi + di], axis=1)
        last = lax.broadcasted_iota(jnp.int32, xx.shape, 0) == CH - 1
        x0n.append(jnp.sum(jnp.where(last, xx, 0.0), axis=0, keepdims=True))
        xxs.append(xx)
    y = jnp.concatenate([mm(xxs[j], wc[j]) for j in blocks], axis=1)
    return (jnp.stack(x0n, axis=0),), (jax.nn.gelu(y + dsk * a),)


def _s5_args(a, tb, dsk):
    lp = a.shape[0]
    wb, wc, apow, ptab = tb
    ps = [Arg(wb, (S5_BB, 128, 2 * S5_W), lambda o, t: (o, 0, 0)), Arg(wc, (S5_BB, 2 * S5_W, 128), lambda o, t: (o, 0, 0)),
          Arg(apow, (S5_BB, 6, 1, 2 * S5_W), lambda o, t: (o, 0, 0, 0)),
          Arg(ptab, (S5_BB, CH, 2 * S5_W), lambda o, t: (o, 0, 0)), Arg(dsk, (1, 128 * S5_BB), lambda o, t: (0, o))]
    xs = [Arg(a, (CH, 128 * S5_BB), lambda o, t: (t, o))]
    return (N_DEV // S5_BB, lp // CH), ps, xs


def _glu_res_fn(p, c, k, x):
    h, vg = x
    return (), ((h + vg[:, :D] * jax.nn.sigmoid(vg[:, D:])) * c[0],)


def _glu_args(h, vg, mask):
    lp = h.shape[0]
    tr = _row_tile(lp)
    row = lambda o, t: (t, 0)
    return (1, lp // tr), [Arg(mask, (tr, 1), row)], [Arg(h, (tr, D), row), Arg(vg, (tr, 2 * D), row)], tr


def s5_fwd(h, mask, w):
    lp = h.shape[0]
    a = norm_fwd("s5_norm", h, w["ng"], mask, out_dtype=F32)
    grid, ps, xs = _s5_args(a, w["tb"], w["dsk"])
    (z,), (st,) = seq_fwd("s5_core", _s5_fn, grid, ps, [], xs,
                          [((lp, D), BF, (CH, 128 * S5_BB), lambda o, t: (t, o))], carries=[(S5_BB, 1, 2 * S5_W)])
    vg = lin("s5_glu", z, w["wglu"])
    grid2, cs, xs2, tr = _glu_args(h, vg, mask)
    (h2,), _ = seq_fwd("s5_res", _glu_res_fn, grid2, [], cs, xs2, [((lp, D), F32, (tr, D), lambda o, t: (t, 0))])
    return h2, (h, a, z, vg, st)


def s5_bwd(dh2, mask, w, saved):
    h, a, z, vg, st = saved
    g = {}
    grid2, cs, xs2, tr = _glu_args(h, vg, mask)
    (dskip, dvg), _ = seq_bwd("s5_res_b", _glu_res_fn, grid2, [], cs, xs2, [Arg(dh2, (tr, D), lambda o, t: (t, 0))])
    g["wglu"] = wgrad("s5_dwglu", z, dvg)
    dz = lin_t("s5_dz", dvg, w["wglu"])
    grid, ps, xs = _s5_args(a, w["tb"], w["dsk"])
    (da,), dps = seq_bwd("s5_core_b", _s5_fn, grid, ps, [], xs, [Arg(dz, (CH, 128 * S5_BB), lambda o, t: (t, o))],
                         saved=[st], carries=[(S5_BB, 1, 2 * S5_W)])
    g["tb"] = tuple(dps[:4])
    g["dsk"] = dps[4]
    dh, g["ng"] = norm_bwd("s5_norm_b", h, w["ng"], mask, da, dskip)
    return dh, g


def ret_consts(lp):
    pos = jnp.maximum(jnp.arange(lp, dtype=F32) - PAD, 0.0)
    inv = 1.0 / (ROPE_BASE ** (jnp.arange(0, RET_DK, 2, dtype=F32) / RET_DK))
    ang = pos[:, None] * inv[None, :]
    lg = jnp.log(1.0 - jnp.exp2(-5.0 - jnp.arange(RET_H, dtype=F32)))
    p = jnp.arange(CH, dtype=F32)
    diff = p[:, None] - p[None, :]
    decay = jnp.where(diff >= 0, jnp.exp(diff[None] * lg[:, None, None]), 0.0)
    qd = jnp.exp((p[None, :] + 1.0) * lg[:, None])[..., None]
    kd = jnp.exp((CH - 1.0 - p[None, :]) * lg[:, None])[..., None]
    cd = jnp.exp(CH * lg)[:, None, None]
    return jnp.cos(ang), jnp.sin(ang), decay, qd, kd, cd


def _ret_fn(p, c, k, x):
    (gn,) = p
    cos, sin, decay, qd, kd, cd = c
    (st,) = k
    (u,) = x
    hd = RET_DK // 2
    qk_w = RET_H * RET_DK
    heads = range(RET_H)

    def rope(t):
        t1, t2 = t[:, :hd], t[:, hd:]
        return jnp.concatenate([t1 * cos - t2 * sin, t1 * sin + t2 * cos], axis=1)

    qr = [rope(u[:, RET_DK * h:RET_DK * (h + 1)]) for h in heads]
    kr = [rope(u[:, qk_w + RET_DK * h:qk_w + RET_DK * (h + 1)]) * (RET_DK ** -0.5) for h in heads]
    v = [u[:, 2 * qk_w + RET_DV * h:2 * qk_w + RET_DV * (h + 1)] for h in heads]
    scores = [mm_nt(qr[h], kr[h]) for h in heads]
    inter = [mm(qr[h] * qd[h], st[h]) for h in heads]
    st_new = jnp.stack([st[h] * cd[h] + mm_tn(kr[h] * kd[h], v[h]) for h in heads], axis=0)
    o = [mm(scores[h] * decay[h], v[h]) + inter[h] for h in heads]
    zs = []
    for h in heads:
        mu = jnp.mean(o[h], axis=-1, keepdims=True)
        var = jnp.mean(jnp.square(o[h] - mu), axis=-1, keepdims=True)
        gate = u[:, 2 * qk_w + RET_H * RET_DV + RET_DV * h:2 * qk_w + RET_H * RET_DV + RET_DV * (h + 1)]
        zs.append((o[h] - mu) * lax.rsqrt(var + EPS) * gn[:, RET_DV * h:RET_DV * (h + 1)] * _silu(gate))
    return (st_new,), (jnp.concatenate(zs, axis=1),)


def _ret_args(u, gn, rc):
    lp, uw = u.shape
    cos, sin, decay, qd, kd, cd = rc
    full = lambda o, t: (0, 0, 0)
    ps = [Arg(gn, (1, RET_H * RET_DV), lambda o, t: (0, 0))]
    cs = [Arg(cos, (CH, RET_DK // 2), lambda o, t: (t, 0)), Arg(sin, (CH, RET_DK // 2), lambda o, t: (t, 0)),
          Arg(decay, (RET_H, CH, CH), full), Arg(qd, (RET_H, CH, 1), full), Arg(kd, (RET_H, CH, 1), full),
          Arg(cd, (RET_H, 1, 1), full)]
    xs = [Arg(u, (CH, uw), lambda o, t: (t, 0))]
    return (1, lp // CH), ps, cs, xs


def ret_fwd(h, mask, w):
    lp = h.shape[0]
    a = norm_fwd("ret_norm", h, w["ng"], mask)
    u = lin("ret_in", a, w["win"])
    grid, ps, cs, xs = _ret_args(u, w["gn"], w["rc"])
    (z,), (st,) = seq_fwd("ret_core", _ret_fn, grid, ps, cs, xs,
                          [((lp, 2 * D), BF, (CH, RET_H * RET_DV), lambda o, t: (t, 0))],
                          carries=[(RET_H, RET_DK, RET_DV)], save_dtype=BF)
    h2 = lin("ret_out", z, w["wo"], res=h)
    return h2, (h, a, u, z, st)


def ret_bwd(dh2, mask, w, saved):
    h, a, u, z, st = saved
    g = {}
    g["wo"] = wgrad("ret_dwo", z, dh2)
    dz = lin_t("ret_dz", dh2, w["wo"])
    grid, ps, cs, xs = _ret_args(u, w["gn"], w["rc"])
    (du,), (g["gn"],) = seq_bwd("ret_core_b", _ret_fn, grid, ps, cs, xs,
                                [Arg(dz, (CH, RET_H * RET_DV), lambda o, t: (t, 0))], saved=[st],
                                carries=[(RET_H, RET_DK, RET_DV)])
    g["win"] = wgrad("ret_dwin", a, du)
    da = lin_t("ret_da", du, w["win"])
    dh, g["ng"] = norm_bwd("ret_norm_b", h, w["ng"], mask, da, dh2)
    return dh, g


def mla_consts(lp):
    pos = jnp.maximum(jnp.arange(lp, dtype=F32) - PAD, 0.0)
    inv = 1.0 / (ROPE_BASE ** (jnp.arange(0, MLA_ROPE, 2, dtype=F32) / MLA_ROPE))
    ang = pos[:, None] * inv[None, :]
    cos = jnp.concatenate([jnp.cos(ang), jnp.cos(ang)], axis=1)
    sin = jnp.concatenate([jnp.sin(ang), jnp.sin(ang)], axis=1)
    hd = MLA_ROPE // 2
    r = lax.broadcasted_iota(jnp.int32, (MLA_ROPE, MLA_ROPE), 0)
    c = lax.broadcasted_iota(jnp.int32, (MLA_ROPE, MLA_ROPE), 1)
    rot = jnp.where(r == c + hd, -1.0, jnp.where(c == r + hd, 1.0, 0.0)).astype(F32)
    return cos, sin, rot


def _mla_prep1_fn(p, c, k, x):
    gq, gkv = p
    (down,) = x
    return (), (_rms(down[:, :MLA_QL], gq), _rms(down[:, MLA_QL:MLA_QL + MLA_KVL], gkv), down[:, MLA_QL + MLA_KVL:])


def _mla_prep2(p, c, x):
    gq, gk = p
    cos, sin, rot = c
    q, kv, kpe = x
    qn = _rms(q, gq)
    qn_n, qn_r = qn[:, :MLA_NOPE], qn[:, MLA_NOPE:]
    qo = jnp.concatenate([qn_n, qn_r * cos + cright(qn_r, rot) * sin], axis=1)
    kn = kv[:, :MLA_NOPE]
    ms = (jnp.sum(kn * kn, axis=-1, keepdims=True) + jnp.sum(kpe * kpe, axis=-1, keepdims=True)) / MLA_QK
    r = lax.rsqrt(ms + EPS)
    kr = kpe * r * gk[:, MLA_NOPE:]
    ko = jnp.concatenate([kn * r * gk[:, :MLA_NOPE], kr * cos + cright(kr, rot) * sin], axis=1)
    return qo, ko, kv[:, MLA_NOPE:]


def _mla_prep2_fn(p, c, k, x):
    return (), _mla_prep2(p, c, x)[:2]


def _mla_prep2_b_fn(p, c, k, x):
    return (), _mla_prep2(p, c, x)


def _prep1_args(down, gq, gkv):
    lp = down.shape[0]
    tr = _row_tile(lp)
    ps = [Arg(gq, (1, MLA_QL), lambda o, t: (0, 0), shared=True), Arg(gkv, (1, MLA_KVL), lambda o, t: (0, 0), shared=True)]
    return (1, lp // tr), ps, [Arg(down, (tr, down.shape[1]), lambda o, t: (t, 0))], tr


def _prep2_args(qraw, kvraw, kpe, gq, gk, mc):
    lp = kpe.shape[0]
    tr = _row_tile(lp)
    cos, sin, rot = mc
    ps = [Arg(gq, (1, MLA_QK), lambda o, t: (0, 0), shared=True), Arg(gk, (1, MLA_QK), lambda o, t: (0, 0), shared=True)]
    cs = [Arg(cos, (tr, MLA_ROPE), lambda o, t: (o, 0)), Arg(sin, (tr, MLA_ROPE), lambda o, t: (o, 0)),
          Arg(rot, (MLA_ROPE, MLA_ROPE), lambda o, t: (0, 0))]
    xs = [Arg(qraw, (None, tr, MLA_QK), lambda o, t: (t, o, 0)), Arg(kvraw, (None, tr, MLA_NOPE + MLA_V), lambda o, t: (t, o, 0)),
          Arg(kpe, (tr, MLA_ROPE), lambda o, t: (o, 0), acc=True)]
    return (lp // tr, MLA_H), ps, cs, xs, tr


def _attn_tile(lp):
    return 832 if (lp % 832 == 0 and lp > 832) else 64


def _attn_mask(qi, ki, ta):
    rows = qi * ta + lax.broadcasted_iota(jnp.int32, (ta, ta), 0)
    cols = ki * ta + lax.broadcasted_iota(jnp.int32, (ta, ta), 1)
    return (cols >= PAD) & ((cols // CH) <= (rows // CH))


def attn_fwd(q, k, kv):
    nh, lp, dq = q.shape
    ta = _attn_tile(lp)
    nb = lp // ta
    scale = MLA_QK ** -0.5

    def body(q_ref, k_ref, v_ref, o_ref, lse_ref, m_s, l_s, acc_s):
        qi, ki = pl.program_id(1), pl.program_id(2)

        @pl.when(ki == 0)
        def _():
            m_s[...] = jnp.full_like(m_s, NEG)
            l_s[...] = jnp.zeros_like(l_s)
            acc_s[...] = jnp.zeros_like(acc_s)

        def step(masked):
            s = _bdot(q_ref[...], k_ref[...], 1, 1) * scale
            if masked:
                s = jnp.where(_attn_mask(qi, ki, ta), s, NEG)
            m_new = jnp.maximum(m_s[...], jnp.max(s, axis=-1, keepdims=True))
            p = jnp.exp(s - m_new)
            alpha = jnp.exp(m_s[...] - m_new)
            l_s[...] = alpha * l_s[...] + jnp.sum(p, axis=-1, keepdims=True)
            acc_s[...] = alpha * acc_s[...] + _bdot(p, v_ref[...], 1, 0)
            m_s[...] = m_new

        pl.when((ki == qi) | (ki == 0))(functools.partial(step, True))
        pl.when((ki < qi) & (ki > 0))(functools.partial(step, False))

        @pl.when(ki == nb - 1)
        def _():
            o_ref[...] = (acc_s[...] / l_s[...]).astype(o_ref.dtype)
            lse_ref[...] = m_s[...] + jnp.log(l_s[...])

    return pl.pallas_call(
        body, name="mla_attn", grid=(nh, nb, nb),
        in_specs=[pl.BlockSpec((None, ta, dq), lambda h, qi, ki: (h, qi, 0)),
                  pl.BlockSpec((None, ta, dq), lambda h, qi, ki: (h, jnp.minimum(ki, qi), 0)),
                  pl.BlockSpec((None, ta, MLA_V), lambda h, qi, ki: (h, jnp.minimum(ki, qi), 1))],
        out_specs=[pl.BlockSpec((ta, MLA_V), lambda h, qi, ki: (qi, h)),
                   pl.BlockSpec((None, ta, 1), lambda h, qi, ki: (h, qi, 0))],
        out_shape=[SDS((lp, nh * MLA_V), BF), SDS((nh, lp, 1), F32)],
        scratch_shapes=[pltpu.VMEM((ta, 1), F32), pltpu.VMEM((ta, 1), F32), pltpu.VMEM((ta, MLA_V), F32)],
        compiler_params=_cp(3))(q, k, kv)


def attn_bwd(q, k, kv, o, do, lse):
    nh, lp, dq = q.shape
    ta = _attn_tile(lp)
    nb = lp // ta
    scale = MLA_QK ** -0.5

    def body(q_ref, k_ref, v_ref, o_ref, do_ref, lse_ref, dq_ref, dk_ref, dv_ref, dk_s, dv_s):
        ki, qi = pl.program_id(1), pl.program_id(2)

        @pl.when((ki == 0) & (qi == 0))
        def _():
            dq_ref[...] = jnp.zeros_like(dq_ref)

        @pl.when(qi == 0)
        def _():
            dk_s[...] = jnp.zeros_like(dk_s)
            dv_s[...] = jnp.zeros_like(dv_s)

        def step(masked):
            dov = do_ref[...]
            s = _bdot(q_ref[...], k_ref[...], 1, 1) * scale
            dp = _bdot(dov, v_ref[...], 1, 1)
            if masked:
                s = jnp.where(_attn_mask(qi, ki, ta), s, NEG)
            p = jnp.exp(s - lse_ref[...])
            delta = jnp.sum(dov * o_ref[...].astype(F32), axis=-1, keepdims=True)
            dv_s[...] += _bdot(p, dov, 0, 0)
            ds = p * (dp - delta) * scale
            rows = pl.ds(pl.multiple_of(qi * ta, ta), ta)
            dq_ref[rows, :] += _bdot(ds, k_ref[...], 1, 0)
            dk_s[...] += _bdot(ds, q_ref[...], 0, 0)

        pl.when((ki == qi) | (ki == 0))(functools.partial(step, True))
        pl.when((ki < qi) & (ki > 0))(functools.partial(step, False))

        @pl.when(qi == nb - 1)
        def _():
            dk_ref[...] = dk_s[...]
            dv_ref[...] = dv_s[...]

    qmap = lambda h, ki, qi: (h, jnp.maximum(qi, ki), 0)
    return pl.pallas_call(
        body, name="mla_attn_b", grid=(nh, nb, nb),
        in_specs=[pl.BlockSpec((None, ta, dq), qmap),
                  pl.BlockSpec((None, ta, dq), lambda h, ki, qi: (h, ki, 0)),
                  pl.BlockSpec((None, ta, MLA_V), lambda h, ki, qi: (h, ki, 1)),
                  pl.BlockSpec((ta, MLA_V), lambda h, ki, qi: (jnp.maximum(qi, ki), h)),
                  pl.BlockSpec((ta, MLA_V), lambda h, ki, qi: (jnp.maximum(qi, ki), h)),
                  pl.BlockSpec((None, ta, 1), qmap)],
        out_specs=[pl.BlockSpec((None, lp, dq), lambda h, ki, qi: (h, 0, 0)),
                   pl.BlockSpec((None, ta, dq), lambda h, ki, qi: (h, ki, 0)),
                   pl.BlockSpec((None, ta, MLA_V), lambda h, ki, qi: (h, ki, 0))],
        out_shape=[SDS((nh, lp, dq), F32), SDS((nh, lp, dq), F32), SDS((nh, lp, MLA_V), F32)],
        scratch_shapes=[pltpu.VMEM((ta, dq), F32), pltpu.VMEM((ta, MLA_V), F32)],
        compiler_params=_cp(3))(q, k, kv, o, do, lse)


def mla_fwd(h, mask, w):
    lp = h.shape[0]
    a = norm_fwd("mla_norm", h, w["ng"], mask)
    down = lin("mla_down", a, w["wdown"])
    grid, ps, xs, tr = _prep1_args(down, w["gcq"], w["gckv"])
    row = lambda o, t: (t, 0)
    (cq, ckv, kpe), _ = seq_fwd("mla_prep1", _mla_prep1_fn, grid, ps, [], xs,
                                [((lp, MLA_QL), BF, (tr, MLA_QL), row), ((lp, MLA_KVL), BF, (tr, MLA_KVL), row),
                                 ((lp, MLA_ROPE), F32, (tr, MLA_ROPE), row)])
    qraw = lin_bo("mla_uq", cq, w["wuq"])
    kvraw = lin_bo("mla_ukv", ckv, w["wukv"])
    grid, ps, cs, xs, tr = _prep2_args(qraw, kvraw, kpe, w["gq"], w["gk"], w["mc"])
    hm = lambda o, t: (t, o, 0)
    (q, k), _ = seq_fwd("mla_prep2", _mla_prep2_fn, grid, ps, cs, xs,
                        [((MLA_H, lp, MLA_QK), BF, (None, tr, MLA_QK), hm), ((MLA_H, lp, MLA_QK), BF, (None, tr, MLA_QK), hm)])
    o, lse = attn_fwd(q, k, kvraw)
    h2 = lin("mla_out", o, w["wo"], res=h)
    return h2, (h, a, down, cq, ckv, kpe, qraw, kvraw, q, k, o, lse)


def mla_bwd(dh2, mask, w, saved):
    h, a, down, cq, ckv, kpe, qraw, kvraw, q, k, o, lse = saved
    lp = h.shape[0]
    g = {}
    g["wo"] = wgrad("mla_dwo", o, dh2)
    do = lin_t("mla_do", dh2, w["wo"])
    dq, dk, dv = attn_bwd(q, k, kvraw, o, do, lse)
    grid, ps, cs, xs, tr = _prep2_args(qraw, kvraw, kpe, w["gq"], w["gk"], w["mc"])
    hm = lambda o, t: (t, o, 0)
    (dqraw, dkvraw, dkpe), (g["gq"], g["gk"]) = seq_bwd(
        "mla_prep2_b", _mla_prep2_b_fn, grid, ps, cs, xs,
        [Arg(dq, (None, tr, MLA_QK), hm), Arg(dk, (None, tr, MLA_QK), hm), Arg(dv, (None, tr, MLA_V), hm)])
    g["wuq"] = wgrad_bo("mla_dwuq", cq, dqraw)
    dcq = lin_t_bi("mla_dcq", dqraw, w["wuq"])
    g["wukv"] = wgrad_bo("mla_dwukv", ckv, dkvraw)
    dckv = lin_t_bi("mla_dckv", dkvraw, w["wukv"])
    grid, ps, xs, tr = _prep1_args(down, w["gcq"], w["gckv"])
    row = lambda o, t: (t, 0)
    (ddown,), (g["gcq"], g["gckv"]) = seq_bwd(
        "mla_prep1_b", _mla_prep1_fn, grid, ps, [], xs,
        [Arg(dcq, (tr, MLA_QL), row), Arg(dckv, (tr, MLA_KVL), row), Arg(dkpe, (tr, MLA_ROPE), row)])
    g["wdown"] = wgrad("mla_dwdown", a, ddown)
    da = lin_t("mla_da", ddown, w["wdown"])
    dh, g["ng"] = norm_bwd("mla_norm_b", h, w["ng"], mask, da, dh2)
    return dh, g


def loss_head(h, target):
    lp, d = h.shape
    assert OFF == CH

    def body(h_ref, t_ref, loss_ref, dh_ref):
        i = pl.program_id(0)

        @pl.when(i == 0)
        def _():
            loss_ref[...] = jnp.zeros_like(loss_ref)

        e = jnp.where(i > 0, h_ref[...] - t_ref[...], 0.0)
        loss_ref[...] += jnp.sum(e * e) * (0.5 / d)
        dh_ref[...] = e * (1.0 / d)

    return pl.pallas_call(
        body, name="loss_head", grid=(lp // CH,),
        in_specs=[pl.BlockSpec((CH, d), lambda i: (i, 0)), pl.BlockSpec((CH, d), lambda i: (jnp.maximum(i - 1, 0), 0))],
        out_specs=[pl.BlockSpec((8, 128), lambda i: (0, 0)), pl.BlockSpec((CH, d), lambda i: (i, 0))],
        out_shape=[SDS((8, 128), F32), SDS((lp, d), F32)], compiler_params=_cp(1))(h, target)


ADAM_LAND_BYTES = 20 * 1024 * 1024


def _adam_tile(r, c, nl):
    if r % 8:
        return r
    best = 8
    for t in range(8, r + 1, 8):
        if r % t == 0 and N_DEV * t * c * 4 * 2 * nl <= ADAM_LAND_BYTES:
            best = t
    return best


def adamw(name, lands, w, m, v):
    nl, r, c = w.shape
    tr = _adam_tile(r, c, nl)
    c1 = 1.0 / (1.0 - ADAM_B1 ** ADAM_STEP)
    c2 = 1.0 / (1.0 - ADAM_B2 ** ADAM_STEP)

    def body(*refs):
        l_refs = refs[:nl]
        w_ref, m_ref, v_ref, g_ref, d_ref, nm_ref, nv_ref = refs[nl:]
        layer = pl.program_id(0)
        for j in range(nl):
            @pl.when(layer == j)
            def _(j=j):
                g = l_refs[j][0]
                for i in range(1, N_DEV):
                    g = g + l_refs[j][i]
                g_ref[...] = g

        g = g_ref[...]
        nm = ADAM_B1 * m_ref[...] + (1.0 - ADAM_B1) * g
        nv = ADAM_B2 * v_ref[...] + (1.0 - ADAM_B2) * (g * g)
        nm_ref[...] = nm
        nv_ref[...] = nv
        d_ref[...] = -ADAM_LR * ((nm * c1) / (jnp.sqrt(nv * c2) + ADAM_EPS) + ADAM_WD * w_ref[...])

    blk = pl.BlockSpec((None, tr, c), lambda l, i: (l, i, 0))
    land_specs = [pl.BlockSpec((N_DEV, tr, c), lambda l, i, j=j: (0, jnp.where(l == j, i, 0), 0)) for j in range(nl)]
    return pl.pallas_call(
        body, name=name, grid=(nl, r // tr), in_specs=land_specs + [blk, blk, blk],
        out_specs=[blk, blk, blk, blk], out_shape=[SDS((nl, r, c), F32)] * 4, compiler_params=_cp(2))(*lands, w, m, v)


ANY = pl.BlockSpec(memory_space=pl.ANY)
MESH = pl.DeviceIdType.MESH


def _me():
    return lax.axis_index("x"), lax.axis_index("y"), lax.axis_index("c")


def _peers():
    x, y, c = _me()
    out = []
    for k in range(1, N_DEV):
        px = 1 - x if k & 4 else x
        py = 1 - y if k & 2 else y
        pc = 1 - c if k & 1 else c
        out.append(((px, py, pc), 4 * px + 2 * py + pc))
    return out


HBM_SPEC = pl.BlockSpec(memory_space=pltpu.HBM)
SEM_SPEC = pl.BlockSpec(memory_space=pltpu.SEMAPHORE)
DATAFLOW = pltpu.SideEffectType.DATAFLOW_SIDE_EFFECTING


def _my_index():
    return 4 * lax.axis_index("x") + 2 * lax.axis_index("y") + lax.axis_index("c")


def _hbm(a):
    return pltpu.with_memory_space_constraint(a, pltpu.HBM)


NP = N_DEV - 1


def _push_copy(x_ref, land_ref, send, recv, pid, src_idx, dst_idx, scatter):
    src = x_ref.at[src_idx] if scatter else x_ref
    return pltpu.make_async_remote_copy(src_ref=src, dst_ref=land_ref.at[dst_idx], send_sem=send, recv_sem=recv,
                                        device_id=pid, device_id_type=MESH)


def push_start(name, xs, me, scatter, carry=None):
    n = len(xs)
    lands = []
    for a in xs:
        own = lax.dynamic_index_in_dim(a, me, 0, keepdims=True) if scatter else a[None]
        z = lax.empty((N_DEV,) + own.shape[1:], a.dtype)
        lands.append(lax.dynamic_update_slice(z, own, (me,) + (0,) * (own.ndim - 1)))
    ns = 2 * NP * n
    ops = xs + lands + ([carry] if carry is not None else [])
    na = len(ops)

    def body(*refs):
        x_refs, land_refs = refs[:n], refs[n:2 * n]
        sems = refs[na:na + ns]
        token = refs[-1]
        x, y, c = _me()
        mine = 4 * x + 2 * y + c
        for i in range(n):
            for k, (pid, pidx) in enumerate(_peers()):
                s = 2 * (NP * i + k)
                _push_copy(x_refs[i], land_refs[i], sems[s], sems[s + 1], pid, pidx, mine, scatter).start()
        token[...] = jnp.zeros_like(token)

    out_shape = ([pltpu.SemaphoreType.DMA(())] * ns + [pltpu.HBM(a.shape, a.dtype) for a in ops]
                 + [SDS((8, 128), F32)])
    res = pl.pallas_call(
        body, name=name, out_shape=out_shape, in_specs=[HBM_SPEC] * na,
        out_specs=[SEM_SPEC] * ns + [HBM_SPEC] * na + [pl.BlockSpec(memory_space=pltpu.VMEM)],
        input_output_aliases={i: ns + i for i in range(na)},
        compiler_params=pltpu.CompilerParams(has_side_effects=DATAFLOW))(*[_hbm(a) for a in ops])
    sems, thru, token = res[:ns], res[ns:-1], res[-1]
    handles = [dict(x=thru[i], land=thru[n + i], sems=list(sems[2 * NP * i:2 * NP * (i + 1)]), token=token)
               for i in range(n)]
    return (handles, thru[2 * n]) if carry is not None else handles


def push_wait(name, hds, after, scatter):
    n = len(hds)
    ns = 2 * NP

    def body(*refs):
        x_refs, land_refs = refs[:n], refs[n:2 * n]
        sems = refs[2 * n:2 * n + ns * n]
        for i in range(n):
            for k, (pid, pidx) in enumerate(_peers()):
                cp = _push_copy(x_refs[i], land_refs[i], sems[ns * i + 2 * k], sems[ns * i + 2 * k + 1], pid, pidx, pidx,
                                scatter)
                cp.wait_send()
                cp.wait_recv()

    arrs = [hd["x"] for hd in hds] + [hd["land"] for hd in hds]
    sems = [s for hd in hds for s in hd["sems"]]
    res = pl.pallas_call(
        body, name=name, out_shape=[pltpu.HBM(a.shape, a.dtype) for a in arrs],
        in_specs=[HBM_SPEC] * (2 * n) + [SEM_SPEC] * (ns * n) + [ANY], out_specs=[HBM_SPEC] * (2 * n),
        input_output_aliases={i: i for i in range(2 * n)},
        compiler_params=pltpu.CompilerParams(has_side_effects=DATAFLOW))(*arrs, *sems, after)
    return list(res[n:])


WEIGHTS = ['meta_tokens', 'norm_mix_g', 'norm_ffn_g', 'mla_w_down', 'mla_cq_norm_g', 'mla_ckv_norm_g', 'mla_w_uq',
           'mla_w_ukv', 'mla_q_head_g', 'mla_k_head_g', 'mla_w_o', 'hgrn_w_in', 'hgrn_lb_logits', 'hgrn_o_norm_g',
           'hgrn_w_o', 's5_lam_re', 's5_lam_im', 's5_log_dt', 's5_b_re', 's5_b_im', 's5_c_re', 's5_c_im', 's5_d',
           's5_w_glu', 'ret_w_in', 'ret_gn_g', 'ret_w_o', 'ffn_w_up', 'ffn_conv_w', 'ffn_conv_b', 'ffn_w_down']
BIG = ['mla_w_down', 'mla_w_uq', 'mla_w_ukv', 'mla_w_o', 'hgrn_w_in', 'hgrn_w_o', 's5_w_glu', 'ret_w_in', 'ret_w_o',
       'ffn_w_up', 'ffn_w_down']
SMALL_SH = ['meta_tokens', 's5_d', 'ret_gn_g', 'ffn_conv_w']
SMALL_REP = ['norm_mix_g', 'norm_ffn_g', 'mla_cq_norm_g', 'mla_ckv_norm_g', 'mla_q_head_g', 'mla_k_head_g',
             'hgrn_lb_logits', 'hgrn_o_norm_g', 's5_lam_re', 's5_lam_im', 's5_log_dt', 's5_b_re', 's5_b_im',
             's5_c_re', 's5_c_im', 'ffn_conv_b']
LANE = 128


def _flat(arrs, mult):
    v = jnp.concatenate([a.reshape(-1) for a in arrs])
    pad = (-v.shape[0]) % mult
    return jnp.pad(v, (0, pad)).reshape(-1, LANE)


def _unflat(flat2d, like):
    v = flat2d.reshape(-1)
    out, o = [], 0
    for a in like:
        out.append(v[o:o + a.size].reshape(a.shape))
        o += a.size
    return out


def _lb_of(logits):
    cum = jnp.cumsum(jax.nn.softmax(logits, axis=0), axis=0)
    return (cum - cum[0:1])[1:2]


def _cols_to_blocks(g):
    k, n = g.shape
    return g.reshape(k, N_DEV, n // N_DEV).transpose(1, 0, 2)


def _blocks_to_cols(wb):
    nb, k, n = wb.shape
    return wb.transpose(1, 0, 2).reshape(k, nb * n)


SUBS = ['mla', 'ffn0', 'hgrn', 'ffn1', 's5', 'ffn2', 'ret', 'ffn3']
GROUPS = [[('mla_w_down', 0), ('mla_w_uq', 0), ('mla_w_ukv', 0), ('mla_w_o', 0)],
          [('ffn_w_up', 0), ('ffn_w_down', 0)],
          [('hgrn_w_in', 0), ('hgrn_w_o', 0)],
          [('ffn_w_up', 1), ('ffn_w_down', 1)],
          [('s5_w_glu', 0)],
          [('ffn_w_up', 2), ('ffn_w_down', 2)],
          [('ret_w_in', 0), ('ret_w_o', 0)],
          [('ffn_w_up', 3), ('ffn_w_down', 3)]]


def _pack8(parts, mult):
    v = jnp.concatenate(parts, axis=1)
    return jnp.pad(v, ((0, 0), (0, (-v.shape[1]) % mult))).reshape(N_DEV, -1, LANE)


def _sub_weights(k, got, rep, tabs, lp):
    ngm, ngf = rep['norm_mix_g'], rep['norm_ffn_g']
    if k == 0:
        return dict(ng=ngm[0:1], wdown=got[0].reshape(D, -1), gcq=rep['mla_cq_norm_g'], gckv=rep['mla_ckv_norm_g'],
                    wuq=got[1], wukv=got[2], gq=rep['mla_q_head_g'], gk=rep['mla_k_head_g'], wo=got[3].reshape(D, D),
                    mc=mla_consts(lp))
    if k == 2:
        return dict(ng=ngm[1:2], win=got[0], lb=tabs['lb'], go=rep['hgrn_o_norm_g'], wo=got[1].reshape(D, D))
    if k == 4:
        return dict(ng=ngm[2:3], tb=tabs['tb'], dsk=tabs['s5_d'], wglu=_blocks_to_cols(got[0]))
    if k == 6:
        return dict(ng=ngm[3:4], win=_blocks_to_cols(got[0]), gn=tabs['ret_gn_g'], wo=got[1].reshape(2 * D, D),
                    rc=ret_consts(lp))
    i = k // 2
    return dict(ng=ngf[i:i + 1], up=got[0], cw=tabs['conv_w'][:, i].reshape(2, 4, 3, 1, FFN_B),
                cb=rep['ffn_conv_b'][i].reshape(2, 4, 1, FFN_B), down=got[1].reshape(4, FFN_B, D))


def _sub_grad_blocks(k, g):
    if k == 0:
        parts = [g['wdown'], g['wuq'], g['wukv'], g['wo']]
    elif k == 2:
        parts = [g['win'], g['wo']]
    elif k == 4:
        parts = [_cols_to_blocks(g['wglu'])]
    elif k == 6:
        parts = [_cols_to_blocks(g['win']), g['wo']]
    else:
        parts = [g['up'], g['down']]
    return parts


_FWD = [mla_fwd, None, hgrn_fwd, None, s5_fwd, None, ret_fwd, None]
_BWD = [mla_bwd, None, hgrn_bwd, None, s5_bwd, None, ret_bwd, None]


def _step(args):
    w = {n: args[n] for n in WEIGHTS}
    x2, tgt = args['x'][0], args['loss_target'][0]

    lp = x2.shape[0] + OFF
    me = _my_index()
    mask = _rowmask(lp)
    rep = {n: w[n] for n in SMALL_REP}

    xs, slots = [], []
    for gi, grp in enumerate(GROUPS):
        items = [w[n][l].astype(BF) for n, l in grp] + ([_flat([w[n] for n in SMALL_SH], LANE)] if gi == 0 else [])
        slots.append((len(xs), len(items)))
        xs += items
    gh = push_start("gather_start", xs, me, scatter=False)

    def fetch(gi, after):
        s, cnt = slots[gi]
        return push_wait("gather_wait_" + SUBS[gi], gh[s:s + cnt], after, scatter=False)

    got = fetch(0, x2)
    sm, o, smp = got[-1].reshape(N_DEV, -1), 0, {}
    for n in SMALL_SH:
        smp[n] = sm[:, o:o + w[n].size].reshape((N_DEV,) + w[n].shape)
        o += w[n].size
    meta = smp['meta_tokens'].transpose(1, 0, 2).reshape(N_META, D)
    lb, lb_vjp = jax.vjp(_lb_of, rep['hgrn_lb_logits'])
    s5p = [rep[n][0] for n in ('s5_lam_re', 's5_lam_im', 's5_log_dt', 's5_b_re', 's5_b_im', 's5_c_re', 's5_c_im')]
    tb, tb_vjp = jax.vjp(s5_tables, *s5p)
    tabs = dict(lb=lb, tb=tb, s5_d=smp['s5_d'].reshape(1, D), ret_gn_g=smp['ret_gn_g'].reshape(1, 2 * D),
                conv_w=smp['ffn_conv_w'])
    h = jnp.concatenate([jnp.zeros((PAD, D), F32), meta, x2], axis=0)
    ws, saved = [], []
    for k in range(8):
        if k > 0:
            got = fetch(k, h)
        ws.append(_sub_weights(k, got, rep, tabs, lp))
        if k % 2:
            h, sv = ffn_fwd(k // 2, h, mask, ws[k])
        else:
            h, sv = _FWD[k](h, mask, ws[k])
        saved.append(sv)
    loss, dh = loss_head(h, tgt)

    gs, sh = [None] * 8, [None] * 8
    for k in reversed(range(8)):
        if k % 2:
            dh, gs[k] = ffn_bwd(k // 2, dh, mask, ws[k], saved[k])
        else:
            dh, gs[k] = _BWD[k](dh, mask, ws[k], saved[k])
        blocks = [b.reshape((N_DEV,) + w[n].shape[1:]) for b, (n, _) in zip(_sub_grad_blocks(k, gs[k]), GROUPS[k])]
        if k == 0:
            dmeta = dh[PAD:OFF].reshape(N_META, N_DEV, D // N_DEV).transpose(1, 0, 2)
            dcw = jnp.stack([gs[2 * i + 1]['cw'].reshape(N_DEV, 3, FFN_B) for i in range(4)], axis=1)
            blocks.append(_pack8([t.reshape(N_DEV, -1) for t in (dmeta, gs[4]['dsk'], gs[6]['gn'], dcw)], LANE))
        sh[k], dh = push_start("scatter_start_" + SUBS[k], blocks, me, scatter=True, carry=dh)
    grad_x = dh[OFF:]

    ds5 = tb_vjp(gs[4]['tb'])
    g_rep = {
        'norm_mix_g': jnp.concatenate([gs[k]['ng'] for k in (0, 2, 4, 6)], axis=0),
        'norm_ffn_g': jnp.concatenate([gs[k]['ng'] for k in (1, 3, 5, 7)], axis=0),
        'mla_cq_norm_g': gs[0]['gcq'], 'mla_ckv_norm_g': gs[0]['gckv'], 'mla_q_head_g': gs[0]['gq'],
        'mla_k_head_g': gs[0]['gk'], 'hgrn_lb_logits': lb_vjp(gs[2]['lb'])[0], 'hgrn_o_norm_g': gs[2]['go'],
        's5_lam_re': ds5[0], 's5_lam_im': ds5[1], 's5_log_dt': ds5[2], 's5_b_re': ds5[3], 's5_b_im': ds5[4],
        's5_c_re': ds5[5], 's5_c_im': ds5[6],
        'ffn_conv_b': jnp.stack([gs[k]['cb'].reshape(-1) for k in (1, 3, 5, 7)], axis=0),
    }
    grep = _flat([g_rep[n] for n in SMALL_REP], 8 * LANE)
    rh = push_start("small_grads_start", [grep], me, scatter=False)

    lands = {n: [None] * w[n].shape[0] for n in BIG}
    res = {}
    late = [n for n, _ in GROUPS[0]]
    for k in reversed(range(1, 8)):
        got = push_wait("scatter_wait_" + SUBS[k], sh[k], grep, scatter=True)
        for (n, l), t in zip(GROUPS[k], got):
            lands[n][l] = t
    for n in BIG:
        if n not in late:
            res[n] = adamw("adam_" + n, lands[n], w[n], args['m_' + n], args['v_' + n])
    after = res['ffn_w_up'][1]
    got = push_wait("scatter_wait_" + SUBS[0], sh[0], after, scatter=True)
    small_land = got[-1]
    (rep_land,) = push_wait("small_grads_wait", rh, after, scatter=False)
    for (n, _), t in zip(GROUPS[0], got):
        res[n] = adamw("adam_" + n, [t], w[n], args['m_' + n], args['v_' + n])

    def flat_adam(name, land, names, mult):
        like = [w[n] for n in names]
        out = adamw(name, [land], _flat(like, mult)[None], _flat([args['m_' + n] for n in names], mult)[None],
                    _flat([args['v_' + n] for n in names], mult)[None])
        for n, parts in zip(names, zip(*[_unflat(t, like) for t in out])):
            res[n] = list(parts)

    flat_adam("adam_small_sharded", small_land, SMALL_SH, LANE)
    flat_adam("adam_small_replicated", rep_land, SMALL_REP, 8 * LANE)

    total = lax.psum(loss[0, 0], ("x", "y", "c"))
    outs = [total, grad_x[None]]
    for k in range(4):
        outs += [res[n][k] for n in WEIGHTS]
    return tuple(outs)


def kernel(x, meta_tokens, norm_mix_g, norm_ffn_g, mla_w_down, mla_cq_norm_g, mla_ckv_norm_g, mla_w_uq, mla_w_ukv, mla_q_head_g, mla_k_head_g, mla_w_o, hgrn_w_in, hgrn_lb_logits, hgrn_o_norm_g, hgrn_w_o, s5_lam_re, s5_lam_im, s5_log_dt, s5_b_re, s5_b_im, s5_c_re, s5_c_im, s5_d, s5_w_glu, ret_w_in, ret_gn_g, ret_w_o, ffn_w_up, ffn_conv_w, ffn_conv_b, ffn_w_down, loss_target, m_meta_tokens, m_norm_mix_g, m_norm_ffn_g, m_mla_w_down, m_mla_cq_norm_g, m_mla_ckv_norm_g, m_mla_w_uq, m_mla_w_ukv, m_mla_q_head_g, m_mla_k_head_g, m_mla_w_o, m_hgrn_w_in, m_hgrn_lb_logits, m_hgrn_o_norm_g, m_hgrn_w_o, m_s5_lam_re, m_s5_lam_im, m_s5_log_dt, m_s5_b_re, m_s5_b_im, m_s5_c_re, m_s5_c_im, m_s5_d, m_s5_w_glu, m_ret_w_in, m_ret_gn_g, m_ret_w_o, m_ffn_w_up, m_ffn_conv_w, m_ffn_conv_b, m_ffn_w_down, v_meta_tokens, v_norm_mix_g, v_norm_ffn_g, v_mla_w_down, v_mla_cq_norm_g, v_mla_ckv_norm_g, v_mla_w_uq, v_mla_w_ukv, v_mla_q_head_g, v_mla_k_head_g, v_mla_w_o, v_hgrn_w_in, v_hgrn_lb_logits, v_hgrn_o_norm_g, v_hgrn_w_o, v_s5_lam_re, v_s5_lam_im, v_s5_log_dt, v_s5_b_re, v_s5_b_im, v_s5_c_re, v_s5_c_im, v_s5_d, v_s5_w_glu, v_ret_w_in, v_ret_gn_g, v_ret_w_o, v_ffn_w_up, v_ffn_conv_w, v_ffn_conv_b, v_ffn_w_down):
    return _step(dict(locals()))
```

```python
import functools
import math

import jax
import jax.numpy as jnp
import numpy as np
from jax import lax
from jax.experimental import pallas as pl
from jax.experimental.pallas import tpu as pltpu

F32 = jnp.float32
BF = jnp.bfloat16
SDS = jax.ShapeDtypeStruct

N_DEV = 8
D = 1024
N_META = 16
PAD = 48
OFF = PAD + N_META
CH = 64
EPS = 1e-6
NEG = -1e30
ROPE_BASE = 10000.0

MLA_H, MLA_NOPE, MLA_ROPE, MLA_V = 8, 128, 64, 128
MLA_QK = MLA_NOPE + MLA_ROPE
MLA_QL, MLA_KVL = 384, 256
HG_H, HG_D, HG_C = 8, 128, 16
S5_G, S5_P, S5_K = 64, 64, 16
S5_SG = 8
RET_H, RET_DK, RET_DV = 4, 256, 512
FFN_F = 2816
FFN_B = 704

ADAM_LR, ADAM_B1, ADAM_B2, ADAM_EPS, ADAM_WD, ADAM_STEP = 0.001, 0.9, 0.999, 1e-08, 0.01, 10

VMEM_LIMIT = 56 * 1024 * 1024
ARB = "arbitrary"


def _cp(n):
    return pltpu.CompilerParams(dimension_semantics=(ARB,) * n, vmem_limit_bytes=VMEM_LIMIT)


def _bdot(a, b, ca, cb):
    return lax.dot_general(a.astype(BF), b.astype(BF), (((ca,), (cb,)), ((), ())), preferred_element_type=F32)


@jax.custom_vjp
def mm(a, b):
    return _bdot(a, b, 1, 0)


@jax.custom_vjp
def mm_nt(a, b):
    return _bdot(a, b, 1, 1)


@jax.custom_vjp
def mm_tn(a, b):
    return _bdot(a, b, 0, 0)


mm.defvjp(lambda a, b: (mm(a, b), (a, b)),
          lambda r, g: (mm_nt(g, r[1]).astype(r[0].dtype), mm_tn(r[0], g).astype(r[1].dtype)))
mm_nt.defvjp(lambda a, b: (mm_nt(a, b), (a, b)),
             lambda r, g: (mm(g, r[1]).astype(r[0].dtype), mm_tn(g, r[0]).astype(r[1].dtype)))
mm_tn.defvjp(lambda a, b: (mm_tn(a, b), (a, b)),
             lambda r, g: (mm_nt(r[1], g).astype(r[0].dtype), mm(r[0], g).astype(r[1].dtype)))


def _xdot(a, b, ca, cb):
    return lax.dot_general(a, b, (((ca,), (cb,)), ((), ())), preferred_element_type=F32,
                           precision=lax.Precision.HIGHEST)


@jax.custom_vjp
def cright(x, r):
    return _xdot(x, r, 1, 0)


cright.defvjp(lambda x, r: (cright(x, r), r), lambda r, g: (_xdot(g, r, 1, 1), jnp.zeros_like(r)))


def _shift_raw(x, s):
    n = x.shape[0]
    r = lax.broadcasted_iota(jnp.int32, x.shape, 0)
    y = pltpu.roll(x, s % n, 0)
    return jnp.where((r >= s) & (r < n + s), y, 0.0)


@functools.partial(jax.custom_vjp, nondiff_argnums=(1,))
def shift_rows(x, s):
    return _shift_raw(x, s)


shift_rows.defvjp(lambda x, s: (_shift_raw(x, s), None), lambda s, _, g: (_shift_raw(g, -s),))


def _seg_shift_raw(x, s, seg, up):
    n = x.shape[0]
    r = lax.broadcasted_iota(jnp.int32, x.shape, 0) % seg
    if up:
        return jnp.where(r < seg - s, pltpu.roll(x, n - s, 0), 0.0)
    return jnp.where(r >= s, pltpu.roll(x, s, 0), 0.0)


@functools.partial(jax.custom_vjp, nondiff_argnums=(1, 2))
def seg_shift(x, s, seg):
    return _seg_shift_raw(x, s, seg, False)


seg_shift.defvjp(lambda x, s, seg: (_seg_shift_raw(x, s, seg, False), None),
                 lambda s, seg, _, g: (_seg_shift_raw(g, s, seg, True),))


def _seg_cumsum(x, seg):
    s = 1
    while s < seg:
        x = x + seg_shift(x, s, seg)
        s *= 2
    return x


def _rms(x, g):
    return x * lax.rsqrt(jnp.mean(x * x, axis=-1, keepdims=True) + EPS) * g


def _silu(x):
    return x * jax.nn.sigmoid(x)


def _mm_call(name, a, b, *, grid, a_spec, b_spec, o_shape, o_spec, dims, acc_shape, res=None, res_spec=None,
             mask_tm=None):
    nk = grid[2]

    def body(*refs):
        if res is None:
            a_ref, b_ref, o_ref = refs[:3]
        else:
            a_ref, b_ref, r_ref, o_ref = refs[:4]
        k = pl.program_id(2)

        def dot():
            return lax.dot_general(a_ref[...].astype(BF), b_ref[...].astype(BF), dims, preferred_element_type=F32)

        def finish(v):
            if res is not None:
                v = v + r_ref[...].astype(F32)
                rows = pl.program_id(0) * mask_tm + lax.broadcasted_iota(jnp.int32, v.shape, 0)
                v = jnp.where(rows >= PAD, v, 0.0)
            o_ref[...] = v.astype(o_ref.dtype)

        if nk == 1:
            finish(dot())
            return
        acc = refs[-1]

        @pl.when(k == 0)
        def _():
            acc[...] = dot()

        @pl.when((k > 0) & (k < nk - 1))
        def _():
            acc[...] += dot()

        @pl.when(k == nk - 1)
        def _():
            finish(acc[...] + dot())

    ins = [a, b] + ([res] if res is not None else [])
    specs = [a_spec, b_spec] + ([res_spec] if res is not None else [])
    scratch = [pltpu.VMEM(acc_shape, F32)] if nk > 1 else []
    return pl.pallas_call(body, name=name, grid=grid, in_specs=specs, out_specs=o_spec, out_shape=o_shape,
                          scratch_shapes=scratch, compiler_params=_cp(3))(*ins)


NN = (((1,), (0,)), ((), ()))
NT = (((1,), (1,)), ((), ()))
TN = (((0,), (0,)), ((), ()))


def _row_tile(lp):
    for t in (832, 640, 320, 64):
        if lp % t == 0:
            return t
    raise ValueError(lp)


def _col_tile(n):
    for t in (1024, 768, 512, 384, 256, 128):
        if n % t == 0:
            return t
    return n


def lin(name, a, w, out_dtype=F32, res=None):
    m, k = a.shape
    n = w.shape[1]
    tm, tn, tc = _row_tile(m), _col_tile(n), _col_tile(k)
    return _mm_call(name, a, w, grid=(m // tm, n // tn, k // tc),
                    a_spec=pl.BlockSpec((tm, tc), lambda i, j, kk: (i, kk)),
                    b_spec=pl.BlockSpec((tc, tn), lambda i, j, kk: (kk, j)),
                    o_shape=SDS((m, n), out_dtype), o_spec=pl.BlockSpec((tm, tn), lambda i, j, kk: (i, j)),
                    dims=NN, acc_shape=(tm, tn), res=res,
                    res_spec=pl.BlockSpec((tm, tn), lambda i, j, kk: (i, j)), mask_tm=tm)


def lin_bo(name, a, wb, out_dtype=F32):
    m, k = a.shape
    nb, _, n = wb.shape
    tm = _row_tile(m)
    return _mm_call(name, a, wb, grid=(m // tm, nb, 1),
                    a_spec=pl.BlockSpec((tm, k), lambda i, j, kk: (i, 0)),
                    b_spec=pl.BlockSpec((None, k, n), lambda i, j, kk: (j, 0, 0)),
                    o_shape=SDS((nb, m, n), out_dtype), o_spec=pl.BlockSpec((None, tm, n), lambda i, j, kk: (j, i, 0)),
                    dims=NN, acc_shape=(tm, n))


def lin_bi(name, ab, wb, out_dtype=F32, res=None):
    nb, m, k = ab.shape
    n = wb.shape[2]
    tm, tn = _row_tile(m), _col_tile(n)
    return _mm_call(name, ab, wb, grid=(m // tm, n // tn, nb),
                    a_spec=pl.BlockSpec((None, tm, k), lambda i, j, kk: (kk, i, 0)),
                    b_spec=pl.BlockSpec((None, k, tn), lambda i, j, kk: (kk, 0, j)),
                    o_shape=SDS((m, n), out_dtype), o_spec=pl.BlockSpec((tm, tn), lambda i, j, kk: (i, j)),
                    dims=NN, acc_shape=(tm, tn), res=res,
                    res_spec=pl.BlockSpec((tm, tn), lambda i, j, kk: (i, j)), mask_tm=tm)


def lin_t(name, g, w, out_dtype=F32):
    m, n = g.shape
    k = w.shape[0]
    tm, tk, tc = _row_tile(m), _col_tile(k), _col_tile(n)
    return _mm_call(name, g, w, grid=(m // tm, k // tk, n // tc),
                    a_spec=pl.BlockSpec((tm, tc), lambda i, j, kk: (i, kk)),
                    b_spec=pl.BlockSpec((tk, tc), lambda i, j, kk: (j, kk)),
                    o_shape=SDS((m, k), out_dtype), o_spec=pl.BlockSpec((tm, tk), lambda i, j, kk: (i, j)),
                    dims=NT, acc_shape=(tm, tk))


def lin_t_bi(name, gb, wb, out_dtype=F32):
    nb, m, n = gb.shape
    k = wb.shape[1]
    tm, tk = _row_tile(m), _col_tile(k)
    return _mm_call(name, gb, wb, grid=(m // tm, k // tk, nb),
                    a_spec=pl.BlockSpec((None, tm, n), lambda i, j, kk: (kk, i, 0)),
                    b_spec=pl.BlockSpec((None, tk, n), lambda i, j, kk: (kk, j, 0)),
                    o_shape=SDS((m, k), out_dtype), o_spec=pl.BlockSpec((tm, tk), lambda i, j, kk: (i, j)),
                    dims=NT, acc_shape=(tm, tk))


def lin_t_bo(name, g, wb, out_dtype=F32):
    m, n = g.shape
    nb, k, _ = wb.shape
    tm = _row_tile(m)
    return _mm_call(name, g, wb, grid=(m // tm, nb, 1),
                    a_spec=pl.BlockSpec((tm, n), lambda i, j, kk: (i, 0)),
                    b_spec=pl.BlockSpec((None, k, n), lambda i, j, kk: (j, 0, 0)),
                    o_shape=SDS((nb, m, k), out_dtype), o_spec=pl.BlockSpec((None, tm, k), lambda i, j, kk: (j, i, 0)),
                    dims=NT, acc_shape=(tm, k))


def wgrad(name, a, g):
    m, k = a.shape
    n = g.shape[1]
    tm, tn = _row_tile(m), _col_tile(n)
    return _mm_call(name, a, g, grid=(1, n // tn, m // tm),
                    a_spec=pl.BlockSpec((tm, k), lambda i, j, kk: (kk, 0)),
                    b_spec=pl.BlockSpec((tm, tn), lambda i, j, kk: (kk, j)),
                    o_shape=SDS((k, n), F32), o_spec=pl.BlockSpec((k, tn), lambda i, j, kk: (0, j)),
                    dims=TN, acc_shape=(k, tn))


def wgrad_bo(name, a, gb):
    m, k = a.shape
    nb, _, n = gb.shape
    tm = _row_tile(m)
    return _mm_call(name, a, gb, grid=(nb, 1, m // tm),
                    a_spec=pl.BlockSpec((tm, k), lambda i, j, kk: (kk, 0)),
                    b_spec=pl.BlockSpec((None, tm, n), lambda i, j, kk: (i, kk, 0)),
                    o_shape=SDS((nb, k, n), F32), o_spec=pl.BlockSpec((None, k, n), lambda i, j, kk: (i, 0, 0)),
                    dims=TN, acc_shape=(k, n))


def wgrad_bi(name, zb, g):
    nb, m, k = zb.shape
    n = g.shape[1]
    tm, tn = _row_tile(m), _col_tile(n)
    return _mm_call(name, zb, g, grid=(nb, n // tn, m // tm),
                    a_spec=pl.BlockSpec((None, tm, k), lambda i, j, kk: (i, kk, 0)),
                    b_spec=pl.BlockSpec((tm, tn), lambda i, j, kk: (kk, j)),
                    o_shape=SDS((nb, k, n), F32), o_spec=pl.BlockSpec((None, k, tn), lambda i, j, kk: (i, 0, j)),
                    dims=TN, acc_shape=(k, tn))


class Arg:
    def __init__(self, arr, block, imap, shared=False, acc=False):
        self.arr, self.block, self.imap = arr, block, imap
        self.shared = shared
        self.acc = acc

    @property
    def spec(self):
        return pl.BlockSpec(self.block, self.imap)

    def vshape(self):
        return tuple(b for b in self.block if b is not None)


def _rev(arg, nt):
    return pl.BlockSpec(arg.block, lambda o, t, _f=arg.imap: _f(o, nt - 1 - t))


def seq_fwd(name, fn, grid, params, consts, xs, outs, carries=(), save_dtype=F32):
    no, nt = grid
    n_p, n_c, n_x, n_y, n_k = len(params), len(consts), len(xs), len(outs), len(carries)

    def body(*refs):
        p_refs = refs[:n_p]
        c_refs = refs[n_p:n_p + n_c]
        x_refs = refs[n_p + n_c:n_p + n_c + n_x]
        r = n_p + n_c + n_x
        y_refs = refs[r:r + n_y]
        s_refs = refs[r + n_y:r + n_y + n_k]
        k_refs = refs[r + n_y + n_k:]
        t = pl.program_id(1)

        if n_k:
            @pl.when(t == 0)
            def _():
                for k in k_refs:
                    k[...] = jnp.zeros_like(k)

        carry = tuple(k[...] for k in k_refs)
        for s, c in zip(s_refs, carry):
            s[...] = c.astype(s.dtype)
        new_carry, ys = fn(tuple(p[...] for p in p_refs), tuple(c[...] for c in c_refs), carry,
                           tuple(x[...] for x in x_refs))
        for k, c in zip(k_refs, new_carry):
            k[...] = c
        for y_ref, y in zip(y_refs, ys):
            y_ref[...] = y.astype(y_ref.dtype)

    out_shape = [SDS(s, d) for (s, d, _, _) in outs]
    out_specs = [pl.BlockSpec(b, im) for (_, _, b, im) in outs]
    for cs in carries:
        out_shape.append(SDS((no, nt) + cs, save_dtype))
        out_specs.append(pl.BlockSpec((None, None) + cs, lambda o, t, _n=len(cs): (o, t) + (0,) * _n))
    res = pl.pallas_call(
        body, name=name, grid=grid, in_specs=[a.spec for a in list(params) + list(consts) + list(xs)],
        out_specs=out_specs, out_shape=out_shape, scratch_shapes=[pltpu.VMEM(cs, F32) for cs in carries],
        compiler_params=_cp(2))(*[a.arr for a in list(params) + list(consts) + list(xs)])
    return res[:n_y], res[n_y:]


def seq_bwd(name, fn, grid, params, consts, xs, dys, saved=(), carries=()):
    no, nt = grid
    n_p, n_c, n_x, n_y, n_k = len(params), len(consts), len(xs), len(dys), len(carries)

    def body(*refs):
        p_refs = refs[:n_p]
        c_refs = refs[n_p:n_p + n_c]
        x_refs = refs[n_p + n_c:n_p + n_c + n_x]
        r = n_p + n_c + n_x
        g_refs = refs[r:r + n_y]
        s_refs = refs[r + n_y:r + n_y + n_k]
        r = r + n_y + n_k
        dx_refs = refs[r:r + n_x]
        dp_refs = refs[r + n_x:r + n_x + n_p]
        k_refs = refs[r + n_x + n_p:]
        o = pl.program_id(0)
        t = pl.program_id(1)

        if n_k:
            @pl.when(t == 0)
            def _():
                for k in k_refs:
                    k[...] = jnp.zeros_like(k)

        for a, dp in zip(params, dp_refs):
            @pl.when((t == 0) & (o == 0) if a.shared else (t == 0))
            def _(dp=dp):
                dp[...] = jnp.zeros_like(dp)

        for a, dx in zip(xs, dx_refs):
            if a.acc:
                @pl.when(t == 0)
                def _(dx=dx):
                    dx[...] = jnp.zeros_like(dx)

        consts_v = tuple(c[...] for c in c_refs)

        def f(pv, cv, xv):
            return fn(pv, consts_v, cv, xv)

        pv = tuple(p[...] for p in p_refs)
        cv = tuple(s[...].astype(F32) for s in s_refs)
        xv = tuple(x[...] for x in x_refs)
        (new_carry, ys), vjp = jax.vjp(f, pv, cv, xv)
        cot = (tuple(k[...] for k in k_refs), tuple(g[...].astype(y.dtype) for g, y in zip(g_refs, ys)))
        dpv, dcv, dxv = vjp(cot)
        for k, c in zip(k_refs, dcv):
            k[...] = c
        for dp, v in zip(dp_refs, dpv):
            dp[...] += v
        for a, dx, v in zip(xs, dx_refs, dxv):
            if a.acc:
                dx[...] += v
            else:
                dx[...] = v.astype(dx.dtype)

    in_specs = ([_rev(a, nt) for a in list(params) + list(consts) + list(xs) + list(dys)]
                + [pl.BlockSpec((None, None) + cs, lambda o, t, _n=len(cs): (o, nt - 1 - t) + (0,) * _n) for cs in carries])
    out_shape = [SDS(a.arr.shape, F32) for a in xs] + [SDS(a.arr.shape, F32) for a in params]
    out_specs = [_rev(a, nt) for a in list(xs) + list(params)]
    res = pl.pallas_call(
        body, name=name, grid=grid, in_specs=in_specs, out_specs=out_specs, out_shape=out_shape,
        scratch_shapes=[pltpu.VMEM(cs, F32) for cs in carries], compiler_params=_cp(2))(
            *[a.arr for a in list(params) + list(consts) + list(xs) + list(dys)], *saved)
    return res[:n_x], res[n_x:]


def _rowmask(lp):
    return (jnp.arange(lp) >= PAD).astype(F32)[:, None]


def _norm_fn(p, c, k, x):
    return (), (_rms(x[0] * c[0], p[0]),)


def _norm_b_fn(p, c, k, x):
    h = x[0] * c[0]
    return (), (_rms(h, p[0]), h)


def norm_fwd(name, h, g, mask, out_dtype=BF):
    lp, d = h.shape
    tr = _row_tile(lp)
    row = lambda o, t: (t, 0)
    (a,), _ = seq_fwd(name, _norm_fn, (1, lp // tr), [Arg(g, (1, d), lambda o, t: (0, 0), shared=True)],
                      [Arg(mask, (tr, 1), row)], [Arg(h, (tr, d), row)], [((lp, d), out_dtype, (tr, d), row)])
    return a


def norm_bwd(name, h, g, mask, da, dskip):
    lp, d = h.shape
    tr = _row_tile(lp)
    row = lambda o, t: (t, 0)
    (dh,), (dg,) = seq_bwd(name, _norm_b_fn, (1, lp // tr), [Arg(g, (1, d), lambda o, t: (0, 0), shared=True)],
                           [Arg(mask, (tr, 1), row)], [Arg(h, (tr, d), row)],
                           [Arg(da, (tr, d), row), Arg(dskip, (tr, d), row)])
    return dh, dg


def _ffn_tile(lp):
    return 320 if (lp % 320 == 0 and lp > 320) else 64


def _conv_rows(ext, w, b, n):
    u2 = ext[8:8 + n]
    u1 = pltpu.roll(ext, 1, 0)[8:8 + n]
    u0 = pltpu.roll(ext, 2, 0)[8:8 + n]
    return w[2] * u2 + w[1] * u1 + w[0] * u0 + b, (u0, u1, u2)


def ffn_core_fwd(name, u, cw, cb):
    _, nj, lp, fb = u.shape
    tr = _ffn_tile(lp)
    nt = lp // tr

    def body(u_ref, up_ref, w_ref, b_ref, z_ref):
        i = pl.program_id(1)
        prev = jnp.where(i > 0, up_ref[...], 0.0)
        cs = []
        for s in range(2):
            ext = jnp.concatenate([prev[s], u_ref[s]], axis=0)
            c, _ = _conv_rows(ext, w_ref[s], b_ref[s], tr)
            cs.append(c)
        z_ref[...] = (_silu(cs[0]) * cs[1]).astype(z_ref.dtype)

    return pl.pallas_call(
        body, name=name, grid=(nj, nt),
        in_specs=[pl.BlockSpec((2, None, tr, fb), lambda j, i: (0, j, i, 0)),
                  pl.BlockSpec((2, None, 8, fb), lambda j, i: (0, j, jnp.maximum(i * (tr // 8) - 1, 0), 0)),
                  pl.BlockSpec((2, None, 3, 1, fb), lambda j, i: (0, j, 0, 0, 0)),
                  pl.BlockSpec((2, None, 1, fb), lambda j, i: (0, j, 0, 0))],
        out_specs=pl.BlockSpec((None, tr, fb), lambda j, i: (j, i, 0)),
        out_shape=SDS((nj, lp, fb), BF), compiler_params=_cp(2))(u, u, cw, cb)


def ffn_core_bwd(name, u, dz, cw, cb):
    _, nj, lp, fb = u.shape
    tr = _ffn_tile(lp)
    nt = lp // tr
    nb8 = lp // 8

    def body(u_ref, up_ref, un_ref, dz_ref, dzn_ref, w_ref, b_ref, du_ref, dw_ref, db_ref):
        i = pl.program_id(1)

        @pl.when(i == 0)
        def _():
            dw_ref[...] = jnp.zeros_like(dw_ref)
            db_ref[...] = jnp.zeros_like(db_ref)

        prev = jnp.where(i > 0, up_ref[...], 0.0)
        nxt = jnp.where(i < nt - 1, un_ref[...], 0.0)
        dz_e = jnp.concatenate([dz_ref[...], jnp.where(i < nt - 1, dzn_ref[...], 0.0)], axis=0)
        n = tr + 8
        cs, taps = [], []
        for s in range(2):
            ext = jnp.concatenate([prev[s], u_ref[s], nxt[s]], axis=0)
            c, tp = _conv_rows(ext, w_ref[s], b_ref[s], n)
            cs.append(c)
            taps.append(tp)
        sg = jax.nn.sigmoid(cs[0])
        dcs = [dz_e * cs[1] * sg * (1.0 + cs[0] * (1.0 - sg)), dz_e * cs[0] * sg]
        for s in range(2):
            dc = dcs[s]
            w = w_ref[s]
            d1 = pltpu.roll(dc, n - 1, 0)[:tr]
            d2 = pltpu.roll(dc, n - 2, 0)[:tr]
            dcm = dc[:tr]
            du_ref[s] = w[2] * dcm + w[1] * d1 + w[0] * d2
            for k in range(3):
                dw_ref[s, k] += jnp.sum(dcm * taps[s][k][:tr], axis=0, keepdims=True)
            db_ref[s] += jnp.sum(dcm, axis=0, keepdims=True)

    return pl.pallas_call(
        body, name=name, grid=(nj, nt),
        in_specs=[pl.BlockSpec((2, None, tr, fb), lambda j, i: (0, j, i, 0)),
                  pl.BlockSpec((2, None, 8, fb), lambda j, i: (0, j, jnp.maximum(i * (tr // 8) - 1, 0), 0)),
                  pl.BlockSpec((2, None, 8, fb), lambda j, i: (0, j, jnp.minimum((i + 1) * (tr // 8), nb8 - 1), 0)),
                  pl.BlockSpec((None, tr, fb), lambda j, i: (j, i, 0)),
                  pl.BlockSpec((None, 8, fb), lambda j, i: (j, jnp.minimum((i + 1) * (tr // 8), nb8 - 1), 0)),
                  pl.BlockSpec((2, None, 3, 1, fb), lambda j, i: (0, j, 0, 0, 0)),
                  pl.BlockSpec((2, None, 1, fb), lambda j, i: (0, j, 0, 0))],
        out_specs=[pl.BlockSpec((2, None, tr, fb), lambda j, i: (0, j, i, 0)),
                   pl.BlockSpec((2, None, 3, 1, fb), lambda j, i: (0, j, 0, 0, 0)),
                   pl.BlockSpec((2, None, 1, fb), lambda j, i: (0, j, 0, 0))],
        out_shape=[SDS(u.shape, F32), SDS(cw.shape, F32), SDS(cb.shape, F32)],
        compiler_params=_cp(2))(u, u, u, dz, dz, cw, cb)


def ffn_fwd(i, h, mask, w):
    a = norm_fwd(f"ffn{i}_norm", h, w["ng"], mask)
    u = lin_bo(f"ffn{i}_up", a, w["up"])
    lp = h.shape[0]
    u = u.reshape(2, 4, lp, FFN_B)
    z = ffn_core_fwd(f"ffn{i}_core", u, w["cw"], w["cb"])
    h2 = lin_bi(f"ffn{i}_down", z, w["down"], res=h)
    return h2, (h, a, u, z)


def ffn_bwd(i, dh2, mask, w, saved):
    h, a, u, z = saved
    lp = h.shape[0]
    g = {}
    g["down"] = wgrad_bi(f"ffn{i}_dwdown", z, dh2)
    dz = lin_t_bo(f"ffn{i}_dz", dh2, w["down"])
    du, g["cw"], g["cb"] = ffn_core_bwd(f"ffn{i}_core_b", u, dz, w["cw"], w["cb"])
    du = du.reshape(8, lp, FFN_B)
    g["up"] = wgrad_bo(f"ffn{i}_dwup", a, du)
    da = lin_t_bi(f"ffn{i}_da", du, w["up"])
    dh, g["ng"] = norm_bwd(f"ffn{i}_norm_b", h, w["ng"], mask, da, dh2)
    return dh, g


HG_HB = 4


def _hgrn_fn(p, c, k, x):
    lb, go = p
    q, f, iv, g = x[0][0], x[0][1], x[0][2], x[0][3]
    (st_all,) = k
    qs = _silu(q)
    forget = lb + (1.0 - lb) * jax.nn.sigmoid(f)
    logf = jnp.log(forget)
    kk = 1.0 - forget
    gc_all = _seg_cumsum(logf, HG_C)
    r = lax.broadcasted_iota(jnp.int32, (HG_C, HG_C), 0)
    cc = lax.broadcasted_iota(jnp.int32, (HG_C, HG_C), 1)
    ns = CH // HG_C
    cells = [(j, s) for j in range(HG_HB) for s in range(ns)]

    def blk(t, j, s):
        return t[HG_C * s:HG_C * (s + 1), HG_D * j:HG_D * (j + 1)]

    gl = {c: jnp.sum(blk(logf, *c), axis=0, keepdims=True) for c in cells}
    qd = {c: blk(qs, *c) * jnp.exp(blk(gc_all, *c)) for c in cells}
    ki = {c: blk(kk, *c) * jnp.exp(-blk(gc_all, *c)) for c in cells}
    up = {c: mm_tn(blk(iv, *c), blk(kk, *c) * jnp.exp(gl[c] - blk(gc_all, *c))) for c in cells}
    st, sts = {}, []
    for j in range(HG_HB):
        cur = st_all[j]
        for s in range(ns):
            st[(j, s)] = cur
            cur = cur * jnp.exp(gl[(j, s)]) + up[(j, s)]
        sts.append(cur)
    both = {c: mm_nt(qd[c], jnp.concatenate([st[c], ki[c]], axis=0)) for c in cells}
    oc = {c: mm(jnp.where(r >= cc, both[c][:, HG_D:], 0.0), blk(iv, *c)) + both[c][:, :HG_D] for c in cells}
    zs = []
    for j in range(HG_HB):
        o = jnp.concatenate([oc[(j, s)] for s in range(ns)], axis=0)
        zs.append(_rms(o, go) * _silu(g[:, HG_D * j:HG_D * (j + 1)]))
    return (jnp.stack(sts, axis=0),), (jnp.concatenate(zs, axis=1),)


def _hgrn_args(u4, lb, go):
    lp = u4.shape[2]
    wb = HG_HB * HG_D
    xs = [Arg(u4, (4, None, CH, wb), lambda o, t: (0, o, t, 0))]
    ps = [Arg(lb, (1, wb), lambda o, t: (0, o)), Arg(go, (1, HG_D), lambda o, t: (0, 0), shared=True)]
    return (HG_H // HG_HB, lp // CH), ps, xs


def hgrn_fwd(h, mask, w):
    lp = h.shape[0]
    a = norm_fwd("hgrn_norm", h, w["ng"], mask)
    u4 = lin_bo("hgrn_in", a, w["win"]).reshape(4, HG_H // HG_HB, lp, HG_HB * HG_D)
    grid, ps, xs = _hgrn_args(u4, w["lb"], w["go"])
    (z,), (st,) = seq_fwd("hgrn_core", _hgrn_fn, grid, ps, [], xs,
                          [((lp, D), BF, (CH, HG_HB * HG_D), lambda o, t: (t, o))], carries=[(HG_HB, HG_D, HG_D)])
    h2 = lin("hgrn_out", z, w["wo"], res=h)
    return h2, (h, a, u4, z, st)


def hgrn_bwd(dh2, mask, w, saved):
    h, a, u4, z, st = saved
    lp = h.shape[0]
    g = {}
    g["wo"] = wgrad("hgrn_dwo", z, dh2)
    dz = lin_t("hgrn_dz", dh2, w["wo"])
    grid, ps, xs = _hgrn_args(u4, w["lb"], w["go"])
    (du4,), (g["lb"], g["go"]) = seq_bwd("hgrn_core_b", _hgrn_fn, grid, ps, [], xs,
                                         [Arg(dz, (CH, HG_HB * HG_D), lambda o, t: (t, o))], saved=[st],
                                         carries=[(HG_HB, HG_D, HG_D)])
    du = du4.reshape(N_DEV, lp, HG_HB * HG_D)
    g["win"] = wgrad_bo("hgrn_dwin", a, du)
    da = lin_t_bi("hgrn_da", du, w["win"])
    dh, g["ng"] = norm_bwd("hgrn_norm_b", h, w["ng"], mask, da, dh2)
    return dh, g


S5_W = S5_SG * S5_P


def s5_tables(lam_re, lam_im, log_dt, b_re, b_im, c_re, c_im):
    dt = jnp.exp(log_dt)[:, None]
    mag = jnp.exp(lam_re * dt)
    abar_re = mag * jnp.cos(lam_im * dt)
    abar_im = mag * jnp.sin(lam_im * dt)
    den = lam_re * lam_re + lam_im * lam_im
    zoh_re = ((abar_re - 1.0) * lam_re + abar_im * lam_im) / den
    zoh_im = (abar_im * lam_re - (abar_re - 1.0) * lam_im) / den
    bbar_re = zoh_re[..., None] * b_re - zoh_im[..., None] * b_im
    bbar_im = zoh_re[..., None] * b_im + zoh_im[..., None] * b_re
    eye = jnp.eye(S5_SG, dtype=F32)

    def blockdiag_in(b):
        t = b.reshape(N_DEV, S5_SG, S5_P, S5_K).transpose(0, 1, 3, 2)
        return jnp.einsum("jakp,ab->jakbp", t, eye).reshape(N_DEV, S5_SG * S5_K, S5_W)

    def blockdiag_out(c):
        t = c.reshape(N_DEV, S5_SG, S5_K, S5_P).transpose(0, 1, 3, 2)
        return jnp.einsum("japk,ab->japbk", t, eye).reshape(N_DEV, S5_W, S5_SG * S5_K)

    wb = jnp.concatenate([blockdiag_in(bbar_re), blockdiag_in(bbar_im)], axis=2)
    wc = jnp.concatenate([blockdiag_out(c_re), -blockdiag_out(c_im)], axis=1)

    def powers(n):
        steps = n[:, None, None] * dt[None]
        pm = jnp.exp(lam_re[None] * steps)
        pr = (pm * jnp.cos(lam_im[None] * steps)).reshape(-1, N_DEV, S5_W).transpose(1, 0, 2)
        pi = (pm * jnp.sin(lam_im[None] * steps)).reshape(-1, N_DEV, S5_W).transpose(1, 0, 2)
        return jnp.concatenate([pr, pi], axis=2)

    apow = powers(2.0 ** jnp.arange(6, dtype=F32))[:, :, None, :]
    ptab = powers(jnp.arange(CH, dtype=F32) + 1.0)
    return wb, wc, apow, ptab


def _cmul(ar, ai, xr, xi):
    return ar * xr - ai * xi, ar * xi + ai * xr


S5_BB = 4


def _s5_fn(p, c, k, x):
    wb, wc, apow, ptab, dsk = p
    (a,) = x
    (x0,) = k
    blocks = range(S5_BB)
    aj = [a[:, 128 * j:128 * (j + 1)] for j in blocks]
    bu = [mm(aj[j], wb[j]) for j in blocks]
    xxs, x0n = [], []
    for j in blocks:
        xr, xi = bu[j][:, :S5_W], bu[j][:, S5_W:]
        for s in range(6):
            asr, asi = apow[j][s][:, :S5_W], apow[j][s][:, S5_W:]
            dr, di = _cmul(asr, asi, shift_rows(xr, 1 << s), shift_rows(xi, 1 << s))
            xr, xi = xr + dr, xi + di
        dr, di = _cmul(ptab[j][:, :S5_W], ptab[j][:, S5_W:], x0[j][:, :S5_W], x0[j][:, S5_W:])
        xx = jnp.concatenate([xr + dr, xi + di], axis=1)
        last = lax.broadcasted_iota(jnp.int32, xx.shape, 0) == CH - 1
        x0n.append(jnp.sum(jnp.where(last, xx, 0.0), axis=0, keepdims=True))
        xxs.append(xx)
    y = jnp.concatenate([mm(xxs[j], wc[j]) for j in blocks], axis=1)
    return (jnp.stack(x0n, axis=0),), (jax.nn.gelu(y + dsk * a),)


def _s5_args(a, tb, dsk):
    lp = a.shape[0]
    wb, wc, apow, ptab = tb
    ps = [Arg(wb, (S5_BB, 128, 2 * S5_W), lambda o, t: (o, 0, 0)), Arg(wc, (S5_BB, 2 * S5_W, 128), lambda o, t: (o, 0, 0)),
          Arg(apow, (S5_BB, 6, 1, 2 * S5_W), lambda o, t: (o, 0, 0, 0)),
          Arg(ptab, (S5_BB, CH, 2 * S5_W), lambda o, t: (o, 0, 0)), Arg(dsk, (1, 128 * S5_BB), lambda o, t: (0, o))]
    xs = [Arg(a, (CH, 128 * S5_BB), lambda o, t: (t, o))]
    return (N_DEV // S5_BB, lp // CH), ps, xs


def _glu_res_fn(p, c, k, x):
    h, vg = x
    return (), ((h + vg[:, :D] * jax.nn.sigmoid(vg[:, D:])) * c[0],)


def _glu_args(h, vg, mask):
    lp = h.shape[0]
    tr = _row_tile(lp)
    row = lambda o, t: (t, 0)
    return (1, lp // tr), [Arg(mask, (tr, 1), row)], [Arg(h, (tr, D), row), Arg(vg, (tr, 2 * D), row)], tr


def s5_fwd(h, mask, w):
    lp = h.shape[0]
    a = norm_fwd("s5_norm", h, w["ng"], mask, out_dtype=F32)
    grid, ps, xs = _s5_args(a, w["tb"], w["dsk"])
    (z,), (st,) = seq_fwd("s5_core", _s5_fn, grid, ps, [], xs,
                          [((lp, D), BF, (CH, 128 * S5_BB), lambda o, t: (t, o))], carries=[(S5_BB, 1, 2 * S5_W)])
    vg = lin("s5_glu", z, w["wglu"])
    grid2, cs, xs2, tr = _glu_args(h, vg, mask)
    (h2,), _ = seq_fwd("s5_res", _glu_res_fn, grid2, [], cs, xs2, [((lp, D), F32, (tr, D), lambda o, t: (t, 0))])
    return h2, (h, a, z, vg, st)


def s5_bwd(dh2, mask, w, saved):
    h, a, z, vg, st = saved
    g = {}
    grid2, cs, xs2, tr = _glu_args(h, vg, mask)
    (dskip, dvg), _ = seq_bwd("s5_res_b", _glu_res_fn, grid2, [], cs, xs2, [Arg(dh2, (tr, D), lambda o, t: (t, 0))])
    g["wglu"] = wgrad("s5_dwglu", z, dvg)
    dz = lin_t("s5_dz", dvg, w["wglu"])
    grid, ps, xs = _s5_args(a, w["tb"], w["dsk"])
    (da,), dps = seq_bwd("s5_core_b", _s5_fn, grid, ps, [], xs, [Arg(dz, (CH, 128 * S5_BB), lambda o, t: (t, o))],
                         saved=[st], carries=[(S5_BB, 1, 2 * S5_W)])
    g["tb"] = tuple(dps[:4])
    g["dsk"] = dps[4]
    dh, g["ng"] = norm_bwd("s5_norm_b", h, w["ng"], mask, da, dskip)
    return dh, g


def _rope_angles(lp, dim):
    pos = np.maximum(np.arange(lp, dtype=np.float32) - PAD, 0.0).astype(np.float32)
    inv = (1.0 / (ROPE_BASE ** (np.arange(0, dim, 2, dtype=np.float32) / dim))).astype(np.float32)
    return (pos[:, None] * inv[None, :]).astype(np.float32)


def ret_consts(lp):
    f = np.float32
    ang = _rope_angles(lp, RET_DK)
    lg = np.log(1.0 - np.exp2(-5.0 - np.arange(RET_H, dtype=f))).astype(f)
    p = np.arange(CH, dtype=f)
    diff = p[:, None] - p[None, :]
    decay = np.where(diff >= 0, np.exp(diff[None] * lg[:, None, None]), 0.0).astype(f)
    qd = np.exp((p[None, :] + 1.0) * lg[:, None])[..., None].astype(f)
    kd = np.exp((CH - 1.0 - p[None, :]) * lg[:, None])[..., None].astype(f)
    cd = np.exp(CH * lg)[:, None, None].astype(f)
    return np.cos(ang).astype(f), np.sin(ang).astype(f), decay, qd, kd, cd


def _ret_fn(p, c, k, x):
    (gn,) = p
    cos, sin, decay, qd, kd, cd = c
    (st,) = k
    (u,) = x
    hd = RET_DK // 2
    qk_w = RET_H * RET_DK
    heads = range(RET_H)

    def rope(t):
        t1, t2 = t[:, :hd], t[:, hd:]
        return jnp.concatenate([t1 * cos - t2 * sin, t1 * sin + t2 * cos], axis=1)

    qr = [rope(u[:, RET_DK * h:RET_DK * (h + 1)]) for h in heads]
    kr = [rope(u[:, qk_w + RET_DK * h:qk_w + RET_DK * (h + 1)]) * (RET_DK ** -0.5) for h in heads]
    v = [u[:, 2 * qk_w + RET_DV * h:2 * qk_w + RET_DV * (h + 1)] for h in heads]
    scores = [mm_nt(qr[h], kr[h]) for h in heads]
    inter = [mm(qr[h] * qd[h], st[h]) for h in heads]
    st_new = jnp.stack([st[h] * cd[h] + mm_tn(kr[h] * kd[h], v[h]) for h in heads], axis=0)
    o = [mm(scores[h] * decay[h], v[h]) + inter[h] for h in heads]
    zs = []
    for h in heads:
        mu = jnp.mean(o[h], axis=-1, keepdims=True)
        var = jnp.mean(jnp.square(o[h] - mu), axis=-1, keepdims=True)
        gate = u[:, 2 * qk_w + RET_H * RET_DV + RET_DV * h:2 * qk_w + RET_H * RET_DV + RET_DV * (h + 1)]
        zs.append((o[h] - mu) * lax.rsqrt(var + EPS) * gn[:, RET_DV * h:RET_DV * (h + 1)] * _silu(gate))
    return (st_new,), (jnp.concatenate(zs, axis=1),)


def _ret_args(u, gn, rc):
    lp, uw = u.shape
    cos, sin, decay, qd, kd, cd = rc
    full = lambda o, t: (0, 0, 0)
    ps = [Arg(gn, (1, RET_H * RET_DV), lambda o, t: (0, 0))]
    cs = [Arg(cos, (CH, RET_DK // 2), lambda o, t: (t, 0)), Arg(sin, (CH, RET_DK // 2), lambda o, t: (t, 0)),
          Arg(decay, (RET_H, CH, CH), full), Arg(qd, (RET_H, CH, 1), full), Arg(kd, (RET_H, CH, 1), full),
          Arg(cd, (RET_H, 1, 1), full)]
    xs = [Arg(u, (CH, uw), lambda o, t: (t, 0))]
    return (1, lp // CH), ps, cs, xs


def ret_fwd(h, mask, w):
    lp = h.shape[0]
    a = norm_fwd("ret_norm", h, w["ng"], mask)
    u = lin("ret_in", a, w["win"])
    grid, ps, cs, xs = _ret_args(u, w["gn"], w["rc"])
    (z,), (st,) = seq_fwd("ret_core", _ret_fn, grid, ps, cs, xs,
                          [((lp, 2 * D), BF, (CH, RET_H * RET_DV), lambda o, t: (t, 0))],
                          carries=[(RET_H, RET_DK, RET_DV)], save_dtype=BF)
    h2 = lin("ret_out", z, w["wo"], res=h)
    return h2, (h, a, u, z, st)


def ret_bwd(dh2, mask, w, saved):
    h, a, u, z, st = saved
    g = {}
    g["wo"] = wgrad("ret_dwo", z, dh2)
    dz = lin_t("ret_dz", dh2, w["wo"])
    grid, ps, cs, xs = _ret_args(u, w["gn"], w["rc"])
    (du,), (g["gn"],) = seq_bwd("ret_core_b", _ret_fn, grid, ps, cs, xs,
                                [Arg(dz, (CH, RET_H * RET_DV), lambda o, t: (t, 0))], saved=[st],
                                carries=[(RET_H, RET_DK, RET_DV)])
    g["win"] = wgrad("ret_dwin", a, du)
    da = lin_t("ret_da", du, w["win"])
    dh, g["ng"] = norm_bwd("ret_norm_b", h, w["ng"], mask, da, dh2)
    return dh, g


def mla_consts(lp):
    ang = _rope_angles(lp, MLA_ROPE)
    cos = np.concatenate([np.cos(ang), np.cos(ang)], axis=1).astype(np.float32)
    sin = np.concatenate([np.sin(ang), np.sin(ang)], axis=1).astype(np.float32)
    hd = MLA_ROPE // 2
    i = np.arange(hd)
    rot = np.zeros((MLA_ROPE, MLA_ROPE), np.float32)
    rot[hd + i, i] = -1.0
    rot[i, hd + i] = 1.0
    return cos, sin, rot


def _mla_prep1_fn(p, c, k, x):
    gq, gkv = p
    (down,) = x
    return (), (_rms(down[:, :MLA_QL], gq), _rms(down[:, MLA_QL:MLA_QL + MLA_KVL], gkv), down[:, MLA_QL + MLA_KVL:])


def _mla_prep2(p, c, x):
    gq, gk = p
    cos, sin, rot = c
    q, kv, kpe = x
    qn = _rms(q, gq)
    qn_n, qn_r = qn[:, :MLA_NOPE], qn[:, MLA_NOPE:]
    qo = jnp.concatenate([qn_n, qn_r * cos + cright(qn_r, rot) * sin], axis=1)
    kn = kv[:, :MLA_NOPE]
    ms = (jnp.sum(kn * kn, axis=-1, keepdims=True) + jnp.sum(kpe * kpe, axis=-1, keepdims=True)) / MLA_QK
    r = lax.rsqrt(ms + EPS)
    kr = kpe * r * gk[:, MLA_NOPE:]
    ko = jnp.concatenate([kn * r * gk[:, :MLA_NOPE], kr * cos + cright(kr, rot) * sin], axis=1)
    return qo, ko, kv[:, MLA_NOPE:]


def _mla_prep2_fn(p, c, k, x):
    return (), _mla_prep2(p, c, x)[:2]


def _mla_prep2_b_fn(p, c, k, x):
    return (), _mla_prep2(p, c, x)


def _prep1_args(down, gq, gkv):
    lp = down.shape[0]
    tr = _row_tile(lp)
    ps = [Arg(gq, (1, MLA_QL), lambda o, t: (0, 0), shared=True), Arg(gkv, (1, MLA_KVL), lambda o, t: (0, 0), shared=True)]
    return (1, lp // tr), ps, [Arg(down, (tr, down.shape[1]), lambda o, t: (t, 0))], tr


def _prep2_args(qraw, kvraw, kpe, gq, gk, mc):
    lp = kpe.shape[0]
    tr = _row_tile(lp)
    cos, sin, rot = mc
    ps = [Arg(gq, (1, MLA_QK), lambda o, t: (0, 0), shared=True), Arg(gk, (1, MLA_QK), lambda o, t: (0, 0), shared=True)]
    cs = [Arg(cos, (tr, MLA_ROPE), lambda o, t: (o, 0)), Arg(sin, (tr, MLA_ROPE), lambda o, t: (o, 0)),
          Arg(rot, (MLA_ROPE, MLA_ROPE), lambda o, t: (0, 0))]
    xs = [Arg(qraw, (None, tr, MLA_QK), lambda o, t: (t, o, 0)), Arg(kvraw, (None, tr, MLA_NOPE + MLA_V), lambda o, t: (t, o, 0)),
          Arg(kpe, (tr, MLA_ROPE), lambda o, t: (o, 0), acc=True)]
    return (lp // tr, MLA_H), ps, cs, xs, tr


def _attn_tile(lp):
    return 832 if (lp % 832 == 0 and lp > 832) else 64


def _attn_mask(qi, ki, ta):
    rows = qi * ta + lax.broadcasted_iota(jnp.int32, (ta, ta), 0)
    cols = ki * ta + lax.broadcasted_iota(jnp.int32, (ta, ta), 1)
    return (cols >= PAD) & ((cols // CH) <= (rows // CH))


def attn_fwd(q, k, kv):
    nh, lp, dq = q.shape
    ta = _attn_tile(lp)
    nb = lp // ta
    scale = MLA_QK ** -0.5

    def body(q_ref, k_ref, v_ref, o_ref, lse_ref, m_s, l_s, acc_s):
        qi, ki = pl.program_id(1), pl.program_id(2)

        @pl.when(ki == 0)
        def _():
            m_s[...] = jnp.full_like(m_s, NEG)
            l_s[...] = jnp.zeros_like(l_s)
            acc_s[...] = jnp.zeros_like(acc_s)

        def step(masked):
            s = _bdot(q_ref[...], k_ref[...], 1, 1) * scale
            if masked:
                s = jnp.where(_attn_mask(qi, ki, ta), s, NEG)
            m_new = jnp.maximum(m_s[...], jnp.max(s, axis=-1, keepdims=True))
            p = jnp.exp(s - m_new)
            alpha = jnp.exp(m_s[...] - m_new)
            l_s[...] = alpha * l_s[...] + jnp.sum(p, axis=-1, keepdims=True)
            acc_s[...] = alpha * acc_s[...] + _bdot(p, v_ref[...], 1, 0)
            m_s[...] = m_new

        pl.when((ki == qi) | (ki == 0))(functools.partial(step, True))
        pl.when((ki < qi) & (ki > 0))(functools.partial(step, False))

        @pl.when(ki == nb - 1)
        def _():
            o_ref[...] = (acc_s[...] / l_s[...]).astype(o_ref.dtype)
            lse_ref[...] = m_s[...] + jnp.log(l_s[...])

    return pl.pallas_call(
        body, name="mla_attn", grid=(nh, nb, nb),
        in_specs=[pl.BlockSpec((None, ta, dq), lambda h, qi, ki: (h, qi, 0)),
                  pl.BlockSpec((None, ta, dq), lambda h, qi, ki: (h, jnp.minimum(ki, qi), 0)),
                  pl.BlockSpec((None, ta, MLA_V), lambda h, qi, ki: (h, jnp.minimum(ki, qi), 1))],
        out_specs=[pl.BlockSpec((ta, MLA_V), lambda h, qi, ki: (qi, h)),
                   pl.BlockSpec((None, ta, 1), lambda h, qi, ki: (h, qi, 0))],
        out_shape=[SDS((lp, nh * MLA_V), BF), SDS((nh, lp, 1), F32)],
        scratch_shapes=[pltpu.VMEM((ta, 1), F32), pltpu.VMEM((ta, 1), F32), pltpu.VMEM((ta, MLA_V), F32)],
        compiler_params=_cp(3))(q, k, kv)


def attn_bwd(q, k, kv, o, do, lse):
    nh, lp, dq = q.shape
    ta = _attn_tile(lp)
    nb = lp // ta
    scale = MLA_QK ** -0.5

    def body(q_ref, k_ref, v_ref, o_ref, do_ref, lse_ref, dq_ref, dk_ref, dv_ref, dk_s, dv_s):
        ki, qi = pl.program_id(1), pl.program_id(2)

        @pl.when((ki == 0) & (qi == 0))
        def _():
            dq_ref[...] = jnp.zeros_like(dq_ref)

        @pl.when(qi == 0)
        def _():
            dk_s[...] = jnp.zeros_like(dk_s)
            dv_s[...] = jnp.zeros_like(dv_s)

        def step(masked):
            dov = do_ref[...]
            s = _bdot(q_ref[...], k_ref[...], 1, 1) * scale
            dp = _bdot(dov, v_ref[...], 1, 1)
            if masked:
                s = jnp.where(_attn_mask(qi, ki, ta), s, NEG)
            p = jnp.exp(s - lse_ref[...])
            delta = jnp.sum(dov * o_ref[...].astype(F32), axis=-1, keepdims=True)
            dv_s[...] += _bdot(p, dov, 0, 0)
            ds = p * (dp - delta) * scale
            rows = pl.ds(pl.multiple_of(qi * ta, ta), ta)
            dq_ref[rows, :] += _bdot(ds, k_ref[...], 1, 0)
            dk_s[...] += _bdot(ds, q_ref[...], 0, 0)

        pl.when((ki == qi) | (ki == 0))(functools.partial(step, True))
        pl.when((ki < qi) & (ki > 0))(functools.partial(step, False))

        @pl.when(qi == nb - 1)
        def _():
            dk_ref[...] = dk_s[...]
            dv_ref[...] = dv_s[...]

    qmap = lambda h, ki, qi: (h, jnp.maximum(qi, ki), 0)
    return pl.pallas_call(
        body, name="mla_attn_b", grid=(nh, nb, nb),
        in_specs=[pl.BlockSpec((None, ta, dq), qmap),
                  pl.BlockSpec((None, ta, dq), lambda h, ki, qi: (h, ki, 0)),
                  pl.BlockSpec((None, ta, MLA_V), lambda h, ki, qi: (h, ki, 1)),
                  pl.BlockSpec((ta, MLA_V), lambda h, ki, qi: (jnp.maximum(qi, ki), h)),
                  pl.BlockSpec((ta, MLA_V), lambda h, ki, qi: (jnp.maximum(qi, ki), h)),
                  pl.BlockSpec((None, ta, 1), qmap)],
        out_specs=[pl.BlockSpec((None, lp, dq), lambda h, ki, qi: (h, 0, 0)),
                   pl.BlockSpec((None, ta, dq), lambda h, ki, qi: (h, ki, 0)),
                   pl.BlockSpec((None, ta, MLA_V), lambda h, ki, qi: (h, ki, 0))],
        out_shape=[SDS((nh, lp, dq), F32), SDS((nh, lp, dq), F32), SDS((nh, lp, MLA_V), F32)],
        scratch_shapes=[pltpu.VMEM((ta, dq), F32), pltpu.VMEM((ta, MLA_V), F32)],
        compiler_params=_cp(3))(q, k, kv, o, do, lse)


def mla_fwd(h, mask, w):
    lp = h.shape[0]
    a = norm_fwd("mla_norm", h, w["ng"], mask)
    down = lin("mla_down", a, w["wdown"])
    grid, ps, xs, tr = _prep1_args(down, w["gcq"], w["gckv"])
    row = lambda o, t: (t, 0)
    (cq, ckv, kpe), _ = seq_fwd("mla_prep1", _mla_prep1_fn, grid, ps, [], xs,
                                [((lp, MLA_QL), BF, (tr, MLA_QL), row), ((lp, MLA_KVL), BF, (tr, MLA_KVL), row),
                                 ((lp, MLA_ROPE), F32, (tr, MLA_ROPE), row)])
    qraw = lin_bo("mla_uq", cq, w["wuq"])
    kvraw = lin_bo("mla_ukv", ckv, w["wukv"])
    grid, ps, cs, xs, tr = _prep2_args(qraw, kvraw, kpe, w["gq"], w["gk"], w["mc"])
    hm = lambda o, t: (t, o, 0)
    (q, k), _ = seq_fwd("mla_prep2", _mla_prep2_fn, grid, ps, cs, xs,
                        [((MLA_H, lp, MLA_QK), BF, (None, tr, MLA_QK), hm), ((MLA_H, lp, MLA_QK), BF, (None, tr, MLA_QK), hm)])
    o, lse = attn_fwd(q, k, kvraw)
    h2 = lin("mla_out", o, w["wo"], res=h)
    return h2, (h, a, down, cq, ckv, kpe, qraw, kvraw, q, k, o, lse)


def mla_bwd(dh2, mask, w, saved, emit):
    h, a, down, cq, ckv, kpe, qraw, kvraw, q, k, o, lse = saved
    lp = h.shape[0]
    g = {}
    g["wo"] = wgrad("mla_dwo", o, dh2)
    do = lin_t("mla_do", dh2, w["wo"])
    do = emit("wo", [g["wo"]], do)
    dq, dk, dv = attn_bwd(q, k, kvraw, o, do, lse)
    grid, ps, cs, xs, tr = _prep2_args(qraw, kvraw, kpe, w["gq"], w["gk"], w["mc"])
    hm = lambda o, t: (t, o, 0)
    (dqraw, dkvraw, dkpe), (g["gq"], g["gk"]) = seq_bwd(
        "mla_prep2_b", _mla_prep2_b_fn, grid, ps, cs, xs,
        [Arg(dq, (None, tr, MLA_QK), hm), Arg(dk, (None, tr, MLA_QK), hm), Arg(dv, (None, tr, MLA_V), hm)])
    g["wuq"] = wgrad_bo("mla_dwuq", cq, dqraw)
    dcq = lin_t_bi("mla_dcq", dqraw, w["wuq"])
    g["wukv"] = wgrad_bo("mla_dwukv", ckv, dkvraw)
    dckv = lin_t_bi("mla_dckv", dkvraw, w["wukv"])
    dckv = emit("wu", [g["wuq"], g["wukv"]], dckv)
    grid, ps, xs, tr = _prep1_args(down, w["gcq"], w["gckv"])
    row = lambda o, t: (t, 0)
    (ddown,), (g["gcq"], g["gckv"]) = seq_bwd(
        "mla_prep1_b", _mla_prep1_fn, grid, ps, [], xs,
        [Arg(dcq, (tr, MLA_QL), row), Arg(dckv, (tr, MLA_KVL), row), Arg(dkpe, (tr, MLA_ROPE), row)])
    g["wdown"] = wgrad("mla_dwdown", a, ddown)
    da = lin_t("mla_da", ddown, w["wdown"])
    dh, g["ng"] = norm_bwd("mla_norm_b", h, w["ng"], mask, da, dh2)
    return dh, g


def loss_head(h, target):
    lp, d = h.shape
    assert OFF == CH

    def body(h_ref, t_ref, loss_ref, dh_ref):
        i = pl.program_id(0)

        @pl.when(i == 0)
        def _():
            loss_ref[...] = jnp.zeros_like(loss_ref)

        e = jnp.where(i > 0, h_ref[...] - t_ref[...], 0.0)
        loss_ref[...] += jnp.sum(e * e) * (0.5 / d)
        dh_ref[...] = e * (1.0 / d)

    return pl.pallas_call(
        body, name="loss_head", grid=(lp // CH,),
        in_specs=[pl.BlockSpec((CH, d), lambda i: (i, 0)), pl.BlockSpec((CH, d), lambda i: (jnp.maximum(i - 1, 0), 0))],
        out_specs=[pl.BlockSpec((8, 128), lambda i: (0, 0)), pl.BlockSpec((CH, d), lambda i: (i, 0))],
        out_shape=[SDS((8, 128), F32), SDS((lp, d), F32)], compiler_params=_cp(1))(h, target)


ADAM_LAND_BYTES = 20 * 1024 * 1024


def _adam_tile(r, c, nl):
    if r % 8:
        return r
    best = 8
    for t in range(8, r + 1, 8):
        if r % t == 0 and N_DEV * t * c * 4 * 2 * nl <= ADAM_LAND_BYTES:
            best = t
    return best


def adamw(name, lands, w, m, v):
    nl, r, c = w.shape
    tr = _adam_tile(r, c, nl)
    c1 = 1.0 / (1.0 - ADAM_B1 ** ADAM_STEP)
    c2 = 1.0 / (1.0 - ADAM_B2 ** ADAM_STEP)

    def body(*refs):
        l_refs = refs[:nl]
        w_ref, m_ref, v_ref, g_ref, d_ref, nm_ref, nv_ref = refs[nl:]
        layer = pl.program_id(0)
        for j in range(nl):
            @pl.when(layer == j)
            def _(j=j):
                g = l_refs[j][0]
                for i in range(1, N_DEV):
                    g = g + l_refs[j][i]
                g_ref[...] = g

        g = g_ref[...]
        nm = ADAM_B1 * m_ref[...] + (1.0 - ADAM_B1) * g
        nv = ADAM_B2 * v_ref[...] + (1.0 - ADAM_B2) * (g * g)
        nm_ref[...] = nm
        nv_ref[...] = nv
        d_ref[...] = -ADAM_LR * ((nm * c1) / (jnp.sqrt(nv * c2) + ADAM_EPS) + ADAM_WD * w_ref[...])

    blk = pl.BlockSpec((None, tr, c), lambda l, i: (l, i, 0))
    land_specs = [pl.BlockSpec((N_DEV, tr, c), lambda l, i, j=j: (0, jnp.where(l == j, i, 0), 0)) for j in range(nl)]
    return pl.pallas_call(
        body, name=name, grid=(nl, r // tr), in_specs=land_specs + [blk, blk, blk],
        out_specs=[blk, blk, blk, blk], out_shape=[SDS((nl, r, c), F32)] * 4, compiler_params=_cp(2))(*lands, w, m, v)


ANY = pl.BlockSpec(memory_space=pl.ANY)
MESH = pl.DeviceIdType.MESH


def _me():
    return lax.axis_index("x"), lax.axis_index("y"), lax.axis_index("c")


def _peers():
    x, y, c = _me()
    out = []
    for k in range(1, N_DEV):
        px = 1 - x if k & 4 else x
        py = 1 - y if k & 2 else y
        pc = 1 - c if k & 1 else c
        out.append(((px, py, pc), 4 * px + 2 * py + pc))
    return out


HBM_SPEC = pl.BlockSpec(memory_space=pltpu.HBM)
SEM_SPEC = pl.BlockSpec(memory_space=pltpu.SEMAPHORE)
DATAFLOW = pltpu.SideEffectType.DATAFLOW_SIDE_EFFECTING


def _my_index():
    return 4 * lax.axis_index("x") + 2 * lax.axis_index("y") + lax.axis_index("c")


def _hbm(a):
    return pltpu.with_memory_space_constraint(a, pltpu.HBM)


NP = N_DEV - 1


def _push_copy(x_ref, land_ref, send, recv, pid, src_idx, dst_idx, scatter):
    src = x_ref.at[src_idx] if scatter else x_ref
    return pltpu.make_async_remote_copy(src_ref=src, dst_ref=land_ref.at[dst_idx], send_sem=send, recv_sem=recv,
                                        device_id=pid, device_id_type=MESH)


def push_start(name, xs, me, scatter, carry=None):
    n = len(xs)
    lands = []
    for a in xs:
        own = lax.dynamic_index_in_dim(a, me, 0, keepdims=True) if scatter else a[None]
        z = lax.empty((N_DEV,) + own.shape[1:], a.dtype)
        lands.append(lax.dynamic_update_slice(z, own, (me,) + (0,) * (own.ndim - 1)))
    ns = 2 * NP * n
    ops = xs + lands + ([carry] if carry is not None else [])
    na = len(ops)

    def body(*refs):
        x_refs, land_refs = refs[:n], refs[n:2 * n]
        sems = refs[na:na + ns]
        token = refs[-1]
        x, y, c = _me()
        mine = 4 * x + 2 * y + c
        for i in range(n):
            for k, (pid, pidx) in enumerate(_peers()):
                s = 2 * (NP * i + k)
                _push_copy(x_refs[i], land_refs[i], sems[s], sems[s + 1], pid, pidx, mine, scatter).start()
        token[...] = jnp.zeros_like(token)

    out_shape = ([pltpu.SemaphoreType.DMA(())] * ns + [pltpu.HBM(a.shape, a.dtype) for a in ops]
                 + [SDS((8, 128), F32)])
    res = pl.pallas_call(
        body, name=name, out_shape=out_shape, in_specs=[HBM_SPEC] * na,
        out_specs=[SEM_SPEC] * ns + [HBM_SPEC] * na + [pl.BlockSpec(memory_space=pltpu.VMEM)],
        input_output_aliases={i: ns + i for i in range(na)},
        compiler_params=pltpu.CompilerParams(has_side_effects=DATAFLOW))(*[_hbm(a) for a in ops])
    sems, thru, token = res[:ns], res[ns:-1], res[-1]
    handles = [dict(x=thru[i], land=thru[n + i], sems=list(sems[2 * NP * i:2 * NP * (i + 1)]), token=token)
               for i in range(n)]
    return (handles, thru[2 * n]) if carry is not None else handles


def push_wait(name, hds, after, scatter):
    n = len(hds)
    ns = 2 * NP

    def body(*refs):
        x_refs, land_refs = refs[:n], refs[n:2 * n]
        sems = refs[2 * n:2 * n + ns * n]
        for i in range(n):
            for k, (pid, pidx) in enumerate(_peers()):
                cp = _push_copy(x_refs[i], land_refs[i], sems[ns * i + 2 * k], sems[ns * i + 2 * k + 1], pid, pidx, pidx,
                                scatter)
                cp.wait_send()
                cp.wait_recv()

    arrs = [hd["x"] for hd in hds] + [hd["land"] for hd in hds]
    sems = [s for hd in hds for s in hd["sems"]]
    res = pl.pallas_call(
        body, name=name, out_shape=[pltpu.HBM(a.shape, a.dtype) for a in arrs],
        in_specs=[HBM_SPEC] * (2 * n) + [SEM_SPEC] * (ns * n) + [ANY], out_specs=[HBM_SPEC] * (2 * n),
        input_output_aliases={i: i for i in range(2 * n)},
        compiler_params=pltpu.CompilerParams(has_side_effects=DATAFLOW))(*arrs, *sems, after)
    return list(res[n:])


WEIGHTS = ['meta_tokens', 'norm_mix_g', 'norm_ffn_g', 'mla_w_down', 'mla_cq_norm_g', 'mla_ckv_norm_g', 'mla_w_uq',
           'mla_w_ukv', 'mla_q_head_g', 'mla_k_head_g', 'mla_w_o', 'hgrn_w_in', 'hgrn_lb_logits', 'hgrn_o_norm_g',
           'hgrn_w_o', 's5_lam_re', 's5_lam_im', 's5_log_dt', 's5_b_re', 's5_b_im', 's5_c_re', 's5_c_im', 's5_d',
           's5_w_glu', 'ret_w_in', 'ret_gn_g', 'ret_w_o', 'ffn_w_up', 'ffn_conv_w', 'ffn_conv_b', 'ffn_w_down']
BIG = ['mla_w_down', 'mla_w_uq', 'mla_w_ukv', 'mla_w_o', 'hgrn_w_in', 'hgrn_w_o', 's5_w_glu', 'ret_w_in', 'ret_w_o',
       'ffn_w_up', 'ffn_w_down']
SMALL_SH = ['meta_tokens', 's5_d', 'ret_gn_g', 'ffn_conv_w']
SMALL_REP = ['norm_mix_g', 'norm_ffn_g', 'mla_cq_norm_g', 'mla_ckv_norm_g', 'mla_q_head_g', 'mla_k_head_g',
             'hgrn_lb_logits', 'hgrn_o_norm_g', 's5_lam_re', 's5_lam_im', 's5_log_dt', 's5_b_re', 's5_b_im',
             's5_c_re', 's5_c_im', 'ffn_conv_b']
LANE = 128


def _flat(arrs, mult):
    v = jnp.concatenate([a.reshape(-1) for a in arrs])
    pad = (-v.shape[0]) % mult
    return jnp.pad(v, (0, pad)).reshape(-1, LANE)


def _unflat(flat2d, like):
    v = flat2d.reshape(-1)
    out, o = [], 0
    for a in like:
        out.append(v[o:o + a.size].reshape(a.shape))
        o += a.size
    return out


def _lb_of(logits):
    cum = jnp.cumsum(jax.nn.softmax(logits, axis=0), axis=0)
    return (cum - cum[0:1])[1:2]


def _cols_to_blocks(g):
    k, n = g.shape
    return g.reshape(k, N_DEV, n // N_DEV).transpose(1, 0, 2)


def _blocks_to_cols(wb):
    nb, k, n = wb.shape
    return wb.transpose(1, 0, 2).reshape(k, nb * n)


SUBS = ['mla', 'ffn0', 'hgrn', 'ffn1', 's5', 'ffn2', 'ret', 'ffn3']
GROUPS = [[('mla_w_down', 0), ('mla_w_uq', 0), ('mla_w_ukv', 0), ('mla_w_o', 0)],
          [('ffn_w_up', 0), ('ffn_w_down', 0)],
          [('hgrn_w_in', 0), ('hgrn_w_o', 0)],
          [('ffn_w_up', 1), ('ffn_w_down', 1)],
          [('s5_w_glu', 0)],
          [('ffn_w_up', 2), ('ffn_w_down', 2)],
          [('ret_w_in', 0), ('ret_w_o', 0)],
          [('ffn_w_up', 3), ('ffn_w_down', 3)]]


def _pack8(parts, mult):
    v = jnp.concatenate(parts, axis=1)
    return jnp.pad(v, ((0, 0), (0, (-v.shape[1]) % mult))).reshape(N_DEV, -1, LANE)


def _sub_weights(k, got, rep, tabs, lp):
    ngm, ngf = rep['norm_mix_g'], rep['norm_ffn_g']
    if k == 0:
        return dict(ng=ngm[0:1], wdown=got[0].reshape(D, -1), gcq=rep['mla_cq_norm_g'], gckv=rep['mla_ckv_norm_g'],
                    wuq=got[1], wukv=got[2], gq=rep['mla_q_head_g'], gk=rep['mla_k_head_g'], wo=got[3].reshape(D, D),
                    mc=mla_consts(lp))
    if k == 2:
        return dict(ng=ngm[1:2], win=got[0], lb=tabs['lb'], go=rep['hgrn_o_norm_g'], wo=got[1].reshape(D, D))
    if k == 4:
        return dict(ng=ngm[2:3], tb=tabs['tb'], dsk=tabs['s5_d'], wglu=_blocks_to_cols(got[0]))
    if k == 6:
        return dict(ng=ngm[3:4], win=_blocks_to_cols(got[0]), gn=tabs['ret_gn_g'], wo=got[1].reshape(2 * D, D),
                    rc=ret_consts(lp))
    i = k // 2
    return dict(ng=ngf[i:i + 1], up=got[0], cw=tabs['conv_w'][:, i].reshape(2, 4, 3, 1, FFN_B),
                cb=rep['ffn_conv_b'][i].reshape(2, 4, 1, FFN_B), down=got[1].reshape(4, FFN_B, D))


def _sub_grad_blocks(k, g):
    if k == 0:
        parts = [g['wdown'], g['wuq'], g['wukv'], g['wo']]
    elif k == 2:
        parts = [g['win'], g['wo']]
    elif k == 4:
        parts = [_cols_to_blocks(g['wglu'])]
    elif k == 6:
        parts = [_cols_to_blocks(g['win']), g['wo']]
    else:
        parts = [g['up'], g['down']]
    return parts


_FWD = [mla_fwd, None, hgrn_fwd, None, s5_fwd, None, ret_fwd, None]
_BWD = [mla_bwd, None, hgrn_bwd, None, s5_bwd, None, ret_bwd, None]


def _step(args):
    w = {n: args[n] for n in WEIGHTS}
    x2, tgt = args['x'][0], args['loss_target'][0]

    lp = x2.shape[0] + OFF
    me = _my_index()
    mask = _rowmask(lp)
    rep = {n: w[n] for n in SMALL_REP}

    xs, slots = [], []
    for gi, grp in enumerate(GROUPS):
        items = [w[n][l].astype(BF) for n, l in grp] + ([_flat([w[n] for n in SMALL_SH], LANE)] if gi == 0 else [])
        slots.append((len(xs), len(items)))
        xs += items
    gh = push_start("gather_start", xs, me, scatter=False)

    def fetch(gi, after):
        s, cnt = slots[gi]
        return push_wait("gather_wait_" + SUBS[gi], gh[s:s + cnt], after, scatter=False)

    got = fetch(0, x2)
    sm, o, smp = got[-1].reshape(N_DEV, -1), 0, {}
    for n in SMALL_SH:
        smp[n] = sm[:, o:o + w[n].size].reshape((N_DEV,) + w[n].shape)
        o += w[n].size
    meta = smp['meta_tokens'].transpose(1, 0, 2).reshape(N_META, D)
    lb, lb_vjp = jax.vjp(_lb_of, rep['hgrn_lb_logits'])
    s5p = [rep[n][0] for n in ('s5_lam_re', 's5_lam_im', 's5_log_dt', 's5_b_re', 's5_b_im', 's5_c_re', 's5_c_im')]
    tb, tb_vjp = jax.vjp(s5_tables, *s5p)
    tabs = dict(lb=lb, tb=tb, s5_d=smp['s5_d'].reshape(1, D), ret_gn_g=smp['ret_gn_g'].reshape(1, 2 * D),
                conv_w=smp['ffn_conv_w'])
    h = jnp.concatenate([jnp.zeros((PAD, D), F32), meta, x2], axis=0)
    ws, saved = [], []
    for k in range(8):
        if k > 0:
            got = fetch(k, h)
        ws.append(_sub_weights(k, got, rep, tabs, lp))
        if k % 2:
            h, sv = ffn_fwd(k // 2, h, mask, ws[k])
        else:
            h, sv = _FWD[k](h, mask, ws[k])
        saved.append(sv)
    loss, dh = loss_head(h, tgt)

    gs, sh = [None] * 8, [None] * 8
    early = {}

    def emit(tag, grads, carry):
        blocks = [t.reshape((N_DEV, -1) + t.shape[-1:]) if t.ndim == 2 else t for t in grads]
        early[tag], carry = push_start("scatter_start_mla_" + tag, blocks, me, scatter=True, carry=carry)
        return carry

    for k in reversed(range(1, 8)):
        if k % 2:
            dh, gs[k] = ffn_bwd(k // 2, dh, mask, ws[k], saved[k])
        else:
            dh, gs[k] = _BWD[k](dh, mask, ws[k], saved[k])
        blocks = [b.reshape((N_DEV,) + w[n].shape[1:]) for b, (n, _) in zip(_sub_grad_blocks(k, gs[k]), GROUPS[k])]
        sh[k], dh = push_start("scatter_start_" + SUBS[k], blocks, me, scatter=True, carry=dh)
    dh, gs[0] = mla_bwd(dh, mask, ws[0], saved[0], emit)
    dmeta = dh[PAD:OFF].reshape(N_META, N_DEV, D // N_DEV).transpose(1, 0, 2)
    dcw = jnp.stack([gs[2 * i + 1]['cw'].reshape(N_DEV, 3, FFN_B) for i in range(4)], axis=1)
    last = push_start("scatter_start_mla", [gs[0]['wdown'].reshape(N_DEV, D // N_DEV, -1),
                                            _pack8([t.reshape(N_DEV, -1) for t in (dmeta, gs[4]['dsk'], gs[6]['gn'], dcw)], LANE)],
                      me, scatter=True)
    sh[0] = [last[0], early['wu'][0], early['wu'][1], early['wo'][0], last[1]]
    grad_x = dh[OFF:]

    ds5 = tb_vjp(gs[4]['tb'])
    g_rep = {
        'norm_mix_g': jnp.concatenate([gs[k]['ng'] for k in (0, 2, 4, 6)], axis=0),
        'norm_ffn_g': jnp.concatenate([gs[k]['ng'] for k in (1, 3, 5, 7)], axis=0),
        'mla_cq_norm_g': gs[0]['gcq'], 'mla_ckv_norm_g': gs[0]['gckv'], 'mla_q_head_g': gs[0]['gq'],
        'mla_k_head_g': gs[0]['gk'], 'hgrn_lb_logits': lb_vjp(gs[2]['lb'])[0], 'hgrn_o_norm_g': gs[2]['go'],
        's5_lam_re': ds5[0], 's5_lam_im': ds5[1], 's5_log_dt': ds5[2], 's5_b_re': ds5[3], 's5_b_im': ds5[4],
        's5_c_re': ds5[5], 's5_c_im': ds5[6],
        'ffn_conv_b': jnp.stack([gs[k]['cb'].reshape(-1) for k in (1, 3, 5, 7)], axis=0),
    }
    loss_part = loss[0, 0:1]
    grep = _flat([g_rep[n] for n in SMALL_REP] + [loss_part], 8 * LANE)
    rh = push_start("small_grads_start", [grep], me, scatter=False)

    lands = {n: [None] * w[n].shape[0] for n in BIG}
    res = {}
    late = [n for n, _ in GROUPS[0]]
    for k in reversed(range(1, 8)):
        got = push_wait("scatter_wait_" + SUBS[k], sh[k], grep, scatter=True)
        for (n, l), t in zip(GROUPS[k], got):
            lands[n][l] = t
    for n in BIG:
        if n not in late:
            res[n] = adamw("adam_" + n, lands[n], w[n], args['m_' + n], args['v_' + n])
    after = res['ffn_w_up'][1]
    got = push_wait("scatter_wait_" + SUBS[0], sh[0], after, scatter=True)
    small_land = got[-1]
    (rep_land,) = push_wait("small_grads_wait", rh, after, scatter=False)
    for (n, _), t in zip(GROUPS[0], got):
        res[n] = adamw("adam_" + n, [t], w[n], args['m_' + n], args['v_' + n])

    def flat_adam(name, land, names, mult, extra=()):
        like = [w[n] for n in names]
        pad = [jnp.zeros_like(e) for e in extra]
        out = adamw(name, [land], _flat(like + pad, mult)[None], _flat([args['m_' + n] for n in names] + pad, mult)[None],
                    _flat([args['v_' + n] for n in names] + pad, mult)[None])
        for n, parts in zip(names, zip(*[_unflat(t, like) for t in out])):
            res[n] = list(parts)
        return out[0]

    flat_adam("adam_small_sharded", small_land, SMALL_SH, LANE)
    gsum = flat_adam("adam_small_replicated", rep_land, SMALL_REP, 8 * LANE, extra=[loss_part])
    total = gsum.reshape(-1)[sum(w[n].size for n in SMALL_REP)]
    outs = [total, grad_x[None]]
    for k in range(4):
        outs += [res[n][k] for n in WEIGHTS]
    return tuple(outs)


def kernel(x, meta_tokens, norm_mix_g, norm_ffn_g, mla_w_down, mla_cq_norm_g, mla_ckv_norm_g, mla_w_uq, mla_w_ukv, mla_q_head_g, mla_k_head_g, mla_w_o, hgrn_w_in, hgrn_lb_logits, hgrn_o_norm_g, hgrn_w_o, s5_lam_re, s5_lam_im, s5_log_dt, s5_b_re, s5_b_im, s5_c_re, s5_c_im, s5_d, s5_w_glu, ret_w_in, ret_gn_g, ret_w_o, ffn_w_up, ffn_conv_w, ffn_conv_b, ffn_w_down, loss_target, m_meta_tokens, m_norm_mix_g, m_norm_ffn_g, m_mla_w_down, m_mla_cq_norm_g, m_mla_ckv_norm_g, m_mla_w_uq, m_mla_w_ukv, m_mla_q_head_g, m_mla_k_head_g, m_mla_w_o, m_hgrn_w_in, m_hgrn_lb_logits, m_hgrn_o_norm_g, m_hgrn_w_o, m_s5_lam_re, m_s5_lam_im, m_s5_log_dt, m_s5_b_re, m_s5_b_im, m_s5_c_re, m_s5_c_im, m_s5_d, m_s5_w_glu, m_ret_w_in, m_ret_gn_g, m_ret_w_o, m_ffn_w_up, m_ffn_conv_w, m_ffn_conv_b, m_ffn_w_down, v_meta_tokens, v_norm_mix_g, v_norm_ffn_g, v_mla_w_down, v_mla_cq_norm_g, v_mla_ckv_norm_g, v_mla_w_uq, v_mla_w_ukv, v_mla_q_head_g, v_mla_k_head_g, v_mla_w_o, v_hgrn_w_in, v_hgrn_lb_logits, v_hgrn_o_norm_g, v_hgrn_w_o, v_s5_lam_re, v_s5_lam_im, v_s5_log_dt, v_s5_b_re, v_s5_b_im, v_s5_c_re, v_s5_c_im, v_s5_d, v_s5_w_glu, v_ret_w_in, v_ret_gn_g, v_ret_w_o, v_ffn_w_up, v_ffn_conv_w, v_ffn_conv_b, v_ffn_w_down):
    return _step(dict(locals()))
```

```python
import functools
import math

import jax
import jax.numpy as jnp
import numpy as np
from jax import lax
from jax.experimental import pallas as pl
from jax.experimental.pallas import tpu as pltpu

F32 = jnp.float32
BF = jnp.bfloat16
SDS = jax.ShapeDtypeStruct

N_DEV = 8
D = 1024
N_META = 16
PAD = 48
OFF = PAD + N_META
CH = 64
EPS = 1e-6
NEG = -1e30
ROPE_BASE = 10000.0

MLA_H, MLA_NOPE, MLA_ROPE, MLA_V = 8, 128, 64, 128
MLA_QK = MLA_NOPE + MLA_ROPE
MLA_QL, MLA_KVL = 384, 256
HG_H, HG_D, HG_C = 8, 128, 16
S5_G, S5_P, S5_K = 64, 64, 16
S5_SG = 8
RET_H, RET_DK, RET_DV = 4, 256, 512
FFN_F = 2816
FFN_B = 704

ADAM_LR, ADAM_B1, ADAM_B2, ADAM_EPS, ADAM_WD, ADAM_STEP = 0.001, 0.9, 0.999, 1e-08, 0.01, 10

VMEM_LIMIT = 56 * 1024 * 1024
ARB = "arbitrary"


def _cp(n):
    return pltpu.CompilerParams(dimension_semantics=(ARB,) * n, vmem_limit_bytes=VMEM_LIMIT)


def _bdot(a, b, ca, cb):
    return lax.dot_general(a.astype(BF), b.astype(BF), (((ca,), (cb,)), ((), ())), preferred_element_type=F32)


@jax.custom_vjp
def mm(a, b):
    return _bdot(a, b, 1, 0)


@jax.custom_vjp
def mm_nt(a, b):
    return _bdot(a, b, 1, 1)


@jax.custom_vjp
def mm_tn(a, b):
    return _bdot(a, b, 0, 0)


mm.defvjp(lambda a, b: (mm(a, b), (a, b)),
          lambda r, g: (mm_nt(g, r[1]).astype(r[0].dtype), mm_tn(r[0], g).astype(r[1].dtype)))
mm_nt.defvjp(lambda a, b: (mm_nt(a, b), (a, b)),
             lambda r, g: (mm(g, r[1]).astype(r[0].dtype), mm_tn(g, r[0]).astype(r[1].dtype)))
mm_tn.defvjp(lambda a, b: (mm_tn(a, b), (a, b)),
             lambda r, g: (mm_nt(r[1], g).astype(r[0].dtype), mm(r[0], g).astype(r[1].dtype)))


def _xdot(a, b, ca, cb):
    return lax.dot_general(a, b, (((ca,), (cb,)), ((), ())), preferred_element_type=F32,
                           precision=lax.Precision.HIGHEST)


@jax.custom_vjp
def cright(x, r):
    return _xdot(x, r, 1, 0)


cright.defvjp(lambda x, r: (cright(x, r), r), lambda r, g: (_xdot(g, r, 1, 1), jnp.zeros_like(r)))


def _shift_raw(x, s):
    n = x.shape[0]
    r = lax.broadcasted_iota(jnp.int32, x.shape, 0)
    y = pltpu.roll(x, s % n, 0)
    return jnp.where((r >= s) & (r < n + s), y, 0.0)


@functools.partial(jax.custom_vjp, nondiff_argnums=(1,))
def shift_rows(x, s):
    return _shift_raw(x, s)


shift_rows.defvjp(lambda x, s: (_shift_raw(x, s), None), lambda s, _, g: (_shift_raw(g, -s),))


def _seg_shift_raw(x, s, seg, up):
    n = x.shape[0]
    r = lax.broadcasted_iota(jnp.int32, x.shape, 0) % seg
    if up:
        return jnp.where(r < seg - s, pltpu.roll(x, n - s, 0), 0.0)
    return jnp.where(r >= s, pltpu.roll(x, s, 0), 0.0)


@functools.partial(jax.custom_vjp, nondiff_argnums=(1, 2))
def seg_shift(x, s, seg):
    return _seg_shift_raw(x, s, seg, False)


seg_shift.defvjp(lambda x, s, seg: (_seg_shift_raw(x, s, seg, False), None),
                 lambda s, seg, _, g: (_seg_shift_raw(g, s, seg, True),))


def _seg_cumsum(x, seg):
    s = 1
    while s < seg:
        x = x + seg_shift(x, s, seg)
        s *= 2
    return x


def _rms(x, g):
    return x * lax.rsqrt(jnp.mean(x * x, axis=-1, keepdims=True) + EPS) * g


def _silu(x):
    return x * jax.nn.sigmoid(x)


def _mm_call(name, a, b, *, grid, a_spec, b_spec, o_shape, o_spec, dims, acc_shape, res=None, res_spec=None,
             mask_tm=None):
    nk = grid[2]

    def body(*refs):
        if res is None:
            a_ref, b_ref, o_ref = refs[:3]
        else:
            a_ref, b_ref, r_ref, o_ref = refs[:4]
        k = pl.program_id(2)

        def dot():
            return lax.dot_general(a_ref[...].astype(BF), b_ref[...].astype(BF), dims, preferred_element_type=F32)

        def finish(v):
            if res is not None:
                v = v + r_ref[...].astype(F32)
                rows = pl.program_id(0) * mask_tm + lax.broadcasted_iota(jnp.int32, v.shape, 0)
                v = jnp.where(rows >= PAD, v, 0.0)
            o_ref[...] = v.astype(o_ref.dtype)

        if nk == 1:
            finish(dot())
            return
        acc = refs[-1]

        @pl.when(k == 0)
        def _():
            acc[...] = dot()

        @pl.when((k > 0) & (k < nk - 1))
        def _():
            acc[...] += dot()

        @pl.when(k == nk - 1)
        def _():
            finish(acc[...] + dot())

    ins = [a, b] + ([res] if res is not None else [])
    specs = [a_spec, b_spec] + ([res_spec] if res is not None else [])
    scratch = [pltpu.VMEM(acc_shape, F32)] if nk > 1 else []
    return pl.pallas_call(body, name=name, grid=grid, in_specs=specs, out_specs=o_spec, out_shape=o_shape,
                          scratch_shapes=scratch, compiler_params=_cp(3))(*ins)


NN = (((1,), (0,)), ((), ()))
NT = (((1,), (1,)), ((), ()))
TN = (((0,), (0,)), ((), ()))


def _row_tile(lp):
    for t in (832, 640, 320, 64):
        if lp % t == 0:
            return t
    raise ValueError(lp)


def _col_tile(n):
    for t in (1024, 768, 512, 384, 256, 128):
        if n % t == 0:
            return t
    return n


def lin(name, a, w, out_dtype=F32, res=None):
    m, k = a.shape
    n = w.shape[1]
    tm, tn, tc = _row_tile(m), _col_tile(n), _col_tile(k)
    return _mm_call(name, a, w, grid=(m // tm, n // tn, k // tc),
                    a_spec=pl.BlockSpec((tm, tc), lambda i, j, kk: (i, kk)),
                    b_spec=pl.BlockSpec((tc, tn), lambda i, j, kk: (kk, j)),
                    o_shape=SDS((m, n), out_dtype), o_spec=pl.BlockSpec((tm, tn), lambda i, j, kk: (i, j)),
                    dims=NN, acc_shape=(tm, tn), res=res,
                    res_spec=pl.BlockSpec((tm, tn), lambda i, j, kk: (i, j)), mask_tm=tm)


def lin_bo(name, a, wb, out_dtype=F32):
    m, k = a.shape
    nb, _, n = wb.shape
    tm = _row_tile(m)
    return _mm_call(name, a, wb, grid=(m // tm, nb, 1),
                    a_spec=pl.BlockSpec((tm, k), lambda i, j, kk: (i, 0)),
                    b_spec=pl.BlockSpec((None, k, n), lambda i, j, kk: (j, 0, 0)),
                    o_shape=SDS((nb, m, n), out_dtype), o_spec=pl.BlockSpec((None, tm, n), lambda i, j, kk: (j, i, 0)),
                    dims=NN, acc_shape=(tm, n))


def lin_bi(name, ab, wb, out_dtype=F32, res=None):
    nb, m, k = ab.shape
    n = wb.shape[2]
    tm, tn = _row_tile(m), _col_tile(n)
    return _mm_call(name, ab, wb, grid=(m // tm, n // tn, nb),
                    a_spec=pl.BlockSpec((None, tm, k), lambda i, j, kk: (kk, i, 0)),
                    b_spec=pl.BlockSpec((None, k, tn), lambda i, j, kk: (kk, 0, j)),
                    o_shape=SDS((m, n), out_dtype), o_spec=pl.BlockSpec((tm, tn), lambda i, j, kk: (i, j)),
                    dims=NN, acc_shape=(tm, tn), res=res,
                    res_spec=pl.BlockSpec((tm, tn), lambda i, j, kk: (i, j)), mask_tm=tm)


def lin_t(name, g, w, out_dtype=F32):
    m, n = g.shape
    k = w.shape[0]
    tm, tk, tc = _row_tile(m), _col_tile(k), _col_tile(n)
    return _mm_call(name, g, w, grid=(m // tm, k // tk, n // tc),
                    a_spec=pl.BlockSpec((tm, tc), lambda i, j, kk: (i, kk)),
                    b_spec=pl.BlockSpec((tk, tc), lambda i, j, kk: (j, kk)),
                    o_shape=SDS((m, k), out_dtype), o_spec=pl.BlockSpec((tm, tk), lambda i, j, kk: (i, j)),
                    dims=NT, acc_shape=(tm, tk))


def lin_t_bi(name, gb, wb, out_dtype=F32):
    nb, m, n = gb.shape
    k = wb.shape[1]
    tm, tk = _row_tile(m), _col_tile(k)
    return _mm_call(name, gb, wb, grid=(m // tm, k // tk, nb),
                    a_spec=pl.BlockSpec((None, tm, n), lambda i, j, kk: (kk, i, 0)),
                    b_spec=pl.BlockSpec((None, tk, n), lambda i, j, kk: (kk, j, 0)),
                    o_shape=SDS((m, k), out_dtype), o_spec=pl.BlockSpec((tm, tk), lambda i, j, kk: (i, j)),
                    dims=NT, acc_shape=(tm, tk))


def lin_t_bo(name, g, wb, out_dtype=F32):
    m, n = g.shape
    nb, k, _ = wb.shape
    tm = _row_tile(m)
    return _mm_call(name, g, wb, grid=(m // tm, nb, 1),
                    a_spec=pl.BlockSpec((tm, n), lambda i, j, kk: (i, 0)),
                    b_spec=pl.BlockSpec((None, k, n), lambda i, j, kk: (j, 0, 0)),
                    o_shape=SDS((nb, m, k), out_dtype), o_spec=pl.BlockSpec((None, tm, k), lambda i, j, kk: (j, i, 0)),
                    dims=NT, acc_shape=(tm, k))


def wgrad(name, a, g):
    m, k = a.shape
    n = g.shape[1]
    tm, tn = _row_tile(m), _col_tile(n)
    return _mm_call(name, a, g, grid=(1, n // tn, m // tm),
                    a_spec=pl.BlockSpec((tm, k), lambda i, j, kk: (kk, 0)),
                    b_spec=pl.BlockSpec((tm, tn), lambda i, j, kk: (kk, j)),
                    o_shape=SDS((k, n), F32), o_spec=pl.BlockSpec((k, tn), lambda i, j, kk: (0, j)),
                    dims=TN, acc_shape=(k, tn))


def wgrad_bo(name, a, gb):
    m, k = a.shape
    nb, _, n = gb.shape
    tm = _row_tile(m)
    return _mm_call(name, a, gb, grid=(nb, 1, m // tm),
                    a_spec=pl.BlockSpec((tm, k), lambda i, j, kk: (kk, 0)),
                    b_spec=pl.BlockSpec((None, tm, n), lambda i, j, kk: (i, kk, 0)),
                    o_shape=SDS((nb, k, n), F32), o_spec=pl.BlockSpec((None, k, n), lambda i, j, kk: (i, 0, 0)),
                    dims=TN, acc_shape=(k, n))


def wgrad_bi(name, zb, g):
    nb, m, k = zb.shape
    n = g.shape[1]
    tm, tn = _row_tile(m), _col_tile(n)
    return _mm_call(name, zb, g, grid=(nb, n // tn, m // tm),
                    a_spec=pl.BlockSpec((None, tm, k), lambda i, j, kk: (i, kk, 0)),
                    b_spec=pl.BlockSpec((tm, tn), lambda i, j, kk: (kk, j)),
                    o_shape=SDS((nb, k, n), F32), o_spec=pl.BlockSpec((None, k, tn), lambda i, j, kk: (i, 0, j)),
                    dims=TN, acc_shape=(k, tn))


class Arg:
    def __init__(self, arr, block, imap, shared=False, acc=False):
        self.arr, self.block, self.imap = arr, block, imap
        self.shared = shared
        self.acc = acc

    @property
    def spec(self):
        return pl.BlockSpec(self.block, self.imap)

    def vshape(self):
        return tuple(b for b in self.block if b is not None)


def _rev(arg, nt):
    return pl.BlockSpec(arg.block, lambda o, t, _f=arg.imap: _f(o, nt - 1 - t))


def seq_fwd(name, fn, grid, params, consts, xs, outs, carries=(), save_dtype=F32):
    no, nt = grid
    n_p, n_c, n_x, n_y, n_k = len(params), len(consts), len(xs), len(outs), len(carries)

    def body(*refs):
        p_refs = refs[:n_p]
        c_refs = refs[n_p:n_p + n_c]
        x_refs = refs[n_p + n_c:n_p + n_c + n_x]
        r = n_p + n_c + n_x
        y_refs = refs[r:r + n_y]
        s_refs = refs[r + n_y:r + n_y + n_k]
        k_refs = refs[r + n_y + n_k:]
        t = pl.program_id(1)

        if n_k:
            @pl.when(t == 0)
            def _():
                for k in k_refs:
                    k[...] = jnp.zeros_like(k)

        carry = tuple(k[...] for k in k_refs)
        for s, c in zip(s_refs, carry):
            s[...] = c.astype(s.dtype)
        new_carry, ys = fn(tuple(p[...] for p in p_refs), tuple(c[...] for c in c_refs), carry,
                           tuple(x[...] for x in x_refs))
        for k, c in zip(k_refs, new_carry):
            k[...] = c
        for y_ref, y in zip(y_refs, ys):
            y_ref[...] = y.astype(y_ref.dtype)

    out_shape = [SDS(s, d) for (s, d, _, _) in outs]
    out_specs = [pl.BlockSpec(b, im) for (_, _, b, im) in outs]
    for cs in carries:
        out_shape.append(SDS((no, nt) + cs, save_dtype))
        out_specs.append(pl.BlockSpec((None, None) + cs, lambda o, t, _n=len(cs): (o, t) + (0,) * _n))
    res = pl.pallas_call(
        body, name=name, grid=grid, in_specs=[a.spec for a in list(params) + list(consts) + list(xs)],
        out_specs=out_specs, out_shape=out_shape, scratch_shapes=[pltpu.VMEM(cs, F32) for cs in carries],
        compiler_params=_cp(2))(*[a.arr for a in list(params) + list(consts) + list(xs)])
    return res[:n_y], res[n_y:]


def seq_bwd(name, fn, grid, params, consts, xs, dys, saved=(), carries=()):
    no, nt = grid
    n_p, n_c, n_x, n_y, n_k = len(params), len(consts), len(xs), len(dys), len(carries)

    def body(*refs):
        p_refs = refs[:n_p]
        c_refs = refs[n_p:n_p + n_c]
        x_refs = refs[n_p + n_c:n_p + n_c + n_x]
        r = n_p + n_c + n_x
        g_refs = refs[r:r + n_y]
        s_refs = refs[r + n_y:r + n_y + n_k]
        r = r + n_y + n_k
        dx_refs = refs[r:r + n_x]
        dp_refs = refs[r + n_x:r + n_x + n_p]
        k_refs = refs[r + n_x + n_p:]
        o = pl.program_id(0)
        t = pl.program_id(1)

        if n_k:
            @pl.when(t == 0)
            def _():
                for k in k_refs:
                    k[...] = jnp.zeros_like(k)

        for a, dp in zip(params, dp_refs):
            @pl.when((t == 0) & (o == 0) if a.shared else (t == 0))
            def _(dp=dp):
                dp[...] = jnp.zeros_like(dp)

        for a, dx in zip(xs, dx_refs):
            if a.acc:
                @pl.when(t == 0)
                def _(dx=dx):
                    dx[...] = jnp.zeros_like(dx)

        consts_v = tuple(c[...] for c in c_refs)

        def f(pv, cv, xv):
            return fn(pv, consts_v, cv, xv)

        pv = tuple(p[...] for p in p_refs)
        cv = tuple(s[...].astype(F32) for s in s_refs)
        xv = tuple(x[...] for x in x_refs)
        (new_carry, ys), vjp = jax.vjp(f, pv, cv, xv)
        cot = (tuple(k[...] for k in k_refs), tuple(g[...].astype(y.dtype) for g, y in zip(g_refs, ys)))
        dpv, dcv, dxv = vjp(cot)
        for k, c in zip(k_refs, dcv):
            k[...] = c
        for dp, v in zip(dp_refs, dpv):
            dp[...] += v
        for a, dx, v in zip(xs, dx_refs, dxv):
            if a.acc:
                dx[...] += v
            else:
                dx[...] = v.astype(dx.dtype)

    in_specs = ([_rev(a, nt) for a in list(params) + list(consts) + list(xs) + list(dys)]
                + [pl.BlockSpec((None, None) + cs, lambda o, t, _n=len(cs): (o, nt - 1 - t) + (0,) * _n) for cs in carries])
    out_shape = [SDS(a.arr.shape, F32) for a in xs] + [SDS(a.arr.shape, F32) for a in params]
    out_specs = [_rev(a, nt) for a in list(xs) + list(params)]
    res = pl.pallas_call(
        body, name=name, grid=grid, in_specs=in_specs, out_specs=out_specs, out_shape=out_shape,
        scratch_shapes=[pltpu.VMEM(cs, F32) for cs in carries], compiler_params=_cp(2))(
            *[a.arr for a in list(params) + list(consts) + list(xs) + list(dys)], *saved)
    return res[:n_x], res[n_x:]


def _rowmask(lp):
    return (jnp.arange(lp) >= PAD).astype(F32)[:, None]


def _norm_fn(p, c, k, x):
    return (), (_rms(x[0] * c[0], p[0]),)


def _norm_b_fn(p, c, k, x):
    h = x[0] * c[0]
    return (), (_rms(h, p[0]), h)


def norm_fwd(name, h, g, mask, out_dtype=BF):
    lp, d = h.shape
    tr = _row_tile(lp)
    row = lambda o, t: (t, 0)
    (a,), _ = seq_fwd(name, _norm_fn, (1, lp // tr), [Arg(g, (1, d), lambda o, t: (0, 0), shared=True)],
                      [Arg(mask, (tr, 1), row)], [Arg(h, (tr, d), row)], [((lp, d), out_dtype, (tr, d), row)])
    return a


def norm_bwd(name, h, g, mask, da, dskip):
    lp, d = h.shape
    tr = _row_tile(lp)
    row = lambda o, t: (t, 0)
    (dh,), (dg,) = seq_bwd(name, _norm_b_fn, (1, lp // tr), [Arg(g, (1, d), lambda o, t: (0, 0), shared=True)],
                           [Arg(mask, (tr, 1), row)], [Arg(h, (tr, d), row)],
                           [Arg(da, (tr, d), row), Arg(dskip, (tr, d), row)])
    return dh, dg


def _ffn_tile(lp):
    return 320 if (lp % 320 == 0 and lp > 320) else 64


def _conv_rows(ext, w, b, n):
    u2 = ext[8:8 + n]
    u1 = pltpu.roll(ext, 1, 0)[8:8 + n]
    u0 = pltpu.roll(ext, 2, 0)[8:8 + n]
    return w[2] * u2 + w[1] * u1 + w[0] * u0 + b, (u0, u1, u2)


def ffn_core_fwd(name, u, cw, cb):
    _, nj, lp, fb = u.shape
    tr = _ffn_tile(lp)
    nt = lp // tr

    def body(u_ref, up_ref, w_ref, b_ref, z_ref):
        i = pl.program_id(1)
        prev = jnp.where(i > 0, up_ref[...], 0.0)
        cs = []
        for s in range(2):
            ext = jnp.concatenate([prev[s], u_ref[s]], axis=0)
            c, _ = _conv_rows(ext, w_ref[s], b_ref[s], tr)
            cs.append(c)
        z_ref[...] = (_silu(cs[0]) * cs[1]).astype(z_ref.dtype)

    return pl.pallas_call(
        body, name=name, grid=(nj, nt),
        in_specs=[pl.BlockSpec((2, None, tr, fb), lambda j, i: (0, j, i, 0)),
                  pl.BlockSpec((2, None, 8, fb), lambda j, i: (0, j, jnp.maximum(i * (tr // 8) - 1, 0), 0)),
                  pl.BlockSpec((2, None, 3, 1, fb), lambda j, i: (0, j, 0, 0, 0)),
                  pl.BlockSpec((2, None, 1, fb), lambda j, i: (0, j, 0, 0))],
        out_specs=pl.BlockSpec((None, tr, fb), lambda j, i: (j, i, 0)),
        out_shape=SDS((nj, lp, fb), BF), compiler_params=_cp(2))(u, u, cw, cb)


def ffn_core_bwd(name, u, dz, cw, cb):
    _, nj, lp, fb = u.shape
    tr = _ffn_tile(lp)
    nt = lp // tr
    nb8 = lp // 8

    def body(u_ref, up_ref, un_ref, dz_ref, dzn_ref, w_ref, b_ref, du_ref, dw_ref, db_ref):
        i = pl.program_id(1)

        @pl.when(i == 0)
        def _():
            dw_ref[...] = jnp.zeros_like(dw_ref)
            db_ref[...] = jnp.zeros_like(db_ref)

        prev = jnp.where(i > 0, up_ref[...], 0.0)
        nxt = jnp.where(i < nt - 1, un_ref[...], 0.0)
        dz_e = jnp.concatenate([dz_ref[...], jnp.where(i < nt - 1, dzn_ref[...], 0.0)], axis=0)
        n = tr + 8
        cs, taps = [], []
        for s in range(2):
            ext = jnp.concatenate([prev[s], u_ref[s], nxt[s]], axis=0)
            c, tp = _conv_rows(ext, w_ref[s], b_ref[s], n)
            cs.append(c)
            taps.append(tp)
        sg = jax.nn.sigmoid(cs[0])
        dcs = [dz_e * cs[1] * sg * (1.0 + cs[0] * (1.0 - sg)), dz_e * cs[0] * sg]
        for s in range(2):
            dc = dcs[s]
            w = w_ref[s]
            d1 = pltpu.roll(dc, n - 1, 0)[:tr]
            d2 = pltpu.roll(dc, n - 2, 0)[:tr]
            dcm = dc[:tr]
            du_ref[s] = w[2] * dcm + w[1] * d1 + w[0] * d2
            for k in range(3):
                dw_ref[s, k] += jnp.sum(dcm * taps[s][k][:tr], axis=0, keepdims=True)
            db_ref[s] += jnp.sum(dcm, axis=0, keepdims=True)

    return pl.pallas_call(
        body, name=name, grid=(nj, nt),
        in_specs=[pl.BlockSpec((2, None, tr, fb), lambda j, i: (0, j, i, 0)),
                  pl.BlockSpec((2, None, 8, fb), lambda j, i: (0, j, jnp.maximum(i * (tr // 8) - 1, 0), 0)),
                  pl.BlockSpec((2, None, 8, fb), lambda j, i: (0, j, jnp.minimum((i + 1) * (tr // 8), nb8 - 1), 0)),
                  pl.BlockSpec((None, tr, fb), lambda j, i: (j, i, 0)),
                  pl.BlockSpec((None, 8, fb), lambda j, i: (j, jnp.minimum((i + 1) * (tr // 8), nb8 - 1), 0)),
                  pl.BlockSpec((2, None, 3, 1, fb), lambda j, i: (0, j, 0, 0, 0)),
                  pl.BlockSpec((2, None, 1, fb), lambda j, i: (0, j, 0, 0))],
        out_specs=[pl.BlockSpec((2, None, tr, fb), lambda j, i: (0, j, i, 0)),
                   pl.BlockSpec((2, None, 3, 1, fb), lambda j, i: (0, j, 0, 0, 0)),
                   pl.BlockSpec((2, None, 1, fb), lambda j, i: (0, j, 0, 0))],
        out_shape=[SDS(u.shape, F32), SDS(cw.shape, F32), SDS(cb.shape, F32)],
        compiler_params=_cp(2))(u, u, u, dz, dz, cw, cb)


def ffn_fwd(i, h, mask, w):
    a = norm_fwd(f"ffn{i}_norm", h, w["ng"], mask)
    u = lin_bo(f"ffn{i}_up", a, w["up"])
    lp = h.shape[0]
    u = u.reshape(2, 4, lp, FFN_B)
    z = ffn_core_fwd(f"ffn{i}_core", u, w["cw"], w["cb"])
    h2 = lin_bi(f"ffn{i}_down", z, w["down"], res=h)
    return h2, (h, a, u, z)


def ffn_bwd(i, dh2, mask, w, saved):
    h, a, u, z = saved
    lp = h.shape[0]
    g = {}
    g["down"] = wgrad_bi(f"ffn{i}_dwdown", z, dh2)
    dz = lin_t_bo(f"ffn{i}_dz", dh2, w["down"])
    du, g["cw"], g["cb"] = ffn_core_bwd(f"ffn{i}_core_b", u, dz, w["cw"], w["cb"])
    du = du.reshape(8, lp, FFN_B)
    g["up"] = wgrad_bo(f"ffn{i}_dwup", a, du)
    da = lin_t_bi(f"ffn{i}_da", du, w["up"])
    dh, g["ng"] = norm_bwd(f"ffn{i}_norm_b", h, w["ng"], mask, da, dh2)
    return dh, g


HG_HB = 4


def _hgrn_fn(p, c, k, x):
    lb, go = p
    q, f, iv, g = x[0][0], x[0][1], x[0][2], x[0][3]
    (st_all,) = k
    qs = _silu(q)
    forget = lb + (1.0 - lb) * jax.nn.sigmoid(f)
    logf = jnp.log(forget)
    kk = 1.0 - forget
    gc_all = _seg_cumsum(logf, HG_C)
    r = lax.broadcasted_iota(jnp.int32, (HG_C, HG_C), 0)
    cc = lax.broadcasted_iota(jnp.int32, (HG_C, HG_C), 1)
    ns = CH // HG_C
    cells = [(j, s) for j in range(HG_HB) for s in range(ns)]

    def blk(t, j, s):
        return t[HG_C * s:HG_C * (s + 1), HG_D * j:HG_D * (j + 1)]

    gl = {c: jnp.sum(blk(logf, *c), axis=0, keepdims=True) for c in cells}
    qd = {c: blk(qs, *c) * jnp.exp(blk(gc_all, *c)) for c in cells}
    ki = {c: blk(kk, *c) * jnp.exp(-blk(gc_all, *c)) for c in cells}
    up = {c: mm_tn(blk(iv, *c), blk(kk, *c) * jnp.exp(gl[c] - blk(gc_all, *c))) for c in cells}
    st, sts = {}, []
    for j in range(HG_HB):
        cur = st_all[j]
        for s in range(ns):
            st[(j, s)] = cur
            cur = cur * jnp.exp(gl[(j, s)]) + up[(j, s)]
        sts.append(cur)
    both = {c: mm_nt(qd[c], jnp.concatenate([st[c], ki[c]], axis=0)) for c in cells}
    oc = {c: mm(jnp.where(r >= cc, both[c][:, HG_D:], 0.0), blk(iv, *c)) + both[c][:, :HG_D] for c in cells}
    zs = []
    for j in range(HG_HB):
        o = jnp.concatenate([oc[(j, s)] for s in range(ns)], axis=0)
        zs.append(_rms(o, go) * _silu(g[:, HG_D * j:HG_D * (j + 1)]))
    return (jnp.stack(sts, axis=0),), (jnp.concatenate(zs, axis=1),)


def _hgrn_args(u4, lb, go):
    lp = u4.shape[2]
    wb = HG_HB * HG_D
    xs = [Arg(u4, (4, None, CH, wb), lambda o, t: (0, o, t, 0))]
    ps = [Arg(lb, (1, wb), lambda o, t: (0, o)), Arg(go, (1, HG_D), lambda o, t: (0, 0), shared=True)]
    return (HG_H // HG_HB, lp // CH), ps, xs


def hgrn_fwd(h, mask, w):
    lp = h.shape[0]
    a = norm_fwd("hgrn_norm", h, w["ng"], mask)
    u4 = lin_bo("hgrn_in", a, w["win"]).reshape(4, HG_H // HG_HB, lp, HG_HB * HG_D)
    grid, ps, xs = _hgrn_args(u4, w["lb"], w["go"])
    (z,), (st,) = seq_fwd("hgrn_core", _hgrn_fn, grid, ps, [], xs,
                          [((lp, D), BF, (CH, HG_HB * HG_D), lambda o, t: (t, o))], carries=[(HG_HB, HG_D, HG_D)])
    h2 = lin("hgrn_out", z, w["wo"], res=h)
    return h2, (h, a, u4, z, st)


def hgrn_bwd(dh2, mask, w, saved):
    h, a, u4, z, st = saved
    lp = h.shape[0]
    g = {}
    g["wo"] = wgrad("hgrn_dwo", z, dh2)
    dz = lin_t("hgrn_dz", dh2, w["wo"])
    grid, ps, xs = _hgrn_args(u4, w["lb"], w["go"])
    (du4,), (g["lb"], g["go"]) = seq_bwd("hgrn_core_b", _hgrn_fn, grid, ps, [], xs,
                                         [Arg(dz, (CH, HG_HB * HG_D), lambda o, t: (t, o))], saved=[st],
                                         carries=[(HG_HB, HG_D, HG_D)])
    du = du4.reshape(N_DEV, lp, HG_HB * HG_D)
    g["win"] = wgrad_bo("hgrn_dwin", a, du)
    da = lin_t_bi("hgrn_da", du, w["win"])
    dh, g["ng"] = norm_bwd("hgrn_norm_b", h, w["ng"], mask, da, dh2)
    return dh, g


S5_W = S5_SG * S5_P


def s5_tables(lam_re, lam_im, log_dt, b_re, b_im, c_re, c_im):
    dt = jnp.exp(log_dt)[:, None]
    mag = jnp.exp(lam_re * dt)
    abar_re = mag * jnp.cos(lam_im * dt)
    abar_im = mag * jnp.sin(lam_im * dt)
    den = lam_re * lam_re + lam_im * lam_im
    zoh_re = ((abar_re - 1.0) * lam_re + abar_im * lam_im) / den
    zoh_im = (abar_im * lam_re - (abar_re - 1.0) * lam_im) / den
    bbar_re = zoh_re[..., None] * b_re - zoh_im[..., None] * b_im
    bbar_im = zoh_re[..., None] * b_im + zoh_im[..., None] * b_re
    eye = jnp.eye(S5_SG, dtype=F32)

    def blockdiag_in(b):
        t = b.reshape(N_DEV, S5_SG, S5_P, S5_K).transpose(0, 1, 3, 2)
        return jnp.einsum("jakp,ab->jakbp", t, eye).reshape(N_DEV, S5_SG * S5_K, S5_W)

    def blockdiag_out(c):
        t = c.reshape(N_DEV, S5_SG, S5_K, S5_P).transpose(0, 1, 3, 2)
        return jnp.einsum("japk,ab->japbk", t, eye).reshape(N_DEV, S5_W, S5_SG * S5_K)

    wb = jnp.concatenate([blockdiag_in(bbar_re), blockdiag_in(bbar_im)], axis=2)
    wc = jnp.concatenate([blockdiag_out(c_re), -blockdiag_out(c_im)], axis=1)

    def powers(n):
        steps = n[:, None, None] * dt[None]
        pm = jnp.exp(lam_re[None] * steps)
        pr = (pm * jnp.cos(lam_im[None] * steps)).reshape(-1, N_DEV, S5_W).transpose(1, 0, 2)
        pi = (pm * jnp.sin(lam_im[None] * steps)).reshape(-1, N_DEV, S5_W).transpose(1, 0, 2)
        return jnp.concatenate([pr, pi], axis=2)

    apow = powers(2.0 ** jnp.arange(6, dtype=F32))[:, :, None, :]
    ptab = powers(jnp.arange(CH, dtype=F32) + 1.0)
    return wb, wc, apow, ptab


def _cmul(ar, ai, xr, xi):
    return ar * xr - ai * xi, ar * xi + ai * xr


S5_BB = 4


def _s5_fn(p, c, k, x):
    wb, wc, apow, ptab, dsk = p
    (a,) = x
    (x0,) = k
    blocks = range(S5_BB)
    aj = [a[:, 128 * j:128 * (j + 1)] for j in blocks]
    bu = [mm(aj[j], wb[j]) for j in blocks]
    xxs, x0n = [], []
    for j in blocks:
        xr, xi = bu[j][:, :S5_W], bu[j][:, S5_W:]
        for s in range(6):
            asr, asi = apow[j][s][:, :S5_W], apow[j][s][:, S5_W:]
            dr, di = _cmul(asr, asi, shift_rows(xr, 1 << s), shift_rows(xi, 1 << s))
            xr, xi = xr + dr, xi + di
        dr, di = _cmul(ptab[j][:, :S5_W], ptab[j][:, S5_W:], x0[j][:, :S5_W], x0[j][:, S5_W:])
        xx = jnp.concatenate([xr + dr, xi + di], axis=1)
        last = lax.broadcasted_iota(jnp.int32, xx.shape, 0) == CH - 1
        x0n.append(jnp.sum(jnp.where(last, xx, 0.0), axis=0, keepdims=True))
        xxs.append(xx)
    y = jnp.concatenate([mm(xxs[j], wc[j]) for j in blocks], axis=1)
    return (jnp.stack(x0n, axis=0),), (jax.nn.gelu(y + dsk * a),)


def _s5_args(a, tb, dsk):
    lp = a.shape[0]
    wb, wc, apow, ptab = tb
    ps = [Arg(wb, (S5_BB, 128, 2 * S5_W), lambda o, t: (o, 0, 0)), Arg(wc, (S5_BB, 2 * S5_W, 128), lambda o, t: (o, 0, 0)),
          Arg(apow, (S5_BB, 6, 1, 2 * S5_W), lambda o, t: (o, 0, 0, 0)),
          Arg(ptab, (S5_BB, CH, 2 * S5_W), lambda o, t: (o, 0, 0)), Arg(dsk, (1, 128 * S5_BB), lambda o, t: (0, o))]
    xs = [Arg(a, (CH, 128 * S5_BB), lambda o, t: (t, o))]
    return (N_DEV // S5_BB, lp // CH), ps, xs


def _glu_res_fn(p, c, k, x):
    h, vg = x
    return (), ((h + vg[:, :D] * jax.nn.sigmoid(vg[:, D:])) * c[0],)


def _glu_args(h, vg, mask):
    lp = h.shape[0]
    tr = _row_tile(lp)
    row = lambda o, t: (t, 0)
    return (1, lp // tr), [Arg(mask, (tr, 1), row)], [Arg(h, (tr, D), row), Arg(vg, (tr, 2 * D), row)], tr


def s5_fwd(h, mask, w):
    lp = h.shape[0]
    a = norm_fwd("s5_norm", h, w["ng"], mask, out_dtype=F32)
    grid, ps, xs = _s5_args(a, w["tb"], w["dsk"])
    (z,), (st,) = seq_fwd("s5_core", _s5_fn, grid, ps, [], xs,
                          [((lp, D), BF, (CH, 128 * S5_BB), lambda o, t: (t, o))], carries=[(S5_BB, 1, 2 * S5_W)])
    vg = lin("s5_glu", z, w["wglu"])
    grid2, cs, xs2, tr = _glu_args(h, vg, mask)
    (h2,), _ = seq_fwd("s5_res", _glu_res_fn, grid2, [], cs, xs2, [((lp, D), F32, (tr, D), lambda o, t: (t, 0))])
    return h2, (h, a, z, vg, st)


def s5_bwd(dh2, mask, w, saved):
    h, a, z, vg, st = saved
    g = {}
    grid2, cs, xs2, tr = _glu_args(h, vg, mask)
    (dskip, dvg), _ = seq_bwd("s5_res_b", _glu_res_fn, grid2, [], cs, xs2, [Arg(dh2, (tr, D), lambda o, t: (t, 0))])
    g["wglu"] = wgrad("s5_dwglu", z, dvg)
    dz = lin_t("s5_dz", dvg, w["wglu"])
    grid, ps, xs = _s5_args(a, w["tb"], w["dsk"])
    (da,), dps = seq_bwd("s5_core_b", _s5_fn, grid, ps, [], xs, [Arg(dz, (CH, 128 * S5_BB), lambda o, t: (t, o))],
                         saved=[st], carries=[(S5_BB, 1, 2 * S5_W)])
    g["tb"] = tuple(dps[:4])
    g["dsk"] = dps[4]
    dh, g["ng"] = norm_bwd("s5_norm_b", h, w["ng"], mask, da, dskip)
    return dh, g


def _rope_angles(lp, dim):
    pos = np.maximum(np.arange(lp, dtype=np.float32) - PAD, 0.0).astype(np.float32)
    inv = (1.0 / (ROPE_BASE ** (np.arange(0, dim, 2, dtype=np.float32) / dim))).astype(np.float32)
    return (pos[:, None] * inv[None, :]).astype(np.float32)


def ret_consts(lp):
    f = np.float32
    ang = _rope_angles(lp, RET_DK)
    lg = np.log(1.0 - np.exp2(-5.0 - np.arange(RET_H, dtype=f))).astype(f)
    p = np.arange(CH, dtype=f)
    diff = p[:, None] - p[None, :]
    decay = np.where(diff >= 0, np.exp(diff[None] * lg[:, None, None]), 0.0).astype(f)
    qd = np.exp((p[None, :] + 1.0) * lg[:, None])[..., None].astype(f)
    kd = np.exp((CH - 1.0 - p[None, :]) * lg[:, None])[..., None].astype(f)
    cd = np.exp(CH * lg)[:, None, None].astype(f)
    return np.cos(ang).astype(f), np.sin(ang).astype(f), decay, qd, kd, cd


def _ret_fn(p, c, k, x):
    (gn,) = p
    cos, sin, decay, qd, kd, cd = c
    (st,) = k
    (u,) = x
    hd = RET_DK // 2
    qk_w = RET_H * RET_DK
    heads = range(RET_H)

    def rope(t):
        t1, t2 = t[:, :hd], t[:, hd:]
        return jnp.concatenate([t1 * cos - t2 * sin, t1 * sin + t2 * cos], axis=1)

    qr = [rope(u[:, RET_DK * h:RET_DK * (h + 1)]) for h in heads]
    kr = [rope(u[:, qk_w + RET_DK * h:qk_w + RET_DK * (h + 1)]) * (RET_DK ** -0.5) for h in heads]
    v = [u[:, 2 * qk_w + RET_DV * h:2 * qk_w + RET_DV * (h + 1)] for h in heads]
    scores = [mm_nt(qr[h], kr[h]) for h in heads]
    inter = [mm(qr[h] * qd[h], st[h]) for h in heads]
    st_new = jnp.stack([st[h] * cd[h] + mm_tn(kr[h] * kd[h], v[h]) for h in heads], axis=0)
    o = [mm(scores[h] * decay[h], v[h]) + inter[h] for h in heads]
    zs = []
    for h in heads:
        mu = jnp.mean(o[h], axis=-1, keepdims=True)
        var = jnp.mean(jnp.square(o[h] - mu), axis=-1, keepdims=True)
        gate = u[:, 2 * qk_w + RET_H * RET_DV + RET_DV * h:2 * qk_w + RET_H * RET_DV + RET_DV * (h + 1)]
        zs.append((o[h] - mu) * lax.rsqrt(var + EPS) * gn[:, RET_DV * h:RET_DV * (h + 1)] * _silu(gate))
    return (st_new,), (jnp.concatenate(zs, axis=1),)


def _ret_args(u, gn, rc):
    lp, uw = u.shape
    cos, sin, decay, qd, kd, cd = rc
    full = lambda o, t: (0, 0, 0)
    ps = [Arg(gn, (1, RET_H * RET_DV), lambda o, t: (0, 0))]
    cs = [Arg(cos, (CH, RET_DK // 2), lambda o, t: (t, 0)), Arg(sin, (CH, RET_DK // 2), lambda o, t: (t, 0)),
          Arg(decay, (RET_H, CH, CH), full), Arg(qd, (RET_H, CH, 1), full), Arg(kd, (RET_H, CH, 1), full),
          Arg(cd, (RET_H, 1, 1), full)]
    xs = [Arg(u, (CH, uw), lambda o, t: (t, 0))]
    return (1, lp // CH), ps, cs, xs


def ret_fwd(h, mask, w):
    lp = h.shape[0]
    a = norm_fwd("ret_norm", h, w["ng"], mask)
    u = lin("ret_in", a, w["win"])
    grid, ps, cs, xs = _ret_args(u, w["gn"], w["rc"])
    (z,), (st,) = seq_fwd("ret_core", _ret_fn, grid, ps, cs, xs,
                          [((lp, 2 * D), BF, (CH, RET_H * RET_DV), lambda o, t: (t, 0))],
                          carries=[(RET_H, RET_DK, RET_DV)], save_dtype=BF)
    h2 = lin("ret_out", z, w["wo"], res=h)
    return h2, (h, a, u, z, st)


def ret_bwd(dh2, mask, w, saved):
    h, a, u, z, st = saved
    g = {}
    g["wo"] = wgrad("ret_dwo", z, dh2)
    dz = lin_t("ret_dz", dh2, w["wo"])
    grid, ps, cs, xs = _ret_args(u, w["gn"], w["rc"])
    (du,), (g["gn"],) = seq_bwd("ret_core_b", _ret_fn, grid, ps, cs, xs,
                                [Arg(dz, (CH, RET_H * RET_DV), lambda o, t: (t, 0))], saved=[st],
                                carries=[(RET_H, RET_DK, RET_DV)])
    g["win"] = wgrad("ret_dwin", a, du)
    da = lin_t("ret_da", du, w["win"])
    dh, g["ng"] = norm_bwd("ret_norm_b", h, w["ng"], mask, da, dh2)
    return dh, g


def mla_consts(lp):
    ang = _rope_angles(lp, MLA_ROPE)
    cos = np.concatenate([np.cos(ang), np.cos(ang)], axis=1).astype(np.float32)
    sin = np.concatenate([np.sin(ang), np.sin(ang)], axis=1).astype(np.float32)
    hd = MLA_ROPE // 2
    i = np.arange(hd)
    rot = np.zeros((MLA_ROPE, MLA_ROPE), np.float32)
    rot[hd + i, i] = -1.0
    rot[i, hd + i] = 1.0
    return cos, sin, rot


def _mla_prep1_fn(p, c, k, x):
    gq, gkv = p
    (down,) = x
    return (), (_rms(down[:, :MLA_QL], gq), _rms(down[:, MLA_QL:MLA_QL + MLA_KVL], gkv), down[:, MLA_QL + MLA_KVL:])


def _mla_prep2(p, c, x):
    gq, gk = p
    cos, sin, rot = c
    q, kv, kpe = x
    qn = _rms(q, gq)
    qn_n, qn_r = qn[:, :MLA_NOPE], qn[:, MLA_NOPE:]
    qo = jnp.concatenate([qn_n, qn_r * cos + cright(qn_r, rot) * sin], axis=1)
    kn = kv[:, :MLA_NOPE]
    ms = (jnp.sum(kn * kn, axis=-1, keepdims=True) + jnp.sum(kpe * kpe, axis=-1, keepdims=True)) / MLA_QK
    r = lax.rsqrt(ms + EPS)
    kr = kpe * r * gk[:, MLA_NOPE:]
    ko = jnp.concatenate([kn * r * gk[:, :MLA_NOPE], kr * cos + cright(kr, rot) * sin], axis=1)
    return qo, ko, kv[:, MLA_NOPE:]


def _mla_prep2_fn(p, c, k, x):
    return (), _mla_prep2(p, c, x)[:2]


def _mla_prep2_b_fn(p, c, k, x):
    return (), _mla_prep2(p, c, x)


def _prep1_args(down, gq, gkv):
    lp = down.shape[0]
    tr = _row_tile(lp)
    ps = [Arg(gq, (1, MLA_QL), lambda o, t: (0, 0), shared=True), Arg(gkv, (1, MLA_KVL), lambda o, t: (0, 0), shared=True)]
    return (1, lp // tr), ps, [Arg(down, (tr, down.shape[1]), lambda o, t: (t, 0))], tr


def _prep2_args(qraw, kvraw, kpe, gq, gk, mc):
    lp = kpe.shape[0]
    tr = _row_tile(lp)
    cos, sin, rot = mc
    ps = [Arg(gq, (1, MLA_QK), lambda o, t: (0, 0), shared=True), Arg(gk, (1, MLA_QK), lambda o, t: (0, 0), shared=True)]
    cs = [Arg(cos, (tr, MLA_ROPE), lambda o, t: (o, 0)), Arg(sin, (tr, MLA_ROPE), lambda o, t: (o, 0)),
          Arg(rot, (MLA_ROPE, MLA_ROPE), lambda o, t: (0, 0))]
    xs = [Arg(qraw, (None, tr, MLA_QK), lambda o, t: (t, o, 0)), Arg(kvraw, (None, tr, MLA_NOPE + MLA_V), lambda o, t: (t, o, 0)),
          Arg(kpe, (tr, MLA_ROPE), lambda o, t: (o, 0), acc=True)]
    return (lp // tr, MLA_H), ps, cs, xs, tr


ATT_HB = 2


def _attn_tile(lp):
    return 832 if (lp % 832 == 0 and lp > 832) else 64


def _attn_mask(qi, ki, ta):
    rows = qi * ta + lax.broadcasted_iota(jnp.int32, (ta, ta), 0)
    cols = ki * ta + lax.broadcasted_iota(jnp.int32, (ta, ta), 1)
    return (cols >= PAD) & ((cols // CH) <= (rows // CH))


def attn_fwd(q, k, kv):
    nh, lp, dq = q.shape
    ta = _attn_tile(lp)
    nb = lp // ta
    scale = MLA_QK ** -0.5

    hb = ATT_HB

    def body(q_ref, k_ref, v_ref, o_ref, lse_ref, m_s, l_s, acc_s):
        qi, ki = pl.program_id(1), pl.program_id(2)

        @pl.when(ki == 0)
        def _():
            m_s[...] = jnp.full_like(m_s, NEG)
            l_s[...] = jnp.zeros_like(l_s)
            acc_s[...] = jnp.zeros_like(acc_s)

        def step(masked):
            ss = [_bdot(q_ref[j], k_ref[j], 1, 1) * scale for j in range(hb)]
            ps = []
            for j in range(hb):
                s = jnp.where(_attn_mask(qi, ki, ta), ss[j], NEG) if masked else ss[j]
                m_new = jnp.maximum(m_s[j], jnp.max(s, axis=-1, keepdims=True))
                p = jnp.exp(s - m_new)
                alpha = jnp.exp(m_s[j] - m_new)
                l_s[j] = alpha * l_s[j] + jnp.sum(p, axis=-1, keepdims=True)
                m_s[j] = m_new
                ps.append((p, alpha))
            for j in range(hb):
                acc_s[j] = ps[j][1] * acc_s[j] + _bdot(ps[j][0], v_ref[j], 1, 0)

        pl.when((ki == qi) | (ki == 0))(functools.partial(step, True))
        pl.when((ki < qi) & (ki > 0))(functools.partial(step, False))

        @pl.when(ki == nb - 1)
        def _():
            for j in range(hb):
                o_ref[:, MLA_V * j:MLA_V * (j + 1)] = (acc_s[j] / l_s[j]).astype(o_ref.dtype)
                lse_ref[j] = m_s[j] + jnp.log(l_s[j])

    return pl.pallas_call(
        body, name="mla_attn", grid=(nh // hb, nb, nb),
        in_specs=[pl.BlockSpec((hb, ta, dq), lambda h, qi, ki: (h, qi, 0)),
                  pl.BlockSpec((hb, ta, dq), lambda h, qi, ki: (h, jnp.minimum(ki, qi), 0)),
                  pl.BlockSpec((hb, ta, MLA_V), lambda h, qi, ki: (h, jnp.minimum(ki, qi), 1))],
        out_specs=[pl.BlockSpec((ta, hb * MLA_V), lambda h, qi, ki: (qi, h)),
                   pl.BlockSpec((hb, ta, 1), lambda h, qi, ki: (h, qi, 0))],
        out_shape=[SDS((lp, nh * MLA_V), BF), SDS((nh, lp, 1), F32)],
        scratch_shapes=[pltpu.VMEM((hb, ta, 1), F32), pltpu.VMEM((hb, ta, 1), F32), pltpu.VMEM((hb, ta, MLA_V), F32)],
        compiler_params=_cp(3))(q, k, kv)


def attn_bwd(q, k, kv, o, do, lse):
    nh, lp, dq = q.shape
    ta = _attn_tile(lp)
    nb = lp // ta
    scale = MLA_QK ** -0.5

    def body(q_ref, k_ref, v_ref, o_ref, do_ref, lse_ref, dq_ref, dk_ref, dv_ref, dk_s, dv_s):
        ki, qi = pl.program_id(1), pl.program_id(2)

        @pl.when((ki == 0) & (qi == 0))
        def _():
            dq_ref[...] = jnp.zeros_like(dq_ref)

        @pl.when(qi == 0)
        def _():
            dk_s[...] = jnp.zeros_like(dk_s)
            dv_s[...] = jnp.zeros_like(dv_s)

        def step(masked):
            dov = do_ref[...]
            s = _bdot(q_ref[...], k_ref[...], 1, 1) * scale
            dp = _bdot(dov, v_ref[...], 1, 1)
            if masked:
                s = jnp.where(_attn_mask(qi, ki, ta), s, NEG)
            p = jnp.exp(s - lse_ref[...])
            delta = jnp.sum(dov * o_ref[...].astype(F32), axis=-1, keepdims=True)
            dv_s[...] += _bdot(p, dov, 0, 0)
            ds = p * (dp - delta) * scale
            rows = pl.ds(pl.multiple_of(qi * ta, ta), ta)
            dq_ref[rows, :] += _bdot(ds, k_ref[...], 1, 0)
            dk_s[...] += _bdot(ds, q_ref[...], 0, 0)

        pl.when((ki == qi) | (ki == 0))(functools.partial(step, True))
        pl.when((ki < qi) & (ki > 0))(functools.partial(step, False))

        @pl.when(qi == nb - 1)
        def _():
            dk_ref[...] = dk_s[...]
            dv_ref[...] = dv_s[...]

    qmap = lambda h, ki, qi: (h, jnp.maximum(qi, ki), 0)
    return pl.pallas_call(
        body, name="mla_attn_b", grid=(nh, nb, nb),
        in_specs=[pl.BlockSpec((None, ta, dq), qmap),
                  pl.BlockSpec((None, ta, dq), lambda h, ki, qi: (h, ki, 0)),
                  pl.BlockSpec((None, ta, MLA_V), lambda h, ki, qi: (h, ki, 1)),
                  pl.BlockSpec((ta, MLA_V), lambda h, ki, qi: (jnp.maximum(qi, ki), h)),
                  pl.BlockSpec((ta, MLA_V), lambda h, ki, qi: (jnp.maximum(qi, ki), h)),
                  pl.BlockSpec((None, ta, 1), qmap)],
        out_specs=[pl.BlockSpec((None, lp, dq), lambda h, ki, qi: (h, 0, 0)),
                   pl.BlockSpec((None, ta, dq), lambda h, ki, qi: (h, ki, 0)),
                   pl.BlockSpec((None, ta, MLA_V), lambda h, ki, qi: (h, ki, 0))],
        out_shape=[SDS((nh, lp, dq), F32), SDS((nh, lp, dq), F32), SDS((nh, lp, MLA_V), F32)],
        scratch_shapes=[pltpu.VMEM((ta, dq), F32), pltpu.VMEM((ta, MLA_V), F32)],
        compiler_params=_cp(3))(q, k, kv, o, do, lse)


def mla_fwd(h, mask, w):
    lp = h.shape[0]
    a = norm_fwd("mla_norm", h, w["ng"], mask)
    down = lin("mla_down", a, w["wdown"])
    grid, ps, xs, tr = _prep1_args(down, w["gcq"], w["gckv"])
    row = lambda o, t: (t, 0)
    (cq, ckv, kpe), _ = seq_fwd("mla_prep1", _mla_prep1_fn, grid, ps, [], xs,
                                [((lp, MLA_QL), BF, (tr, MLA_QL), row), ((lp, MLA_KVL), BF, (tr, MLA_KVL), row),
                                 ((lp, MLA_ROPE), F32, (tr, MLA_ROPE), row)])
    qraw = lin_bo("mla_uq", cq, w["wuq"])
    kvraw = lin_bo("mla_ukv", ckv, w["wukv"])
    grid, ps, cs, xs, tr = _prep2_args(qraw, kvraw, kpe, w["gq"], w["gk"], w["mc"])
    hm = lambda o, t: (t, o, 0)
    (q, k), _ = seq_fwd("mla_prep2", _mla_prep2_fn, grid, ps, cs, xs,
                        [((MLA_H, lp, MLA_QK), BF, (None, tr, MLA_QK), hm), ((MLA_H, lp, MLA_QK), BF, (None, tr, MLA_QK), hm)])
    o, lse = attn_fwd(q, k, kvraw)
    h2 = lin("mla_out", o, w["wo"], res=h)
    return h2, (h, a, down, cq, ckv, kpe, qraw, kvraw, q, k, o, lse)


def mla_bwd(dh2, mask, w, saved, emit):
    h, a, down, cq, ckv, kpe, qraw, kvraw, q, k, o, lse = saved
    lp = h.shape[0]
    g = {}
    g["wo"] = wgrad("mla_dwo", o, dh2)
    do = lin_t("mla_do", dh2, w["wo"])
    do = emit("wo", [g["wo"]], do)
    dq, dk, dv = attn_bwd(q, k, kvraw, o, do, lse)
    grid, ps, cs, xs, tr = _prep2_args(qraw, kvraw, kpe, w["gq"], w["gk"], w["mc"])
    hm = lambda o, t: (t, o, 0)
    (dqraw, dkvraw, dkpe), (g["gq"], g["gk"]) = seq_bwd(
        "mla_prep2_b", _mla_prep2_b_fn, grid, ps, cs, xs,
        [Arg(dq, (None, tr, MLA_QK), hm), Arg(dk, (None, tr, MLA_QK), hm), Arg(dv, (None, tr, MLA_V), hm)])
    g["wuq"] = wgrad_bo("mla_dwuq", cq, dqraw)
    dcq = lin_t_bi("mla_dcq", dqraw, w["wuq"])
    g["wukv"] = wgrad_bo("mla_dwukv", ckv, dkvraw)
    dckv = lin_t_bi("mla_dckv", dkvraw, w["wukv"])
    dckv = emit("wu", [g["wuq"], g["wukv"]], dckv)
    grid, ps, xs, tr = _prep1_args(down, w["gcq"], w["gckv"])
    row = lambda o, t: (t, 0)
    (ddown,), (g["gcq"], g["gckv"]) = seq_bwd(
        "mla_prep1_b", _mla_prep1_fn, grid, ps, [], xs,
        [Arg(dcq, (tr, MLA_QL), row), Arg(dckv, (tr, MLA_KVL), row), Arg(dkpe, (tr, MLA_ROPE), row)])
    g["wdown"] = wgrad("mla_dwdown", a, ddown)
    da = lin_t("mla_da", ddown, w["wdown"])
    dh, g["ng"] = norm_bwd("mla_norm_b", h, w["ng"], mask, da, dh2)
    return dh, g


def loss_head(h, target):
    lp, d = h.shape
    assert OFF == CH

    def body(h_ref, t_ref, loss_ref, dh_ref):
        i = pl.program_id(0)

        @pl.when(i == 0)
        def _():
            loss_ref[...] = jnp.zeros_like(loss_ref)

        e = jnp.where(i > 0, h_ref[...] - t_ref[...], 0.0)
        loss_ref[...] += jnp.sum(e * e) * (0.5 / d)
        dh_ref[...] = e * (1.0 / d)

    return pl.pallas_call(
        body, name="loss_head", grid=(lp // CH,),
        in_specs=[pl.BlockSpec((CH, d), lambda i: (i, 0)), pl.BlockSpec((CH, d), lambda i: (jnp.maximum(i - 1, 0), 0))],
        out_specs=[pl.BlockSpec((8, 128), lambda i: (0, 0)), pl.BlockSpec((CH, d), lambda i: (i, 0))],
        out_shape=[SDS((8, 128), F32), SDS((lp, d), F32)], compiler_params=_cp(1))(h, target)


ADAM_LAND_BYTES = 20 * 1024 * 1024


def _adam_tile(r, c, nl):
    if r % 8:
        return r
    best = 8
    for t in range(8, r + 1, 8):
        if r % t == 0 and N_DEV * t * c * 4 * 2 * nl <= ADAM_LAND_BYTES:
            best = t
    return best


def adamw(name, lands, w, m, v):
    nl, r, c = w.shape
    tr = _adam_tile(r, c, nl)
    c1 = 1.0 / (1.0 - ADAM_B1 ** ADAM_STEP)
    c2 = 1.0 / (1.0 - ADAM_B2 ** ADAM_STEP)

    def body(*refs):
        l_refs = refs[:nl]
        w_ref, m_ref, v_ref, g_ref, d_ref, nm_ref, nv_ref = refs[nl:]
        layer = pl.program_id(0)
        for j in range(nl):
            @pl.when(layer == j)
            def _(j=j):
                g = l_refs[j][0]
                for i in range(1, N_DEV):
                    g = g + l_refs[j][i]
                g_ref[...] = g

        g = g_ref[...]
        nm = ADAM_B1 * m_ref[...] + (1.0 - ADAM_B1) * g
        nv = ADAM_B2 * v_ref[...] + (1.0 - ADAM_B2) * (g * g)
        nm_ref[...] = nm
        nv_ref[...] = nv
        d_ref[...] = -ADAM_LR * ((nm * c1) / (jnp.sqrt(nv * c2) + ADAM_EPS) + ADAM_WD * w_ref[...])

    blk = pl.BlockSpec((None, tr, c), lambda l, i: (l, i, 0))
    land_specs = [pl.BlockSpec((N_DEV, tr, c), lambda l, i, j=j: (0, jnp.where(l == j, i, 0), 0)) for j in range(nl)]
    return pl.pallas_call(
        body, name=name, grid=(nl, r // tr), in_specs=land_specs + [blk, blk, blk],
        out_specs=[blk, blk, blk, blk], out_shape=[SDS((nl, r, c), F32)] * 4, compiler_params=_cp(2))(*lands, w, m, v)


ANY = pl.BlockSpec(memory_space=pl.ANY)
MESH = pl.DeviceIdType.MESH


def _me():
    return lax.axis_index("x"), lax.axis_index("y"), lax.axis_index("c")


def _peers():
    x, y, c = _me()
    out = []
    for k in range(1, N_DEV):
        px = 1 - x if k & 4 else x
        py = 1 - y if k & 2 else y
        pc = 1 - c if k & 1 else c
        out.append(((px, py, pc), 4 * px + 2 * py + pc))
    return out


HBM_SPEC = pl.BlockSpec(memory_space=pltpu.HBM)
SEM_SPEC = pl.BlockSpec(memory_space=pltpu.SEMAPHORE)
DATAFLOW = pltpu.SideEffectType.DATAFLOW_SIDE_EFFECTING


def _my_index():
    return 4 * lax.axis_index("x") + 2 * lax.axis_index("y") + lax.axis_index("c")


def _hbm(a):
    return pltpu.with_memory_space_constraint(a, pltpu.HBM)


NP = N_DEV - 1


def _push_copy(x_ref, land_ref, send, recv, pid, src_idx, dst_idx, scatter):
    src = x_ref.at[src_idx] if scatter else x_ref
    return pltpu.make_async_remote_copy(src_ref=src, dst_ref=land_ref.at[dst_idx], send_sem=send, recv_sem=recv,
                                        device_id=pid, device_id_type=MESH)


def push_start(name, xs, me, scatter, carry=None):
    n = len(xs)
    lands = []
    for a in xs:
        own = lax.dynamic_index_in_dim(a, me, 0, keepdims=True) if scatter else a[None]
        z = lax.empty((N_DEV,) + own.shape[1:], a.dtype)
        lands.append(lax.dynamic_update_slice(z, own, (me,) + (0,) * (own.ndim - 1)))
    ns = 2 * NP * n
    ops = xs + lands + ([carry] if carry is not None else [])
    na = len(ops)

    def body(*refs):
        x_refs, land_refs = refs[:n], refs[n:2 * n]
        sems = refs[na:na + ns]
        token = refs[-1]
        x, y, c = _me()
        mine = 4 * x + 2 * y + c
        for i in range(n):
            for k, (pid, pidx) in enumerate(_peers()):
                s = 2 * (NP * i + k)
                _push_copy(x_refs[i], land_refs[i], sems[s], sems[s + 1], pid, pidx, mine, scatter).start()
        token[...] = jnp.zeros_like(token)

    out_shape = ([pltpu.SemaphoreType.DMA(())] * ns + [pltpu.HBM(a.shape, a.dtype) for a in ops]
                 + [SDS((8, 128), F32)])
    res = pl.pallas_call(
        body, name=name, out_shape=out_shape, in_specs=[HBM_SPEC] * na,
        out_specs=[SEM_SPEC] * ns + [HBM_SPEC] * na + [pl.BlockSpec(memory_space=pltpu.VMEM)],
        input_output_aliases={i: ns + i for i in range(na)},
        compiler_params=pltpu.CompilerParams(has_side_effects=DATAFLOW))(*[_hbm(a) for a in ops])
    sems, thru, token = res[:ns], res[ns:-1], res[-1]
    handles = [dict(x=thru[i], land=thru[n + i], sems=list(sems[2 * NP * i:2 * NP * (i + 1)]), token=token)
               for i in range(n)]
    return (handles, thru[2 * n]) if carry is not None else handles


def push_wait(name, hds, after, scatter):
    n = len(hds)
    ns = 2 * NP

    def body(*refs):
        x_refs, land_refs = refs[:n], refs[n:2 * n]
        sems = refs[2 * n:2 * n + ns * n]
        for i in range(n):
            for k, (pid, pidx) in enumerate(_peers()):
                cp = _push_copy(x_refs[i], land_refs[i], sems[ns * i + 2 * k], sems[ns * i + 2 * k + 1], pid, pidx, pidx,
                                scatter)
                cp.wait_send()
                cp.wait_recv()

    arrs = [hd["x"] for hd in hds] + [hd["land"] for hd in hds]
    sems = [s for hd in hds for s in hd["sems"]]
    res = pl.pallas_call(
        body, name=name, out_shape=[pltpu.HBM(a.shape, a.dtype) for a in arrs],
        in_specs=[HBM_SPEC] * (2 * n) + [SEM_SPEC] * (ns * n) + [ANY], out_specs=[HBM_SPEC] * (2 * n),
        input_output_aliases={i: i for i in range(2 * n)},
        compiler_params=pltpu.CompilerParams(has_side_effects=DATAFLOW))(*arrs, *sems, after)
    return list(res[n:])


WEIGHTS = ['meta_tokens', 'norm_mix_g', 'norm_ffn_g', 'mla_w_down', 'mla_cq_norm_g', 'mla_ckv_norm_g', 'mla_w_uq',
           'mla_w_ukv', 'mla_q_head_g', 'mla_k_head_g', 'mla_w_o', 'hgrn_w_in', 'hgrn_lb_logits', 'hgrn_o_norm_g',
           'hgrn_w_o', 's5_lam_re', 's5_lam_im', 's5_log_dt', 's5_b_re', 's5_b_im', 's5_c_re', 's5_c_im', 's5_d',
           's5_w_glu', 'ret_w_in', 'ret_gn_g', 'ret_w_o', 'ffn_w_up', 'ffn_conv_w', 'ffn_conv_b', 'ffn_w_down']
BIG = ['mla_w_down', 'mla_w_uq', 'mla_w_ukv', 'mla_w_o', 'hgrn_w_in', 'hgrn_w_o', 's5_w_glu', 'ret_w_in', 'ret_w_o',
       'ffn_w_up', 'ffn_w_down']
SMALL_SH = ['meta_tokens', 's5_d', 'ret_gn_g', 'ffn_conv_w']
REP_S5 = ['s5_lam_re', 's5_lam_im', 's5_log_dt', 's5_b_re', 's5_b_im', 's5_c_re', 's5_c_im']
REP_REST = ['norm_mix_g', 'norm_ffn_g', 'mla_cq_norm_g', 'mla_ckv_norm_g', 'mla_q_head_g', 'mla_k_head_g',
            'hgrn_lb_logits', 'hgrn_o_norm_g', 'ffn_conv_b']
SMALL_REP = REP_REST + REP_S5
LANE = 128


def _flat(arrs, mult):
    v = jnp.concatenate([a.reshape(-1) for a in arrs])
    pad = (-v.shape[0]) % mult
    return jnp.pad(v, (0, pad)).reshape(-1, LANE)


def _unflat(flat2d, like):
    v = flat2d.reshape(-1)
    out, o = [], 0
    for a in like:
        out.append(v[o:o + a.size].reshape(a.shape))
        o += a.size
    return out


def _lb_of(logits):
    cum = jnp.cumsum(jax.nn.softmax(logits, axis=0), axis=0)
    return (cum - cum[0:1])[1:2]


def _cols_to_blocks(g):
    k, n = g.shape
    return g.reshape(k, N_DEV, n // N_DEV).transpose(1, 0, 2)


def _blocks_to_cols(wb):
    nb, k, n = wb.shape
    return wb.transpose(1, 0, 2).reshape(k, nb * n)


SUBS = ['mla', 'ffn0', 'hgrn', 'ffn1', 's5', 'ffn2', 'ret', 'ffn3']
GROUPS = [[('mla_w_down', 0), ('mla_w_uq', 0), ('mla_w_ukv', 0), ('mla_w_o', 0)],
          [('ffn_w_up', 0), ('ffn_w_down', 0)],
          [('hgrn_w_in', 0), ('hgrn_w_o', 0)],
          [('ffn_w_up', 1), ('ffn_w_down', 1)],
          [('s5_w_glu', 0)],
          [('ffn_w_up', 2), ('ffn_w_down', 2)],
          [('ret_w_in', 0), ('ret_w_o', 0)],
          [('ffn_w_up', 3), ('ffn_w_down', 3)]]


def _pack8(parts, mult):
    v = jnp.concatenate(parts, axis=1)
    return jnp.pad(v, ((0, 0), (0, (-v.shape[1]) % mult))).reshape(N_DEV, -1, LANE)


def _sub_weights(k, got, rep, tabs, lp):
    ngm, ngf = rep['norm_mix_g'], rep['norm_ffn_g']
    if k == 0:
        return dict(ng=ngm[0:1], wdown=got[0].reshape(D, -1), gcq=rep['mla_cq_norm_g'], gckv=rep['mla_ckv_norm_g'],
                    wuq=got[1], wukv=got[2], gq=rep['mla_q_head_g'], gk=rep['mla_k_head_g'], wo=got[3].reshape(D, D),
                    mc=mla_consts(lp))
    if k == 2:
        return dict(ng=ngm[1:2], win=got[0], lb=tabs['lb'], go=rep['hgrn_o_norm_g'], wo=got[1].reshape(D, D))
    if k == 4:
        return dict(ng=ngm[2:3], tb=tabs['tb'], dsk=tabs['s5_d'], wglu=_blocks_to_cols(got[0]))
    if k == 6:
        return dict(ng=ngm[3:4], win=_blocks_to_cols(got[0]), gn=tabs['ret_gn_g'], wo=got[1].reshape(2 * D, D),
                    rc=ret_consts(lp))
    i = k // 2
    return dict(ng=ngf[i:i + 1], up=got[0], cw=tabs['conv_w'][:, i].reshape(2, 4, 3, 1, FFN_B),
                cb=rep['ffn_conv_b'][i].reshape(2, 4, 1, FFN_B), down=got[1].reshape(4, FFN_B, D))


def _sub_grad_blocks(k, g):
    if k == 0:
        parts = [g['wdown'], g['wuq'], g['wukv'], g['wo']]
    elif k == 2:
        parts = [g['win'], g['wo']]
    elif k == 4:
        parts = [_cols_to_blocks(g['wglu'])]
    elif k == 6:
        parts = [_cols_to_blocks(g['win']), g['wo']]
    else:
        parts = [g['up'], g['down']]
    return parts


_FWD = [mla_fwd, None, hgrn_fwd, None, s5_fwd, None, ret_fwd, None]
_BWD = [mla_bwd, None, hgrn_bwd, None, s5_bwd, None, ret_bwd, None]


def _step(args):
    w = {n: args[n] for n in WEIGHTS}
    x2, tgt = args['x'][0], args['loss_target'][0]

    lp = x2.shape[0] + OFF
    me = _my_index()
    mask = _rowmask(lp)
    rep = {n: w[n] for n in SMALL_REP}

    xs, slots = [], []
    for gi, grp in enumerate(GROUPS):
        items = [w[n][l].astype(BF) for n, l in grp] + ([_flat([w[n] for n in SMALL_SH], LANE)] if gi == 0 else [])
        slots.append((len(xs), len(items)))
        xs += items
    gh = push_start("gather_start", xs, me, scatter=False)

    def fetch(gi, after):
        s, cnt = slots[gi]
        return push_wait("gather_wait_" + SUBS[gi], gh[s:s + cnt], after, scatter=False)

    got = fetch(0, x2)
    sm, o, smp = got[-1].reshape(N_DEV, -1), 0, {}
    for n in SMALL_SH:
        smp[n] = sm[:, o:o + w[n].size].reshape((N_DEV,) + w[n].shape)
        o += w[n].size
    meta = smp['meta_tokens'].transpose(1, 0, 2).reshape(N_META, D)
    lb, lb_vjp = jax.vjp(_lb_of, rep['hgrn_lb_logits'])
    s5p = [rep[n][0] for n in ('s5_lam_re', 's5_lam_im', 's5_log_dt', 's5_b_re', 's5_b_im', 's5_c_re', 's5_c_im')]
    tb, tb_vjp = jax.vjp(s5_tables, *s5p)
    tabs = dict(lb=lb, tb=tb, s5_d=smp['s5_d'].reshape(1, D), ret_gn_g=smp['ret_gn_g'].reshape(1, 2 * D),
                conv_w=smp['ffn_conv_w'])
    h = jnp.concatenate([jnp.zeros((PAD, D), F32), meta, x2], axis=0)
    ws, saved = [], []
    for k in range(8):
        if k > 0:
            got = fetch(k, h)
        ws.append(_sub_weights(k, got, rep, tabs, lp))
        if k % 2:
            h, sv = ffn_fwd(k // 2, h, mask, ws[k])
        else:
            h, sv = _FWD[k](h, mask, ws[k])
        saved.append(sv)
    loss, dh = loss_head(h, tgt)

    gs, sh = [None] * 8, [None] * 8
    early = {}

    def emit(tag, grads, carry):
        blocks = [t.reshape((N_DEV, -1) + t.shape[-1:]) if t.ndim == 2 else t for t in grads]
        early[tag], carry = push_start("scatter_start_mla_" + tag, blocks, me, scatter=True, carry=carry)
        return carry

    for k in reversed(range(1, 8)):
        if k % 2:
            dh, gs[k] = ffn_bwd(k // 2, dh, mask, ws[k], saved[k])
        else:
            dh, gs[k] = _BWD[k](dh, mask, ws[k], saved[k])
        blocks = [b.reshape((N_DEV,) + w[n].shape[1:]) for b, (n, _) in zip(_sub_grad_blocks(k, gs[k]), GROUPS[k])]
        sh[k], dh = push_start("scatter_start_" + SUBS[k], blocks, me, scatter=True, carry=dh)
        if k == 4:
            gs5 = _flat(list(tb_vjp(gs[4]['tb'])), 8 * LANE)
            rh_s5, dh = push_start("small_grads_start_s5", [gs5], me, scatter=False, carry=dh)
    dh, gs[0] = mla_bwd(dh, mask, ws[0], saved[0], emit)
    dmeta = dh[PAD:OFF].reshape(N_META, N_DEV, D // N_DEV).transpose(1, 0, 2)
    dcw = jnp.stack([gs[2 * i + 1]['cw'].reshape(N_DEV, 3, FFN_B) for i in range(4)], axis=1)
    last = push_start("scatter_start_mla", [gs[0]['wdown'].reshape(N_DEV, D // N_DEV, -1),
                                            _pack8([t.reshape(N_DEV, -1) for t in (dmeta, gs[4]['dsk'], gs[6]['gn'], dcw)], LANE)],
                      me, scatter=True)
    sh[0] = [last[0], early['wu'][0], early['wu'][1], early['wo'][0], last[1]]
    grad_x = dh[OFF:]

    g_rep = {
        'norm_mix_g': jnp.concatenate([gs[k]['ng'] for k in (0, 2, 4, 6)], axis=0),
        'norm_ffn_g': jnp.concatenate([gs[k]['ng'] for k in (1, 3, 5, 7)], axis=0),
        'mla_cq_norm_g': gs[0]['gcq'], 'mla_ckv_norm_g': gs[0]['gckv'], 'mla_q_head_g': gs[0]['gq'],
        'mla_k_head_g': gs[0]['gk'], 'hgrn_lb_logits': lb_vjp(gs[2]['lb'])[0], 'hgrn_o_norm_g': gs[2]['go'],
        'ffn_conv_b': jnp.stack([gs[k]['cb'].reshape(-1) for k in (1, 3, 5, 7)], axis=0),
    }
    loss_part = loss[0, 0:1]
    grep = _flat([g_rep[n] for n in REP_REST] + [loss_part], 8 * LANE)
    rh = push_start("small_grads_start", [grep], me, scatter=False)

    lands = {n: [None] * w[n].shape[0] for n in BIG}
    res = {}
    late = [n for n, _ in GROUPS[0]]
    for k in reversed(range(1, 8)):
        got = push_wait("scatter_wait_" + SUBS[k], sh[k], grep, scatter=True)
        for (n, l), t in zip(GROUPS[k], got):
            lands[n][l] = t
    for n in BIG:
        if n not in late:
            res[n] = adamw("adam_" + n, lands[n], w[n], args['m_' + n], args['v_' + n])
    after = res['ffn_w_up'][1]
    got = push_wait("scatter_wait_" + SUBS[0], sh[0], after, scatter=True)
    small_land = got[-1]
    (rep_land,) = push_wait("small_grads_wait", rh, after, scatter=False)
    (s5_land,) = push_wait("small_grads_wait_s5", rh_s5, after, scatter=False)
    for (n, _), t in zip(GROUPS[0], got):
        res[n] = adamw("adam_" + n, [t], w[n], args['m_' + n], args['v_' + n])

    def flat_adam(name, land, names, mult, extra=()):
        like = [w[n] for n in names]
        pad = [jnp.zeros_like(e) for e in extra]
        out = adamw(name, [land], _flat(like + pad, mult)[None], _flat([args['m_' + n] for n in names] + pad, mult)[None],
                    _flat([args['v_' + n] for n in names] + pad, mult)[None])
        for n, parts in zip(names, zip(*[_unflat(t, like) for t in out])):
            res[n] = list(parts)
        return out[0]

    flat_adam("adam_small_sharded", small_land, SMALL_SH, LANE)
    flat_adam("adam_s5_replicated", s5_land, REP_S5, 8 * LANE)
    gsum = flat_adam("adam_small_replicated", rep_land, REP_REST, 8 * LANE, extra=[loss_part])
    total = gsum.reshape(-1)[sum(w[n].size for n in REP_REST)]
    outs = [total, grad_x[None]]
    for k in range(4):
        outs += [res[n][k] for n in WEIGHTS]
    return tuple(outs)


def kernel(x, meta_tokens, norm_mix_g, norm_ffn_g, mla_w_down, mla_cq_norm_g, mla_ckv_norm_g, mla_w_uq, mla_w_ukv, mla_q_head_g, mla_k_head_g, mla_w_o, hgrn_w_in, hgrn_lb_logits, hgrn_o_norm_g, hgrn_w_o, s5_lam_re, s5_lam_im, s5_log_dt, s5_b_re, s5_b_im, s5_c_re, s5_c_im, s5_d, s5_w_glu, ret_w_in, ret_gn_g, ret_w_o, ffn_w_up, ffn_conv_w, ffn_conv_b, ffn_w_down, loss_target, m_meta_tokens, m_norm_mix_g, m_norm_ffn_g, m_mla_w_down, m_mla_cq_norm_g, m_mla_ckv_norm_g, m_mla_w_uq, m_mla_w_ukv, m_mla_q_head_g, m_mla_k_head_g, m_mla_w_o, m_hgrn_w_in, m_hgrn_lb_logits, m_hgrn_o_norm_g, m_hgrn_w_o, m_s5_lam_re, m_s5_lam_im, m_s5_log_dt, m_s5_b_re, m_s5_b_im, m_s5_c_re, m_s5_c_im, m_s5_d, m_s5_w_glu, m_ret_w_in, m_ret_gn_g, m_ret_w_o, m_ffn_w_up, m_ffn_conv_w, m_ffn_conv_b, m_ffn_w_down, v_meta_tokens, v_norm_mix_g, v_norm_ffn_g, v_mla_w_down, v_mla_cq_norm_g, v_mla_ckv_norm_g, v_mla_w_uq, v_mla_w_ukv, v_mla_q_head_g, v_mla_k_head_g, v_mla_w_o, v_hgrn_w_in, v_hgrn_lb_logits, v_hgrn_o_norm_g, v_hgrn_w_o, v_s5_lam_re, v_s5_lam_im, v_s5_log_dt, v_s5_b_re, v_s5_b_im, v_s5_c_re, v_s5_c_im, v_s5_d, v_s5_w_glu, v_ret_w_in, v_ret_gn_g, v_ret_w_o, v_ffn_w_up, v_ffn_conv_w, v_ffn_conv_b, v_ffn_w_down):
    return _step(dict(locals()))
```

```python
import functools
import math

import jax
import jax.numpy as jnp
import numpy as np
from jax import lax
from jax.experimental import pallas as pl
from jax.experimental.pallas import tpu as pltpu

F32 = jnp.float32
BF = jnp.bfloat16
SDS = jax.ShapeDtypeStruct

N_DEV = 8
D = 1024
N_META = 16
PAD = 48
OFF = PAD + N_META
CH = 64
EPS = 1e-6
NEG = -1e30
ROPE_BASE = 10000.0

MLA_H, MLA_NOPE, MLA_ROPE, MLA_V = 8, 128, 64, 128
MLA_QK = MLA_NOPE + MLA_ROPE
MLA_QL, MLA_KVL = 384, 256
HG_H, HG_D, HG_C = 8, 128, 16
S5_G, S5_P, S5_K = 64, 64, 16
S5_SG = 8
RET_H, RET_DK, RET_DV = 4, 256, 512
FFN_F = 2816
FFN_B = 704

ADAM_LR, ADAM_B1, ADAM_B2, ADAM_EPS, ADAM_WD, ADAM_STEP = 0.001, 0.9, 0.999, 1e-08, 0.01, 10

VMEM_LIMIT = 56 * 1024 * 1024
ARB = "arbitrary"


def _cp(n):
    return pltpu.CompilerParams(dimension_semantics=(ARB,) * n, vmem_limit_bytes=VMEM_LIMIT)


def _bdot(a, b, ca, cb):
    return lax.dot_general(a.astype(BF), b.astype(BF), (((ca,), (cb,)), ((), ())), preferred_element_type=F32)


@jax.custom_vjp
def mm(a, b):
    return _bdot(a, b, 1, 0)


@jax.custom_vjp
def mm_nt(a, b):
    return _bdot(a, b, 1, 1)


@jax.custom_vjp
def mm_tn(a, b):
    return _bdot(a, b, 0, 0)


mm.defvjp(lambda a, b: (mm(a, b), (a, b)),
          lambda r, g: (mm_nt(g, r[1]).astype(r[0].dtype), mm_tn(r[0], g).astype(r[1].dtype)))
mm_nt.defvjp(lambda a, b: (mm_nt(a, b), (a, b)),
             lambda r, g: (mm(g, r[1]).astype(r[0].dtype), mm_tn(g, r[0]).astype(r[1].dtype)))
mm_tn.defvjp(lambda a, b: (mm_tn(a, b), (a, b)),
             lambda r, g: (mm_nt(r[1], g).astype(r[0].dtype), mm(r[0], g).astype(r[1].dtype)))


def _xdot(a, b, ca, cb):
    return lax.dot_general(a, b, (((ca,), (cb,)), ((), ())), preferred_element_type=F32,
                           precision=lax.Precision.HIGHEST)


@jax.custom_vjp
def cright(x, r):
    return _xdot(x, r, 1, 0)


cright.defvjp(lambda x, r: (cright(x, r), r), lambda r, g: (_xdot(g, r, 1, 1), jnp.zeros_like(r)))


def _shift_raw(x, s):
    n = x.shape[0]
    r = lax.broadcasted_iota(jnp.int32, x.shape, 0)
    y = pltpu.roll(x, s % n, 0)
    return jnp.where((r >= s) & (r < n + s), y, 0.0)


@functools.partial(jax.custom_vjp, nondiff_argnums=(1,))
def shift_rows(x, s):
    return _shift_raw(x, s)


shift_rows.defvjp(lambda x, s: (_shift_raw(x, s), None), lambda s, _, g: (_shift_raw(g, -s),))


def _seg_shift_raw(x, s, seg, up):
    n = x.shape[0]
    r = lax.broadcasted_iota(jnp.int32, x.shape, 0) % seg
    if up:
        return jnp.where(r < seg - s, pltpu.roll(x, n - s, 0), 0.0)
    return jnp.where(r >= s, pltpu.roll(x, s, 0), 0.0)


@functools.partial(jax.custom_vjp, nondiff_argnums=(1, 2))
def seg_shift(x, s, seg):
    return _seg_shift_raw(x, s, seg, False)


seg_shift.defvjp(lambda x, s, seg: (_seg_shift_raw(x, s, seg, False), None),
                 lambda s, seg, _, g: (_seg_shift_raw(g, s, seg, True),))


def _seg_cumsum(x, seg):
    s = 1
    while s < seg:
        x = x + seg_shift(x, s, seg)
        s *= 2
    return x


def _rms(x, g):
    return x * lax.rsqrt(jnp.mean(x * x, axis=-1, keepdims=True) + EPS) * g


def _silu(x):
    return x * jax.nn.sigmoid(x)


def _mm_call(name, a, b, *, grid, a_spec, b_spec, o_shape, o_spec, dims, acc_shape, res=None, res_spec=None,
             mask_tm=None):
    nk = grid[2]

    def body(*refs):
        if res is None:
            a_ref, b_ref, o_ref = refs[:3]
        else:
            a_ref, b_ref, r_ref, o_ref = refs[:4]
        k = pl.program_id(2)

        def dot():
            return lax.dot_general(a_ref[...].astype(BF), b_ref[...].astype(BF), dims, preferred_element_type=F32)

        def finish(v):
            if res is not None:
                v = v + r_ref[...].astype(F32)
                rows = pl.program_id(0) * mask_tm + lax.broadcasted_iota(jnp.int32, v.shape, 0)
                v = jnp.where(rows >= PAD, v, 0.0)
            o_ref[...] = v.astype(o_ref.dtype)

        if nk == 1:
            finish(dot())
            return
        acc = refs[-1]

        @pl.when(k == 0)
        def _():
            acc[...] = dot()

        @pl.when((k > 0) & (k < nk - 1))
        def _():
            acc[...] += dot()

        @pl.when(k == nk - 1)
        def _():
            finish(acc[...] + dot())

    ins = [a, b] + ([res] if res is not None else [])
    specs = [a_spec, b_spec] + ([res_spec] if res is not None else [])
    scratch = [pltpu.VMEM(acc_shape, F32)] if nk > 1 else []
    return pl.pallas_call(body, name=name, grid=grid, in_specs=specs, out_specs=o_spec, out_shape=o_shape,
                          scratch_shapes=scratch, compiler_params=_cp(3))(*ins)


NN = (((1,), (0,)), ((), ()))
NT = (((1,), (1,)), ((), ()))
TN = (((0,), (0,)), ((), ()))


def _row_tile(lp):
    for t in (832, 640, 320, 64):
        if lp % t == 0:
            return t
    raise ValueError(lp)


def _col_tile(n):
    for t in (1024, 768, 512, 384, 256, 128):
        if n % t == 0:
            return t
    return n


def lin(name, a, w, out_dtype=F32, res=None):
    m, k = a.shape
    n = w.shape[1]
    tm, tn, tc = _row_tile(m), _col_tile(n), _col_tile(k)
    return _mm_call(name, a, w, grid=(m // tm, n // tn, k // tc),
                    a_spec=pl.BlockSpec((tm, tc), lambda i, j, kk: (i, kk)),
                    b_spec=pl.BlockSpec((tc, tn), lambda i, j, kk: (kk, j)),
                    o_shape=SDS((m, n), out_dtype), o_spec=pl.BlockSpec((tm, tn), lambda i, j, kk: (i, j)),
                    dims=NN, acc_shape=(tm, tn), res=res,
                    res_spec=pl.BlockSpec((tm, tn), lambda i, j, kk: (i, j)), mask_tm=tm)


def lin_bo(name, a, wb, out_dtype=F32):
    m, k = a.shape
    nb, _, n = wb.shape
    tm = _row_tile(m)
    return _mm_call(name, a, wb, grid=(m // tm, nb, 1),
                    a_spec=pl.BlockSpec((tm, k), lambda i, j, kk: (i, 0)),
                    b_spec=pl.BlockSpec((None, k, n), lambda i, j, kk: (j, 0, 0)),
                    o_shape=SDS((nb, m, n), out_dtype), o_spec=pl.BlockSpec((None, tm, n), lambda i, j, kk: (j, i, 0)),
                    dims=NN, acc_shape=(tm, n))


def lin_bi(name, ab, wb, out_dtype=F32, res=None):
    nb, m, k = ab.shape
    n = wb.shape[2]
    tm, tn = _row_tile(m), _col_tile(n)
    return _mm_call(name, ab, wb, grid=(m // tm, n // tn, nb),
                    a_spec=pl.BlockSpec((None, tm, k), lambda i, j, kk: (kk, i, 0)),
                    b_spec=pl.BlockSpec((None, k, tn), lambda i, j, kk: (kk, 0, j)),
                    o_shape=SDS((m, n), out_dtype), o_spec=pl.BlockSpec((tm, tn), lambda i, j, kk: (i, j)),
                    dims=NN, acc_shape=(tm, tn), res=res,
                    res_spec=pl.BlockSpec((tm, tn), lambda i, j, kk: (i, j)), mask_tm=tm)


def lin_t(name, g, w, out_dtype=F32):
    m, n = g.shape
    k = w.shape[0]
    tm, tk, tc = _row_tile(m), _col_tile(k), _col_tile(n)
    return _mm_call(name, g, w, grid=(m // tm, k // tk, n // tc),
                    a_spec=pl.BlockSpec((tm, tc), lambda i, j, kk: (i, kk)),
                    b_spec=pl.BlockSpec((tk, tc), lambda i, j, kk: (j, kk)),
                    o_shape=SDS((m, k), out_dtype), o_spec=pl.BlockSpec((tm, tk), lambda i, j, kk: (i, j)),
                    dims=NT, acc_shape=(tm, tk))


def lin_t_bi(name, gb, wb, out_dtype=F32):
    nb, m, n = gb.shape
    k = wb.shape[1]
    tm, tk = _row_tile(m), _col_tile(k)
    return _mm_call(name, gb, wb, grid=(m // tm, k // tk, nb),
                    a_spec=pl.BlockSpec((None, tm, n), lambda i, j, kk: (kk, i, 0)),
                    b_spec=pl.BlockSpec((None, tk, n), lambda i, j, kk: (kk, j, 0)),
                    o_shape=SDS((m, k), out_dtype), o_spec=pl.BlockSpec((tm, tk), lambda i, j, kk: (i, j)),
                    dims=NT, acc_shape=(tm, tk))


def lin_t_bo(name, g, wb, out_dtype=F32):
    m, n = g.shape
    nb, k, _ = wb.shape
    tm = _row_tile(m)
    return _mm_call(name, g, wb, grid=(m // tm, nb, 1),
                    a_spec=pl.BlockSpec((tm, n), lambda i, j, kk: (i, 0)),
                    b_spec=pl.BlockSpec((None, k, n), lambda i, j, kk: (j, 0, 0)),
                    o_shape=SDS((nb, m, k), out_dtype), o_spec=pl.BlockSpec((None, tm, k), lambda i, j, kk: (j, i, 0)),
                    dims=NT, acc_shape=(tm, k))


def wgrad(name, a, g):
    m, k = a.shape
    n = g.shape[1]
    tm, tn = _row_tile(m), _col_tile(n)
    return _mm_call(name, a, g, grid=(1, n // tn, m // tm),
                    a_spec=pl.BlockSpec((tm, k), lambda i, j, kk: (kk, 0)),
                    b_spec=pl.BlockSpec((tm, tn), lambda i, j, kk: (kk, j)),
                    o_shape=SDS((k, n), F32), o_spec=pl.BlockSpec((k, tn), lambda i, j, kk: (0, j)),
                    dims=TN, acc_shape=(k, tn))


def wgrad_bo(name, a, gb):
    m, k = a.shape
    nb, _, n = gb.shape
    tm = _row_tile(m)
    return _mm_call(name, a, gb, grid=(nb, 1, m // tm),
                    a_spec=pl.BlockSpec((tm, k), lambda i, j, kk: (kk, 0)),
                    b_spec=pl.BlockSpec((None, tm, n), lambda i, j, kk: (i, kk, 0)),
                    o_shape=SDS((nb, k, n), F32), o_spec=pl.BlockSpec((None, k, n), lambda i, j, kk: (i, 0, 0)),
                    dims=TN, acc_shape=(k, n))


def wgrad_bi(name, zb, g):
    nb, m, k = zb.shape
    n = g.shape[1]
    tm, tn = _row_tile(m), _col_tile(n)
    return _mm_call(name, zb, g, grid=(nb, n // tn, m // tm),
                    a_spec=pl.BlockSpec((None, tm, k), lambda i, j, kk: (i, kk, 0)),
                    b_spec=pl.BlockSpec((tm, tn), lambda i, j, kk: (kk, j)),
                    o_shape=SDS((nb, k, n), F32), o_spec=pl.BlockSpec((None, k, tn), lambda i, j, kk: (i, 0, j)),
                    dims=TN, acc_shape=(k, tn))


class Arg:
    def __init__(self, arr, block, imap, shared=False, acc=False):
        self.arr, self.block, self.imap = arr, block, imap
        self.shared = shared
        self.acc = acc

    @property
    def spec(self):
        return pl.BlockSpec(self.block, self.imap)

    def vshape(self):
        return tuple(b for b in self.block if b is not None)


def _rev(arg, nt):
    return pl.BlockSpec(arg.block, lambda o, t, _f=arg.imap: _f(o, nt - 1 - t))


def seq_fwd(name, fn, grid, params, consts, xs, outs, carries=(), save_dtype=F32):
    no, nt = grid
    n_p, n_c, n_x, n_y, n_k = len(params), len(consts), len(xs), len(outs), len(carries)

    def body(*refs):
        p_refs = refs[:n_p]
        c_refs = refs[n_p:n_p + n_c]
        x_refs = refs[n_p + n_c:n_p + n_c + n_x]
        r = n_p + n_c + n_x
        y_refs = refs[r:r + n_y]
        s_refs = refs[r + n_y:r + n_y + n_k]
        k_refs = refs[r + n_y + n_k:]
        t = pl.program_id(1)

        if n_k:
            @pl.when(t == 0)
            def _():
                for k in k_refs:
                    k[...] = jnp.zeros_like(k)

        carry = tuple(k[...] for k in k_refs)
        for s, c in zip(s_refs, carry):
            s[...] = c.astype(s.dtype)
        new_carry, ys = fn(tuple(p[...] for p in p_refs), tuple(c[...] for c in c_refs), carry,
                           tuple(x[...] for x in x_refs))
        for k, c in zip(k_refs, new_carry):
            k[...] = c
        for y_ref, y in zip(y_refs, ys):
            y_ref[...] = y.astype(y_ref.dtype)

    out_shape = [SDS(s, d) for (s, d, _, _) in outs]
    out_specs = [pl.BlockSpec(b, im) for (_, _, b, im) in outs]
    for cs in carries:
        out_shape.append(SDS((no, nt) + cs, save_dtype))
        out_specs.append(pl.BlockSpec((None, None) + cs, lambda o, t, _n=len(cs): (o, t) + (0,) * _n))
    res = pl.pallas_call(
        body, name=name, grid=grid, in_specs=[a.spec for a in list(params) + list(consts) + list(xs)],
        out_specs=out_specs, out_shape=out_shape, scratch_shapes=[pltpu.VMEM(cs, F32) for cs in carries],
        compiler_params=_cp(2))(*[a.arr for a in list(params) + list(consts) + list(xs)])
    return res[:n_y], res[n_y:]


def seq_bwd(name, fn, grid, params, consts, xs, dys, saved=(), carries=()):
    no, nt = grid
    n_p, n_c, n_x, n_y, n_k = len(params), len(consts), len(xs), len(dys), len(carries)

    def body(*refs):
        p_refs = refs[:n_p]
        c_refs = refs[n_p:n_p + n_c]
        x_refs = refs[n_p + n_c:n_p + n_c + n_x]
        r = n_p + n_c + n_x
        g_refs = refs[r:r + n_y]
        s_refs = refs[r + n_y:r + n_y + n_k]
        r = r + n_y + n_k
        dx_refs = refs[r:r + n_x]
        dp_refs = refs[r + n_x:r + n_x + n_p]
        k_refs = refs[r + n_x + n_p:]
        o = pl.program_id(0)
        t = pl.program_id(1)

        if n_k:
            @pl.when(t == 0)
            def _():
                for k in k_refs:
                    k[...] = jnp.zeros_like(k)

        for a, dp in zip(params, dp_refs):
            @pl.when((t == 0) & (o == 0) if a.shared else (t == 0))
            def _(dp=dp):
                dp[...] = jnp.zeros_like(dp)

        for a, dx in zip(xs, dx_refs):
            if a.acc:
                @pl.when(t == 0)
                def _(dx=dx):
                    dx[...] = jnp.zeros_like(dx)

        consts_v = tuple(c[...] for c in c_refs)

        def f(pv, cv, xv):
            return fn(pv, consts_v, cv, xv)

        pv = tuple(p[...] for p in p_refs)
        cv = tuple(s[...].astype(F32) for s in s_refs)
        xv = tuple(x[...] for x in x_refs)
        (new_carry, ys), vjp = jax.vjp(f, pv, cv, xv)
        cot = (tuple(k[...] for k in k_refs), tuple(g[...].astype(y.dtype) for g, y in zip(g_refs, ys)))
        dpv, dcv, dxv = vjp(cot)
        for k, c in zip(k_refs, dcv):
            k[...] = c
        for dp, v in zip(dp_refs, dpv):
            dp[...] += v
        for a, dx, v in zip(xs, dx_refs, dxv):
            if a.acc:
                dx[...] += v
            else:
                dx[...] = v.astype(dx.dtype)

    in_specs = ([_rev(a, nt) for a in list(params) + list(consts) + list(xs) + list(dys)]
                + [pl.BlockSpec((None, None) + cs, lambda o, t, _n=len(cs): (o, nt - 1 - t) + (0,) * _n) for cs in carries])
    out_shape = [SDS(a.arr.shape, F32) for a in xs] + [SDS(a.arr.shape, F32) for a in params]
    out_specs = [_rev(a, nt) for a in list(xs) + list(params)]
    res = pl.pallas_call(
        body, name=name, grid=grid, in_specs=in_specs, out_specs=out_specs, out_shape=out_shape,
        scratch_shapes=[pltpu.VMEM(cs, F32) for cs in carries], compiler_params=_cp(2))(
            *[a.arr for a in list(params) + list(consts) + list(xs) + list(dys)], *saved)
    return res[:n_x], res[n_x:]


def _rowmask(lp):
    return (jnp.arange(lp) >= PAD).astype(F32)[:, None]


def _norm_fn(p, c, k, x):
    return (), (_rms(x[0] * c[0], p[0]),)


def _norm_b_fn(p, c, k, x):
    h = x[0] * c[0]
    return (), (_rms(h, p[0]), h)


def norm_fwd(name, h, g, mask, out_dtype=BF):
    lp, d = h.shape
    tr = _row_tile(lp)
    row = lambda o, t: (t, 0)
    (a,), _ = seq_fwd(name, _norm_fn, (1, lp // tr), [Arg(g, (1, d), lambda o, t: (0, 0), shared=True)],
                      [Arg(mask, (tr, 1), row)], [Arg(h, (tr, d), row)], [((lp, d), out_dtype, (tr, d), row)])
    return a


def norm_bwd(name, h, g, mask, da, dskip):
    lp, d = h.shape
    tr = _row_tile(lp)
    row = lambda o, t: (t, 0)
    (dh,), (dg,) = seq_bwd(name, _norm_b_fn, (1, lp // tr), [Arg(g, (1, d), lambda o, t: (0, 0), shared=True)],
                           [Arg(mask, (tr, 1), row)], [Arg(h, (tr, d), row)],
                           [Arg(da, (tr, d), row), Arg(dskip, (tr, d), row)])
    return dh, dg


def _ffn_tile(lp):
    return 320 if (lp % 320 == 0 and lp > 320) else 64


def _conv_rows(ext, w, b, n):
    u2 = ext[8:8 + n]
    u1 = pltpu.roll(ext, 1, 0)[8:8 + n]
    u0 = pltpu.roll(ext, 2, 0)[8:8 + n]
    return w[2] * u2 + w[1] * u1 + w[0] * u0 + b, (u0, u1, u2)


def ffn_up_core(name, a, wup, cw, cb):
    lp, kd = a.shape
    _, nj, _, fb = wup.shape
    tr = _ffn_tile(lp)
    nt = lp // tr

    def body(a_ref, wu_ref, w_ref, b_ref, u_ref, z_ref, u_s, halo_s):
        i = pl.program_id(1)

        @pl.when(i == 0)
        def _():
            u_s[...] = jnp.zeros_like(u_s)
            halo_s[...] = jnp.zeros_like(halo_s)

        old = (i + 1) % 2
        cs = []
        for s in range(2):
            tile = u_s[old, s]
            ext = jnp.concatenate([halo_s[s], tile], axis=0)
            c, _ = _conv_rows(ext, w_ref[s], b_ref[s], tr)
            cs.append(c)
            halo_s[s] = tile[tr - 8:]
        z_ref[...] = (_silu(cs[0]) * cs[1]).astype(z_ref.dtype)
        for s in range(2):
            un = lax.dot_general(a_ref[...], wu_ref[s], NN, preferred_element_type=F32)
            u_ref[s] = un
            u_s[i % 2, s] = un

    cur = lambda j, i: (0, j, jnp.minimum(i, nt - 1), 0)
    return pl.pallas_call(
        body, name=name, grid=(nj, nt + 1),
        in_specs=[pl.BlockSpec((tr, kd), lambda j, i: (jnp.minimum(i, nt - 1), 0)),
                  pl.BlockSpec((2, None, kd, fb), lambda j, i: (0, j, 0, 0)),
                  pl.BlockSpec((2, None, 3, 1, fb), lambda j, i: (0, j, 0, 0, 0)),
                  pl.BlockSpec((2, None, 1, fb), lambda j, i: (0, j, 0, 0))],
        out_specs=[pl.BlockSpec((2, None, tr, fb), cur),
                   pl.BlockSpec((None, tr, fb), lambda j, i: (j, jnp.maximum(i - 1, 0), 0))],
        out_shape=[SDS((2, nj, lp, fb), F32), SDS((nj, lp, fb), BF)],
        scratch_shapes=[pltpu.VMEM((2, 2, tr, fb), F32), pltpu.VMEM((2, 8, fb), F32)],
        compiler_params=_cp(2))(a, wup, cw, cb)


def ffn_core_bwd(name, u, dz, cw, cb, a):
    _, nj, lp, fb = u.shape
    kd = a.shape[1]
    tr = _ffn_tile(lp)
    nt = lp // tr
    nb8 = lp // 8

    def body(u_ref, up_ref, un_ref, dz_ref, dzn_ref, w_ref, b_ref, a_ref, du_ref, dw_ref, db_ref, dwup_ref, du_s):
        i = pl.program_id(1)

        @pl.when(i == 0)
        def _():
            dw_ref[...] = jnp.zeros_like(dw_ref)
            db_ref[...] = jnp.zeros_like(db_ref)
            dwup_ref[...] = jnp.zeros_like(dwup_ref)
            du_s[...] = jnp.zeros_like(du_s)

        old = (i + 1) % 2
        for s in range(2):
            dwup_ref[s] += lax.dot_general(a_ref[...], du_s[old, s], TN, preferred_element_type=F32)

        it = jnp.minimum(i, nt - 1)
        live = (i < nt).astype(F32)
        prev = jnp.where(it > 0, up_ref[...], 0.0)
        nxt = jnp.where(it < nt - 1, un_ref[...], 0.0)
        dz_e = jnp.concatenate([dz_ref[...], jnp.where(it < nt - 1, dzn_ref[...], 0.0)], axis=0)
        n = tr + 8
        cs, taps = [], []
        for s in range(2):
            ext = jnp.concatenate([prev[s], u_ref[s], nxt[s]], axis=0)
            c, tp = _conv_rows(ext, w_ref[s], b_ref[s], n)
            cs.append(c)
            taps.append(tp)
        sg = jax.nn.sigmoid(cs[0])
        dcs = [dz_e * cs[1] * sg * (1.0 + cs[0] * (1.0 - sg)), dz_e * cs[0] * sg]
        for s in range(2):
            dc = dcs[s]
            w = w_ref[s]
            d1 = pltpu.roll(dc, n - 1, 0)[:tr]
            d2 = pltpu.roll(dc, n - 2, 0)[:tr]
            dcm = dc[:tr]
            du = w[2] * dcm + w[1] * d1 + w[0] * d2
            du_ref[s] = du
            du_s[i % 2, s] = (du * live).astype(BF)
            for k in range(3):
                dw_ref[s, k] += live * jnp.sum(dcm * taps[s][k][:tr], axis=0, keepdims=True)
            db_ref[s] += live * jnp.sum(dcm, axis=0, keepdims=True)

    row = lambda j, i: (0, j, jnp.minimum(i, nt - 1), 0)
    return pl.pallas_call(
        body, name=name, grid=(nj, nt + 1),
        in_specs=[pl.BlockSpec((2, None, tr, fb), row),
                  pl.BlockSpec((2, None, 8, fb), lambda j, i: (0, j, jnp.maximum(jnp.minimum(i, nt - 1) * (tr // 8) - 1, 0), 0)),
                  pl.BlockSpec((2, None, 8, fb), lambda j, i: (0, j, jnp.minimum((i + 1) * (tr // 8), nb8 - 1), 0)),
                  pl.BlockSpec((None, tr, fb), lambda j, i: (j, jnp.minimum(i, nt - 1), 0)),
                  pl.BlockSpec((None, 8, fb), lambda j, i: (j, jnp.minimum((i + 1) * (tr // 8), nb8 - 1), 0)),
                  pl.BlockSpec((2, None, 3, 1, fb), lambda j, i: (0, j, 0, 0, 0)),
                  pl.BlockSpec((2, None, 1, fb), lambda j, i: (0, j, 0, 0)),
                  pl.BlockSpec((tr, kd), lambda j, i: (jnp.maximum(i - 1, 0), 0))],
        out_specs=[pl.BlockSpec((2, None, tr, fb), row),
                   pl.BlockSpec((2, None, 3, 1, fb), lambda j, i: (0, j, 0, 0, 0)),
                   pl.BlockSpec((2, None, 1, fb), lambda j, i: (0, j, 0, 0)),
                   pl.BlockSpec((2, None, kd, fb), lambda j, i: (0, j, 0, 0))],
        out_shape=[SDS(u.shape, F32), SDS(cw.shape, F32), SDS(cb.shape, F32), SDS((2, nj, kd, fb), F32)],
        scratch_shapes=[pltpu.VMEM((2, 2, tr, fb), BF)],
        compiler_params=_cp(2))(u, u, u, dz, dz, cw, cb, a)


def ffn_fwd(i, h, mask, w):
    a = norm_fwd(f"ffn{i}_norm", h, w["ng"], mask)
    u, z = ffn_up_core(f"ffn{i}_up_core", a, w["up"].reshape(2, 4, D, FFN_B), w["cw"], w["cb"])
    h2 = lin_bi(f"ffn{i}_down", z, w["down"], res=h)
    return h2, (h, a, u, z)


def ffn_bwd(i, dh2, mask, w, saved):
    h, a, u, z = saved
    lp = h.shape[0]
    g = {}
    g["down"] = wgrad_bi(f"ffn{i}_dwdown", z, dh2)
    dz = lin_t_bo(f"ffn{i}_dz", dh2, w["down"])
    du, g["cw"], g["cb"], dwup = ffn_core_bwd(f"ffn{i}_core_b", u, dz, w["cw"], w["cb"], a)
    du = du.reshape(8, lp, FFN_B)
    g["up"] = dwup.reshape(8, D, FFN_B)
    da = lin_t_bi(f"ffn{i}_da", du, w["up"])
    dh, g["ng"] = norm_bwd(f"ffn{i}_norm_b", h, w["ng"], mask, da, dh2)
    return dh, g


HG_HB = 4


def _hgrn_fn(p, c, k, x):
    lb, go = p
    q, f, iv, g = x[0][0], x[0][1], x[0][2], x[0][3]
    (st_all,) = k
    qs = _silu(q)
    forget = lb + (1.0 - lb) * jax.nn.sigmoid(f)
    logf = jnp.log(forget)
    kk = 1.0 - forget
    gc_all = _seg_cumsum(logf, HG_C)
    r = lax.broadcasted_iota(jnp.int32, (HG_C, HG_C), 0)
    cc = lax.broadcasted_iota(jnp.int32, (HG_C, HG_C), 1)
    ns = CH // HG_C
    cells = [(j, s) for j in range(HG_HB) for s in range(ns)]

    def blk(t, j, s):
        return t[HG_C * s:HG_C * (s + 1), HG_D * j:HG_D * (j + 1)]

    gl = {c: jnp.sum(blk(logf, *c), axis=0, keepdims=True) for c in cells}
    qd = {c: blk(qs, *c) * jnp.exp(blk(gc_all, *c)) for c in cells}
    ki = {c: blk(kk, *c) * jnp.exp(-blk(gc_all, *c)) for c in cells}
    up = {c: mm_tn(blk(iv, *c), blk(kk, *c) * jnp.exp(gl[c] - blk(gc_all, *c))) for c in cells}
    st, sts = {}, []
    for j in range(HG_HB):
        cur = st_all[j]
        for s in range(ns):
            st[(j, s)] = cur
            cur = cur * jnp.exp(gl[(j, s)]) + up[(j, s)]
        sts.append(cur)
    both = {c: mm_nt(qd[c], jnp.concatenate([st[c], ki[c]], axis=0)) for c in cells}
    oc = {c: mm(jnp.where(r >= cc, both[c][:, HG_D:], 0.0), blk(iv, *c)) + both[c][:, :HG_D] for c in cells}
    zs = []
    for j in range(HG_HB):
        o = jnp.concatenate([oc[(j, s)] for s in range(ns)], axis=0)
        zs.append(_rms(o, go) * _silu(g[:, HG_D * j:HG_D * (j + 1)]))
    return (jnp.stack(sts, axis=0),), (jnp.concatenate(zs, axis=1),)


def _hgrn_args(u4, lb, go):
    lp = u4.shape[2]
    wb = HG_HB * HG_D
    xs = [Arg(u4, (4, None, CH, wb), lambda o, t: (0, o, t, 0))]
    ps = [Arg(lb, (1, wb), lambda o, t: (0, o)), Arg(go, (1, HG_D), lambda o, t: (0, 0), shared=True)]
    return (HG_H // HG_HB, lp // CH), ps, xs


def hgrn_fwd(h, mask, w):
    lp = h.shape[0]
    a = norm_fwd("hgrn_norm", h, w["ng"], mask)
    u4 = lin_bo("hgrn_in", a, w["win"]).reshape(4, HG_H // HG_HB, lp, HG_HB * HG_D)
    grid, ps, xs = _hgrn_args(u4, w["lb"], w["go"])
    (z,), (st,) = seq_fwd("hgrn_core", _hgrn_fn, grid, ps, [], xs,
                          [((lp, D), BF, (CH, HG_HB * HG_D), lambda o, t: (t, o))], carries=[(HG_HB, HG_D, HG_D)])
    h2 = lin("hgrn_out", z, w["wo"], res=h)
    return h2, (h, a, u4, z, st)


def hgrn_bwd(dh2, mask, w, saved):
    h, a, u4, z, st = saved
    lp = h.shape[0]
    g = {}
    g["wo"] = wgrad("hgrn_dwo", z, dh2)
    dz = lin_t("hgrn_dz", dh2, w["wo"])
    grid, ps, xs = _hgrn_args(u4, w["lb"], w["go"])
    (du4,), (g["lb"], g["go"]) = seq_bwd("hgrn_core_b", _hgrn_fn, grid, ps, [], xs,
                                         [Arg(dz, (CH, HG_HB * HG_D), lambda o, t: (t, o))], saved=[st],
                                         carries=[(HG_HB, HG_D, HG_D)])
    du = du4.reshape(N_DEV, lp, HG_HB * HG_D)
    g["win"] = wgrad_bo("hgrn_dwin", a, du)
    da = lin_t_bi("hgrn_da", du, w["win"])
    dh, g["ng"] = norm_bwd("hgrn_norm_b", h, w["ng"], mask, da, dh2)
    return dh, g


S5_W = S5_SG * S5_P


def s5_tables(lam_re, lam_im, log_dt, b_re, b_im, c_re, c_im):
    dt = jnp.exp(log_dt)[:, None]
    mag = jnp.exp(lam_re * dt)
    abar_re = mag * jnp.cos(lam_im * dt)
    abar_im = mag * jnp.sin(lam_im * dt)
    den = lam_re * lam_re + lam_im * lam_im
    zoh_re = ((abar_re - 1.0) * lam_re + abar_im * lam_im) / den
    zoh_im = (abar_im * lam_re - (abar_re - 1.0) * lam_im) / den
    bbar_re = zoh_re[..., None] * b_re - zoh_im[..., None] * b_im
    bbar_im = zoh_re[..., None] * b_im + zoh_im[..., None] * b_re
    eye = jnp.eye(S5_SG, dtype=F32)

    def blockdiag_in(b):
        t = b.reshape(N_DEV, S5_SG, S5_P, S5_K).transpose(0, 1, 3, 2)
        return jnp.einsum("jakp,ab->jakbp", t, eye).reshape(N_DEV, S5_SG * S5_K, S5_W)

    def blockdiag_out(c):
        t = c.reshape(N_DEV, S5_SG, S5_K, S5_P).transpose(0, 1, 3, 2)
        return jnp.einsum("japk,ab->japbk", t, eye).reshape(N_DEV, S5_W, S5_SG * S5_K)

    wb = jnp.concatenate([blockdiag_in(bbar_re), blockdiag_in(bbar_im)], axis=2)
    wc = jnp.concatenate([blockdiag_out(c_re), -blockdiag_out(c_im)], axis=1)

    def powers(n):
        steps = n[:, None, None] * dt[None]
        pm = jnp.exp(lam_re[None] * steps)
        pr = (pm * jnp.cos(lam_im[None] * steps)).reshape(-1, N_DEV, S5_W).transpose(1, 0, 2)
        pi = (pm * jnp.sin(lam_im[None] * steps)).reshape(-1, N_DEV, S5_W).transpose(1, 0, 2)
        return jnp.concatenate([pr, pi], axis=2)

    apow = powers(2.0 ** jnp.arange(6, dtype=F32))[:, :, None, :]
    ptab = powers(jnp.arange(CH, dtype=F32) + 1.0)
    return wb, wc, apow, ptab


def _cmul(ar, ai, xr, xi):
    return ar * xr - ai * xi, ar * xi + ai * xr


S5_BB = 4


def _s5_fn(p, c, k, x):
    wb, wc, apow, ptab, dsk = p
    (a,) = x
    (x0,) = k
    blocks = range(S5_BB)
    aj = [a[:, 128 * j:128 * (j + 1)] for j in blocks]
    bu = [mm(aj[j], wb[j]) for j in blocks]
    xxs, x0n = [], []
    for j in blocks:
        xr, xi = bu[j][:, :S5_W], bu[j][:, S5_W:]
        for s in range(6):
            asr, asi = apow[j][s][:, :S5_W], apow[j][s][:, S5_W:]
            dr, di = _cmul(asr, asi, shift_rows(xr, 1 << s), shift_rows(xi, 1 << s))
            xr, xi = xr + dr, xi + di
        dr, di = _cmul(ptab[j][:, :S5_W], ptab[j][:, S5_W:], x0[j][:, :S5_W], x0[j][:, S5_W:])
        xx = jnp.concatenate([xr + dr, xi + di], axis=1)
        last = lax.broadcasted_iota(jnp.int32, xx.shape, 0) == CH - 1
        x0n.append(jnp.sum(jnp.where(last, xx, 0.0), axis=0, keepdims=True))
        xxs.append(xx)
    y = jnp.concatenate([mm(xxs[j], wc[j]) for j in blocks], axis=1)
    return (jnp.stack(x0n, axis=0),), (jax.nn.gelu(y + dsk * a),)


def _s5_args(a, tb, dsk):
    lp = a.shape[0]
    wb, wc, apow, ptab = tb
    ps = [Arg(wb, (S5_BB, 128, 2 * S5_W), lambda o, t: (o, 0, 0)), Arg(wc, (S5_BB, 2 * S5_W, 128), lambda o, t: (o, 0, 0)),
          Arg(apow, (S5_BB, 6, 1, 2 * S5_W), lambda o, t: (o, 0, 0, 0)),
          Arg(ptab, (S5_BB, CH, 2 * S5_W), lambda o, t: (o, 0, 0)), Arg(dsk, (1, 128 * S5_BB), lambda o, t: (0, o))]
    xs = [Arg(a, (CH, 128 * S5_BB), lambda o, t: (t, o))]
    return (N_DEV // S5_BB, lp // CH), ps, xs


def _glu_res_fn(p, c, k, x):
    h, vg = x
    return (), ((h + vg[:, :D] * jax.nn.sigmoid(vg[:, D:])) * c[0],)


def _glu_args(h, vg, mask):
    lp = h.shape[0]
    tr = _row_tile(lp)
    row = lambda o, t: (t, 0)
    return (1, lp // tr), [Arg(mask, (tr, 1), row)], [Arg(h, (tr, D), row), Arg(vg, (tr, 2 * D), row)], tr


def s5_fwd(h, mask, w):
    lp = h.shape[0]
    a = norm_fwd("s5_norm", h, w["ng"], mask, out_dtype=F32)
    grid, ps, xs = _s5_args(a, w["tb"], w["dsk"])
    (z,), (st,) = seq_fwd("s5_core", _s5_fn, grid, ps, [], xs,
                          [((lp, D), BF, (CH, 128 * S5_BB), lambda o, t: (t, o))], carries=[(S5_BB, 1, 2 * S5_W)])
    vg = lin("s5_glu", z, w["wglu"])
    grid2, cs, xs2, tr = _glu_args(h, vg, mask)
    (h2,), _ = seq_fwd("s5_res", _glu_res_fn, grid2, [], cs, xs2, [((lp, D), F32, (tr, D), lambda o, t: (t, 0))])
    return h2, (h, a, z, vg, st)


def s5_bwd(dh2, mask, w, saved):
    h, a, z, vg, st = saved
    g = {}
    grid2, cs, xs2, tr = _glu_args(h, vg, mask)
    (dskip, dvg), _ = seq_bwd("s5_res_b", _glu_res_fn, grid2, [], cs, xs2, [Arg(dh2, (tr, D), lambda o, t: (t, 0))])
    g["wglu"] = wgrad("s5_dwglu", z, dvg)
    dz = lin_t("s5_dz", dvg, w["wglu"])
    grid, ps, xs = _s5_args(a, w["tb"], w["dsk"])
    (da,), dps = seq_bwd("s5_core_b", _s5_fn, grid, ps, [], xs, [Arg(dz, (CH, 128 * S5_BB), lambda o, t: (t, o))],
                         saved=[st], carries=[(S5_BB, 1, 2 * S5_W)])
    g["tb"] = tuple(dps[:4])
    g["dsk"] = dps[4]
    dh, g["ng"] = norm_bwd("s5_norm_b", h, w["ng"], mask, da, dskip)
    return dh, g


def _rope_angles(lp, dim):
    pos = np.maximum(np.arange(lp, dtype=np.float32) - PAD, 0.0).astype(np.float32)
    inv = (1.0 / (ROPE_BASE ** (np.arange(0, dim, 2, dtype=np.float32) / dim))).astype(np.float32)
    return (pos[:, None] * inv[None, :]).astype(np.float32)


def ret_consts(lp):
    f = np.float32
    ang = _rope_angles(lp, RET_DK)
    lg = np.log(1.0 - np.exp2(-5.0 - np.arange(RET_H, dtype=f))).astype(f)
    p = np.arange(CH, dtype=f)
    diff = p[:, None] - p[None, :]
    decay = np.where(diff >= 0, np.exp(diff[None] * lg[:, None, None]), 0.0).astype(f)
    qd = np.exp((p[None, :] + 1.0) * lg[:, None])[..., None].astype(f)
    kd = np.exp((CH - 1.0 - p[None, :]) * lg[:, None])[..., None].astype(f)
    cd = np.exp(CH * lg)[:, None, None].astype(f)
    return np.cos(ang).astype(f), np.sin(ang).astype(f), decay, qd, kd, cd


def _ret_fn(p, c, k, x):
    (gn,) = p
    cos, sin, decay, qd, kd, cd = c
    (st,) = k
    (u,) = x
    hd = RET_DK // 2
    qk_w = RET_H * RET_DK
    heads = range(RET_H)

    def rope(t):
        t1, t2 = t[:, :hd], t[:, hd:]
        return jnp.concatenate([t1 * cos - t2 * sin, t1 * sin + t2 * cos], axis=1)

    qr = [rope(u[:, RET_DK * h:RET_DK * (h + 1)]) for h in heads]
    kr = [rope(u[:, qk_w + RET_DK * h:qk_w + RET_DK * (h + 1)]) * (RET_DK ** -0.5) for h in heads]
    v = [u[:, 2 * qk_w + RET_DV * h:2 * qk_w + RET_DV * (h + 1)] for h in heads]
    scores = [mm_nt(qr[h], kr[h]) for h in heads]
    inter = [mm(qr[h] * qd[h], st[h]) for h in heads]
    st_new = jnp.stack([st[h] * cd[h] + mm_tn(kr[h] * kd[h], v[h]) for h in heads], axis=0)
    o = [mm(scores[h] * decay[h], v[h]) + inter[h] for h in heads]
    zs = []
    for h in heads:
        mu = jnp.mean(o[h], axis=-1, keepdims=True)
        var = jnp.mean(jnp.square(o[h] - mu), axis=-1, keepdims=True)
        gate = u[:, 2 * qk_w + RET_H * RET_DV + RET_DV * h:2 * qk_w + RET_H * RET_DV + RET_DV * (h + 1)]
        zs.append((o[h] - mu) * lax.rsqrt(var + EPS) * gn[:, RET_DV * h:RET_DV * (h + 1)] * _silu(gate))
    return (st_new,), (jnp.concatenate(zs, axis=1),)


def _ret_args(u, gn, rc):
    lp, uw = u.shape
    cos, sin, decay, qd, kd, cd = rc
    full = lambda o, t: (0, 0, 0)
    ps = [Arg(gn, (1, RET_H * RET_DV), lambda o, t: (0, 0))]
    cs = [Arg(cos, (CH, RET_DK // 2), lambda o, t: (t, 0)), Arg(sin, (CH, RET_DK // 2), lambda o, t: (t, 0)),
          Arg(decay, (RET_H, CH, CH), full), Arg(qd, (RET_H, CH, 1), full), Arg(kd, (RET_H, CH, 1), full),
          Arg(cd, (RET_H, 1, 1), full)]
    xs = [Arg(u, (CH, uw), lambda o, t: (t, 0))]
    return (1, lp // CH), ps, cs, xs


def ret_fwd(h, mask, w):
    lp = h.shape[0]
    a = norm_fwd("ret_norm", h, w["ng"], mask)
    u = lin("ret_in", a, w["win"])
    grid, ps, cs, xs = _ret_args(u, w["gn"], w["rc"])
    (z,), (st,) = seq_fwd("ret_core", _ret_fn, grid, ps, cs, xs,
                          [((lp, 2 * D), BF, (CH, RET_H * RET_DV), lambda o, t: (t, 0))],
                          carries=[(RET_H, RET_DK, RET_DV)], save_dtype=BF)
    h2 = lin("ret_out", z, w["wo"], res=h)
    return h2, (h, a, u, z, st)


def ret_bwd(dh2, mask, w, saved):
    h, a, u, z, st = saved
    g = {}
    g["wo"] = wgrad("ret_dwo", z, dh2)
    dz = lin_t("ret_dz", dh2, w["wo"])
    grid, ps, cs, xs = _ret_args(u, w["gn"], w["rc"])
    (du,), (g["gn"],) = seq_bwd("ret_core_b", _ret_fn, grid, ps, cs, xs,
                                [Arg(dz, (CH, RET_H * RET_DV), lambda o, t: (t, 0))], saved=[st],
                                carries=[(RET_H, RET_DK, RET_DV)])
    g["win"] = wgrad("ret_dwin", a, du)
    da = lin_t("ret_da", du, w["win"])
    dh, g["ng"] = norm_bwd("ret_norm_b", h, w["ng"], mask, da, dh2)
    return dh, g


def mla_consts(lp):
    ang = _rope_angles(lp, MLA_ROPE)
    cos = np.concatenate([np.cos(ang), np.cos(ang)], axis=1).astype(np.float32)
    sin = np.concatenate([np.sin(ang), np.sin(ang)], axis=1).astype(np.float32)
    hd = MLA_ROPE // 2
    i = np.arange(hd)
    rot = np.zeros((MLA_ROPE, MLA_ROPE), np.float32)
    rot[hd + i, i] = -1.0
    rot[i, hd + i] = 1.0
    return cos, sin, rot


def _mla_prep1_fn(p, c, k, x):
    gq, gkv = p
    (down,) = x
    return (), (_rms(down[:, :MLA_QL], gq), _rms(down[:, MLA_QL:MLA_QL + MLA_KVL], gkv), down[:, MLA_QL + MLA_KVL:])


def _mla_prep2(p, c, x):
    gq, gk = p
    cos, sin, rot = c
    q, kv, kpe = x
    qn = _rms(q, gq)
    qn_n, qn_r = qn[:, :MLA_NOPE], qn[:, MLA_NOPE:]
    qo = jnp.concatenate([qn_n, qn_r * cos + cright(qn_r, rot) * sin], axis=1)
    kn = kv[:, :MLA_NOPE]
    ms = (jnp.sum(kn * kn, axis=-1, keepdims=True) + jnp.sum(kpe * kpe, axis=-1, keepdims=True)) / MLA_QK
    r = lax.rsqrt(ms + EPS)
    kr = kpe * r * gk[:, MLA_NOPE:]
    ko = jnp.concatenate([kn * r * gk[:, :MLA_NOPE], kr * cos + cright(kr, rot) * sin], axis=1)
    return qo, ko, kv[:, MLA_NOPE:]


def _mla_prep2_fn(p, c, k, x):
    return (), _mla_prep2(p, c, x)[:2]


def _mla_prep2_b_fn(p, c, k, x):
    return (), _mla_prep2(p, c, x)


def _prep1_args(down, gq, gkv):
    lp = down.shape[0]
    tr = _row_tile(lp)
    ps = [Arg(gq, (1, MLA_QL), lambda o, t: (0, 0), shared=True), Arg(gkv, (1, MLA_KVL), lambda o, t: (0, 0), shared=True)]
    return (1, lp // tr), ps, [Arg(down, (tr, down.shape[1]), lambda o, t: (t, 0))], tr


def _prep2_args(qraw, kvraw, kpe, gq, gk, mc):
    lp = kpe.shape[0]
    tr = _row_tile(lp)
    cos, sin, rot = mc
    ps = [Arg(gq, (1, MLA_QK), lambda o, t: (0, 0), shared=True), Arg(gk, (1, MLA_QK), lambda o, t: (0, 0), shared=True)]
    cs = [Arg(cos, (tr, MLA_ROPE), lambda o, t: (o, 0)), Arg(sin, (tr, MLA_ROPE), lambda o, t: (o, 0)),
          Arg(rot, (MLA_ROPE, MLA_ROPE), lambda o, t: (0, 0))]
    xs = [Arg(qraw, (None, tr, MLA_QK), lambda o, t: (t, o, 0)), Arg(kvraw, (None, tr, MLA_NOPE + MLA_V), lambda o, t: (t, o, 0)),
          Arg(kpe, (tr, MLA_ROPE), lambda o, t: (o, 0), acc=True)]
    return (lp // tr, MLA_H), ps, cs, xs, tr


ATT_HB = 2


def _attn_tile(lp):
    return 832 if (lp % 832 == 0 and lp > 832) else 64


def _attn_mask(qi, ki, ta):
    rows = qi * ta + lax.broadcasted_iota(jnp.int32, (ta, ta), 0)
    cols = ki * ta + lax.broadcasted_iota(jnp.int32, (ta, ta), 1)
    return (cols >= PAD) & ((cols // CH) <= (rows // CH))


def attn_fwd(q, k, kv):
    nh, lp, dq = q.shape
    ta = _attn_tile(lp)
    nb = lp // ta
    scale = MLA_QK ** -0.5

    hb = ATT_HB

    def body(q_ref, k_ref, v_ref, o_ref, lse_ref, m_s, l_s, acc_s):
        qi, ki = pl.program_id(1), pl.program_id(2)

        @pl.when(ki == 0)
        def _():
            m_s[...] = jnp.full_like(m_s, NEG)
            l_s[...] = jnp.zeros_like(l_s)
            acc_s[...] = jnp.zeros_like(acc_s)

        def step(masked):
            ss = [_bdot(q_ref[j], k_ref[j], 1, 1) * scale for j in range(hb)]
            ps = []
            for j in range(hb):
                s = jnp.where(_attn_mask(qi, ki, ta), ss[j], NEG) if masked else ss[j]
                m_new = jnp.maximum(m_s[j], jnp.max(s, axis=-1, keepdims=True))
                p = jnp.exp(s - m_new)
                alpha = jnp.exp(m_s[j] - m_new)
                l_s[j] = alpha * l_s[j] + jnp.sum(p, axis=-1, keepdims=True)
                m_s[j] = m_new
                ps.append((p, alpha))
            for j in range(hb):
                acc_s[j] = ps[j][1] * acc_s[j] + _bdot(ps[j][0], v_ref[j], 1, 0)

        pl.when((ki == qi) | (ki == 0))(functools.partial(step, True))
        pl.when((ki < qi) & (ki > 0))(functools.partial(step, False))

        @pl.when(ki == nb - 1)
        def _():
            for j in range(hb):
                o_ref[:, MLA_V * j:MLA_V * (j + 1)] = (acc_s[j] / l_s[j]).astype(o_ref.dtype)
                lse_ref[j] = m_s[j] + jnp.log(l_s[j])

    return pl.pallas_call(
        body, name="mla_attn", grid=(nh // hb, nb, nb),
        in_specs=[pl.BlockSpec((hb, ta, dq), lambda h, qi, ki: (h, qi, 0)),
                  pl.BlockSpec((hb, ta, dq), lambda h, qi, ki: (h, jnp.minimum(ki, qi), 0)),
                  pl.BlockSpec((hb, ta, MLA_V), lambda h, qi, ki: (h, jnp.minimum(ki, qi), 1))],
        out_specs=[pl.BlockSpec((ta, hb * MLA_V), lambda h, qi, ki: (qi, h)),
                   pl.BlockSpec((hb, ta, 1), lambda h, qi, ki: (h, qi, 0))],
        out_shape=[SDS((lp, nh * MLA_V), BF), SDS((nh, lp, 1), F32)],
        scratch_shapes=[pltpu.VMEM((hb, ta, 1), F32), pltpu.VMEM((hb, ta, 1), F32), pltpu.VMEM((hb, ta, MLA_V), F32)],
        compiler_params=_cp(3))(q, k, kv)


def attn_bwd(q, k, kv, o, do, lse):
    nh, lp, dq = q.shape
    ta = _attn_tile(lp)
    nb = lp // ta
    scale = MLA_QK ** -0.5

    def body(q_ref, k_ref, v_ref, o_ref, do_ref, lse_ref, dq_ref, dk_ref, dv_ref, dk_s, dv_s):
        ki, qi = pl.program_id(1), pl.program_id(2)

        @pl.when((ki == 0) & (qi == 0))
        def _():
            dq_ref[...] = jnp.zeros_like(dq_ref)

        @pl.when(qi == 0)
        def _():
            dk_s[...] = jnp.zeros_like(dk_s)
            dv_s[...] = jnp.zeros_like(dv_s)

        def step(masked):
            dov = do_ref[...]
            s = _bdot(q_ref[...], k_ref[...], 1, 1) * scale
            dp = _bdot(dov, v_ref[...], 1, 1)
            if masked:
                s = jnp.where(_attn_mask(qi, ki, ta), s, NEG)
            p = jnp.exp(s - lse_ref[...])
            delta = jnp.sum(dov * o_ref[...].astype(F32), axis=-1, keepdims=True)
            dv_s[...] += _bdot(p, dov, 0, 0)
            ds = p * (dp - delta) * scale
            rows = pl.ds(pl.multiple_of(qi * ta, ta), ta)
            dq_ref[rows, :] += _bdot(ds, k_ref[...], 1, 0)
            dk_s[...] += _bdot(ds, q_ref[...], 0, 0)

        pl.when((ki == qi) | (ki == 0))(functools.partial(step, True))
        pl.when((ki < qi) & (ki > 0))(functools.partial(step, False))

        @pl.when(qi == nb - 1)
        def _():
            dk_ref[...] = dk_s[...]
            dv_ref[...] = dv_s[...]

    qmap = lambda h, ki, qi: (h, jnp.maximum(qi, ki), 0)
    return pl.pallas_call(
        body, name="mla_attn_b", grid=(nh, nb, nb),
        in_specs=[pl.BlockSpec((None, ta, dq), qmap),
                  pl.BlockSpec((None, ta, dq), lambda h, ki, qi: (h, ki, 0)),
                  pl.BlockSpec((None, ta, MLA_V), lambda h, ki, qi: (h, ki, 1)),
                  pl.BlockSpec((ta, MLA_V), lambda h, ki, qi: (jnp.maximum(qi, ki), h)),
                  pl.BlockSpec((ta, MLA_V), lambda h, ki, qi: (jnp.maximum(qi, ki), h)),
                  pl.BlockSpec((None, ta, 1), qmap)],
        out_specs=[pl.BlockSpec((None, lp, dq), lambda h, ki, qi: (h, 0, 0)),
                   pl.BlockSpec((None, ta, dq), lambda h, ki, qi: (h, ki, 0)),
                   pl.BlockSpec((None, ta, MLA_V), lambda h, ki, qi: (h, ki, 0))],
        out_shape=[SDS((nh, lp, dq), F32), SDS((nh, lp, dq), F32), SDS((nh, lp, MLA_V), F32)],
        scratch_shapes=[pltpu.VMEM((ta, dq), F32), pltpu.VMEM((ta, MLA_V), F32)],
        compiler_params=_cp(3))(q, k, kv, o, do, lse)


def mla_fwd(h, mask, w):
    lp = h.shape[0]
    a = norm_fwd("mla_norm", h, w["ng"], mask)
    down = lin("mla_down", a, w["wdown"])
    grid, ps, xs, tr = _prep1_args(down, w["gcq"], w["gckv"])
    row = lambda o, t: (t, 0)
    (cq, ckv, kpe), _ = seq_fwd("mla_prep1", _mla_prep1_fn, grid, ps, [], xs,
                                [((lp, MLA_QL), BF, (tr, MLA_QL), row), ((lp, MLA_KVL), BF, (tr, MLA_KVL), row),
                                 ((lp, MLA_ROPE), F32, (tr, MLA_ROPE), row)])
    qraw = lin_bo("mla_uq", cq, w["wuq"])
    kvraw = lin_bo("mla_ukv", ckv, w["wukv"])
    grid, ps, cs, xs, tr = _prep2_args(qraw, kvraw, kpe, w["gq"], w["gk"], w["mc"])
    hm = lambda o, t: (t, o, 0)
    (q, k), _ = seq_fwd("mla_prep2", _mla_prep2_fn, grid, ps, cs, xs,
                        [((MLA_H, lp, MLA_QK), BF, (None, tr, MLA_QK), hm), ((MLA_H, lp, MLA_QK), BF, (None, tr, MLA_QK), hm)])
    o, lse = attn_fwd(q, k, kvraw)
    h2 = lin("mla_out", o, w["wo"], res=h)
    return h2, (h, a, down, cq, ckv, kpe, qraw, kvraw, q, k, o, lse)


def mla_bwd(dh2, mask, w, saved, emit):
    h, a, down, cq, ckv, kpe, qraw, kvraw, q, k, o, lse = saved
    lp = h.shape[0]
    g = {}
    g["wo"] = wgrad("mla_dwo", o, dh2)
    do = lin_t("mla_do", dh2, w["wo"])
    do = emit("wo", [g["wo"]], do)
    dq, dk, dv = attn_bwd(q, k, kvraw, o, do, lse)
    grid, ps, cs, xs, tr = _prep2_args(qraw, kvraw, kpe, w["gq"], w["gk"], w["mc"])
    hm = lambda o, t: (t, o, 0)
    (dqraw, dkvraw, dkpe), (g["gq"], g["gk"]) = seq_bwd(
        "mla_prep2_b", _mla_prep2_b_fn, grid, ps, cs, xs,
        [Arg(dq, (None, tr, MLA_QK), hm), Arg(dk, (None, tr, MLA_QK), hm), Arg(dv, (None, tr, MLA_V), hm)])
    g["wuq"] = wgrad_bo("mla_dwuq", cq, dqraw)
    dcq = lin_t_bi("mla_dcq", dqraw, w["wuq"])
    g["wukv"] = wgrad_bo("mla_dwukv", ckv, dkvraw)
    dckv = lin_t_bi("mla_dckv", dkvraw, w["wukv"])
    dckv = emit("wu", [g["wuq"], g["wukv"]], dckv)
    grid, ps, xs, tr = _prep1_args(down, w["gcq"], w["gckv"])
    row = lambda o, t: (t, 0)
    (ddown,), (g["gcq"], g["gckv"]) = seq_bwd(
        "mla_prep1_b", _mla_prep1_fn, grid, ps, [], xs,
        [Arg(dcq, (tr, MLA_QL), row), Arg(dckv, (tr, MLA_KVL), row), Arg(dkpe, (tr, MLA_ROPE), row)])
    g["wdown"] = wgrad("mla_dwdown", a, ddown)
    da = lin_t("mla_da", ddown, w["wdown"])
    dh, g["ng"] = norm_bwd("mla_norm_b", h, w["ng"], mask, da, dh2)
    return dh, g


def loss_head(h, target):
    lp, d = h.shape
    assert OFF == CH

    def body(h_ref, t_ref, loss_ref, dh_ref):
        i = pl.program_id(0)

        @pl.when(i == 0)
        def _():
            loss_ref[...] = jnp.zeros_like(loss_ref)

        e = jnp.where(i > 0, h_ref[...] - t_ref[...], 0.0)
        loss_ref[...] += jnp.sum(e * e) * (0.5 / d)
        dh_ref[...] = e * (1.0 / d)

    return pl.pallas_call(
        body, name="loss_head", grid=(lp // CH,),
        in_specs=[pl.BlockSpec((CH, d), lambda i: (i, 0)), pl.BlockSpec((CH, d), lambda i: (jnp.maximum(i - 1, 0), 0))],
        out_specs=[pl.BlockSpec((8, 128), lambda i: (0, 0)), pl.BlockSpec((CH, d), lambda i: (i, 0))],
        out_shape=[SDS((8, 128), F32), SDS((lp, d), F32)], compiler_params=_cp(1))(h, target)


ADAM_LAND_BYTES = 20 * 1024 * 1024


def _adam_tile(r, c, nl):
    if r % 8:
        return r
    best = 8
    for t in range(8, r + 1, 8):
        if r % t == 0 and N_DEV * t * c * 4 * 2 * nl <= ADAM_LAND_BYTES:
            best = t
    return best


def adamw(name, lands, w, m, v):
    nl, r, c = w.shape
    tr = _adam_tile(r, c, nl)
    c1 = 1.0 / (1.0 - ADAM_B1 ** ADAM_STEP)
    c2 = 1.0 / (1.0 - ADAM_B2 ** ADAM_STEP)

    def body(*refs):
        l_refs = refs[:nl]
        w_ref, m_ref, v_ref, g_ref, d_ref, nm_ref, nv_ref = refs[nl:]
        layer = pl.program_id(0)
        for j in range(nl):
            @pl.when(layer == j)
            def _(j=j):
                g = l_refs[j][0]
                for i in range(1, N_DEV):
                    g = g + l_refs[j][i]
                g_ref[...] = g

        g = g_ref[...]
        nm = ADAM_B1 * m_ref[...] + (1.0 - ADAM_B1) * g
        nv = ADAM_B2 * v_ref[...] + (1.0 - ADAM_B2) * (g * g)
        nm_ref[...] = nm
        nv_ref[...] = nv
        d_ref[...] = -ADAM_LR * ((nm * c1) / (jnp.sqrt(nv * c2) + ADAM_EPS) + ADAM_WD * w_ref[...])

    blk = pl.BlockSpec((None, tr, c), lambda l, i: (l, i, 0))
    land_specs = [pl.BlockSpec((N_DEV, tr, c), lambda l, i, j=j: (0, jnp.where(l == j, i, 0), 0)) for j in range(nl)]
    return pl.pallas_call(
        body, name=name, grid=(nl, r // tr), in_specs=land_specs + [blk, blk, blk],
        out_specs=[blk, blk, blk, blk], out_shape=[SDS((nl, r, c), F32)] * 4, compiler_params=_cp(2))(*lands, w, m, v)


ANY = pl.BlockSpec(memory_space=pl.ANY)
MESH = pl.DeviceIdType.MESH


def _me():
    return lax.axis_index("x"), lax.axis_index("y"), lax.axis_index("c")


def _peers():
    x, y, c = _me()
    out = []
    for k in range(1, N_DEV):
        px = 1 - x if k & 4 else x
        py = 1 - y if k & 2 else y
        pc = 1 - c if k & 1 else c
        out.append(((px, py, pc), 4 * px + 2 * py + pc))
    return out


HBM_SPEC = pl.BlockSpec(memory_space=pltpu.HBM)
SEM_SPEC = pl.BlockSpec(memory_space=pltpu.SEMAPHORE)
DATAFLOW = pltpu.SideEffectType.DATAFLOW_SIDE_EFFECTING


def _my_index():
    return 4 * lax.axis_index("x") + 2 * lax.axis_index("y") + lax.axis_index("c")


def _hbm(a):
    return pltpu.with_memory_space_constraint(a, pltpu.HBM)


NP = N_DEV - 1


def _push_copy(x_ref, land_ref, send, recv, pid, src_idx, dst_idx, scatter):
    src = x_ref.at[src_idx] if scatter else x_ref
    return pltpu.make_async_remote_copy(src_ref=src, dst_ref=land_ref.at[dst_idx], send_sem=send, recv_sem=recv,
                                        device_id=pid, device_id_type=MESH)


def push_start(name, xs, me, scatter, carry=None):
    n = len(xs)
    lands = []
    for a in xs:
        own = lax.dynamic_index_in_dim(a, me, 0, keepdims=True) if scatter else a[None]
        z = lax.empty((N_DEV,) + own.shape[1:], a.dtype)
        lands.append(lax.dynamic_update_slice(z, own, (me,) + (0,) * (own.ndim - 1)))
    ns = 2 * NP * n
    ops = xs + lands + ([carry] if carry is not None else [])
    na = len(ops)

    def body(*refs):
        x_refs, land_refs = refs[:n], refs[n:2 * n]
        sems = refs[na:na + ns]
        token = refs[-1]
        x, y, c = _me()
        mine = 4 * x + 2 * y + c
        for i in range(n):
            for k, (pid, pidx) in enumerate(_peers()):
                s = 2 * (NP * i + k)
                _push_copy(x_refs[i], land_refs[i], sems[s], sems[s + 1], pid, pidx, mine, scatter).start()
        token[...] = jnp.zeros_like(token)

    out_shape = ([pltpu.SemaphoreType.DMA(())] * ns + [pltpu.HBM(a.shape, a.dtype) for a in ops]
                 + [SDS((8, 128), F32)])
    res = pl.pallas_call(
        body, name=name, out_shape=out_shape, in_specs=[HBM_SPEC] * na,
        out_specs=[SEM_SPEC] * ns + [HBM_SPEC] * na + [pl.BlockSpec(memory_space=pltpu.VMEM)],
        input_output_aliases={i: ns + i for i in range(na)},
        compiler_params=pltpu.CompilerParams(has_side_effects=DATAFLOW))(*[_hbm(a) for a in ops])
    sems, thru, token = res[:ns], res[ns:-1], res[-1]
    handles = [dict(x=thru[i], land=thru[n + i], sems=list(sems[2 * NP * i:2 * NP * (i + 1)]), token=token)
               for i in range(n)]
    return (handles, thru[2 * n]) if carry is not None else handles


def push_wait(name, hds, after, scatter):
    n = len(hds)
    ns = 2 * NP

    def body(*refs):
        x_refs, land_refs = refs[:n], refs[n:2 * n]
        sems = refs[2 * n:2 * n + ns * n]
        for i in range(n):
            for k, (pid, pidx) in enumerate(_peers()):
                cp = _push_copy(x_refs[i], land_refs[i], sems[ns * i + 2 * k], sems[ns * i + 2 * k + 1], pid, pidx, pidx,
                                scatter)
                cp.wait_send()
                cp.wait_recv()

    arrs = [hd["x"] for hd in hds] + [hd["land"] for hd in hds]
    sems = [s for hd in hds for s in hd["sems"]]
    res = pl.pallas_call(
        body, name=name, out_shape=[pltpu.HBM(a.shape, a.dtype) for a in arrs],
        in_specs=[HBM_SPEC] * (2 * n) + [SEM_SPEC] * (ns * n) + [ANY], out_specs=[HBM_SPEC] * (2 * n),
        input_output_aliases={i: i for i in range(2 * n)},
        compiler_params=pltpu.CompilerParams(has_side_effects=DATAFLOW))(*arrs, *sems, after)
    return list(res[n:])


WEIGHTS = ['meta_tokens', 'norm_mix_g', 'norm_ffn_g', 'mla_w_down', 'mla_cq_norm_g', 'mla_ckv_norm_g', 'mla_w_uq',
           'mla_w_ukv', 'mla_q_head_g', 'mla_k_head_g', 'mla_w_o', 'hgrn_w_in', 'hgrn_lb_logits', 'hgrn_o_norm_g',
           'hgrn_w_o', 's5_lam_re', 's5_lam_im', 's5_log_dt', 's5_b_re', 's5_b_im', 's5_c_re', 's5_c_im', 's5_d',
           's5_w_glu', 'ret_w_in', 'ret_gn_g', 'ret_w_o', 'ffn_w_up', 'ffn_conv_w', 'ffn_conv_b', 'ffn_w_down']
BIG = ['mla_w_down', 'mla_w_uq', 'mla_w_ukv', 'mla_w_o', 'hgrn_w_in', 'hgrn_w_o', 's5_w_glu', 'ret_w_in', 'ret_w_o',
       'ffn_w_up', 'ffn_w_down']
SMALL_SH = ['meta_tokens', 's5_d', 'ret_gn_g', 'ffn_conv_w']
REP_S5 = ['s5_lam_re', 's5_lam_im', 's5_log_dt', 's5_b_re', 's5_b_im', 's5_c_re', 's5_c_im']
REP_REST = ['norm_mix_g', 'norm_ffn_g', 'mla_cq_norm_g', 'mla_ckv_norm_g', 'mla_q_head_g', 'mla_k_head_g',
            'hgrn_lb_logits', 'hgrn_o_norm_g', 'ffn_conv_b']
SMALL_REP = REP_REST + REP_S5
LANE = 128


def _flat(arrs, mult):
    v = jnp.concatenate([a.reshape(-1) for a in arrs])
    pad = (-v.shape[0]) % mult
    return jnp.pad(v, (0, pad)).reshape(-1, LANE)


def _unflat(flat2d, like):
    v = flat2d.reshape(-1)
    out, o = [], 0
    for a in like:
        out.append(v[o:o + a.size].reshape(a.shape))
        o += a.size
    return out


def _lb_of(logits):
    cum = jnp.cumsum(jax.nn.softmax(logits, axis=0), axis=0)
    return (cum - cum[0:1])[1:2]


def _cols_to_blocks(g):
    k, n = g.shape
    return g.reshape(k, N_DEV, n // N_DEV).transpose(1, 0, 2)


def _blocks_to_cols(wb):
    nb, k, n = wb.shape
    return wb.transpose(1, 0, 2).reshape(k, nb * n)


SUBS = ['mla', 'ffn0', 'hgrn', 'ffn1', 's5', 'ffn2', 'ret', 'ffn3']
GROUPS = [[('mla_w_down', 0), ('mla_w_uq', 0), ('mla_w_ukv', 0), ('mla_w_o', 0)],
          [('ffn_w_up', 0), ('ffn_w_down', 0)],
          [('hgrn_w_in', 0), ('hgrn_w_o', 0)],
          [('ffn_w_up', 1), ('ffn_w_down', 1)],
          [('s5_w_glu', 0)],
          [('ffn_w_up', 2), ('ffn_w_down', 2)],
          [('ret_w_in', 0), ('ret_w_o', 0)],
          [('ffn_w_up', 3), ('ffn_w_down', 3)]]


def _pack8(parts, mult):
    v = jnp.concatenate(parts, axis=1)
    return jnp.pad(v, ((0, 0), (0, (-v.shape[1]) % mult))).reshape(N_DEV, -1, LANE)


def _sub_weights(k, got, rep, tabs, lp):
    ngm, ngf = rep['norm_mix_g'], rep['norm_ffn_g']
    if k == 0:
        return dict(ng=ngm[0:1], wdown=got[0].reshape(D, -1), gcq=rep['mla_cq_norm_g'], gckv=rep['mla_ckv_norm_g'],
                    wuq=got[1], wukv=got[2], gq=rep['mla_q_head_g'], gk=rep['mla_k_head_g'], wo=got[3].reshape(D, D),
                    mc=mla_consts(lp))
    if k == 2:
        return dict(ng=ngm[1:2], win=got[0], lb=tabs['lb'], go=rep['hgrn_o_norm_g'], wo=got[1].reshape(D, D))
    if k == 4:
        return dict(ng=ngm[2:3], tb=tabs['tb'], dsk=tabs['s5_d'], wglu=_blocks_to_cols(got[0]))
    if k == 6:
        return dict(ng=ngm[3:4], win=_blocks_to_cols(got[0]), gn=tabs['ret_gn_g'], wo=got[1].reshape(2 * D, D),
                    rc=ret_consts(lp))
    i = k // 2
    return dict(ng=ngf[i:i + 1], up=got[0], cw=tabs['conv_w'][:, i].reshape(2, 4, 3, 1, FFN_B),
                cb=rep['ffn_conv_b'][i].reshape(2, 4, 1, FFN_B), down=got[1].reshape(4, FFN_B, D))


def _sub_grad_blocks(k, g):
    if k == 0:
        parts = [g['wdown'], g['wuq'], g['wukv'], g['wo']]
    elif k == 2:
        parts = [g['win'], g['wo']]
    elif k == 4:
        parts = [_cols_to_blocks(g['wglu'])]
    elif k == 6:
        parts = [_cols_to_blocks(g['win']), g['wo']]
    else:
        parts = [g['up'], g['down']]
    return parts


_FWD = [mla_fwd, None, hgrn_fwd, None, s5_fwd, None, ret_fwd, None]
_BWD = [mla_bwd, None, hgrn_bwd, None, s5_bwd, None, ret_bwd, None]


def _step(args):
    w = {n: args[n] for n in WEIGHTS}
    x2, tgt = args['x'][0], args['loss_target'][0]

    lp = x2.shape[0] + OFF
    me = _my_index()
    mask = _rowmask(lp)
    rep = {n: w[n] for n in SMALL_REP}

    xs, slots = [], []
    for gi, grp in enumerate(GROUPS):
        items = [w[n][l].astype(BF) for n, l in grp] + ([_flat([w[n] for n in SMALL_SH], LANE)] if gi == 0 else [])
        slots.append((len(xs), len(items)))
        xs += items
    gh = push_start("gather_start", xs, me, scatter=False)

    def fetch(gi, after):
        s, cnt = slots[gi]
        return push_wait("gather_wait_" + SUBS[gi], gh[s:s + cnt], after, scatter=False)

    got = fetch(0, x2)
    sm, o, smp = got[-1].reshape(N_DEV, -1), 0, {}
    for n in SMALL_SH:
        smp[n] = sm[:, o:o + w[n].size].reshape((N_DEV,) + w[n].shape)
        o += w[n].size
    meta = smp['meta_tokens'].transpose(1, 0, 2).reshape(N_META, D)
    lb, lb_vjp = jax.vjp(_lb_of, rep['hgrn_lb_logits'])
    s5p = [rep[n][0] for n in ('s5_lam_re', 's5_lam_im', 's5_log_dt', 's5_b_re', 's5_b_im', 's5_c_re', 's5_c_im')]
    tb, tb_vjp = jax.vjp(s5_tables, *s5p)
    tabs = dict(lb=lb, tb=tb, s5_d=smp['s5_d'].reshape(1, D), ret_gn_g=smp['ret_gn_g'].reshape(1, 2 * D),
                conv_w=smp['ffn_conv_w'])
    h = jnp.concatenate([jnp.zeros((PAD, D), F32), meta, x2], axis=0)
    ws, saved = [], []
    for k in range(8):
        if k > 0:
            got = fetch(k, h)
        ws.append(_sub_weights(k, got, rep, tabs, lp))
        if k % 2:
            h, sv = ffn_fwd(k // 2, h, mask, ws[k])
        else:
            h, sv = _FWD[k](h, mask, ws[k])
        saved.append(sv)
    loss, dh = loss_head(h, tgt)

    gs, sh = [None] * 8, [None] * 8
    early = {}

    def emit(tag, grads, carry):
        blocks = [t.reshape((N_DEV, -1) + t.shape[-1:]) if t.ndim == 2 else t for t in grads]
        early[tag], carry = push_start("scatter_start_mla_" + tag, blocks, me, scatter=True, carry=carry)
        return carry

    for k in reversed(range(1, 8)):
        if k % 2:
            dh, gs[k] = ffn_bwd(k // 2, dh, mask, ws[k], saved[k])
        else:
            dh, gs[k] = _BWD[k](dh, mask, ws[k], saved[k])
        blocks = [b.reshape((N_DEV,) + w[n].shape[1:]) for b, (n, _) in zip(_sub_grad_blocks(k, gs[k]), GROUPS[k])]
        sh[k], dh = push_start("scatter_start_" + SUBS[k], blocks, me, scatter=True, carry=dh)
        if k == 4:
            gs5 = _flat(list(tb_vjp(gs[4]['tb'])), 8 * LANE)
            rh_s5, dh = push_start("small_grads_start_s5", [gs5], me, scatter=False, carry=dh)
    dh, gs[0] = mla_bwd(dh, mask, ws[0], saved[0], emit)
    dmeta = dh[PAD:OFF].reshape(N_META, N_DEV, D // N_DEV).transpose(1, 0, 2)
    dcw = jnp.stack([gs[2 * i + 1]['cw'].reshape(N_DEV, 3, FFN_B) for i in range(4)], axis=1)
    last = push_start("scatter_start_mla", [gs[0]['wdown'].reshape(N_DEV, D // N_DEV, -1),
                                            _pack8([t.reshape(N_DEV, -1) for t in (dmeta, gs[4]['dsk'], gs[6]['gn'], dcw)], LANE)],
                      me, scatter=True)
    sh[0] = [last[0], early['wu'][0], early['wu'][1], early['wo'][0], last[1]]
    grad_x = dh[OFF:]

    g_rep = {
        'norm_mix_g': jnp.concatenate([gs[k]['ng'] for k in (0, 2, 4, 6)], axis=0),
        'norm_ffn_g': jnp.concatenate([gs[k]['ng'] for k in (1, 3, 5, 7)], axis=0),
        'mla_cq_norm_g': gs[0]['gcq'], 'mla_ckv_norm_g': gs[0]['gckv'], 'mla_q_head_g': gs[0]['gq'],
        'mla_k_head_g': gs[0]['gk'], 'hgrn_lb_logits': lb_vjp(gs[2]['lb'])[0], 'hgrn_o_norm_g': gs[2]['go'],
        'ffn_conv_b': jnp.stack([gs[k]['cb'].reshape(-1) for k in (1, 3, 5, 7)], axis=0),
    }
    loss_part = loss[0, 0:1]
    grep = _flat([g_rep[n] for n in REP_REST] + [loss_part], 8 * LANE)
    rh = push_start("small_grads_start", [grep], me, scatter=False)

    lands = {n: [None] * w[n].shape[0] for n in BIG}
    res = {}
    late = [n for n, _ in GROUPS[0]]
    for k in reversed(range(1, 8)):
        got = push_wait("scatter_wait_" + SUBS[k], sh[k], grep, scatter=True)
        for (n, l), t in zip(GROUPS[k], got):
            lands[n][l] = t
    for n in BIG:
        if n not in late:
            res[n] = adamw("adam_" + n, lands[n], w[n], args['m_' + n], args['v_' + n])
    after = res['ffn_w_up'][1]
    got = push_wait("scatter_wait_" + SUBS[0], sh[0], after, scatter=True)
    small_land = got[-1]
    (rep_land,) = push_wait("small_grads_wait", rh, after, scatter=False)
    (s5_land,) = push_wait("small_grads_wait_s5", rh_s5, after, scatter=False)
    for (n, _), t in zip(GROUPS[0], got):
        res[n] = adamw("adam_" + n, [t], w[n], args['m_' + n], args['v_' + n])

    def flat_adam(name, land, names, mult, extra=()):
        like = [w[n] for n in names]
        pad = [jnp.zeros_like(e) for e in extra]
        out = adamw(name, [land], _flat(like + pad, mult)[None], _flat([args['m_' + n] for n in names] + pad, mult)[None],
                    _flat([args['v_' + n] for n in names] + pad, mult)[None])
        for n, parts in zip(names, zip(*[_unflat(t, like) for t in out])):
            res[n] = list(parts)
        return out[0]

    flat_adam("adam_small_sharded", small_land, SMALL_SH, LANE)
    flat_adam("adam_s5_replicated", s5_land, REP_S5, 8 * LANE)
    gsum = flat_adam("adam_small_replicated", rep_land, REP_REST, 8 * LANE, extra=[loss_part])
    total = gsum.reshape(-1)[sum(w[n].size for n in REP_REST)]
    outs = [total, grad_x[None]]
    for k in range(4):
        outs += [res[n][k] for n in WEIGHTS]
    return tuple(outs)


def kernel(x, meta_tokens, norm_mix_g, norm_ffn_g, mla_w_down, mla_cq_norm_g, mla_ckv_norm_g, mla_w_uq, mla_w_ukv, mla_q_head_g, mla_k_head_g, mla_w_o, hgrn_w_in, hgrn_lb_logits, hgrn_o_norm_g, hgrn_w_o, s5_lam_re, s5_lam_im, s5_log_dt, s5_b_re, s5_b_im, s5_c_re, s5_c_im, s5_d, s5_w_glu, ret_w_in, ret_gn_g, ret_w_o, ffn_w_up, ffn_conv_w, ffn_conv_b, ffn_w_down, loss_target, m_meta_tokens, m_norm_mix_g, m_norm_ffn_g, m_mla_w_down, m_mla_cq_norm_g, m_mla_ckv_norm_g, m_mla_w_uq, m_mla_w_ukv, m_mla_q_head_g, m_mla_k_head_g, m_mla_w_o, m_hgrn_w_in, m_hgrn_lb_logits, m_hgrn_o_norm_g, m_hgrn_w_o, m_s5_lam_re, m_s5_lam_im, m_s5_log_dt, m_s5_b_re, m_s5_b_im, m_s5_c_re, m_s5_c_im, m_s5_d, m_s5_w_glu, m_ret_w_in, m_ret_gn_g, m_ret_w_o, m_ffn_w_up, m_ffn_conv_w, m_ffn_conv_b, m_ffn_w_down, v_meta_tokens, v_norm_mix_g, v_norm_ffn_g, v_mla_w_down, v_mla_cq_norm_g, v_mla_ckv_norm_g, v_mla_w_uq, v_mla_w_ukv, v_mla_q_head_g, v_mla_k_head_g, v_mla_w_o, v_hgrn_w_in, v_hgrn_lb_logits, v_hgrn_o_norm_g, v_hgrn_w_o, v_s5_lam_re, v_s5_lam_im, v_s5_log_dt, v_s5_b_re, v_s5_b_im, v_s5_c_re, v_s5_c_im, v_s5_d, v_s5_w_glu, v_ret_w_in, v_ret_gn_g, v_ret_w_o, v_ffn_w_up, v_ffn_conv_w, v_ffn_conv_b, v_ffn_w_down):
    return _step(dict(locals()))
```

```python
import functools
import math

import jax
import jax.numpy as jnp
import numpy as np
from jax import lax
from jax.experimental import pallas as pl
from jax.experimental.pallas import tpu as pltpu

F32 = jnp.float32
BF = jnp.bfloat16
SDS = jax.ShapeDtypeStruct

N_DEV = 8
D = 1024
N_META = 16
PAD = 48
OFF = PAD + N_META
CH = 64
EPS = 1e-6
NEG = -1e30
ROPE_BASE = 10000.0

MLA_H, MLA_NOPE, MLA_ROPE, MLA_V = 8, 128, 64, 128
MLA_QK = MLA_NOPE + MLA_ROPE
MLA_QL, MLA_KVL = 384, 256
HG_H, HG_D, HG_C = 8, 128, 16
S5_G, S5_P, S5_K = 64, 64, 16
S5_SG = 8
RET_H, RET_DK, RET_DV = 4, 256, 512
FFN_F = 2816
FFN_B = 704

ADAM_LR, ADAM_B1, ADAM_B2, ADAM_EPS, ADAM_WD, ADAM_STEP = 0.001, 0.9, 0.999, 1e-08, 0.01, 10

VMEM_LIMIT = 56 * 1024 * 1024
ARB = "arbitrary"


def _cp(n):
    return pltpu.CompilerParams(dimension_semantics=(ARB,) * n, vmem_limit_bytes=VMEM_LIMIT)


def _bdot(a, b, ca, cb):
    return lax.dot_general(a.astype(BF), b.astype(BF), (((ca,), (cb,)), ((), ())), preferred_element_type=F32)


@jax.custom_vjp
def mm(a, b):
    return _bdot(a, b, 1, 0)


@jax.custom_vjp
def mm_nt(a, b):
    return _bdot(a, b, 1, 1)


@jax.custom_vjp
def mm_tn(a, b):
    return _bdot(a, b, 0, 0)


mm.defvjp(lambda a, b: (mm(a, b), (a, b)),
          lambda r, g: (mm_nt(g, r[1]).astype(r[0].dtype), mm_tn(r[0], g).astype(r[1].dtype)))
mm_nt.defvjp(lambda a, b: (mm_nt(a, b), (a, b)),
             lambda r, g: (mm(g, r[1]).astype(r[0].dtype), mm_tn(g, r[0]).astype(r[1].dtype)))
mm_tn.defvjp(lambda a, b: (mm_tn(a, b), (a, b)),
             lambda r, g: (mm_nt(r[1], g).astype(r[0].dtype), mm(r[0], g).astype(r[1].dtype)))


def _xdot(a, b, ca, cb):
    return lax.dot_general(a, b, (((ca,), (cb,)), ((), ())), preferred_element_type=F32,
                           precision=lax.Precision.HIGHEST)


@jax.custom_vjp
def cright(x, r):
    return _xdot(x, r, 1, 0)


cright.defvjp(lambda x, r: (cright(x, r), r), lambda r, g: (_xdot(g, r, 1, 1), jnp.zeros_like(r)))


def _shift_raw(x, s):
    n = x.shape[0]
    r = lax.broadcasted_iota(jnp.int32, x.shape, 0)
    y = pltpu.roll(x, s % n, 0)
    return jnp.where((r >= s) & (r < n + s), y, 0.0)


@functools.partial(jax.custom_vjp, nondiff_argnums=(1,))
def shift_rows(x, s):
    return _shift_raw(x, s)


shift_rows.defvjp(lambda x, s: (_shift_raw(x, s), None), lambda s, _, g: (_shift_raw(g, -s),))


def _seg_shift_raw(x, s, seg, up):
    n = x.shape[0]
    r = lax.broadcasted_iota(jnp.int32, x.shape, 0) % seg
    if up:
        return jnp.where(r < seg - s, pltpu.roll(x, n - s, 0), 0.0)
    return jnp.where(r >= s, pltpu.roll(x, s, 0), 0.0)


@functools.partial(jax.custom_vjp, nondiff_argnums=(1, 2))
def seg_shift(x, s, seg):
    return _seg_shift_raw(x, s, seg, False)


seg_shift.defvjp(lambda x, s, seg: (_seg_shift_raw(x, s, seg, False), None),
                 lambda s, seg, _, g: (_seg_shift_raw(g, s, seg, True),))


def _seg_cumsum(x, seg):
    s = 1
    while s < seg:
        x = x + seg_shift(x, s, seg)
        s *= 2
    return x


def _rms(x, g):
    return x * lax.rsqrt(jnp.mean(x * x, axis=-1, keepdims=True) + EPS) * g


def _silu(x):
    return x * jax.nn.sigmoid(x)


def _mm_call(name, a, b, *, grid, a_spec, b_spec, o_shape, o_spec, dims, acc_shape, res=None, res_spec=None,
             mask_tm=None):
    nk = grid[2]

    def body(*refs):
        if res is None:
            a_ref, b_ref, o_ref = refs[:3]
        else:
            a_ref, b_ref, r_ref, o_ref = refs[:4]
        k = pl.program_id(2)

        def dot():
            return lax.dot_general(a_ref[...].astype(BF), b_ref[...].astype(BF), dims, preferred_element_type=F32)

        def finish(v):
            if res is not None:
                v = v + r_ref[...].astype(F32)
                rows = pl.program_id(0) * mask_tm + lax.broadcasted_iota(jnp.int32, v.shape, 0)
                v = jnp.where(rows >= PAD, v, 0.0)
            o_ref[...] = v.astype(o_ref.dtype)

        if nk == 1:
            finish(dot())
            return
        acc = refs[-1]

        @pl.when(k == 0)
        def _():
            acc[...] = dot()

        @pl.when((k > 0) & (k < nk - 1))
        def _():
            acc[...] += dot()

        @pl.when(k == nk - 1)
        def _():
            finish(acc[...] + dot())

    ins = [a, b] + ([res] if res is not None else [])
    specs = [a_spec, b_spec] + ([res_spec] if res is not None else [])
    scratch = [pltpu.VMEM(acc_shape, F32)] if nk > 1 else []
    return pl.pallas_call(body, name=name, grid=grid, in_specs=specs, out_specs=o_spec, out_shape=o_shape,
                          scratch_shapes=scratch, compiler_params=_cp(3))(*ins)


NN = (((1,), (0,)), ((), ()))
NT = (((1,), (1,)), ((), ()))
TN = (((0,), (0,)), ((), ()))


def _row_tile(lp):
    for t in (832, 640, 320, 64):
        if lp % t == 0:
            return t
    raise ValueError(lp)


def _col_tile(n):
    for t in (1024, 768, 512, 384, 256, 128):
        if n % t == 0:
            return t
    return n


def _mm_rows(m):
    return 2080 if m % 2080 == 0 else _row_tile(m)


def _mm_cols(n):
    for t in (512, 384, 256, 128):
        if n % t == 0:
            return t
    return n


def lin(name, a, w, out_dtype=F32, res=None):
    m, k = a.shape
    n = w.shape[1]
    tm, tn, tc = _mm_rows(m), _mm_cols(n), _col_tile(k)
    return _mm_call(name, a, w, grid=(m // tm, n // tn, k // tc),
                    a_spec=pl.BlockSpec((tm, tc), lambda i, j, kk: (i, kk)),
                    b_spec=pl.BlockSpec((tc, tn), lambda i, j, kk: (kk, j)),
                    o_shape=SDS((m, n), out_dtype), o_spec=pl.BlockSpec((tm, tn), lambda i, j, kk: (i, j)),
                    dims=NN, acc_shape=(tm, tn), res=res,
                    res_spec=pl.BlockSpec((tm, tn), lambda i, j, kk: (i, j)), mask_tm=tm)


def lin_bo(name, a, wb, out_dtype=F32):
    m, k = a.shape
    nb, _, n = wb.shape
    tm = _mm_rows(m)
    return _mm_call(name, a, wb, grid=(m // tm, nb, 1),
                    a_spec=pl.BlockSpec((tm, k), lambda i, j, kk: (i, 0)),
                    b_spec=pl.BlockSpec((None, k, n), lambda i, j, kk: (j, 0, 0)),
                    o_shape=SDS((nb, m, n), out_dtype), o_spec=pl.BlockSpec((None, tm, n), lambda i, j, kk: (j, i, 0)),
                    dims=NN, acc_shape=(tm, n))


def lin_bi(name, ab, wb, out_dtype=F32, res=None):
    nb, m, k = ab.shape
    n = wb.shape[2]
    tm, tn = _mm_rows(m), _mm_cols(n)
    return _mm_call(name, ab, wb, grid=(m // tm, n // tn, nb),
                    a_spec=pl.BlockSpec((None, tm, k), lambda i, j, kk: (kk, i, 0)),
                    b_spec=pl.BlockSpec((None, k, tn), lambda i, j, kk: (kk, 0, j)),
                    o_shape=SDS((m, n), out_dtype), o_spec=pl.BlockSpec((tm, tn), lambda i, j, kk: (i, j)),
                    dims=NN, acc_shape=(tm, tn), res=res,
                    res_spec=pl.BlockSpec((tm, tn), lambda i, j, kk: (i, j)), mask_tm=tm)


def lin_t(name, g, w, out_dtype=F32):
    m, n = g.shape
    k = w.shape[0]
    tm, tk, tc = _mm_rows(m), _mm_cols(k), _col_tile(n)
    return _mm_call(name, g, w, grid=(m // tm, k // tk, n // tc),
                    a_spec=pl.BlockSpec((tm, tc), lambda i, j, kk: (i, kk)),
                    b_spec=pl.BlockSpec((tk, tc), lambda i, j, kk: (j, kk)),
                    o_shape=SDS((m, k), out_dtype), o_spec=pl.BlockSpec((tm, tk), lambda i, j, kk: (i, j)),
                    dims=NT, acc_shape=(tm, tk))


def lin_t_bi(name, gb, wb, out_dtype=F32):
    nb, m, n = gb.shape
    k = wb.shape[1]
    tm, tk = _mm_rows(m), _mm_cols(k)
    return _mm_call(name, gb, wb, grid=(m // tm, k // tk, nb),
                    a_spec=pl.BlockSpec((None, tm, n), lambda i, j, kk: (kk, i, 0)),
                    b_spec=pl.BlockSpec((None, tk, n), lambda i, j, kk: (kk, j, 0)),
                    o_shape=SDS((m, k), out_dtype), o_spec=pl.BlockSpec((tm, tk), lambda i, j, kk: (i, j)),
                    dims=NT, acc_shape=(tm, tk))


def lin_t_bo(name, g, wb, out_dtype=F32):
    m, n = g.shape
    nb, k, _ = wb.shape
    tm = _mm_rows(m)
    return _mm_call(name, g, wb, grid=(m // tm, nb, 1),
                    a_spec=pl.BlockSpec((tm, n), lambda i, j, kk: (i, 0)),
                    b_spec=pl.BlockSpec((None, k, n), lambda i, j, kk: (j, 0, 0)),
                    o_shape=SDS((nb, m, k), out_dtype), o_spec=pl.BlockSpec((None, tm, k), lambda i, j, kk: (j, i, 0)),
                    dims=NT, acc_shape=(tm, k))


def wgrad(name, a, g):
    m, k = a.shape
    n = g.shape[1]
    tm, tn = _mm_rows(m), _mm_cols(n)
    return _mm_call(name, a, g, grid=(1, n // tn, m // tm),
                    a_spec=pl.BlockSpec((tm, k), lambda i, j, kk: (kk, 0)),
                    b_spec=pl.BlockSpec((tm, tn), lambda i, j, kk: (kk, j)),
                    o_shape=SDS((k, n), F32), o_spec=pl.BlockSpec((k, tn), lambda i, j, kk: (0, j)),
                    dims=TN, acc_shape=(k, tn))


def wgrad_bo(name, a, gb):
    m, k = a.shape
    nb, _, n = gb.shape
    tm = _mm_rows(m)
    return _mm_call(name, a, gb, grid=(nb, 1, m // tm),
                    a_spec=pl.BlockSpec((tm, k), lambda i, j, kk: (kk, 0)),
                    b_spec=pl.BlockSpec((None, tm, n), lambda i, j, kk: (i, kk, 0)),
                    o_shape=SDS((nb, k, n), F32), o_spec=pl.BlockSpec((None, k, n), lambda i, j, kk: (i, 0, 0)),
                    dims=TN, acc_shape=(k, n))


def wgrad_bi(name, zb, g):
    nb, m, k = zb.shape
    n = g.shape[1]
    tm, tn = _mm_rows(m), _mm_cols(n)
    return _mm_call(name, zb, g, grid=(nb, n // tn, m // tm),
                    a_spec=pl.BlockSpec((None, tm, k), lambda i, j, kk: (i, kk, 0)),
                    b_spec=pl.BlockSpec((tm, tn), lambda i, j, kk: (kk, j)),
                    o_shape=SDS((nb, k, n), F32), o_spec=pl.BlockSpec((None, k, tn), lambda i, j, kk: (i, 0, j)),
                    dims=TN, acc_shape=(k, tn))


class Arg:
    def __init__(self, arr, block, imap, shared=False, acc=False):
        self.arr, self.block, self.imap = arr, block, imap
        self.shared = shared
        self.acc = acc

    @property
    def spec(self):
        return pl.BlockSpec(self.block, self.imap)

    def vshape(self):
        return tuple(b for b in self.block if b is not None)


def _rev(arg, nt):
    return pl.BlockSpec(arg.block, lambda o, t, _f=arg.imap: _f(o, nt - 1 - t))


def seq_fwd(name, fn, grid, params, consts, xs, outs, carries=(), save_dtype=F32):
    no, nt = grid
    n_p, n_c, n_x, n_y, n_k = len(params), len(consts), len(xs), len(outs), len(carries)

    def body(*refs):
        p_refs = refs[:n_p]
        c_refs = refs[n_p:n_p + n_c]
        x_refs = refs[n_p + n_c:n_p + n_c + n_x]
        r = n_p + n_c + n_x
        y_refs = refs[r:r + n_y]
        s_refs = refs[r + n_y:r + n_y + n_k]
        k_refs = refs[r + n_y + n_k:]
        t = pl.program_id(1)

        if n_k:
            @pl.when(t == 0)
            def _():
                for k in k_refs:
                    k[...] = jnp.zeros_like(k)

        carry = tuple(k[...] for k in k_refs)
        for s, c in zip(s_refs, carry):
            s[...] = c.astype(s.dtype)
        new_carry, ys = fn(tuple(p[...] for p in p_refs), tuple(c[...] for c in c_refs), carry,
                           tuple(x[...] for x in x_refs))
        for k, c in zip(k_refs, new_carry):
            k[...] = c
        for y_ref, y in zip(y_refs, ys):
            y_ref[...] = y.astype(y_ref.dtype)

    out_shape = [SDS(s, d) for (s, d, _, _) in outs]
    out_specs = [pl.BlockSpec(b, im) for (_, _, b, im) in outs]
    for cs in carries:
        out_shape.append(SDS((no, nt) + cs, save_dtype))
        out_specs.append(pl.BlockSpec((None, None) + cs, lambda o, t, _n=len(cs): (o, t) + (0,) * _n))
    res = pl.pallas_call(
        body, name=name, grid=grid, in_specs=[a.spec for a in list(params) + list(consts) + list(xs)],
        out_specs=out_specs, out_shape=out_shape, scratch_shapes=[pltpu.VMEM(cs, F32) for cs in carries],
        compiler_params=_cp(2))(*[a.arr for a in list(params) + list(consts) + list(xs)])
    return res[:n_y], res[n_y:]


def seq_bwd(name, fn, grid, params, consts, xs, dys, saved=(), carries=()):
    no, nt = grid
    n_p, n_c, n_x, n_y, n_k = len(params), len(consts), len(xs), len(dys), len(carries)

    def body(*refs):
        p_refs = refs[:n_p]
        c_refs = refs[n_p:n_p + n_c]
        x_refs = refs[n_p + n_c:n_p + n_c + n_x]
        r = n_p + n_c + n_x
        g_refs = refs[r:r + n_y]
        s_refs = refs[r + n_y:r + n_y + n_k]
        r = r + n_y + n_k
        dx_refs = refs[r:r + n_x]
        dp_refs = refs[r + n_x:r + n_x + n_p]
        k_refs = refs[r + n_x + n_p:]
        o = pl.program_id(0)
        t = pl.program_id(1)

        if n_k:
            @pl.when(t == 0)
            def _():
                for k in k_refs:
                    k[...] = jnp.zeros_like(k)

        for a, dp in zip(params, dp_refs):
            @pl.when((t == 0) & (o == 0) if a.shared else (t == 0))
            def _(dp=dp):
                dp[...] = jnp.zeros_like(dp)

        for a, dx in zip(xs, dx_refs):
            if a.acc:
                @pl.when(t == 0)
                def _(dx=dx):
                    dx[...] = jnp.zeros_like(dx)

        consts_v = tuple(c[...] for c in c_refs)

        def f(pv, cv, xv):
            return fn(pv, consts_v, cv, xv)

        pv = tuple(p[...] for p in p_refs)
        cv = tuple(s[...].astype(F32) for s in s_refs)
        xv = tuple(x[...] for x in x_refs)
        (new_carry, ys), vjp = jax.vjp(f, pv, cv, xv)
        cot = (tuple(k[...] for k in k_refs), tuple(g[...].astype(y.dtype) for g, y in zip(g_refs, ys)))
        dpv, dcv, dxv = vjp(cot)
        for k, c in zip(k_refs, dcv):
            k[...] = c
        for dp, v in zip(dp_refs, dpv):
            dp[...] += v
        for a, dx, v in zip(xs, dx_refs, dxv):
            if a.acc:
                dx[...] += v
            else:
                dx[...] = v.astype(dx.dtype)

    in_specs = ([_rev(a, nt) for a in list(params) + list(consts) + list(xs) + list(dys)]
                + [pl.BlockSpec((None, None) + cs, lambda o, t, _n=len(cs): (o, nt - 1 - t) + (0,) * _n) for cs in carries])
    out_shape = [SDS(a.arr.shape, F32) for a in xs] + [SDS(a.arr.shape, F32) for a in params]
    out_specs = [_rev(a, nt) for a in list(xs) + list(params)]
    res = pl.pallas_call(
        body, name=name, grid=grid, in_specs=in_specs, out_specs=out_specs, out_shape=out_shape,
        scratch_shapes=[pltpu.VMEM(cs, F32) for cs in carries], compiler_params=_cp(2))(
            *[a.arr for a in list(params) + list(consts) + list(xs) + list(dys)], *saved)
    return res[:n_x], res[n_x:]


def _rowmask(lp):
    return (jnp.arange(lp) >= PAD).astype(F32)[:, None]


def _norm_fn(p, c, k, x):
    return (), (_rms(x[0] * c[0], p[0]),)


def _norm_b_fn(p, c, k, x):
    h = x[0] * c[0]
    return (), (_rms(h, p[0]), h)


def norm_fwd(name, h, g, mask, out_dtype=BF):
    lp, d = h.shape
    tr = _row_tile(lp)
    row = lambda o, t: (t, 0)
    (a,), _ = seq_fwd(name, _norm_fn, (1, lp // tr), [Arg(g, (1, d), lambda o, t: (0, 0), shared=True)],
                      [Arg(mask, (tr, 1), row)], [Arg(h, (tr, d), row)], [((lp, d), out_dtype, (tr, d), row)])
    return a


def norm_bwd(name, h, g, mask, da, dskip):
    lp, d = h.shape
    tr = _row_tile(lp)
    row = lambda o, t: (t, 0)
    (dh,), (dg,) = seq_bwd(name, _norm_b_fn, (1, lp // tr), [Arg(g, (1, d), lambda o, t: (0, 0), shared=True)],
                           [Arg(mask, (tr, 1), row)], [Arg(h, (tr, d), row)],
                           [Arg(da, (tr, d), row), Arg(dskip, (tr, d), row)])
    return dh, dg


def _ffn_tile(lp):
    return 320 if (lp % 320 == 0 and lp > 320) else 64


def _conv_rows(ext, w, b, n):
    u2 = ext[8:8 + n]
    u1 = pltpu.roll(ext, 1, 0)[8:8 + n]
    u0 = pltpu.roll(ext, 2, 0)[8:8 + n]
    return w[2] * u2 + w[1] * u1 + w[0] * u0 + b, (u0, u1, u2)


def ffn_up_core(name, a, wup, cw, cb):
    lp, kd = a.shape
    _, nj, _, fb = wup.shape
    tr = _ffn_tile(lp)
    nt = lp // tr

    def body(a_ref, wu_ref, w_ref, b_ref, u_ref, z_ref, u_s, halo_s):
        i = pl.program_id(1)

        @pl.when(i == 0)
        def _():
            u_s[...] = jnp.zeros_like(u_s)
            halo_s[...] = jnp.zeros_like(halo_s)

        old = (i + 1) % 2
        cs = []
        for s in range(2):
            tile = u_s[old, s]
            ext = jnp.concatenate([halo_s[s], tile], axis=0)
            c, _ = _conv_rows(ext, w_ref[s], b_ref[s], tr)
            cs.append(c)
            halo_s[s] = tile[tr - 8:]
        z_ref[...] = (_silu(cs[0]) * cs[1]).astype(z_ref.dtype)
        for s in range(2):
            un = lax.dot_general(a_ref[...], wu_ref[s], NN, preferred_element_type=F32)
            u_ref[s] = un
            u_s[i % 2, s] = un

    cur = lambda j, i: (0, j, jnp.minimum(i, nt - 1), 0)
    return pl.pallas_call(
        body, name=name, grid=(nj, nt + 1),
        in_specs=[pl.BlockSpec((tr, kd), lambda j, i: (jnp.minimum(i, nt - 1), 0)),
                  pl.BlockSpec((2, None, kd, fb), lambda j, i: (0, j, 0, 0)),
                  pl.BlockSpec((2, None, 3, 1, fb), lambda j, i: (0, j, 0, 0, 0)),
                  pl.BlockSpec((2, None, 1, fb), lambda j, i: (0, j, 0, 0))],
        out_specs=[pl.BlockSpec((2, None, tr, fb), cur),
                   pl.BlockSpec((None, tr, fb), lambda j, i: (j, jnp.maximum(i - 1, 0), 0))],
        out_shape=[SDS((2, nj, lp, fb), F32), SDS((nj, lp, fb), BF)],
        scratch_shapes=[pltpu.VMEM((2, 2, tr, fb), F32), pltpu.VMEM((2, 8, fb), F32)],
        compiler_params=_cp(2))(a, wup, cw, cb)


def ffn_core_bwd(name, u, dz, cw, cb, a):
    _, nj, lp, fb = u.shape
    kd = a.shape[1]
    tr = _ffn_tile(lp)
    nt = lp // tr
    nb8 = lp // 8

    def body(u_ref, up_ref, un_ref, dz_ref, dzn_ref, w_ref, b_ref, a_ref, du_ref, dw_ref, db_ref, dwup_ref, du_s):
        i = pl.program_id(1)

        @pl.when(i == 0)
        def _():
            dw_ref[...] = jnp.zeros_like(dw_ref)
            db_ref[...] = jnp.zeros_like(db_ref)
            dwup_ref[...] = jnp.zeros_like(dwup_ref)
            du_s[...] = jnp.zeros_like(du_s)

        old = (i + 1) % 2
        for s in range(2):
            dwup_ref[s] += lax.dot_general(a_ref[...], du_s[old, s], TN, preferred_element_type=F32)

        it = jnp.minimum(i, nt - 1)
        live = (i < nt).astype(F32)
        prev = jnp.where(it > 0, up_ref[...], 0.0)
        nxt = jnp.where(it < nt - 1, un_ref[...], 0.0)
        dz_e = jnp.concatenate([dz_ref[...], jnp.where(it < nt - 1, dzn_ref[...], 0.0)], axis=0)
        n = tr + 8
        cs, taps = [], []
        for s in range(2):
            ext = jnp.concatenate([prev[s], u_ref[s], nxt[s]], axis=0)
            c, tp = _conv_rows(ext, w_ref[s], b_ref[s], n)
            cs.append(c)
            taps.append(tp)
        sg = jax.nn.sigmoid(cs[0])
        dcs = [dz_e * cs[1] * sg * (1.0 + cs[0] * (1.0 - sg)), dz_e * cs[0] * sg]
        for s in range(2):
            dc = dcs[s]
            w = w_ref[s]
            d1 = pltpu.roll(dc, n - 1, 0)[:tr]
            d2 = pltpu.roll(dc, n - 2, 0)[:tr]
            dcm = dc[:tr]
            du = w[2] * dcm + w[1] * d1 + w[0] * d2
            du_ref[s] = du
            du_s[i % 2, s] = (du * live).astype(BF)
            for k in range(3):
                dw_ref[s, k] += live * jnp.sum(dcm * taps[s][k][:tr], axis=0, keepdims=True)
            db_ref[s] += live * jnp.sum(dcm, axis=0, keepdims=True)

    row = lambda j, i: (0, j, jnp.minimum(i, nt - 1), 0)
    return pl.pallas_call(
        body, name=name, grid=(nj, nt + 1),
        in_specs=[pl.BlockSpec((2, None, tr, fb), row),
                  pl.BlockSpec((2, None, 8, fb), lambda j, i: (0, j, jnp.maximum(jnp.minimum(i, nt - 1) * (tr // 8) - 1, 0), 0)),
                  pl.BlockSpec((2, None, 8, fb), lambda j, i: (0, j, jnp.minimum((i + 1) * (tr // 8), nb8 - 1), 0)),
                  pl.BlockSpec((None, tr, fb), lambda j, i: (j, jnp.minimum(i, nt - 1), 0)),
                  pl.BlockSpec((None, 8, fb), lambda j, i: (j, jnp.minimum((i + 1) * (tr // 8), nb8 - 1), 0)),
                  pl.BlockSpec((2, None, 3, 1, fb), lambda j, i: (0, j, 0, 0, 0)),
                  pl.BlockSpec((2, None, 1, fb), lambda j, i: (0, j, 0, 0)),
                  pl.BlockSpec((tr, kd), lambda j, i: (jnp.maximum(i - 1, 0), 0))],
        out_specs=[pl.BlockSpec((2, None, tr, fb), row),
                   pl.BlockSpec((2, None, 3, 1, fb), lambda j, i: (0, j, 0, 0, 0)),
                   pl.BlockSpec((2, None, 1, fb), lambda j, i: (0, j, 0, 0)),
                   pl.BlockSpec((2, None, kd, fb), lambda j, i: (0, j, 0, 0))],
        out_shape=[SDS(u.shape, F32), SDS(cw.shape, F32), SDS(cb.shape, F32), SDS((2, nj, kd, fb), F32)],
        scratch_shapes=[pltpu.VMEM((2, 2, tr, fb), BF)],
        compiler_params=_cp(2))(u, u, u, dz, dz, cw, cb, a)


def ffn_fwd(i, h, mask, w):
    a = norm_fwd(f"ffn{i}_norm", h, w["ng"], mask)
    u, z = ffn_up_core(f"ffn{i}_up_core", a, w["up"].reshape(2, 4, D, FFN_B), w["cw"], w["cb"])
    h2 = lin_bi(f"ffn{i}_down", z, w["down"], res=h)
    return h2, (h, a, u, z)


def ffn_bwd(i, dh2, mask, w, saved):
    h, a, u, z = saved
    lp = h.shape[0]
    g = {}
    g["down"] = wgrad_bi(f"ffn{i}_dwdown", z, dh2)
    dz = lin_t_bo(f"ffn{i}_dz", dh2, w["down"])
    du, g["cw"], g["cb"], dwup = ffn_core_bwd(f"ffn{i}_core_b", u, dz, w["cw"], w["cb"], a)
    du = du.reshape(8, lp, FFN_B)
    g["up"] = dwup.reshape(8, D, FFN_B)
    da = lin_t_bi(f"ffn{i}_da", du, w["up"])
    dh, g["ng"] = norm_bwd(f"ffn{i}_norm_b", h, w["ng"], mask, da, dh2)
    return dh, g


HG_HB = 4


def _hgrn_fn(p, c, k, x):
    lb, go = p
    q, f, iv, g = x[0][0], x[0][1], x[0][2], x[0][3]
    (st_all,) = k
    qs = _silu(q)
    forget = lb + (1.0 - lb) * jax.nn.sigmoid(f)
    logf = jnp.log(forget)
    kk = 1.0 - forget
    gc_all = _seg_cumsum(logf, HG_C)
    r = lax.broadcasted_iota(jnp.int32, (HG_C, HG_C), 0)
    cc = lax.broadcasted_iota(jnp.int32, (HG_C, HG_C), 1)
    ns = CH // HG_C
    cells = [(j, s) for j in range(HG_HB) for s in range(ns)]

    def blk(t, j, s):
        return t[HG_C * s:HG_C * (s + 1), HG_D * j:HG_D * (j + 1)]

    gl = {c: jnp.sum(blk(logf, *c), axis=0, keepdims=True) for c in cells}
    qd = {c: blk(qs, *c) * jnp.exp(blk(gc_all, *c)) for c in cells}
    ki = {c: blk(kk, *c) * jnp.exp(-blk(gc_all, *c)) for c in cells}
    up = {c: mm_tn(blk(iv, *c), blk(kk, *c) * jnp.exp(gl[c] - blk(gc_all, *c))) for c in cells}
    st, sts = {}, []
    for j in range(HG_HB):
        cur = st_all[j]
        for s in range(ns):
            st[(j, s)] = cur
            cur = cur * jnp.exp(gl[(j, s)]) + up[(j, s)]
        sts.append(cur)
    both = {c: mm_nt(qd[c], jnp.concatenate([st[c], ki[c]], axis=0)) for c in cells}
    oc = {c: mm(jnp.where(r >= cc, both[c][:, HG_D:], 0.0), blk(iv, *c)) + both[c][:, :HG_D] for c in cells}
    zs = []
    for j in range(HG_HB):
        o = jnp.concatenate([oc[(j, s)] for s in range(ns)], axis=0)
        zs.append(_rms(o, go) * _silu(g[:, HG_D * j:HG_D * (j + 1)]))
    return (jnp.stack(sts, axis=0),), (jnp.concatenate(zs, axis=1),)


def _hgrn_args(u4, lb, go):
    lp = u4.shape[2]
    wb = HG_HB * HG_D
    xs = [Arg(u4, (4, None, CH, wb), lambda o, t: (0, o, t, 0))]
    ps = [Arg(lb, (1, wb), lambda o, t: (0, o)), Arg(go, (1, HG_D), lambda o, t: (0, 0), shared=True)]
    return (HG_H // HG_HB, lp // CH), ps, xs


def hgrn_fwd(h, mask, w):
    lp = h.shape[0]
    a = norm_fwd("hgrn_norm", h, w["ng"], mask)
    u4 = lin_bo("hgrn_in", a, w["win"]).reshape(4, HG_H // HG_HB, lp, HG_HB * HG_D)
    grid, ps, xs = _hgrn_args(u4, w["lb"], w["go"])
    (z,), (st,) = seq_fwd("hgrn_core", _hgrn_fn, grid, ps, [], xs,
                          [((lp, D), BF, (CH, HG_HB * HG_D), lambda o, t: (t, o))], carries=[(HG_HB, HG_D, HG_D)])
    h2 = lin("hgrn_out", z, w["wo"], res=h)
    return h2, (h, a, u4, z, st)


def hgrn_bwd(dh2, mask, w, saved):
    h, a, u4, z, st = saved
    lp = h.shape[0]
    g = {}
    g["wo"] = wgrad("hgrn_dwo", z, dh2)
    dz = lin_t("hgrn_dz", dh2, w["wo"])
    grid, ps, xs = _hgrn_args(u4, w["lb"], w["go"])
    (du4,), (g["lb"], g["go"]) = seq_bwd("hgrn_core_b", _hgrn_fn, grid, ps, [], xs,
                                         [Arg(dz, (CH, HG_HB * HG_D), lambda o, t: (t, o))], saved=[st],
                                         carries=[(HG_HB, HG_D, HG_D)])
    du = du4.reshape(N_DEV, lp, HG_HB * HG_D)
    g["win"] = wgrad_bo("hgrn_dwin", a, du)
    da = lin_t_bi("hgrn_da", du, w["win"])
    dh, g["ng"] = norm_bwd("hgrn_norm_b", h, w["ng"], mask, da, dh2)
    return dh, g


S5_W = S5_SG * S5_P


def s5_tables(lam_re, lam_im, log_dt, b_re, b_im, c_re, c_im):
    dt = jnp.exp(log_dt)[:, None]
    mag = jnp.exp(lam_re * dt)
    abar_re = mag * jnp.cos(lam_im * dt)
    abar_im = mag * jnp.sin(lam_im * dt)
    den = lam_re * lam_re + lam_im * lam_im
    zoh_re = ((abar_re - 1.0) * lam_re + abar_im * lam_im) / den
    zoh_im = (abar_im * lam_re - (abar_re - 1.0) * lam_im) / den
    bbar_re = zoh_re[..., None] * b_re - zoh_im[..., None] * b_im
    bbar_im = zoh_re[..., None] * b_im + zoh_im[..., None] * b_re
    eye = jnp.eye(S5_SG, dtype=F32)

    def blockdiag_in(b):
        t = b.reshape(N_DEV, S5_SG, S5_P, S5_K).transpose(0, 1, 3, 2)
        return jnp.einsum("jakp,ab->jakbp", t, eye).reshape(N_DEV, S5_SG * S5_K, S5_W)

    def blockdiag_out(c):
        t = c.reshape(N_DEV, S5_SG, S5_K, S5_P).transpose(0, 1, 3, 2)
        return jnp.einsum("japk,ab->japbk", t, eye).reshape(N_DEV, S5_W, S5_SG * S5_K)

    wb = jnp.concatenate([blockdiag_in(bbar_re), blockdiag_in(bbar_im)], axis=2)
    wc = jnp.concatenate([blockdiag_out(c_re), -blockdiag_out(c_im)], axis=1)

    def powers(n):
        steps = n[:, None, None] * dt[None]
        pm = jnp.exp(lam_re[None] * steps)
        pr = (pm * jnp.cos(lam_im[None] * steps)).reshape(-1, N_DEV, S5_W).transpose(1, 0, 2)
        pi = (pm * jnp.sin(lam_im[None] * steps)).reshape(-1, N_DEV, S5_W).transpose(1, 0, 2)
        return jnp.concatenate([pr, pi], axis=2)

    apow = powers(2.0 ** jnp.arange(6, dtype=F32))[:, :, None, :]
    ptab = powers(jnp.arange(CH, dtype=F32) + 1.0)
    return wb, wc, apow, ptab


def _cmul(ar, ai, xr, xi):
    return ar * xr - ai * xi, ar * xi + ai * xr


S5_BB = 8


def _s5_fn(p, c, k, x):
    wb, wc, apow, ptab, dsk = p
    (a,) = x
    (x0,) = k
    blocks = range(S5_BB)
    aj = [a[:, 128 * j:128 * (j + 1)] for j in blocks]
    bu = [mm(aj[j], wb[j]) for j in blocks]
    xxs, x0n = [], []
    for j in blocks:
        xr, xi = bu[j][:, :S5_W], bu[j][:, S5_W:]
        for s in range(6):
            asr, asi = apow[j][s][:, :S5_W], apow[j][s][:, S5_W:]
            dr, di = _cmul(asr, asi, shift_rows(xr, 1 << s), shift_rows(xi, 1 << s))
            xr, xi = xr + dr, xi + di
        dr, di = _cmul(ptab[j][:, :S5_W], ptab[j][:, S5_W:], x0[j][:, :S5_W], x0[j][:, S5_W:])
        xx = jnp.concatenate([xr + dr, xi + di], axis=1)
        last = lax.broadcasted_iota(jnp.int32, xx.shape, 0) == CH - 1
        x0n.append(jnp.sum(jnp.where(last, xx, 0.0), axis=0, keepdims=True))
        xxs.append(xx)
    y = jnp.concatenate([mm(xxs[j], wc[j]) for j in blocks], axis=1)
    return (jnp.stack(x0n, axis=0),), (jax.nn.gelu(y + dsk * a),)


def _s5_args(a, tb, dsk):
    lp = a.shape[0]
    wb, wc, apow, ptab = tb
    ps = [Arg(wb, (S5_BB, 128, 2 * S5_W), lambda o, t: (o, 0, 0)), Arg(wc, (S5_BB, 2 * S5_W, 128), lambda o, t: (o, 0, 0)),
          Arg(apow, (S5_BB, 6, 1, 2 * S5_W), lambda o, t: (o, 0, 0, 0)),
          Arg(ptab, (S5_BB, CH, 2 * S5_W), lambda o, t: (o, 0, 0)), Arg(dsk, (1, 128 * S5_BB), lambda o, t: (0, o))]
    xs = [Arg(a, (CH, 128 * S5_BB), lambda o, t: (t, o))]
    return (N_DEV // S5_BB, lp // CH), ps, xs


def _glu_res_fn(p, c, k, x):
    h, vg = x
    return (), ((h + vg[:, :D] * jax.nn.sigmoid(vg[:, D:])) * c[0],)


def _glu_args(h, vg, mask):
    lp = h.shape[0]
    tr = _row_tile(lp)
    row = lambda o, t: (t, 0)
    return (1, lp // tr), [Arg(mask, (tr, 1), row)], [Arg(h, (tr, D), row), Arg(vg, (tr, 2 * D), row)], tr


def s5_fwd(h, mask, w):
    lp = h.shape[0]
    a = norm_fwd("s5_norm", h, w["ng"], mask, out_dtype=F32)
    grid, ps, xs = _s5_args(a, w["tb"], w["dsk"])
    (z,), (st,) = seq_fwd("s5_core", _s5_fn, grid, ps, [], xs,
                          [((lp, D), BF, (CH, 128 * S5_BB), lambda o, t: (t, o))], carries=[(S5_BB, 1, 2 * S5_W)])
    vg = lin("s5_glu", z, w["wglu"])
    grid2, cs, xs2, tr = _glu_args(h, vg, mask)
    (h2,), _ = seq_fwd("s5_res", _glu_res_fn, grid2, [], cs, xs2, [((lp, D), F32, (tr, D), lambda o, t: (t, 0))])
    return h2, (h, a, z, vg, st)


def s5_bwd(dh2, mask, w, saved):
    h, a, z, vg, st = saved
    g = {}
    grid2, cs, xs2, tr = _glu_args(h, vg, mask)
    (dskip, dvg), _ = seq_bwd("s5_res_b", _glu_res_fn, grid2, [], cs, xs2, [Arg(dh2, (tr, D), lambda o, t: (t, 0))])
    g["wglu"] = wgrad("s5_dwglu", z, dvg)
    dz = lin_t("s5_dz", dvg, w["wglu"])
    grid, ps, xs = _s5_args(a, w["tb"], w["dsk"])
    (da,), dps = seq_bwd("s5_core_b", _s5_fn, grid, ps, [], xs, [Arg(dz, (CH, 128 * S5_BB), lambda o, t: (t, o))],
                         saved=[st], carries=[(S5_BB, 1, 2 * S5_W)])
    g["tb"] = tuple(dps[:4])
    g["dsk"] = dps[4]
    dh, g["ng"] = norm_bwd("s5_norm_b", h, w["ng"], mask, da, dskip)
    return dh, g


def _rope_angles(lp, dim):
    pos = np.maximum(np.arange(lp, dtype=np.float32) - PAD, 0.0).astype(np.float32)
    inv = (1.0 / (ROPE_BASE ** (np.arange(0, dim, 2, dtype=np.float32) / dim))).astype(np.float32)
    return (pos[:, None] * inv[None, :]).astype(np.float32)


def ret_consts(lp):
    f = np.float32
    ang = _rope_angles(lp, RET_DK)
    lg = np.log(1.0 - np.exp2(-5.0 - np.arange(RET_H, dtype=f))).astype(f)
    p = np.arange(CH, dtype=f)
    diff = p[:, None] - p[None, :]
    decay = np.where(diff >= 0, np.exp(diff[None] * lg[:, None, None]), 0.0).astype(f)
    qd = np.exp((p[None, :] + 1.0) * lg[:, None])[..., None].astype(f)
    kd = np.exp((CH - 1.0 - p[None, :]) * lg[:, None])[..., None].astype(f)
    cd = np.exp(CH * lg)[:, None, None].astype(f)
    return np.cos(ang).astype(f), np.sin(ang).astype(f), decay, qd, kd, cd


def _ret_fn(p, c, k, x):
    (gn,) = p
    cos, sin, decay, qd, kd, cd = c
    (st,) = k
    (u,) = x
    hd = RET_DK // 2
    qk_w = RET_H * RET_DK
    heads = range(RET_H)

    def rope(t):
        t1, t2 = t[:, :hd], t[:, hd:]
        return jnp.concatenate([t1 * cos - t2 * sin, t1 * sin + t2 * cos], axis=1)

    qr = [rope(u[:, RET_DK * h:RET_DK * (h + 1)]) for h in heads]
    kr = [rope(u[:, qk_w + RET_DK * h:qk_w + RET_DK * (h + 1)]) * (RET_DK ** -0.5) for h in heads]
    v = [u[:, 2 * qk_w + RET_DV * h:2 * qk_w + RET_DV * (h + 1)] for h in heads]
    scores = [mm_nt(qr[h], kr[h]) for h in heads]
    inter = [mm(qr[h] * qd[h], st[h]) for h in heads]
    st_new = jnp.stack([st[h] * cd[h] + mm_tn(kr[h] * kd[h], v[h]) for h in heads], axis=0)
    o = [mm(scores[h] * decay[h], v[h]) + inter[h] for h in heads]
    zs = []
    for h in heads:
        mu = jnp.mean(o[h], axis=-1, keepdims=True)
        var = jnp.mean(jnp.square(o[h] - mu), axis=-1, keepdims=True)
        gate = u[:, 2 * qk_w + RET_H * RET_DV + RET_DV * h:2 * qk_w + RET_H * RET_DV + RET_DV * (h + 1)]
        zs.append((o[h] - mu) * lax.rsqrt(var + EPS) * gn[:, RET_DV * h:RET_DV * (h + 1)] * _silu(gate))
    return (st_new,), (jnp.concatenate(zs, axis=1),)


def _ret_args(u, gn, rc):
    lp, uw = u.shape
    cos, sin, decay, qd, kd, cd = rc
    full = lambda o, t: (0, 0, 0)
    ps = [Arg(gn, (1, RET_H * RET_DV), lambda o, t: (0, 0))]
    cs = [Arg(cos, (CH, RET_DK // 2), lambda o, t: (t, 0)), Arg(sin, (CH, RET_DK // 2), lambda o, t: (t, 0)),
          Arg(decay, (RET_H, CH, CH), full), Arg(qd, (RET_H, CH, 1), full), Arg(kd, (RET_H, CH, 1), full),
          Arg(cd, (RET_H, 1, 1), full)]
    xs = [Arg(u, (CH, uw), lambda o, t: (t, 0))]
    return (1, lp // CH), ps, cs, xs


def ret_fwd(h, mask, w):
    lp = h.shape[0]
    a = norm_fwd("ret_norm", h, w["ng"], mask)
    u = lin("ret_in", a, w["win"])
    grid, ps, cs, xs = _ret_args(u, w["gn"], w["rc"])
    (z,), (st,) = seq_fwd("ret_core", _ret_fn, grid, ps, cs, xs,
                          [((lp, 2 * D), BF, (CH, RET_H * RET_DV), lambda o, t: (t, 0))],
                          carries=[(RET_H, RET_DK, RET_DV)], save_dtype=BF)
    h2 = lin("ret_out", z, w["wo"], res=h)
    return h2, (h, a, u, z, st)


def ret_bwd(dh2, mask, w, saved):
    h, a, u, z, st = saved
    g = {}
    g["wo"] = wgrad("ret_dwo", z, dh2)
    dz = lin_t("ret_dz", dh2, w["wo"])
    grid, ps, cs, xs = _ret_args(u, w["gn"], w["rc"])
    (du,), (g["gn"],) = seq_bwd("ret_core_b", _ret_fn, grid, ps, cs, xs,
                                [Arg(dz, (CH, RET_H * RET_DV), lambda o, t: (t, 0))], saved=[st],
                                carries=[(RET_H, RET_DK, RET_DV)])
    g["win"] = wgrad("ret_dwin", a, du)
    da = lin_t("ret_da", du, w["win"])
    dh, g["ng"] = norm_bwd("ret_norm_b", h, w["ng"], mask, da, dh2)
    return dh, g


def mla_consts(lp):
    ang = _rope_angles(lp, MLA_ROPE)
    cos = np.concatenate([np.cos(ang), np.cos(ang)], axis=1).astype(np.float32)
    sin = np.concatenate([np.sin(ang), np.sin(ang)], axis=1).astype(np.float32)
    hd = MLA_ROPE // 2
    i = np.arange(hd)
    rot = np.zeros((MLA_ROPE, MLA_ROPE), np.float32)
    rot[hd + i, i] = -1.0
    rot[i, hd + i] = 1.0
    return cos, sin, rot


def _mla_prep1_fn(p, c, k, x):
    gq, gkv = p
    (down,) = x
    return (), (_rms(down[:, :MLA_QL], gq), _rms(down[:, MLA_QL:MLA_QL + MLA_KVL], gkv), down[:, MLA_QL + MLA_KVL:])


def _mla_prep2(p, c, x):
    gq, gk = p
    cos, sin, rot = c
    q, kv, kpe = x
    qn = _rms(q, gq)
    qn_n, qn_r = qn[:, :MLA_NOPE], qn[:, MLA_NOPE:]
    qo = jnp.concatenate([qn_n, qn_r * cos + cright(qn_r, rot) * sin], axis=1)
    kn = kv[:, :MLA_NOPE]
    ms = (jnp.sum(kn * kn, axis=-1, keepdims=True) + jnp.sum(kpe * kpe, axis=-1, keepdims=True)) / MLA_QK
    r = lax.rsqrt(ms + EPS)
    kr = kpe * r * gk[:, MLA_NOPE:]
    ko = jnp.concatenate([kn * r * gk[:, :MLA_NOPE], kr * cos + cright(kr, rot) * sin], axis=1)
    return qo, ko, kv[:, MLA_NOPE:]


def _mla_prep2_fn(p, c, k, x):
    return (), _mla_prep2(p, c, x)[:2]


def _mla_prep2_b_fn(p, c, k, x):
    return (), _mla_prep2(p, c, x)


def _prep1_args(down, gq, gkv):
    lp = down.shape[0]
    tr = _row_tile(lp)
    ps = [Arg(gq, (1, MLA_QL), lambda o, t: (0, 0), shared=True), Arg(gkv, (1, MLA_KVL), lambda o, t: (0, 0), shared=True)]
    return (1, lp // tr), ps, [Arg(down, (tr, down.shape[1]), lambda o, t: (t, 0))], tr


def _prep2_args(qraw, kvraw, kpe, gq, gk, mc):
    lp = kpe.shape[0]
    tr = _row_tile(lp)
    cos, sin, rot = mc
    ps = [Arg(gq, (1, MLA_QK), lambda o, t: (0, 0), shared=True), Arg(gk, (1, MLA_QK), lambda o, t: (0, 0), shared=True)]
    cs = [Arg(cos, (tr, MLA_ROPE), lambda o, t: (o, 0)), Arg(sin, (tr, MLA_ROPE), lambda o, t: (o, 0)),
          Arg(rot, (MLA_ROPE, MLA_ROPE), lambda o, t: (0, 0))]
    xs = [Arg(qraw, (None, tr, MLA_QK), lambda o, t: (t, o, 0)), Arg(kvraw, (None, tr, MLA_NOPE + MLA_V), lambda o, t: (t, o, 0)),
          Arg(kpe, (tr, MLA_ROPE), lambda o, t: (o, 0), acc=True)]
    return (lp // tr, MLA_H), ps, cs, xs, tr


ATT_HB = 2


def _attn_tile(lp):
    return 832 if (lp % 832 == 0 and lp > 832) else 64


def _attn_mask(qi, ki, ta):
    rows = qi * ta + lax.broadcasted_iota(jnp.int32, (ta, ta), 0)
    cols = ki * ta + lax.broadcasted_iota(jnp.int32, (ta, ta), 1)
    return (cols >= PAD) & ((cols // CH) <= (rows // CH))


def attn_fwd(q, k, kv):
    nh, lp, dq = q.shape
    ta = _attn_tile(lp)
    nb = lp // ta
    scale = MLA_QK ** -0.5

    hb = ATT_HB

    def body(q_ref, k_ref, v_ref, o_ref, lse_ref, m_s, l_s, acc_s):
        qi, ki = pl.program_id(1), pl.program_id(2)

        @pl.when(ki == 0)
        def _():
            m_s[...] = jnp.full_like(m_s, NEG)
            l_s[...] = jnp.zeros_like(l_s)
            acc_s[...] = jnp.zeros_like(acc_s)

        def step(masked):
            ss = [_bdot(q_ref[j], k_ref[j], 1, 1) * scale for j in range(hb)]
            ps = []
            for j in range(hb):
                s = jnp.where(_attn_mask(qi, ki, ta), ss[j], NEG) if masked else ss[j]
                m_new = jnp.maximum(m_s[j], jnp.max(s, axis=-1, keepdims=True))
                p = jnp.exp(s - m_new)
                alpha = jnp.exp(m_s[j] - m_new)
                l_s[j] = alpha * l_s[j] + jnp.sum(p, axis=-1, keepdims=True)
                m_s[j] = m_new
                ps.append((p, alpha))
            for j in range(hb):
                acc_s[j] = ps[j][1] * acc_s[j] + _bdot(ps[j][0], v_ref[j], 1, 0)

        pl.when((ki == qi) | (ki == 0))(functools.partial(step, True))
        pl.when((ki < qi) & (ki > 0))(functools.partial(step, False))

        @pl.when(ki == nb - 1)
        def _():
            for j in range(hb):
                o_ref[:, MLA_V * j:MLA_V * (j + 1)] = (acc_s[j] / l_s[j]).astype(o_ref.dtype)
                lse_ref[j] = m_s[j] + jnp.log(l_s[j])

    return pl.pallas_call(
        body, name="mla_attn", grid=(nh // hb, nb, nb),
        in_specs=[pl.BlockSpec((hb, ta, dq), lambda h, qi, ki: (h, qi, 0)),
                  pl.BlockSpec((hb, ta, dq), lambda h, qi, ki: (h, jnp.minimum(ki, qi), 0)),
                  pl.BlockSpec((hb, ta, MLA_V), lambda h, qi, ki: (h, jnp.minimum(ki, qi), 1))],
        out_specs=[pl.BlockSpec((ta, hb * MLA_V), lambda h, qi, ki: (qi, h)),
                   pl.BlockSpec((hb, ta, 1), lambda h, qi, ki: (h, qi, 0))],
        out_shape=[SDS((lp, nh * MLA_V), BF), SDS((nh, lp, 1), F32)],
        scratch_shapes=[pltpu.VMEM((hb, ta, 1), F32), pltpu.VMEM((hb, ta, 1), F32), pltpu.VMEM((hb, ta, MLA_V), F32)],
        compiler_params=_cp(3))(q, k, kv)


def attn_bwd(q, k, kv, o, do, lse):
    nh, lp, dq = q.shape
    ta = _attn_tile(lp)
    nb = lp // ta
    scale = MLA_QK ** -0.5

    def body(q_ref, k_ref, v_ref, o_ref, do_ref, lse_ref, dq_ref, dk_ref, dv_ref, dk_s, dv_s):
        ki, qi = pl.program_id(1), pl.program_id(2)

        @pl.when((ki == 0) & (qi == 0))
        def _():
            dq_ref[...] = jnp.zeros_like(dq_ref)

        @pl.when(qi == 0)
        def _():
            dk_s[...] = jnp.zeros_like(dk_s)
            dv_s[...] = jnp.zeros_like(dv_s)

        def step(masked):
            dov = do_ref[...]
            s = _bdot(q_ref[...], k_ref[...], 1, 1) * scale
            dp = _bdot(dov, v_ref[...], 1, 1)
            if masked:
                s = jnp.where(_attn_mask(qi, ki, ta), s, NEG)
            p = jnp.exp(s - lse_ref[...])
            delta = jnp.sum(dov * o_ref[...].astype(F32), axis=-1, keepdims=True)
            dv_s[...] += _bdot(p, dov, 0, 0)
            ds = p * (dp - delta) * scale
            rows = pl.ds(pl.multiple_of(qi * ta, ta), ta)
            dq_ref[rows, :] += _bdot(ds, k_ref[...], 1, 0)
            dk_s[...] += _bdot(ds, q_ref[...], 0, 0)

        pl.when((ki == qi) | (ki == 0))(functools.partial(step, True))
        pl.when((ki < qi) & (ki > 0))(functools.partial(step, False))

        @pl.when(qi == nb - 1)
        def _():
            dk_ref[...] = dk_s[...]
            dv_ref[...] = dv_s[...]

    qmap = lambda h, ki, qi: (h, jnp.maximum(qi, ki), 0)
    return pl.pallas_call(
        body, name="mla_attn_b", grid=(nh, nb, nb),
        in_specs=[pl.BlockSpec((None, ta, dq), qmap),
                  pl.BlockSpec((None, ta, dq), lambda h, ki, qi: (h, ki, 0)),
                  pl.BlockSpec((None, ta, MLA_V), lambda h, ki, qi: (h, ki, 1)),
                  pl.BlockSpec((ta, MLA_V), lambda h, ki, qi: (jnp.maximum(qi, ki), h)),
                  pl.BlockSpec((ta, MLA_V), lambda h, ki, qi: (jnp.maximum(qi, ki), h)),
                  pl.BlockSpec((None, ta, 1), qmap)],
        out_specs=[pl.BlockSpec((None, lp, dq), lambda h, ki, qi: (h, 0, 0)),
                   pl.BlockSpec((None, ta, dq), lambda h, ki, qi: (h, ki, 0)),
                   pl.BlockSpec((None, ta, MLA_V), lambda h, ki, qi: (h, ki, 0))],
        out_shape=[SDS((nh, lp, dq), F32), SDS((nh, lp, dq), F32), SDS((nh, lp, MLA_V), F32)],
        scratch_shapes=[pltpu.VMEM((ta, dq), F32), pltpu.VMEM((ta, MLA_V), F32)],
        compiler_params=_cp(3))(q, k, kv, o, do, lse)


def mla_fwd(h, mask, w):
    lp = h.shape[0]
    a = norm_fwd("mla_norm", h, w["ng"], mask)
    down = lin("mla_down", a, w["wdown"])
    grid, ps, xs, tr = _prep1_args(down, w["gcq"], w["gckv"])
    row = lambda o, t: (t, 0)
    (cq, ckv, kpe), _ = seq_fwd("mla_prep1", _mla_prep1_fn, grid, ps, [], xs,
                                [((lp, MLA_QL), BF, (tr, MLA_QL), row), ((lp, MLA_KVL), BF, (tr, MLA_KVL), row),
                                 ((lp, MLA_ROPE), F32, (tr, MLA_ROPE), row)])
    qraw = lin_bo("mla_uq", cq, w["wuq"])
    kvraw = lin_bo("mla_ukv", ckv, w["wukv"])
    grid, ps, cs, xs, tr = _prep2_args(qraw, kvraw, kpe, w["gq"], w["gk"], w["mc"])
    hm = lambda o, t: (t, o, 0)
    (q, k), _ = seq_fwd("mla_prep2", _mla_prep2_fn, grid, ps, cs, xs,
                        [((MLA_H, lp, MLA_QK), BF, (None, tr, MLA_QK), hm), ((MLA_H, lp, MLA_QK), BF, (None, tr, MLA_QK), hm)])
    o, lse = attn_fwd(q, k, kvraw)
    h2 = lin("mla_out", o, w["wo"], res=h)
    return h2, (h, a, down, cq, ckv, kpe, qraw, kvraw, q, k, o, lse)


def mla_bwd(dh2, mask, w, saved, emit):
    h, a, down, cq, ckv, kpe, qraw, kvraw, q, k, o, lse = saved
    lp = h.shape[0]
    g = {}
    g["wo"] = wgrad("mla_dwo", o, dh2)
    do = lin_t("mla_do", dh2, w["wo"])
    do = emit("wo", [g["wo"]], do)
    dq, dk, dv = attn_bwd(q, k, kvraw, o, do, lse)
    grid, ps, cs, xs, tr = _prep2_args(qraw, kvraw, kpe, w["gq"], w["gk"], w["mc"])
    hm = lambda o, t: (t, o, 0)
    (dqraw, dkvraw, dkpe), (g["gq"], g["gk"]) = seq_bwd(
        "mla_prep2_b", _mla_prep2_b_fn, grid, ps, cs, xs,
        [Arg(dq, (None, tr, MLA_QK), hm), Arg(dk, (None, tr, MLA_QK), hm), Arg(dv, (None, tr, MLA_V), hm)])
    g["wuq"] = wgrad_bo("mla_dwuq", cq, dqraw)
    dcq = lin_t_bi("mla_dcq", dqraw, w["wuq"])
    g["wukv"] = wgrad_bo("mla_dwukv", ckv, dkvraw)
    dckv = lin_t_bi("mla_dckv", dkvraw, w["wukv"])
    dckv = emit("wu", [g["wuq"], g["wukv"]], dckv)
    grid, ps, xs, tr = _prep1_args(down, w["gcq"], w["gckv"])
    row = lambda o, t: (t, 0)
    (ddown,), (g["gcq"], g["gckv"]) = seq_bwd(
        "mla_prep1_b", _mla_prep1_fn, grid, ps, [], xs,
        [Arg(dcq, (tr, MLA_QL), row), Arg(dckv, (tr, MLA_KVL), row), Arg(dkpe, (tr, MLA_ROPE), row)])
    g["wdown"] = wgrad("mla_dwdown", a, ddown)
    da = lin_t("mla_da", ddown, w["wdown"])
    dh, g["ng"] = norm_bwd("mla_norm_b", h, w["ng"], mask, da, dh2)
    return dh, g


def loss_head(h, target):
    lp, d = h.shape
    assert OFF == CH

    def body(h_ref, t_ref, loss_ref, dh_ref):
        i = pl.program_id(0)

        @pl.when(i == 0)
        def _():
            loss_ref[...] = jnp.zeros_like(loss_ref)

        e = jnp.where(i > 0, h_ref[...] - t_ref[...], 0.0)
        loss_ref[...] += jnp.sum(e * e) * (0.5 / d)
        dh_ref[...] = e * (1.0 / d)

    return pl.pallas_call(
        body, name="loss_head", grid=(lp // CH,),
        in_specs=[pl.BlockSpec((CH, d), lambda i: (i, 0)), pl.BlockSpec((CH, d), lambda i: (jnp.maximum(i - 1, 0), 0))],
        out_specs=[pl.BlockSpec((8, 128), lambda i: (0, 0)), pl.BlockSpec((CH, d), lambda i: (i, 0))],
        out_shape=[SDS((8, 128), F32), SDS((lp, d), F32)], compiler_params=_cp(1))(h, target)


ADAM_LAND_BYTES = 20 * 1024 * 1024


def _adam_tile(r, c, nl):
    if r % 8:
        return r
    best = 8
    for t in range(8, r + 1, 8):
        if r % t == 0 and N_DEV * t * c * 4 * 2 * nl <= ADAM_LAND_BYTES:
            best = t
    return best


def adamw(name, lands, w, m, v):
    nl, r, c = w.shape
    tr = _adam_tile(r, c, nl)
    c1 = 1.0 / (1.0 - ADAM_B1 ** ADAM_STEP)
    c2 = 1.0 / (1.0 - ADAM_B2 ** ADAM_STEP)

    def body(*refs):
        l_refs = refs[:nl]
        w_ref, m_ref, v_ref, g_ref, d_ref, nm_ref, nv_ref = refs[nl:]
        layer = pl.program_id(0)
        for j in range(nl):
            @pl.when(layer == j)
            def _(j=j):
                g = l_refs[j][0]
                for i in range(1, N_DEV):
                    g = g + l_refs[j][i]
                g_ref[...] = g

        g = g_ref[...]
        nm = ADAM_B1 * m_ref[...] + (1.0 - ADAM_B1) * g
        nv = ADAM_B2 * v_ref[...] + (1.0 - ADAM_B2) * (g * g)
        nm_ref[...] = nm
        nv_ref[...] = nv
        d_ref[...] = -ADAM_LR * ((nm * c1) / (jnp.sqrt(nv * c2) + ADAM_EPS) + ADAM_WD * w_ref[...])

    blk = pl.BlockSpec((None, tr, c), lambda l, i: (l, i, 0))
    land_specs = [pl.BlockSpec((N_DEV, tr, c), lambda l, i, j=j: (0, jnp.where(l == j, i, 0), 0)) for j in range(nl)]
    return pl.pallas_call(
        body, name=name, grid=(nl, r // tr), in_specs=land_specs + [blk, blk, blk],
        out_specs=[blk, blk, blk, blk], out_shape=[SDS((nl, r, c), F32)] * 4, compiler_params=_cp(2))(*lands, w, m, v)


ANY = pl.BlockSpec(memory_space=pl.ANY)
MESH = pl.DeviceIdType.MESH


def _me():
    return lax.axis_index("x"), lax.axis_index("y"), lax.axis_index("c")


def _peers():
    x, y, c = _me()
    out = []
    for k in range(1, N_DEV):
        px = 1 - x if k & 4 else x
        py = 1 - y if k & 2 else y
        pc = 1 - c if k & 1 else c
        out.append(((px, py, pc), 4 * px + 2 * py + pc))
    return out


HBM_SPEC = pl.BlockSpec(memory_space=pltpu.HBM)
SEM_SPEC = pl.BlockSpec(memory_space=pltpu.SEMAPHORE)
DATAFLOW = pltpu.SideEffectType.DATAFLOW_SIDE_EFFECTING


def _my_index():
    return 4 * lax.axis_index("x") + 2 * lax.axis_index("y") + lax.axis_index("c")


def _hbm(a):
    return pltpu.with_memory_space_constraint(a, pltpu.HBM)


NP = N_DEV - 1


def _push_copy(x_ref, land_ref, send, recv, pid, src_idx, dst_idx, scatter):
    src = x_ref.at[src_idx] if scatter else x_ref
    return pltpu.make_async_remote_copy(src_ref=src, dst_ref=land_ref.at[dst_idx], send_sem=send, recv_sem=recv,
                                        device_id=pid, device_id_type=MESH)


def push_start(name, xs, me, scatter, carry=None):
    n = len(xs)
    lands = []
    for a in xs:
        own = lax.dynamic_index_in_dim(a, me, 0, keepdims=True) if scatter else a[None]
        z = lax.empty((N_DEV,) + own.shape[1:], a.dtype)
        lands.append(lax.dynamic_update_slice(z, own, (me,) + (0,) * (own.ndim - 1)))
    ns = 2 * NP * n
    ops = xs + lands + ([carry] if carry is not None else [])
    na = len(ops)

    def body(*refs):
        x_refs, land_refs = refs[:n], refs[n:2 * n]
        sems = refs[na:na + ns]
        token = refs[-1]
        x, y, c = _me()
        mine = 4 * x + 2 * y + c
        for i in range(n):
            for k, (pid, pidx) in enumerate(_peers()):
                s = 2 * (NP * i + k)
                _push_copy(x_refs[i], land_refs[i], sems[s], sems[s + 1], pid, pidx, mine, scatter).start()
        token[...] = jnp.zeros_like(token)

    out_shape = ([pltpu.SemaphoreType.DMA(())] * ns + [pltpu.HBM(a.shape, a.dtype) for a in ops]
                 + [SDS((8, 128), F32)])
    res = pl.pallas_call(
        body, name=name, out_shape=out_shape, in_specs=[HBM_SPEC] * na,
        out_specs=[SEM_SPEC] * ns + [HBM_SPEC] * na + [pl.BlockSpec(memory_space=pltpu.VMEM)],
        input_output_aliases={i: ns + i for i in range(na)},
        compiler_params=pltpu.CompilerParams(has_side_effects=DATAFLOW))(*[_hbm(a) for a in ops])
    sems, thru, token = res[:ns], res[ns:-1], res[-1]
    handles = [dict(x=thru[i], land=thru[n + i], sems=list(sems[2 * NP * i:2 * NP * (i + 1)]), token=token)
               for i in range(n)]
    return (handles, thru[2 * n]) if carry is not None else handles


def push_wait(name, hds, after, scatter):
    n = len(hds)
    ns = 2 * NP

    def body(*refs):
        x_refs, land_refs = refs[:n], refs[n:2 * n]
        sems = refs[2 * n:2 * n + ns * n]
        for i in range(n):
            for k, (pid, pidx) in enumerate(_peers()):
                cp = _push_copy(x_refs[i], land_refs[i], sems[ns * i + 2 * k], sems[ns * i + 2 * k + 1], pid, pidx, pidx,
                                scatter)
                cp.wait_send()
                cp.wait_recv()

    arrs = [hd["x"] for hd in hds] + [hd["land"] for hd in hds]
    sems = [s for hd in hds for s in hd["sems"]]
    res = pl.pallas_call(
        body, name=name, out_shape=[pltpu.HBM(a.shape, a.dtype) for a in arrs],
        in_specs=[HBM_SPEC] * (2 * n) + [SEM_SPEC] * (ns * n) + [ANY], out_specs=[HBM_SPEC] * (2 * n),
        input_output_aliases={i: i for i in range(2 * n)},
        compiler_params=pltpu.CompilerParams(has_side_effects=DATAFLOW))(*arrs, *sems, after)
    return list(res[n:])


WEIGHTS = ['meta_tokens', 'norm_mix_g', 'norm_ffn_g', 'mla_w_down', 'mla_cq_norm_g', 'mla_ckv_norm_g', 'mla_w_uq',
           'mla_w_ukv', 'mla_q_head_g', 'mla_k_head_g', 'mla_w_o', 'hgrn_w_in', 'hgrn_lb_logits', 'hgrn_o_norm_g',
           'hgrn_w_o', 's5_lam_re', 's5_lam_im', 's5_log_dt', 's5_b_re', 's5_b_im', 's5_c_re', 's5_c_im', 's5_d',
           's5_w_glu', 'ret_w_in', 'ret_gn_g', 'ret_w_o', 'ffn_w_up', 'ffn_conv_w', 'ffn_conv_b', 'ffn_w_down']
BIG = ['mla_w_down', 'mla_w_uq', 'mla_w_ukv', 'mla_w_o', 'hgrn_w_in', 'hgrn_w_o', 's5_w_glu', 'ret_w_in', 'ret_w_o',
       'ffn_w_up', 'ffn_w_down']
SMALL_SH = ['meta_tokens', 's5_d', 'ret_gn_g', 'ffn_conv_w']
REP_S5 = ['s5_lam_re', 's5_lam_im', 's5_log_dt', 's5_b_re', 's5_b_im', 's5_c_re', 's5_c_im']
REP_REST = ['norm_mix_g', 'norm_ffn_g', 'mla_cq_norm_g', 'mla_ckv_norm_g', 'mla_q_head_g', 'mla_k_head_g',
            'hgrn_lb_logits', 'hgrn_o_norm_g', 'ffn_conv_b']
SMALL_REP = REP_REST + REP_S5
LANE = 128


def _flat(arrs, mult):
    v = jnp.concatenate([a.reshape(-1) for a in arrs])
    pad = (-v.shape[0]) % mult
    return jnp.pad(v, (0, pad)).reshape(-1, LANE)


def _unflat(flat2d, like):
    v = flat2d.reshape(-1)
    out, o = [], 0
    for a in like:
        out.append(v[o:o + a.size].reshape(a.shape))
        o += a.size
    return out


def _lb_of(logits):
    cum = jnp.cumsum(jax.nn.softmax(logits, axis=0), axis=0)
    return (cum - cum[0:1])[1:2]


def _cols_to_blocks(g):
    k, n = g.shape
    return g.reshape(k, N_DEV, n // N_DEV).transpose(1, 0, 2)


def _blocks_to_cols(wb):
    nb, k, n = wb.shape
    return wb.transpose(1, 0, 2).reshape(k, nb * n)


SUBS = ['mla', 'ffn0', 'hgrn', 'ffn1', 's5', 'ffn2', 'ret', 'ffn3']
GROUPS = [[('mla_w_down', 0), ('mla_w_uq', 0), ('mla_w_ukv', 0), ('mla_w_o', 0)],
          [('ffn_w_up', 0), ('ffn_w_down', 0)],
          [('hgrn_w_in', 0), ('hgrn_w_o', 0)],
          [('ffn_w_up', 1), ('ffn_w_down', 1)],
          [('s5_w_glu', 0)],
          [('ffn_w_up', 2), ('ffn_w_down', 2)],
          [('ret_w_in', 0), ('ret_w_o', 0)],
          [('ffn_w_up', 3), ('ffn_w_down', 3)]]


def _pack8(parts, mult):
    v = jnp.concatenate(parts, axis=1)
    return jnp.pad(v, ((0, 0), (0, (-v.shape[1]) % mult))).reshape(N_DEV, -1, LANE)


def _sub_weights(k, got, rep, tabs, lp):
    ngm, ngf = rep['norm_mix_g'], rep['norm_ffn_g']
    if k == 0:
        return dict(ng=ngm[0:1], wdown=got[0].reshape(D, -1), gcq=rep['mla_cq_norm_g'], gckv=rep['mla_ckv_norm_g'],
                    wuq=got[1], wukv=got[2], gq=rep['mla_q_head_g'], gk=rep['mla_k_head_g'], wo=got[3].reshape(D, D),
                    mc=mla_consts(lp))
    if k == 2:
        return dict(ng=ngm[1:2], win=got[0], lb=tabs['lb'], go=rep['hgrn_o_norm_g'], wo=got[1].reshape(D, D))
    if k == 4:
        return dict(ng=ngm[2:3], tb=tabs['tb'], dsk=tabs['s5_d'], wglu=_blocks_to_cols(got[0]))
    if k == 6:
        return dict(ng=ngm[3:4], win=_blocks_to_cols(got[0]), gn=tabs['ret_gn_g'], wo=got[1].reshape(2 * D, D),
                    rc=ret_consts(lp))
    i = k // 2
    return dict(ng=ngf[i:i + 1], up=got[0], cw=tabs['conv_w'][:, i].reshape(2, 4, 3, 1, FFN_B),
                cb=rep['ffn_conv_b'][i].reshape(2, 4, 1, FFN_B), down=got[1].reshape(4, FFN_B, D))


def _sub_grad_blocks(k, g):
    if k == 0:
        parts = [g['wdown'], g['wuq'], g['wukv'], g['wo']]
    elif k == 2:
        parts = [g['win'], g['wo']]
    elif k == 4:
        parts = [_cols_to_blocks(g['wglu'])]
    elif k == 6:
        parts = [_cols_to_blocks(g['win']), g['wo']]
    else:
        parts = [g['up'], g['down']]
    return parts


_FWD = [mla_fwd, None, hgrn_fwd, None, s5_fwd, None, ret_fwd, None]
_BWD = [mla_bwd, None, hgrn_bwd, None, s5_bwd, None, ret_bwd, None]


def _step(args):
    w = {n: args[n] for n in WEIGHTS}
    x2, tgt = args['x'][0], args['loss_target'][0]

    lp = x2.shape[0] + OFF
    me = _my_index()
    mask = _rowmask(lp)
    rep = {n: w[n] for n in SMALL_REP}

    xs, slots = [], []
    for gi, grp in enumerate(GROUPS):
        items = [w[n][l].astype(BF) for n, l in grp] + ([_flat([w[n] for n in SMALL_SH], LANE)] if gi == 0 else [])
        slots.append((len(xs), len(items)))
        xs += items
    gh = push_start("gather_start", xs, me, scatter=False)

    def fetch(gi, after):
        s, cnt = slots[gi]
        return push_wait("gather_wait_" + SUBS[gi], gh[s:s + cnt], after, scatter=False)

    got = fetch(0, x2)
    sm, o, smp = got[-1].reshape(N_DEV, -1), 0, {}
    for n in SMALL_SH:
        smp[n] = sm[:, o:o + w[n].size].reshape((N_DEV,) + w[n].shape)
        o += w[n].size
    meta = smp['meta_tokens'].transpose(1, 0, 2).reshape(N_META, D)
    lb, lb_vjp = jax.vjp(_lb_of, rep['hgrn_lb_logits'])
    s5p = [rep[n][0] for n in ('s5_lam_re', 's5_lam_im', 's5_log_dt', 's5_b_re', 's5_b_im', 's5_c_re', 's5_c_im')]
    tb, tb_vjp = jax.vjp(s5_tables, *s5p)
    tabs = dict(lb=lb, tb=tb, s5_d=smp['s5_d'].reshape(1, D), ret_gn_g=smp['ret_gn_g'].reshape(1, 2 * D),
                conv_w=smp['ffn_conv_w'])
    h = jnp.concatenate([jnp.zeros((PAD, D), F32), meta, x2], axis=0)
    ws, saved = [], []
    for k in range(8):
        if k > 0:
            got = fetch(k, h)
        ws.append(_sub_weights(k, got, rep, tabs, lp))
        if k % 2:
            h, sv = ffn_fwd(k // 2, h, mask, ws[k])
        else:
            h, sv = _FWD[k](h, mask, ws[k])
        saved.append(sv)
    loss, dh = loss_head(h, tgt)

    gs, sh = [None] * 8, [None] * 8
    early = {}

    def emit(tag, grads, carry):
        blocks = [t.reshape((N_DEV, -1) + t.shape[-1:]) if t.ndim == 2 else t for t in grads]
        early[tag], carry = push_start("scatter_start_mla_" + tag, blocks, me, scatter=True, carry=carry)
        return carry

    for k in reversed(range(1, 8)):
        if k % 2:
            dh, gs[k] = ffn_bwd(k // 2, dh, mask, ws[k], saved[k])
        else:
            dh, gs[k] = _BWD[k](dh, mask, ws[k], saved[k])
        blocks = [b.reshape((N_DEV,) + w[n].shape[1:]) for b, (n, _) in zip(_sub_grad_blocks(k, gs[k]), GROUPS[k])]
        sh[k], dh = push_start("scatter_start_" + SUBS[k], blocks, me, scatter=True, carry=dh)
        if k == 4:
            gs5 = _flat(list(tb_vjp(gs[4]['tb'])), 8 * LANE)
            rh_s5, dh = push_start("small_grads_start_s5", [gs5], me, scatter=False, carry=dh)
    dh, gs[0] = mla_bwd(dh, mask, ws[0], saved[0], emit)
    dmeta = dh[PAD:OFF].reshape(N_META, N_DEV, D // N_DEV).transpose(1, 0, 2)
    dcw = jnp.stack([gs[2 * i + 1]['cw'].reshape(N_DEV, 3, FFN_B) for i in range(4)], axis=1)
    last = push_start("scatter_start_mla", [gs[0]['wdown'].reshape(N_DEV, D // N_DEV, -1),
                                            _pack8([t.reshape(N_DEV, -1) for t in (dmeta, gs[4]['dsk'], gs[6]['gn'], dcw)], LANE)],
                      me, scatter=True)
    sh[0] = [last[0], early['wu'][0], early['wu'][1], early['wo'][0], last[1]]
    grad_x = dh[OFF:]

    g_rep = {
        'norm_mix_g': jnp.concatenate([gs[k]['ng'] for k in (0, 2, 4, 6)], axis=0),
        'norm_ffn_g': jnp.concatenate([gs[k]['ng'] for k in (1, 3, 5, 7)], axis=0),
        'mla_cq_norm_g': gs[0]['gcq'], 'mla_ckv_norm_g': gs[0]['gckv'], 'mla_q_head_g': gs[0]['gq'],
        'mla_k_head_g': gs[0]['gk'], 'hgrn_lb_logits': lb_vjp(gs[2]['lb'])[0], 'hgrn_o_norm_g': gs[2]['go'],
        'ffn_conv_b': jnp.stack([gs[k]['cb'].reshape(-1) for k in (1, 3, 5, 7)], axis=0),
    }
    loss_part = loss[0, 0:1]
    grep = _flat([g_rep[n] for n in REP_REST] + [loss_part], 8 * LANE)
    rh = push_start("small_grads_start", [grep], me, scatter=False)

    lands = {n: [None] * w[n].shape[0] for n in BIG}
    res = {}
    late = [n for n, _ in GROUPS[0]]
    for k in reversed(range(1, 8)):
        got = push_wait("scatter_wait_" + SUBS[k], sh[k], grep, scatter=True)
        for (n, l), t in zip(GROUPS[k], got):
            lands[n][l] = t
    for n in BIG:
        if n not in late:
            res[n] = adamw("adam_" + n, lands[n], w[n], args['m_' + n], args['v_' + n])
    after = res['ffn_w_up'][1]
    got = push_wait("scatter_wait_" + SUBS[0], sh[0], after, scatter=True)
    small_land = got[-1]
    (rep_land,) = push_wait("small_grads_wait", rh, after, scatter=False)
    (s5_land,) = push_wait("small_grads_wait_s5", rh_s5, after, scatter=False)
    for (n, _), t in zip(GROUPS[0], got):
        res[n] = adamw("adam_" + n, [t], w[n], args['m_' + n], args['v_' + n])

    def flat_adam(name, land, names, mult, extra=()):
        like = [w[n] for n in names]
        pad = [jnp.zeros_like(e) for e in extra]
        out = adamw(name, [land], _flat(like + pad, mult)[None], _flat([args['m_' + n] for n in names] + pad, mult)[None],
                    _flat([args['v_' + n] for n in names] + pad, mult)[None])
        for n, parts in zip(names, zip(*[_unflat(t, like) for t in out])):
            res[n] = list(parts)
        return out[0]

    flat_adam("adam_small_sharded", small_land, SMALL_SH, LANE)
    flat_adam("adam_s5_replicated", s5_land, REP_S5, 8 * LANE)
    gsum = flat_adam("adam_small_replicated", rep_land, REP_REST, 8 * LANE, extra=[loss_part])
    total = gsum.reshape(-1)[sum(w[n].size for n in REP_REST)]
    outs = [total, grad_x[None]]
    for k in range(4):
        outs += [res[n][k] for n in WEIGHTS]
    return tuple(outs)


def kernel(x, meta_tokens, norm_mix_g, norm_ffn_g, mla_w_down, mla_cq_norm_g, mla_ckv_norm_g, mla_w_uq, mla_w_ukv, mla_q_head_g, mla_k_head_g, mla_w_o, hgrn_w_in, hgrn_lb_logits, hgrn_o_norm_g, hgrn_w_o, s5_lam_re, s5_lam_im, s5_log_dt, s5_b_re, s5_b_im, s5_c_re, s5_c_im, s5_d, s5_w_glu, ret_w_in, ret_gn_g, ret_w_o, ffn_w_up, ffn_conv_w, ffn_conv_b, ffn_w_down, loss_target, m_meta_tokens, m_norm_mix_g, m_norm_ffn_g, m_mla_w_down, m_mla_cq_norm_g, m_mla_ckv_norm_g, m_mla_w_uq, m_mla_w_ukv, m_mla_q_head_g, m_mla_k_head_g, m_mla_w_o, m_hgrn_w_in, m_hgrn_lb_logits, m_hgrn_o_norm_g, m_hgrn_w_o, m_s5_lam_re, m_s5_lam_im, m_s5_log_dt, m_s5_b_re, m_s5_b_im, m_s5_c_re, m_s5_c_im, m_s5_d, m_s5_w_glu, m_ret_w_in, m_ret_gn_g, m_ret_w_o, m_ffn_w_up, m_ffn_conv_w, m_ffn_conv_b, m_ffn_w_down, v_meta_tokens, v_norm_mix_g, v_norm_ffn_g, v_mla_w_down, v_mla_cq_norm_g, v_mla_ckv_norm_g, v_mla_w_uq, v_mla_w_ukv, v_mla_q_head_g, v_mla_k_head_g, v_mla_w_o, v_hgrn_w_in, v_hgrn_lb_logits, v_hgrn_o_norm_g, v_hgrn_w_o, v_s5_lam_re, v_s5_lam_im, v_s5_log_dt, v_s5_b_re, v_s5_b_im, v_s5_c_re, v_s5_c_im, v_s5_d, v_s5_w_glu, v_ret_w_in, v_ret_gn_g, v_ret_w_o, v_ffn_w_up, v_ffn_conv_w, v_ffn_conv_b, v_ffn_w_down):
    return _step(dict(locals()))
```

```python
import functools
import math

import jax
import jax.numpy as jnp
import numpy as np
from jax import lax
from jax.experimental import pallas as pl
from jax.experimental.pallas import tpu as pltpu

F32 = jnp.float32
BF = jnp.bfloat16
SDS = jax.ShapeDtypeStruct

N_DEV = 8
D = 1024
N_META = 16
PAD = 48
OFF = PAD + N_META
CH = 64
EPS = 1e-6
NEG = -1e30
ROPE_BASE = 10000.0

MLA_H, MLA_NOPE, MLA_ROPE, MLA_V = 8, 128, 64, 128
MLA_QK = MLA_NOPE + MLA_ROPE
MLA_QL, MLA_KVL = 384, 256
HG_H, HG_D, HG_C = 8, 128, 16
S5_G, S5_P, S5_K = 64, 64, 16
S5_SG = 8
RET_H, RET_DK, RET_DV = 4, 256, 512
FFN_F = 2816
FFN_B = 704

ADAM_LR, ADAM_B1, ADAM_B2, ADAM_EPS, ADAM_WD, ADAM_STEP = 0.001, 0.9, 0.999, 1e-08, 0.01, 10

VMEM_LIMIT = 56 * 1024 * 1024
ARB = "arbitrary"


def _cp(n):
    return pltpu.CompilerParams(dimension_semantics=(ARB,) * n, vmem_limit_bytes=VMEM_LIMIT)


def _bdot(a, b, ca, cb):
    return lax.dot_general(a.astype(BF), b.astype(BF), (((ca,), (cb,)), ((), ())), preferred_element_type=F32)


@jax.custom_vjp
def mm(a, b):
    return _bdot(a, b, 1, 0)


@jax.custom_vjp
def mm_nt(a, b):
    return _bdot(a, b, 1, 1)


@jax.custom_vjp
def mm_tn(a, b):
    return _bdot(a, b, 0, 0)


mm.defvjp(lambda a, b: (mm(a, b), (a, b)),
          lambda r, g: (mm_nt(g, r[1]).astype(r[0].dtype), mm_tn(r[0], g).astype(r[1].dtype)))
mm_nt.defvjp(lambda a, b: (mm_nt(a, b), (a, b)),
             lambda r, g: (mm(g, r[1]).astype(r[0].dtype), mm_tn(g, r[0]).astype(r[1].dtype)))
mm_tn.defvjp(lambda a, b: (mm_tn(a, b), (a, b)),
             lambda r, g: (mm_nt(r[1], g).astype(r[0].dtype), mm(r[0], g).astype(r[1].dtype)))


def _xdot(a, b, ca, cb):
    return lax.dot_general(a, b, (((ca,), (cb,)), ((), ())), preferred_element_type=F32,
                           precision=lax.Precision.HIGHEST)


@jax.custom_vjp
def cright(x, r):
    return _xdot(x, r, 1, 0)


cright.defvjp(lambda x, r: (cright(x, r), r), lambda r, g: (_xdot(g, r, 1, 1), jnp.zeros_like(r)))


def _shift_raw(x, s):
    n = x.shape[0]
    r = lax.broadcasted_iota(jnp.int32, x.shape, 0)
    y = pltpu.roll(x, s % n, 0)
    return jnp.where((r >= s) & (r < n + s), y, 0.0)


@functools.partial(jax.custom_vjp, nondiff_argnums=(1,))
def shift_rows(x, s):
    return _shift_raw(x, s)


shift_rows.defvjp(lambda x, s: (_shift_raw(x, s), None), lambda s, _, g: (_shift_raw(g, -s),))


def _seg_shift_raw(x, s, seg, up):
    n = x.shape[0]
    r = lax.broadcasted_iota(jnp.int32, x.shape, 0) % seg
    if up:
        return jnp.where(r < seg - s, pltpu.roll(x, n - s, 0), 0.0)
    return jnp.where(r >= s, pltpu.roll(x, s, 0), 0.0)


@functools.partial(jax.custom_vjp, nondiff_argnums=(1, 2))
def seg_shift(x, s, seg):
    return _seg_shift_raw(x, s, seg, False)


seg_shift.defvjp(lambda x, s, seg: (_seg_shift_raw(x, s, seg, False), None),
                 lambda s, seg, _, g: (_seg_shift_raw(g, s, seg, True),))


def _seg_cumsum(x, seg):
    s = 1
    while s < seg:
        x = x + seg_shift(x, s, seg)
        s *= 2
    return x


def _rms(x, g):
    return x * lax.rsqrt(jnp.mean(x * x, axis=-1, keepdims=True) + EPS) * g


def _silu(x):
    return x * jax.nn.sigmoid(x)


def _mm_call(name, a, b, *, grid, a_spec, b_spec, o_shape, o_spec, dims, acc_shape, res=None, res_spec=None,
             mask_tm=None):
    nk = grid[2]

    def body(*refs):
        if res is None:
            a_ref, b_ref, o_ref = refs[:3]
        else:
            a_ref, b_ref, r_ref, o_ref = refs[:4]
        k = pl.program_id(2)

        def dot():
            return lax.dot_general(a_ref[...].astype(BF), b_ref[...].astype(BF), dims, preferred_element_type=F32)

        def finish(v):
            if res is not None:
                v = v + r_ref[...].astype(F32)
                rows = pl.program_id(0) * mask_tm + lax.broadcasted_iota(jnp.int32, v.shape, 0)
                v = jnp.where(rows >= PAD, v, 0.0)
            o_ref[...] = v.astype(o_ref.dtype)

        if nk == 1:
            finish(dot())
            return
        acc = refs[-1]

        @pl.when(k == 0)
        def _():
            acc[...] = dot()

        @pl.when((k > 0) & (k < nk - 1))
        def _():
            acc[...] += dot()

        @pl.when(k == nk - 1)
        def _():
            finish(acc[...] + dot())

    ins = [a, b] + ([res] if res is not None else [])
    specs = [a_spec, b_spec] + ([res_spec] if res is not None else [])
    scratch = [pltpu.VMEM(acc_shape, F32)] if nk > 1 else []
    return pl.pallas_call(body, name=name, grid=grid, in_specs=specs, out_specs=o_spec, out_shape=o_shape,
                          scratch_shapes=scratch, compiler_params=_cp(3))(*ins)


NN = (((1,), (0,)), ((), ()))
NT = (((1,), (1,)), ((), ()))
TN = (((0,), (0,)), ((), ()))


def _row_tile(lp):
    for t in (832, 640, 320, 64):
        if lp % t == 0:
            return t
    raise ValueError(lp)


def _col_tile(n):
    for t in (1024, 768, 512, 384, 256, 128):
        if n % t == 0:
            return t
    return n


def _mm_rows(m):
    return 2080 if m % 2080 == 0 else _row_tile(m)


def _mm_cols(n):
    for t in (512, 384, 256, 128):
        if n % t == 0:
            return t
    return n


def lin(name, a, w, out_dtype=F32, res=None):
    m, k = a.shape
    n = w.shape[1]
    tm, tn, tc = _mm_rows(m), _mm_cols(n), _col_tile(k)
    return _mm_call(name, a, w, grid=(m // tm, n // tn, k // tc),
                    a_spec=pl.BlockSpec((tm, tc), lambda i, j, kk: (i, kk)),
                    b_spec=pl.BlockSpec((tc, tn), lambda i, j, kk: (kk, j)),
                    o_shape=SDS((m, n), out_dtype), o_spec=pl.BlockSpec((tm, tn), lambda i, j, kk: (i, j)),
                    dims=NN, acc_shape=(tm, tn), res=res,
                    res_spec=pl.BlockSpec((tm, tn), lambda i, j, kk: (i, j)), mask_tm=tm)


def lin_bo(name, a, wb, out_dtype=F32):
    m, k = a.shape
    nb, _, n = wb.shape
    tm = _mm_rows(m)
    return _mm_call(name, a, wb, grid=(m // tm, nb, 1),
                    a_spec=pl.BlockSpec((tm, k), lambda i, j, kk: (i, 0)),
                    b_spec=pl.BlockSpec((None, k, n), lambda i, j, kk: (j, 0, 0)),
                    o_shape=SDS((nb, m, n), out_dtype), o_spec=pl.BlockSpec((None, tm, n), lambda i, j, kk: (j, i, 0)),
                    dims=NN, acc_shape=(tm, n))


def lin_bi(name, ab, wb, out_dtype=F32, res=None):
    nb, m, k = ab.shape
    n = wb.shape[2]
    tm, tn = _mm_rows(m), _mm_cols(n)
    return _mm_call(name, ab, wb, grid=(m // tm, n // tn, nb),
                    a_spec=pl.BlockSpec((None, tm, k), lambda i, j, kk: (kk, i, 0)),
                    b_spec=pl.BlockSpec((None, k, tn), lambda i, j, kk: (kk, 0, j)),
                    o_shape=SDS((m, n), out_dtype), o_spec=pl.BlockSpec((tm, tn), lambda i, j, kk: (i, j)),
                    dims=NN, acc_shape=(tm, tn), res=res,
                    res_spec=pl.BlockSpec((tm, tn), lambda i, j, kk: (i, j)), mask_tm=tm)


def lin_t(name, g, w, out_dtype=F32):
    m, n = g.shape
    k = w.shape[0]
    tm, tk, tc = _mm_rows(m), _col_tile(k), _col_tile(n)
    return _mm_call(name, g, w, grid=(m // tm, k // tk, n // tc),
                    a_spec=pl.BlockSpec((tm, tc), lambda i, j, kk: (i, kk)),
                    b_spec=pl.BlockSpec((tk, tc), lambda i, j, kk: (j, kk)),
                    o_shape=SDS((m, k), out_dtype), o_spec=pl.BlockSpec((tm, tk), lambda i, j, kk: (i, j)),
                    dims=NT, acc_shape=(tm, tk))


def lin_t_bi(name, gb, wb, out_dtype=F32):
    nb, m, n = gb.shape
    k = wb.shape[1]
    tm, tk = _mm_rows(m), _col_tile(k)
    return _mm_call(name, gb, wb, grid=(m // tm, k // tk, nb),
                    a_spec=pl.BlockSpec((None, tm, n), lambda i, j, kk: (kk, i, 0)),
                    b_spec=pl.BlockSpec((None, tk, n), lambda i, j, kk: (kk, j, 0)),
                    o_shape=SDS((m, k), out_dtype), o_spec=pl.BlockSpec((tm, tk), lambda i, j, kk: (i, j)),
                    dims=NT, acc_shape=(tm, tk))


def lin_t_bo(name, g, wb, out_dtype=F32):
    m, n = g.shape
    nb, k, _ = wb.shape
    tm = _mm_rows(m)
    return _mm_call(name, g, wb, grid=(m // tm, nb, 1),
                    a_spec=pl.BlockSpec((tm, n), lambda i, j, kk: (i, 0)),
                    b_spec=pl.BlockSpec((None, k, n), lambda i, j, kk: (j, 0, 0)),
                    o_shape=SDS((nb, m, k), out_dtype), o_spec=pl.BlockSpec((None, tm, k), lambda i, j, kk: (j, i, 0)),
                    dims=NT, acc_shape=(tm, k))


def wgrad(name, a, g):
    m, k = a.shape
    n = g.shape[1]
    tm, tn = _mm_rows(m), (_col_tile(n) if k <= 1024 else _mm_cols(n))
    return _mm_call(name, a, g, grid=(1, n // tn, m // tm),
                    a_spec=pl.BlockSpec((tm, k), lambda i, j, kk: (kk, 0)),
                    b_spec=pl.BlockSpec((tm, tn), lambda i, j, kk: (kk, j)),
                    o_shape=SDS((k, n), F32), o_spec=pl.BlockSpec((k, tn), lambda i, j, kk: (0, j)),
                    dims=TN, acc_shape=(k, tn))


def wgrad_bo(name, a, gb):
    m, k = a.shape
    nb, _, n = gb.shape
    tm = _mm_rows(m)
    return _mm_call(name, a, gb, grid=(nb, 1, m // tm),
                    a_spec=pl.BlockSpec((tm, k), lambda i, j, kk: (kk, 0)),
                    b_spec=pl.BlockSpec((None, tm, n), lambda i, j, kk: (i, kk, 0)),
                    o_shape=SDS((nb, k, n), F32), o_spec=pl.BlockSpec((None, k, n), lambda i, j, kk: (i, 0, 0)),
                    dims=TN, acc_shape=(k, n))


def wgrad_bi(name, zb, g):
    nb, m, k = zb.shape
    n = g.shape[1]
    tm, tn = _mm_rows(m), _col_tile(n)
    return _mm_call(name, zb, g, grid=(nb, n // tn, m // tm),
                    a_spec=pl.BlockSpec((None, tm, k), lambda i, j, kk: (i, kk, 0)),
                    b_spec=pl.BlockSpec((tm, tn), lambda i, j, kk: (kk, j)),
                    o_shape=SDS((nb, k, n), F32), o_spec=pl.BlockSpec((None, k, tn), lambda i, j, kk: (i, 0, j)),
                    dims=TN, acc_shape=(k, tn))


class Arg:
    def __init__(self, arr, block, imap, shared=False, acc=False):
        self.arr, self.block, self.imap = arr, block, imap
        self.shared = shared
        self.acc = acc

    @property
    def spec(self):
        return pl.BlockSpec(self.block, self.imap)

    def vshape(self):
        return tuple(b for b in self.block if b is not None)


def _rev(arg, nt):
    return pl.BlockSpec(arg.block, lambda o, t, _f=arg.imap: _f(o, nt - 1 - t))


def seq_fwd(name, fn, grid, params, consts, xs, outs, carries=(), save_dtype=F32):
    no, nt = grid
    n_p, n_c, n_x, n_y, n_k = len(params), len(consts), len(xs), len(outs), len(carries)

    def body(*refs):
        p_refs = refs[:n_p]
        c_refs = refs[n_p:n_p + n_c]
        x_refs = refs[n_p + n_c:n_p + n_c + n_x]
        r = n_p + n_c + n_x
        y_refs = refs[r:r + n_y]
        s_refs = refs[r + n_y:r + n_y + n_k]
        k_refs = refs[r + n_y + n_k:]
        t = pl.program_id(1)

        if n_k:
            @pl.when(t == 0)
            def _():
                for k in k_refs:
                    k[...] = jnp.zeros_like(k)

        carry = tuple(k[...] for k in k_refs)
        for s, c in zip(s_refs, carry):
            s[...] = c.astype(s.dtype)
        new_carry, ys = fn(tuple(p[...] for p in p_refs), tuple(c[...] for c in c_refs), carry,
                           tuple(x[...] for x in x_refs))
        for k, c in zip(k_refs, new_carry):
            k[...] = c
        for y_ref, y in zip(y_refs, ys):
            y_ref[...] = y.astype(y_ref.dtype)

    out_shape = [SDS(s, d) for (s, d, _, _) in outs]
    out_specs = [pl.BlockSpec(b, im) for (_, _, b, im) in outs]
    for cs in carries:
        out_shape.append(SDS((no, nt) + cs, save_dtype))
        out_specs.append(pl.BlockSpec((None, None) + cs, lambda o, t, _n=len(cs): (o, t) + (0,) * _n))
    res = pl.pallas_call(
        body, name=name, grid=grid, in_specs=[a.spec for a in list(params) + list(consts) + list(xs)],
        out_specs=out_specs, out_shape=out_shape, scratch_shapes=[pltpu.VMEM(cs, F32) for cs in carries],
        compiler_params=_cp(2))(*[a.arr for a in list(params) + list(consts) + list(xs)])
    return res[:n_y], res[n_y:]


def seq_bwd(name, fn, grid, params, consts, xs, dys, saved=(), carries=()):
    no, nt = grid
    n_p, n_c, n_x, n_y, n_k = len(params), len(consts), len(xs), len(dys), len(carries)

    def body(*refs):
        p_refs = refs[:n_p]
        c_refs = refs[n_p:n_p + n_c]
        x_refs = refs[n_p + n_c:n_p + n_c + n_x]
        r = n_p + n_c + n_x
        g_refs = refs[r:r + n_y]
        s_refs = refs[r + n_y:r + n_y + n_k]
        r = r + n_y + n_k
        dx_refs = refs[r:r + n_x]
        dp_refs = refs[r + n_x:r + n_x + n_p]
        k_refs = refs[r + n_x + n_p:]
        o = pl.program_id(0)
        t = pl.program_id(1)

        if n_k:
            @pl.when(t == 0)
            def _():
                for k in k_refs:
                    k[...] = jnp.zeros_like(k)

        for a, dp in zip(params, dp_refs):
            @pl.when((t == 0) & (o == 0) if a.shared else (t == 0))
            def _(dp=dp):
                dp[...] = jnp.zeros_like(dp)

        for a, dx in zip(xs, dx_refs):
            if a.acc:
                @pl.when(t == 0)
                def _(dx=dx):
                    dx[...] = jnp.zeros_like(dx)

        consts_v = tuple(c[...] for c in c_refs)

        def f(pv, cv, xv):
            return fn(pv, consts_v, cv, xv)

        pv = tuple(p[...] for p in p_refs)
        cv = tuple(s[...].astype(F32) for s in s_refs)
        xv = tuple(x[...] for x in x_refs)
        (new_carry, ys), vjp = jax.vjp(f, pv, cv, xv)
        cot = (tuple(k[...] for k in k_refs), tuple(g[...].astype(y.dtype) for g, y in zip(g_refs, ys)))
        dpv, dcv, dxv = vjp(cot)
        for k, c in zip(k_refs, dcv):
            k[...] = c
        for dp, v in zip(dp_refs, dpv):
            dp[...] += v
        for a, dx, v in zip(xs, dx_refs, dxv):
            if a.acc:
                dx[...] += v
            else:
                dx[...] = v.astype(dx.dtype)

    in_specs = ([_rev(a, nt) for a in list(params) + list(consts) + list(xs) + list(dys)]
                + [pl.BlockSpec((None, None) + cs, lambda o, t, _n=len(cs): (o, nt - 1 - t) + (0,) * _n) for cs in carries])
    out_shape = [SDS(a.arr.shape, F32) for a in xs] + [SDS(a.arr.shape, F32) for a in params]
    out_specs = [_rev(a, nt) for a in list(xs) + list(params)]
    res = pl.pallas_call(
        body, name=name, grid=grid, in_specs=in_specs, out_specs=out_specs, out_shape=out_shape,
        scratch_shapes=[pltpu.VMEM(cs, F32) for cs in carries], compiler_params=_cp(2))(
            *[a.arr for a in list(params) + list(consts) + list(xs) + list(dys)], *saved)
    return res[:n_x], res[n_x:]


def _rowmask(lp):
    return (jnp.arange(lp) >= PAD).astype(F32)[:, None]


def _norm_fn(p, c, k, x):
    return (), (_rms(x[0] * c[0], p[0]),)


def _norm_b_fn(p, c, k, x):
    h = x[0] * c[0]
    return (), (_rms(h, p[0]), h)


def norm_fwd(name, h, g, mask, out_dtype=BF):
    lp, d = h.shape
    tr = _row_tile(lp)
    row = lambda o, t: (t, 0)
    (a,), _ = seq_fwd(name, _norm_fn, (1, lp // tr), [Arg(g, (1, d), lambda o, t: (0, 0), shared=True)],
                      [Arg(mask, (tr, 1), row)], [Arg(h, (tr, d), row)], [((lp, d), out_dtype, (tr, d), row)])
    return a


def norm_bwd(name, h, g, mask, da, dskip):
    lp, d = h.shape
    tr = _row_tile(lp)
    row = lambda o, t: (t, 0)
    (dh,), (dg,) = seq_bwd(name, _norm_b_fn, (1, lp // tr), [Arg(g, (1, d), lambda o, t: (0, 0), shared=True)],
                           [Arg(mask, (tr, 1), row)], [Arg(h, (tr, d), row)],
                           [Arg(da, (tr, d), row), Arg(dskip, (tr, d), row)])
    return dh, dg


def _ffn_tile(lp):
    return 320 if (lp % 320 == 0 and lp > 320) else 64


def _conv_rows(ext, w, b, n):
    u2 = ext[8:8 + n]
    u1 = pltpu.roll(ext, 1, 0)[8:8 + n]
    u0 = pltpu.roll(ext, 2, 0)[8:8 + n]
    return w[2] * u2 + w[1] * u1 + w[0] * u0 + b, (u0, u1, u2)


def ffn_up_core(name, a, wup, cw, cb):
    lp, kd = a.shape
    _, nj, _, fb = wup.shape
    tr = _ffn_tile(lp)
    nt = lp // tr

    def body(a_ref, wu_ref, w_ref, b_ref, u_ref, z_ref, u_s, halo_s):
        i = pl.program_id(1)

        @pl.when(i == 0)
        def _():
            u_s[...] = jnp.zeros_like(u_s)
            halo_s[...] = jnp.zeros_like(halo_s)

        old = (i + 1) % 2
        cs = []
        for s in range(2):
            tile = u_s[old, s]
            ext = jnp.concatenate([halo_s[s], tile], axis=0)
            c, _ = _conv_rows(ext, w_ref[s], b_ref[s], tr)
            cs.append(c)
            halo_s[s] = tile[tr - 8:]
        z_ref[...] = (_silu(cs[0]) * cs[1]).astype(z_ref.dtype)
        for s in range(2):
            un = lax.dot_general(a_ref[...], wu_ref[s], NN, preferred_element_type=F32)
            u_ref[s] = un
            u_s[i % 2, s] = un

    cur = lambda j, i: (0, j, jnp.minimum(i, nt - 1), 0)
    return pl.pallas_call(
        body, name=name, grid=(nj, nt + 1),
        in_specs=[pl.BlockSpec((tr, kd), lambda j, i: (jnp.minimum(i, nt - 1), 0)),
                  pl.BlockSpec((2, None, kd, fb), lambda j, i: (0, j, 0, 0)),
                  pl.BlockSpec((2, None, 3, 1, fb), lambda j, i: (0, j, 0, 0, 0)),
                  pl.BlockSpec((2, None, 1, fb), lambda j, i: (0, j, 0, 0))],
        out_specs=[pl.BlockSpec((2, None, tr, fb), cur),
                   pl.BlockSpec((None, tr, fb), lambda j, i: (j, jnp.maximum(i - 1, 0), 0))],
        out_shape=[SDS((2, nj, lp, fb), F32), SDS((nj, lp, fb), BF)],
        scratch_shapes=[pltpu.VMEM((2, 2, tr, fb), F32), pltpu.VMEM((2, 8, fb), F32)],
        compiler_params=_cp(2))(a, wup, cw, cb)


def ffn_core_bwd(name, u, dz, cw, cb, a):
    _, nj, lp, fb = u.shape
    kd = a.shape[1]
    tr = _ffn_tile(lp)
    nt = lp // tr
    nb8 = lp // 8

    def body(u_ref, up_ref, un_ref, dz_ref, dzn_ref, w_ref, b_ref, a_ref, du_ref, dw_ref, db_ref, dwup_ref, du_s):
        i = pl.program_id(1)

        @pl.when(i == 0)
        def _():
            dw_ref[...] = jnp.zeros_like(dw_ref)
            db_ref[...] = jnp.zeros_like(db_ref)
            dwup_ref[...] = jnp.zeros_like(dwup_ref)
            du_s[...] = jnp.zeros_like(du_s)

        old = (i + 1) % 2
        for s in range(2):
            dwup_ref[s] += lax.dot_general(a_ref[...], du_s[old, s], TN, preferred_element_type=F32)

        it = jnp.minimum(i, nt - 1)
        live = (i < nt).astype(F32)
        prev = jnp.where(it > 0, up_ref[...], 0.0)
        nxt = jnp.where(it < nt - 1, un_ref[...], 0.0)
        dz_e = jnp.concatenate([dz_ref[...], jnp.where(it < nt - 1, dzn_ref[...], 0.0)], axis=0)
        n = tr + 8
        cs, taps = [], []
        for s in range(2):
            ext = jnp.concatenate([prev[s], u_ref[s], nxt[s]], axis=0)
            c, tp = _conv_rows(ext, w_ref[s], b_ref[s], n)
            cs.append(c)
            taps.append(tp)
        sg = jax.nn.sigmoid(cs[0])
        dcs = [dz_e * cs[1] * sg * (1.0 + cs[0] * (1.0 - sg)), dz_e * cs[0] * sg]
        for s in range(2):
            dc = dcs[s]
            w = w_ref[s]
            d1 = pltpu.roll(dc, n - 1, 0)[:tr]
            d2 = pltpu.roll(dc, n - 2, 0)[:tr]
            dcm = dc[:tr]
            du = w[2] * dcm + w[1] * d1 + w[0] * d2
            du_ref[s] = du
            du_s[i % 2, s] = (du * live).astype(BF)
            for k in range(3):
                dw_ref[s, k] += live * jnp.sum(dcm * taps[s][k][:tr], axis=0, keepdims=True)
            db_ref[s] += live * jnp.sum(dcm, axis=0, keepdims=True)

    row = lambda j, i: (0, j, jnp.minimum(i, nt - 1), 0)
    return pl.pallas_call(
        body, name=name, grid=(nj, nt + 1),
        in_specs=[pl.BlockSpec((2, None, tr, fb), row),
                  pl.BlockSpec((2, None, 8, fb), lambda j, i: (0, j, jnp.maximum(jnp.minimum(i, nt - 1) * (tr // 8) - 1, 0), 0)),
                  pl.BlockSpec((2, None, 8, fb), lambda j, i: (0, j, jnp.minimum((i + 1) * (tr // 8), nb8 - 1), 0)),
                  pl.BlockSpec((None, tr, fb), lambda j, i: (j, jnp.minimum(i, nt - 1), 0)),
                  pl.BlockSpec((None, 8, fb), lambda j, i: (j, jnp.minimum((i + 1) * (tr // 8), nb8 - 1), 0)),
                  pl.BlockSpec((2, None, 3, 1, fb), lambda j, i: (0, j, 0, 0, 0)),
                  pl.BlockSpec((2, None, 1, fb), lambda j, i: (0, j, 0, 0)),
                  pl.BlockSpec((tr, kd), lambda j, i: (jnp.maximum(i - 1, 0), 0))],
        out_specs=[pl.BlockSpec((2, None, tr, fb), row),
                   pl.BlockSpec((2, None, 3, 1, fb), lambda j, i: (0, j, 0, 0, 0)),
                   pl.BlockSpec((2, None, 1, fb), lambda j, i: (0, j, 0, 0)),
                   pl.BlockSpec((2, None, kd, fb), lambda j, i: (0, j, 0, 0))],
        out_shape=[SDS(u.shape, F32), SDS(cw.shape, F32), SDS(cb.shape, F32), SDS((2, nj, kd, fb), F32)],
        scratch_shapes=[pltpu.VMEM((2, 2, tr, fb), BF)],
        compiler_params=_cp(2))(u, u, u, dz, dz, cw, cb, a)


def ffn_fwd(i, h, mask, w):
    a = norm_fwd(f"ffn{i}_norm", h, w["ng"], mask)
    u, z = ffn_up_core(f"ffn{i}_up_core", a, w["up"].reshape(2, 4, D, FFN_B), w["cw"], w["cb"])
    h2 = lin_bi(f"ffn{i}_down", z, w["down"], res=h)
    return h2, (h, a, u, z)


def ffn_bwd(i, dh2, mask, w, saved):
    h, a, u, z = saved
    lp = h.shape[0]
    g = {}
    g["down"] = wgrad_bi(f"ffn{i}_dwdown", z, dh2)
    dz = lin_t_bo(f"ffn{i}_dz", dh2, w["down"])
    du, g["cw"], g["cb"], dwup = ffn_core_bwd(f"ffn{i}_core_b", u, dz, w["cw"], w["cb"], a)
    du = du.reshape(8, lp, FFN_B)
    g["up"] = dwup.reshape(8, D, FFN_B)
    da = lin_t_bi(f"ffn{i}_da", du, w["up"])
    dh, g["ng"] = norm_bwd(f"ffn{i}_norm_b", h, w["ng"], mask, da, dh2)
    return dh, g


HG_HB = 4


def _hgrn_fn(p, c, k, x):
    lb, go = p
    q, f, iv, g = x[0][0], x[0][1], x[0][2], x[0][3]
    (st_all,) = k
    qs = _silu(q)
    forget = lb + (1.0 - lb) * jax.nn.sigmoid(f)
    logf = jnp.log(forget)
    kk = 1.0 - forget
    gc_all = _seg_cumsum(logf, HG_C)
    r = lax.broadcasted_iota(jnp.int32, (HG_C, HG_C), 0)
    cc = lax.broadcasted_iota(jnp.int32, (HG_C, HG_C), 1)
    ns = CH // HG_C
    cells = [(j, s) for j in range(HG_HB) for s in range(ns)]

    def blk(t, j, s):
        return t[HG_C * s:HG_C * (s + 1), HG_D * j:HG_D * (j + 1)]

    gl = {c: jnp.sum(blk(logf, *c), axis=0, keepdims=True) for c in cells}
    qd = {c: blk(qs, *c) * jnp.exp(blk(gc_all, *c)) for c in cells}
    ki = {c: blk(kk, *c) * jnp.exp(-blk(gc_all, *c)) for c in cells}
    up = {c: mm_tn(blk(iv, *c), blk(kk, *c) * jnp.exp(gl[c] - blk(gc_all, *c))) for c in cells}
    st, sts = {}, []
    for j in range(HG_HB):
        cur = st_all[j]
        for s in range(ns):
            st[(j, s)] = cur
            cur = cur * jnp.exp(gl[(j, s)]) + up[(j, s)]
        sts.append(cur)
    both = {c: mm_nt(qd[c], jnp.concatenate([st[c], ki[c]], axis=0)) for c in cells}
    oc = {c: mm(jnp.where(r >= cc, both[c][:, HG_D:], 0.0), blk(iv, *c)) + both[c][:, :HG_D] for c in cells}
    zs = []
    for j in range(HG_HB):
        o = jnp.concatenate([oc[(j, s)] for s in range(ns)], axis=0)
        zs.append(_rms(o, go) * _silu(g[:, HG_D * j:HG_D * (j + 1)]))
    return (jnp.stack(sts, axis=0),), (jnp.concatenate(zs, axis=1),)


def _hgrn_args(u4, lb, go):
    lp = u4.shape[2]
    wb = HG_HB * HG_D
    xs = [Arg(u4, (4, None, CH, wb), lambda o, t: (0, o, t, 0))]
    ps = [Arg(lb, (1, wb), lambda o, t: (0, o)), Arg(go, (1, HG_D), lambda o, t: (0, 0), shared=True)]
    return (HG_H // HG_HB, lp // CH), ps, xs


def hgrn_fwd(h, mask, w):
    lp = h.shape[0]
    a = norm_fwd("hgrn_norm", h, w["ng"], mask)
    u4 = lin_bo("hgrn_in", a, w["win"]).reshape(4, HG_H // HG_HB, lp, HG_HB * HG_D)
    grid, ps, xs = _hgrn_args(u4, w["lb"], w["go"])
    (z,), (st,) = seq_fwd("hgrn_core", _hgrn_fn, grid, ps, [], xs,
                          [((lp, D), BF, (CH, HG_HB * HG_D), lambda o, t: (t, o))], carries=[(HG_HB, HG_D, HG_D)])
    h2 = lin("hgrn_out", z, w["wo"], res=h)
    return h2, (h, a, u4, z, st)


def hgrn_bwd(dh2, mask, w, saved):
    h, a, u4, z, st = saved
    lp = h.shape[0]
    g = {}
    g["wo"] = wgrad("hgrn_dwo", z, dh2)
    dz = lin_t("hgrn_dz", dh2, w["wo"])
    grid, ps, xs = _hgrn_args(u4, w["lb"], w["go"])
    (du4,), (g["lb"], g["go"]) = seq_bwd("hgrn_core_b", _hgrn_fn, grid, ps, [], xs,
                                         [Arg(dz, (CH, HG_HB * HG_D), lambda o, t: (t, o))], saved=[st],
                                         carries=[(HG_HB, HG_D, HG_D)])
    du = du4.reshape(N_DEV, lp, HG_HB * HG_D)
    g["win"] = wgrad_bo("hgrn_dwin", a, du)
    da = lin_t_bi("hgrn_da", du, w["win"])
    dh, g["ng"] = norm_bwd("hgrn_norm_b", h, w["ng"], mask, da, dh2)
    return dh, g


S5_W = S5_SG * S5_P


def s5_tables(lam_re, lam_im, log_dt, b_re, b_im, c_re, c_im):
    dt = jnp.exp(log_dt)[:, None]
    mag = jnp.exp(lam_re * dt)
    abar_re = mag * jnp.cos(lam_im * dt)
    abar_im = mag * jnp.sin(lam_im * dt)
    den = lam_re * lam_re + lam_im * lam_im
    zoh_re = ((abar_re - 1.0) * lam_re + abar_im * lam_im) / den
    zoh_im = (abar_im * lam_re - (abar_re - 1.0) * lam_im) / den
    bbar_re = zoh_re[..., None] * b_re - zoh_im[..., None] * b_im
    bbar_im = zoh_re[..., None] * b_im + zoh_im[..., None] * b_re
    eye = jnp.eye(S5_SG, dtype=F32)

    def blockdiag_in(b):
        t = b.reshape(N_DEV, S5_SG, S5_P, S5_K).transpose(0, 1, 3, 2)
        return jnp.einsum("jakp,ab->jakbp", t, eye).reshape(N_DEV, S5_SG * S5_K, S5_W)

    def blockdiag_out(c):
        t = c.reshape(N_DEV, S5_SG, S5_K, S5_P).transpose(0, 1, 3, 2)
        return jnp.einsum("japk,ab->japbk", t, eye).reshape(N_DEV, S5_W, S5_SG * S5_K)

    wb = jnp.concatenate([blockdiag_in(bbar_re), blockdiag_in(bbar_im)], axis=2)
    wc = jnp.concatenate([blockdiag_out(c_re), -blockdiag_out(c_im)], axis=1)

    def powers(n):
        steps = n[:, None, None] * dt[None]
        pm = jnp.exp(lam_re[None] * steps)
        pr = (pm * jnp.cos(lam_im[None] * steps)).reshape(-1, N_DEV, S5_W).transpose(1, 0, 2)
        pi = (pm * jnp.sin(lam_im[None] * steps)).reshape(-1, N_DEV, S5_W).transpose(1, 0, 2)
        return jnp.concatenate([pr, pi], axis=2)

    apow = powers(2.0 ** jnp.arange(6, dtype=F32))[:, :, None, :]
    ptab = powers(jnp.arange(CH, dtype=F32) + 1.0)
    return wb, wc, apow, ptab


def _cmul(ar, ai, xr, xi):
    return ar * xr - ai * xi, ar * xi + ai * xr


S5_BB = 8


def _s5_fn(p, c, k, x):
    wb, wc, apow, ptab, dsk = p
    (a,) = x
    (x0,) = k
    blocks = range(S5_BB)
    aj = [a[:, 128 * j:128 * (j + 1)] for j in blocks]
    bu = [mm(aj[j], wb[j]) for j in blocks]
    xxs, x0n = [], []
    for j in blocks:
        xr, xi = bu[j][:, :S5_W], bu[j][:, S5_W:]
        for s in range(6):
            asr, asi = apow[j][s][:, :S5_W], apow[j][s][:, S5_W:]
            dr, di = _cmul(asr, asi, shift_rows(xr, 1 << s), shift_rows(xi, 1 << s))
            xr, xi = xr + dr, xi + di
        dr, di = _cmul(ptab[j][:, :S5_W], ptab[j][:, S5_W:], x0[j][:, :S5_W], x0[j][:, S5_W:])
        xx = jnp.concatenate([xr + dr, xi + di], axis=1)
        last = lax.broadcasted_iota(jnp.int32, xx.shape, 0) == CH - 1
        x0n.append(jnp.sum(jnp.where(last, xx, 0.0), axis=0, keepdims=True))
        xxs.append(xx)
    y = jnp.concatenate([mm(xxs[j], wc[j]) for j in blocks], axis=1)
    return (jnp.stack(x0n, axis=0),), (jax.nn.gelu(y + dsk * a),)


def _s5_args(a, tb, dsk):
    lp = a.shape[0]
    wb, wc, apow, ptab = tb
    ps = [Arg(wb, (S5_BB, 128, 2 * S5_W), lambda o, t: (o, 0, 0)), Arg(wc, (S5_BB, 2 * S5_W, 128), lambda o, t: (o, 0, 0)),
          Arg(apow, (S5_BB, 6, 1, 2 * S5_W), lambda o, t: (o, 0, 0, 0)),
          Arg(ptab, (S5_BB, CH, 2 * S5_W), lambda o, t: (o, 0, 0)), Arg(dsk, (1, 128 * S5_BB), lambda o, t: (0, o))]
    xs = [Arg(a, (CH, 128 * S5_BB), lambda o, t: (t, o))]
    return (N_DEV // S5_BB, lp // CH), ps, xs


def _glu_res_fn(p, c, k, x):
    h, vg = x
    return (), ((h + vg[:, :D] * jax.nn.sigmoid(vg[:, D:])) * c[0],)


def _glu_args(h, vg, mask):
    lp = h.shape[0]
    tr = _row_tile(lp)
    row = lambda o, t: (t, 0)
    return (1, lp // tr), [Arg(mask, (tr, 1), row)], [Arg(h, (tr, D), row), Arg(vg, (tr, 2 * D), row)], tr


def s5_fwd(h, mask, w):
    lp = h.shape[0]
    a = norm_fwd("s5_norm", h, w["ng"], mask, out_dtype=F32)
    grid, ps, xs = _s5_args(a, w["tb"], w["dsk"])
    (z,), (st,) = seq_fwd("s5_core", _s5_fn, grid, ps, [], xs,
                          [((lp, D), BF, (CH, 128 * S5_BB), lambda o, t: (t, o))], carries=[(S5_BB, 1, 2 * S5_W)])
    vg = lin("s5_glu", z, w["wglu"])
    grid2, cs, xs2, tr = _glu_args(h, vg, mask)
    (h2,), _ = seq_fwd("s5_res", _glu_res_fn, grid2, [], cs, xs2, [((lp, D), F32, (tr, D), lambda o, t: (t, 0))])
    return h2, (h, a, z, vg, st)


def s5_bwd(dh2, mask, w, saved):
    h, a, z, vg, st = saved
    g = {}
    grid2, cs, xs2, tr = _glu_args(h, vg, mask)
    (dskip, dvg), _ = seq_bwd("s5_res_b", _glu_res_fn, grid2, [], cs, xs2, [Arg(dh2, (tr, D), lambda o, t: (t, 0))])
    g["wglu"] = wgrad("s5_dwglu", z, dvg)
    dz = lin_t("s5_dz", dvg, w["wglu"])
    grid, ps, xs = _s5_args(a, w["tb"], w["dsk"])
    (da,), dps = seq_bwd("s5_core_b", _s5_fn, grid, ps, [], xs, [Arg(dz, (CH, 128 * S5_BB), lambda o, t: (t, o))],
                         saved=[st], carries=[(S5_BB, 1, 2 * S5_W)])
    g["tb"] = tuple(dps[:4])
    g["dsk"] = dps[4]
    dh, g["ng"] = norm_bwd("s5_norm_b", h, w["ng"], mask, da, dskip)
    return dh, g


def _rope_angles(lp, dim):
    pos = np.maximum(np.arange(lp, dtype=np.float32) - PAD, 0.0).astype(np.float32)
    inv = (1.0 / (ROPE_BASE ** (np.arange(0, dim, 2, dtype=np.float32) / dim))).astype(np.float32)
    return (pos[:, None] * inv[None, :]).astype(np.float32)


def ret_consts(lp):
    f = np.float32
    ang = _rope_angles(lp, RET_DK)
    lg = np.log(1.0 - np.exp2(-5.0 - np.arange(RET_H, dtype=f))).astype(f)
    p = np.arange(CH, dtype=f)
    diff = p[:, None] - p[None, :]
    decay = np.where(diff >= 0, np.exp(diff[None] * lg[:, None, None]), 0.0).astype(f)
    qd = np.exp((p[None, :] + 1.0) * lg[:, None])[..., None].astype(f)
    kd = np.exp((CH - 1.0 - p[None, :]) * lg[:, None])[..., None].astype(f)
    cd = np.exp(CH * lg)[:, None, None].astype(f)
    return np.cos(ang).astype(f), np.sin(ang).astype(f), decay, qd, kd, cd


def _ret_fn(p, c, k, x):
    (gn,) = p
    cos, sin, decay, qd, kd, cd = c
    (st,) = k
    (u,) = x
    hd = RET_DK // 2
    qk_w = RET_H * RET_DK
    heads = range(RET_H)

    def rope(t):
        t1, t2 = t[:, :hd], t[:, hd:]
        return jnp.concatenate([t1 * cos - t2 * sin, t1 * sin + t2 * cos], axis=1)

    qr = [rope(u[:, RET_DK * h:RET_DK * (h + 1)]) for h in heads]
    kr = [rope(u[:, qk_w + RET_DK * h:qk_w + RET_DK * (h + 1)]) * (RET_DK ** -0.5) for h in heads]
    v = [u[:, 2 * qk_w + RET_DV * h:2 * qk_w + RET_DV * (h + 1)] for h in heads]
    scores = [mm_nt(qr[h], kr[h]) for h in heads]
    inter = [mm(qr[h] * qd[h], st[h]) for h in heads]
    st_new = jnp.stack([st[h] * cd[h] + mm_tn(kr[h] * kd[h], v[h]) for h in heads], axis=0)
    o = [mm(scores[h] * decay[h], v[h]) + inter[h] for h in heads]
    zs = []
    for h in heads:
        mu = jnp.mean(o[h], axis=-1, keepdims=True)
        var = jnp.mean(jnp.square(o[h] - mu), axis=-1, keepdims=True)
        gate = u[:, 2 * qk_w + RET_H * RET_DV + RET_DV * h:2 * qk_w + RET_H * RET_DV + RET_DV * (h + 1)]
        zs.append((o[h] - mu) * lax.rsqrt(var + EPS) * gn[:, RET_DV * h:RET_DV * (h + 1)] * _silu(gate))
    return (st_new,), (jnp.concatenate(zs, axis=1),)


def _ret_args(u, gn, rc):
    lp, uw = u.shape
    cos, sin, decay, qd, kd, cd = rc
    full = lambda o, t: (0, 0, 0)
    ps = [Arg(gn, (1, RET_H * RET_DV), lambda o, t: (0, 0))]
    cs = [Arg(cos, (CH, RET_DK // 2), lambda o, t: (t, 0)), Arg(sin, (CH, RET_DK // 2), lambda o, t: (t, 0)),
          Arg(decay, (RET_H, CH, CH), full), Arg(qd, (RET_H, CH, 1), full), Arg(kd, (RET_H, CH, 1), full),
          Arg(cd, (RET_H, 1, 1), full)]
    xs = [Arg(u, (CH, uw), lambda o, t: (t, 0))]
    return (1, lp // CH), ps, cs, xs


def ret_fwd(h, mask, w):
    lp = h.shape[0]
    a = norm_fwd("ret_norm", h, w["ng"], mask)
    u = lin("ret_in", a, w["win"])
    grid, ps, cs, xs = _ret_args(u, w["gn"], w["rc"])
    (z,), (st,) = seq_fwd("ret_core", _ret_fn, grid, ps, cs, xs,
                          [((lp, 2 * D), BF, (CH, RET_H * RET_DV), lambda o, t: (t, 0))],
                          carries=[(RET_H, RET_DK, RET_DV)], save_dtype=BF)
    h2 = lin("ret_out", z, w["wo"], res=h)
    return h2, (h, a, u, z, st)


def ret_bwd(dh2, mask, w, saved):
    h, a, u, z, st = saved
    g = {}
    g["wo"] = wgrad("ret_dwo", z, dh2)
    dz = lin_t("ret_dz", dh2, w["wo"])
    grid, ps, cs, xs = _ret_args(u, w["gn"], w["rc"])
    (du,), (g["gn"],) = seq_bwd("ret_core_b", _ret_fn, grid, ps, cs, xs,
                                [Arg(dz, (CH, RET_H * RET_DV), lambda o, t: (t, 0))], saved=[st],
                                carries=[(RET_H, RET_DK, RET_DV)])
    g["win"] = wgrad("ret_dwin", a, du)
    da = lin_t("ret_da", du, w["win"])
    dh, g["ng"] = norm_bwd("ret_norm_b", h, w["ng"], mask, da, dh2)
    return dh, g


def mla_consts(lp):
    ang = _rope_angles(lp, MLA_ROPE)
    cos = np.concatenate([np.cos(ang), np.cos(ang)], axis=1).astype(np.float32)
    sin = np.concatenate([np.sin(ang), np.sin(ang)], axis=1).astype(np.float32)
    hd = MLA_ROPE // 2
    i = np.arange(hd)
    rot = np.zeros((MLA_ROPE, MLA_ROPE), np.float32)
    rot[hd + i, i] = -1.0
    rot[i, hd + i] = 1.0
    return cos, sin, rot


def _mla_prep1_fn(p, c, k, x):
    gq, gkv = p
    (down,) = x
    return (), (_rms(down[:, :MLA_QL], gq), _rms(down[:, MLA_QL:MLA_QL + MLA_KVL], gkv), down[:, MLA_QL + MLA_KVL:])


def _mla_prep2(p, c, x):
    gq, gk = p
    cos, sin, rot = c
    q, kv, kpe = x
    qn = _rms(q, gq)
    qn_n, qn_r = qn[:, :MLA_NOPE], qn[:, MLA_NOPE:]
    qo = jnp.concatenate([qn_n, qn_r * cos + cright(qn_r, rot) * sin], axis=1)
    kn = kv[:, :MLA_NOPE]
    ms = (jnp.sum(kn * kn, axis=-1, keepdims=True) + jnp.sum(kpe * kpe, axis=-1, keepdims=True)) / MLA_QK
    r = lax.rsqrt(ms + EPS)
    kr = kpe * r * gk[:, MLA_NOPE:]
    ko = jnp.concatenate([kn * r * gk[:, :MLA_NOPE], kr * cos + cright(kr, rot) * sin], axis=1)
    return qo, ko, kv[:, MLA_NOPE:]


def _mla_prep2_fn(p, c, k, x):
    return (), _mla_prep2(p, c, x)[:2]


def _mla_prep2_b_fn(p, c, k, x):
    return (), _mla_prep2(p, c, x)


def _prep1_args(down, gq, gkv):
    lp = down.shape[0]
    tr = _row_tile(lp)
    ps = [Arg(gq, (1, MLA_QL), lambda o, t: (0, 0), shared=True), Arg(gkv, (1, MLA_KVL), lambda o, t: (0, 0), shared=True)]
    return (1, lp // tr), ps, [Arg(down, (tr, down.shape[1]), lambda o, t: (t, 0))], tr


def _prep2_args(qraw, kvraw, kpe, gq, gk, mc):
    lp = kpe.shape[0]
    tr = _row_tile(lp)
    cos, sin, rot = mc
    ps = [Arg(gq, (1, MLA_QK), lambda o, t: (0, 0), shared=True), Arg(gk, (1, MLA_QK), lambda o, t: (0, 0), shared=True)]
    cs = [Arg(cos, (tr, MLA_ROPE), lambda o, t: (o, 0)), Arg(sin, (tr, MLA_ROPE), lambda o, t: (o, 0)),
          Arg(rot, (MLA_ROPE, MLA_ROPE), lambda o, t: (0, 0))]
    xs = [Arg(qraw, (None, tr, MLA_QK), lambda o, t: (t, o, 0)), Arg(kvraw, (None, tr, MLA_NOPE + MLA_V), lambda o, t: (t, o, 0)),
          Arg(kpe, (tr, MLA_ROPE), lambda o, t: (o, 0), acc=True)]
    return (lp // tr, MLA_H), ps, cs, xs, tr


ATT_HB = 2


def _attn_tile(lp):
    return 832 if (lp % 832 == 0 and lp > 832) else 64


def _attn_mask(qi, ki, ta):
    rows = qi * ta + lax.broadcasted_iota(jnp.int32, (ta, ta), 0)
    cols = ki * ta + lax.broadcasted_iota(jnp.int32, (ta, ta), 1)
    return (cols >= PAD) & ((cols // CH) <= (rows // CH))


def attn_fwd(q, k, kv):
    nh, lp, dq = q.shape
    ta = _attn_tile(lp)
    nb = lp // ta
    scale = MLA_QK ** -0.5

    hb = ATT_HB

    def body(q_ref, k_ref, v_ref, o_ref, lse_ref, m_s, l_s, acc_s):
        qi, ki = pl.program_id(1), pl.program_id(2)

        @pl.when(ki == 0)
        def _():
            m_s[...] = jnp.full_like(m_s, NEG)
            l_s[...] = jnp.zeros_like(l_s)
            acc_s[...] = jnp.zeros_like(acc_s)

        def step(masked):
            ss = [_bdot(q_ref[j], k_ref[j], 1, 1) * scale for j in range(hb)]
            ps = []
            for j in range(hb):
                s = jnp.where(_attn_mask(qi, ki, ta), ss[j], NEG) if masked else ss[j]
                m_new = jnp.maximum(m_s[j], jnp.max(s, axis=-1, keepdims=True))
                p = jnp.exp(s - m_new)
                alpha = jnp.exp(m_s[j] - m_new)
                l_s[j] = alpha * l_s[j] + jnp.sum(p, axis=-1, keepdims=True)
                m_s[j] = m_new
                ps.append((p, alpha))
            for j in range(hb):
                acc_s[j] = ps[j][1] * acc_s[j] + _bdot(ps[j][0], v_ref[j], 1, 0)

        pl.when((ki == qi) | (ki == 0))(functools.partial(step, True))
        pl.when((ki < qi) & (ki > 0))(functools.partial(step, False))

        @pl.when(ki == nb - 1)
        def _():
            for j in range(hb):
                o_ref[:, MLA_V * j:MLA_V * (j + 1)] = (acc_s[j] / l_s[j]).astype(o_ref.dtype)
                lse_ref[j] = m_s[j] + jnp.log(l_s[j])

    return pl.pallas_call(
        body, name="mla_attn", grid=(nh // hb, nb, nb),
        in_specs=[pl.BlockSpec((hb, ta, dq), lambda h, qi, ki: (h, qi, 0)),
                  pl.BlockSpec((hb, ta, dq), lambda h, qi, ki: (h, jnp.minimum(ki, qi), 0)),
                  pl.BlockSpec((hb, ta, MLA_V), lambda h, qi, ki: (h, jnp.minimum(ki, qi), 1))],
        out_specs=[pl.BlockSpec((ta, hb * MLA_V), lambda h, qi, ki: (qi, h)),
                   pl.BlockSpec((hb, ta, 1), lambda h, qi, ki: (h, qi, 0))],
        out_shape=[SDS((lp, nh * MLA_V), BF), SDS((nh, lp, 1), F32)],
        scratch_shapes=[pltpu.VMEM((hb, ta, 1), F32), pltpu.VMEM((hb, ta, 1), F32), pltpu.VMEM((hb, ta, MLA_V), F32)],
        compiler_params=_cp(3))(q, k, kv)


def attn_bwd(q, k, kv, o, do, lse):
    nh, lp, dq = q.shape
    ta = _attn_tile(lp)
    nb = lp // ta
    scale = MLA_QK ** -0.5

    def body(q_ref, k_ref, v_ref, o_ref, do_ref, lse_ref, dq_ref, dk_ref, dv_ref, dk_s, dv_s):
        ki, qi = pl.program_id(1), pl.program_id(2)

        @pl.when((ki == 0) & (qi == 0))
        def _():
            dq_ref[...] = jnp.zeros_like(dq_ref)

        @pl.when(qi == 0)
        def _():
            dk_s[...] = jnp.zeros_like(dk_s)
            dv_s[...] = jnp.zeros_like(dv_s)

        def step(masked):
            dov = do_ref[...]
            s = _bdot(q_ref[...], k_ref[...], 1, 1) * scale
            dp = _bdot(dov, v_ref[...], 1, 1)
            if masked:
                s = jnp.where(_attn_mask(qi, ki, ta), s, NEG)
            p = jnp.exp(s - lse_ref[...])
            delta = jnp.sum(dov * o_ref[...].astype(F32), axis=-1, keepdims=True)
            dv_s[...] += _bdot(p, dov, 0, 0)
            ds = p * (dp - delta) * scale
            rows = pl.ds(pl.multiple_of(qi * ta, ta), ta)
            dq_ref[rows, :] += _bdot(ds, k_ref[...], 1, 0)
            dk_s[...] += _bdot(ds, q_ref[...], 0, 0)

        pl.when((ki == qi) | (ki == 0))(functools.partial(step, True))
        pl.when((ki < qi) & (ki > 0))(functools.partial(step, False))

        @pl.when(qi == nb - 1)
        def _():
            dk_ref[...] = dk_s[...]
            dv_ref[...] = dv_s[...]

    qmap = lambda h, ki, qi: (h, jnp.maximum(qi, ki), 0)
    return pl.pallas_call(
        body, name="mla_attn_b", grid=(nh, nb, nb),
        in_specs=[pl.BlockSpec((None, ta, dq), qmap),
                  pl.BlockSpec((None, ta, dq), lambda h, ki, qi: (h, ki, 0)),
                  pl.BlockSpec((None, ta, MLA_V), lambda h, ki, qi: (h, ki, 1)),
                  pl.BlockSpec((ta, MLA_V), lambda h, ki, qi: (jnp.maximum(qi, ki), h)),
                  pl.BlockSpec((ta, MLA_V), lambda h, ki, qi: (jnp.maximum(qi, ki), h)),
                  pl.BlockSpec((None, ta, 1), qmap)],
        out_specs=[pl.BlockSpec((None, lp, dq), lambda h, ki, qi: (h, 0, 0)),
                   pl.BlockSpec((None, ta, dq), lambda h, ki, qi: (h, ki, 0)),
                   pl.BlockSpec((None, ta, MLA_V), lambda h, ki, qi: (h, ki, 0))],
        out_shape=[SDS((nh, lp, dq), F32), SDS((nh, lp, dq), F32), SDS((nh, lp, MLA_V), F32)],
        scratch_shapes=[pltpu.VMEM((ta, dq), F32), pltpu.VMEM((ta, MLA_V), F32)],
        compiler_params=_cp(3))(q, k, kv, o, do, lse)


def mla_fwd(h, mask, w):
    lp = h.shape[0]
    a = norm_fwd("mla_norm", h, w["ng"], mask)
    down = lin("mla_down", a, w["wdown"])
    grid, ps, xs, tr = _prep1_args(down, w["gcq"], w["gckv"])
    row = lambda o, t: (t, 0)
    (cq, ckv, kpe), _ = seq_fwd("mla_prep1", _mla_prep1_fn, grid, ps, [], xs,
                                [((lp, MLA_QL), BF, (tr, MLA_QL), row), ((lp, MLA_KVL), BF, (tr, MLA_KVL), row),
                                 ((lp, MLA_ROPE), F32, (tr, MLA_ROPE), row)])
    qraw = lin_bo("mla_uq", cq, w["wuq"])
    kvraw = lin_bo("mla_ukv", ckv, w["wukv"])
    grid, ps, cs, xs, tr = _prep2_args(qraw, kvraw, kpe, w["gq"], w["gk"], w["mc"])
    hm = lambda o, t: (t, o, 0)
    (q, k), _ = seq_fwd("mla_prep2", _mla_prep2_fn, grid, ps, cs, xs,
                        [((MLA_H, lp, MLA_QK), BF, (None, tr, MLA_QK), hm), ((MLA_H, lp, MLA_QK), BF, (None, tr, MLA_QK), hm)])
    o, lse = attn_fwd(q, k, kvraw)
    h2 = lin("mla_out", o, w["wo"], res=h)
    return h2, (h, a, down, cq, ckv, kpe, qraw, kvraw, q, k, o, lse)


def mla_bwd(dh2, mask, w, saved, emit):
    h, a, down, cq, ckv, kpe, qraw, kvraw, q, k, o, lse = saved
    lp = h.shape[0]
    g = {}
    g["wo"] = wgrad("mla_dwo", o, dh2)
    do = lin_t("mla_do", dh2, w["wo"])
    do = emit("wo", [g["wo"]], do)
    dq, dk, dv = attn_bwd(q, k, kvraw, o, do, lse)
    grid, ps, cs, xs, tr = _prep2_args(qraw, kvraw, kpe, w["gq"], w["gk"], w["mc"])
    hm = lambda o, t: (t, o, 0)
    (dqraw, dkvraw, dkpe), (g["gq"], g["gk"]) = seq_bwd(
        "mla_prep2_b", _mla_prep2_b_fn, grid, ps, cs, xs,
        [Arg(dq, (None, tr, MLA_QK), hm), Arg(dk, (None, tr, MLA_QK), hm), Arg(dv, (None, tr, MLA_V), hm)])
    g["wuq"] = wgrad_bo("mla_dwuq", cq, dqraw)
    dcq = lin_t_bi("mla_dcq", dqraw, w["wuq"])
    g["wukv"] = wgrad_bo("mla_dwukv", ckv, dkvraw)
    dckv = lin_t_bi("mla_dckv", dkvraw, w["wukv"])
    dckv = emit("wu", [g["wuq"], g["wukv"]], dckv)
    grid, ps, xs, tr = _prep1_args(down, w["gcq"], w["gckv"])
    row = lambda o, t: (t, 0)
    (ddown,), (g["gcq"], g["gckv"]) = seq_bwd(
        "mla_prep1_b", _mla_prep1_fn, grid, ps, [], xs,
        [Arg(dcq, (tr, MLA_QL), row), Arg(dckv, (tr, MLA_KVL), row), Arg(dkpe, (tr, MLA_ROPE), row)])
    g["wdown"] = wgrad("mla_dwdown", a, ddown)
    da = lin_t("mla_da", ddown, w["wdown"])
    dh, g["ng"] = norm_bwd("mla_norm_b", h, w["ng"], mask, da, dh2)
    return dh, g


def loss_head(h, target):
    lp, d = h.shape
    assert OFF == CH

    def body(h_ref, t_ref, loss_ref, dh_ref):
        i = pl.program_id(0)

        @pl.when(i == 0)
        def _():
            loss_ref[...] = jnp.zeros_like(loss_ref)

        e = jnp.where(i > 0, h_ref[...] - t_ref[...], 0.0)
        loss_ref[...] += jnp.sum(e * e) * (0.5 / d)
        dh_ref[...] = e * (1.0 / d)

    return pl.pallas_call(
        body, name="loss_head", grid=(lp // CH,),
        in_specs=[pl.BlockSpec((CH, d), lambda i: (i, 0)), pl.BlockSpec((CH, d), lambda i: (jnp.maximum(i - 1, 0), 0))],
        out_specs=[pl.BlockSpec((8, 128), lambda i: (0, 0)), pl.BlockSpec((CH, d), lambda i: (i, 0))],
        out_shape=[SDS((8, 128), F32), SDS((lp, d), F32)], compiler_params=_cp(1))(h, target)


ADAM_LAND_BYTES = 20 * 1024 * 1024


def _adam_tile(r, c, nl):
    if r % 8:
        return r
    best = 8
    for t in range(8, r + 1, 8):
        if r % t == 0 and N_DEV * t * c * 4 * 2 * nl <= ADAM_LAND_BYTES:
            best = t
    return best


def adamw(name, lands, w, m, v):
    nl, r, c = w.shape
    tr = _adam_tile(r, c, nl)
    c1 = 1.0 / (1.0 - ADAM_B1 ** ADAM_STEP)
    c2 = 1.0 / (1.0 - ADAM_B2 ** ADAM_STEP)

    def body(*refs):
        l_refs = refs[:nl]
        w_ref, m_ref, v_ref, g_ref, d_ref, nm_ref, nv_ref = refs[nl:]
        layer = pl.program_id(0)
        for j in range(nl):
            @pl.when(layer == j)
            def _(j=j):
                g = l_refs[j][0]
                for i in range(1, N_DEV):
                    g = g + l_refs[j][i]
                g_ref[...] = g

        g = g_ref[...]
        nm = ADAM_B1 * m_ref[...] + (1.0 - ADAM_B1) * g
        nv = ADAM_B2 * v_ref[...] + (1.0 - ADAM_B2) * (g * g)
        nm_ref[...] = nm
        nv_ref[...] = nv
        d_ref[...] = -ADAM_LR * ((nm * c1) / (jnp.sqrt(nv * c2) + ADAM_EPS) + ADAM_WD * w_ref[...])

    blk = pl.BlockSpec((None, tr, c), lambda l, i: (l, i, 0))
    land_specs = [pl.BlockSpec((N_DEV, tr, c), lambda l, i, j=j: (0, jnp.where(l == j, i, 0), 0)) for j in range(nl)]
    return pl.pallas_call(
        body, name=name, grid=(nl, r // tr), in_specs=land_specs + [blk, blk, blk],
        out_specs=[blk, blk, blk, blk], out_shape=[SDS((nl, r, c), F32)] * 4, compiler_params=_cp(2))(*lands, w, m, v)


ANY = pl.BlockSpec(memory_space=pl.ANY)
MESH = pl.DeviceIdType.MESH


def _me():
    return lax.axis_index("x"), lax.axis_index("y"), lax.axis_index("c")


def _peers():
    x, y, c = _me()
    out = []
    for k in range(1, N_DEV):
        px = 1 - x if k & 4 else x
        py = 1 - y if k & 2 else y
        pc = 1 - c if k & 1 else c
        out.append(((px, py, pc), 4 * px + 2 * py + pc))
    return out


HBM_SPEC = pl.BlockSpec(memory_space=pltpu.HBM)
SEM_SPEC = pl.BlockSpec(memory_space=pltpu.SEMAPHORE)
DATAFLOW = pltpu.SideEffectType.DATAFLOW_SIDE_EFFECTING


def _my_index():
    return 4 * lax.axis_index("x") + 2 * lax.axis_index("y") + lax.axis_index("c")


def _hbm(a):
    return pltpu.with_memory_space_constraint(a, pltpu.HBM)


NP = N_DEV - 1


def _push_copy(x_ref, land_ref, send, recv, pid, src_idx, dst_idx, scatter):
    src = x_ref.at[src_idx] if scatter else x_ref
    return pltpu.make_async_remote_copy(src_ref=src, dst_ref=land_ref.at[dst_idx], send_sem=send, recv_sem=recv,
                                        device_id=pid, device_id_type=MESH)


def push_start(name, xs, me, scatter, carry=None):
    n = len(xs)
    lands = []
    for a in xs:
        own = lax.dynamic_index_in_dim(a, me, 0, keepdims=True) if scatter else a[None]
        z = lax.empty((N_DEV,) + own.shape[1:], a.dtype)
        lands.append(lax.dynamic_update_slice(z, own, (me,) + (0,) * (own.ndim - 1)))
    ns = 2 * NP * n
    ops = xs + lands + ([carry] if carry is not None else [])
    na = len(ops)

    def body(*refs):
        x_refs, land_refs = refs[:n], refs[n:2 * n]
        sems = refs[na:na + ns]
        token = refs[-1]
        x, y, c = _me()
        mine = 4 * x + 2 * y + c
        for i in range(n):
            for k, (pid, pidx) in enumerate(_peers()):
                s = 2 * (NP * i + k)
                _push_copy(x_refs[i], land_refs[i], sems[s], sems[s + 1], pid, pidx, mine, scatter).start()
        token[...] = jnp.zeros_like(token)

    out_shape = ([pltpu.SemaphoreType.DMA(())] * ns + [pltpu.HBM(a.shape, a.dtype) for a in ops]
                 + [SDS((8, 128), F32)])
    res = pl.pallas_call(
        body, name=name, out_shape=out_shape, in_specs=[HBM_SPEC] * na,
        out_specs=[SEM_SPEC] * ns + [HBM_SPEC] * na + [pl.BlockSpec(memory_space=pltpu.VMEM)],
        input_output_aliases={i: ns + i for i in range(na)},
        compiler_params=pltpu.CompilerParams(has_side_effects=DATAFLOW))(*[_hbm(a) for a in ops])
    sems, thru, token = res[:ns], res[ns:-1], res[-1]
    handles = [dict(x=thru[i], land=thru[n + i], sems=list(sems[2 * NP * i:2 * NP * (i + 1)]), token=token)
               for i in range(n)]
    return (handles, thru[2 * n]) if carry is not None else handles


def push_wait(name, hds, after, scatter):
    n = len(hds)
    ns = 2 * NP

    def body(*refs):
        x_refs, land_refs = refs[:n], refs[n:2 * n]
        sems = refs[2 * n:2 * n + ns * n]
        for i in range(n):
            for k, (pid, pidx) in enumerate(_peers()):
                cp = _push_copy(x_refs[i], land_refs[i], sems[ns * i + 2 * k], sems[ns * i + 2 * k + 1], pid, pidx, pidx,
                                scatter)
                cp.wait_send()
                cp.wait_recv()

    arrs = [hd["x"] for hd in hds] + [hd["land"] for hd in hds]
    sems = [s for hd in hds for s in hd["sems"]]
    res = pl.pallas_call(
        body, name=name, out_shape=[pltpu.HBM(a.shape, a.dtype) for a in arrs],
        in_specs=[HBM_SPEC] * (2 * n) + [SEM_SPEC] * (ns * n) + [ANY], out_specs=[HBM_SPEC] * (2 * n),
        input_output_aliases={i: i for i in range(2 * n)},
        compiler_params=pltpu.CompilerParams(has_side_effects=DATAFLOW))(*arrs, *sems, after)
    return list(res[n:])


WEIGHTS = ['meta_tokens', 'norm_mix_g', 'norm_ffn_g', 'mla_w_down', 'mla_cq_norm_g', 'mla_ckv_norm_g', 'mla_w_uq',
           'mla_w_ukv', 'mla_q_head_g', 'mla_k_head_g', 'mla_w_o', 'hgrn_w_in', 'hgrn_lb_logits', 'hgrn_o_norm_g',
           'hgrn_w_o', 's5_lam_re', 's5_lam_im', 's5_log_dt', 's5_b_re', 's5_b_im', 's5_c_re', 's5_c_im', 's5_d',
           's5_w_glu', 'ret_w_in', 'ret_gn_g', 'ret_w_o', 'ffn_w_up', 'ffn_conv_w', 'ffn_conv_b', 'ffn_w_down']
BIG = ['mla_w_down', 'mla_w_uq', 'mla_w_ukv', 'mla_w_o', 'hgrn_w_in', 'hgrn_w_o', 's5_w_glu', 'ret_w_in', 'ret_w_o',
       'ffn_w_up', 'ffn_w_down']
SMALL_SH = ['meta_tokens', 's5_d', 'ret_gn_g', 'ffn_conv_w']
REP_S5 = ['s5_lam_re', 's5_lam_im', 's5_log_dt', 's5_b_re', 's5_b_im', 's5_c_re', 's5_c_im']
REP_REST = ['norm_mix_g', 'norm_ffn_g', 'mla_cq_norm_g', 'mla_ckv_norm_g', 'mla_q_head_g', 'mla_k_head_g',
            'hgrn_lb_logits', 'hgrn_o_norm_g', 'ffn_conv_b']
SMALL_REP = REP_REST + REP_S5
LANE = 128


def _flat(arrs, mult):
    v = jnp.concatenate([a.reshape(-1) for a in arrs])
    pad = (-v.shape[0]) % mult
    return jnp.pad(v, (0, pad)).reshape(-1, LANE)


def _unflat(flat2d, like):
    v = flat2d.reshape(-1)
    out, o = [], 0
    for a in like:
        out.append(v[o:o + a.size].reshape(a.shape))
        o += a.size
    return out


def _lb_of(logits):
    cum = jnp.cumsum(jax.nn.softmax(logits, axis=0), axis=0)
    return (cum - cum[0:1])[1:2]


def _cols_to_blocks(g):
    k, n = g.shape
    return g.reshape(k, N_DEV, n // N_DEV).transpose(1, 0, 2)


def _blocks_to_cols(wb):
    nb, k, n = wb.shape
    return wb.transpose(1, 0, 2).reshape(k, nb * n)


SUBS = ['mla', 'ffn0', 'hgrn', 'ffn1', 's5', 'ffn2', 'ret', 'ffn3']
GROUPS = [[('mla_w_down', 0), ('mla_w_uq', 0), ('mla_w_ukv', 0), ('mla_w_o', 0)],
          [('ffn_w_up', 0), ('ffn_w_down', 0)],
          [('hgrn_w_in', 0), ('hgrn_w_o', 0)],
          [('ffn_w_up', 1), ('ffn_w_down', 1)],
          [('s5_w_glu', 0)],
          [('ffn_w_up', 2), ('ffn_w_down', 2)],
          [('ret_w_in', 0), ('ret_w_o', 0)],
          [('ffn_w_up', 3), ('ffn_w_down', 3)]]


def _pack8(parts, mult):
    v = jnp.concatenate(parts, axis=1)
    return jnp.pad(v, ((0, 0), (0, (-v.shape[1]) % mult))).reshape(N_DEV, -1, LANE)


def _sub_weights(k, got, rep, tabs, lp):
    ngm, ngf = rep['norm_mix_g'], rep['norm_ffn_g']
    if k == 0:
        return dict(ng=ngm[0:1], wdown=got[0].reshape(D, -1), gcq=rep['mla_cq_norm_g'], gckv=rep['mla_ckv_norm_g'],
                    wuq=got[1], wukv=got[2], gq=rep['mla_q_head_g'], gk=rep['mla_k_head_g'], wo=got[3].reshape(D, D),
                    mc=mla_consts(lp))
    if k == 2:
        return dict(ng=ngm[1:2], win=got[0], lb=tabs['lb'], go=rep['hgrn_o_norm_g'], wo=got[1].reshape(D, D))
    if k == 4:
        return dict(ng=ngm[2:3], tb=tabs['tb'], dsk=tabs['s5_d'], wglu=_blocks_to_cols(got[0]))
    if k == 6:
        return dict(ng=ngm[3:4], win=_blocks_to_cols(got[0]), gn=tabs['ret_gn_g'], wo=got[1].reshape(2 * D, D),
                    rc=ret_consts(lp))
    i = k // 2
    return dict(ng=ngf[i:i + 1], up=got[0], cw=tabs['conv_w'][:, i].reshape(2, 4, 3, 1, FFN_B),
                cb=rep['ffn_conv_b'][i].reshape(2, 4, 1, FFN_B), down=got[1].reshape(4, FFN_B, D))


def _sub_grad_blocks(k, g):
    if k == 0:
        parts = [g['wdown'], g['wuq'], g['wukv'], g['wo']]
    elif k == 2:
        parts = [g['win'], g['wo']]
    elif k == 4:
        parts = [_cols_to_blocks(g['wglu'])]
    elif k == 6:
        parts = [_cols_to_blocks(g['win']), g['wo']]
    else:
        parts = [g['up'], g['down']]
    return parts


_FWD = [mla_fwd, None, hgrn_fwd, None, s5_fwd, None, ret_fwd, None]
_BWD = [mla_bwd, None, hgrn_bwd, None, s5_bwd, None, ret_bwd, None]


def _step(args):
    w = {n: args[n] for n in WEIGHTS}
    x2, tgt = args['x'][0], args['loss_target'][0]

    lp = x2.shape[0] + OFF
    me = _my_index()
    mask = _rowmask(lp)
    rep = {n: w[n] for n in SMALL_REP}

    xs, slots = [], []
    for gi, grp in enumerate(GROUPS):
        items = [w[n][l].astype(BF) for n, l in grp] + ([_flat([w[n] for n in SMALL_SH], LANE)] if gi == 0 else [])
        slots.append((len(xs), len(items)))
        xs += items
    gh = push_start("gather_start", xs, me, scatter=False)

    def fetch(gi, after):
        s, cnt = slots[gi]
        return push_wait("gather_wait_" + SUBS[gi], gh[s:s + cnt], after, scatter=False)

    got = fetch(0, x2)
    sm, o, smp = got[-1].reshape(N_DEV, -1), 0, {}
    for n in SMALL_SH:
        smp[n] = sm[:, o:o + w[n].size].reshape((N_DEV,) + w[n].shape)
        o += w[n].size
    meta = smp['meta_tokens'].transpose(1, 0, 2).reshape(N_META, D)
    lb, lb_vjp = jax.vjp(_lb_of, rep['hgrn_lb_logits'])
    s5p = [rep[n][0] for n in ('s5_lam_re', 's5_lam_im', 's5_log_dt', 's5_b_re', 's5_b_im', 's5_c_re', 's5_c_im')]
    tb, tb_vjp = jax.vjp(s5_tables, *s5p)
    tabs = dict(lb=lb, tb=tb, s5_d=smp['s5_d'].reshape(1, D), ret_gn_g=smp['ret_gn_g'].reshape(1, 2 * D),
                conv_w=smp['ffn_conv_w'])
    h = jnp.concatenate([jnp.zeros((PAD, D), F32), meta, x2], axis=0)
    ws, saved = [], []
    for k in range(8):
        if k > 0:
            got = fetch(k, h)
        ws.append(_sub_weights(k, got, rep, tabs, lp))
        if k % 2:
            h, sv = ffn_fwd(k // 2, h, mask, ws[k])
        else:
            h, sv = _FWD[k](h, mask, ws[k])
        saved.append(sv)
    loss, dh = loss_head(h, tgt)

    gs, sh = [None] * 8, [None] * 8
    early = {}

    def emit(tag, grads, carry):
        blocks = [t.reshape((N_DEV, -1) + t.shape[-1:]) if t.ndim == 2 else t for t in grads]
        early[tag], carry = push_start("scatter_start_mla_" + tag, blocks, me, scatter=True, carry=carry)
        return carry

    for k in reversed(range(1, 8)):
        if k % 2:
            dh, gs[k] = ffn_bwd(k // 2, dh, mask, ws[k], saved[k])
        else:
            dh, gs[k] = _BWD[k](dh, mask, ws[k], saved[k])
        blocks = [b.reshape((N_DEV,) + w[n].shape[1:]) for b, (n, _) in zip(_sub_grad_blocks(k, gs[k]), GROUPS[k])]
        sh[k], dh = push_start("scatter_start_" + SUBS[k], blocks, me, scatter=True, carry=dh)
        if k == 4:
            gs5 = _flat(list(tb_vjp(gs[4]['tb'])), 8 * LANE)
            rh_s5, dh = push_start("small_grads_start_s5", [gs5], me, scatter=False, carry=dh)
    dh, gs[0] = mla_bwd(dh, mask, ws[0], saved[0], emit)
    dmeta = dh[PAD:OFF].reshape(N_META, N_DEV, D // N_DEV).transpose(1, 0, 2)
    dcw = jnp.stack([gs[2 * i + 1]['cw'].reshape(N_DEV, 3, FFN_B) for i in range(4)], axis=1)
    last = push_start("scatter_start_mla", [gs[0]['wdown'].reshape(N_DEV, D // N_DEV, -1),
                                            _pack8([t.reshape(N_DEV, -1) for t in (dmeta, gs[4]['dsk'], gs[6]['gn'], dcw)], LANE)],
                      me, scatter=True)
    sh[0] = [last[0], early['wu'][0], early['wu'][1], early['wo'][0], last[1]]
    grad_x = dh[OFF:]

    g_rep = {
        'norm_mix_g': jnp.concatenate([gs[k]['ng'] for k in (0, 2, 4, 6)], axis=0),
        'norm_ffn_g': jnp.concatenate([gs[k]['ng'] for k in (1, 3, 5, 7)], axis=0),
        'mla_cq_norm_g': gs[0]['gcq'], 'mla_ckv_norm_g': gs[0]['gckv'], 'mla_q_head_g': gs[0]['gq'],
        'mla_k_head_g': gs[0]['gk'], 'hgrn_lb_logits': lb_vjp(gs[2]['lb'])[0], 'hgrn_o_norm_g': gs[2]['go'],
        'ffn_conv_b': jnp.stack([gs[k]['cb'].reshape(-1) for k in (1, 3, 5, 7)], axis=0),
    }
    loss_part = loss[0, 0:1]
    grep = _flat([g_rep[n] for n in REP_REST] + [loss_part], 8 * LANE)
    rh = push_start("small_grads_start", [grep], me, scatter=False)

    lands = {n: [None] * w[n].shape[0] for n in BIG}
    res = {}
    late = [n for n, _ in GROUPS[0]]
    for k in reversed(range(1, 8)):
        got = push_wait("scatter_wait_" + SUBS[k], sh[k], grep, scatter=True)
        for (n, l), t in zip(GROUPS[k], got):
            lands[n][l] = t
    for n in BIG:
        if n not in late:
            res[n] = adamw("adam_" + n, lands[n], w[n], args['m_' + n], args['v_' + n])
    after = res['ffn_w_up'][1]
    got = push_wait("scatter_wait_" + SUBS[0], sh[0], after, scatter=True)
    small_land = got[-1]
    (rep_land,) = push_wait("small_grads_wait", rh, after, scatter=False)
    (s5_land,) = push_wait("small_grads_wait_s5", rh_s5, after, scatter=False)
    for (n, _), t in zip(GROUPS[0], got):
        res[n] = adamw("adam_" + n, [t], w[n], args['m_' + n], args['v_' + n])

    def flat_adam(name, land, names, mult, extra=()):
        like = [w[n] for n in names]
        pad = [jnp.zeros_like(e) for e in extra]
        out = adamw(name, [land], _flat(like + pad, mult)[None], _flat([args['m_' + n] for n in names] + pad, mult)[None],
                    _flat([args['v_' + n] for n in names] + pad, mult)[None])
        for n, parts in zip(names, zip(*[_unflat(t, like) for t in out])):
            res[n] = list(parts)
        return out[0]

    flat_adam("adam_small_sharded", small_land, SMALL_SH, LANE)
    flat_adam("adam_s5_replicated", s5_land, REP_S5, 8 * LANE)
    gsum = flat_adam("adam_small_replicated", rep_land, REP_REST, 8 * LANE, extra=[loss_part])
    total = gsum.reshape(-1)[sum(w[n].size for n in REP_REST)]
    outs = [total, grad_x[None]]
    for k in range(4):
        outs += [res[n][k] for n in WEIGHTS]
    return tuple(outs)


def kernel(x, meta_tokens, norm_mix_g, norm_ffn_g, mla_w_down, mla_cq_norm_g, mla_ckv_norm_g, mla_w_uq, mla_w_ukv, mla_q_head_g, mla_k_head_g, mla_w_o, hgrn_w_in, hgrn_lb_logits, hgrn_o_norm_g, hgrn_w_o, s5_lam_re, s5_lam_im, s5_log_dt, s5_b_re, s5_b_im, s5_c_re, s5_c_im, s5_d, s5_w_glu, ret_w_in, ret_gn_g, ret_w_o, ffn_w_up, ffn_conv_w, ffn_conv_b, ffn_w_down, loss_target, m_meta_tokens, m_norm_mix_g, m_norm_ffn_g, m_mla_w_down, m_mla_cq_norm_g, m_mla_ckv_norm_g, m_mla_w_uq, m_mla_w_ukv, m_mla_q_head_g, m_mla_k_head_g, m_mla_w_o, m_hgrn_w_in, m_hgrn_lb_logits, m_hgrn_o_norm_g, m_hgrn_w_o, m_s5_lam_re, m_s5_lam_im, m_s5_log_dt, m_s5_b_re, m_s5_b_im, m_s5_c_re, m_s5_c_im, m_s5_d, m_s5_w_glu, m_ret_w_in, m_ret_gn_g, m_ret_w_o, m_ffn_w_up, m_ffn_conv_w, m_ffn_conv_b, m_ffn_w_down, v_meta_tokens, v_norm_mix_g, v_norm_ffn_g, v_mla_w_down, v_mla_cq_norm_g, v_mla_ckv_norm_g, v_mla_w_uq, v_mla_w_ukv, v_mla_q_head_g, v_mla_k_head_g, v_mla_w_o, v_hgrn_w_in, v_hgrn_lb_logits, v_hgrn_o_norm_g, v_hgrn_w_o, v_s5_lam_re, v_s5_lam_im, v_s5_log_dt, v_s5_b_re, v_s5_b_im, v_s5_c_re, v_s5_c_im, v_s5_d, v_s5_w_glu, v_ret_w_in, v_ret_gn_g, v_ret_w_o, v_ffn_w_up, v_ffn_conv_w, v_ffn_conv_b, v_ffn_w_down):
    return _step(dict(locals()))
```

```python
import functools
import math

import jax
import jax.numpy as jnp
import numpy as np
from jax import lax
from jax.experimental import pallas as pl
from jax.experimental.pallas import tpu as pltpu

F32 = jnp.float32
BF = jnp.bfloat16
SDS = jax.ShapeDtypeStruct

N_DEV = 8
D = 1024
N_META = 16
PAD = 48
OFF = PAD + N_META
CH = 64
EPS = 1e-6
NEG = -1e30
ROPE_BASE = 10000.0

MLA_H, MLA_NOPE, MLA_ROPE, MLA_V = 8, 128, 64, 128
MLA_QK = MLA_NOPE + MLA_ROPE
MLA_QL, MLA_KVL = 384, 256
HG_H, HG_D, HG_C = 8, 128, 16
S5_G, S5_P, S5_K = 64, 64, 16
S5_SG = 8
RET_H, RET_DK, RET_DV = 4, 256, 512
FFN_F = 2816
FFN_B = 704

ADAM_LR, ADAM_B1, ADAM_B2, ADAM_EPS, ADAM_WD, ADAM_STEP = 0.001, 0.9, 0.999, 1e-08, 0.01, 10

VMEM_LIMIT = 56 * 1024 * 1024
ARB = "arbitrary"


def _cp(n):
    return pltpu.CompilerParams(dimension_semantics=(ARB,) * n, vmem_limit_bytes=VMEM_LIMIT)


def _bdot(a, b, ca, cb):
    return lax.dot_general(a.astype(BF), b.astype(BF), (((ca,), (cb,)), ((), ())), preferred_element_type=F32)


@jax.custom_vjp
def mm(a, b):
    return _bdot(a, b, 1, 0)


@jax.custom_vjp
def mm_nt(a, b):
    return _bdot(a, b, 1, 1)


@jax.custom_vjp
def mm_tn(a, b):
    return _bdot(a, b, 0, 0)


mm.defvjp(lambda a, b: (mm(a, b), (a, b)),
          lambda r, g: (mm_nt(g, r[1]).astype(r[0].dtype), mm_tn(r[0], g).astype(r[1].dtype)))
mm_nt.defvjp(lambda a, b: (mm_nt(a, b), (a, b)),
             lambda r, g: (mm(g, r[1]).astype(r[0].dtype), mm_tn(g, r[0]).astype(r[1].dtype)))
mm_tn.defvjp(lambda a, b: (mm_tn(a, b), (a, b)),
             lambda r, g: (mm_nt(r[1], g).astype(r[0].dtype), mm(r[0], g).astype(r[1].dtype)))


def _xdot(a, b, ca, cb):
    return lax.dot_general(a, b, (((ca,), (cb,)), ((), ())), preferred_element_type=F32,
                           precision=lax.Precision.HIGHEST)


@jax.custom_vjp
def cright(x, r):
    return _xdot(x, r, 1, 0)


cright.defvjp(lambda x, r: (cright(x, r), r), lambda r, g: (_xdot(g, r, 1, 1), jnp.zeros_like(r)))


def _shift_raw(x, s):
    n = x.shape[0]
    r = lax.broadcasted_iota(jnp.int32, x.shape, 0)
    y = pltpu.roll(x, s % n, 0)
    return jnp.where((r >= s) & (r < n + s), y, 0.0)


@functools.partial(jax.custom_vjp, nondiff_argnums=(1,))
def shift_rows(x, s):
    return _shift_raw(x, s)


shift_rows.defvjp(lambda x, s: (_shift_raw(x, s), None), lambda s, _, g: (_shift_raw(g, -s),))


def _seg_shift_raw(x, s, seg, up):
    n = x.shape[0]
    r = lax.broadcasted_iota(jnp.int32, x.shape, 0) % seg
    if up:
        return jnp.where(r < seg - s, pltpu.roll(x, n - s, 0), 0.0)
    return jnp.where(r >= s, pltpu.roll(x, s, 0), 0.0)


@functools.partial(jax.custom_vjp, nondiff_argnums=(1, 2))
def seg_shift(x, s, seg):
    return _seg_shift_raw(x, s, seg, False)


seg_shift.defvjp(lambda x, s, seg: (_seg_shift_raw(x, s, seg, False), None),
                 lambda s, seg, _, g: (_seg_shift_raw(g, s, seg, True),))


def _seg_cumsum(x, seg):
    s = 1
    while s < seg:
        x = x + seg_shift(x, s, seg)
        s *= 2
    return x


def _rms(x, g):
    return x * lax.rsqrt(jnp.mean(x * x, axis=-1, keepdims=True) + EPS) * g


def _silu(x):
    return x * jax.nn.sigmoid(x)


def _mm_call(name, a, b, *, grid, a_spec, b_spec, o_shape, o_spec, dims, acc_shape, res=None, res_spec=None,
             mask_tm=None):
    nk = grid[2]

    def body(*refs):
        if res is None:
            a_ref, b_ref, o_ref = refs[:3]
        else:
            a_ref, b_ref, r_ref, o_ref = refs[:4]
        k = pl.program_id(2)

        def dot():
            return lax.dot_general(a_ref[...].astype(BF), b_ref[...].astype(BF), dims, preferred_element_type=F32)

        def finish(v):
            if res is not None:
                v = v + r_ref[...].astype(F32)
                rows = pl.program_id(0) * mask_tm + lax.broadcasted_iota(jnp.int32, v.shape, 0)
                v = jnp.where(rows >= PAD, v, 0.0)
            o_ref[...] = v.astype(o_ref.dtype)

        if nk == 1:
            finish(dot())
            return
        acc = refs[-1]

        @pl.when(k == 0)
        def _():
            acc[...] = dot()

        @pl.when((k > 0) & (k < nk - 1))
        def _():
            acc[...] += dot()

        @pl.when(k == nk - 1)
        def _():
            finish(acc[...] + dot())

    ins = [a, b] + ([res] if res is not None else [])
    specs = [a_spec, b_spec] + ([res_spec] if res is not None else [])
    scratch = [pltpu.VMEM(acc_shape, F32)] if nk > 1 else []
    return pl.pallas_call(body, name=name, grid=grid, in_specs=specs, out_specs=o_spec, out_shape=o_shape,
                          scratch_shapes=scratch, compiler_params=_cp(3))(*ins)


NN = (((1,), (0,)), ((), ()))
NT = (((1,), (1,)), ((), ()))
TN = (((0,), (0,)), ((), ()))


def _row_tile(lp):
    for t in (832, 640, 320, 64):
        if lp % t == 0:
            return t
    raise ValueError(lp)


def _col_tile(n):
    for t in (1024, 768, 512, 384, 256, 128):
        if n % t == 0:
            return t
    return n


def _mm_rows(m):
    return 2080 if m % 2080 == 0 else _row_tile(m)


def _mm_cols(n):
    for t in (512, 384, 256, 128):
        if n % t == 0:
            return t
    return n


def lin(name, a, w, out_dtype=F32, res=None):
    m, k = a.shape
    n = w.shape[1]
    tm, tn, tc = _mm_rows(m), _mm_cols(n), _col_tile(k)
    return _mm_call(name, a, w, grid=(m // tm, n // tn, k // tc),
                    a_spec=pl.BlockSpec((tm, tc), lambda i, j, kk: (i, kk)),
                    b_spec=pl.BlockSpec((tc, tn), lambda i, j, kk: (kk, j)),
                    o_shape=SDS((m, n), out_dtype), o_spec=pl.BlockSpec((tm, tn), lambda i, j, kk: (i, j)),
                    dims=NN, acc_shape=(tm, tn), res=res,
                    res_spec=pl.BlockSpec((tm, tn), lambda i, j, kk: (i, j)), mask_tm=tm)


def lin_bo(name, a, wb, out_dtype=F32):
    m, k = a.shape
    nb, _, n = wb.shape
    tm = _mm_rows(m)
    return _mm_call(name, a, wb, grid=(m // tm, nb, 1),
                    a_spec=pl.BlockSpec((tm, k), lambda i, j, kk: (i, 0)),
                    b_spec=pl.BlockSpec((None, k, n), lambda i, j, kk: (j, 0, 0)),
                    o_shape=SDS((nb, m, n), out_dtype), o_spec=pl.BlockSpec((None, tm, n), lambda i, j, kk: (j, i, 0)),
                    dims=NN, acc_shape=(tm, n))


def lin_bi(name, ab, wb, out_dtype=F32, res=None):
    nb, m, k = ab.shape
    n = wb.shape[2]
    tm, tn = _mm_rows(m), _mm_cols(n)
    return _mm_call(name, ab, wb, grid=(m // tm, n // tn, nb),
                    a_spec=pl.BlockSpec((None, tm, k), lambda i, j, kk: (kk, i, 0)),
                    b_spec=pl.BlockSpec((None, k, tn), lambda i, j, kk: (kk, 0, j)),
                    o_shape=SDS((m, n), out_dtype), o_spec=pl.BlockSpec((tm, tn), lambda i, j, kk: (i, j)),
                    dims=NN, acc_shape=(tm, tn), res=res,
                    res_spec=pl.BlockSpec((tm, tn), lambda i, j, kk: (i, j)), mask_tm=tm)


def lin_t(name, g, w, out_dtype=F32):
    m, n = g.shape
    k = w.shape[0]
    tm, tk, tc = _mm_rows(m), _col_tile(k), _col_tile(n)
    return _mm_call(name, g, w, grid=(m // tm, k // tk, n // tc),
                    a_spec=pl.BlockSpec((tm, tc), lambda i, j, kk: (i, kk)),
                    b_spec=pl.BlockSpec((tk, tc), lambda i, j, kk: (j, kk)),
                    o_shape=SDS((m, k), out_dtype), o_spec=pl.BlockSpec((tm, tk), lambda i, j, kk: (i, j)),
                    dims=NT, acc_shape=(tm, tk))


def lin_t_bi(name, gb, wb, out_dtype=F32):
    nb, m, n = gb.shape
    k = wb.shape[1]
    tm, tk = _mm_rows(m), _col_tile(k)
    return _mm_call(name, gb, wb, grid=(m // tm, k // tk, nb),
                    a_spec=pl.BlockSpec((None, tm, n), lambda i, j, kk: (kk, i, 0)),
                    b_spec=pl.BlockSpec((None, tk, n), lambda i, j, kk: (kk, j, 0)),
                    o_shape=SDS((m, k), out_dtype), o_spec=pl.BlockSpec((tm, tk), lambda i, j, kk: (i, j)),
                    dims=NT, acc_shape=(tm, tk))


def lin_t_bo(name, g, wb, out_dtype=F32):
    m, n = g.shape
    nb, k, _ = wb.shape
    tm = _mm_rows(m)
    return _mm_call(name, g, wb, grid=(m // tm, nb, 1),
                    a_spec=pl.BlockSpec((tm, n), lambda i, j, kk: (i, 0)),
                    b_spec=pl.BlockSpec((None, k, n), lambda i, j, kk: (j, 0, 0)),
                    o_shape=SDS((nb, m, k), out_dtype), o_spec=pl.BlockSpec((None, tm, k), lambda i, j, kk: (j, i, 0)),
                    dims=NT, acc_shape=(tm, k))


def wgrad(name, a, g):
    m, k = a.shape
    n = g.shape[1]
    tm, tn = _mm_rows(m), (_col_tile(n) if k <= 1024 else _mm_cols(n))
    return _mm_call(name, a, g, grid=(1, n // tn, m // tm),
                    a_spec=pl.BlockSpec((tm, k), lambda i, j, kk: (kk, 0)),
                    b_spec=pl.BlockSpec((tm, tn), lambda i, j, kk: (kk, j)),
                    o_shape=SDS((k, n), F32), o_spec=pl.BlockSpec((k, tn), lambda i, j, kk: (0, j)),
                    dims=TN, acc_shape=(k, tn))


def wgrad_bo(name, a, gb):
    m, k = a.shape
    nb, _, n = gb.shape
    tm = _mm_rows(m)
    return _mm_call(name, a, gb, grid=(nb, 1, m // tm),
                    a_spec=pl.BlockSpec((tm, k), lambda i, j, kk: (kk, 0)),
                    b_spec=pl.BlockSpec((None, tm, n), lambda i, j, kk: (i, kk, 0)),
                    o_shape=SDS((nb, k, n), F32), o_spec=pl.BlockSpec((None, k, n), lambda i, j, kk: (i, 0, 0)),
                    dims=TN, acc_shape=(k, n))


def wgrad_bi(name, zb, g):
    nb, m, k = zb.shape
    n = g.shape[1]
    tm, tn = _mm_rows(m), _col_tile(n)
    return _mm_call(name, zb, g, grid=(nb, n // tn, m // tm),
                    a_spec=pl.BlockSpec((None, tm, k), lambda i, j, kk: (i, kk, 0)),
                    b_spec=pl.BlockSpec((tm, tn), lambda i, j, kk: (kk, j)),
                    o_shape=SDS((nb, k, n), F32), o_spec=pl.BlockSpec((None, k, tn), lambda i, j, kk: (i, 0, j)),
                    dims=TN, acc_shape=(k, tn))


class Arg:
    def __init__(self, arr, block, imap, shared=False, acc=False):
        self.arr, self.block, self.imap = arr, block, imap
        self.shared = shared
        self.acc = acc

    @property
    def spec(self):
        return pl.BlockSpec(self.block, self.imap)

    def vshape(self):
        return tuple(b for b in self.block if b is not None)


def _rev(arg, nt):
    return pl.BlockSpec(arg.block, lambda o, t, _f=arg.imap: _f(o, nt - 1 - t))


def seq_fwd(name, fn, grid, params, consts, xs, outs, carries=(), save_dtype=F32):
    no, nt = grid
    n_p, n_c, n_x, n_y, n_k = len(params), len(consts), len(xs), len(outs), len(carries)

    def body(*refs):
        p_refs = refs[:n_p]
        c_refs = refs[n_p:n_p + n_c]
        x_refs = refs[n_p + n_c:n_p + n_c + n_x]
        r = n_p + n_c + n_x
        y_refs = refs[r:r + n_y]
        s_refs = refs[r + n_y:r + n_y + n_k]
        k_refs = refs[r + n_y + n_k:]
        t = pl.program_id(1)

        if n_k:
            @pl.when(t == 0)
            def _():
                for k in k_refs:
                    k[...] = jnp.zeros_like(k)

        carry = tuple(k[...] for k in k_refs)
        for s, c in zip(s_refs, carry):
            s[...] = c.astype(s.dtype)
        new_carry, ys = fn(tuple(p[...] for p in p_refs), tuple(c[...] for c in c_refs), carry,
                           tuple(x[...] for x in x_refs))
        for k, c in zip(k_refs, new_carry):
            k[...] = c
        for y_ref, y in zip(y_refs, ys):
            y_ref[...] = y.astype(y_ref.dtype)

    out_shape = [SDS(s, d) for (s, d, _, _) in outs]
    out_specs = [pl.BlockSpec(b, im) for (_, _, b, im) in outs]
    for cs in carries:
        out_shape.append(SDS((no, nt) + cs, save_dtype))
        out_specs.append(pl.BlockSpec((None, None) + cs, lambda o, t, _n=len(cs): (o, t) + (0,) * _n))
    res = pl.pallas_call(
        body, name=name, grid=grid, in_specs=[a.spec for a in list(params) + list(consts) + list(xs)],
        out_specs=out_specs, out_shape=out_shape, scratch_shapes=[pltpu.VMEM(cs, F32) for cs in carries],
        compiler_params=_cp(2))(*[a.arr for a in list(params) + list(consts) + list(xs)])
    return res[:n_y], res[n_y:]


def seq_bwd(name, fn, grid, params, consts, xs, dys, saved=(), carries=()):
    no, nt = grid
    n_p, n_c, n_x, n_y, n_k = len(params), len(consts), len(xs), len(dys), len(carries)

    def body(*refs):
        p_refs = refs[:n_p]
        c_refs = refs[n_p:n_p + n_c]
        x_refs = refs[n_p + n_c:n_p + n_c + n_x]
        r = n_p + n_c + n_x
        g_refs = refs[r:r + n_y]
        s_refs = refs[r + n_y:r + n_y + n_k]
        r = r + n_y + n_k
        dx_refs = refs[r:r + n_x]
        dp_refs = refs[r + n_x:r + n_x + n_p]
        k_refs = refs[r + n_x + n_p:]
        o = pl.program_id(0)
        t = pl.program_id(1)

        if n_k:
            @pl.when(t == 0)
            def _():
                for k in k_refs:
                    k[...] = jnp.zeros_like(k)

        for a, dp in zip(params, dp_refs):
            @pl.when((t == 0) & (o == 0) if a.shared else (t == 0))
            def _(dp=dp):
                dp[...] = jnp.zeros_like(dp)

        for a, dx in zip(xs, dx_refs):
            if a.acc:
                @pl.when(t == 0)
                def _(dx=dx):
                    dx[...] = jnp.zeros_like(dx)

        consts_v = tuple(c[...] for c in c_refs)

        def f(pv, cv, xv):
            return fn(pv, consts_v, cv, xv)

        pv = tuple(p[...] for p in p_refs)
        cv = tuple(s[...].astype(F32) for s in s_refs)
        xv = tuple(x[...] for x in x_refs)
        (new_carry, ys), vjp = jax.vjp(f, pv, cv, xv)
        cot = (tuple(k[...] for k in k_refs), tuple(g[...].astype(y.dtype) for g, y in zip(g_refs, ys)))
        dpv, dcv, dxv = vjp(cot)
        for k, c in zip(k_refs, dcv):
            k[...] = c
        for dp, v in zip(dp_refs, dpv):
            dp[...] += v
        for a, dx, v in zip(xs, dx_refs, dxv):
            if a.acc:
                dx[...] += v
            else:
                dx[...] = v.astype(dx.dtype)

    in_specs = ([_rev(a, nt) for a in list(params) + list(consts) + list(xs) + list(dys)]
                + [pl.BlockSpec((None, None) + cs, lambda o, t, _n=len(cs): (o, nt - 1 - t) + (0,) * _n) for cs in carries])
    out_shape = [SDS(a.arr.shape, F32) for a in xs] + [SDS(a.arr.shape, F32) for a in params]
    out_specs = [_rev(a, nt) for a in list(xs) + list(params)]
    res = pl.pallas_call(
        body, name=name, grid=grid, in_specs=in_specs, out_specs=out_specs, out_shape=out_shape,
        scratch_shapes=[pltpu.VMEM(cs, F32) for cs in carries], compiler_params=_cp(2))(
            *[a.arr for a in list(params) + list(consts) + list(xs) + list(dys)], *saved)
    return res[:n_x], res[n_x:]


def _rowmask(lp):
    return (jnp.arange(lp) >= PAD).astype(F32)[:, None]


def _norm_fn(p, c, k, x):
    return (), (_rms(x[0] * c[0], p[0]),)


def _norm_b_fn(p, c, k, x):
    h = x[0] * c[0]
    return (), (_rms(h, p[0]), h)


def norm_fwd(name, h, g, mask, out_dtype=BF):
    lp, d = h.shape
    tr = _row_tile(lp)
    row = lambda o, t: (t, 0)
    (a,), _ = seq_fwd(name, _norm_fn, (1, lp // tr), [Arg(g, (1, d), lambda o, t: (0, 0), shared=True)],
                      [Arg(mask, (tr, 1), row)], [Arg(h, (tr, d), row)], [((lp, d), out_dtype, (tr, d), row)])
    return a


def norm_bwd(name, h, g, mask, da, dskip):
    lp, d = h.shape
    tr = _row_tile(lp)
    row = lambda o, t: (t, 0)
    (dh,), (dg,) = seq_bwd(name, _norm_b_fn, (1, lp // tr), [Arg(g, (1, d), lambda o, t: (0, 0), shared=True)],
                           [Arg(mask, (tr, 1), row)], [Arg(h, (tr, d), row)],
                           [Arg(da, (tr, d), row), Arg(dskip, (tr, d), row)])
    return dh, dg


def _ffn_tile(lp):
    return 320 if (lp % 320 == 0 and lp > 320) else 64


def _conv_rows(ext, w, b, n):
    u2 = ext[8:8 + n]
    u1 = pltpu.roll(ext, 1, 0)[8:8 + n]
    u0 = pltpu.roll(ext, 2, 0)[8:8 + n]
    return w[2] * u2 + w[1] * u1 + w[0] * u0 + b, (u0, u1, u2)


def ffn_up_core(name, a, wup, cw, cb):
    lp, kd = a.shape
    _, nj, _, fb = wup.shape
    tr = 416 if (lp % 416 == 0 and lp > 416) else _ffn_tile(lp)
    nt = lp // tr

    def body(a_ref, wu_ref, w_ref, b_ref, u_ref, z_ref, u_s, halo_s):
        i = pl.program_id(1)

        @pl.when(i == 0)
        def _():
            u_s[...] = jnp.zeros_like(u_s)
            halo_s[...] = jnp.zeros_like(halo_s)

        old = (i + 1) % 2
        cs = []
        for s in range(2):
            tile = u_s[old, s]
            ext = jnp.concatenate([halo_s[s], tile], axis=0)
            c, _ = _conv_rows(ext, w_ref[s], b_ref[s], tr)
            cs.append(c)
            halo_s[s] = tile[tr - 8:]
        z_ref[...] = (_silu(cs[0]) * cs[1]).astype(z_ref.dtype)
        for s in range(2):
            un = lax.dot_general(a_ref[...], wu_ref[s], NN, preferred_element_type=F32)
            u_ref[s] = un
            u_s[i % 2, s] = un

    cur = lambda j, i: (0, j, jnp.minimum(i, nt - 1), 0)
    return pl.pallas_call(
        body, name=name, grid=(nj, nt + 1),
        in_specs=[pl.BlockSpec((tr, kd), lambda j, i: (jnp.minimum(i, nt - 1), 0)),
                  pl.BlockSpec((2, None, kd, fb), lambda j, i: (0, j, 0, 0)),
                  pl.BlockSpec((2, None, 3, 1, fb), lambda j, i: (0, j, 0, 0, 0)),
                  pl.BlockSpec((2, None, 1, fb), lambda j, i: (0, j, 0, 0))],
        out_specs=[pl.BlockSpec((2, None, tr, fb), cur),
                   pl.BlockSpec((None, tr, fb), lambda j, i: (j, jnp.maximum(i - 1, 0), 0))],
        out_shape=[SDS((2, nj, lp, fb), F32), SDS((nj, lp, fb), BF)],
        scratch_shapes=[pltpu.VMEM((2, 2, tr, fb), F32), pltpu.VMEM((2, 8, fb), F32)],
        compiler_params=_cp(2))(a, wup, cw, cb)


def ffn_core_bwd(name, u, dz, cw, cb):
    _, nj, lp, fb = u.shape
    tr = _ffn_tile(lp)
    nt = lp // tr
    nb8 = lp // 8

    def body(u_ref, up_ref, un_ref, dz_ref, dzn_ref, w_ref, b_ref, du_ref, dw_ref, db_ref):
        i = pl.program_id(1)

        @pl.when(i == 0)
        def _():
            dw_ref[...] = jnp.zeros_like(dw_ref)
            db_ref[...] = jnp.zeros_like(db_ref)

        prev = jnp.where(i > 0, up_ref[...], 0.0)
        nxt = jnp.where(i < nt - 1, un_ref[...], 0.0)
        dz_e = jnp.concatenate([dz_ref[...], jnp.where(i < nt - 1, dzn_ref[...], 0.0)], axis=0)
        n = tr + 8
        cs, taps = [], []
        for s in range(2):
            ext = jnp.concatenate([prev[s], u_ref[s], nxt[s]], axis=0)
            c, tp = _conv_rows(ext, w_ref[s], b_ref[s], n)
            cs.append(c)
            taps.append(tp)
        sg = jax.nn.sigmoid(cs[0])
        dcs = [dz_e * cs[1] * sg * (1.0 + cs[0] * (1.0 - sg)), dz_e * cs[0] * sg]
        for s in range(2):
            dc = dcs[s]
            w = w_ref[s]
            d1 = pltpu.roll(dc, n - 1, 0)[:tr]
            d2 = pltpu.roll(dc, n - 2, 0)[:tr]
            dcm = dc[:tr]
            du_ref[s] = w[2] * dcm + w[1] * d1 + w[0] * d2
            for k in range(3):
                dw_ref[s, k] += jnp.sum(dcm * taps[s][k][:tr], axis=0, keepdims=True)
            db_ref[s] += jnp.sum(dcm, axis=0, keepdims=True)

    return pl.pallas_call(
        body, name=name, grid=(nj, nt),
        in_specs=[pl.BlockSpec((2, None, tr, fb), lambda j, i: (0, j, i, 0)),
                  pl.BlockSpec((2, None, 8, fb), lambda j, i: (0, j, jnp.maximum(i * (tr // 8) - 1, 0), 0)),
                  pl.BlockSpec((2, None, 8, fb), lambda j, i: (0, j, jnp.minimum((i + 1) * (tr // 8), nb8 - 1), 0)),
                  pl.BlockSpec((None, tr, fb), lambda j, i: (j, i, 0)),
                  pl.BlockSpec((None, 8, fb), lambda j, i: (j, jnp.minimum((i + 1) * (tr // 8), nb8 - 1), 0)),
                  pl.BlockSpec((2, None, 3, 1, fb), lambda j, i: (0, j, 0, 0, 0)),
                  pl.BlockSpec((2, None, 1, fb), lambda j, i: (0, j, 0, 0))],
        out_specs=[pl.BlockSpec((2, None, tr, fb), lambda j, i: (0, j, i, 0)),
                   pl.BlockSpec((2, None, 3, 1, fb), lambda j, i: (0, j, 0, 0, 0)),
                   pl.BlockSpec((2, None, 1, fb), lambda j, i: (0, j, 0, 0))],
        out_shape=[SDS(u.shape, F32), SDS(cw.shape, F32), SDS(cb.shape, F32)],
        compiler_params=_cp(2))(u, u, u, dz, dz, cw, cb)


def ffn_fwd(i, h, mask, w):
    a = norm_fwd(f"ffn{i}_norm", h, w["ng"], mask)
    u, z = ffn_up_core(f"ffn{i}_up_core", a, w["up"].reshape(2, 4, D, FFN_B), w["cw"], w["cb"])
    h2 = lin_bi(f"ffn{i}_down", z, w["down"], res=h)
    return h2, (h, a, u, z)


def ffn_bwd(i, dh2, mask, w, saved):
    h, a, u, z = saved
    lp = h.shape[0]
    g = {}
    g["down"] = wgrad_bi(f"ffn{i}_dwdown", z, dh2)
    dz = lin_t_bo(f"ffn{i}_dz", dh2, w["down"])
    du, g["cw"], g["cb"] = ffn_core_bwd(f"ffn{i}_core_b", u, dz, w["cw"], w["cb"])
    du = du.reshape(8, lp, FFN_B)
    g["up"] = wgrad_bo(f"ffn{i}_dwup", a, du)
    da = lin_t_bi(f"ffn{i}_da", du, w["up"])
    dh, g["ng"] = norm_bwd(f"ffn{i}_norm_b", h, w["ng"], mask, da, dh2)
    return dh, g


HG_HB = 4


def _hgrn_fn(p, c, k, x):
    lb, go = p
    q, f, iv, g = x[0][0], x[0][1], x[0][2], x[0][3]
    (st_all,) = k
    qs = _silu(q)
    forget = lb + (1.0 - lb) * jax.nn.sigmoid(f)
    logf = jnp.log(forget)
    kk = 1.0 - forget
    gc_all = _seg_cumsum(logf, HG_C)
    r = lax.broadcasted_iota(jnp.int32, (HG_C, HG_C), 0)
    cc = lax.broadcasted_iota(jnp.int32, (HG_C, HG_C), 1)
    ns = CH // HG_C
    cells = [(j, s) for j in range(HG_HB) for s in range(ns)]

    def blk(t, j, s):
        return t[HG_C * s:HG_C * (s + 1), HG_D * j:HG_D * (j + 1)]

    gl = {c: jnp.sum(blk(logf, *c), axis=0, keepdims=True) for c in cells}
    qd = {c: blk(qs, *c) * jnp.exp(blk(gc_all, *c)) for c in cells}
    ki = {c: blk(kk, *c) * jnp.exp(-blk(gc_all, *c)) for c in cells}
    up = {c: mm_tn(blk(iv, *c), blk(kk, *c) * jnp.exp(gl[c] - blk(gc_all, *c))) for c in cells}
    st, sts = {}, []
    for j in range(HG_HB):
        cur = st_all[j]
        for s in range(ns):
            st[(j, s)] = cur
            cur = cur * jnp.exp(gl[(j, s)]) + up[(j, s)]
        sts.append(cur)
    both = {c: mm_nt(qd[c], jnp.concatenate([st[c], ki[c]], axis=0)) for c in cells}
    oc = {c: mm(jnp.where(r >= cc, both[c][:, HG_D:], 0.0), blk(iv, *c)) + both[c][:, :HG_D] for c in cells}
    zs = []
    for j in range(HG_HB):
        o = jnp.concatenate([oc[(j, s)] for s in range(ns)], axis=0)
        zs.append(_rms(o, go) * _silu(g[:, HG_D * j:HG_D * (j + 1)]))
    return (jnp.stack(sts, axis=0),), (jnp.concatenate(zs, axis=1),)


def _hgrn_args(u4, lb, go):
    lp = u4.shape[2]
    wb = HG_HB * HG_D
    xs = [Arg(u4, (4, None, CH, wb), lambda o, t: (0, o, t, 0))]
    ps = [Arg(lb, (1, wb), lambda o, t: (0, o)), Arg(go, (1, HG_D), lambda o, t: (0, 0), shared=True)]
    return (HG_H // HG_HB, lp // CH), ps, xs


def hgrn_fwd(h, mask, w):
    lp = h.shape[0]
    a = norm_fwd("hgrn_norm", h, w["ng"], mask)
    u4 = lin_bo("hgrn_in", a, w["win"]).reshape(4, HG_H // HG_HB, lp, HG_HB * HG_D)
    grid, ps, xs = _hgrn_args(u4, w["lb"], w["go"])
    (z,), (st,) = seq_fwd("hgrn_core", _hgrn_fn, grid, ps, [], xs,
                          [((lp, D), BF, (CH, HG_HB * HG_D), lambda o, t: (t, o))], carries=[(HG_HB, HG_D, HG_D)])
    h2 = lin("hgrn_out", z, w["wo"], res=h)
    return h2, (h, a, u4, z, st)


def hgrn_bwd(dh2, mask, w, saved):
    h, a, u4, z, st = saved
    lp = h.shape[0]
    g = {}
    g["wo"] = wgrad("hgrn_dwo", z, dh2)
    dz = lin_t("hgrn_dz", dh2, w["wo"])
    grid, ps, xs = _hgrn_args(u4, w["lb"], w["go"])
    (du4,), (g["lb"], g["go"]) = seq_bwd("hgrn_core_b", _hgrn_fn, grid, ps, [], xs,
                                         [Arg(dz, (CH, HG_HB * HG_D), lambda o, t: (t, o))], saved=[st],
                                         carries=[(HG_HB, HG_D, HG_D)])
    du = du4.reshape(N_DEV, lp, HG_HB * HG_D)
    g["win"] = wgrad_bo("hgrn_dwin", a, du)
    da = lin_t_bi("hgrn_da", du, w["win"])
    dh, g["ng"] = norm_bwd("hgrn_norm_b", h, w["ng"], mask, da, dh2)
    return dh, g


S5_W = S5_SG * S5_P


def s5_tables(lam_re, lam_im, log_dt, b_re, b_im, c_re, c_im):
    dt = jnp.exp(log_dt)[:, None]
    mag = jnp.exp(lam_re * dt)
    abar_re = mag * jnp.cos(lam_im * dt)
    abar_im = mag * jnp.sin(lam_im * dt)
    den = lam_re * lam_re + lam_im * lam_im
    zoh_re = ((abar_re - 1.0) * lam_re + abar_im * lam_im) / den
    zoh_im = (abar_im * lam_re - (abar_re - 1.0) * lam_im) / den
    bbar_re = zoh_re[..., None] * b_re - zoh_im[..., None] * b_im
    bbar_im = zoh_re[..., None] * b_im + zoh_im[..., None] * b_re
    eye = jnp.eye(S5_SG, dtype=F32)

    def blockdiag_in(b):
        t = b.reshape(N_DEV, S5_SG, S5_P, S5_K).transpose(0, 1, 3, 2)
        return jnp.einsum("jakp,ab->jakbp", t, eye).reshape(N_DEV, S5_SG * S5_K, S5_W)

    def blockdiag_out(c):
        t = c.reshape(N_DEV, S5_SG, S5_K, S5_P).transpose(0, 1, 3, 2)
        return jnp.einsum("japk,ab->japbk", t, eye).reshape(N_DEV, S5_W, S5_SG * S5_K)

    wb = jnp.concatenate([blockdiag_in(bbar_re), blockdiag_in(bbar_im)], axis=2)
    wc = jnp.concatenate([blockdiag_out(c_re), -blockdiag_out(c_im)], axis=1)

    def powers(n):
        steps = n[:, None, None] * dt[None]
        pm = jnp.exp(lam_re[None] * steps)
        pr = (pm * jnp.cos(lam_im[None] * steps)).reshape(-1, N_DEV, S5_W).transpose(1, 0, 2)
        pi = (pm * jnp.sin(lam_im[None] * steps)).reshape(-1, N_DEV, S5_W).transpose(1, 0, 2)
        return jnp.concatenate([pr, pi], axis=2)

    apow = powers(2.0 ** jnp.arange(6, dtype=F32))[:, :, None, :]
    ptab = powers(jnp.arange(CH, dtype=F32) + 1.0)
    return wb, wc, apow, ptab


def _cmul(ar, ai, xr, xi):
    return ar * xr - ai * xi, ar * xi + ai * xr


S5_BB = 8


def _s5_fn(p, c, k, x):
    wb, wc, apow, ptab, dsk = p
    (a,) = x
    (x0,) = k
    blocks = range(S5_BB)
    aj = [a[:, 128 * j:128 * (j + 1)] for j in blocks]
    bu = [mm(aj[j], wb[j]) for j in blocks]
    xxs, x0n = [], []
    for j in blocks:
        xr, xi = bu[j][:, :S5_W], bu[j][:, S5_W:]
        for s in range(6):
            asr, asi = apow[j][s][:, :S5_W], apow[j][s][:, S5_W:]
            dr, di = _cmul(asr, asi, shift_rows(xr, 1 << s), shift_rows(xi, 1 << s))
            xr, xi = xr + dr, xi + di
        dr, di = _cmul(ptab[j][:, :S5_W], ptab[j][:, S5_W:], x0[j][:, :S5_W], x0[j][:, S5_W:])
        xx = jnp.concatenate([xr + dr, xi + di], axis=1)
        last = lax.broadcasted_iota(jnp.int32, xx.shape, 0) == CH - 1
        x0n.append(jnp.sum(jnp.where(last, xx, 0.0), axis=0, keepdims=True))
        xxs.append(xx)
    y = jnp.concatenate([mm(xxs[j], wc[j]) for j in blocks], axis=1)
    return (jnp.stack(x0n, axis=0),), (jax.nn.gelu(y + dsk * a),)


def _s5_args(a, tb, dsk):
    lp = a.shape[0]
    wb, wc, apow, ptab = tb
    ps = [Arg(wb, (S5_BB, 128, 2 * S5_W), lambda o, t: (o, 0, 0)), Arg(wc, (S5_BB, 2 * S5_W, 128), lambda o, t: (o, 0, 0)),
          Arg(apow, (S5_BB, 6, 1, 2 * S5_W), lambda o, t: (o, 0, 0, 0)),
          Arg(ptab, (S5_BB, CH, 2 * S5_W), lambda o, t: (o, 0, 0)), Arg(dsk, (1, 128 * S5_BB), lambda o, t: (0, o))]
    xs = [Arg(a, (CH, 128 * S5_BB), lambda o, t: (t, o))]
    return (N_DEV // S5_BB, lp // CH), ps, xs


def _glu_res_fn(p, c, k, x):
    h, vg = x
    return (), ((h + vg[:, :D] * jax.nn.sigmoid(vg[:, D:])) * c[0],)


def _glu_args(h, vg, mask):
    lp = h.shape[0]
    tr = _row_tile(lp)
    row = lambda o, t: (t, 0)
    return (1, lp // tr), [Arg(mask, (tr, 1), row)], [Arg(h, (tr, D), row), Arg(vg, (tr, 2 * D), row)], tr


def s5_fwd(h, mask, w):
    lp = h.shape[0]
    a = norm_fwd("s5_norm", h, w["ng"], mask, out_dtype=F32)
    grid, ps, xs = _s5_args(a, w["tb"], w["dsk"])
    (z,), (st,) = seq_fwd("s5_core", _s5_fn, grid, ps, [], xs,
                          [((lp, D), BF, (CH, 128 * S5_BB), lambda o, t: (t, o))], carries=[(S5_BB, 1, 2 * S5_W)])
    vg = lin("s5_glu", z, w["wglu"])
    grid2, cs, xs2, tr = _glu_args(h, vg, mask)
    (h2,), _ = seq_fwd("s5_res", _glu_res_fn, grid2, [], cs, xs2, [((lp, D), F32, (tr, D), lambda o, t: (t, 0))])
    return h2, (h, a, z, vg, st)


def s5_bwd(dh2, mask, w, saved):
    h, a, z, vg, st = saved
    g = {}
    grid2, cs, xs2, tr = _glu_args(h, vg, mask)
    (dskip, dvg), _ = seq_bwd("s5_res_b", _glu_res_fn, grid2, [], cs, xs2, [Arg(dh2, (tr, D), lambda o, t: (t, 0))])
    g["wglu"] = wgrad("s5_dwglu", z, dvg)
    dz = lin_t("s5_dz", dvg, w["wglu"])
    grid, ps, xs = _s5_args(a, w["tb"], w["dsk"])
    (da,), dps = seq_bwd("s5_core_b", _s5_fn, grid, ps, [], xs, [Arg(dz, (CH, 128 * S5_BB), lambda o, t: (t, o))],
                         saved=[st], carries=[(S5_BB, 1, 2 * S5_W)])
    g["tb"] = tuple(dps[:4])
    g["dsk"] = dps[4]
    dh, g["ng"] = norm_bwd("s5_norm_b", h, w["ng"], mask, da, dskip)
    return dh, g


def _rope_angles(lp, dim):
    pos = np.maximum(np.arange(lp, dtype=np.float32) - PAD, 0.0).astype(np.float32)
    inv = (1.0 / (ROPE_BASE ** (np.arange(0, dim, 2, dtype=np.float32) / dim))).astype(np.float32)
    return (pos[:, None] * inv[None, :]).astype(np.float32)


def ret_consts(lp):
    f = np.float32
    ang = _rope_angles(lp, RET_DK)
    lg = np.log(1.0 - np.exp2(-5.0 - np.arange(RET_H, dtype=f))).astype(f)
    p = np.arange(CH, dtype=f)
    diff = p[:, None] - p[None, :]
    decay = np.where(diff >= 0, np.exp(diff[None] * lg[:, None, None]), 0.0).astype(f)
    qd = np.exp((p[None, :] + 1.0) * lg[:, None])[..., None].astype(f)
    kd = np.exp((CH - 1.0 - p[None, :]) * lg[:, None])[..., None].astype(f)
    cd = np.exp(CH * lg)[:, None, None].astype(f)
    return np.cos(ang).astype(f), np.sin(ang).astype(f), decay, qd, kd, cd


def _ret_fn(p, c, k, x):
    (gn,) = p
    cos, sin, decay, qd, kd, cd = c
    (st,) = k
    (u,) = x
    hd = RET_DK // 2
    qk_w = RET_H * RET_DK
    heads = range(RET_H)

    def rope(t):
        t1, t2 = t[:, :hd], t[:, hd:]
        return jnp.concatenate([t1 * cos - t2 * sin, t1 * sin + t2 * cos], axis=1)

    qr = [rope(u[:, RET_DK * h:RET_DK * (h + 1)]) for h in heads]
    kr = [rope(u[:, qk_w + RET_DK * h:qk_w + RET_DK * (h + 1)]) * (RET_DK ** -0.5) for h in heads]
    v = [u[:, 2 * qk_w + RET_DV * h:2 * qk_w + RET_DV * (h + 1)] for h in heads]
    scores = [mm_nt(qr[h], kr[h]) for h in heads]
    inter = [mm(qr[h] * qd[h], st[h]) for h in heads]
    st_new = jnp.stack([st[h] * cd[h] + mm_tn(kr[h] * kd[h], v[h]) for h in heads], axis=0)
    o = [mm(scores[h] * decay[h], v[h]) + inter[h] for h in heads]
    zs = []
    for h in heads:
        mu = jnp.mean(o[h], axis=-1, keepdims=True)
        var = jnp.mean(jnp.square(o[h] - mu), axis=-1, keepdims=True)
        gate = u[:, 2 * qk_w + RET_H * RET_DV + RET_DV * h:2 * qk_w + RET_H * RET_DV + RET_DV * (h + 1)]
        zs.append((o[h] - mu) * lax.rsqrt(var + EPS) * gn[:, RET_DV * h:RET_DV * (h + 1)] * _silu(gate))
    return (st_new,), (jnp.concatenate(zs, axis=1),)


def _ret_args(u, gn, rc):
    lp, uw = u.shape
    cos, sin, decay, qd, kd, cd = rc
    full = lambda o, t: (0, 0, 0)
    ps = [Arg(gn, (1, RET_H * RET_DV), lambda o, t: (0, 0))]
    cs = [Arg(cos, (CH, RET_DK // 2), lambda o, t: (t, 0)), Arg(sin, (CH, RET_DK // 2), lambda o, t: (t, 0)),
          Arg(decay, (RET_H, CH, CH), full), Arg(qd, (RET_H, CH, 1), full), Arg(kd, (RET_H, CH, 1), full),
          Arg(cd, (RET_H, 1, 1), full)]
    xs = [Arg(u, (CH, uw), lambda o, t: (t, 0))]
    return (1, lp // CH), ps, cs, xs


def ret_fwd(h, mask, w):
    lp = h.shape[0]
    a = norm_fwd("ret_norm", h, w["ng"], mask)
    u = lin("ret_in", a, w["win"])
    grid, ps, cs, xs = _ret_args(u, w["gn"], w["rc"])
    (z,), (st,) = seq_fwd("ret_core", _ret_fn, grid, ps, cs, xs,
                          [((lp, 2 * D), BF, (CH, RET_H * RET_DV), lambda o, t: (t, 0))],
                          carries=[(RET_H, RET_DK, RET_DV)], save_dtype=BF)
    h2 = lin("ret_out", z, w["wo"], res=h)
    return h2, (h, a, u, z, st)


def ret_bwd(dh2, mask, w, saved):
    h, a, u, z, st = saved
    g = {}
    g["wo"] = wgrad("ret_dwo", z, dh2)
    dz = lin_t("ret_dz", dh2, w["wo"])
    grid, ps, cs, xs = _ret_args(u, w["gn"], w["rc"])
    (du,), (g["gn"],) = seq_bwd("ret_core_b", _ret_fn, grid, ps, cs, xs,
                                [Arg(dz, (CH, RET_H * RET_DV), lambda o, t: (t, 0))], saved=[st],
                                carries=[(RET_H, RET_DK, RET_DV)])
    g["win"] = wgrad("ret_dwin", a, du)
    da = lin_t("ret_da", du, w["win"])
    dh, g["ng"] = norm_bwd("ret_norm_b", h, w["ng"], mask, da, dh2)
    return dh, g


def mla_consts(lp):
    ang = _rope_angles(lp, MLA_ROPE)
    cos = np.concatenate([np.cos(ang), np.cos(ang)], axis=1).astype(np.float32)
    sin = np.concatenate([np.sin(ang), np.sin(ang)], axis=1).astype(np.float32)
    hd = MLA_ROPE // 2
    i = np.arange(hd)
    rot = np.zeros((MLA_ROPE, MLA_ROPE), np.float32)
    rot[hd + i, i] = -1.0
    rot[i, hd + i] = 1.0
    return cos, sin, rot


def _mla_prep1_fn(p, c, k, x):
    gq, gkv = p
    (down,) = x
    return (), (_rms(down[:, :MLA_QL], gq), _rms(down[:, MLA_QL:MLA_QL + MLA_KVL], gkv), down[:, MLA_QL + MLA_KVL:])


def _mla_prep2(p, c, x):
    gq, gk = p
    cos, sin, rot = c
    q, kv, kpe = x
    qn = _rms(q, gq)
    qn_n, qn_r = qn[:, :MLA_NOPE], qn[:, MLA_NOPE:]
    qo = jnp.concatenate([qn_n, qn_r * cos + cright(qn_r, rot) * sin], axis=1)
    kn = kv[:, :MLA_NOPE]
    ms = (jnp.sum(kn * kn, axis=-1, keepdims=True) + jnp.sum(kpe * kpe, axis=-1, keepdims=True)) / MLA_QK
    r = lax.rsqrt(ms + EPS)
    kr = kpe * r * gk[:, MLA_NOPE:]
    ko = jnp.concatenate([kn * r * gk[:, :MLA_NOPE], kr * cos + cright(kr, rot) * sin], axis=1)
    return qo, ko, kv[:, MLA_NOPE:]


def _mla_prep2_fn(p, c, k, x):
    return (), _mla_prep2(p, c, x)[:2]


def _mla_prep2_b_fn(p, c, k, x):
    return (), _mla_prep2(p, c, x)


def _prep1_args(down, gq, gkv):
    lp = down.shape[0]
    tr = _row_tile(lp)
    ps = [Arg(gq, (1, MLA_QL), lambda o, t: (0, 0), shared=True), Arg(gkv, (1, MLA_KVL), lambda o, t: (0, 0), shared=True)]
    return (1, lp // tr), ps, [Arg(down, (tr, down.shape[1]), lambda o, t: (t, 0))], tr


def _prep2_args(qraw, kvraw, kpe, gq, gk, mc):
    lp = kpe.shape[0]
    tr = _row_tile(lp)
    cos, sin, rot = mc
    ps = [Arg(gq, (1, MLA_QK), lambda o, t: (0, 0), shared=True), Arg(gk, (1, MLA_QK), lambda o, t: (0, 0), shared=True)]
    cs = [Arg(cos, (tr, MLA_ROPE), lambda o, t: (o, 0)), Arg(sin, (tr, MLA_ROPE), lambda o, t: (o, 0)),
          Arg(rot, (MLA_ROPE, MLA_ROPE), lambda o, t: (0, 0))]
    xs = [Arg(qraw, (None, tr, MLA_QK), lambda o, t: (t, o, 0)), Arg(kvraw, (None, tr, MLA_NOPE + MLA_V), lambda o, t: (t, o, 0)),
          Arg(kpe, (tr, MLA_ROPE), lambda o, t: (o, 0), acc=True)]
    return (lp // tr, MLA_H), ps, cs, xs, tr


ATT_HB = 2


def _attn_tile(lp):
    return 832 if (lp % 832 == 0 and lp > 832) else 64


def _attn_mask(qi, ki, ta):
    rows = qi * ta + lax.broadcasted_iota(jnp.int32, (ta, ta), 0)
    cols = ki * ta + lax.broadcasted_iota(jnp.int32, (ta, ta), 1)
    return (cols >= PAD) & ((cols // CH) <= (rows // CH))


def attn_fwd(q, k, kv):
    nh, lp, dq = q.shape
    ta = _attn_tile(lp)
    nb = lp // ta
    scale = MLA_QK ** -0.5

    hb = ATT_HB

    def body(q_ref, k_ref, v_ref, o_ref, lse_ref, m_s, l_s, acc_s):
        qi, ki = pl.program_id(1), pl.program_id(2)

        @pl.when(ki == 0)
        def _():
            m_s[...] = jnp.full_like(m_s, NEG)
            l_s[...] = jnp.zeros_like(l_s)
            acc_s[...] = jnp.zeros_like(acc_s)

        def step(masked):
            ss = [_bdot(q_ref[j], k_ref[j], 1, 1) * scale for j in range(hb)]
            ps = []
            for j in range(hb):
                s = jnp.where(_attn_mask(qi, ki, ta), ss[j], NEG) if masked else ss[j]
                m_new = jnp.maximum(m_s[j], jnp.max(s, axis=-1, keepdims=True))
                p = jnp.exp(s - m_new)
                alpha = jnp.exp(m_s[j] - m_new)
                l_s[j] = alpha * l_s[j] + jnp.sum(p, axis=-1, keepdims=True)
                m_s[j] = m_new
                ps.append((p, alpha))
            for j in range(hb):
                acc_s[j] = ps[j][1] * acc_s[j] + _bdot(ps[j][0], v_ref[j], 1, 0)

        pl.when((ki == qi) | (ki == 0))(functools.partial(step, True))
        pl.when((ki < qi) & (ki > 0))(functools.partial(step, False))

        @pl.when(ki == nb - 1)
        def _():
            for j in range(hb):
                o_ref[:, MLA_V * j:MLA_V * (j + 1)] = (acc_s[j] / l_s[j]).astype(o_ref.dtype)
                lse_ref[j] = m_s[j] + jnp.log(l_s[j])

    return pl.pallas_call(
        body, name="mla_attn", grid=(nh // hb, nb, nb),
        in_specs=[pl.BlockSpec((hb, ta, dq), lambda h, qi, ki: (h, qi, 0)),
                  pl.BlockSpec((hb, ta, dq), lambda h, qi, ki: (h, jnp.minimum(ki, qi), 0)),
                  pl.BlockSpec((hb, ta, MLA_V), lambda h, qi, ki: (h, jnp.minimum(ki, qi), 1))],
        out_specs=[pl.BlockSpec((ta, hb * MLA_V), lambda h, qi, ki: (qi, h)),
                   pl.BlockSpec((hb, ta, 1), lambda h, qi, ki: (h, qi, 0))],
        out_shape=[SDS((lp, nh * MLA_V), BF), SDS((nh, lp, 1), F32)],
        scratch_shapes=[pltpu.VMEM((hb, ta, 1), F32), pltpu.VMEM((hb, ta, 1), F32), pltpu.VMEM((hb, ta, MLA_V), F32)],
        compiler_params=_cp(3))(q, k, kv)


def attn_bwd(q, k, kv, o, do, lse):
    nh, lp, dq = q.shape
    ta = _attn_tile(lp)
    nb = lp // ta
    scale = MLA_QK ** -0.5

    def body(q_ref, k_ref, v_ref, o_ref, do_ref, lse_ref, dq_ref, dk_ref, dv_ref, dk_s, dv_s):
        ki, qi = pl.program_id(1), pl.program_id(2)

        @pl.when((ki == 0) & (qi == 0))
        def _():
            dq_ref[...] = jnp.zeros_like(dq_ref)

        @pl.when(qi == 0)
        def _():
            dk_s[...] = jnp.zeros_like(dk_s)
            dv_s[...] = jnp.zeros_like(dv_s)

        def step(masked):
            dov = do_ref[...]
            s = _bdot(q_ref[...], k_ref[...], 1, 1) * scale
            dp = _bdot(dov, v_ref[...], 1, 1)
            if masked:
                s = jnp.where(_attn_mask(qi, ki, ta), s, NEG)
            p = jnp.exp(s - lse_ref[...])
            delta = jnp.sum(dov * o_ref[...].astype(F32), axis=-1, keepdims=True)
            dv_s[...] += _bdot(p, dov, 0, 0)
            ds = p * (dp - delta) * scale
            rows = pl.ds(pl.multiple_of(qi * ta, ta), ta)
            dq_ref[rows, :] += _bdot(ds, k_ref[...], 1, 0)
            dk_s[...] += _bdot(ds, q_ref[...], 0, 0)

        pl.when((ki == qi) | (ki == 0))(functools.partial(step, True))
        pl.when((ki < qi) & (ki > 0))(functools.partial(step, False))

        @pl.when(qi == nb - 1)
        def _():
            dk_ref[...] = dk_s[...]
            dv_ref[...] = dv_s[...]

    qmap = lambda h, ki, qi: (h, jnp.maximum(qi, ki), 0)
    return pl.pallas_call(
        body, name="mla_attn_b", grid=(nh, nb, nb),
        in_specs=[pl.BlockSpec((None, ta, dq), qmap),
                  pl.BlockSpec((None, ta, dq), lambda h, ki, qi: (h, ki, 0)),
                  pl.BlockSpec((None, ta, MLA_V), lambda h, ki, qi: (h, ki, 1)),
                  pl.BlockSpec((ta, MLA_V), lambda h, ki, qi: (jnp.maximum(qi, ki), h)),
                  pl.BlockSpec((ta, MLA_V), lambda h, ki, qi: (jnp.maximum(qi, ki), h)),
                  pl.BlockSpec((None, ta, 1), qmap)],
        out_specs=[pl.BlockSpec((None, lp, dq), lambda h, ki, qi: (h, 0, 0)),
                   pl.BlockSpec((None, ta, dq), lambda h, ki, qi: (h, ki, 0)),
                   pl.BlockSpec((None, ta, MLA_V), lambda h, ki, qi: (h, ki, 0))],
        out_shape=[SDS((nh, lp, dq), F32), SDS((nh, lp, dq), F32), SDS((nh, lp, MLA_V), F32)],
        scratch_shapes=[pltpu.VMEM((ta, dq), F32), pltpu.VMEM((ta, MLA_V), F32)],
        compiler_params=_cp(3))(q, k, kv, o, do, lse)


def mla_fwd(h, mask, w):
    lp = h.shape[0]
    a = norm_fwd("mla_norm", h, w["ng"], mask)
    down = lin("mla_down", a, w["wdown"])
    grid, ps, xs, tr = _prep1_args(down, w["gcq"], w["gckv"])
    row = lambda o, t: (t, 0)
    (cq, ckv, kpe), _ = seq_fwd("mla_prep1", _mla_prep1_fn, grid, ps, [], xs,
                                [((lp, MLA_QL), BF, (tr, MLA_QL), row), ((lp, MLA_KVL), BF, (tr, MLA_KVL), row),
                                 ((lp, MLA_ROPE), F32, (tr, MLA_ROPE), row)])
    qraw = lin_bo("mla_uq", cq, w["wuq"])
    kvraw = lin_bo("mla_ukv", ckv, w["wukv"])
    grid, ps, cs, xs, tr = _prep2_args(qraw, kvraw, kpe, w["gq"], w["gk"], w["mc"])
    hm = lambda o, t: (t, o, 0)
    (q, k), _ = seq_fwd("mla_prep2", _mla_prep2_fn, grid, ps, cs, xs,
                        [((MLA_H, lp, MLA_QK), BF, (None, tr, MLA_QK), hm), ((MLA_H, lp, MLA_QK), BF, (None, tr, MLA_QK), hm)])
    o, lse = attn_fwd(q, k, kvraw)
    h2 = lin("mla_out", o, w["wo"], res=h)
    return h2, (h, a, down, cq, ckv, kpe, qraw, kvraw, q, k, o, lse)


def mla_bwd(dh2, mask, w, saved, emit):
    h, a, down, cq, ckv, kpe, qraw, kvraw, q, k, o, lse = saved
    lp = h.shape[0]
    g = {}
    g["wo"] = wgrad("mla_dwo", o, dh2)
    do = lin_t("mla_do", dh2, w["wo"])
    do = emit("wo", [g["wo"]], do)
    dq, dk, dv = attn_bwd(q, k, kvraw, o, do, lse)
    grid, ps, cs, xs, tr = _prep2_args(qraw, kvraw, kpe, w["gq"], w["gk"], w["mc"])
    hm = lambda o, t: (t, o, 0)
    (dqraw, dkvraw, dkpe), (g["gq"], g["gk"]) = seq_bwd(
        "mla_prep2_b", _mla_prep2_b_fn, grid, ps, cs, xs,
        [Arg(dq, (None, tr, MLA_QK), hm), Arg(dk, (None, tr, MLA_QK), hm), Arg(dv, (None, tr, MLA_V), hm)])
    g["wuq"] = wgrad_bo("mla_dwuq", cq, dqraw)
    dcq = lin_t_bi("mla_dcq", dqraw, w["wuq"])
    g["wukv"] = wgrad_bo("mla_dwukv", ckv, dkvraw)
    dckv = lin_t_bi("mla_dckv", dkvraw, w["wukv"])
    dckv = emit("wu", [g["wuq"], g["wukv"]], dckv)
    grid, ps, xs, tr = _prep1_args(down, w["gcq"], w["gckv"])
    row = lambda o, t: (t, 0)
    (ddown,), (g["gcq"], g["gckv"]) = seq_bwd(
        "mla_prep1_b", _mla_prep1_fn, grid, ps, [], xs,
        [Arg(dcq, (tr, MLA_QL), row), Arg(dckv, (tr, MLA_KVL), row), Arg(dkpe, (tr, MLA_ROPE), row)])
    g["wdown"] = wgrad("mla_dwdown", a, ddown)
    da = lin_t("mla_da", ddown, w["wdown"])
    dh, g["ng"] = norm_bwd("mla_norm_b", h, w["ng"], mask, da, dh2)
    return dh, g


def loss_head(h, target):
    lp, d = h.shape
    assert OFF == CH

    def body(h_ref, t_ref, loss_ref, dh_ref):
        i = pl.program_id(0)

        @pl.when(i == 0)
        def _():
            loss_ref[...] = jnp.zeros_like(loss_ref)

        e = jnp.where(i > 0, h_ref[...] - t_ref[...], 0.0)
        loss_ref[...] += jnp.sum(e * e) * (0.5 / d)
        dh_ref[...] = e * (1.0 / d)

    return pl.pallas_call(
        body, name="loss_head", grid=(lp // CH,),
        in_specs=[pl.BlockSpec((CH, d), lambda i: (i, 0)), pl.BlockSpec((CH, d), lambda i: (jnp.maximum(i - 1, 0), 0))],
        out_specs=[pl.BlockSpec((8, 128), lambda i: (0, 0)), pl.BlockSpec((CH, d), lambda i: (i, 0))],
        out_shape=[SDS((8, 128), F32), SDS((lp, d), F32)], compiler_params=_cp(1))(h, target)


ADAM_LAND_BYTES = 20 * 1024 * 1024


def _adam_tile(r, c, nl):
    if r % 8:
        return r
    best = 8
    for t in range(8, r + 1, 8):
        if r % t == 0 and N_DEV * t * c * 4 * 2 * nl <= ADAM_LAND_BYTES:
            best = t
    return best


def adamw(name, lands, w, m, v):
    nl, r, c = w.shape
    tr = _adam_tile(r, c, nl)
    c1 = 1.0 / (1.0 - ADAM_B1 ** ADAM_STEP)
    c2 = 1.0 / (1.0 - ADAM_B2 ** ADAM_STEP)

    def body(*refs):
        l_refs = refs[:nl]
        w_ref, m_ref, v_ref, g_ref, d_ref, nm_ref, nv_ref = refs[nl:]
        layer = pl.program_id(0)
        for j in range(nl):
            @pl.when(layer == j)
            def _(j=j):
                g = l_refs[j][0]
                for i in range(1, N_DEV):
                    g = g + l_refs[j][i]
                g_ref[...] = g

        g = g_ref[...]
        nm = ADAM_B1 * m_ref[...] + (1.0 - ADAM_B1) * g
        nv = ADAM_B2 * v_ref[...] + (1.0 - ADAM_B2) * (g * g)
        nm_ref[...] = nm
        nv_ref[...] = nv
        d_ref[...] = -ADAM_LR * ((nm * c1) / (jnp.sqrt(nv * c2) + ADAM_EPS) + ADAM_WD * w_ref[...])

    blk = pl.BlockSpec((None, tr, c), lambda l, i: (l, i, 0))
    land_specs = [pl.BlockSpec((N_DEV, tr, c), lambda l, i, j=j: (0, jnp.where(l == j, i, 0), 0)) for j in range(nl)]
    return pl.pallas_call(
        body, name=name, grid=(nl, r // tr), in_specs=land_specs + [blk, blk, blk],
        out_specs=[blk, blk, blk, blk], out_shape=[SDS((nl, r, c), F32)] * 4, compiler_params=_cp(2))(*lands, w, m, v)


ANY = pl.BlockSpec(memory_space=pl.ANY)
MESH = pl.DeviceIdType.MESH


def _me():
    return lax.axis_index("x"), lax.axis_index("y"), lax.axis_index("c")


def _peers():
    x, y, c = _me()
    out = []
    for k in range(1, N_DEV):
        px = 1 - x if k & 4 else x
        py = 1 - y if k & 2 else y
        pc = 1 - c if k & 1 else c
        out.append(((px, py, pc), 4 * px + 2 * py + pc))
    return out


HBM_SPEC = pl.BlockSpec(memory_space=pltpu.HBM)
SEM_SPEC = pl.BlockSpec(memory_space=pltpu.SEMAPHORE)
DATAFLOW = pltpu.SideEffectType.DATAFLOW_SIDE_EFFECTING


def _my_index():
    return 4 * lax.axis_index("x") + 2 * lax.axis_index("y") + lax.axis_index("c")


def _hbm(a):
    return pltpu.with_memory_space_constraint(a, pltpu.HBM)


NP = N_DEV - 1


def _push_copy(x_ref, land_ref, send, recv, pid, src_idx, dst_idx, scatter):
    src = x_ref.at[src_idx] if scatter else x_ref
    return pltpu.make_async_remote_copy(src_ref=src, dst_ref=land_ref.at[dst_idx], send_sem=send, recv_sem=recv,
                                        device_id=pid, device_id_type=MESH)


def push_start(name, xs, me, scatter, carry=None):
    n = len(xs)
    lands = []
    for a in xs:
        own = lax.dynamic_index_in_dim(a, me, 0, keepdims=True) if scatter else a[None]
        z = lax.empty((N_DEV,) + own.shape[1:], a.dtype)
        lands.append(lax.dynamic_update_slice(z, own, (me,) + (0,) * (own.ndim - 1)))
    ns = 2 * NP * n
    ops = xs + lands + ([carry] if carry is not None else [])
    na = len(ops)

    def body(*refs):
        x_refs, land_refs = refs[:n], refs[n:2 * n]
        sems = refs[na:na + ns]
        token = refs[-1]
        x, y, c = _me()
        mine = 4 * x + 2 * y + c
        for i in range(n):
            for k, (pid, pidx) in enumerate(_peers()):
                s = 2 * (NP * i + k)
                _push_copy(x_refs[i], land_refs[i], sems[s], sems[s + 1], pid, pidx, mine, scatter).start()
        token[...] = jnp.zeros_like(token)

    out_shape = ([pltpu.SemaphoreType.DMA(())] * ns + [pltpu.HBM(a.shape, a.dtype) for a in ops]
                 + [SDS((8, 128), F32)])
    res = pl.pallas_call(
        body, name=name, out_shape=out_shape, in_specs=[HBM_SPEC] * na,
        out_specs=[SEM_SPEC] * ns + [HBM_SPEC] * na + [pl.BlockSpec(memory_space=pltpu.VMEM)],
        input_output_aliases={i: ns + i for i in range(na)},
        compiler_params=pltpu.CompilerParams(has_side_effects=DATAFLOW))(*[_hbm(a) for a in ops])
    sems, thru, token = res[:ns], res[ns:-1], res[-1]
    handles = [dict(x=thru[i], land=thru[n + i], sems=list(sems[2 * NP * i:2 * NP * (i + 1)]), token=token)
               for i in range(n)]
    return (handles, thru[2 * n]) if carry is not None else handles


def push_wait(name, hds, after, scatter):
    n = len(hds)
    ns = 2 * NP

    def body(*refs):
        x_refs, land_refs = refs[:n], refs[n:2 * n]
        sems = refs[2 * n:2 * n + ns * n]
        for i in range(n):
            for k, (pid, pidx) in enumerate(_peers()):
                cp = _push_copy(x_refs[i], land_refs[i], sems[ns * i + 2 * k], sems[ns * i + 2 * k + 1], pid, pidx, pidx,
                                scatter)
                cp.wait_send()
                cp.wait_recv()

    arrs = [hd["x"] for hd in hds] + [hd["land"] for hd in hds]
    sems = [s for hd in hds for s in hd["sems"]]
    res = pl.pallas_call(
        body, name=name, out_shape=[pltpu.HBM(a.shape, a.dtype) for a in arrs],
        in_specs=[HBM_SPEC] * (2 * n) + [SEM_SPEC] * (ns * n) + [ANY], out_specs=[HBM_SPEC] * (2 * n),
        input_output_aliases={i: i for i in range(2 * n)},
        compiler_params=pltpu.CompilerParams(has_side_effects=DATAFLOW))(*arrs, *sems, after)
    return list(res[n:])


WEIGHTS = ['meta_tokens', 'norm_mix_g', 'norm_ffn_g', 'mla_w_down', 'mla_cq_norm_g', 'mla_ckv_norm_g', 'mla_w_uq',
           'mla_w_ukv', 'mla_q_head_g', 'mla_k_head_g', 'mla_w_o', 'hgrn_w_in', 'hgrn_lb_logits', 'hgrn_o_norm_g',
           'hgrn_w_o', 's5_lam_re', 's5_lam_im', 's5_log_dt', 's5_b_re', 's5_b_im', 's5_c_re', 's5_c_im', 's5_d',
           's5_w_glu', 'ret_w_in', 'ret_gn_g', 'ret_w_o', 'ffn_w_up', 'ffn_conv_w', 'ffn_conv_b', 'ffn_w_down']
BIG = ['mla_w_down', 'mla_w_uq', 'mla_w_ukv', 'mla_w_o', 'hgrn_w_in', 'hgrn_w_o', 's5_w_glu', 'ret_w_in', 'ret_w_o',
       'ffn_w_up', 'ffn_w_down']
SMALL_SH = ['meta_tokens', 's5_d', 'ret_gn_g', 'ffn_conv_w']
REP_S5 = ['s5_lam_re', 's5_lam_im', 's5_log_dt', 's5_b_re', 's5_b_im', 's5_c_re', 's5_c_im']
REP_REST = ['norm_mix_g', 'norm_ffn_g', 'mla_cq_norm_g', 'mla_ckv_norm_g', 'mla_q_head_g', 'mla_k_head_g',
            'hgrn_lb_logits', 'hgrn_o_norm_g', 'ffn_conv_b']
SMALL_REP = REP_REST + REP_S5
LANE = 128


def _flat(arrs, mult):
    v = jnp.concatenate([a.reshape(-1) for a in arrs])
    pad = (-v.shape[0]) % mult
    return jnp.pad(v, (0, pad)).reshape(-1, LANE)


def _unflat(flat2d, like):
    v = flat2d.reshape(-1)
    out, o = [], 0
    for a in like:
        out.append(v[o:o + a.size].reshape(a.shape))
        o += a.size
    return out


def _lb_of(logits):
    cum = jnp.cumsum(jax.nn.softmax(logits, axis=0), axis=0)
    return (cum - cum[0:1])[1:2]


def _cols_to_blocks(g):
    k, n = g.shape
    return g.reshape(k, N_DEV, n // N_DEV).transpose(1, 0, 2)


def _blocks_to_cols(wb):
    nb, k, n = wb.shape
    return wb.transpose(1, 0, 2).reshape(k, nb * n)


SUBS = ['mla', 'ffn0', 'hgrn', 'ffn1', 's5', 'ffn2', 'ret', 'ffn3']
GROUPS = [[('mla_w_down', 0), ('mla_w_uq', 0), ('mla_w_ukv', 0), ('mla_w_o', 0)],
          [('ffn_w_up', 0), ('ffn_w_down', 0)],
          [('hgrn_w_in', 0), ('hgrn_w_o', 0)],
          [('ffn_w_up', 1), ('ffn_w_down', 1)],
          [('s5_w_glu', 0)],
          [('ffn_w_up', 2), ('ffn_w_down', 2)],
          [('ret_w_in', 0), ('ret_w_o', 0)],
          [('ffn_w_up', 3), ('ffn_w_down', 3)]]


def _pack8(parts, mult):
    v = jnp.concatenate(parts, axis=1)
    return jnp.pad(v, ((0, 0), (0, (-v.shape[1]) % mult))).reshape(N_DEV, -1, LANE)


def _sub_weights(k, got, rep, tabs, lp):
    ngm, ngf = rep['norm_mix_g'], rep['norm_ffn_g']
    if k == 0:
        return dict(ng=ngm[0:1], wdown=got[0].reshape(D, -1), gcq=rep['mla_cq_norm_g'], gckv=rep['mla_ckv_norm_g'],
                    wuq=got[1], wukv=got[2], gq=rep['mla_q_head_g'], gk=rep['mla_k_head_g'], wo=got[3].reshape(D, D),
                    mc=mla_consts(lp))
    if k == 2:
        return dict(ng=ngm[1:2], win=got[0], lb=tabs['lb'], go=rep['hgrn_o_norm_g'], wo=got[1].reshape(D, D))
    if k == 4:
        return dict(ng=ngm[2:3], tb=tabs['tb'], dsk=tabs['s5_d'], wglu=_blocks_to_cols(got[0]))
    if k == 6:
        return dict(ng=ngm[3:4], win=_blocks_to_cols(got[0]), gn=tabs['ret_gn_g'], wo=got[1].reshape(2 * D, D),
                    rc=ret_consts(lp))
    i = k // 2
    return dict(ng=ngf[i:i + 1], up=got[0], cw=tabs['conv_w'][:, i].reshape(2, 4, 3, 1, FFN_B),
                cb=rep['ffn_conv_b'][i].reshape(2, 4, 1, FFN_B), down=got[1].reshape(4, FFN_B, D))


def _sub_grad_blocks(k, g):
    if k == 0:
        parts = [g['wdown'], g['wuq'], g['wukv'], g['wo']]
    elif k == 2:
        parts = [g['win'], g['wo']]
    elif k == 4:
        parts = [_cols_to_blocks(g['wglu'])]
    elif k == 6:
        parts = [_cols_to_blocks(g['win']), g['wo']]
    else:
        parts = [g['up'], g['down']]
    return parts


_FWD = [mla_fwd, None, hgrn_fwd, None, s5_fwd, None, ret_fwd, None]
_BWD = [mla_bwd, None, hgrn_bwd, None, s5_bwd, None, ret_bwd, None]


def _step(args):
    w = {n: args[n] for n in WEIGHTS}
    x2, tgt = args['x'][0], args['loss_target'][0]

    lp = x2.shape[0] + OFF
    me = _my_index()
    mask = _rowmask(lp)
    rep = {n: w[n] for n in SMALL_REP}

    xs, slots = [], []
    for gi, grp in enumerate(GROUPS):
        items = [w[n][l].astype(BF) for n, l in grp] + ([_flat([w[n] for n in SMALL_SH], LANE)] if gi == 0 else [])
        slots.append((len(xs), len(items)))
        xs += items
    gh = push_start("gather_start", xs, me, scatter=False)

    def fetch(gi, after):
        s, cnt = slots[gi]
        return push_wait("gather_wait_" + SUBS[gi], gh[s:s + cnt], after, scatter=False)

    got = fetch(0, x2)
    sm, o, smp = got[-1].reshape(N_DEV, -1), 0, {}
    for n in SMALL_SH:
        smp[n] = sm[:, o:o + w[n].size].reshape((N_DEV,) + w[n].shape)
        o += w[n].size
    meta = smp['meta_tokens'].transpose(1, 0, 2).reshape(N_META, D)
    lb, lb_vjp = jax.vjp(_lb_of, rep['hgrn_lb_logits'])
    s5p = [rep[n][0] for n in ('s5_lam_re', 's5_lam_im', 's5_log_dt', 's5_b_re', 's5_b_im', 's5_c_re', 's5_c_im')]
    tb, tb_vjp = jax.vjp(s5_tables, *s5p)
    tabs = dict(lb=lb, tb=tb, s5_d=smp['s5_d'].reshape(1, D), ret_gn_g=smp['ret_gn_g'].reshape(1, 2 * D),
                conv_w=smp['ffn_conv_w'])
    h = jnp.concatenate([jnp.zeros((PAD, D), F32), meta, x2], axis=0)
    ws, saved = [], []
    for k in range(8):
        if k > 0:
            got = fetch(k, h)
        ws.append(_sub_weights(k, got, rep, tabs, lp))
        if k % 2:
            h, sv = ffn_fwd(k // 2, h, mask, ws[k])
        else:
            h, sv = _FWD[k](h, mask, ws[k])
        saved.append(sv)
    loss, dh = loss_head(h, tgt)

    gs, sh = [None] * 8, [None] * 8
    early = {}

    def emit(tag, grads, carry):
        blocks = [t.reshape((N_DEV, -1) + t.shape[-1:]) if t.ndim == 2 else t for t in grads]
        early[tag], carry = push_start("scatter_start_mla_" + tag, blocks, me, scatter=True, carry=carry)
        return carry

    for k in reversed(range(1, 8)):
        if k % 2:
            dh, gs[k] = ffn_bwd(k // 2, dh, mask, ws[k], saved[k])
        else:
            dh, gs[k] = _BWD[k](dh, mask, ws[k], saved[k])
        blocks = [b.reshape((N_DEV,) + w[n].shape[1:]) for b, (n, _) in zip(_sub_grad_blocks(k, gs[k]), GROUPS[k])]
        sh[k], dh = push_start("scatter_start_" + SUBS[k], blocks, me, scatter=True, carry=dh)
        if k == 4:
            gs5 = _flat(list(tb_vjp(gs[4]['tb'])), 8 * LANE)
            rh_s5, dh = push_start("small_grads_start_s5", [gs5], me, scatter=False, carry=dh)
    dh, gs[0] = mla_bwd(dh, mask, ws[0], saved[0], emit)
    dmeta = dh[PAD:OFF].reshape(N_META, N_DEV, D // N_DEV).transpose(1, 0, 2)
    dcw = jnp.stack([gs[2 * i + 1]['cw'].reshape(N_DEV, 3, FFN_B) for i in range(4)], axis=1)
    last = push_start("scatter_start_mla", [gs[0]['wdown'].reshape(N_DEV, D // N_DEV, -1),
                                            _pack8([t.reshape(N_DEV, -1) for t in (dmeta, gs[4]['dsk'], gs[6]['gn'], dcw)], LANE)],
                      me, scatter=True)
    sh[0] = [last[0], early['wu'][0], early['wu'][1], early['wo'][0], last[1]]
    grad_x = dh[OFF:]

    g_rep = {
        'norm_mix_g': jnp.concatenate([gs[k]['ng'] for k in (0, 2, 4, 6)], axis=0),
        'norm_ffn_g': jnp.concatenate([gs[k]['ng'] for k in (1, 3, 5, 7)], axis=0),
        'mla_cq_norm_g': gs[0]['gcq'], 'mla_ckv_norm_g': gs[0]['gckv'], 'mla_q_head_g': gs[0]['gq'],
        'mla_k_head_g': gs[0]['gk'], 'hgrn_lb_logits': lb_vjp(gs[2]['lb'])[0], 'hgrn_o_norm_g': gs[2]['go'],
        'ffn_conv_b': jnp.stack([gs[k]['cb'].reshape(-1) for k in (1, 3, 5, 7)], axis=0),
    }
    loss_part = loss[0, 0:1]
    grep = _flat([g_rep[n] for n in REP_REST] + [loss_part], 8 * LANE)
    rh = push_start("small_grads_start", [grep], me, scatter=False)

    lands = {n: [None] * w[n].shape[0] for n in BIG}
    res = {}
    late = [n for n, _ in GROUPS[0]]
    for k in reversed(range(1, 8)):
        got = push_wait("scatter_wait_" + SUBS[k], sh[k], grep, scatter=True)
        for (n, l), t in zip(GROUPS[k], got):
            lands[n][l] = t
    for n in BIG:
        if n not in late:
            res[n] = adamw("adam_" + n, lands[n], w[n], args['m_' + n], args['v_' + n])
    after = res['ffn_w_up'][1]
    got = push_wait("scatter_wait_" + SUBS[0], sh[0], after, scatter=True)
    small_land = got[-1]
    (rep_land,) = push_wait("small_grads_wait", rh, after, scatter=False)
    (s5_land,) = push_wait("small_grads_wait_s5", rh_s5, after, scatter=False)
    for (n, _), t in zip(GROUPS[0], got):
        res[n] = adamw("adam_" + n, [t], w[n], args['m_' + n], args['v_' + n])

    def flat_adam(name, land, names, mult, extra=()):
        like = [w[n] for n in names]
        pad = [jnp.zeros_like(e) for e in extra]
        out = adamw(name, [land], _flat(like + pad, mult)[None], _flat([args['m_' + n] for n in names] + pad, mult)[None],
                    _flat([args['v_' + n] for n in names] + pad, mult)[None])
        for n, parts in zip(names, zip(*[_unflat(t, like) for t in out])):
            res[n] = list(parts)
        return out[0]

    flat_adam("adam_small_sharded", small_land, SMALL_SH, LANE)
    flat_adam("adam_s5_replicated", s5_land, REP_S5, 8 * LANE)
    gsum = flat_adam("adam_small_replicated", rep_land, REP_REST, 8 * LANE, extra=[loss_part])
    total = gsum.reshape(-1)[sum(w[n].size for n in REP_REST)]
    outs = [total, grad_x[None]]
    for k in range(4):
        outs += [res[n][k] for n in WEIGHTS]
    return tuple(outs)


def kernel(x, meta_tokens, norm_mix_g, norm_ffn_g, mla_w_down, mla_cq_norm_g, mla_ckv_norm_g, mla_w_uq, mla_w_ukv, mla_q_head_g, mla_k_head_g, mla_w_o, hgrn_w_in, hgrn_lb_logits, hgrn_o_norm_g, hgrn_w_o, s5_lam_re, s5_lam_im, s5_log_dt, s5_b_re, s5_b_im, s5_c_re, s5_c_im, s5_d, s5_w_glu, ret_w_in, ret_gn_g, ret_w_o, ffn_w_up, ffn_conv_w, ffn_conv_b, ffn_w_down, loss_target, m_meta_tokens, m_norm_mix_g, m_norm_ffn_g, m_mla_w_down, m_mla_cq_norm_g, m_mla_ckv_norm_g, m_mla_w_uq, m_mla_w_ukv, m_mla_q_head_g, m_mla_k_head_g, m_mla_w_o, m_hgrn_w_in, m_hgrn_lb_logits, m_hgrn_o_norm_g, m_hgrn_w_o, m_s5_lam_re, m_s5_lam_im, m_s5_log_dt, m_s5_b_re, m_s5_b_im, m_s5_c_re, m_s5_c_im, m_s5_d, m_s5_w_glu, m_ret_w_in, m_ret_gn_g, m_ret_w_o, m_ffn_w_up, m_ffn_conv_w, m_ffn_conv_b, m_ffn_w_down, v_meta_tokens, v_norm_mix_g, v_norm_ffn_g, v_mla_w_down, v_mla_cq_norm_g, v_mla_ckv_norm_g, v_mla_w_uq, v_mla_w_ukv, v_mla_q_head_g, v_mla_k_head_g, v_mla_w_o, v_hgrn_w_in, v_hgrn_lb_logits, v_hgrn_o_norm_g, v_hgrn_w_o, v_s5_lam_re, v_s5_lam_im, v_s5_log_dt, v_s5_b_re, v_s5_b_im, v_s5_c_re, v_s5_c_im, v_s5_d, v_s5_w_glu, v_ret_w_in, v_ret_gn_g, v_ret_w_o, v_ffn_w_up, v_ffn_conv_w, v_ffn_conv_b, v_ffn_w_down):
    return _step(dict(locals()))
```

```python
import functools
import math

import jax
import jax.numpy as jnp
import numpy as np
from jax import lax
from jax.experimental import pallas as pl
from jax.experimental.pallas import tpu as pltpu

F32 = jnp.float32
BF = jnp.bfloat16
SDS = jax.ShapeDtypeStruct

N_DEV = 8
D = 1024
N_META = 16
PAD = 48
OFF = PAD + N_META
CH = 64
EPS = 1e-6
NEG = -1e30
ROPE_BASE = 10000.0

MLA_H, MLA_NOPE, MLA_ROPE, MLA_V = 8, 128, 64, 128
MLA_QK = MLA_NOPE + MLA_ROPE
MLA_QL, MLA_KVL = 384, 256
HG_H, HG_D, HG_C = 8, 128, 16
S5_G, S5_P, S5_K = 64, 64, 16
S5_SG = 8
RET_H, RET_DK, RET_DV = 4, 256, 512
FFN_F = 2816
FFN_B = 704

ADAM_LR, ADAM_B1, ADAM_B2, ADAM_EPS, ADAM_WD, ADAM_STEP = 0.001, 0.9, 0.999, 1e-08, 0.01, 10

VMEM_LIMIT = 56 * 1024 * 1024
ARB = "arbitrary"


def _cp(n):
    return pltpu.CompilerParams(dimension_semantics=(ARB,) * n, vmem_limit_bytes=VMEM_LIMIT)


def _bdot(a, b, ca, cb):
    return lax.dot_general(a.astype(BF), b.astype(BF), (((ca,), (cb,)), ((), ())), preferred_element_type=F32)


@jax.custom_vjp
def mm(a, b):
    return _bdot(a, b, 1, 0)


@jax.custom_vjp
def mm_nt(a, b):
    return _bdot(a, b, 1, 1)


@jax.custom_vjp
def mm_tn(a, b):
    return _bdot(a, b, 0, 0)


mm.defvjp(lambda a, b: (mm(a, b), (a, b)),
          lambda r, g: (mm_nt(g, r[1]).astype(r[0].dtype), mm_tn(r[0], g).astype(r[1].dtype)))
mm_nt.defvjp(lambda a, b: (mm_nt(a, b), (a, b)),
             lambda r, g: (mm(g, r[1]).astype(r[0].dtype), mm_tn(g, r[0]).astype(r[1].dtype)))
mm_tn.defvjp(lambda a, b: (mm_tn(a, b), (a, b)),
             lambda r, g: (mm_nt(r[1], g).astype(r[0].dtype), mm(r[0], g).astype(r[1].dtype)))


def _xdot(a, b, ca, cb):
    return lax.dot_general(a, b, (((ca,), (cb,)), ((), ())), preferred_element_type=F32,
                           precision=lax.Precision.HIGHEST)


@jax.custom_vjp
def cright(x, r):
    return _xdot(x, r, 1, 0)


cright.defvjp(lambda x, r: (cright(x, r), r), lambda r, g: (_xdot(g, r, 1, 1), jnp.zeros_like(r)))


def _shift_raw(x, s):
    n = x.shape[0]
    r = lax.broadcasted_iota(jnp.int32, x.shape, 0)
    y = pltpu.roll(x, s % n, 0)
    return jnp.where((r >= s) & (r < n + s), y, 0.0)


def _seg_shift_raw(x, s, seg, up):
    n = x.shape[0]
    r = lax.broadcasted_iota(jnp.int32, x.shape, 0) % seg
    if up:
        return jnp.where(r < seg - s, pltpu.roll(x, n - s, 0), 0.0)
    return jnp.where(r >= s, pltpu.roll(x, s, 0), 0.0)


@functools.partial(jax.custom_vjp, nondiff_argnums=(1, 2))
def seg_shift(x, s, seg):
    return _seg_shift_raw(x, s, seg, False)


seg_shift.defvjp(lambda x, s, seg: (_seg_shift_raw(x, s, seg, False), None),
                 lambda s, seg, _, g: (_seg_shift_raw(g, s, seg, True),))


def _seg_cumsum(x, seg):
    s = 1
    while s < seg:
        x = x + seg_shift(x, s, seg)
        s *= 2
    return x


def _rms(x, g):
    return x * lax.rsqrt(jnp.mean(x * x, axis=-1, keepdims=True) + EPS) * g


def _silu(x):
    return x * jax.nn.sigmoid(x)


def _mm_call(name, a, b, *, grid, a_spec, b_spec, o_shape, o_spec, dims, acc_shape, res=None, res_spec=None,
             mask_tm=None):
    nk = grid[2]

    def body(*refs):
        if res is None:
            a_ref, b_ref, o_ref = refs[:3]
        else:
            a_ref, b_ref, r_ref, o_ref = refs[:4]
        k = pl.program_id(2)

        def dot():
            return lax.dot_general(a_ref[...].astype(BF), b_ref[...].astype(BF), dims, preferred_element_type=F32)

        def finish(v):
            if res is not None:
                v = v + r_ref[...].astype(F32)
                rows = pl.program_id(0) * mask_tm + lax.broadcasted_iota(jnp.int32, v.shape, 0)
                v = jnp.where(rows >= PAD, v, 0.0)
            o_ref[...] = v.astype(o_ref.dtype)

        if nk == 1:
            finish(dot())
            return
        acc = refs[-1]

        @pl.when(k == 0)
        def _():
            acc[...] = dot()

        @pl.when((k > 0) & (k < nk - 1))
        def _():
            acc[...] += dot()

        @pl.when(k == nk - 1)
        def _():
            finish(acc[...] + dot())

    ins = [a, b] + ([res] if res is not None else [])
    specs = [a_spec, b_spec] + ([res_spec] if res is not None else [])
    scratch = [pltpu.VMEM(acc_shape, F32)] if nk > 1 else []
    return pl.pallas_call(body, name=name, grid=grid, in_specs=specs, out_specs=o_spec, out_shape=o_shape,
                          scratch_shapes=scratch, compiler_params=_cp(3))(*ins)


NN = (((1,), (0,)), ((), ()))
NT = (((1,), (1,)), ((), ()))
TN = (((0,), (0,)), ((), ()))


def _row_tile(lp):
    for t in (832, 640, 320, 64):
        if lp % t == 0:
            return t
    raise ValueError(lp)


def _col_tile(n):
    for t in (1024, 768, 512, 384, 256, 128):
        if n % t == 0:
            return t
    return n


def _mm_rows(m):
    return 2080 if m % 2080 == 0 else _row_tile(m)


def _mm_cols(n):
    for t in (512, 384, 256, 128):
        if n % t == 0:
            return t
    return n


def lin(name, a, w, out_dtype=F32, res=None):
    m, k = a.shape
    n = w.shape[1]
    tm, tn, tc = _mm_rows(m), _mm_cols(n), _col_tile(k)
    return _mm_call(name, a, w, grid=(m // tm, n // tn, k // tc),
                    a_spec=pl.BlockSpec((tm, tc), lambda i, j, kk: (i, kk)),
                    b_spec=pl.BlockSpec((tc, tn), lambda i, j, kk: (kk, j)),
                    o_shape=SDS((m, n), out_dtype), o_spec=pl.BlockSpec((tm, tn), lambda i, j, kk: (i, j)),
                    dims=NN, acc_shape=(tm, tn), res=res,
                    res_spec=pl.BlockSpec((tm, tn), lambda i, j, kk: (i, j)), mask_tm=tm)


def lin_bo(name, a, wb, out_dtype=F32):
    m, k = a.shape
    nb, _, n = wb.shape
    tm = _mm_rows(m)
    return _mm_call(name, a, wb, grid=(m // tm, nb, 1),
                    a_spec=pl.BlockSpec((tm, k), lambda i, j, kk: (i, 0)),
                    b_spec=pl.BlockSpec((None, k, n), lambda i, j, kk: (j, 0, 0)),
                    o_shape=SDS((nb, m, n), out_dtype), o_spec=pl.BlockSpec((None, tm, n), lambda i, j, kk: (j, i, 0)),
                    dims=NN, acc_shape=(tm, n))


def lin_bi(name, ab, wb, out_dtype=F32, res=None):
    nb, m, k = ab.shape
    n = wb.shape[2]
    tm, tn = _mm_rows(m), _mm_cols(n)
    return _mm_call(name, ab, wb, grid=(m // tm, n // tn, nb),
                    a_spec=pl.BlockSpec((None, tm, k), lambda i, j, kk: (kk, i, 0)),
                    b_spec=pl.BlockSpec((None, k, tn), lambda i, j, kk: (kk, 0, j)),
                    o_shape=SDS((m, n), out_dtype), o_spec=pl.BlockSpec((tm, tn), lambda i, j, kk: (i, j)),
                    dims=NN, acc_shape=(tm, tn), res=res,
                    res_spec=pl.BlockSpec((tm, tn), lambda i, j, kk: (i, j)), mask_tm=tm)


def lin_t(name, g, w, out_dtype=F32):
    m, n = g.shape
    k = w.shape[0]
    tm, tk, tc = _mm_rows(m), _col_tile(k), _col_tile(n)
    return _mm_call(name, g, w, grid=(m // tm, k // tk, n // tc),
                    a_spec=pl.BlockSpec((tm, tc), lambda i, j, kk: (i, kk)),
                    b_spec=pl.BlockSpec((tk, tc), lambda i, j, kk: (j, kk)),
                    o_shape=SDS((m, k), out_dtype), o_spec=pl.BlockSpec((tm, tk), lambda i, j, kk: (i, j)),
                    dims=NT, acc_shape=(tm, tk))


def lin_t_bi(name, gb, wb, out_dtype=F32):
    nb, m, n = gb.shape
    k = wb.shape[1]
    tm, tk = _mm_rows(m), _col_tile(k)
    return _mm_call(name, gb, wb, grid=(m // tm, k // tk, nb),
                    a_spec=pl.BlockSpec((None, tm, n), lambda i, j, kk: (kk, i, 0)),
                    b_spec=pl.BlockSpec((None, tk, n), lambda i, j, kk: (kk, j, 0)),
                    o_shape=SDS((m, k), out_dtype), o_spec=pl.BlockSpec((tm, tk), lambda i, j, kk: (i, j)),
                    dims=NT, acc_shape=(tm, tk))


def lin_t_bo(name, g, wb, out_dtype=F32):
    m, n = g.shape
    nb, k, _ = wb.shape
    tm = _mm_rows(m)
    return _mm_call(name, g, wb, grid=(m // tm, nb, 1),
                    a_spec=pl.BlockSpec((tm, n), lambda i, j, kk: (i, 0)),
                    b_spec=pl.BlockSpec((None, k, n), lambda i, j, kk: (j, 0, 0)),
                    o_shape=SDS((nb, m, k), out_dtype), o_spec=pl.BlockSpec((None, tm, k), lambda i, j, kk: (j, i, 0)),
                    dims=NT, acc_shape=(tm, k))


def wgrad(name, a, g):
    m, k = a.shape
    n = g.shape[1]
    tm, tn = _mm_rows(m), (_col_tile(n) if k <= 1024 else _mm_cols(n))
    return _mm_call(name, a, g, grid=(1, n // tn, m // tm),
                    a_spec=pl.BlockSpec((tm, k), lambda i, j, kk: (kk, 0)),
                    b_spec=pl.BlockSpec((tm, tn), lambda i, j, kk: (kk, j)),
                    o_shape=SDS((k, n), F32), o_spec=pl.BlockSpec((k, tn), lambda i, j, kk: (0, j)),
                    dims=TN, acc_shape=(k, tn))


def wgrad_bo(name, a, gb):
    m, k = a.shape
    nb, _, n = gb.shape
    tm = _mm_rows(m)
    return _mm_call(name, a, gb, grid=(nb, 1, m // tm),
                    a_spec=pl.BlockSpec((tm, k), lambda i, j, kk: (kk, 0)),
                    b_spec=pl.BlockSpec((None, tm, n), lambda i, j, kk: (i, kk, 0)),
                    o_shape=SDS((nb, k, n), F32), o_spec=pl.BlockSpec((None, k, n), lambda i, j, kk: (i, 0, 0)),
                    dims=TN, acc_shape=(k, n))


def wgrad_bi(name, zb, g):
    nb, m, k = zb.shape
    n = g.shape[1]
    tm, tn = _mm_rows(m), _col_tile(n)
    return _mm_call(name, zb, g, grid=(nb, n // tn, m // tm),
                    a_spec=pl.BlockSpec((None, tm, k), lambda i, j, kk: (i, kk, 0)),
                    b_spec=pl.BlockSpec((tm, tn), lambda i, j, kk: (kk, j)),
                    o_shape=SDS((nb, k, n), F32), o_spec=pl.BlockSpec((None, k, tn), lambda i, j, kk: (i, 0, j)),
                    dims=TN, acc_shape=(k, tn))


class Arg:
    def __init__(self, arr, block, imap, shared=False, acc=False):
        self.arr, self.block, self.imap = arr, block, imap
        self.shared = shared
        self.acc = acc

    @property
    def spec(self):
        return pl.BlockSpec(self.block, self.imap)

    def vshape(self):
        return tuple(b for b in self.block if b is not None)


def _rev(arg, nt):
    return pl.BlockSpec(arg.block, lambda o, t, _f=arg.imap: _f(o, nt - 1 - t))


def seq_fwd(name, fn, grid, params, consts, xs, outs, carries=(), save_dtype=F32):
    no, nt = grid
    n_p, n_c, n_x, n_y, n_k = len(params), len(consts), len(xs), len(outs), len(carries)

    def body(*refs):
        p_refs = refs[:n_p]
        c_refs = refs[n_p:n_p + n_c]
        x_refs = refs[n_p + n_c:n_p + n_c + n_x]
        r = n_p + n_c + n_x
        y_refs = refs[r:r + n_y]
        s_refs = refs[r + n_y:r + n_y + n_k]
        k_refs = refs[r + n_y + n_k:]
        t = pl.program_id(1)

        if n_k:
            @pl.when(t == 0)
            def _():
                for k in k_refs:
                    k[...] = jnp.zeros_like(k)

        carry = tuple(k[...] for k in k_refs)
        for s, c in zip(s_refs, carry):
            s[...] = c.astype(s.dtype)
        new_carry, ys = fn(tuple(p[...] for p in p_refs), tuple(c[...] for c in c_refs), carry,
                           tuple(x[...] for x in x_refs))
        for k, c in zip(k_refs, new_carry):
            k[...] = c
        for y_ref, y in zip(y_refs, ys):
            y_ref[...] = y.astype(y_ref.dtype)

    out_shape = [SDS(s, d) for (s, d, _, _) in outs]
    out_specs = [pl.BlockSpec(b, im) for (_, _, b, im) in outs]
    for cs in carries:
        out_shape.append(SDS((no, nt) + cs, save_dtype))
        out_specs.append(pl.BlockSpec((None, None) + cs, lambda o, t, _n=len(cs): (o, t) + (0,) * _n))
    res = pl.pallas_call(
        body, name=name, grid=grid, in_specs=[a.spec for a in list(params) + list(consts) + list(xs)],
        out_specs=out_specs, out_shape=out_shape, scratch_shapes=[pltpu.VMEM(cs, F32) for cs in carries],
        compiler_params=_cp(2))(*[a.arr for a in list(params) + list(consts) + list(xs)])
    return res[:n_y], res[n_y:]


def seq_bwd(name, fn, grid, params, consts, xs, dys, saved=(), carries=()):
    no, nt = grid
    n_p, n_c, n_x, n_y, n_k = len(params), len(consts), len(xs), len(dys), len(carries)

    def body(*refs):
        p_refs = refs[:n_p]
        c_refs = refs[n_p:n_p + n_c]
        x_refs = refs[n_p + n_c:n_p + n_c + n_x]
        r = n_p + n_c + n_x
        g_refs = refs[r:r + n_y]
        s_refs = refs[r + n_y:r + n_y + n_k]
        r = r + n_y + n_k
        dx_refs = refs[r:r + n_x]
        dp_refs = refs[r + n_x:r + n_x + n_p]
        k_refs = refs[r + n_x + n_p:]
        o = pl.program_id(0)
        t = pl.program_id(1)

        if n_k:
            @pl.when(t == 0)
            def _():
                for k in k_refs:
                    k[...] = jnp.zeros_like(k)

        for a, dp in zip(params, dp_refs):
            @pl.when((t == 0) & (o == 0) if a.shared else (t == 0))
            def _(dp=dp):
                dp[...] = jnp.zeros_like(dp)

        for a, dx in zip(xs, dx_refs):
            if a.acc:
                @pl.when(t == 0)
                def _(dx=dx):
                    dx[...] = jnp.zeros_like(dx)

        consts_v = tuple(c[...] for c in c_refs)

        def f(pv, cv, xv):
            return fn(pv, consts_v, cv, xv)

        pv = tuple(p[...] for p in p_refs)
        cv = tuple(s[...].astype(F32) for s in s_refs)
        xv = tuple(x[...] for x in x_refs)
        (new_carry, ys), vjp = jax.vjp(f, pv, cv, xv)
        cot = (tuple(k[...] for k in k_refs), tuple(g[...].astype(y.dtype) for g, y in zip(g_refs, ys)))
        dpv, dcv, dxv = vjp(cot)
        for k, c in zip(k_refs, dcv):
            k[...] = c
        for dp, v in zip(dp_refs, dpv):
            dp[...] += v
        for a, dx, v in zip(xs, dx_refs, dxv):
            if a.acc:
                dx[...] += v
            else:
                dx[...] = v.astype(dx.dtype)

    in_specs = ([_rev(a, nt) for a in list(params) + list(consts) + list(xs) + list(dys)]
                + [pl.BlockSpec((None, None) + cs, lambda o, t, _n=len(cs): (o, nt - 1 - t) + (0,) * _n) for cs in carries])
    out_shape = [SDS(a.arr.shape, F32) for a in xs] + [SDS(a.arr.shape, F32) for a in params]
    out_specs = [_rev(a, nt) for a in list(xs) + list(params)]
    res = pl.pallas_call(
        body, name=name, grid=grid, in_specs=in_specs, out_specs=out_specs, out_shape=out_shape,
        scratch_shapes=[pltpu.VMEM(cs, F32) for cs in carries], compiler_params=_cp(2))(
            *[a.arr for a in list(params) + list(consts) + list(xs) + list(dys)], *saved)
    return res[:n_x], res[n_x:]


def _rowmask(lp):
    return (jnp.arange(lp) >= PAD).astype(F32)[:, None]


def _norm_fn(p, c, k, x):
    return (), (_rms(x[0] * c[0], p[0]),)


def _norm_b_fn(p, c, k, x):
    h = x[0] * c[0]
    return (), (_rms(h, p[0]), h)


def norm_fwd(name, h, g, mask, out_dtype=BF):
    lp, d = h.shape
    tr = _row_tile(lp)
    row = lambda o, t: (t, 0)
    (a,), _ = seq_fwd(name, _norm_fn, (1, lp // tr), [Arg(g, (1, d), lambda o, t: (0, 0), shared=True)],
                      [Arg(mask, (tr, 1), row)], [Arg(h, (tr, d), row)], [((lp, d), out_dtype, (tr, d), row)])
    return a


def norm_bwd(name, h, g, mask, da, dskip):
    lp, d = h.shape
    tr = _row_tile(lp)
    row = lambda o, t: (t, 0)
    (dh,), (dg,) = seq_bwd(name, _norm_b_fn, (1, lp // tr), [Arg(g, (1, d), lambda o, t: (0, 0), shared=True)],
                           [Arg(mask, (tr, 1), row)], [Arg(h, (tr, d), row)],
                           [Arg(da, (tr, d), row), Arg(dskip, (tr, d), row)])
    return dh, dg


def _ffn_tile(lp):
    return 320 if (lp % 320 == 0 and lp > 320) else 64


def _conv_rows(ext, w, b, n):
    u2 = ext[8:8 + n]
    u1 = pltpu.roll(ext, 1, 0)[8:8 + n]
    u0 = pltpu.roll(ext, 2, 0)[8:8 + n]
    return w[2] * u2 + w[1] * u1 + w[0] * u0 + b, (u0, u1, u2)


def ffn_up_core(name, a, wup, cw, cb):
    lp, kd = a.shape
    _, nj, _, fb = wup.shape
    tr = 416 if (lp % 416 == 0 and lp > 416) else _ffn_tile(lp)
    nt = lp // tr

    def body(a_ref, wu_ref, w_ref, b_ref, u_ref, z_ref, u_s, halo_s):
        i = pl.program_id(1)

        @pl.when(i == 0)
        def _():
            u_s[...] = jnp.zeros_like(u_s)
            halo_s[...] = jnp.zeros_like(halo_s)

        old = (i + 1) % 2
        cs = []
        for s in range(2):
            tile = u_s[old, s]
            ext = jnp.concatenate([halo_s[s], tile], axis=0)
            c, _ = _conv_rows(ext, w_ref[s], b_ref[s], tr)
            cs.append(c)
            halo_s[s] = tile[tr - 8:]
        z_ref[...] = (_silu(cs[0]) * cs[1]).astype(z_ref.dtype)
        for s in range(2):
            un = lax.dot_general(a_ref[...], wu_ref[s], NN, preferred_element_type=F32)
            u_ref[s] = un
            u_s[i % 2, s] = un

    cur = lambda j, i: (0, j, jnp.minimum(i, nt - 1), 0)
    return pl.pallas_call(
        body, name=name, grid=(nj, nt + 1),
        in_specs=[pl.BlockSpec((tr, kd), lambda j, i: (jnp.minimum(i, nt - 1), 0)),
                  pl.BlockSpec((2, None, kd, fb), lambda j, i: (0, j, 0, 0)),
                  pl.BlockSpec((2, None, 3, 1, fb), lambda j, i: (0, j, 0, 0, 0)),
                  pl.BlockSpec((2, None, 1, fb), lambda j, i: (0, j, 0, 0))],
        out_specs=[pl.BlockSpec((2, None, tr, fb), cur),
                   pl.BlockSpec((None, tr, fb), lambda j, i: (j, jnp.maximum(i - 1, 0), 0))],
        out_shape=[SDS((2, nj, lp, fb), F32), SDS((nj, lp, fb), BF)],
        scratch_shapes=[pltpu.VMEM((2, 2, tr, fb), F32), pltpu.VMEM((2, 8, fb), F32)],
        compiler_params=_cp(2))(a, wup, cw, cb)


def ffn_core_bwd(name, u, dz, cw, cb):
    _, nj, lp, fb = u.shape
    tr = _ffn_tile(lp)
    nt = lp // tr
    nb8 = lp // 8

    def body(u_ref, up_ref, un_ref, dz_ref, dzn_ref, w_ref, b_ref, du_ref, dw_ref, db_ref):
        i = pl.program_id(1)

        @pl.when(i == 0)
        def _():
            dw_ref[...] = jnp.zeros_like(dw_ref)
            db_ref[...] = jnp.zeros_like(db_ref)

        prev = jnp.where(i > 0, up_ref[...], 0.0)
        nxt = jnp.where(i < nt - 1, un_ref[...], 0.0)
        dz_e = jnp.concatenate([dz_ref[...], jnp.where(i < nt - 1, dzn_ref[...], 0.0)], axis=0)
        n = tr + 8
        cs, taps = [], []
        for s in range(2):
            ext = jnp.concatenate([prev[s], u_ref[s], nxt[s]], axis=0)
            c, tp = _conv_rows(ext, w_ref[s], b_ref[s], n)
            cs.append(c)
            taps.append(tp)
        sg = jax.nn.sigmoid(cs[0])
        dcs = [dz_e * cs[1] * sg * (1.0 + cs[0] * (1.0 - sg)), dz_e * cs[0] * sg]
        for s in range(2):
            dc = dcs[s]
            w = w_ref[s]
            d1 = pltpu.roll(dc, n - 1, 0)[:tr]
            d2 = pltpu.roll(dc, n - 2, 0)[:tr]
            dcm = dc[:tr]
            du_ref[s] = w[2] * dcm + w[1] * d1 + w[0] * d2
            for k in range(3):
                dw_ref[s, k] += jnp.sum(dcm * taps[s][k][:tr], axis=0, keepdims=True)
            db_ref[s] += jnp.sum(dcm, axis=0, keepdims=True)

    return pl.pallas_call(
        body, name=name, grid=(nj, nt),
        in_specs=[pl.BlockSpec((2, None, tr, fb), lambda j, i: (0, j, i, 0)),
                  pl.BlockSpec((2, None, 8, fb), lambda j, i: (0, j, jnp.maximum(i * (tr // 8) - 1, 0), 0)),
                  pl.BlockSpec((2, None, 8, fb), lambda j, i: (0, j, jnp.minimum((i + 1) * (tr // 8), nb8 - 1), 0)),
                  pl.BlockSpec((None, tr, fb), lambda j, i: (j, i, 0)),
                  pl.BlockSpec((None, 8, fb), lambda j, i: (j, jnp.minimum((i + 1) * (tr // 8), nb8 - 1), 0)),
                  pl.BlockSpec((2, None, 3, 1, fb), lambda j, i: (0, j, 0, 0, 0)),
                  pl.BlockSpec((2, None, 1, fb), lambda j, i: (0, j, 0, 0))],
        out_specs=[pl.BlockSpec((2, None, tr, fb), lambda j, i: (0, j, i, 0)),
                   pl.BlockSpec((2, None, 3, 1, fb), lambda j, i: (0, j, 0, 0, 0)),
                   pl.BlockSpec((2, None, 1, fb), lambda j, i: (0, j, 0, 0))],
        out_shape=[SDS(u.shape, F32), SDS(cw.shape, F32), SDS(cb.shape, F32)],
        compiler_params=_cp(2))(u, u, u, dz, dz, cw, cb)


def ffn_fwd(i, h, mask, w):
    a = norm_fwd(f"ffn{i}_norm", h, w["ng"], mask)
    u, z = ffn_up_core(f"ffn{i}_up_core", a, w["up"].reshape(2, 4, D, FFN_B), w["cw"], w["cb"])
    h2 = lin_bi(f"ffn{i}_down", z, w["down"], res=h)
    return h2, (h, a, u, z)


def ffn_bwd(i, dh2, mask, w, saved):
    h, a, u, z = saved
    lp = h.shape[0]
    g = {}
    g["down"] = wgrad_bi(f"ffn{i}_dwdown", z, dh2)
    dz = lin_t_bo(f"ffn{i}_dz", dh2, w["down"])
    du, g["cw"], g["cb"] = ffn_core_bwd(f"ffn{i}_core_b", u, dz, w["cw"], w["cb"])
    du = du.reshape(8, lp, FFN_B)
    g["up"] = wgrad_bo(f"ffn{i}_dwup", a, du)
    da = lin_t_bi(f"ffn{i}_da", du, w["up"])
    dh, g["ng"] = norm_bwd(f"ffn{i}_norm_b", h, w["ng"], mask, da, dh2)
    return dh, g


HG_HB = 4


def _hgrn_fn(p, c, k, x):
    lb, go = p
    q, f, iv, g = x[0][0], x[0][1], x[0][2], x[0][3]
    (st_all,) = k
    qs = _silu(q)
    forget = lb + (1.0 - lb) * jax.nn.sigmoid(f)
    logf = jnp.log(forget)
    kk = 1.0 - forget
    gc_all = _seg_cumsum(logf, HG_C)
    r = lax.broadcasted_iota(jnp.int32, (HG_C, HG_C), 0)
    cc = lax.broadcasted_iota(jnp.int32, (HG_C, HG_C), 1)
    ns = CH // HG_C
    cells = [(j, s) for j in range(HG_HB) for s in range(ns)]

    def blk(t, j, s):
        return t[HG_C * s:HG_C * (s + 1), HG_D * j:HG_D * (j + 1)]

    gl = {c: jnp.sum(blk(logf, *c), axis=0, keepdims=True) for c in cells}
    qd = {c: blk(qs, *c) * jnp.exp(blk(gc_all, *c)) for c in cells}
    ki = {c: blk(kk, *c) * jnp.exp(-blk(gc_all, *c)) for c in cells}
    up = {c: mm_tn(blk(iv, *c), blk(kk, *c) * jnp.exp(gl[c] - blk(gc_all, *c))) for c in cells}
    st, sts = {}, []
    for j in range(HG_HB):
        cur = st_all[j]
        for s in range(ns):
            st[(j, s)] = cur
            cur = cur * jnp.exp(gl[(j, s)]) + up[(j, s)]
        sts.append(cur)
    both = {c: mm_nt(qd[c], jnp.concatenate([st[c], ki[c]], axis=0)) for c in cells}
    oc = {c: mm(jnp.where(r >= cc, both[c][:, HG_D:], 0.0), blk(iv, *c)) + both[c][:, :HG_D] for c in cells}
    zs = []
    for j in range(HG_HB):
        o = jnp.concatenate([oc[(j, s)] for s in range(ns)], axis=0)
        zs.append(_rms(o, go) * _silu(g[:, HG_D * j:HG_D * (j + 1)]))
    return (jnp.stack(sts, axis=0),), (jnp.concatenate(zs, axis=1),)


def _hgrn_args(u4, lb, go):
    lp = u4.shape[2]
    wb = HG_HB * HG_D
    xs = [Arg(u4, (4, None, CH, wb), lambda o, t: (0, o, t, 0))]
    ps = [Arg(lb, (1, wb), lambda o, t: (0, o)), Arg(go, (1, HG_D), lambda o, t: (0, 0), shared=True)]
    return (HG_H // HG_HB, lp // CH), ps, xs


def hgrn_fwd(h, mask, w):
    lp = h.shape[0]
    a = norm_fwd("hgrn_norm", h, w["ng"], mask)
    u4 = lin_bo("hgrn_in", a, w["win"]).reshape(4, HG_H // HG_HB, lp, HG_HB * HG_D)
    grid, ps, xs = _hgrn_args(u4, w["lb"], w["go"])
    (z,), (st,) = seq_fwd("hgrn_core", _hgrn_fn, grid, ps, [], xs,
                          [((lp, D), BF, (CH, HG_HB * HG_D), lambda o, t: (t, o))], carries=[(HG_HB, HG_D, HG_D)])
    h2 = lin("hgrn_out", z, w["wo"], res=h)
    return h2, (h, a, u4, z, st)


def hgrn_bwd(dh2, mask, w, saved):
    h, a, u4, z, st = saved
    lp = h.shape[0]
    g = {}
    g["wo"] = wgrad("hgrn_dwo", z, dh2)
    dz = lin_t("hgrn_dz", dh2, w["wo"])
    grid, ps, xs = _hgrn_args(u4, w["lb"], w["go"])
    (du4,), (g["lb"], g["go"]) = seq_bwd("hgrn_core_b", _hgrn_fn, grid, ps, [], xs,
                                         [Arg(dz, (CH, HG_HB * HG_D), lambda o, t: (t, o))], saved=[st],
                                         carries=[(HG_HB, HG_D, HG_D)])
    du = du4.reshape(N_DEV, lp, HG_HB * HG_D)
    g["win"] = wgrad_bo("hgrn_dwin", a, du)
    da = lin_t_bi("hgrn_da", du, w["win"])
    dh, g["ng"] = norm_bwd("hgrn_norm_b", h, w["ng"], mask, da, dh2)
    return dh, g


S5_W = S5_SG * S5_P


def s5_tables(lam_re, lam_im, log_dt, b_re, b_im, c_re, c_im):
    dt = jnp.exp(log_dt)[:, None]
    mag = jnp.exp(lam_re * dt)
    abar_re = mag * jnp.cos(lam_im * dt)
    abar_im = mag * jnp.sin(lam_im * dt)
    den = lam_re * lam_re + lam_im * lam_im
    zoh_re = ((abar_re - 1.0) * lam_re + abar_im * lam_im) / den
    zoh_im = (abar_im * lam_re - (abar_re - 1.0) * lam_im) / den
    bbar_re = zoh_re[..., None] * b_re - zoh_im[..., None] * b_im
    bbar_im = zoh_re[..., None] * b_im + zoh_im[..., None] * b_re
    eye = jnp.eye(S5_SG, dtype=F32)

    def blockdiag_in(b):
        t = b.reshape(N_DEV, S5_SG, S5_P, S5_K).transpose(0, 1, 3, 2)
        return jnp.einsum("jakp,ab->jakbp", t, eye).reshape(N_DEV, S5_SG * S5_K, S5_W)

    def blockdiag_out(c):
        t = c.reshape(N_DEV, S5_SG, S5_K, S5_P).transpose(0, 1, 3, 2)
        return jnp.einsum("japk,ab->japbk", t, eye).reshape(N_DEV, S5_W, S5_SG * S5_K)

    wb = jnp.concatenate([blockdiag_in(bbar_re), blockdiag_in(bbar_im)], axis=2)
    wc = jnp.concatenate([blockdiag_out(c_re), -blockdiag_out(c_im)], axis=1)

    abar = jnp.concatenate([abar_re.reshape(N_DEV, 1, S5_W), abar_im.reshape(N_DEV, 1, S5_W)], axis=2)
    return wb, wc, abar


def _cmul(ar, ai, xr, xi):
    return ar * xr - ai * xi, ar * xi + ai * xr


def _scan_rows(ar, ai, xr, xi, reverse):
    for s in range(6):
        sh = -(1 << s) if reverse else (1 << s)
        dr, di = _cmul(ar, ai, _shift_raw(xr, sh), _shift_raw(xi, sh))
        xr, xi = xr + dr, xi + di
        ar, ai = ar * ar - ai * ai, 2.0 * ar * ai
    return xr, xi


@jax.custom_vjp
def cscan(ar, ai, br, bi):
    return _scan_rows(ar, ai, br, bi, False)


def _cscan_fwd(ar, ai, br, bi):
    xr, xi = _scan_rows(ar, ai, br, bi, False)
    return (xr, xi), (ar, ai, xr, xi)


def _cscan_bwd(res, g):
    ar, ai, xr, xi = res
    lr, li = _scan_rows(ar, -ai, g[0], g[1], True)
    pr, pi = _shift_raw(xr, 1), _shift_raw(xi, 1)
    dar = jnp.sum(lr * pr + li * pi, axis=0, keepdims=True)
    dai = jnp.sum(li * pr - lr * pi, axis=0, keepdims=True)
    return dar, dai, lr, li


cscan.defvjp(_cscan_fwd, _cscan_bwd)

S5_BB = 8


def _s5_fn(p, c, k, x):
    wb, wc, abar, dsk = p
    (a,) = x
    (x0,) = k
    blocks = range(S5_BB)
    aj = [a[:, 128 * j:128 * (j + 1)] for j in blocks]
    bu = [mm(aj[j], wb[j]) for j in blocks]
    first = lax.broadcasted_iota(jnp.int32, (CH, S5_W), 0) == 0
    last = lax.broadcasted_iota(jnp.int32, (CH, 2 * S5_W), 0) == CH - 1
    xxs, x0n = [], []
    for j in blocks:
        ar, ai = abar[j][:, :S5_W], abar[j][:, S5_W:]
        cr, ci = _cmul(ar, ai, x0[j][:, :S5_W], x0[j][:, S5_W:])
        xr, xi = cscan(ar, ai, bu[j][:, :S5_W] + jnp.where(first, cr, 0.0), bu[j][:, S5_W:] + jnp.where(first, ci, 0.0))
        xx = jnp.concatenate([xr, xi], axis=1)
        x0n.append(jnp.sum(jnp.where(last, xx, 0.0), axis=0, keepdims=True))
        xxs.append(xx)
    y = jnp.concatenate([mm(xxs[j], wc[j]) for j in blocks], axis=1)
    return (jnp.stack(x0n, axis=0),), (jax.nn.gelu(y + dsk * a),)


def _s5_args(a, tb, dsk):
    lp = a.shape[0]
    wb, wc, abar = tb
    ps = [Arg(wb, (S5_BB, 128, 2 * S5_W), lambda o, t: (o, 0, 0)), Arg(wc, (S5_BB, 2 * S5_W, 128), lambda o, t: (o, 0, 0)),
          Arg(abar, (S5_BB, 1, 2 * S5_W), lambda o, t: (o, 0, 0)), Arg(dsk, (1, 128 * S5_BB), lambda o, t: (0, o))]
    xs = [Arg(a, (CH, 128 * S5_BB), lambda o, t: (t, o))]
    return (N_DEV // S5_BB, lp // CH), ps, xs


def _glu_res_fn(p, c, k, x):
    h, vg = x
    return (), ((h + vg[:, :D] * jax.nn.sigmoid(vg[:, D:])) * c[0],)


def _glu_args(h, vg, mask):
    lp = h.shape[0]
    tr = _row_tile(lp)
    row = lambda o, t: (t, 0)
    return (1, lp // tr), [Arg(mask, (tr, 1), row)], [Arg(h, (tr, D), row), Arg(vg, (tr, 2 * D), row)], tr


def s5_fwd(h, mask, w):
    lp = h.shape[0]
    a = norm_fwd("s5_norm", h, w["ng"], mask, out_dtype=F32)
    grid, ps, xs = _s5_args(a, w["tb"], w["dsk"])
    (z,), (st,) = seq_fwd("s5_core", _s5_fn, grid, ps, [], xs,
                          [((lp, D), BF, (CH, 128 * S5_BB), lambda o, t: (t, o))], carries=[(S5_BB, 1, 2 * S5_W)])
    vg = lin("s5_glu", z, w["wglu"])
    grid2, cs, xs2, tr = _glu_args(h, vg, mask)
    (h2,), _ = seq_fwd("s5_res", _glu_res_fn, grid2, [], cs, xs2, [((lp, D), F32, (tr, D), lambda o, t: (t, 0))])
    return h2, (h, a, z, vg, st)


def s5_bwd(dh2, mask, w, saved):
    h, a, z, vg, st = saved
    g = {}
    grid2, cs, xs2, tr = _glu_args(h, vg, mask)
    (dskip, dvg), _ = seq_bwd("s5_res_b", _glu_res_fn, grid2, [], cs, xs2, [Arg(dh2, (tr, D), lambda o, t: (t, 0))])
    g["wglu"] = wgrad("s5_dwglu", z, dvg)
    dz = lin_t("s5_dz", dvg, w["wglu"])
    grid, ps, xs = _s5_args(a, w["tb"], w["dsk"])
    (da,), dps = seq_bwd("s5_core_b", _s5_fn, grid, ps, [], xs, [Arg(dz, (CH, 128 * S5_BB), lambda o, t: (t, o))],
                         saved=[st], carries=[(S5_BB, 1, 2 * S5_W)])
    g["tb"] = tuple(dps[:3])
    g["dsk"] = dps[3]
    dh, g["ng"] = norm_bwd("s5_norm_b", h, w["ng"], mask, da, dskip)
    return dh, g


def _rope_angles(lp, dim):
    pos = np.maximum(np.arange(lp, dtype=np.float32) - PAD, 0.0).astype(np.float32)
    inv = (1.0 / (ROPE_BASE ** (np.arange(0, dim, 2, dtype=np.float32) / dim))).astype(np.float32)
    return (pos[:, None] * inv[None, :]).astype(np.float32)


def ret_consts(lp):
    f = np.float32
    ang = _rope_angles(lp, RET_DK)
    lg = np.log(1.0 - np.exp2(-5.0 - np.arange(RET_H, dtype=f))).astype(f)
    p = np.arange(CH, dtype=f)
    diff = p[:, None] - p[None, :]
    decay = np.where(diff >= 0, np.exp(diff[None] * lg[:, None, None]), 0.0).astype(f)
    qd = np.exp((p[None, :] + 1.0) * lg[:, None])[..., None].astype(f)
    kd = np.exp((CH - 1.0 - p[None, :]) * lg[:, None])[..., None].astype(f)
    cd = np.exp(CH * lg)[:, None, None].astype(f)
    return np.cos(ang).astype(f), np.sin(ang).astype(f), decay, qd, kd, cd


def _ret_fn(p, c, k, x):
    (gn,) = p
    cos, sin, decay, qd, kd, cd = c
    (st,) = k
    (u,) = x
    hd = RET_DK // 2
    qk_w = RET_H * RET_DK
    heads = range(RET_H)

    def rope(t):
        t1, t2 = t[:, :hd], t[:, hd:]
        return jnp.concatenate([t1 * cos - t2 * sin, t1 * sin + t2 * cos], axis=1)

    qr = [rope(u[:, RET_DK * h:RET_DK * (h + 1)]) for h in heads]
    kr = [rope(u[:, qk_w + RET_DK * h:qk_w + RET_DK * (h + 1)]) * (RET_DK ** -0.5) for h in heads]
    v = [u[:, 2 * qk_w + RET_DV * h:2 * qk_w + RET_DV * (h + 1)] for h in heads]
    scores = [mm_nt(qr[h], kr[h]) for h in heads]
    inter = [mm(qr[h] * qd[h], st[h]) for h in heads]
    st_new = jnp.stack([st[h] * cd[h] + mm_tn(kr[h] * kd[h], v[h]) for h in heads], axis=0)
    o = [mm(scores[h] * decay[h], v[h]) + inter[h] for h in heads]
    zs = []
    for h in heads:
        mu = jnp.mean(o[h], axis=-1, keepdims=True)
        var = jnp.mean(jnp.square(o[h] - mu), axis=-1, keepdims=True)
        gate = u[:, 2 * qk_w + RET_H * RET_DV + RET_DV * h:2 * qk_w + RET_H * RET_DV + RET_DV * (h + 1)]
        zs.append((o[h] - mu) * lax.rsqrt(var + EPS) * gn[:, RET_DV * h:RET_DV * (h + 1)] * _silu(gate))
    return (st_new,), (jnp.concatenate(zs, axis=1),)


def _ret_args(u, gn, rc):
    lp, uw = u.shape
    cos, sin, decay, qd, kd, cd = rc
    full = lambda o, t: (0, 0, 0)
    ps = [Arg(gn, (1, RET_H * RET_DV), lambda o, t: (0, 0))]
    cs = [Arg(cos, (CH, RET_DK // 2), lambda o, t: (t, 0)), Arg(sin, (CH, RET_DK // 2), lambda o, t: (t, 0)),
          Arg(decay, (RET_H, CH, CH), full), Arg(qd, (RET_H, CH, 1), full), Arg(kd, (RET_H, CH, 1), full),
          Arg(cd, (RET_H, 1, 1), full)]
    xs = [Arg(u, (CH, uw), lambda o, t: (t, 0))]
    return (1, lp // CH), ps, cs, xs


def ret_fwd(h, mask, w):
    lp = h.shape[0]
    a = norm_fwd("ret_norm", h, w["ng"], mask)
    u = lin("ret_in", a, w["win"])
    grid, ps, cs, xs = _ret_args(u, w["gn"], w["rc"])
    (z,), (st,) = seq_fwd("ret_core", _ret_fn, grid, ps, cs, xs,
                          [((lp, 2 * D), BF, (CH, RET_H * RET_DV), lambda o, t: (t, 0))],
                          carries=[(RET_H, RET_DK, RET_DV)], save_dtype=BF)
    h2 = lin("ret_out", z, w["wo"], res=h)
    return h2, (h, a, u, z, st)


def ret_bwd(dh2, mask, w, saved):
    h, a, u, z, st = saved
    g = {}
    g["wo"] = wgrad("ret_dwo", z, dh2)
    dz = lin_t("ret_dz", dh2, w["wo"])
    grid, ps, cs, xs = _ret_args(u, w["gn"], w["rc"])
    (du,), (g["gn"],) = seq_bwd("ret_core_b", _ret_fn, grid, ps, cs, xs,
                                [Arg(dz, (CH, RET_H * RET_DV), lambda o, t: (t, 0))], saved=[st],
                                carries=[(RET_H, RET_DK, RET_DV)])
    g["win"] = wgrad("ret_dwin", a, du)
    da = lin_t("ret_da", du, w["win"])
    dh, g["ng"] = norm_bwd("ret_norm_b", h, w["ng"], mask, da, dh2)
    return dh, g


def mla_consts(lp):
    ang = _rope_angles(lp, MLA_ROPE)
    cos = np.concatenate([np.cos(ang), np.cos(ang)], axis=1).astype(np.float32)
    sin = np.concatenate([np.sin(ang), np.sin(ang)], axis=1).astype(np.float32)
    hd = MLA_ROPE // 2
    i = np.arange(hd)
    rot = np.zeros((MLA_ROPE, MLA_ROPE), np.float32)
    rot[hd + i, i] = -1.0
    rot[i, hd + i] = 1.0
    return cos, sin, rot


def _mla_prep1_fn(p, c, k, x):
    gq, gkv = p
    (down,) = x
    return (), (_rms(down[:, :MLA_QL], gq), _rms(down[:, MLA_QL:MLA_QL + MLA_KVL], gkv), down[:, MLA_QL + MLA_KVL:])


def _mla_prep2(p, c, x):
    gq, gk = p
    cos, sin, rot = c
    q, kv, kpe = x
    qn = _rms(q, gq)
    qn_n, qn_r = qn[:, :MLA_NOPE], qn[:, MLA_NOPE:]
    qo = jnp.concatenate([qn_n, qn_r * cos + cright(qn_r, rot) * sin], axis=1)
    kn = kv[:, :MLA_NOPE]
    ms = (jnp.sum(kn * kn, axis=-1, keepdims=True) + jnp.sum(kpe * kpe, axis=-1, keepdims=True)) / MLA_QK
    r = lax.rsqrt(ms + EPS)
    kr = kpe * r * gk[:, MLA_NOPE:]
    ko = jnp.concatenate([kn * r * gk[:, :MLA_NOPE], kr * cos + cright(kr, rot) * sin], axis=1)
    return qo, ko, kv[:, MLA_NOPE:]


def _mla_prep2_fn(p, c, k, x):
    return (), _mla_prep2(p, c, x)[:2]


def _mla_prep2_b_fn(p, c, k, x):
    return (), _mla_prep2(p, c, x)


def _prep1_args(down, gq, gkv):
    lp = down.shape[0]
    tr = _row_tile(lp)
    ps = [Arg(gq, (1, MLA_QL), lambda o, t: (0, 0), shared=True), Arg(gkv, (1, MLA_KVL), lambda o, t: (0, 0), shared=True)]
    return (1, lp // tr), ps, [Arg(down, (tr, down.shape[1]), lambda o, t: (t, 0))], tr


def _prep2_args(qraw, kvraw, kpe, gq, gk, mc):
    lp = kpe.shape[0]
    tr = _row_tile(lp)
    cos, sin, rot = mc
    ps = [Arg(gq, (1, MLA_QK), lambda o, t: (0, 0), shared=True), Arg(gk, (1, MLA_QK), lambda o, t: (0, 0), shared=True)]
    cs = [Arg(cos, (tr, MLA_ROPE), lambda o, t: (o, 0)), Arg(sin, (tr, MLA_ROPE), lambda o, t: (o, 0)),
          Arg(rot, (MLA_ROPE, MLA_ROPE), lambda o, t: (0, 0))]
    xs = [Arg(qraw, (None, tr, MLA_QK), lambda o, t: (t, o, 0)), Arg(kvraw, (None, tr, MLA_NOPE + MLA_V), lambda o, t: (t, o, 0)),
          Arg(kpe, (tr, MLA_ROPE), lambda o, t: (o, 0), acc=True)]
    return (lp // tr, MLA_H), ps, cs, xs, tr


ATT_HB = 2


def _attn_tile(lp):
    return 832 if (lp % 832 == 0 and lp > 832) else 64


def _attn_mask(qi, ki, ta):
    rows = qi * ta + lax.broadcasted_iota(jnp.int32, (ta, ta), 0)
    cols = ki * ta + lax.broadcasted_iota(jnp.int32, (ta, ta), 1)
    return (cols >= PAD) & ((cols // CH) <= (rows // CH))


def attn_fwd(q, k, kv):
    nh, lp, dq = q.shape
    ta = _attn_tile(lp)
    nb = lp // ta
    scale = MLA_QK ** -0.5

    hb = ATT_HB

    def body(q_ref, k_ref, v_ref, o_ref, lse_ref, m_s, l_s, acc_s):
        qi, ki = pl.program_id(1), pl.program_id(2)

        @pl.when(ki == 0)
        def _():
            m_s[...] = jnp.full_like(m_s, NEG)
            l_s[...] = jnp.zeros_like(l_s)
            acc_s[...] = jnp.zeros_like(acc_s)

        def step(masked):
            ss = [_bdot(q_ref[j], k_ref[j], 1, 1) * scale for j in range(hb)]
            ps = []
            for j in range(hb):
                s = jnp.where(_attn_mask(qi, ki, ta), ss[j], NEG) if masked else ss[j]
                m_new = jnp.maximum(m_s[j], jnp.max(s, axis=-1, keepdims=True))
                p = jnp.exp(s - m_new)
                alpha = jnp.exp(m_s[j] - m_new)
                l_s[j] = alpha * l_s[j] + jnp.sum(p, axis=-1, keepdims=True)
                m_s[j] = m_new
                ps.append((p, alpha))
            for j in range(hb):
                acc_s[j] = ps[j][1] * acc_s[j] + _bdot(ps[j][0], v_ref[j], 1, 0)

        pl.when((ki == qi) | (ki == 0))(functools.partial(step, True))
        pl.when((ki < qi) & (ki > 0))(functools.partial(step, False))

        @pl.when(ki == nb - 1)
        def _():
            for j in range(hb):
                o_ref[:, MLA_V * j:MLA_V * (j + 1)] = (acc_s[j] / l_s[j]).astype(o_ref.dtype)
                lse_ref[j] = m_s[j] + jnp.log(l_s[j])

    return pl.pallas_call(
        body, name="mla_attn", grid=(nh // hb, nb, nb),
        in_specs=[pl.BlockSpec((hb, ta, dq), lambda h, qi, ki: (h, qi, 0)),
                  pl.BlockSpec((hb, ta, dq), lambda h, qi, ki: (h, jnp.minimum(ki, qi), 0)),
                  pl.BlockSpec((hb, ta, MLA_V), lambda h, qi, ki: (h, jnp.minimum(ki, qi), 1))],
        out_specs=[pl.BlockSpec((ta, hb * MLA_V), lambda h, qi, ki: (qi, h)),
                   pl.BlockSpec((hb, ta, 1), lambda h, qi, ki: (h, qi, 0))],
        out_shape=[SDS((lp, nh * MLA_V), BF), SDS((nh, lp, 1), F32)],
        scratch_shapes=[pltpu.VMEM((hb, ta, 1), F32), pltpu.VMEM((hb, ta, 1), F32), pltpu.VMEM((hb, ta, MLA_V), F32)],
        compiler_params=_cp(3))(q, k, kv)


def attn_bwd(q, k, kv, o, do, lse):
    nh, lp, dq = q.shape
    ta = _attn_tile(lp)
    nb = lp // ta
    scale = MLA_QK ** -0.5

    def body(q_ref, k_ref, v_ref, o_ref, do_ref, lse_ref, dq_ref, dk_ref, dv_ref, dk_s, dv_s):
        ki, qi = pl.program_id(1), pl.program_id(2)

        @pl.when((ki == 0) & (qi == 0))
        def _():
            dq_ref[...] = jnp.zeros_like(dq_ref)

        @pl.when(qi == 0)
        def _():
            dk_s[...] = jnp.zeros_like(dk_s)
            dv_s[...] = jnp.zeros_like(dv_s)

        def step(masked):
            dov = do_ref[...]
            s = _bdot(q_ref[...], k_ref[...], 1, 1) * scale
            dp = _bdot(dov, v_ref[...], 1, 1)
            if masked:
                s = jnp.where(_attn_mask(qi, ki, ta), s, NEG)
            p = jnp.exp(s - lse_ref[...])
            delta = jnp.sum(dov * o_ref[...].astype(F32), axis=-1, keepdims=True)
            dv_s[...] += _bdot(p, dov, 0, 0)
            ds = p * (dp - delta) * scale
            rows = pl.ds(pl.multiple_of(qi * ta, ta), ta)
            dq_ref[rows, :] += _bdot(ds, k_ref[...], 1, 0)
            dk_s[...] += _bdot(ds, q_ref[...], 0, 0)

        pl.when((ki == qi) | (ki == 0))(functools.partial(step, True))
        pl.when((ki < qi) & (ki > 0))(functools.partial(step, False))

        @pl.when(qi == nb - 1)
        def _():
            dk_ref[...] = dk_s[...]
            dv_ref[...] = dv_s[...]

    qmap = lambda h, ki, qi: (h, jnp.maximum(qi, ki), 0)
    return pl.pallas_call(
        body, name="mla_attn_b", grid=(nh, nb, nb),
        in_specs=[pl.BlockSpec((None, ta, dq), qmap),
                  pl.BlockSpec((None, ta, dq), lambda h, ki, qi: (h, ki, 0)),
                  pl.BlockSpec((None, ta, MLA_V), lambda h, ki, qi: (h, ki, 1)),
                  pl.BlockSpec((ta, MLA_V), lambda h, ki, qi: (jnp.maximum(qi, ki), h)),
                  pl.BlockSpec((ta, MLA_V), lambda h, ki, qi: (jnp.maximum(qi, ki), h)),
                  pl.BlockSpec((None, ta, 1), qmap)],
        out_specs=[pl.BlockSpec((None, lp, dq), lambda h, ki, qi: (h, 0, 0)),
                   pl.BlockSpec((None, ta, dq), lambda h, ki, qi: (h, ki, 0)),
                   pl.BlockSpec((None, ta, MLA_V), lambda h, ki, qi: (h, ki, 0))],
        out_shape=[SDS((nh, lp, dq), F32), SDS((nh, lp, dq), F32), SDS((nh, lp, MLA_V), F32)],
        scratch_shapes=[pltpu.VMEM((ta, dq), F32), pltpu.VMEM((ta, MLA_V), F32)],
        compiler_params=_cp(3))(q, k, kv, o, do, lse)


def mla_fwd(h, mask, w):
    lp = h.shape[0]
    a = norm_fwd("mla_norm", h, w["ng"], mask)
    down = lin("mla_down", a, w["wdown"])
    grid, ps, xs, tr = _prep1_args(down, w["gcq"], w["gckv"])
    row = lambda o, t: (t, 0)
    (cq, ckv, kpe), _ = seq_fwd("mla_prep1", _mla_prep1_fn, grid, ps, [], xs,
                                [((lp, MLA_QL), BF, (tr, MLA_QL), row), ((lp, MLA_KVL), BF, (tr, MLA_KVL), row),
                                 ((lp, MLA_ROPE), F32, (tr, MLA_ROPE), row)])
    qraw = lin_bo("mla_uq", cq, w["wuq"])
    kvraw = lin_bo("mla_ukv", ckv, w["wukv"])
    grid, ps, cs, xs, tr = _prep2_args(qraw, kvraw, kpe, w["gq"], w["gk"], w["mc"])
    hm = lambda o, t: (t, o, 0)
    (q, k), _ = seq_fwd("mla_prep2", _mla_prep2_fn, grid, ps, cs, xs,
                        [((MLA_H, lp, MLA_QK), BF, (None, tr, MLA_QK), hm), ((MLA_H, lp, MLA_QK), BF, (None, tr, MLA_QK), hm)])
    o, lse = attn_fwd(q, k, kvraw)
    h2 = lin("mla_out", o, w["wo"], res=h)
    return h2, (h, a, down, cq, ckv, kpe, qraw, kvraw, q, k, o, lse)


def mla_bwd(dh2, mask, w, saved, emit):
    h, a, down, cq, ckv, kpe, qraw, kvraw, q, k, o, lse = saved
    lp = h.shape[0]
    g = {}
    g["wo"] = wgrad("mla_dwo", o, dh2)
    do = lin_t("mla_do", dh2, w["wo"])
    do = emit("wo", [g["wo"]], do)
    dq, dk, dv = attn_bwd(q, k, kvraw, o, do, lse)
    grid, ps, cs, xs, tr = _prep2_args(qraw, kvraw, kpe, w["gq"], w["gk"], w["mc"])
    hm = lambda o, t: (t, o, 0)
    (dqraw, dkvraw, dkpe), (g["gq"], g["gk"]) = seq_bwd(
        "mla_prep2_b", _mla_prep2_b_fn, grid, ps, cs, xs,
        [Arg(dq, (None, tr, MLA_QK), hm), Arg(dk, (None, tr, MLA_QK), hm), Arg(dv, (None, tr, MLA_V), hm)])
    g["wuq"] = wgrad_bo("mla_dwuq", cq, dqraw)
    dcq = lin_t_bi("mla_dcq", dqraw, w["wuq"])
    g["wukv"] = wgrad_bo("mla_dwukv", ckv, dkvraw)
    dckv = lin_t_bi("mla_dckv", dkvraw, w["wukv"])
    dckv = emit("wu", [g["wuq"], g["wukv"]], dckv)
    grid, ps, xs, tr = _prep1_args(down, w["gcq"], w["gckv"])
    row = lambda o, t: (t, 0)
    (ddown,), (g["gcq"], g["gckv"]) = seq_bwd(
        "mla_prep1_b", _mla_prep1_fn, grid, ps, [], xs,
        [Arg(dcq, (tr, MLA_QL), row), Arg(dckv, (tr, MLA_KVL), row), Arg(dkpe, (tr, MLA_ROPE), row)])
    g["wdown"] = wgrad("mla_dwdown", a, ddown)
    da = lin_t("mla_da", ddown, w["wdown"])
    dh, g["ng"] = norm_bwd("mla_norm_b", h, w["ng"], mask, da, dh2)
    return dh, g


def loss_head(h, target):
    lp, d = h.shape
    assert OFF == CH

    def body(h_ref, t_ref, loss_ref, dh_ref):
        i = pl.program_id(0)

        @pl.when(i == 0)
        def _():
            loss_ref[...] = jnp.zeros_like(loss_ref)

        e = jnp.where(i > 0, h_ref[...] - t_ref[...], 0.0)
        loss_ref[...] += jnp.sum(e * e) * (0.5 / d)
        dh_ref[...] = e * (1.0 / d)

    return pl.pallas_call(
        body, name="loss_head", grid=(lp // CH,),
        in_specs=[pl.BlockSpec((CH, d), lambda i: (i, 0)), pl.BlockSpec((CH, d), lambda i: (jnp.maximum(i - 1, 0), 0))],
        out_specs=[pl.BlockSpec((8, 128), lambda i: (0, 0)), pl.BlockSpec((CH, d), lambda i: (i, 0))],
        out_shape=[SDS((8, 128), F32), SDS((lp, d), F32)], compiler_params=_cp(1))(h, target)


ADAM_LAND_BYTES = 20 * 1024 * 1024


def _adam_tile(r, c, nl):
    if r % 8:
        return r
    best = 8
    for t in range(8, r + 1, 8):
        if r % t == 0 and N_DEV * t * c * 4 * 2 * nl <= ADAM_LAND_BYTES:
            best = t
    return best


def adamw(name, lands, w, m, v):
    nl, r, c = w.shape
    tr = _adam_tile(r, c, nl)
    c1 = 1.0 / (1.0 - ADAM_B1 ** ADAM_STEP)
    c2 = 1.0 / (1.0 - ADAM_B2 ** ADAM_STEP)

    def body(*refs):
        l_refs = refs[:nl]
        w_ref, m_ref, v_ref, g_ref, d_ref, nm_ref, nv_ref = refs[nl:]
        layer = pl.program_id(0)
        for j in range(nl):
            @pl.when(layer == j)
            def _(j=j):
                g = l_refs[j][0]
                for i in range(1, N_DEV):
                    g = g + l_refs[j][i]
                g_ref[...] = g

        g = g_ref[...]
        nm = ADAM_B1 * m_ref[...] + (1.0 - ADAM_B1) * g
        nv = ADAM_B2 * v_ref[...] + (1.0 - ADAM_B2) * (g * g)
        nm_ref[...] = nm
        nv_ref[...] = nv
        d_ref[...] = -ADAM_LR * ((nm * c1) / (jnp.sqrt(nv * c2) + ADAM_EPS) + ADAM_WD * w_ref[...])

    blk = pl.BlockSpec((None, tr, c), lambda l, i: (l, i, 0))
    land_specs = [pl.BlockSpec((N_DEV, tr, c), lambda l, i, j=j: (0, jnp.where(l == j, i, 0), 0)) for j in range(nl)]
    return pl.pallas_call(
        body, name=name, grid=(nl, r // tr), in_specs=land_specs + [blk, blk, blk],
        out_specs=[blk, blk, blk, blk], out_shape=[SDS((nl, r, c), F32)] * 4, compiler_params=_cp(2))(*lands, w, m, v)


ANY = pl.BlockSpec(memory_space=pl.ANY)
MESH = pl.DeviceIdType.MESH


def _me():
    return lax.axis_index("x"), lax.axis_index("y"), lax.axis_index("c")


def _peers():
    x, y, c = _me()
    out = []
    for k in range(1, N_DEV):
        px = 1 - x if k & 4 else x
        py = 1 - y if k & 2 else y
        pc = 1 - c if k & 1 else c
        out.append(((px, py, pc), 4 * px + 2 * py + pc))
    return out


HBM_SPEC = pl.BlockSpec(memory_space=pltpu.HBM)
SEM_SPEC = pl.BlockSpec(memory_space=pltpu.SEMAPHORE)
DATAFLOW = pltpu.SideEffectType.DATAFLOW_SIDE_EFFECTING


def _my_index():
    return 4 * lax.axis_index("x") + 2 * lax.axis_index("y") + lax.axis_index("c")


def _hbm(a):
    return pltpu.with_memory_space_constraint(a, pltpu.HBM)


NP = N_DEV - 1


def _push_copy(x_ref, land_ref, send, recv, pid, src_idx, dst_idx, scatter):
    src = x_ref.at[src_idx] if scatter else x_ref
    return pltpu.make_async_remote_copy(src_ref=src, dst_ref=land_ref.at[dst_idx], send_sem=send, recv_sem=recv,
                                        device_id=pid, device_id_type=MESH)


def push_start(name, xs, me, scatter, carry=None):
    n = len(xs)
    lands = []
    for a in xs:
        own = lax.dynamic_index_in_dim(a, me, 0, keepdims=True) if scatter else a[None]
        z = lax.empty((N_DEV,) + own.shape[1:], a.dtype)
        lands.append(lax.dynamic_update_slice(z, own, (me,) + (0,) * (own.ndim - 1)))
    ns = 2 * NP * n
    ops = xs + lands + ([carry] if carry is not None else [])
    na = len(ops)

    def body(*refs):
        x_refs, land_refs = refs[:n], refs[n:2 * n]
        sems = refs[na:na + ns]
        token = refs[-1]
        x, y, c = _me()
        mine = 4 * x + 2 * y + c
        for i in range(n):
            for k, (pid, pidx) in enumerate(_peers()):
                s = 2 * (NP * i + k)
                _push_copy(x_refs[i], land_refs[i], sems[s], sems[s + 1], pid, pidx, mine, scatter).start()
        token[...] = jnp.zeros_like(token)

    out_shape = ([pltpu.SemaphoreType.DMA(())] * ns + [pltpu.HBM(a.shape, a.dtype) for a in ops]
                 + [SDS((8, 128), F32)])
    res = pl.pallas_call(
        body, name=name, out_shape=out_shape, in_specs=[HBM_SPEC] * na,
        out_specs=[SEM_SPEC] * ns + [HBM_SPEC] * na + [pl.BlockSpec(memory_space=pltpu.VMEM)],
        input_output_aliases={i: ns + i for i in range(na)},
        compiler_params=pltpu.CompilerParams(has_side_effects=DATAFLOW))(*[_hbm(a) for a in ops])
    sems, thru, token = res[:ns], res[ns:-1], res[-1]
    handles = [dict(x=thru[i], land=thru[n + i], sems=list(sems[2 * NP * i:2 * NP * (i + 1)]), token=token)
               for i in range(n)]
    return (handles, thru[2 * n]) if carry is not None else handles


def push_wait(name, hds, after, scatter):
    n = len(hds)
    ns = 2 * NP

    def body(*refs):
        x_refs, land_refs = refs[:n], refs[n:2 * n]
        sems = refs[2 * n:2 * n + ns * n]
        for i in range(n):
            for k, (pid, pidx) in enumerate(_peers()):
                cp = _push_copy(x_refs[i], land_refs[i], sems[ns * i + 2 * k], sems[ns * i + 2 * k + 1], pid, pidx, pidx,
                                scatter)
                cp.wait_send()
                cp.wait_recv()

    arrs = [hd["x"] for hd in hds] + [hd["land"] for hd in hds]
    sems = [s for hd in hds for s in hd["sems"]]
    res = pl.pallas_call(
        body, name=name, out_shape=[pltpu.HBM(a.shape, a.dtype) for a in arrs],
        in_specs=[HBM_SPEC] * (2 * n) + [SEM_SPEC] * (ns * n) + [ANY], out_specs=[HBM_SPEC] * (2 * n),
        input_output_aliases={i: i for i in range(2 * n)},
        compiler_params=pltpu.CompilerParams(has_side_effects=DATAFLOW))(*arrs, *sems, after)
    return list(res[n:])


WEIGHTS = ['meta_tokens', 'norm_mix_g', 'norm_ffn_g', 'mla_w_down', 'mla_cq_norm_g', 'mla_ckv_norm_g', 'mla_w_uq',
           'mla_w_ukv', 'mla_q_head_g', 'mla_k_head_g', 'mla_w_o', 'hgrn_w_in', 'hgrn_lb_logits', 'hgrn_o_norm_g',
           'hgrn_w_o', 's5_lam_re', 's5_lam_im', 's5_log_dt', 's5_b_re', 's5_b_im', 's5_c_re', 's5_c_im', 's5_d',
           's5_w_glu', 'ret_w_in', 'ret_gn_g', 'ret_w_o', 'ffn_w_up', 'ffn_conv_w', 'ffn_conv_b', 'ffn_w_down']
BIG = ['mla_w_down', 'mla_w_uq', 'mla_w_ukv', 'mla_w_o', 'hgrn_w_in', 'hgrn_w_o', 's5_w_glu', 'ret_w_in', 'ret_w_o',
       'ffn_w_up', 'ffn_w_down']
SMALL_SH = ['meta_tokens', 's5_d', 'ret_gn_g', 'ffn_conv_w']
REP_S5 = ['s5_lam_re', 's5_lam_im', 's5_log_dt', 's5_b_re', 's5_b_im', 's5_c_re', 's5_c_im']
REP_REST = ['norm_mix_g', 'norm_ffn_g', 'mla_cq_norm_g', 'mla_ckv_norm_g', 'mla_q_head_g', 'mla_k_head_g',
            'hgrn_lb_logits', 'hgrn_o_norm_g', 'ffn_conv_b']
SMALL_REP = REP_REST + REP_S5
LANE = 128


def _flat(arrs, mult):
    v = jnp.concatenate([a.reshape(-1) for a in arrs])
    pad = (-v.shape[0]) % mult
    return jnp.pad(v, (0, pad)).reshape(-1, LANE)


def _unflat(flat2d, like):
    v = flat2d.reshape(-1)
    out, o = [], 0
    for a in like:
        out.append(v[o:o + a.size].reshape(a.shape))
        o += a.size
    return out


def _lb_of(logits):
    cum = jnp.cumsum(jax.nn.softmax(logits, axis=0), axis=0)
    return (cum - cum[0:1])[1:2]


def _cols_to_blocks(g):
    k, n = g.shape
    return g.reshape(k, N_DEV, n // N_DEV).transpose(1, 0, 2)


def _blocks_to_cols(wb):
    nb, k, n = wb.shape
    return wb.transpose(1, 0, 2).reshape(k, nb * n)


SUBS = ['mla', 'ffn0', 'hgrn', 'ffn1', 's5', 'ffn2', 'ret', 'ffn3']
GROUPS = [[('mla_w_down', 0), ('mla_w_uq', 0), ('mla_w_ukv', 0), ('mla_w_o', 0)],
          [('ffn_w_up', 0), ('ffn_w_down', 0)],
          [('hgrn_w_in', 0), ('hgrn_w_o', 0)],
          [('ffn_w_up', 1), ('ffn_w_down', 1)],
          [('s5_w_glu', 0)],
          [('ffn_w_up', 2), ('ffn_w_down', 2)],
          [('ret_w_in', 0), ('ret_w_o', 0)],
          [('ffn_w_up', 3), ('ffn_w_down', 3)]]


def _pack8(parts, mult):
    v = jnp.concatenate(parts, axis=1)
    return jnp.pad(v, ((0, 0), (0, (-v.shape[1]) % mult))).reshape(N_DEV, -1, LANE)


def _sub_weights(k, got, rep, tabs, lp):
    ngm, ngf = rep['norm_mix_g'], rep['norm_ffn_g']
    if k == 0:
        return dict(ng=ngm[0:1], wdown=got[0].reshape(D, -1), gcq=rep['mla_cq_norm_g'], gckv=rep['mla_ckv_norm_g'],
                    wuq=got[1], wukv=got[2], gq=rep['mla_q_head_g'], gk=rep['mla_k_head_g'], wo=got[3].reshape(D, D),
                    mc=mla_consts(lp))
    if k == 2:
        return dict(ng=ngm[1:2], win=got[0], lb=tabs['lb'], go=rep['hgrn_o_norm_g'], wo=got[1].reshape(D, D))
    if k == 4:
        return dict(ng=ngm[2:3], tb=tabs['tb'], dsk=tabs['s5_d'], wglu=_blocks_to_cols(got[0]))
    if k == 6:
        return dict(ng=ngm[3:4], win=_blocks_to_cols(got[0]), gn=tabs['ret_gn_g'], wo=got[1].reshape(2 * D, D),
                    rc=ret_consts(lp))
    i = k // 2
    return dict(ng=ngf[i:i + 1], up=got[0], cw=tabs['conv_w'][:, i].reshape(2, 4, 3, 1, FFN_B),
                cb=rep['ffn_conv_b'][i].reshape(2, 4, 1, FFN_B), down=got[1].reshape(4, FFN_B, D))


def _sub_grad_blocks(k, g):
    if k == 0:
        parts = [g['wdown'], g['wuq'], g['wukv'], g['wo']]
    elif k == 2:
        parts = [g['win'], g['wo']]
    elif k == 4:
        parts = [_cols_to_blocks(g['wglu'])]
    elif k == 6:
        parts = [_cols_to_blocks(g['win']), g['wo']]
    else:
        parts = [g['up'], g['down']]
    return parts


_FWD = [mla_fwd, None, hgrn_fwd, None, s5_fwd, None, ret_fwd, None]
_BWD = [mla_bwd, None, hgrn_bwd, None, s5_bwd, None, ret_bwd, None]


def _step(args):
    w = {n: args[n] for n in WEIGHTS}
    x2, tgt = args['x'][0], args['loss_target'][0]

    lp = x2.shape[0] + OFF
    me = _my_index()
    mask = _rowmask(lp)
    rep = {n: w[n] for n in SMALL_REP}

    xs, slots = [], []
    for gi, grp in enumerate(GROUPS):
        items = [w[n][l].astype(BF) for n, l in grp] + ([_flat([w[n] for n in SMALL_SH], LANE)] if gi == 0 else [])
        slots.append((len(xs), len(items)))
        xs += items
    gh = push_start("gather_start", xs, me, scatter=False)

    def fetch(gi, after):
        s, cnt = slots[gi]
        return push_wait("gather_wait_" + SUBS[gi], gh[s:s + cnt], after, scatter=False)

    got = fetch(0, x2)
    sm, o, smp = got[-1].reshape(N_DEV, -1), 0, {}
    for n in SMALL_SH:
        smp[n] = sm[:, o:o + w[n].size].reshape((N_DEV,) + w[n].shape)
        o += w[n].size
    meta = smp['meta_tokens'].transpose(1, 0, 2).reshape(N_META, D)
    lb, lb_vjp = jax.vjp(_lb_of, rep['hgrn_lb_logits'])
    s5p = [rep[n][0] for n in ('s5_lam_re', 's5_lam_im', 's5_log_dt', 's5_b_re', 's5_b_im', 's5_c_re', 's5_c_im')]
    tb, tb_vjp = jax.vjp(s5_tables, *s5p)
    tabs = dict(lb=lb, tb=tb, s5_d=smp['s5_d'].reshape(1, D), ret_gn_g=smp['ret_gn_g'].reshape(1, 2 * D),
                conv_w=smp['ffn_conv_w'])
    h = jnp.concatenate([jnp.zeros((PAD, D), F32), meta, x2], axis=0)
    ws, saved = [], []
    for k in range(8):
        if k > 0:
            got = fetch(k, h)
        ws.append(_sub_weights(k, got, rep, tabs, lp))
        if k % 2:
            h, sv = ffn_fwd(k // 2, h, mask, ws[k])
        else:
            h, sv = _FWD[k](h, mask, ws[k])
        saved.append(sv)
    loss, dh = loss_head(h, tgt)

    gs, sh = [None] * 8, [None] * 8
    early = {}

    def emit(tag, grads, carry):
        blocks = [t.reshape((N_DEV, -1) + t.shape[-1:]) if t.ndim == 2 else t for t in grads]
        early[tag], carry = push_start("scatter_start_mla_" + tag, blocks, me, scatter=True, carry=carry)
        return carry

    for k in reversed(range(1, 8)):
        if k % 2:
            dh, gs[k] = ffn_bwd(k // 2, dh, mask, ws[k], saved[k])
        else:
            dh, gs[k] = _BWD[k](dh, mask, ws[k], saved[k])
        blocks = [b.reshape((N_DEV,) + w[n].shape[1:]) for b, (n, _) in zip(_sub_grad_blocks(k, gs[k]), GROUPS[k])]
        sh[k], dh = push_start("scatter_start_" + SUBS[k], blocks, me, scatter=True, carry=dh)
        if k == 4:
            gs5 = _flat(list(tb_vjp(gs[4]['tb'])), 8 * LANE)
            rh_s5, dh = push_start("small_grads_start_s5", [gs5], me, scatter=False, carry=dh)
    dh, gs[0] = mla_bwd(dh, mask, ws[0], saved[0], emit)
    dmeta = dh[PAD:OFF].reshape(N_META, N_DEV, D // N_DEV).transpose(1, 0, 2)
    dcw = jnp.stack([gs[2 * i + 1]['cw'].reshape(N_DEV, 3, FFN_B) for i in range(4)], axis=1)
    last = push_start("scatter_start_mla", [gs[0]['wdown'].reshape(N_DEV, D // N_DEV, -1),
                                            _pack8([t.reshape(N_DEV, -1) for t in (dmeta, gs[4]['dsk'], gs[6]['gn'], dcw)], LANE)],
                      me, scatter=True)
    sh[0] = [last[0], early['wu'][0], early['wu'][1], early['wo'][0], last[1]]
    grad_x = dh[OFF:]

    g_rep = {
        'norm_mix_g': jnp.concatenate([gs[k]['ng'] for k in (0, 2, 4, 6)], axis=0),
        'norm_ffn_g': jnp.concatenate([gs[k]['ng'] for k in (1, 3, 5, 7)], axis=0),
        'mla_cq_norm_g': gs[0]['gcq'], 'mla_ckv_norm_g': gs[0]['gckv'], 'mla_q_head_g': gs[0]['gq'],
        'mla_k_head_g': gs[0]['gk'], 'hgrn_lb_logits': lb_vjp(gs[2]['lb'])[0], 'hgrn_o_norm_g': gs[2]['go'],
        'ffn_conv_b': jnp.stack([gs[k]['cb'].reshape(-1) for k in (1, 3, 5, 7)], axis=0),
    }
    loss_part = loss[0, 0:1]
    grep = _flat([g_rep[n] for n in REP_REST] + [loss_part], 8 * LANE)
    rh = push_start("small_grads_start", [grep], me, scatter=False)

    lands = {n: [None] * w[n].shape[0] for n in BIG}
    res = {}
    late = [n for n, _ in GROUPS[0]]
    for k in reversed(range(1, 8)):
        got = push_wait("scatter_wait_" + SUBS[k], sh[k], grep, scatter=True)
        for (n, l), t in zip(GROUPS[k], got):
            lands[n][l] = t
    for n in BIG:
        if n not in late:
            res[n] = adamw("adam_" + n, lands[n], w[n], args['m_' + n], args['v_' + n])
    after = res['ffn_w_up'][1]
    got = push_wait("scatter_wait_" + SUBS[0], sh[0], after, scatter=True)
    small_land = got[-1]
    (rep_land,) = push_wait("small_grads_wait", rh, after, scatter=False)
    (s5_land,) = push_wait("small_grads_wait_s5", rh_s5, after, scatter=False)
    for (n, _), t in zip(GROUPS[0], got):
        res[n] = adamw("adam_" + n, [t], w[n], args['m_' + n], args['v_' + n])

    def flat_adam(name, land, names, mult, extra=()):
        like = [w[n] for n in names]
        pad = [jnp.zeros_like(e) for e in extra]
        out = adamw(name, [land], _flat(like + pad, mult)[None], _flat([args['m_' + n] for n in names] + pad, mult)[None],
                    _flat([args['v_' + n] for n in names] + pad, mult)[None])
        for n, parts in zip(names, zip(*[_unflat(t, like) for t in out])):
            res[n] = list(parts)
        return out[0]

    flat_adam("adam_small_sharded", small_land, SMALL_SH, LANE)
    flat_adam("adam_s5_replicated", s5_land, REP_S5, 8 * LANE)
    gsum = flat_adam("adam_small_replicated", rep_land, REP_REST, 8 * LANE, extra=[loss_part])
    total = gsum.reshape(-1)[sum(w[n].size for n in REP_REST)]
    outs = [total, grad_x[None]]
    for k in range(4):
        outs += [res[n][k] for n in WEIGHTS]
    return tuple(outs)


def kernel(x, meta_tokens, norm_mix_g, norm_ffn_g, mla_w_down, mla_cq_norm_g, mla_ckv_norm_g, mla_w_uq, mla_w_ukv, mla_q_head_g, mla_k_head_g, mla_w_o, hgrn_w_in, hgrn_lb_logits, hgrn_o_norm_g, hgrn_w_o, s5_lam_re, s5_lam_im, s5_log_dt, s5_b_re, s5_b_im, s5_c_re, s5_c_im, s5_d, s5_w_glu, ret_w_in, ret_gn_g, ret_w_o, ffn_w_up, ffn_conv_w, ffn_conv_b, ffn_w_down, loss_target, m_meta_tokens, m_norm_mix_g, m_norm_ffn_g, m_mla_w_down, m_mla_cq_norm_g, m_mla_ckv_norm_g, m_mla_w_uq, m_mla_w_ukv, m_mla_q_head_g, m_mla_k_head_g, m_mla_w_o, m_hgrn_w_in, m_hgrn_lb_logits, m_hgrn_o_norm_g, m_hgrn_w_o, m_s5_lam_re, m_s5_lam_im, m_s5_log_dt, m_s5_b_re, m_s5_b_im, m_s5_c_re, m_s5_c_im, m_s5_d, m_s5_w_glu, m_ret_w_in, m_ret_gn_g, m_ret_w_o, m_ffn_w_up, m_ffn_conv_w, m_ffn_conv_b, m_ffn_w_down, v_meta_tokens, v_norm_mix_g, v_norm_ffn_g, v_mla_w_down, v_mla_cq_norm_g, v_mla_ckv_norm_g, v_mla_w_uq, v_mla_w_ukv, v_mla_q_head_g, v_mla_k_head_g, v_mla_w_o, v_hgrn_w_in, v_hgrn_lb_logits, v_hgrn_o_norm_g, v_hgrn_w_o, v_s5_lam_re, v_s5_lam_im, v_s5_log_dt, v_s5_b_re, v_s5_b_im, v_s5_c_re, v_s5_c_im, v_s5_d, v_s5_w_glu, v_ret_w_in, v_ret_gn_g, v_ret_w_o, v_ffn_w_up, v_ffn_conv_w, v_ffn_conv_b, v_ffn_w_down):
    return _step(dict(locals()))
```

```python
import functools
import math

import jax
import jax.numpy as jnp
import numpy as np
from jax import lax
from jax.experimental import pallas as pl
from jax.experimental.pallas import tpu as pltpu

F32 = jnp.float32
BF = jnp.bfloat16
SDS = jax.ShapeDtypeStruct

N_DEV = 8
D = 1024
N_META = 16
PAD = 48
OFF = PAD + N_META
CH = 64
EPS = 1e-6
NEG = -1e30
ROPE_BASE = 10000.0

MLA_H, MLA_NOPE, MLA_ROPE, MLA_V = 8, 128, 64, 128
MLA_QK = MLA_NOPE + MLA_ROPE
MLA_QL, MLA_KVL = 384, 256
HG_H, HG_D, HG_C = 8, 128, 16
S5_G, S5_P, S5_K = 64, 64, 16
S5_SG = 8
RET_H, RET_DK, RET_DV = 4, 256, 512
FFN_F = 2816
FFN_B = 704

ADAM_LR, ADAM_B1, ADAM_B2, ADAM_EPS, ADAM_WD, ADAM_STEP = 0.001, 0.9, 0.999, 1e-08, 0.01, 10

VMEM_LIMIT = 56 * 1024 * 1024
ARB = "arbitrary"


def _cp(n):
    return pltpu.CompilerParams(dimension_semantics=(ARB,) * n, vmem_limit_bytes=VMEM_LIMIT)


def _bdot(a, b, ca, cb):
    return lax.dot_general(a.astype(BF), b.astype(BF), (((ca,), (cb,)), ((), ())), preferred_element_type=F32)


@jax.custom_vjp
def mm(a, b):
    return _bdot(a, b, 1, 0)


@jax.custom_vjp
def mm_nt(a, b):
    return _bdot(a, b, 1, 1)


@jax.custom_vjp
def mm_tn(a, b):
    return _bdot(a, b, 0, 0)


mm.defvjp(lambda a, b: (mm(a, b), (a, b)),
          lambda r, g: (mm_nt(g, r[1]).astype(r[0].dtype), mm_tn(r[0], g).astype(r[1].dtype)))
mm_nt.defvjp(lambda a, b: (mm_nt(a, b), (a, b)),
             lambda r, g: (mm(g, r[1]).astype(r[0].dtype), mm_tn(g, r[0]).astype(r[1].dtype)))
mm_tn.defvjp(lambda a, b: (mm_tn(a, b), (a, b)),
             lambda r, g: (mm_nt(r[1], g).astype(r[0].dtype), mm(r[0], g).astype(r[1].dtype)))


def _xdot(a, b, ca, cb):
    return lax.dot_general(a, b, (((ca,), (cb,)), ((), ())), preferred_element_type=F32,
                           precision=lax.Precision.HIGHEST)


@jax.custom_vjp
def cright(x, r):
    return _xdot(x, r, 1, 0)


cright.defvjp(lambda x, r: (cright(x, r), r), lambda r, g: (_xdot(g, r, 1, 1), jnp.zeros_like(r)))


def _shift_raw(x, s):
    n = x.shape[0]
    r = lax.broadcasted_iota(jnp.int32, x.shape, 0)
    y = pltpu.roll(x, s % n, 0)
    return jnp.where((r >= s) & (r < n + s), y, 0.0)


def _seg_shift_raw(x, s, seg, up):
    n = x.shape[0]
    r = lax.broadcasted_iota(jnp.int32, x.shape, 0) % seg
    if up:
        return jnp.where(r < seg - s, pltpu.roll(x, n - s, 0), 0.0)
    return jnp.where(r >= s, pltpu.roll(x, s, 0), 0.0)


@functools.partial(jax.custom_vjp, nondiff_argnums=(1, 2))
def seg_shift(x, s, seg):
    return _seg_shift_raw(x, s, seg, False)


seg_shift.defvjp(lambda x, s, seg: (_seg_shift_raw(x, s, seg, False), None),
                 lambda s, seg, _, g: (_seg_shift_raw(g, s, seg, True),))


def _seg_cumsum(x, seg):
    s = 1
    while s < seg:
        x = x + seg_shift(x, s, seg)
        s *= 2
    return x


def _rms(x, g):
    return x * lax.rsqrt(jnp.mean(x * x, axis=-1, keepdims=True) + EPS) * g


def _silu(x):
    return x * jax.nn.sigmoid(x)


def _mm_call(name, a, b, *, grid, a_spec, b_spec, o_shape, o_spec, dims, acc_shape, res=None, res_spec=None,
             mask_tm=None):
    nk = grid[2]

    def body(*refs):
        if res is None:
            a_ref, b_ref, o_ref = refs[:3]
        else:
            a_ref, b_ref, r_ref, o_ref = refs[:4]
        k = pl.program_id(2)

        def dot():
            return lax.dot_general(a_ref[...].astype(BF), b_ref[...].astype(BF), dims, preferred_element_type=F32)

        def finish(v):
            if res is not None:
                v = v + r_ref[...].astype(F32)
                rows = pl.program_id(0) * mask_tm + lax.broadcasted_iota(jnp.int32, v.shape, 0)
                v = jnp.where(rows >= PAD, v, 0.0)
            o_ref[...] = v.astype(o_ref.dtype)

        if nk == 1:
            finish(dot())
            return
        acc = refs[-1]

        @pl.when(k == 0)
        def _():
            acc[...] = dot()

        @pl.when((k > 0) & (k < nk - 1))
        def _():
            acc[...] += dot()

        @pl.when(k == nk - 1)
        def _():
            finish(acc[...] + dot())

    ins = [a, b] + ([res] if res is not None else [])
    specs = [a_spec, b_spec] + ([res_spec] if res is not None else [])
    scratch = [pltpu.VMEM(acc_shape, F32)] if nk > 1 else []
    return pl.pallas_call(body, name=name, grid=grid, in_specs=specs, out_specs=o_spec, out_shape=o_shape,
                          scratch_shapes=scratch, compiler_params=_cp(3))(*ins)


NN = (((1,), (0,)), ((), ()))
NT = (((1,), (1,)), ((), ()))
TN = (((0,), (0,)), ((), ()))


def _row_tile(lp):
    for t in (832, 640, 320, 64):
        if lp % t == 0:
            return t
    raise ValueError(lp)


def _col_tile(n):
    for t in (1024, 768, 512, 384, 256, 128):
        if n % t == 0:
            return t
    return n


MM_WHOLE_ROWS = 4160


def _mm_rows(m, whole=False):
    if whole and m <= MM_WHOLE_ROWS:
        return m
    return 2080 if m % 2080 == 0 else _row_tile(m)


def _mm_cols(n):
    for t in (512, 384, 256, 128):
        if n % t == 0:
            return t
    return n


def lin(name, a, w, out_dtype=F32, res=None):
    m, k = a.shape
    n = w.shape[1]
    tm, tn, tc = _mm_rows(m, res is None and a.dtype == BF and k <= 1024), _mm_cols(n), _col_tile(k)
    return _mm_call(name, a, w, grid=(m // tm, n // tn, k // tc),
                    a_spec=pl.BlockSpec((tm, tc), lambda i, j, kk: (i, kk)),
                    b_spec=pl.BlockSpec((tc, tn), lambda i, j, kk: (kk, j)),
                    o_shape=SDS((m, n), out_dtype), o_spec=pl.BlockSpec((tm, tn), lambda i, j, kk: (i, j)),
                    dims=NN, acc_shape=(tm, tn), res=res,
                    res_spec=pl.BlockSpec((tm, tn), lambda i, j, kk: (i, j)), mask_tm=tm)


def lin_bo(name, a, wb, out_dtype=F32):
    m, k = a.shape
    nb, _, n = wb.shape
    tm = _mm_rows(m, a.dtype == BF)
    return _mm_call(name, a, wb, grid=(m // tm, nb, 1),
                    a_spec=pl.BlockSpec((tm, k), lambda i, j, kk: (i, 0)),
                    b_spec=pl.BlockSpec((None, k, n), lambda i, j, kk: (j, 0, 0)),
                    o_shape=SDS((nb, m, n), out_dtype), o_spec=pl.BlockSpec((None, tm, n), lambda i, j, kk: (j, i, 0)),
                    dims=NN, acc_shape=(tm, n))


def lin_bi(name, ab, wb, out_dtype=F32, res=None):
    nb, m, k = ab.shape
    n = wb.shape[2]
    tm, tn = _mm_rows(m), _mm_cols(n)
    return _mm_call(name, ab, wb, grid=(m // tm, n // tn, nb),
                    a_spec=pl.BlockSpec((None, tm, k), lambda i, j, kk: (kk, i, 0)),
                    b_spec=pl.BlockSpec((None, k, tn), lambda i, j, kk: (kk, 0, j)),
                    o_shape=SDS((m, n), out_dtype), o_spec=pl.BlockSpec((tm, tn), lambda i, j, kk: (i, j)),
                    dims=NN, acc_shape=(tm, tn), res=res,
                    res_spec=pl.BlockSpec((tm, tn), lambda i, j, kk: (i, j)), mask_tm=tm)


def lin_t(name, g, w, out_dtype=F32):
    m, n = g.shape
    k = w.shape[0]
    tm, tk, tc = _mm_rows(m), _col_tile(k), _col_tile(n)
    return _mm_call(name, g, w, grid=(m // tm, k // tk, n // tc),
                    a_spec=pl.BlockSpec((tm, tc), lambda i, j, kk: (i, kk)),
                    b_spec=pl.BlockSpec((tk, tc), lambda i, j, kk: (j, kk)),
                    o_shape=SDS((m, k), out_dtype), o_spec=pl.BlockSpec((tm, tk), lambda i, j, kk: (i, j)),
                    dims=NT, acc_shape=(tm, tk))


def lin_t_bi(name, gb, wb, out_dtype=F32):
    nb, m, n = gb.shape
    k = wb.shape[1]
    tm, tk = _mm_rows(m), _col_tile(k)
    return _mm_call(name, gb, wb, grid=(m // tm, k // tk, nb),
                    a_spec=pl.BlockSpec((None, tm, n), lambda i, j, kk: (kk, i, 0)),
                    b_spec=pl.BlockSpec((None, tk, n), lambda i, j, kk: (kk, j, 0)),
                    o_shape=SDS((m, k), out_dtype), o_spec=pl.BlockSpec((tm, tk), lambda i, j, kk: (i, j)),
                    dims=NT, acc_shape=(tm, tk))


def lin_t_bo(name, g, wb, out_dtype=F32):
    m, n = g.shape
    nb, k, _ = wb.shape
    tm = _mm_rows(m)
    return _mm_call(name, g, wb, grid=(m // tm, nb, 1),
                    a_spec=pl.BlockSpec((tm, n), lambda i, j, kk: (i, 0)),
                    b_spec=pl.BlockSpec((None, k, n), lambda i, j, kk: (j, 0, 0)),
                    o_shape=SDS((nb, m, k), out_dtype), o_spec=pl.BlockSpec((None, tm, k), lambda i, j, kk: (j, i, 0)),
                    dims=NT, acc_shape=(tm, k))


def wgrad(name, a, g):
    m, k = a.shape
    n = g.shape[1]
    tm, tn = _mm_rows(m), (_col_tile(n) if k <= 1024 else _mm_cols(n))
    return _mm_call(name, a, g, grid=(1, n // tn, m // tm),
                    a_spec=pl.BlockSpec((tm, k), lambda i, j, kk: (kk, 0)),
                    b_spec=pl.BlockSpec((tm, tn), lambda i, j, kk: (kk, j)),
                    o_shape=SDS((k, n), F32), o_spec=pl.BlockSpec((k, tn), lambda i, j, kk: (0, j)),
                    dims=TN, acc_shape=(k, tn))


def wgrad_bo(name, a, gb):
    m, k = a.shape
    nb, _, n = gb.shape
    tm = _mm_rows(m)
    return _mm_call(name, a, gb, grid=(nb, 1, m // tm),
                    a_spec=pl.BlockSpec((tm, k), lambda i, j, kk: (kk, 0)),
                    b_spec=pl.BlockSpec((None, tm, n), lambda i, j, kk: (i, kk, 0)),
                    o_shape=SDS((nb, k, n), F32), o_spec=pl.BlockSpec((None, k, n), lambda i, j, kk: (i, 0, 0)),
                    dims=TN, acc_shape=(k, n))


def wgrad_bi(name, zb, g):
    nb, m, k = zb.shape
    n = g.shape[1]
    tm, tn = _mm_rows(m), _col_tile(n)
    return _mm_call(name, zb, g, grid=(nb, n // tn, m // tm),
                    a_spec=pl.BlockSpec((None, tm, k), lambda i, j, kk: (i, kk, 0)),
                    b_spec=pl.BlockSpec((tm, tn), lambda i, j, kk: (kk, j)),
                    o_shape=SDS((nb, k, n), F32), o_spec=pl.BlockSpec((None, k, tn), lambda i, j, kk: (i, 0, j)),
                    dims=TN, acc_shape=(k, tn))


class Arg:
    def __init__(self, arr, block, imap, shared=False, acc=False):
        self.arr, self.block, self.imap = arr, block, imap
        self.shared = shared
        self.acc = acc

    @property
    def spec(self):
        return pl.BlockSpec(self.block, self.imap)

    def vshape(self):
        return tuple(b for b in self.block if b is not None)


def _rev(arg, nt):
    return pl.BlockSpec(arg.block, lambda o, t, _f=arg.imap: _f(o, nt - 1 - t))


def seq_fwd(name, fn, grid, params, consts, xs, outs, carries=(), save_dtype=F32):
    no, nt = grid
    n_p, n_c, n_x, n_y, n_k = len(params), len(consts), len(xs), len(outs), len(carries)

    def body(*refs):
        p_refs = refs[:n_p]
        c_refs = refs[n_p:n_p + n_c]
        x_refs = refs[n_p + n_c:n_p + n_c + n_x]
        r = n_p + n_c + n_x
        y_refs = refs[r:r + n_y]
        s_refs = refs[r + n_y:r + n_y + n_k]
        k_refs = refs[r + n_y + n_k:]
        t = pl.program_id(1)

        if n_k:
            @pl.when(t == 0)
            def _():
                for k in k_refs:
                    k[...] = jnp.zeros_like(k)

        carry = tuple(k[...] for k in k_refs)
        for s, c in zip(s_refs, carry):
            s[...] = c.astype(s.dtype)
        new_carry, ys = fn(tuple(p[...] for p in p_refs), tuple(c[...] for c in c_refs), carry,
                           tuple(x[...] for x in x_refs))
        for k, c in zip(k_refs, new_carry):
            k[...] = c
        for y_ref, y in zip(y_refs, ys):
            y_ref[...] = y.astype(y_ref.dtype)

    out_shape = [SDS(s, d) for (s, d, _, _) in outs]
    out_specs = [pl.BlockSpec(b, im) for (_, _, b, im) in outs]
    for cs in carries:
        out_shape.append(SDS((no, nt) + cs, save_dtype))
        out_specs.append(pl.BlockSpec((None, None) + cs, lambda o, t, _n=len(cs): (o, t) + (0,) * _n))
    res = pl.pallas_call(
        body, name=name, grid=grid, in_specs=[a.spec for a in list(params) + list(consts) + list(xs)],
        out_specs=out_specs, out_shape=out_shape, scratch_shapes=[pltpu.VMEM(cs, F32) for cs in carries],
        compiler_params=_cp(2))(*[a.arr for a in list(params) + list(consts) + list(xs)])
    return res[:n_y], res[n_y:]


def seq_bwd(name, fn, grid, params, consts, xs, dys, saved=(), carries=()):
    no, nt = grid
    n_p, n_c, n_x, n_y, n_k = len(params), len(consts), len(xs), len(dys), len(carries)

    def body(*refs):
        p_refs = refs[:n_p]
        c_refs = refs[n_p:n_p + n_c]
        x_refs = refs[n_p + n_c:n_p + n_c + n_x]
        r = n_p + n_c + n_x
        g_refs = refs[r:r + n_y]
        s_refs = refs[r + n_y:r + n_y + n_k]
        r = r + n_y + n_k
        dx_refs = refs[r:r + n_x]
        dp_refs = refs[r + n_x:r + n_x + n_p]
        k_refs = refs[r + n_x + n_p:]
        o = pl.program_id(0)
        t = pl.program_id(1)

        if n_k:
            @pl.when(t == 0)
            def _():
                for k in k_refs:
                    k[...] = jnp.zeros_like(k)

        for a, dp in zip(params, dp_refs):
            @pl.when((t == 0) & (o == 0) if a.shared else (t == 0))
            def _(dp=dp):
                dp[...] = jnp.zeros_like(dp)

        for a, dx in zip(xs, dx_refs):
            if a.acc:
                @pl.when(t == 0)
                def _(dx=dx):
                    dx[...] = jnp.zeros_like(dx)

        consts_v = tuple(c[...] for c in c_refs)

        def f(pv, cv, xv):
            return fn(pv, consts_v, cv, xv)

        pv = tuple(p[...] for p in p_refs)
        cv = tuple(s[...].astype(F32) for s in s_refs)
        xv = tuple(x[...] for x in x_refs)
        (new_carry, ys), vjp = jax.vjp(f, pv, cv, xv)
        cot = (tuple(k[...] for k in k_refs), tuple(g[...].astype(y.dtype) for g, y in zip(g_refs, ys)))
        dpv, dcv, dxv = vjp(cot)
        for k, c in zip(k_refs, dcv):
            k[...] = c
        for dp, v in zip(dp_refs, dpv):
            dp[...] += v
        for a, dx, v in zip(xs, dx_refs, dxv):
            if a.acc:
                dx[...] += v
            else:
                dx[...] = v.astype(dx.dtype)

    in_specs = ([_rev(a, nt) for a in list(params) + list(consts) + list(xs) + list(dys)]
                + [pl.BlockSpec((None, None) + cs, lambda o, t, _n=len(cs): (o, nt - 1 - t) + (0,) * _n) for cs in carries])
    out_shape = [SDS(a.arr.shape, F32) for a in xs] + [SDS(a.arr.shape, F32) for a in params]
    out_specs = [_rev(a, nt) for a in list(xs) + list(params)]
    res = pl.pallas_call(
        body, name=name, grid=grid, in_specs=in_specs, out_specs=out_specs, out_shape=out_shape,
        scratch_shapes=[pltpu.VMEM(cs, F32) for cs in carries], compiler_params=_cp(2))(
            *[a.arr for a in list(params) + list(consts) + list(xs) + list(dys)], *saved)
    return res[:n_x], res[n_x:]


def _rowmask(lp):
    return (jnp.arange(lp) >= PAD).astype(F32)[:, None]


def _norm_fn(p, c, k, x):
    return (), (_rms(x[0] * c[0], p[0]),)


def _norm_b_fn(p, c, k, x):
    h = x[0] * c[0]
    return (), (_rms(h, p[0]), h)


def norm_fwd(name, h, g, mask, out_dtype=BF):
    lp, d = h.shape
    tr = _row_tile(lp)
    row = lambda o, t: (t, 0)
    (a,), _ = seq_fwd(name, _norm_fn, (1, lp // tr), [Arg(g, (1, d), lambda o, t: (0, 0), shared=True)],
                      [Arg(mask, (tr, 1), row)], [Arg(h, (tr, d), row)], [((lp, d), out_dtype, (tr, d), row)])
    return a


def norm_bwd(name, h, g, mask, da, dskip):
    lp, d = h.shape
    tr = _row_tile(lp)
    row = lambda o, t: (t, 0)
    (dh,), (dg,) = seq_bwd(name, _norm_b_fn, (1, lp // tr), [Arg(g, (1, d), lambda o, t: (0, 0), shared=True)],
                           [Arg(mask, (tr, 1), row)], [Arg(h, (tr, d), row)],
                           [Arg(da, (tr, d), row), Arg(dskip, (tr, d), row)])
    return dh, dg


def _ffn_tile(lp):
    return 320 if (lp % 320 == 0 and lp > 320) else 64


def _conv_rows(ext, w, b, n):
    u2 = ext[8:8 + n]
    u1 = pltpu.roll(ext, 1, 0)[8:8 + n]
    u0 = pltpu.roll(ext, 2, 0)[8:8 + n]
    return w[2] * u2 + w[1] * u1 + w[0] * u0 + b, (u0, u1, u2)


def ffn_up_core(name, a, wup, cw, cb):
    lp, kd = a.shape
    _, nj, _, fb = wup.shape
    tr = 416 if (lp % 416 == 0 and lp > 416) else _ffn_tile(lp)
    nt = lp // tr

    def body(a_ref, wu_ref, w_ref, b_ref, u_ref, z_ref, u_s, halo_s):
        i = pl.program_id(1)

        @pl.when(i == 0)
        def _():
            u_s[...] = jnp.zeros_like(u_s)
            halo_s[...] = jnp.zeros_like(halo_s)

        old = (i + 1) % 2
        cs = []
        for s in range(2):
            tile = u_s[old, s]
            ext = jnp.concatenate([halo_s[s], tile], axis=0)
            c, _ = _conv_rows(ext, w_ref[s], b_ref[s], tr)
            cs.append(c)
            halo_s[s] = tile[tr - 8:]
        z_ref[...] = (_silu(cs[0]) * cs[1]).astype(z_ref.dtype)
        for s in range(2):
            un = lax.dot_general(a_ref[...], wu_ref[s], NN, preferred_element_type=F32)
            u_ref[s] = un
            u_s[i % 2, s] = un

    cur = lambda j, i: (0, j, jnp.minimum(i, nt - 1), 0)
    return pl.pallas_call(
        body, name=name, grid=(nj, nt + 1),
        in_specs=[pl.BlockSpec((tr, kd), lambda j, i: (jnp.minimum(i, nt - 1), 0)),
                  pl.BlockSpec((2, None, kd, fb), lambda j, i: (0, j, 0, 0)),
                  pl.BlockSpec((2, None, 3, 1, fb), lambda j, i: (0, j, 0, 0, 0)),
                  pl.BlockSpec((2, None, 1, fb), lambda j, i: (0, j, 0, 0))],
        out_specs=[pl.BlockSpec((2, None, tr, fb), cur),
                   pl.BlockSpec((None, tr, fb), lambda j, i: (j, jnp.maximum(i - 1, 0), 0))],
        out_shape=[SDS((2, nj, lp, fb), F32), SDS((nj, lp, fb), BF)],
        scratch_shapes=[pltpu.VMEM((2, 2, tr, fb), F32), pltpu.VMEM((2, 8, fb), F32)],
        compiler_params=_cp(2))(a, wup, cw, cb)


def ffn_core_bwd(name, u, dz, cw, cb):
    _, nj, lp, fb = u.shape
    tr = _ffn_tile(lp)
    nt = lp // tr
    nb8 = lp // 8

    def body(u_ref, up_ref, un_ref, dz_ref, dzn_ref, w_ref, b_ref, du_ref, dw_ref, db_ref):
        i = pl.program_id(1)

        @pl.when(i == 0)
        def _():
            dw_ref[...] = jnp.zeros_like(dw_ref)
            db_ref[...] = jnp.zeros_like(db_ref)

        prev = jnp.where(i > 0, up_ref[...], 0.0)
        nxt = jnp.where(i < nt - 1, un_ref[...], 0.0)
        dz_e = jnp.concatenate([dz_ref[...], jnp.where(i < nt - 1, dzn_ref[...], 0.0)], axis=0)
        n = tr + 8
        cs, taps = [], []
        for s in range(2):
            ext = jnp.concatenate([prev[s], u_ref[s], nxt[s]], axis=0)
            c, tp = _conv_rows(ext, w_ref[s], b_ref[s], n)
            cs.append(c)
            taps.append(tp)
        sg = jax.nn.sigmoid(cs[0])
        dcs = [dz_e * cs[1] * sg * (1.0 + cs[0] * (1.0 - sg)), dz_e * cs[0] * sg]
        for s in range(2):
            dc = dcs[s]
            w = w_ref[s]
            d1 = pltpu.roll(dc, n - 1, 0)[:tr]
            d2 = pltpu.roll(dc, n - 2, 0)[:tr]
            dcm = dc[:tr]
            du_ref[s] = w[2] * dcm + w[1] * d1 + w[0] * d2
            for k in range(3):
                dw_ref[s, k] += jnp.sum(dcm * taps[s][k][:tr], axis=0, keepdims=True)
            db_ref[s] += jnp.sum(dcm, axis=0, keepdims=True)

    return pl.pallas_call(
        body, name=name, grid=(nj, nt),
        in_specs=[pl.BlockSpec((2, None, tr, fb), lambda j, i: (0, j, i, 0)),
                  pl.BlockSpec((2, None, 8, fb), lambda j, i: (0, j, jnp.maximum(i * (tr // 8) - 1, 0), 0)),
                  pl.BlockSpec((2, None, 8, fb), lambda j, i: (0, j, jnp.minimum((i + 1) * (tr // 8), nb8 - 1), 0)),
                  pl.BlockSpec((None, tr, fb), lambda j, i: (j, i, 0)),
                  pl.BlockSpec((None, 8, fb), lambda j, i: (j, jnp.minimum((i + 1) * (tr // 8), nb8 - 1), 0)),
                  pl.BlockSpec((2, None, 3, 1, fb), lambda j, i: (0, j, 0, 0, 0)),
                  pl.BlockSpec((2, None, 1, fb), lambda j, i: (0, j, 0, 0))],
        out_specs=[pl.BlockSpec((2, None, tr, fb), lambda j, i: (0, j, i, 0)),
                   pl.BlockSpec((2, None, 3, 1, fb), lambda j, i: (0, j, 0, 0, 0)),
                   pl.BlockSpec((2, None, 1, fb), lambda j, i: (0, j, 0, 0))],
        out_shape=[SDS(u.shape, F32), SDS(cw.shape, F32), SDS(cb.shape, F32)],
        compiler_params=_cp(2))(u, u, u, dz, dz, cw, cb)


def ffn_fwd(i, h, mask, w):
    a = norm_fwd(f"ffn{i}_norm", h, w["ng"], mask)
    u, z = ffn_up_core(f"ffn{i}_up_core", a, w["up"].reshape(2, 4, D, FFN_B), w["cw"], w["cb"])
    h2 = lin_bi(f"ffn{i}_down", z, w["down"], res=h)
    return h2, (h, a, u, z)


def ffn_bwd(i, dh2, mask, w, saved):
    h, a, u, z = saved
    lp = h.shape[0]
    g = {}
    g["down"] = wgrad_bi(f"ffn{i}_dwdown", z, dh2)
    dz = lin_t_bo(f"ffn{i}_dz", dh2, w["down"])
    du, g["cw"], g["cb"] = ffn_core_bwd(f"ffn{i}_core_b", u, dz, w["cw"], w["cb"])
    du = du.reshape(8, lp, FFN_B)
    g["up"] = wgrad_bo(f"ffn{i}_dwup", a, du)
    da = lin_t_bi(f"ffn{i}_da", du, w["up"])
    dh, g["ng"] = norm_bwd(f"ffn{i}_norm_b", h, w["ng"], mask, da, dh2)
    return dh, g


HG_HB = 4


def _hgrn_fn(p, c, k, x):
    lb, go = p
    q, f, iv, g = x[0][0], x[0][1], x[0][2], x[0][3]
    (st_all,) = k
    qs = _silu(q)
    forget = lb + (1.0 - lb) * jax.nn.sigmoid(f)
    logf = jnp.log(forget)
    kk = 1.0 - forget
    gc_all = _seg_cumsum(logf, HG_C)
    r = lax.broadcasted_iota(jnp.int32, (HG_C, HG_C), 0)
    cc = lax.broadcasted_iota(jnp.int32, (HG_C, HG_C), 1)
    ns = CH // HG_C
    cells = [(j, s) for j in range(HG_HB) for s in range(ns)]

    def blk(t, j, s):
        return t[HG_C * s:HG_C * (s + 1), HG_D * j:HG_D * (j + 1)]

    gl = {c: jnp.sum(blk(logf, *c), axis=0, keepdims=True) for c in cells}
    qd = {c: blk(qs, *c) * jnp.exp(blk(gc_all, *c)) for c in cells}
    ki = {c: blk(kk, *c) * jnp.exp(-blk(gc_all, *c)) for c in cells}
    up = {c: mm_tn(blk(iv, *c), blk(kk, *c) * jnp.exp(gl[c] - blk(gc_all, *c))) for c in cells}
    st, sts = {}, []
    for j in range(HG_HB):
        cur = st_all[j]
        for s in range(ns):
            st[(j, s)] = cur
            cur = cur * jnp.exp(gl[(j, s)]) + up[(j, s)]
        sts.append(cur)
    both = {c: mm_nt(qd[c], jnp.concatenate([st[c], ki[c]], axis=0)) for c in cells}
    oc = {c: mm(jnp.where(r >= cc, both[c][:, HG_D:], 0.0), blk(iv, *c)) + both[c][:, :HG_D] for c in cells}
    zs = []
    for j in range(HG_HB):
        o = jnp.concatenate([oc[(j, s)] for s in range(ns)], axis=0)
        zs.append(_rms(o, go) * _silu(g[:, HG_D * j:HG_D * (j + 1)]))
    return (jnp.stack(sts, axis=0),), (jnp.concatenate(zs, axis=1),)


def _hgrn_args(u4, lb, go):
    lp = u4.shape[2]
    wb = HG_HB * HG_D
    xs = [Arg(u4, (4, None, CH, wb), lambda o, t: (0, o, t, 0))]
    ps = [Arg(lb, (1, wb), lambda o, t: (0, o)), Arg(go, (1, HG_D), lambda o, t: (0, 0), shared=True)]
    return (HG_H // HG_HB, lp // CH), ps, xs


def hgrn_fwd(h, mask, w):
    lp = h.shape[0]
    a = norm_fwd("hgrn_norm", h, w["ng"], mask)
    u4 = lin_bo("hgrn_in", a, w["win"]).reshape(4, HG_H // HG_HB, lp, HG_HB * HG_D)
    grid, ps, xs = _hgrn_args(u4, w["lb"], w["go"])
    (z,), (st,) = seq_fwd("hgrn_core", _hgrn_fn, grid, ps, [], xs,
                          [((lp, D), BF, (CH, HG_HB * HG_D), lambda o, t: (t, o))], carries=[(HG_HB, HG_D, HG_D)])
    h2 = lin("hgrn_out", z, w["wo"], res=h)
    return h2, (h, a, u4, z, st)


def hgrn_bwd(dh2, mask, w, saved):
    h, a, u4, z, st = saved
    lp = h.shape[0]
    g = {}
    g["wo"] = wgrad("hgrn_dwo", z, dh2)
    dz = lin_t("hgrn_dz", dh2, w["wo"])
    grid, ps, xs = _hgrn_args(u4, w["lb"], w["go"])
    (du4,), (g["lb"], g["go"]) = seq_bwd("hgrn_core_b", _hgrn_fn, grid, ps, [], xs,
                                         [Arg(dz, (CH, HG_HB * HG_D), lambda o, t: (t, o))], saved=[st],
                                         carries=[(HG_HB, HG_D, HG_D)])
    du = du4.reshape(N_DEV, lp, HG_HB * HG_D)
    g["win"] = wgrad_bo("hgrn_dwin", a, du)
    da = lin_t_bi("hgrn_da", du, w["win"])
    dh, g["ng"] = norm_bwd("hgrn_norm_b", h, w["ng"], mask, da, dh2)
    return dh, g


S5_W = S5_SG * S5_P


def s5_tables(lam_re, lam_im, log_dt, b_re, b_im, c_re, c_im):
    dt = jnp.exp(log_dt)[:, None]
    mag = jnp.exp(lam_re * dt)
    abar_re = mag * jnp.cos(lam_im * dt)
    abar_im = mag * jnp.sin(lam_im * dt)
    den = lam_re * lam_re + lam_im * lam_im
    zoh_re = ((abar_re - 1.0) * lam_re + abar_im * lam_im) / den
    zoh_im = (abar_im * lam_re - (abar_re - 1.0) * lam_im) / den
    bbar_re = zoh_re[..., None] * b_re - zoh_im[..., None] * b_im
    bbar_im = zoh_re[..., None] * b_im + zoh_im[..., None] * b_re
    eye = jnp.eye(S5_SG, dtype=F32)

    def blockdiag_in(b):
        t = b.reshape(N_DEV, S5_SG, S5_P, S5_K).transpose(0, 1, 3, 2)
        return jnp.einsum("jakp,ab->jakbp", t, eye).reshape(N_DEV, S5_SG * S5_K, S5_W)

    def blockdiag_out(c):
        t = c.reshape(N_DEV, S5_SG, S5_K, S5_P).transpose(0, 1, 3, 2)
        return jnp.einsum("japk,ab->japbk", t, eye).reshape(N_DEV, S5_W, S5_SG * S5_K)

    wb = jnp.concatenate([blockdiag_in(bbar_re), blockdiag_in(bbar_im)], axis=2)
    wc = jnp.concatenate([blockdiag_out(c_re), -blockdiag_out(c_im)], axis=1)

    abar = jnp.concatenate([abar_re.reshape(N_DEV, 1, S5_W), abar_im.reshape(N_DEV, 1, S5_W)], axis=2)
    return wb, wc, abar


def _cmul(ar, ai, xr, xi):
    return ar * xr - ai * xi, ar * xi + ai * xr


def _scan_rows(ar, ai, xr, xi, reverse):
    for s in range(6):
        sh = -(1 << s) if reverse else (1 << s)
        dr, di = _cmul(ar, ai, _shift_raw(xr, sh), _shift_raw(xi, sh))
        xr, xi = xr + dr, xi + di
        ar, ai = ar * ar - ai * ai, 2.0 * ar * ai
    return xr, xi


@jax.custom_vjp
def cscan(ar, ai, br, bi):
    return _scan_rows(ar, ai, br, bi, False)


def _cscan_fwd(ar, ai, br, bi):
    xr, xi = _scan_rows(ar, ai, br, bi, False)
    return (xr, xi), (ar, ai, xr, xi)


def _cscan_bwd(res, g):
    ar, ai, xr, xi = res
    lr, li = _scan_rows(ar, -ai, g[0], g[1], True)
    pr, pi = _shift_raw(xr, 1), _shift_raw(xi, 1)
    dar = jnp.sum(lr * pr + li * pi, axis=0, keepdims=True)
    dai = jnp.sum(li * pr - lr * pi, axis=0, keepdims=True)
    return dar, dai, lr, li


cscan.defvjp(_cscan_fwd, _cscan_bwd)

S5_BB = 8


def _s5_fn(p, c, k, x):
    wb, wc, abar, dsk = p
    (a,) = x
    (x0,) = k
    blocks = range(S5_BB)
    aj = [a[:, 128 * j:128 * (j + 1)] for j in blocks]
    bu = [mm(aj[j], wb[j]) for j in blocks]
    first = lax.broadcasted_iota(jnp.int32, (CH, S5_W), 0) == 0
    last = lax.broadcasted_iota(jnp.int32, (CH, 2 * S5_W), 0) == CH - 1
    xxs, x0n = [], []
    for j in blocks:
        ar, ai = abar[j][:, :S5_W], abar[j][:, S5_W:]
        cr, ci = _cmul(ar, ai, x0[j][:, :S5_W], x0[j][:, S5_W:])
        xr, xi = cscan(ar, ai, bu[j][:, :S5_W] + jnp.where(first, cr, 0.0), bu[j][:, S5_W:] + jnp.where(first, ci, 0.0))
        xx = jnp.concatenate([xr, xi], axis=1)
        x0n.append(jnp.sum(jnp.where(last, xx, 0.0), axis=0, keepdims=True))
        xxs.append(xx)
    y = jnp.concatenate([mm(xxs[j], wc[j]) for j in blocks], axis=1)
    return (jnp.stack(x0n, axis=0),), (jax.nn.gelu(y + dsk * a),)


def _s5_args(a, tb, dsk):
    lp = a.shape[0]
    wb, wc, abar = tb
    ps = [Arg(wb, (S5_BB, 128, 2 * S5_W), lambda o, t: (o, 0, 0)), Arg(wc, (S5_BB, 2 * S5_W, 128), lambda o, t: (o, 0, 0)),
          Arg(abar, (S5_BB, 1, 2 * S5_W), lambda o, t: (o, 0, 0)), Arg(dsk, (1, 128 * S5_BB), lambda o, t: (0, o))]
    xs = [Arg(a, (CH, 128 * S5_BB), lambda o, t: (t, o))]
    return (N_DEV // S5_BB, lp // CH), ps, xs


def _glu_res_fn(p, c, k, x):
    h, vg = x
    return (), ((h + vg[:, :D] * jax.nn.sigmoid(vg[:, D:])) * c[0],)


def _glu_args(h, vg, mask):
    lp = h.shape[0]
    tr = _row_tile(lp)
    row = lambda o, t: (t, 0)
    return (1, lp // tr), [Arg(mask, (tr, 1), row)], [Arg(h, (tr, D), row), Arg(vg, (tr, 2 * D), row)], tr


def s5_fwd(h, mask, w):
    lp = h.shape[0]
    a = norm_fwd("s5_norm", h, w["ng"], mask, out_dtype=F32)
    grid, ps, xs = _s5_args(a, w["tb"], w["dsk"])
    (z,), (st,) = seq_fwd("s5_core", _s5_fn, grid, ps, [], xs,
                          [((lp, D), BF, (CH, 128 * S5_BB), lambda o, t: (t, o))], carries=[(S5_BB, 1, 2 * S5_W)])
    vg = lin("s5_glu", z, w["wglu"])
    grid2, cs, xs2, tr = _glu_args(h, vg, mask)
    (h2,), _ = seq_fwd("s5_res", _glu_res_fn, grid2, [], cs, xs2, [((lp, D), F32, (tr, D), lambda o, t: (t, 0))])
    return h2, (h, a, z, vg, st)


def s5_bwd(dh2, mask, w, saved):
    h, a, z, vg, st = saved
    g = {}
    grid2, cs, xs2, tr = _glu_args(h, vg, mask)
    (dskip, dvg), _ = seq_bwd("s5_res_b", _glu_res_fn, grid2, [], cs, xs2, [Arg(dh2, (tr, D), lambda o, t: (t, 0))])
    g["wglu"] = wgrad("s5_dwglu", z, dvg)
    dz = lin_t("s5_dz", dvg, w["wglu"])
    grid, ps, xs = _s5_args(a, w["tb"], w["dsk"])
    (da,), dps = seq_bwd("s5_core_b", _s5_fn, grid, ps, [], xs, [Arg(dz, (CH, 128 * S5_BB), lambda o, t: (t, o))],
                         saved=[st], carries=[(S5_BB, 1, 2 * S5_W)])
    g["tb"] = tuple(dps[:3])
    g["dsk"] = dps[3]
    dh, g["ng"] = norm_bwd("s5_norm_b", h, w["ng"], mask, da, dskip)
    return dh, g


def _rope_angles(lp, dim):
    pos = np.maximum(np.arange(lp, dtype=np.float32) - PAD, 0.0).astype(np.float32)
    inv = (1.0 / (ROPE_BASE ** (np.arange(0, dim, 2, dtype=np.float32) / dim))).astype(np.float32)
    return (pos[:, None] * inv[None, :]).astype(np.float32)


def ret_consts(lp):
    f = np.float32
    ang = _rope_angles(lp, RET_DK)
    lg = np.log(1.0 - np.exp2(-5.0 - np.arange(RET_H, dtype=f))).astype(f)
    p = np.arange(CH, dtype=f)
    diff = p[:, None] - p[None, :]
    decay = np.where(diff >= 0, np.exp(diff[None] * lg[:, None, None]), 0.0).astype(f)
    qd = np.exp((p[None, :] + 1.0) * lg[:, None])[..., None].astype(f)
    kd = np.exp((CH - 1.0 - p[None, :]) * lg[:, None])[..., None].astype(f)
    cd = np.exp(CH * lg)[:, None, None].astype(f)
    return np.cos(ang).astype(f), np.sin(ang).astype(f), decay, qd, kd, cd


def _ret_fn(p, c, k, x):
    (gn,) = p
    cos, sin, decay, qd, kd, cd = c
    (st,) = k
    (u,) = x
    hd = RET_DK // 2
    qk_w = RET_H * RET_DK
    heads = range(RET_H)

    def rope(t):
        t1, t2 = t[:, :hd], t[:, hd:]
        return jnp.concatenate([t1 * cos - t2 * sin, t1 * sin + t2 * cos], axis=1)

    qr = [rope(u[:, RET_DK * h:RET_DK * (h + 1)]) for h in heads]
    kr = [rope(u[:, qk_w + RET_DK * h:qk_w + RET_DK * (h + 1)]) * (RET_DK ** -0.5) for h in heads]
    v = [u[:, 2 * qk_w + RET_DV * h:2 * qk_w + RET_DV * (h + 1)] for h in heads]
    scores = [mm_nt(qr[h], kr[h]) for h in heads]
    inter = [mm(qr[h] * qd[h], st[h]) for h in heads]
    st_new = jnp.stack([st[h] * cd[h] + mm_tn(kr[h] * kd[h], v[h]) for h in heads], axis=0)
    o = [mm(scores[h] * decay[h], v[h]) + inter[h] for h in heads]
    zs = []
    for h in heads:
        mu = jnp.mean(o[h], axis=-1, keepdims=True)
        var = jnp.mean(jnp.square(o[h] - mu), axis=-1, keepdims=True)
        gate = u[:, 2 * qk_w + RET_H * RET_DV + RET_DV * h:2 * qk_w + RET_H * RET_DV + RET_DV * (h + 1)]
        zs.append((o[h] - mu) * lax.rsqrt(var + EPS) * gn[:, RET_DV * h:RET_DV * (h + 1)] * _silu(gate))
    return (st_new,), (jnp.concatenate(zs, axis=1),)


def _ret_args(u, gn, rc):
    lp, uw = u.shape
    cos, sin, decay, qd, kd, cd = rc
    full = lambda o, t: (0, 0, 0)
    ps = [Arg(gn, (1, RET_H * RET_DV), lambda o, t: (0, 0))]
    cs = [Arg(cos, (CH, RET_DK // 2), lambda o, t: (t, 0)), Arg(sin, (CH, RET_DK // 2), lambda o, t: (t, 0)),
          Arg(decay, (RET_H, CH, CH), full), Arg(qd, (RET_H, CH, 1), full), Arg(kd, (RET_H, CH, 1), full),
          Arg(cd, (RET_H, 1, 1), full)]
    xs = [Arg(u, (CH, uw), lambda o, t: (t, 0))]
    return (1, lp // CH), ps, cs, xs


def ret_fwd(h, mask, w):
    lp = h.shape[0]
    a = norm_fwd("ret_norm", h, w["ng"], mask)
    u = lin("ret_in", a, w["win"])
    grid, ps, cs, xs = _ret_args(u, w["gn"], w["rc"])
    (z,), (st,) = seq_fwd("ret_core", _ret_fn, grid, ps, cs, xs,
                          [((lp, 2 * D), BF, (CH, RET_H * RET_DV), lambda o, t: (t, 0))],
                          carries=[(RET_H, RET_DK, RET_DV)], save_dtype=BF)
    h2 = lin("ret_out", z, w["wo"], res=h)
    return h2, (h, a, u, z, st)


def ret_bwd(dh2, mask, w, saved):
    h, a, u, z, st = saved
    g = {}
    g["wo"] = wgrad("ret_dwo", z, dh2)
    dz = lin_t("ret_dz", dh2, w["wo"])
    grid, ps, cs, xs = _ret_args(u, w["gn"], w["rc"])
    (du,), (g["gn"],) = seq_bwd("ret_core_b", _ret_fn, grid, ps, cs, xs,
                                [Arg(dz, (CH, RET_H * RET_DV), lambda o, t: (t, 0))], saved=[st],
                                carries=[(RET_H, RET_DK, RET_DV)])
    g["win"] = wgrad("ret_dwin", a, du)
    da = lin_t("ret_da", du, w["win"])
    dh, g["ng"] = norm_bwd("ret_norm_b", h, w["ng"], mask, da, dh2)
    return dh, g


def mla_consts(lp):
    ang = _rope_angles(lp, MLA_ROPE)
    cos = np.concatenate([np.cos(ang), np.cos(ang)], axis=1).astype(np.float32)
    sin = np.concatenate([np.sin(ang), np.sin(ang)], axis=1).astype(np.float32)
    hd = MLA_ROPE // 2
    i = np.arange(hd)
    rot = np.zeros((MLA_ROPE, MLA_ROPE), np.float32)
    rot[hd + i, i] = -1.0
    rot[i, hd + i] = 1.0
    return cos, sin, rot


def _mla_prep1_fn(p, c, k, x):
    gq, gkv = p
    (down,) = x
    return (), (_rms(down[:, :MLA_QL], gq), _rms(down[:, MLA_QL:MLA_QL + MLA_KVL], gkv), down[:, MLA_QL + MLA_KVL:])


def _mla_prep2(p, c, x):
    gq, gk = p
    cos, sin, rot = c
    q, kv, kpe = x
    qn = _rms(q, gq)
    qn_n, qn_r = qn[:, :MLA_NOPE], qn[:, MLA_NOPE:]
    qo = jnp.concatenate([qn_n, qn_r * cos + cright(qn_r, rot) * sin], axis=1)
    kn = kv[:, :MLA_NOPE]
    ms = (jnp.sum(kn * kn, axis=-1, keepdims=True) + jnp.sum(kpe * kpe, axis=-1, keepdims=True)) / MLA_QK
    r = lax.rsqrt(ms + EPS)
    kr = kpe * r * gk[:, MLA_NOPE:]
    ko = jnp.concatenate([kn * r * gk[:, :MLA_NOPE], kr * cos + cright(kr, rot) * sin], axis=1)
    return qo, ko, kv[:, MLA_NOPE:]


def _mla_prep2_fn(p, c, k, x):
    return (), _mla_prep2(p, c, x)[:2]


def _mla_prep2_b_fn(p, c, k, x):
    return (), _mla_prep2(p, c, x)


def _prep1_args(down, gq, gkv):
    lp = down.shape[0]
    tr = _row_tile(lp)
    ps = [Arg(gq, (1, MLA_QL), lambda o, t: (0, 0), shared=True), Arg(gkv, (1, MLA_KVL), lambda o, t: (0, 0), shared=True)]
    return (1, lp // tr), ps, [Arg(down, (tr, down.shape[1]), lambda o, t: (t, 0))], tr


def _prep2_args(qraw, kvraw, kpe, gq, gk, mc):
    lp = kpe.shape[0]
    tr = _row_tile(lp)
    cos, sin, rot = mc
    ps = [Arg(gq, (1, MLA_QK), lambda o, t: (0, 0), shared=True), Arg(gk, (1, MLA_QK), lambda o, t: (0, 0), shared=True)]
    cs = [Arg(cos, (tr, MLA_ROPE), lambda o, t: (o, 0)), Arg(sin, (tr, MLA_ROPE), lambda o, t: (o, 0)),
          Arg(rot, (MLA_ROPE, MLA_ROPE), lambda o, t: (0, 0))]
    xs = [Arg(qraw, (None, tr, MLA_QK), lambda o, t: (t, o, 0)), Arg(kvraw, (None, tr, MLA_NOPE + MLA_V), lambda o, t: (t, o, 0)),
          Arg(kpe, (tr, MLA_ROPE), lambda o, t: (o, 0), acc=True)]
    return (lp // tr, MLA_H), ps, cs, xs, tr


ATT_HB = 2


def _attn_tile(lp):
    return 832 if (lp % 832 == 0 and lp > 832) else 64


def _attn_mask(qi, ki, ta):
    rows = qi * ta + lax.broadcasted_iota(jnp.int32, (ta, ta), 0)
    cols = ki * ta + lax.broadcasted_iota(jnp.int32, (ta, ta), 1)
    return (cols >= PAD) & ((cols // CH) <= (rows // CH))


def attn_fwd(q, k, kv):
    nh, lp, dq = q.shape
    ta = _attn_tile(lp)
    nb = lp // ta
    scale = MLA_QK ** -0.5

    hb = ATT_HB

    def body(q_ref, k_ref, v_ref, o_ref, lse_ref, m_s, l_s, acc_s):
        qi, ki = pl.program_id(1), pl.program_id(2)

        @pl.when(ki == 0)
        def _():
            m_s[...] = jnp.full_like(m_s, NEG)
            l_s[...] = jnp.zeros_like(l_s)
            acc_s[...] = jnp.zeros_like(acc_s)

        def step(masked):
            ss = [_bdot(q_ref[j], k_ref[j], 1, 1) * scale for j in range(hb)]
            ps = []
            for j in range(hb):
                s = jnp.where(_attn_mask(qi, ki, ta), ss[j], NEG) if masked else ss[j]
                m_new = jnp.maximum(m_s[j], jnp.max(s, axis=-1, keepdims=True))
                p = jnp.exp(s - m_new)
                alpha = jnp.exp(m_s[j] - m_new)
                l_s[j] = alpha * l_s[j] + jnp.sum(p, axis=-1, keepdims=True)
                m_s[j] = m_new
                ps.append((p, alpha))
            for j in range(hb):
                acc_s[j] = ps[j][1] * acc_s[j] + _bdot(ps[j][0], v_ref[j], 1, 0)

        pl.when((ki == qi) | (ki == 0))(functools.partial(step, True))
        pl.when((ki < qi) & (ki > 0))(functools.partial(step, False))

        @pl.when(ki == nb - 1)
        def _():
            for j in range(hb):
                o_ref[:, MLA_V * j:MLA_V * (j + 1)] = (acc_s[j] / l_s[j]).astype(o_ref.dtype)
                lse_ref[j] = m_s[j] + jnp.log(l_s[j])

    return pl.pallas_call(
        body, name="mla_attn", grid=(nh // hb, nb, nb),
        in_specs=[pl.BlockSpec((hb, ta, dq), lambda h, qi, ki: (h, qi, 0)),
                  pl.BlockSpec((hb, ta, dq), lambda h, qi, ki: (h, jnp.minimum(ki, qi), 0)),
                  pl.BlockSpec((hb, ta, MLA_V), lambda h, qi, ki: (h, jnp.minimum(ki, qi), 1))],
        out_specs=[pl.BlockSpec((ta, hb * MLA_V), lambda h, qi, ki: (qi, h)),
                   pl.BlockSpec((hb, ta, 1), lambda h, qi, ki: (h, qi, 0))],
        out_shape=[SDS((lp, nh * MLA_V), BF), SDS((nh, lp, 1), F32)],
        scratch_shapes=[pltpu.VMEM((hb, ta, 1), F32), pltpu.VMEM((hb, ta, 1), F32), pltpu.VMEM((hb, ta, MLA_V), F32)],
        compiler_params=_cp(3))(q, k, kv)


def attn_bwd(q, k, kv, o, do, lse):
    nh, lp, dq = q.shape
    ta = _attn_tile(lp)
    nb = lp // ta
    scale = MLA_QK ** -0.5

    def body(q_ref, k_ref, v_ref, o_ref, do_ref, lse_ref, dq_ref, dk_ref, dv_ref, dk_s, dv_s):
        ki, qi = pl.program_id(1), pl.program_id(2)

        @pl.when((ki == 0) & (qi == 0))
        def _():
            dq_ref[...] = jnp.zeros_like(dq_ref)

        @pl.when(qi == 0)
        def _():
            dk_s[...] = jnp.zeros_like(dk_s)
            dv_s[...] = jnp.zeros_like(dv_s)

        def step(masked):
            dov = do_ref[...]
            s = _bdot(q_ref[...], k_ref[...], 1, 1) * scale
            dp = _bdot(dov, v_ref[...], 1, 1)
            if masked:
                s = jnp.where(_attn_mask(qi, ki, ta), s, NEG)
            p = jnp.exp(s - lse_ref[...])
            delta = jnp.sum(dov * o_ref[...].astype(F32), axis=-1, keepdims=True)
            dv_s[...] += _bdot(p, dov, 0, 0)
            ds = p * (dp - delta) * scale
            rows = pl.ds(pl.multiple_of(qi * ta, ta), ta)
            dq_ref[rows, :] += _bdot(ds, k_ref[...], 1, 0)
            dk_s[...] += _bdot(ds, q_ref[...], 0, 0)

        pl.when((ki == qi) | (ki == 0))(functools.partial(step, True))
        pl.when((ki < qi) & (ki > 0))(functools.partial(step, False))

        @pl.when(qi == nb - 1)
        def _():
            dk_ref[...] = dk_s[...]
            dv_ref[...] = dv_s[...]

    qmap = lambda h, ki, qi: (h, jnp.maximum(qi, ki), 0)
    return pl.pallas_call(
        body, name="mla_attn_b", grid=(nh, nb, nb),
        in_specs=[pl.BlockSpec((None, ta, dq), qmap),
                  pl.BlockSpec((None, ta, dq), lambda h, ki, qi: (h, ki, 0)),
                  pl.BlockSpec((None, ta, MLA_V), lambda h, ki, qi: (h, ki, 1)),
                  pl.BlockSpec((ta, MLA_V), lambda h, ki, qi: (jnp.maximum(qi, ki), h)),
                  pl.BlockSpec((ta, MLA_V), lambda h, ki, qi: (jnp.maximum(qi, ki), h)),
                  pl.BlockSpec((None, ta, 1), qmap)],
        out_specs=[pl.BlockSpec((None, lp, dq), lambda h, ki, qi: (h, 0, 0)),
                   pl.BlockSpec((None, ta, dq), lambda h, ki, qi: (h, ki, 0)),
                   pl.BlockSpec((None, ta, MLA_V), lambda h, ki, qi: (h, ki, 0))],
        out_shape=[SDS((nh, lp, dq), F32), SDS((nh, lp, dq), F32), SDS((nh, lp, MLA_V), F32)],
        scratch_shapes=[pltpu.VMEM((ta, dq), F32), pltpu.VMEM((ta, MLA_V), F32)],
        compiler_params=_cp(3))(q, k, kv, o, do, lse)


def mla_fwd(h, mask, w):
    lp = h.shape[0]
    a = norm_fwd("mla_norm", h, w["ng"], mask)
    down = lin("mla_down", a, w["wdown"])
    grid, ps, xs, tr = _prep1_args(down, w["gcq"], w["gckv"])
    row = lambda o, t: (t, 0)
    (cq, ckv, kpe), _ = seq_fwd("mla_prep1", _mla_prep1_fn, grid, ps, [], xs,
                                [((lp, MLA_QL), BF, (tr, MLA_QL), row), ((lp, MLA_KVL), BF, (tr, MLA_KVL), row),
                                 ((lp, MLA_ROPE), F32, (tr, MLA_ROPE), row)])
    qraw = lin_bo("mla_uq", cq, w["wuq"])
    kvraw = lin_bo("mla_ukv", ckv, w["wukv"])
    grid, ps, cs, xs, tr = _prep2_args(qraw, kvraw, kpe, w["gq"], w["gk"], w["mc"])
    hm = lambda o, t: (t, o, 0)
    (q, k), _ = seq_fwd("mla_prep2", _mla_prep2_fn, grid, ps, cs, xs,
                        [((MLA_H, lp, MLA_QK), BF, (None, tr, MLA_QK), hm), ((MLA_H, lp, MLA_QK), BF, (None, tr, MLA_QK), hm)])
    o, lse = attn_fwd(q, k, kvraw)
    h2 = lin("mla_out", o, w["wo"], res=h)
    return h2, (h, a, down, cq, ckv, kpe, qraw, kvraw, q, k, o, lse)


def mla_bwd(dh2, mask, w, saved, emit):
    h, a, down, cq, ckv, kpe, qraw, kvraw, q, k, o, lse = saved
    lp = h.shape[0]
    g = {}
    g["wo"] = wgrad("mla_dwo", o, dh2)
    do = lin_t("mla_do", dh2, w["wo"])
    do = emit("wo", [g["wo"]], do)
    dq, dk, dv = attn_bwd(q, k, kvraw, o, do, lse)
    grid, ps, cs, xs, tr = _prep2_args(qraw, kvraw, kpe, w["gq"], w["gk"], w["mc"])
    hm = lambda o, t: (t, o, 0)
    (dqraw, dkvraw, dkpe), (g["gq"], g["gk"]) = seq_bwd(
        "mla_prep2_b", _mla_prep2_b_fn, grid, ps, cs, xs,
        [Arg(dq, (None, tr, MLA_QK), hm), Arg(dk, (None, tr, MLA_QK), hm), Arg(dv, (None, tr, MLA_V), hm)])
    g["wuq"] = wgrad_bo("mla_dwuq", cq, dqraw)
    dcq = lin_t_bi("mla_dcq", dqraw, w["wuq"])
    g["wukv"] = wgrad_bo("mla_dwukv", ckv, dkvraw)
    dckv = lin_t_bi("mla_dckv", dkvraw, w["wukv"])
    dckv = emit("wu", [g["wuq"], g["wukv"]], dckv)
    grid, ps, xs, tr = _prep1_args(down, w["gcq"], w["gckv"])
    row = lambda o, t: (t, 0)
    (ddown,), (g["gcq"], g["gckv"]) = seq_bwd(
        "mla_prep1_b", _mla_prep1_fn, grid, ps, [], xs,
        [Arg(dcq, (tr, MLA_QL), row), Arg(dckv, (tr, MLA_KVL), row), Arg(dkpe, (tr, MLA_ROPE), row)])
    g["wdown"] = wgrad("mla_dwdown", a, ddown)
    da = lin_t("mla_da", ddown, w["wdown"])
    dh, g["ng"] = norm_bwd("mla_norm_b", h, w["ng"], mask, da, dh2)
    return dh, g


def loss_head(h, target):
    lp, d = h.shape
    assert OFF == CH

    def body(h_ref, t_ref, loss_ref, dh_ref):
        i = pl.program_id(0)

        @pl.when(i == 0)
        def _():
            loss_ref[...] = jnp.zeros_like(loss_ref)

        e = jnp.where(i > 0, h_ref[...] - t_ref[...], 0.0)
        loss_ref[...] += jnp.sum(e * e) * (0.5 / d)
        dh_ref[...] = e * (1.0 / d)

    return pl.pallas_call(
        body, name="loss_head", grid=(lp // CH,),
        in_specs=[pl.BlockSpec((CH, d), lambda i: (i, 0)), pl.BlockSpec((CH, d), lambda i: (jnp.maximum(i - 1, 0), 0))],
        out_specs=[pl.BlockSpec((8, 128), lambda i: (0, 0)), pl.BlockSpec((CH, d), lambda i: (i, 0))],
        out_shape=[SDS((8, 128), F32), SDS((lp, d), F32)], compiler_params=_cp(1))(h, target)


ADAM_LAND_BYTES = 20 * 1024 * 1024


def _adam_tile(r, c, nl):
    if r % 8:
        return r
    best = 8
    for t in range(8, r + 1, 8):
        if r % t == 0 and N_DEV * t * c * 4 * 2 * nl <= ADAM_LAND_BYTES:
            best = t
    return best


def adamw(name, lands, w, m, v):
    nl, r, c = w.shape
    tr = _adam_tile(r, c, nl)
    c1 = 1.0 / (1.0 - ADAM_B1 ** ADAM_STEP)
    c2 = 1.0 / (1.0 - ADAM_B2 ** ADAM_STEP)

    def body(*refs):
        l_refs = refs[:nl]
        w_ref, m_ref, v_ref, g_ref, d_ref, nm_ref, nv_ref = refs[nl:]
        layer = pl.program_id(0)
        for j in range(nl):
            @pl.when(layer == j)
            def _(j=j):
                g = l_refs[j][0]
                for i in range(1, N_DEV):
                    g = g + l_refs[j][i]
                g_ref[...] = g

        g = g_ref[...]
        nm = ADAM_B1 * m_ref[...] + (1.0 - ADAM_B1) * g
        nv = ADAM_B2 * v_ref[...] + (1.0 - ADAM_B2) * (g * g)
        nm_ref[...] = nm
        nv_ref[...] = nv
        d_ref[...] = -ADAM_LR * ((nm * c1) / (jnp.sqrt(nv * c2) + ADAM_EPS) + ADAM_WD * w_ref[...])

    blk = pl.BlockSpec((None, tr, c), lambda l, i: (l, i, 0))
    land_specs = [pl.BlockSpec((N_DEV, tr, c), lambda l, i, j=j: (0, jnp.where(l == j, i, 0), 0)) for j in range(nl)]
    return pl.pallas_call(
        body, name=name, grid=(nl, r // tr), in_specs=land_specs + [blk, blk, blk],
        out_specs=[blk, blk, blk, blk], out_shape=[SDS((nl, r, c), F32)] * 4, compiler_params=_cp(2))(*lands, w, m, v)


ANY = pl.BlockSpec(memory_space=pl.ANY)
MESH = pl.DeviceIdType.MESH


def _me():
    return lax.axis_index("x"), lax.axis_index("y"), lax.axis_index("c")


def _peers():
    x, y, c = _me()
    out = []
    for k in range(1, N_DEV):
        px = 1 - x if k & 4 else x
        py = 1 - y if k & 2 else y
        pc = 1 - c if k & 1 else c
        out.append(((px, py, pc), 4 * px + 2 * py + pc))
    return out


HBM_SPEC = pl.BlockSpec(memory_space=pltpu.HBM)
SEM_SPEC = pl.BlockSpec(memory_space=pltpu.SEMAPHORE)
DATAFLOW = pltpu.SideEffectType.DATAFLOW_SIDE_EFFECTING


def _my_index():
    return 4 * lax.axis_index("x") + 2 * lax.axis_index("y") + lax.axis_index("c")


def _hbm(a):
    return pltpu.with_memory_space_constraint(a, pltpu.HBM)


NP = N_DEV - 1


def _push_copy(x_ref, land_ref, send, recv, pid, src_idx, dst_idx, scatter):
    src = x_ref.at[src_idx] if scatter else x_ref
    return pltpu.make_async_remote_copy(src_ref=src, dst_ref=land_ref.at[dst_idx], send_sem=send, recv_sem=recv,
                                        device_id=pid, device_id_type=MESH)


def push_start(name, xs, me, scatter, carry=None):
    n = len(xs)
    lands = []
    for a in xs:
        own = lax.dynamic_index_in_dim(a, me, 0, keepdims=True) if scatter else a[None]
        z = lax.empty((N_DEV,) + own.shape[1:], a.dtype)
        lands.append(lax.dynamic_update_slice(z, own, (me,) + (0,) * (own.ndim - 1)))
    ns = 2 * NP * n
    ops = xs + lands + ([carry] if carry is not None else [])
    na = len(ops)

    def body(*refs):
        x_refs, land_refs = refs[:n], refs[n:2 * n]
        sems = refs[na:na + ns]
        token = refs[-1]
        x, y, c = _me()
        mine = 4 * x + 2 * y + c
        for i in range(n):
            for k, (pid, pidx) in enumerate(_peers()):
                s = 2 * (NP * i + k)
                _push_copy(x_refs[i], land_refs[i], sems[s], sems[s + 1], pid, pidx, mine, scatter).start()
        token[...] = jnp.zeros_like(token)

    out_shape = ([pltpu.SemaphoreType.DMA(())] * ns + [pltpu.HBM(a.shape, a.dtype) for a in ops]
                 + [SDS((8, 128), F32)])
    res = pl.pallas_call(
        body, name=name, out_shape=out_shape, in_specs=[HBM_SPEC] * na,
        out_specs=[SEM_SPEC] * ns + [HBM_SPEC] * na + [pl.BlockSpec(memory_space=pltpu.VMEM)],
        input_output_aliases={i: ns + i for i in range(na)},
        compiler_params=pltpu.CompilerParams(has_side_effects=DATAFLOW))(*[_hbm(a) for a in ops])
    sems, thru, token = res[:ns], res[ns:-1], res[-1]
    handles = [dict(x=thru[i], land=thru[n + i], sems=list(sems[2 * NP * i:2 * NP * (i + 1)]), token=token)
               for i in range(n)]
    return (handles, thru[2 * n]) if carry is not None else handles


def push_wait(name, hds, after, scatter):
    n = len(hds)
    ns = 2 * NP

    def body(*refs):
        x_refs, land_refs = refs[:n], refs[n:2 * n]
        sems = refs[2 * n:2 * n + ns * n]
        for i in range(n):
            for k, (pid, pidx) in enumerate(_peers()):
                cp = _push_copy(x_refs[i], land_refs[i], sems[ns * i + 2 * k], sems[ns * i + 2 * k + 1], pid, pidx, pidx,
                                scatter)
                cp.wait_send()
                cp.wait_recv()

    arrs = [hd["x"] for hd in hds] + [hd["land"] for hd in hds]
    sems = [s for hd in hds for s in hd["sems"]]
    res = pl.pallas_call(
        body, name=name, out_shape=[pltpu.HBM(a.shape, a.dtype) for a in arrs],
        in_specs=[HBM_SPEC] * (2 * n) + [SEM_SPEC] * (ns * n) + [ANY], out_specs=[HBM_SPEC] * (2 * n),
        input_output_aliases={i: i for i in range(2 * n)},
        compiler_params=pltpu.CompilerParams(has_side_effects=DATAFLOW))(*arrs, *sems, after)
    return list(res[n:])


WEIGHTS = ['meta_tokens', 'norm_mix_g', 'norm_ffn_g', 'mla_w_down', 'mla_cq_norm_g', 'mla_ckv_norm_g', 'mla_w_uq',
           'mla_w_ukv', 'mla_q_head_g', 'mla_k_head_g', 'mla_w_o', 'hgrn_w_in', 'hgrn_lb_logits', 'hgrn_o_norm_g',
           'hgrn_w_o', 's5_lam_re', 's5_lam_im', 's5_log_dt', 's5_b_re', 's5_b_im', 's5_c_re', 's5_c_im', 's5_d',
           's5_w_glu', 'ret_w_in', 'ret_gn_g', 'ret_w_o', 'ffn_w_up', 'ffn_conv_w', 'ffn_conv_b', 'ffn_w_down']
BIG = ['mla_w_down', 'mla_w_uq', 'mla_w_ukv', 'mla_w_o', 'hgrn_w_in', 'hgrn_w_o', 's5_w_glu', 'ret_w_in', 'ret_w_o',
       'ffn_w_up', 'ffn_w_down']
SMALL_SH = ['meta_tokens', 's5_d', 'ret_gn_g', 'ffn_conv_w']
REP_S5 = ['s5_lam_re', 's5_lam_im', 's5_log_dt', 's5_b_re', 's5_b_im', 's5_c_re', 's5_c_im']
REP_REST = ['norm_mix_g', 'norm_ffn_g', 'mla_cq_norm_g', 'mla_ckv_norm_g', 'mla_q_head_g', 'mla_k_head_g',
            'hgrn_lb_logits', 'hgrn_o_norm_g', 'ffn_conv_b']
SMALL_REP = REP_REST + REP_S5
LANE = 128


def _flat(arrs, mult):
    v = jnp.concatenate([a.reshape(-1) for a in arrs])
    pad = (-v.shape[0]) % mult
    return jnp.pad(v, (0, pad)).reshape(-1, LANE)


def _unflat(flat2d, like):
    v = flat2d.reshape(-1)
    out, o = [], 0
    for a in like:
        out.append(v[o:o + a.size].reshape(a.shape))
        o += a.size
    return out


def _lb_of(logits):
    cum = jnp.cumsum(jax.nn.softmax(logits, axis=0), axis=0)
    return (cum - cum[0:1])[1:2]


def _cols_to_blocks(g):
    k, n = g.shape
    return g.reshape(k, N_DEV, n // N_DEV).transpose(1, 0, 2)


def _blocks_to_cols(wb):
    nb, k, n = wb.shape
    return wb.transpose(1, 0, 2).reshape(k, nb * n)


SUBS = ['mla', 'ffn0', 'hgrn', 'ffn1', 's5', 'ffn2', 'ret', 'ffn3']
GROUPS = [[('mla_w_down', 0), ('mla_w_uq', 0), ('mla_w_ukv', 0), ('mla_w_o', 0)],
          [('ffn_w_up', 0), ('ffn_w_down', 0)],
          [('hgrn_w_in', 0), ('hgrn_w_o', 0)],
          [('ffn_w_up', 1), ('ffn_w_down', 1)],
          [('s5_w_glu', 0)],
          [('ffn_w_up', 2), ('ffn_w_down', 2)],
          [('ret_w_in', 0), ('ret_w_o', 0)],
          [('ffn_w_up', 3), ('ffn_w_down', 3)]]


def _pack8(parts, mult):
    v = jnp.concatenate(parts, axis=1)
    return jnp.pad(v, ((0, 0), (0, (-v.shape[1]) % mult))).reshape(N_DEV, -1, LANE)


def _sub_weights(k, got, rep, tabs, lp):
    ngm, ngf = rep['norm_mix_g'], rep['norm_ffn_g']
    if k == 0:
        return dict(ng=ngm[0:1], wdown=got[0].reshape(D, -1), gcq=rep['mla_cq_norm_g'], gckv=rep['mla_ckv_norm_g'],
                    wuq=got[1], wukv=got[2], gq=rep['mla_q_head_g'], gk=rep['mla_k_head_g'], wo=got[3].reshape(D, D),
                    mc=mla_consts(lp))
    if k == 2:
        return dict(ng=ngm[1:2], win=got[0], lb=tabs['lb'], go=rep['hgrn_o_norm_g'], wo=got[1].reshape(D, D))
    if k == 4:
        return dict(ng=ngm[2:3], tb=tabs['tb'], dsk=tabs['s5_d'], wglu=_blocks_to_cols(got[0]))
    if k == 6:
        return dict(ng=ngm[3:4], win=_blocks_to_cols(got[0]), gn=tabs['ret_gn_g'], wo=got[1].reshape(2 * D, D),
                    rc=ret_consts(lp))
    i = k // 2
    return dict(ng=ngf[i:i + 1], up=got[0], cw=tabs['conv_w'][:, i].reshape(2, 4, 3, 1, FFN_B),
                cb=rep['ffn_conv_b'][i].reshape(2, 4, 1, FFN_B), down=got[1].reshape(4, FFN_B, D))


def _sub_grad_blocks(k, g):
    if k == 0:
        parts = [g['wdown'], g['wuq'], g['wukv'], g['wo']]
    elif k == 2:
        parts = [g['win'], g['wo']]
    elif k == 4:
        parts = [_cols_to_blocks(g['wglu'])]
    elif k == 6:
        parts = [_cols_to_blocks(g['win']), g['wo']]
    else:
        parts = [g['up'], g['down']]
    return parts


_FWD = [mla_fwd, None, hgrn_fwd, None, s5_fwd, None, ret_fwd, None]
_BWD = [mla_bwd, None, hgrn_bwd, None, s5_bwd, None, ret_bwd, None]


def _step(args):
    w = {n: args[n] for n in WEIGHTS}
    x2, tgt = args['x'][0], args['loss_target'][0]

    lp = x2.shape[0] + OFF
    me = _my_index()
    mask = _rowmask(lp)
    rep = {n: w[n] for n in SMALL_REP}

    xs, slots = [], []
    for gi, grp in enumerate(GROUPS):
        items = [w[n][l].astype(BF) for n, l in grp] + ([_flat([w[n] for n in SMALL_SH], LANE)] if gi == 0 else [])
        slots.append((len(xs), len(items)))
        xs += items
    gh = push_start("gather_start", xs, me, scatter=False)

    def fetch(gi, after):
        s, cnt = slots[gi]
        return push_wait("gather_wait_" + SUBS[gi], gh[s:s + cnt], after, scatter=False)

    got = fetch(0, x2)
    sm, o, smp = got[-1].reshape(N_DEV, -1), 0, {}
    for n in SMALL_SH:
        smp[n] = sm[:, o:o + w[n].size].reshape((N_DEV,) + w[n].shape)
        o += w[n].size
    meta = smp['meta_tokens'].transpose(1, 0, 2).reshape(N_META, D)
    lb, lb_vjp = jax.vjp(_lb_of, rep['hgrn_lb_logits'])
    s5p = [rep[n][0] for n in ('s5_lam_re', 's5_lam_im', 's5_log_dt', 's5_b_re', 's5_b_im', 's5_c_re', 's5_c_im')]
    tb, tb_vjp = jax.vjp(s5_tables, *s5p)
    tabs = dict(lb=lb, tb=tb, s5_d=smp['s5_d'].reshape(1, D), ret_gn_g=smp['ret_gn_g'].reshape(1, 2 * D),
                conv_w=smp['ffn_conv_w'])
    h = jnp.concatenate([jnp.zeros((PAD, D), F32), meta, x2], axis=0)
    ws, saved = [], []
    for k in range(8):
        if k > 0:
            got = fetch(k, h)
        ws.append(_sub_weights(k, got, rep, tabs, lp))
        if k % 2:
            h, sv = ffn_fwd(k // 2, h, mask, ws[k])
        else:
            h, sv = _FWD[k](h, mask, ws[k])
        saved.append(sv)
    loss, dh = loss_head(h, tgt)

    gs, sh = [None] * 8, [None] * 8
    early = {}

    def emit(tag, grads, carry):
        blocks = [t.reshape((N_DEV, -1) + t.shape[-1:]) if t.ndim == 2 else t for t in grads]
        early[tag], carry = push_start("scatter_start_mla_" + tag, blocks, me, scatter=True, carry=carry)
        return carry

    for k in reversed(range(1, 8)):
        if k % 2:
            dh, gs[k] = ffn_bwd(k // 2, dh, mask, ws[k], saved[k])
        else:
            dh, gs[k] = _BWD[k](dh, mask, ws[k], saved[k])
        blocks = [b.reshape((N_DEV,) + w[n].shape[1:]) for b, (n, _) in zip(_sub_grad_blocks(k, gs[k]), GROUPS[k])]
        sh[k], dh = push_start("scatter_start_" + SUBS[k], blocks, me, scatter=True, carry=dh)
        if k == 4:
            gs5 = _flat(list(tb_vjp(gs[4]['tb'])), 8 * LANE)
            rh_s5, dh = push_start("small_grads_start_s5", [gs5], me, scatter=False, carry=dh)
    dh, gs[0] = mla_bwd(dh, mask, ws[0], saved[0], emit)
    dmeta = dh[PAD:OFF].reshape(N_META, N_DEV, D // N_DEV).transpose(1, 0, 2)
    dcw = jnp.stack([gs[2 * i + 1]['cw'].reshape(N_DEV, 3, FFN_B) for i in range(4)], axis=1)
    last = push_start("scatter_start_mla", [gs[0]['wdown'].reshape(N_DEV, D // N_DEV, -1),
                                            _pack8([t.reshape(N_DEV, -1) for t in (dmeta, gs[4]['dsk'], gs[6]['gn'], dcw)], LANE)],
                      me, scatter=True)
    sh[0] = [last[0], early['wu'][0], early['wu'][1], early['wo'][0], last[1]]
    grad_x = dh[OFF:]

    g_rep = {
        'norm_mix_g': jnp.concatenate([gs[k]['ng'] for k in (0, 2, 4, 6)], axis=0),
        'norm_ffn_g': jnp.concatenate([gs[k]['ng'] for k in (1, 3, 5, 7)], axis=0),
        'mla_cq_norm_g': gs[0]['gcq'], 'mla_ckv_norm_g': gs[0]['gckv'], 'mla_q_head_g': gs[0]['gq'],
        'mla_k_head_g': gs[0]['gk'], 'hgrn_lb_logits': lb_vjp(gs[2]['lb'])[0], 'hgrn_o_norm_g': gs[2]['go'],
        'ffn_conv_b': jnp.stack([gs[k]['cb'].reshape(-1) for k in (1, 3, 5, 7)], axis=0),
    }
    loss_part = loss[0, 0:1]
    grep = _flat([g_rep[n] for n in REP_REST] + [loss_part], 8 * LANE)
    rh = push_start("small_grads_start", [grep], me, scatter=False)

    lands = {n: [None] * w[n].shape[0] for n in BIG}
    res = {}
    late = [n for n, _ in GROUPS[0]]
    for k in reversed(range(1, 8)):
        got = push_wait("scatter_wait_" + SUBS[k], sh[k], grep, scatter=True)
        for (n, l), t in zip(GROUPS[k], got):
            lands[n][l] = t
    for n in BIG:
        if n not in late:
            res[n] = adamw("adam_" + n, lands[n], w[n], args['m_' + n], args['v_' + n])
    after = res['ffn_w_up'][1]
    got = push_wait("scatter_wait_" + SUBS[0], sh[0], after, scatter=True)
    small_land = got[-1]
    (rep_land,) = push_wait("small_grads_wait", rh, after, scatter=False)
    (s5_land,) = push_wait("small_grads_wait_s5", rh_s5, after, scatter=False)
    for (n, _), t in zip(GROUPS[0], got):
        res[n] = adamw("adam_" + n, [t], w[n], args['m_' + n], args['v_' + n])

    def flat_adam(name, land, names, mult, extra=()):
        like = [w[n] for n in names]
        pad = [jnp.zeros_like(e) for e in extra]
        out = adamw(name, [land], _flat(like + pad, mult)[None], _flat([args['m_' + n] for n in names] + pad, mult)[None],
                    _flat([args['v_' + n] for n in names] + pad, mult)[None])
        for n, parts in zip(names, zip(*[_unflat(t, like) for t in out])):
            res[n] = list(parts)
        return out[0]

    flat_adam("adam_small_sharded", small_land, SMALL_SH, LANE)
    flat_adam("adam_s5_replicated", s5_land, REP_S5, 8 * LANE)
    gsum = flat_adam("adam_small_replicated", rep_land, REP_REST, 8 * LANE, extra=[loss_part])
    total = gsum.reshape(-1)[sum(w[n].size for n in REP_REST)]
    outs = [total, grad_x[None]]
    for k in range(4):
        outs += [res[n][k] for n in WEIGHTS]
    return tuple(outs)


def kernel(x, meta_tokens, norm_mix_g, norm_ffn_g, mla_w_down, mla_cq_norm_g, mla_ckv_norm_g, mla_w_uq, mla_w_ukv, mla_q_head_g, mla_k_head_g, mla_w_o, hgrn_w_in, hgrn_lb_logits, hgrn_o_norm_g, hgrn_w_o, s5_lam_re, s5_lam_im, s5_log_dt, s5_b_re, s5_b_im, s5_c_re, s5_c_im, s5_d, s5_w_glu, ret_w_in, ret_gn_g, ret_w_o, ffn_w_up, ffn_conv_w, ffn_conv_b, ffn_w_down, loss_target, m_meta_tokens, m_norm_mix_g, m_norm_ffn_g, m_mla_w_down, m_mla_cq_norm_g, m_mla_ckv_norm_g, m_mla_w_uq, m_mla_w_ukv, m_mla_q_head_g, m_mla_k_head_g, m_mla_w_o, m_hgrn_w_in, m_hgrn_lb_logits, m_hgrn_o_norm_g, m_hgrn_w_o, m_s5_lam_re, m_s5_lam_im, m_s5_log_dt, m_s5_b_re, m_s5_b_im, m_s5_c_re, m_s5_c_im, m_s5_d, m_s5_w_glu, m_ret_w_in, m_ret_gn_g, m_ret_w_o, m_ffn_w_up, m_ffn_conv_w, m_ffn_conv_b, m_ffn_w_down, v_meta_tokens, v_norm_mix_g, v_norm_ffn_g, v_mla_w_down, v_mla_cq_norm_g, v_mla_ckv_norm_g, v_mla_w_uq, v_mla_w_ukv, v_mla_q_head_g, v_mla_k_head_g, v_mla_w_o, v_hgrn_w_in, v_hgrn_lb_logits, v_hgrn_o_norm_g, v_hgrn_w_o, v_s5_lam_re, v_s5_lam_im, v_s5_log_dt, v_s5_b_re, v_s5_b_im, v_s5_c_re, v_s5_c_im, v_s5_d, v_s5_w_glu, v_ret_w_in, v_ret_gn_g, v_ret_w_o, v_ffn_w_up, v_ffn_conv_w, v_ffn_conv_b, v_ffn_w_down):
    return _step(dict(locals()))
```

```python
import functools
import math

import jax
import jax.numpy as jnp
import numpy as np
from jax import lax
from jax.experimental import pallas as pl
from jax.experimental.pallas import tpu as pltpu

F32 = jnp.float32
BF = jnp.bfloat16
SDS = jax.ShapeDtypeStruct

N_DEV = 8
D = 1024
N_META = 16
PAD = 48
OFF = PAD + N_META
CH = 64
EPS = 1e-6
NEG = -1e30
ROPE_BASE = 10000.0

MLA_H, MLA_NOPE, MLA_ROPE, MLA_V = 8, 128, 64, 128
MLA_QK = MLA_NOPE + MLA_ROPE
MLA_QL, MLA_KVL = 384, 256
HG_H, HG_D, HG_C = 8, 128, 16
S5_G, S5_P, S5_K = 64, 64, 16
S5_SG = 8
RET_H, RET_DK, RET_DV = 4, 256, 512
FFN_F = 2816
FFN_B = 704

ADAM_LR, ADAM_B1, ADAM_B2, ADAM_EPS, ADAM_WD, ADAM_STEP = 0.001, 0.9, 0.999, 1e-08, 0.01, 10

VMEM_LIMIT = 56 * 1024 * 1024
ARB = "arbitrary"


def _cp(n):
    return pltpu.CompilerParams(dimension_semantics=(ARB,) * n, vmem_limit_bytes=VMEM_LIMIT)


def _bdot(a, b, ca, cb):
    return lax.dot_general(a.astype(BF), b.astype(BF), (((ca,), (cb,)), ((), ())), preferred_element_type=F32)


@jax.custom_vjp
def mm(a, b):
    return _bdot(a, b, 1, 0)


@jax.custom_vjp
def mm_nt(a, b):
    return _bdot(a, b, 1, 1)


@jax.custom_vjp
def mm_tn(a, b):
    return _bdot(a, b, 0, 0)


mm.defvjp(lambda a, b: (mm(a, b), (a, b)),
          lambda r, g: (mm_nt(g, r[1]).astype(r[0].dtype), mm_tn(r[0], g).astype(r[1].dtype)))
mm_nt.defvjp(lambda a, b: (mm_nt(a, b), (a, b)),
             lambda r, g: (mm(g, r[1]).astype(r[0].dtype), mm_tn(g, r[0]).astype(r[1].dtype)))
mm_tn.defvjp(lambda a, b: (mm_tn(a, b), (a, b)),
             lambda r, g: (mm_nt(r[1], g).astype(r[0].dtype), mm(r[0], g).astype(r[1].dtype)))


def _xdot(a, b, ca, cb):
    return lax.dot_general(a, b, (((ca,), (cb,)), ((), ())), preferred_element_type=F32,
                           precision=lax.Precision.HIGHEST)


@jax.custom_vjp
def cright(x, r):
    return _xdot(x, r, 1, 0)


cright.defvjp(lambda x, r: (cright(x, r), r), lambda r, g: (_xdot(g, r, 1, 1), jnp.zeros_like(r)))


def _shift_raw(x, s):
    n = x.shape[0]
    r = lax.broadcasted_iota(jnp.int32, x.shape, 0)
    y = pltpu.roll(x, s % n, 0)
    return jnp.where((r >= s) & (r < n + s), y, 0.0)


def _seg_shift_raw(x, s, seg, up):
    n = x.shape[0]
    r = lax.broadcasted_iota(jnp.int32, x.shape, 0) % seg
    if up:
        return jnp.where(r < seg - s, pltpu.roll(x, n - s, 0), 0.0)
    return jnp.where(r >= s, pltpu.roll(x, s, 0), 0.0)


@functools.partial(jax.custom_vjp, nondiff_argnums=(1, 2))
def seg_shift(x, s, seg):
    return _seg_shift_raw(x, s, seg, False)


seg_shift.defvjp(lambda x, s, seg: (_seg_shift_raw(x, s, seg, False), None),
                 lambda s, seg, _, g: (_seg_shift_raw(g, s, seg, True),))


def _seg_cumsum(x, seg):
    s = 1
    while s < seg:
        x = x + seg_shift(x, s, seg)
        s *= 2
    return x


def _rms(x, g):
    return x * lax.rsqrt(jnp.mean(x * x, axis=-1, keepdims=True) + EPS) * g


def _silu(x):
    return x * jax.nn.sigmoid(x)


def _mm_call(name, a, b, *, grid, a_spec, b_spec, o_shape, o_spec, dims, acc_shape, res=None, res_spec=None,
             mask_tm=None):
    nk = grid[2]

    def body(*refs):
        if res is None:
            a_ref, b_ref, o_ref = refs[:3]
        else:
            a_ref, b_ref, r_ref, o_ref = refs[:4]
        k = pl.program_id(2)

        def dot():
            return lax.dot_general(a_ref[...].astype(BF), b_ref[...].astype(BF), dims, preferred_element_type=F32)

        def finish(v):
            if res is not None:
                v = v + r_ref[...].astype(F32)
                rows = pl.program_id(0) * mask_tm + lax.broadcasted_iota(jnp.int32, v.shape, 0)
                v = jnp.where(rows >= PAD, v, 0.0)
            o_ref[...] = v.astype(o_ref.dtype)

        if nk == 1:
            finish(dot())
            return
        acc = refs[-1]

        @pl.when(k == 0)
        def _():
            acc[...] = dot()

        @pl.when((k > 0) & (k < nk - 1))
        def _():
            acc[...] += dot()

        @pl.when(k == nk - 1)
        def _():
            finish(acc[...] + dot())

    ins = [a, b] + ([res] if res is not None else [])
    specs = [a_spec, b_spec] + ([res_spec] if res is not None else [])
    scratch = [pltpu.VMEM(acc_shape, F32)] if nk > 1 else []
    return pl.pallas_call(body, name=name, grid=grid, in_specs=specs, out_specs=o_spec, out_shape=o_shape,
                          scratch_shapes=scratch, compiler_params=_cp(3))(*ins)


NN = (((1,), (0,)), ((), ()))
NT = (((1,), (1,)), ((), ()))
TN = (((0,), (0,)), ((), ()))


def _row_tile(lp):
    for t in (832, 640, 320, 64):
        if lp % t == 0:
            return t
    raise ValueError(lp)


def _col_tile(n):
    for t in (1024, 768, 512, 384, 256, 128):
        if n % t == 0:
            return t
    return n


MM_WHOLE_ROWS = 4160


def _mm_rows(m, whole=False):
    if whole and m <= MM_WHOLE_ROWS:
        return m
    return 2080 if m % 2080 == 0 else _row_tile(m)


def _mm_cols(n):
    for t in (512, 384, 256, 128):
        if n % t == 0:
            return t
    return n


def lin(name, a, w, out_dtype=F32, res=None):
    m, k = a.shape
    n = w.shape[1]
    tm, tn, tc = _mm_rows(m, res is None and a.dtype == BF and k <= 1024), _mm_cols(n), _col_tile(k)
    return _mm_call(name, a, w, grid=(m // tm, n // tn, k // tc),
                    a_spec=pl.BlockSpec((tm, tc), lambda i, j, kk: (i, kk)),
                    b_spec=pl.BlockSpec((tc, tn), lambda i, j, kk: (kk, j)),
                    o_shape=SDS((m, n), out_dtype), o_spec=pl.BlockSpec((tm, tn), lambda i, j, kk: (i, j)),
                    dims=NN, acc_shape=(tm, tn), res=res,
                    res_spec=pl.BlockSpec((tm, tn), lambda i, j, kk: (i, j)), mask_tm=tm)


def lin_bo(name, a, wb, out_dtype=F32):
    m, k = a.shape
    nb, _, n = wb.shape
    tm = _mm_rows(m, a.dtype == BF)
    return _mm_call(name, a, wb, grid=(m // tm, nb, 1),
                    a_spec=pl.BlockSpec((tm, k), lambda i, j, kk: (i, 0)),
                    b_spec=pl.BlockSpec((None, k, n), lambda i, j, kk: (j, 0, 0)),
                    o_shape=SDS((nb, m, n), out_dtype), o_spec=pl.BlockSpec((None, tm, n), lambda i, j, kk: (j, i, 0)),
                    dims=NN, acc_shape=(tm, n))


def lin_bi(name, ab, wb, out_dtype=F32, res=None):
    nb, m, k = ab.shape
    n = wb.shape[2]
    tm, tn = _mm_rows(m), _mm_cols(n)
    return _mm_call(name, ab, wb, grid=(m // tm, n // tn, nb),
                    a_spec=pl.BlockSpec((None, tm, k), lambda i, j, kk: (kk, i, 0)),
                    b_spec=pl.BlockSpec((None, k, tn), lambda i, j, kk: (kk, 0, j)),
                    o_shape=SDS((m, n), out_dtype), o_spec=pl.BlockSpec((tm, tn), lambda i, j, kk: (i, j)),
                    dims=NN, acc_shape=(tm, tn), res=res,
                    res_spec=pl.BlockSpec((tm, tn), lambda i, j, kk: (i, j)), mask_tm=tm)


def lin_t(name, g, w, out_dtype=F32):
    m, n = g.shape
    k = w.shape[0]
    tm, tk, tc = _mm_rows(m), _col_tile(k), _col_tile(n)
    return _mm_call(name, g, w, grid=(m // tm, k // tk, n // tc),
                    a_spec=pl.BlockSpec((tm, tc), lambda i, j, kk: (i, kk)),
                    b_spec=pl.BlockSpec((tk, tc), lambda i, j, kk: (j, kk)),
                    o_shape=SDS((m, k), out_dtype), o_spec=pl.BlockSpec((tm, tk), lambda i, j, kk: (i, j)),
                    dims=NT, acc_shape=(tm, tk))


def lin_t_bi(name, gb, wb, out_dtype=F32):
    nb, m, n = gb.shape
    k = wb.shape[1]
    tm, tk = _mm_rows(m), _col_tile(k)
    return _mm_call(name, gb, wb, grid=(m // tm, k // tk, nb),
                    a_spec=pl.BlockSpec((None, tm, n), lambda i, j, kk: (kk, i, 0)),
                    b_spec=pl.BlockSpec((None, tk, n), lambda i, j, kk: (kk, j, 0)),
                    o_shape=SDS((m, k), out_dtype), o_spec=pl.BlockSpec((tm, tk), lambda i, j, kk: (i, j)),
                    dims=NT, acc_shape=(tm, tk))


def lin_t_bo(name, g, wb, out_dtype=F32):
    m, n = g.shape
    nb, k, _ = wb.shape
    tm = _mm_rows(m)
    return _mm_call(name, g, wb, grid=(m // tm, nb, 1),
                    a_spec=pl.BlockSpec((tm, n), lambda i, j, kk: (i, 0)),
                    b_spec=pl.BlockSpec((None, k, n), lambda i, j, kk: (j, 0, 0)),
                    o_shape=SDS((nb, m, k), out_dtype), o_spec=pl.BlockSpec((None, tm, k), lambda i, j, kk: (j, i, 0)),
                    dims=NT, acc_shape=(tm, k))


def wgrad(name, a, g):
    m, k = a.shape
    n = g.shape[1]
    tm, tn = _mm_rows(m), (_col_tile(n) if k <= 1024 else _mm_cols(n))
    return _mm_call(name, a, g, grid=(1, n // tn, m // tm),
                    a_spec=pl.BlockSpec((tm, k), lambda i, j, kk: (kk, 0)),
                    b_spec=pl.BlockSpec((tm, tn), lambda i, j, kk: (kk, j)),
                    o_shape=SDS((k, n), F32), o_spec=pl.BlockSpec((k, tn), lambda i, j, kk: (0, j)),
                    dims=TN, acc_shape=(k, tn))


def wgrad_bo(name, a, gb):
    m, k = a.shape
    nb, _, n = gb.shape
    tm = _mm_rows(m)
    return _mm_call(name, a, gb, grid=(nb, 1, m // tm),
                    a_spec=pl.BlockSpec((tm, k), lambda i, j, kk: (kk, 0)),
                    b_spec=pl.BlockSpec((None, tm, n), lambda i, j, kk: (i, kk, 0)),
                    o_shape=SDS((nb, k, n), F32), o_spec=pl.BlockSpec((None, k, n), lambda i, j, kk: (i, 0, 0)),
                    dims=TN, acc_shape=(k, n))


def wgrad_bi(name, zb, g):
    nb, m, k = zb.shape
    n = g.shape[1]
    tm, tn = _mm_rows(m), _col_tile(n)
    return _mm_call(name, zb, g, grid=(nb, n // tn, m // tm),
                    a_spec=pl.BlockSpec((None, tm, k), lambda i, j, kk: (i, kk, 0)),
                    b_spec=pl.BlockSpec((tm, tn), lambda i, j, kk: (kk, j)),
                    o_shape=SDS((nb, k, n), F32), o_spec=pl.BlockSpec((None, k, tn), lambda i, j, kk: (i, 0, j)),
                    dims=TN, acc_shape=(k, tn))


class Arg:
    def __init__(self, arr, block, imap, shared=False, acc=False):
        self.arr, self.block, self.imap = arr, block, imap
        self.shared = shared
        self.acc = acc

    @property
    def spec(self):
        return pl.BlockSpec(self.block, self.imap)

    def vshape(self):
        return tuple(b for b in self.block if b is not None)


def _rev(arg, nt):
    return pl.BlockSpec(arg.block, lambda o, t, _f=arg.imap: _f(o, nt - 1 - t))


def seq_fwd(name, fn, grid, params, consts, xs, outs, carries=(), save_dtype=F32):
    no, nt = grid
    n_p, n_c, n_x, n_y, n_k = len(params), len(consts), len(xs), len(outs), len(carries)

    def body(*refs):
        p_refs = refs[:n_p]
        c_refs = refs[n_p:n_p + n_c]
        x_refs = refs[n_p + n_c:n_p + n_c + n_x]
        r = n_p + n_c + n_x
        y_refs = refs[r:r + n_y]
        s_refs = refs[r + n_y:r + n_y + n_k]
        k_refs = refs[r + n_y + n_k:]
        t = pl.program_id(1)

        if n_k:
            @pl.when(t == 0)
            def _():
                for k in k_refs:
                    k[...] = jnp.zeros_like(k)

        carry = tuple(k[...] for k in k_refs)
        for s, c in zip(s_refs, carry):
            s[...] = c.astype(s.dtype)
        new_carry, ys = fn(tuple(p[...] for p in p_refs), tuple(c[...] for c in c_refs), carry,
                           tuple(x[...] for x in x_refs))
        for k, c in zip(k_refs, new_carry):
            k[...] = c
        for y_ref, y in zip(y_refs, ys):
            y_ref[...] = y.astype(y_ref.dtype)

    out_shape = [SDS(s, d) for (s, d, _, _) in outs]
    out_specs = [pl.BlockSpec(b, im) for (_, _, b, im) in outs]
    for cs in carries:
        out_shape.append(SDS((no, nt) + cs, save_dtype))
        out_specs.append(pl.BlockSpec((None, None) + cs, lambda o, t, _n=len(cs): (o, t) + (0,) * _n))
    res = pl.pallas_call(
        body, name=name, grid=grid, in_specs=[a.spec for a in list(params) + list(consts) + list(xs)],
        out_specs=out_specs, out_shape=out_shape, scratch_shapes=[pltpu.VMEM(cs, F32) for cs in carries],
        compiler_params=_cp(2))(*[a.arr for a in list(params) + list(consts) + list(xs)])
    return res[:n_y], res[n_y:]


def seq_bwd(name, fn, grid, params, consts, xs, dys, saved=(), carries=()):
    no, nt = grid
    n_p, n_c, n_x, n_y, n_k = len(params), len(consts), len(xs), len(dys), len(carries)

    def body(*refs):
        p_refs = refs[:n_p]
        c_refs = refs[n_p:n_p + n_c]
        x_refs = refs[n_p + n_c:n_p + n_c + n_x]
        r = n_p + n_c + n_x
        g_refs = refs[r:r + n_y]
        s_refs = refs[r + n_y:r + n_y + n_k]
        r = r + n_y + n_k
        dx_refs = refs[r:r + n_x]
        dp_refs = refs[r + n_x:r + n_x + n_p]
        k_refs = refs[r + n_x + n_p:]
        o = pl.program_id(0)
        t = pl.program_id(1)

        if n_k:
            @pl.when(t == 0)
            def _():
                for k in k_refs:
                    k[...] = jnp.zeros_like(k)

        for a, dp in zip(params, dp_refs):
            @pl.when((t == 0) & (o == 0) if a.shared else (t == 0))
            def _(dp=dp):
                dp[...] = jnp.zeros_like(dp)

        for a, dx in zip(xs, dx_refs):
            if a.acc:
                @pl.when(t == 0)
                def _(dx=dx):
                    dx[...] = jnp.zeros_like(dx)

        consts_v = tuple(c[...] for c in c_refs)

        def f(pv, cv, xv):
            return fn(pv, consts_v, cv, xv)

        pv = tuple(p[...] for p in p_refs)
        cv = tuple(s[...].astype(F32) for s in s_refs)
        xv = tuple(x[...] for x in x_refs)
        (new_carry, ys), vjp = jax.vjp(f, pv, cv, xv)
        cot = (tuple(k[...] for k in k_refs), tuple(g[...].astype(y.dtype) for g, y in zip(g_refs, ys)))
        dpv, dcv, dxv = vjp(cot)
        for k, c in zip(k_refs, dcv):
            k[...] = c
        for dp, v in zip(dp_refs, dpv):
            dp[...] += v
        for a, dx, v in zip(xs, dx_refs, dxv):
            if a.acc:
                dx[...] += v
            else:
                dx[...] = v.astype(dx.dtype)

    in_specs = ([_rev(a, nt) for a in list(params) + list(consts) + list(xs) + list(dys)]
                + [pl.BlockSpec((None, None) + cs, lambda o, t, _n=len(cs): (o, nt - 1 - t) + (0,) * _n) for cs in carries])
    out_shape = [SDS(a.arr.shape, F32) for a in xs] + [SDS(a.arr.shape, F32) for a in params]
    out_specs = [_rev(a, nt) for a in list(xs) + list(params)]
    res = pl.pallas_call(
        body, name=name, grid=grid, in_specs=in_specs, out_specs=out_specs, out_shape=out_shape,
        scratch_shapes=[pltpu.VMEM(cs, F32) for cs in carries], compiler_params=_cp(2))(
            *[a.arr for a in list(params) + list(consts) + list(xs) + list(dys)], *saved)
    return res[:n_x], res[n_x:]


def _rowmask(lp):
    return (jnp.arange(lp) >= PAD).astype(F32)[:, None]


def _norm_fn(p, c, k, x):
    return (), (_rms(x[0] * c[0], p[0]),)


def _norm_b_fn(p, c, k, x):
    h = x[0] * c[0]
    return (), (_rms(h, p[0]), h)


def norm_fwd(name, h, g, mask, out_dtype=BF):
    lp, d = h.shape
    tr = _row_tile(lp)
    row = lambda o, t: (t, 0)
    (a,), _ = seq_fwd(name, _norm_fn, (1, lp // tr), [Arg(g, (1, d), lambda o, t: (0, 0), shared=True)],
                      [Arg(mask, (tr, 1), row)], [Arg(h, (tr, d), row)], [((lp, d), out_dtype, (tr, d), row)])
    return a


def norm_bwd(name, h, g, mask, da, dskip):
    lp, d = h.shape
    tr = _row_tile(lp)
    row = lambda o, t: (t, 0)
    (dh,), (dg,) = seq_bwd(name, _norm_b_fn, (1, lp // tr), [Arg(g, (1, d), lambda o, t: (0, 0), shared=True)],
                           [Arg(mask, (tr, 1), row)], [Arg(h, (tr, d), row)],
                           [Arg(da, (tr, d), row), Arg(dskip, (tr, d), row)])
    return dh, dg


def _ffn_tile(lp):
    return 320 if (lp % 320 == 0 and lp > 320) else 64


def _conv_rows(ext, w, b, n):
    u2 = ext[8:8 + n]
    u1 = pltpu.roll(ext, 1, 0)[8:8 + n]
    u0 = pltpu.roll(ext, 2, 0)[8:8 + n]
    return w[2] * u2 + w[1] * u1 + w[0] * u0 + b, (u0, u1, u2)


def ffn_up_core(name, a, wup, cw, cb):
    lp, kd = a.shape
    _, nj, _, fb = wup.shape
    tr = 416 if (lp % 416 == 0 and lp > 416) else _ffn_tile(lp)
    nt = lp // tr

    def body(a_ref, wu_ref, w_ref, b_ref, u_ref, z_ref, u_s, halo_s):
        i = pl.program_id(1)

        @pl.when(i == 0)
        def _():
            u_s[...] = jnp.zeros_like(u_s)
            halo_s[...] = jnp.zeros_like(halo_s)

        old = (i + 1) % 2
        cs = []
        for s in range(2):
            tile = u_s[old, s]
            ext = jnp.concatenate([halo_s[s], tile], axis=0)
            c, _ = _conv_rows(ext, w_ref[s], b_ref[s], tr)
            cs.append(c)
            halo_s[s] = tile[tr - 8:]
        z_ref[...] = (_silu(cs[0]) * cs[1]).astype(z_ref.dtype)
        for s in range(2):
            un = lax.dot_general(a_ref[...], wu_ref[s], NN, preferred_element_type=F32)
            u_ref[s] = un
            u_s[i % 2, s] = un

    cur = lambda j, i: (0, j, jnp.minimum(i, nt - 1), 0)
    return pl.pallas_call(
        body, name=name, grid=(nj, nt + 1),
        in_specs=[pl.BlockSpec((tr, kd), lambda j, i: (jnp.minimum(i, nt - 1), 0)),
                  pl.BlockSpec((2, None, kd, fb), lambda j, i: (0, j, 0, 0)),
                  pl.BlockSpec((2, None, 3, 1, fb), lambda j, i: (0, j, 0, 0, 0)),
                  pl.BlockSpec((2, None, 1, fb), lambda j, i: (0, j, 0, 0))],
        out_specs=[pl.BlockSpec((2, None, tr, fb), cur),
                   pl.BlockSpec((None, tr, fb), lambda j, i: (j, jnp.maximum(i - 1, 0), 0))],
        out_shape=[SDS((2, nj, lp, fb), F32), SDS((nj, lp, fb), BF)],
        scratch_shapes=[pltpu.VMEM((2, 2, tr, fb), F32), pltpu.VMEM((2, 8, fb), F32)],
        compiler_params=_cp(2))(a, wup, cw, cb)


def ffn_core_bwd(name, u, dz, cw, cb):
    _, nj, lp, fb = u.shape
    tr = _ffn_tile(lp)
    nt = lp // tr
    nb8 = lp // 8

    def body(u_ref, up_ref, un_ref, dz_ref, dzn_ref, w_ref, b_ref, du_ref, dw_ref, db_ref):
        i = pl.program_id(1)

        @pl.when(i == 0)
        def _():
            dw_ref[...] = jnp.zeros_like(dw_ref)
            db_ref[...] = jnp.zeros_like(db_ref)

        prev = jnp.where(i > 0, up_ref[...], 0.0)
        nxt = jnp.where(i < nt - 1, un_ref[...], 0.0)
        dz_e = jnp.concatenate([dz_ref[...], jnp.where(i < nt - 1, dzn_ref[...], 0.0)], axis=0)
        n = tr + 8
        cs, taps = [], []
        for s in range(2):
            ext = jnp.concatenate([prev[s], u_ref[s], nxt[s]], axis=0)
            c, tp = _conv_rows(ext, w_ref[s], b_ref[s], n)
            cs.append(c)
            taps.append(tp)
        sg = jax.nn.sigmoid(cs[0])
        dcs = [dz_e * cs[1] * sg * (1.0 + cs[0] * (1.0 - sg)), dz_e * cs[0] * sg]
        for s in range(2):
            dc = dcs[s]
            w = w_ref[s]
            d1 = pltpu.roll(dc, n - 1, 0)[:tr]
            d2 = pltpu.roll(dc, n - 2, 0)[:tr]
            dcm = dc[:tr]
            du_ref[s] = w[2] * dcm + w[1] * d1 + w[0] * d2
            for k in range(3):
                dw_ref[s, k] += jnp.sum(dcm * taps[s][k][:tr], axis=0, keepdims=True)
            db_ref[s] += jnp.sum(dcm, axis=0, keepdims=True)

    return pl.pallas_call(
        body, name=name, grid=(nj, nt),
        in_specs=[pl.BlockSpec((2, None, tr, fb), lambda j, i: (0, j, i, 0)),
                  pl.BlockSpec((2, None, 8, fb), lambda j, i: (0, j, jnp.maximum(i * (tr // 8) - 1, 0), 0)),
                  pl.BlockSpec((2, None, 8, fb), lambda j, i: (0, j, jnp.minimum((i + 1) * (tr // 8), nb8 - 1), 0)),
                  pl.BlockSpec((None, tr, fb), lambda j, i: (j, i, 0)),
                  pl.BlockSpec((None, 8, fb), lambda j, i: (j, jnp.minimum((i + 1) * (tr // 8), nb8 - 1), 0)),
                  pl.BlockSpec((2, None, 3, 1, fb), lambda j, i: (0, j, 0, 0, 0)),
                  pl.BlockSpec((2, None, 1, fb), lambda j, i: (0, j, 0, 0))],
        out_specs=[pl.BlockSpec((2, None, tr, fb), lambda j, i: (0, j, i, 0)),
                   pl.BlockSpec((2, None, 3, 1, fb), lambda j, i: (0, j, 0, 0, 0)),
                   pl.BlockSpec((2, None, 1, fb), lambda j, i: (0, j, 0, 0))],
        out_shape=[SDS(u.shape, F32), SDS(cw.shape, F32), SDS(cb.shape, F32)],
        compiler_params=_cp(2))(u, u, u, dz, dz, cw, cb)


def ffn_fwd(i, h, mask, w):
    a = norm_fwd(f"ffn{i}_norm", h, w["ng"], mask)
    u, z = ffn_up_core(f"ffn{i}_up_core", a, w["up"].reshape(2, 4, D, FFN_B), w["cw"], w["cb"])
    h2 = lin_bi(f"ffn{i}_down", z, w["down"], res=h)
    return h2, (h, a, u, z)


def ffn_bwd(i, dh2, mask, w, saved):
    h, a, u, z = saved
    lp = h.shape[0]
    g = {}
    g["down"] = wgrad_bi(f"ffn{i}_dwdown", z, dh2)
    dz = lin_t_bo(f"ffn{i}_dz", dh2, w["down"])
    du, g["cw"], g["cb"] = ffn_core_bwd(f"ffn{i}_core_b", u, dz, w["cw"], w["cb"])
    du = du.reshape(8, lp, FFN_B)
    g["up"] = wgrad_bo(f"ffn{i}_dwup", a, du)
    da = lin_t_bi(f"ffn{i}_da", du, w["up"])
    dh, g["ng"] = norm_bwd(f"ffn{i}_norm_b", h, w["ng"], mask, da, dh2)
    return dh, g


HG_HB = 4


def _hgrn_fn(p, c, k, x):
    lb, go = p
    q, f, iv, g = x[0][0], x[0][1], x[0][2], x[0][3]
    (st_all,) = k
    qs = _silu(q)
    forget = lb + (1.0 - lb) * jax.nn.sigmoid(f)
    logf = jnp.log(forget)
    kk = 1.0 - forget
    gc_all = _seg_cumsum(logf, HG_C)
    r = lax.broadcasted_iota(jnp.int32, (HG_C, HG_C), 0)
    cc = lax.broadcasted_iota(jnp.int32, (HG_C, HG_C), 1)
    ns = CH // HG_C
    cells = [(j, s) for j in range(HG_HB) for s in range(ns)]

    def blk(t, j, s):
        return t[HG_C * s:HG_C * (s + 1), HG_D * j:HG_D * (j + 1)]

    gl = {c: jnp.sum(blk(logf, *c), axis=0, keepdims=True) for c in cells}
    qd = {c: blk(qs, *c) * jnp.exp(blk(gc_all, *c)) for c in cells}
    ki = {c: blk(kk, *c) * jnp.exp(-blk(gc_all, *c)) for c in cells}
    up = {c: mm_tn(blk(iv, *c), blk(kk, *c) * jnp.exp(gl[c] - blk(gc_all, *c))) for c in cells}
    st, sts = {}, []
    for j in range(HG_HB):
        cur = st_all[j]
        for s in range(ns):
            st[(j, s)] = cur
            cur = cur * jnp.exp(gl[(j, s)]) + up[(j, s)]
        sts.append(cur)
    both = {c: mm_nt(qd[c], jnp.concatenate([st[c], ki[c]], axis=0)) for c in cells}
    oc = {c: mm(jnp.where(r >= cc, both[c][:, HG_D:], 0.0), blk(iv, *c)) + both[c][:, :HG_D] for c in cells}
    zs = []
    for j in range(HG_HB):
        o = jnp.concatenate([oc[(j, s)] for s in range(ns)], axis=0)
        zs.append(_rms(o, go) * _silu(g[:, HG_D * j:HG_D * (j + 1)]))
    return (jnp.stack(sts, axis=0),), (jnp.concatenate(zs, axis=1),)


def _hgrn_args(u4, lb, go):
    lp = u4.shape[2]
    wb = HG_HB * HG_D
    xs = [Arg(u4, (4, None, CH, wb), lambda o, t: (0, o, t, 0))]
    ps = [Arg(lb, (1, wb), lambda o, t: (0, o)), Arg(go, (1, HG_D), lambda o, t: (0, 0), shared=True)]
    return (HG_H // HG_HB, lp // CH), ps, xs


def hgrn_fwd(h, mask, w):
    lp = h.shape[0]
    a = norm_fwd("hgrn_norm", h, w["ng"], mask)
    u4 = lin_bo("hgrn_in", a, w["win"]).reshape(4, HG_H // HG_HB, lp, HG_HB * HG_D)
    grid, ps, xs = _hgrn_args(u4, w["lb"], w["go"])
    (z,), (st,) = seq_fwd("hgrn_core", _hgrn_fn, grid, ps, [], xs,
                          [((lp, D), BF, (CH, HG_HB * HG_D), lambda o, t: (t, o))], carries=[(HG_HB, HG_D, HG_D)])
    h2 = lin("hgrn_out", z, w["wo"], res=h)
    return h2, (h, a, u4, z, st)


def hgrn_bwd(dh2, mask, w, saved):
    h, a, u4, z, st = saved
    lp = h.shape[0]
    g = {}
    g["wo"] = wgrad("hgrn_dwo", z, dh2)
    dz = lin_t("hgrn_dz", dh2, w["wo"])
    grid, ps, xs = _hgrn_args(u4, w["lb"], w["go"])
    (du4,), (g["lb"], g["go"]) = seq_bwd("hgrn_core_b", _hgrn_fn, grid, ps, [], xs,
                                         [Arg(dz, (CH, HG_HB * HG_D), lambda o, t: (t, o))], saved=[st],
                                         carries=[(HG_HB, HG_D, HG_D)])
    du = du4.reshape(N_DEV, lp, HG_HB * HG_D)
    g["win"] = wgrad_bo("hgrn_dwin", a, du)
    da = lin_t_bi("hgrn_da", du, w["win"])
    dh, g["ng"] = norm_bwd("hgrn_norm_b", h, w["ng"], mask, da, dh2)
    return dh, g


S5_W = S5_SG * S5_P


def s5_tables(lam_re, lam_im, log_dt, b_re, b_im, c_re, c_im):
    dt = jnp.exp(log_dt)[:, None]
    mag = jnp.exp(lam_re * dt)
    abar_re = mag * jnp.cos(lam_im * dt)
    abar_im = mag * jnp.sin(lam_im * dt)
    den = lam_re * lam_re + lam_im * lam_im
    zoh_re = ((abar_re - 1.0) * lam_re + abar_im * lam_im) / den
    zoh_im = (abar_im * lam_re - (abar_re - 1.0) * lam_im) / den
    bbar_re = zoh_re[..., None] * b_re - zoh_im[..., None] * b_im
    bbar_im = zoh_re[..., None] * b_im + zoh_im[..., None] * b_re
    eye = jnp.eye(S5_SG, dtype=F32)

    def blockdiag_in(b):
        t = b.reshape(N_DEV, S5_SG, S5_P, S5_K).transpose(0, 1, 3, 2)
        return jnp.einsum("jakp,ab->jakbp", t, eye).reshape(N_DEV, S5_SG * S5_K, S5_W)

    def blockdiag_out(c):
        t = c.reshape(N_DEV, S5_SG, S5_K, S5_P).transpose(0, 1, 3, 2)
        return jnp.einsum("japk,ab->japbk", t, eye).reshape(N_DEV, S5_W, S5_SG * S5_K)

    wb = jnp.concatenate([blockdiag_in(bbar_re), blockdiag_in(bbar_im)], axis=2)
    wc = jnp.concatenate([blockdiag_out(c_re), -blockdiag_out(c_im)], axis=1)

    abar = jnp.concatenate([abar_re.reshape(N_DEV, 1, S5_W), abar_im.reshape(N_DEV, 1, S5_W)], axis=2)
    return wb, wc, abar


def _cmul(ar, ai, xr, xi):
    return ar * xr - ai * xi, ar * xi + ai * xr


def _scan_rows(ar, ai, xr, xi, reverse):
    for s in range(6):
        sh = -(1 << s) if reverse else (1 << s)
        dr, di = _cmul(ar, ai, _shift_raw(xr, sh), _shift_raw(xi, sh))
        xr, xi = xr + dr, xi + di
        ar, ai = ar * ar - ai * ai, 2.0 * ar * ai
    return xr, xi


@jax.custom_vjp
def cscan(ar, ai, br, bi):
    return _scan_rows(ar, ai, br, bi, False)


def _cscan_fwd(ar, ai, br, bi):
    xr, xi = _scan_rows(ar, ai, br, bi, False)
    return (xr, xi), (ar, ai, xr, xi)


def _cscan_bwd(res, g):
    ar, ai, xr, xi = res
    lr, li = _scan_rows(ar, -ai, g[0], g[1], True)
    pr, pi = _shift_raw(xr, 1), _shift_raw(xi, 1)
    dar = jnp.sum(lr * pr + li * pi, axis=0, keepdims=True)
    dai = jnp.sum(li * pr - lr * pi, axis=0, keepdims=True)
    return dar, dai, lr, li


cscan.defvjp(_cscan_fwd, _cscan_bwd)

S5_BB = 8


def _s5_fn(p, c, k, x):
    wb, wc, abar, dsk = p
    (a,) = x
    (x0,) = k
    blocks = range(S5_BB)
    aj = [a[:, 128 * j:128 * (j + 1)] for j in blocks]
    bu = [mm(aj[j], wb[j]) for j in blocks]
    first = lax.broadcasted_iota(jnp.int32, (CH, S5_W), 0) == 0
    last = lax.broadcasted_iota(jnp.int32, (CH, 2 * S5_W), 0) == CH - 1
    xxs, x0n = [], []
    for j in blocks:
        ar, ai = abar[j][:, :S5_W], abar[j][:, S5_W:]
        cr, ci = _cmul(ar, ai, x0[j][:, :S5_W], x0[j][:, S5_W:])
        xr, xi = cscan(ar, ai, bu[j][:, :S5_W] + jnp.where(first, cr, 0.0), bu[j][:, S5_W:] + jnp.where(first, ci, 0.0))
        xx = jnp.concatenate([xr, xi], axis=1)
        x0n.append(jnp.sum(jnp.where(last, xx, 0.0), axis=0, keepdims=True))
        xxs.append(xx)
    y = jnp.concatenate([mm(xxs[j], wc[j]) for j in blocks], axis=1)
    return (jnp.stack(x0n, axis=0),), (jax.nn.gelu(y + dsk * a),)


def _s5_args(a, tb, dsk):
    lp = a.shape[0]
    wb, wc, abar = tb
    ps = [Arg(wb, (S5_BB, 128, 2 * S5_W), lambda o, t: (o, 0, 0)), Arg(wc, (S5_BB, 2 * S5_W, 128), lambda o, t: (o, 0, 0)),
          Arg(abar, (S5_BB, 1, 2 * S5_W), lambda o, t: (o, 0, 0)), Arg(dsk, (1, 128 * S5_BB), lambda o, t: (0, o))]
    xs = [Arg(a, (CH, 128 * S5_BB), lambda o, t: (t, o))]
    return (N_DEV // S5_BB, lp // CH), ps, xs


def _glu_res_fn(p, c, k, x):
    h, vg = x
    return (), ((h + vg[:, :D] * jax.nn.sigmoid(vg[:, D:])) * c[0],)


def _glu_args(h, vg, mask):
    lp = h.shape[0]
    tr = _row_tile(lp)
    row = lambda o, t: (t, 0)
    return (1, lp // tr), [Arg(mask, (tr, 1), row)], [Arg(h, (tr, D), row), Arg(vg, (tr, 2 * D), row)], tr


def s5_fwd(h, mask, w):
    lp = h.shape[0]
    a = norm_fwd("s5_norm", h, w["ng"], mask, out_dtype=F32)
    grid, ps, xs = _s5_args(a, w["tb"], w["dsk"])
    (z,), (st,) = seq_fwd("s5_core", _s5_fn, grid, ps, [], xs,
                          [((lp, D), BF, (CH, 128 * S5_BB), lambda o, t: (t, o))], carries=[(S5_BB, 1, 2 * S5_W)])
    vg = lin("s5_glu", z, w["wglu"])
    grid2, cs, xs2, tr = _glu_args(h, vg, mask)
    (h2,), _ = seq_fwd("s5_res", _glu_res_fn, grid2, [], cs, xs2, [((lp, D), F32, (tr, D), lambda o, t: (t, 0))])
    return h2, (h, a, z, vg, st)


def s5_bwd(dh2, mask, w, saved):
    h, a, z, vg, st = saved
    g = {}
    grid2, cs, xs2, tr = _glu_args(h, vg, mask)
    (dskip, dvg), _ = seq_bwd("s5_res_b", _glu_res_fn, grid2, [], cs, xs2, [Arg(dh2, (tr, D), lambda o, t: (t, 0))])
    g["wglu"] = wgrad("s5_dwglu", z, dvg)
    dz = lin_t("s5_dz", dvg, w["wglu"])
    grid, ps, xs = _s5_args(a, w["tb"], w["dsk"])
    (da,), dps = seq_bwd("s5_core_b", _s5_fn, grid, ps, [], xs, [Arg(dz, (CH, 128 * S5_BB), lambda o, t: (t, o))],
                         saved=[st], carries=[(S5_BB, 1, 2 * S5_W)])
    g["tb"] = tuple(dps[:3])
    g["dsk"] = dps[3]
    dh, g["ng"] = norm_bwd("s5_norm_b", h, w["ng"], mask, da, dskip)
    return dh, g


def _rope_angles(lp, dim):
    pos = np.maximum(np.arange(lp, dtype=np.float32) - PAD, 0.0).astype(np.float32)
    inv = (1.0 / (ROPE_BASE ** (np.arange(0, dim, 2, dtype=np.float32) / dim))).astype(np.float32)
    return (pos[:, None] * inv[None, :]).astype(np.float32)


def ret_consts(lp):
    f = np.float32
    ang = _rope_angles(lp, RET_DK)
    lg = np.log(1.0 - np.exp2(-5.0 - np.arange(RET_H, dtype=f))).astype(f)
    p = np.arange(CH, dtype=f)
    diff = p[:, None] - p[None, :]
    decay = np.where(diff >= 0, np.exp(diff[None] * lg[:, None, None]), 0.0).astype(f)
    qd = np.exp((p[None, :] + 1.0) * lg[:, None])[..., None].astype(f)
    kd = np.exp((CH - 1.0 - p[None, :]) * lg[:, None])[..., None].astype(f)
    cd = np.exp(CH * lg)[:, None, None].astype(f)
    return np.cos(ang).astype(f), np.sin(ang).astype(f), decay, qd, kd, cd


def _ret_fn(p, c, k, x):
    (gn,) = p
    cos, sin, decay, qd, kd, cd = c
    (st,) = k
    (u,) = x
    hd = RET_DK // 2
    qk_w = RET_H * RET_DK
    heads = range(RET_H)

    def rope(t):
        t1, t2 = t[:, :hd], t[:, hd:]
        return jnp.concatenate([t1 * cos - t2 * sin, t1 * sin + t2 * cos], axis=1)

    qr = [rope(u[:, RET_DK * h:RET_DK * (h + 1)]) for h in heads]
    kr = [rope(u[:, qk_w + RET_DK * h:qk_w + RET_DK * (h + 1)]) * (RET_DK ** -0.5) for h in heads]
    v = [u[:, 2 * qk_w + RET_DV * h:2 * qk_w + RET_DV * (h + 1)] for h in heads]
    scores = [mm_nt(qr[h], kr[h]) for h in heads]
    inter = [mm(qr[h] * qd[h], st[h]) for h in heads]
    st_new = jnp.stack([st[h] * cd[h] + mm_tn(kr[h] * kd[h], v[h]) for h in heads], axis=0)
    o = [mm(scores[h] * decay[h], v[h]) + inter[h] for h in heads]
    zs = []
    for h in heads:
        mu = jnp.mean(o[h], axis=-1, keepdims=True)
        var = jnp.mean(jnp.square(o[h] - mu), axis=-1, keepdims=True)
        gate = u[:, 2 * qk_w + RET_H * RET_DV + RET_DV * h:2 * qk_w + RET_H * RET_DV + RET_DV * (h + 1)]
        zs.append((o[h] - mu) * lax.rsqrt(var + EPS) * gn[:, RET_DV * h:RET_DV * (h + 1)] * _silu(gate))
    return (st_new,), (jnp.concatenate(zs, axis=1),)


def _ret_args(u, gn, rc):
    lp, uw = u.shape
    cos, sin, decay, qd, kd, cd = rc
    full = lambda o, t: (0, 0, 0)
    ps = [Arg(gn, (1, RET_H * RET_DV), lambda o, t: (0, 0))]
    cs = [Arg(cos, (CH, RET_DK // 2), lambda o, t: (t, 0)), Arg(sin, (CH, RET_DK // 2), lambda o, t: (t, 0)),
          Arg(decay, (RET_H, CH, CH), full), Arg(qd, (RET_H, CH, 1), full), Arg(kd, (RET_H, CH, 1), full),
          Arg(cd, (RET_H, 1, 1), full)]
    xs = [Arg(u, (CH, uw), lambda o, t: (t, 0))]
    return (1, lp // CH), ps, cs, xs


def ret_fwd(h, mask, w):
    lp = h.shape[0]
    a = norm_fwd("ret_norm", h, w["ng"], mask)
    u = lin("ret_in", a, w["win"])
    grid, ps, cs, xs = _ret_args(u, w["gn"], w["rc"])
    (z,), (st,) = seq_fwd("ret_core", _ret_fn, grid, ps, cs, xs,
                          [((lp, 2 * D), BF, (CH, RET_H * RET_DV), lambda o, t: (t, 0))],
                          carries=[(RET_H, RET_DK, RET_DV)], save_dtype=BF)
    h2 = lin("ret_out", z, w["wo"], res=h)
    return h2, (h, a, u, z, st)


def ret_bwd(dh2, mask, w, saved):
    h, a, u, z, st = saved
    g = {}
    g["wo"] = wgrad("ret_dwo", z, dh2)
    dz = lin_t("ret_dz", dh2, w["wo"])
    grid, ps, cs, xs = _ret_args(u, w["gn"], w["rc"])
    (du,), (g["gn"],) = seq_bwd("ret_core_b", _ret_fn, grid, ps, cs, xs,
                                [Arg(dz, (CH, RET_H * RET_DV), lambda o, t: (t, 0))], saved=[st],
                                carries=[(RET_H, RET_DK, RET_DV)])
    g["win"] = wgrad("ret_dwin", a, du)
    da = lin_t("ret_da", du, w["win"])
    dh, g["ng"] = norm_bwd("ret_norm_b", h, w["ng"], mask, da, dh2)
    return dh, g


def mla_consts(lp):
    ang = _rope_angles(lp, MLA_ROPE)
    cos = np.concatenate([np.cos(ang), np.cos(ang)], axis=1).astype(np.float32)
    sin = np.concatenate([np.sin(ang), np.sin(ang)], axis=1).astype(np.float32)
    hd = MLA_ROPE // 2
    i = np.arange(hd)
    rot = np.zeros((MLA_ROPE, MLA_ROPE), np.float32)
    rot[hd + i, i] = -1.0
    rot[i, hd + i] = 1.0
    return cos, sin, rot


def _mla_prep1_fn(p, c, k, x):
    gq, gkv = p
    (down,) = x
    return (), (_rms(down[:, :MLA_QL], gq), _rms(down[:, MLA_QL:MLA_QL + MLA_KVL], gkv), down[:, MLA_QL + MLA_KVL:])


def _mla_prep2(p, c, x):
    gq, gk = p
    cos, sin, rot = c
    q, kv, kpe = x
    qn = _rms(q, gq)
    qn_n, qn_r = qn[:, :MLA_NOPE], qn[:, MLA_NOPE:]
    qo = jnp.concatenate([qn_n, qn_r * cos + cright(qn_r, rot) * sin], axis=1)
    kn = kv[:, :MLA_NOPE]
    ms = (jnp.sum(kn * kn, axis=-1, keepdims=True) + jnp.sum(kpe * kpe, axis=-1, keepdims=True)) / MLA_QK
    r = lax.rsqrt(ms + EPS)
    kr = kpe * r * gk[:, MLA_NOPE:]
    ko = jnp.concatenate([kn * r * gk[:, :MLA_NOPE], kr * cos + cright(kr, rot) * sin], axis=1)
    return qo, ko, kv[:, MLA_NOPE:]


def _mla_prep2_fn(p, c, k, x):
    return (), _mla_prep2(p, c, x)[:2]


def _mla_prep2_b_fn(p, c, k, x):
    return (), _mla_prep2(p, c, x)


def _prep1_args(down, gq, gkv):
    lp = down.shape[0]
    tr = _row_tile(lp)
    ps = [Arg(gq, (1, MLA_QL), lambda o, t: (0, 0), shared=True), Arg(gkv, (1, MLA_KVL), lambda o, t: (0, 0), shared=True)]
    return (1, lp // tr), ps, [Arg(down, (tr, down.shape[1]), lambda o, t: (t, 0))], tr


def _prep2_args(qraw, kvraw, kpe, gq, gk, mc):
    lp = kpe.shape[0]
    tr = _row_tile(lp)
    cos, sin, rot = mc
    ps = [Arg(gq, (1, MLA_QK), lambda o, t: (0, 0), shared=True), Arg(gk, (1, MLA_QK), lambda o, t: (0, 0), shared=True)]
    cs = [Arg(cos, (tr, MLA_ROPE), lambda o, t: (o, 0)), Arg(sin, (tr, MLA_ROPE), lambda o, t: (o, 0)),
          Arg(rot, (MLA_ROPE, MLA_ROPE), lambda o, t: (0, 0))]
    xs = [Arg(qraw, (None, tr, MLA_QK), lambda o, t: (t, o, 0)), Arg(kvraw, (None, tr, MLA_NOPE + MLA_V), lambda o, t: (t, o, 0)),
          Arg(kpe, (tr, MLA_ROPE), lambda o, t: (o, 0), acc=True)]
    return (lp // tr, MLA_H), ps, cs, xs, tr


ATT_HB = 2


def _attn_tile(lp):
    return 832 if (lp % 832 == 0 and lp > 832) else 64


def _attn_mask(qi, ki, ta):
    rows = qi * ta + lax.broadcasted_iota(jnp.int32, (ta, ta), 0)
    cols = ki * ta + lax.broadcasted_iota(jnp.int32, (ta, ta), 1)
    return (cols >= PAD) & ((cols // CH) <= (rows // CH))


def attn_fwd(q, k, kv):
    nh, lp, dq = q.shape
    ta = _attn_tile(lp)
    nb = lp // ta
    scale = MLA_QK ** -0.5

    hb = ATT_HB

    def body(q_ref, k_ref, v_ref, o_ref, lse_ref, m_s, l_s, acc_s):
        qi, ki = pl.program_id(1), pl.program_id(2)

        @pl.when(ki == 0)
        def _():
            m_s[...] = jnp.full_like(m_s, NEG)
            l_s[...] = jnp.zeros_like(l_s)
            acc_s[...] = jnp.zeros_like(acc_s)

        def step(masked):
            ss = [_bdot(q_ref[j], k_ref[j], 1, 1) * scale for j in range(hb)]
            ps = []
            for j in range(hb):
                s = jnp.where(_attn_mask(qi, ki, ta), ss[j], NEG) if masked else ss[j]
                m_new = jnp.maximum(m_s[j], jnp.max(s, axis=-1, keepdims=True))
                p = jnp.exp(s - m_new)
                alpha = jnp.exp(m_s[j] - m_new)
                l_s[j] = alpha * l_s[j] + jnp.sum(p, axis=-1, keepdims=True)
                m_s[j] = m_new
                ps.append((p, alpha))
            for j in range(hb):
                acc_s[j] = ps[j][1] * acc_s[j] + _bdot(ps[j][0], v_ref[j], 1, 0)

        pl.when((ki == qi) | (ki == 0))(functools.partial(step, True))
        pl.when((ki < qi) & (ki > 0))(functools.partial(step, False))

        @pl.when(ki == nb - 1)
        def _():
            for j in range(hb):
                o_ref[:, MLA_V * j:MLA_V * (j + 1)] = (acc_s[j] / l_s[j]).astype(o_ref.dtype)
                lse_ref[j] = m_s[j] + jnp.log(l_s[j])

    return pl.pallas_call(
        body, name="mla_attn", grid=(nh // hb, nb, nb),
        in_specs=[pl.BlockSpec((hb, ta, dq), lambda h, qi, ki: (h, qi, 0)),
                  pl.BlockSpec((hb, ta, dq), lambda h, qi, ki: (h, jnp.minimum(ki, qi), 0)),
                  pl.BlockSpec((hb, ta, MLA_V), lambda h, qi, ki: (h, jnp.minimum(ki, qi), 1))],
        out_specs=[pl.BlockSpec((ta, hb * MLA_V), lambda h, qi, ki: (qi, h)),
                   pl.BlockSpec((hb, ta, 1), lambda h, qi, ki: (h, qi, 0))],
        out_shape=[SDS((lp, nh * MLA_V), BF), SDS((nh, lp, 1), F32)],
        scratch_shapes=[pltpu.VMEM((hb, ta, 1), F32), pltpu.VMEM((hb, ta, 1), F32), pltpu.VMEM((hb, ta, MLA_V), F32)],
        compiler_params=_cp(3))(q, k, kv)


def attn_bwd(q, k, kv, o, do, lse):
    nh, lp, dq = q.shape
    ta = _attn_tile(lp)
    nb = lp // ta
    scale = MLA_QK ** -0.5

    def body(q_ref, k_ref, v_ref, o_ref, do_ref, lse_ref, dq_ref, dk_ref, dv_ref, dk_s, dv_s):
        ki, qi = pl.program_id(1), pl.program_id(2)

        @pl.when((ki == 0) & (qi == 0))
        def _():
            dq_ref[...] = jnp.zeros_like(dq_ref)

        @pl.when(qi == 0)
        def _():
            dk_s[...] = jnp.zeros_like(dk_s)
            dv_s[...] = jnp.zeros_like(dv_s)

        def step(masked):
            dov = do_ref[...]
            s = _bdot(q_ref[...], k_ref[...], 1, 1) * scale
            dp = _bdot(dov, v_ref[...], 1, 1)
            if masked:
                s = jnp.where(_attn_mask(qi, ki, ta), s, NEG)
            p = jnp.exp(s - lse_ref[...])
            delta = jnp.sum(dov * o_ref[...].astype(F32), axis=-1, keepdims=True)
            dv_s[...] += _bdot(p, dov, 0, 0)
            ds = p * (dp - delta) * scale
            rows = pl.ds(pl.multiple_of(qi * ta, ta), ta)
            dq_ref[rows, :] += _bdot(ds, k_ref[...], 1, 0)
            dk_s[...] += _bdot(ds, q_ref[...], 0, 0)

        pl.when((ki == qi) | (ki == 0))(functools.partial(step, True))
        pl.when((ki < qi) & (ki > 0))(functools.partial(step, False))

        @pl.when(qi == nb - 1)
        def _():
            dk_ref[...] = dk_s[...]
            dv_ref[...] = dv_s[...]

    qmap = lambda h, ki, qi: (h, jnp.maximum(qi, ki), 0)
    return pl.pallas_call(
        body, name="mla_attn_b", grid=(nh, nb, nb),
        in_specs=[pl.BlockSpec((None, ta, dq), qmap),
                  pl.BlockSpec((None, ta, dq), lambda h, ki, qi: (h, ki, 0)),
                  pl.BlockSpec((None, ta, MLA_V), lambda h, ki, qi: (h, ki, 1)),
                  pl.BlockSpec((ta, MLA_V), lambda h, ki, qi: (jnp.maximum(qi, ki), h)),
                  pl.BlockSpec((ta, MLA_V), lambda h, ki, qi: (jnp.maximum(qi, ki), h)),
                  pl.BlockSpec((None, ta, 1), qmap)],
        out_specs=[pl.BlockSpec((None, lp, dq), lambda h, ki, qi: (h, 0, 0)),
                   pl.BlockSpec((None, ta, dq), lambda h, ki, qi: (h, ki, 0)),
                   pl.BlockSpec((None, ta, MLA_V), lambda h, ki, qi: (h, ki, 0))],
        out_shape=[SDS((nh, lp, dq), F32), SDS((nh, lp, dq), F32), SDS((nh, lp, MLA_V), F32)],
        scratch_shapes=[pltpu.VMEM((ta, dq), F32), pltpu.VMEM((ta, MLA_V), F32)],
        compiler_params=_cp(3))(q, k, kv, o, do, lse)


def mla_fwd(h, mask, w):
    lp = h.shape[0]
    a = norm_fwd("mla_norm", h, w["ng"], mask)
    down = lin("mla_down", a, w["wdown"])
    grid, ps, xs, tr = _prep1_args(down, w["gcq"], w["gckv"])
    row = lambda o, t: (t, 0)
    (cq, ckv, kpe), _ = seq_fwd("mla_prep1", _mla_prep1_fn, grid, ps, [], xs,
                                [((lp, MLA_QL), BF, (tr, MLA_QL), row), ((lp, MLA_KVL), BF, (tr, MLA_KVL), row),
                                 ((lp, MLA_ROPE), F32, (tr, MLA_ROPE), row)])
    qraw = lin_bo("mla_uq", cq, w["wuq"])
    kvraw = lin_bo("mla_ukv", ckv, w["wukv"])
    grid, ps, cs, xs, tr = _prep2_args(qraw, kvraw, kpe, w["gq"], w["gk"], w["mc"])
    hm = lambda o, t: (t, o, 0)
    (q, k), _ = seq_fwd("mla_prep2", _mla_prep2_fn, grid, ps, cs, xs,
                        [((MLA_H, lp, MLA_QK), BF, (None, tr, MLA_QK), hm), ((MLA_H, lp, MLA_QK), BF, (None, tr, MLA_QK), hm)])
    o, lse = attn_fwd(q, k, kvraw)
    h2 = lin("mla_out", o, w["wo"], res=h)
    return h2, (h, a, down, cq, ckv, kpe, qraw, kvraw, q, k, o, lse)


def mla_bwd(dh2, mask, w, saved, emit):
    h, a, down, cq, ckv, kpe, qraw, kvraw, q, k, o, lse = saved
    lp = h.shape[0]
    g = {}
    g["wo"] = wgrad("mla_dwo", o, dh2)
    do = lin_t("mla_do", dh2, w["wo"])
    do = emit("wo", [g["wo"]], do)
    dq, dk, dv = attn_bwd(q, k, kvraw, o, do, lse)
    grid, ps, cs, xs, tr = _prep2_args(qraw, kvraw, kpe, w["gq"], w["gk"], w["mc"])
    hm = lambda o, t: (t, o, 0)
    (dqraw, dkvraw, dkpe), (g["gq"], g["gk"]) = seq_bwd(
        "mla_prep2_b", _mla_prep2_b_fn, grid, ps, cs, xs,
        [Arg(dq, (None, tr, MLA_QK), hm), Arg(dk, (None, tr, MLA_QK), hm), Arg(dv, (None, tr, MLA_V), hm)])
    g["wuq"] = wgrad_bo("mla_dwuq", cq, dqraw)
    dcq = lin_t_bi("mla_dcq", dqraw, w["wuq"])
    g["wukv"] = wgrad_bo("mla_dwukv", ckv, dkvraw)
    dckv = lin_t_bi("mla_dckv", dkvraw, w["wukv"])
    dckv = emit("wu", [g["wuq"], g["wukv"]], dckv)
    grid, ps, xs, tr = _prep1_args(down, w["gcq"], w["gckv"])
    row = lambda o, t: (t, 0)
    (ddown,), (g["gcq"], g["gckv"]) = seq_bwd(
        "mla_prep1_b", _mla_prep1_fn, grid, ps, [], xs,
        [Arg(dcq, (tr, MLA_QL), row), Arg(dckv, (tr, MLA_KVL), row), Arg(dkpe, (tr, MLA_ROPE), row)])
    g["wdown"] = wgrad("mla_dwdown", a, ddown)
    da = lin_t("mla_da", ddown, w["wdown"])
    dh, g["ng"] = norm_bwd("mla_norm_b", h, w["ng"], mask, da, dh2)
    return dh, g


def loss_head(h, target):
    lp, d = h.shape
    tr = _row_tile(lp)
    tpad = jnp.concatenate([jnp.zeros((OFF, d), F32), target], axis=0)

    def body(h_ref, t_ref, loss_ref, dh_ref):
        i = pl.program_id(0)

        @pl.when(i == 0)
        def _():
            loss_ref[...] = jnp.zeros_like(loss_ref)

        rows = i * tr + lax.broadcasted_iota(jnp.int32, (tr, d), 0)
        e = jnp.where(rows >= OFF, h_ref[...] - t_ref[...], 0.0)
        loss_ref[...] += jnp.sum(e * e) * (0.5 / d)
        dh_ref[...] = e * (1.0 / d)

    blk = pl.BlockSpec((tr, d), lambda i: (i, 0))
    return pl.pallas_call(
        body, name="loss_head", grid=(lp // tr,), in_specs=[blk, blk],
        out_specs=[pl.BlockSpec((8, 128), lambda i: (0, 0)), blk],
        out_shape=[SDS((8, 128), F32), SDS((lp, d), F32)], compiler_params=_cp(1))(h, tpad)


ADAM_LAND_BYTES = 20 * 1024 * 1024


def _adam_tile(r, c, nl):
    if r % 8:
        return r
    best = 8
    for t in range(8, r + 1, 8):
        if r % t == 0 and N_DEV * t * c * 4 * 2 * nl <= ADAM_LAND_BYTES:
            best = t
    return best


def adamw(name, lands, w, m, v):
    nl, r, c = w.shape
    tr = _adam_tile(r, c, nl)
    c1 = 1.0 / (1.0 - ADAM_B1 ** ADAM_STEP)
    c2 = 1.0 / (1.0 - ADAM_B2 ** ADAM_STEP)

    def body(*refs):
        l_refs = refs[:nl]
        w_ref, m_ref, v_ref, g_ref, d_ref, nm_ref, nv_ref = refs[nl:]
        layer = pl.program_id(0)
        for j in range(nl):
            @pl.when(layer == j)
            def _(j=j):
                g = l_refs[j][0]
                for i in range(1, N_DEV):
                    g = g + l_refs[j][i]
                g_ref[...] = g

        g = g_ref[...]
        nm = ADAM_B1 * m_ref[...] + (1.0 - ADAM_B1) * g
        nv = ADAM_B2 * v_ref[...] + (1.0 - ADAM_B2) * (g * g)
        nm_ref[...] = nm
        nv_ref[...] = nv
        d_ref[...] = -ADAM_LR * ((nm * c1) / (jnp.sqrt(nv * c2) + ADAM_EPS) + ADAM_WD * w_ref[...])

    blk = pl.BlockSpec((None, tr, c), lambda l, i: (l, i, 0))
    land_specs = [pl.BlockSpec((N_DEV, tr, c), lambda l, i, j=j: (0, jnp.where(l == j, i, 0), 0)) for j in range(nl)]
    return pl.pallas_call(
        body, name=name, grid=(nl, r // tr), in_specs=land_specs + [blk, blk, blk],
        out_specs=[blk, blk, blk, blk], out_shape=[SDS((nl, r, c), F32)] * 4, compiler_params=_cp(2))(*lands, w, m, v)


ANY = pl.BlockSpec(memory_space=pl.ANY)
MESH = pl.DeviceIdType.MESH


def _me():
    return lax.axis_index("x"), lax.axis_index("y"), lax.axis_index("c")


def _peers():
    x, y, c = _me()
    out = []
    for k in range(1, N_DEV):
        px = 1 - x if k & 4 else x
        py = 1 - y if k & 2 else y
        pc = 1 - c if k & 1 else c
        out.append(((px, py, pc), 4 * px + 2 * py + pc))
    return out


HBM_SPEC = pl.BlockSpec(memory_space=pltpu.HBM)
SEM_SPEC = pl.BlockSpec(memory_space=pltpu.SEMAPHORE)
DATAFLOW = pltpu.SideEffectType.DATAFLOW_SIDE_EFFECTING


def _my_index():
    return 4 * lax.axis_index("x") + 2 * lax.axis_index("y") + lax.axis_index("c")


def _hbm(a):
    return pltpu.with_memory_space_constraint(a, pltpu.HBM)


NP = N_DEV - 1


def _push_copy(x_ref, land_ref, send, recv, pid, src_idx, dst_idx, scatter):
    src = x_ref.at[src_idx] if scatter else x_ref
    return pltpu.make_async_remote_copy(src_ref=src, dst_ref=land_ref.at[dst_idx], send_sem=send, recv_sem=recv,
                                        device_id=pid, device_id_type=MESH)


def push_start(name, xs, me, scatter, carry=None):
    n = len(xs)
    lands = []
    for a in xs:
        own = lax.dynamic_index_in_dim(a, me, 0, keepdims=True) if scatter else a[None]
        z = lax.empty((N_DEV,) + own.shape[1:], a.dtype)
        lands.append(lax.dynamic_update_slice(z, own, (me,) + (0,) * (own.ndim - 1)))
    ns = 2 * NP * n
    ops = xs + lands + ([carry] if carry is not None else [])
    na = len(ops)

    def body(*refs):
        x_refs, land_refs = refs[:n], refs[n:2 * n]
        sems = refs[na:na + ns]
        token = refs[-1]
        x, y, c = _me()
        mine = 4 * x + 2 * y + c
        for i in range(n):
            for k, (pid, pidx) in enumerate(_peers()):
                s = 2 * (NP * i + k)
                _push_copy(x_refs[i], land_refs[i], sems[s], sems[s + 1], pid, pidx, mine, scatter).start()
        token[...] = jnp.zeros_like(token)

    out_shape = ([pltpu.SemaphoreType.DMA(())] * ns + [pltpu.HBM(a.shape, a.dtype) for a in ops]
                 + [SDS((8, 128), F32)])
    res = pl.pallas_call(
        body, name=name, out_shape=out_shape, in_specs=[HBM_SPEC] * na,
        out_specs=[SEM_SPEC] * ns + [HBM_SPEC] * na + [pl.BlockSpec(memory_space=pltpu.VMEM)],
        input_output_aliases={i: ns + i for i in range(na)},
        compiler_params=pltpu.CompilerParams(has_side_effects=DATAFLOW))(*[_hbm(a) for a in ops])
    sems, thru, token = res[:ns], res[ns:-1], res[-1]
    handles = [dict(x=thru[i], land=thru[n + i], sems=list(sems[2 * NP * i:2 * NP * (i + 1)]), token=token)
               for i in range(n)]
    return (handles, thru[2 * n]) if carry is not None else handles


def push_wait(name, hds, after, scatter):
    n = len(hds)
    ns = 2 * NP

    def body(*refs):
        x_refs, land_refs = refs[:n], refs[n:2 * n]
        sems = refs[2 * n:2 * n + ns * n]
        for i in range(n):
            for k, (pid, pidx) in enumerate(_peers()):
                cp = _push_copy(x_refs[i], land_refs[i], sems[ns * i + 2 * k], sems[ns * i + 2 * k + 1], pid, pidx, pidx,
                                scatter)
                cp.wait_send()
                cp.wait_recv()

    arrs = [hd["x"] for hd in hds] + [hd["land"] for hd in hds]
    sems = [s for hd in hds for s in hd["sems"]]
    res = pl.pallas_call(
        body, name=name, out_shape=[pltpu.HBM(a.shape, a.dtype) for a in arrs],
        in_specs=[HBM_SPEC] * (2 * n) + [SEM_SPEC] * (ns * n) + [ANY], out_specs=[HBM_SPEC] * (2 * n),
        input_output_aliases={i: i for i in range(2 * n)},
        compiler_params=pltpu.CompilerParams(has_side_effects=DATAFLOW))(*arrs, *sems, after)
    return list(res[n:])


WEIGHTS = ['meta_tokens', 'norm_mix_g', 'norm_ffn_g', 'mla_w_down', 'mla_cq_norm_g', 'mla_ckv_norm_g', 'mla_w_uq',
           'mla_w_ukv', 'mla_q_head_g', 'mla_k_head_g', 'mla_w_o', 'hgrn_w_in', 'hgrn_lb_logits', 'hgrn_o_norm_g',
           'hgrn_w_o', 's5_lam_re', 's5_lam_im', 's5_log_dt', 's5_b_re', 's5_b_im', 's5_c_re', 's5_c_im', 's5_d',
           's5_w_glu', 'ret_w_in', 'ret_gn_g', 'ret_w_o', 'ffn_w_up', 'ffn_conv_w', 'ffn_conv_b', 'ffn_w_down']
BIG = ['mla_w_down', 'mla_w_uq', 'mla_w_ukv', 'mla_w_o', 'hgrn_w_in', 'hgrn_w_o', 's5_w_glu', 'ret_w_in', 'ret_w_o',
       'ffn_w_up', 'ffn_w_down']
SMALL_SH = ['meta_tokens', 's5_d', 'ret_gn_g', 'ffn_conv_w']
REP_S5 = ['s5_lam_re', 's5_lam_im', 's5_log_dt', 's5_b_re', 's5_b_im', 's5_c_re', 's5_c_im']
REP_REST = ['norm_mix_g', 'norm_ffn_g', 'mla_cq_norm_g', 'mla_ckv_norm_g', 'mla_q_head_g', 'mla_k_head_g',
            'hgrn_lb_logits', 'hgrn_o_norm_g', 'ffn_conv_b']
SMALL_REP = REP_REST + REP_S5
LANE = 128


def _flat(arrs, mult):
    v = jnp.concatenate([a.reshape(-1) for a in arrs])
    pad = (-v.shape[0]) % mult
    return jnp.pad(v, (0, pad)).reshape(-1, LANE)


def _unflat(flat2d, like):
    v = flat2d.reshape(-1)
    out, o = [], 0
    for a in like:
        out.append(v[o:o + a.size].reshape(a.shape))
        o += a.size
    return out


def _lb_of(logits):
    cum = jnp.cumsum(jax.nn.softmax(logits, axis=0), axis=0)
    return (cum - cum[0:1])[1:2]


def _cols_to_blocks(g):
    k, n = g.shape
    return g.reshape(k, N_DEV, n // N_DEV).transpose(1, 0, 2)


def _blocks_to_cols(wb):
    nb, k, n = wb.shape
    return wb.transpose(1, 0, 2).reshape(k, nb * n)


SUBS = ['mla', 'ffn0', 'hgrn', 'ffn1', 's5', 'ffn2', 'ret', 'ffn3']
GROUPS = [[('mla_w_down', 0), ('mla_w_uq', 0), ('mla_w_ukv', 0), ('mla_w_o', 0)],
          [('ffn_w_up', 0), ('ffn_w_down', 0)],
          [('hgrn_w_in', 0), ('hgrn_w_o', 0)],
          [('ffn_w_up', 1), ('ffn_w_down', 1)],
          [('s5_w_glu', 0)],
          [('ffn_w_up', 2), ('ffn_w_down', 2)],
          [('ret_w_in', 0), ('ret_w_o', 0)],
          [('ffn_w_up', 3), ('ffn_w_down', 3)]]


def _pack8(parts, mult):
    v = jnp.concatenate(parts, axis=1)
    return jnp.pad(v, ((0, 0), (0, (-v.shape[1]) % mult))).reshape(N_DEV, -1, LANE)


def _sub_weights(k, got, rep, tabs, lp):
    ngm, ngf = rep['norm_mix_g'], rep['norm_ffn_g']
    if k == 0:
        return dict(ng=ngm[0:1], wdown=got[0].reshape(D, -1), gcq=rep['mla_cq_norm_g'], gckv=rep['mla_ckv_norm_g'],
                    wuq=got[1], wukv=got[2], gq=rep['mla_q_head_g'], gk=rep['mla_k_head_g'], wo=got[3].reshape(D, D),
                    mc=mla_consts(lp))
    if k == 2:
        return dict(ng=ngm[1:2], win=got[0], lb=tabs['lb'], go=rep['hgrn_o_norm_g'], wo=got[1].reshape(D, D))
    if k == 4:
        return dict(ng=ngm[2:3], tb=tabs['tb'], dsk=tabs['s5_d'], wglu=_blocks_to_cols(got[0]))
    if k == 6:
        return dict(ng=ngm[3:4], win=_blocks_to_cols(got[0]), gn=tabs['ret_gn_g'], wo=got[1].reshape(2 * D, D),
                    rc=ret_consts(lp))
    i = k // 2
    return dict(ng=ngf[i:i + 1], up=got[0], cw=tabs['conv_w'][:, i].reshape(2, 4, 3, 1, FFN_B),
                cb=rep['ffn_conv_b'][i].reshape(2, 4, 1, FFN_B), down=got[1].reshape(4, FFN_B, D))


def _sub_grad_blocks(k, g):
    if k == 0:
        parts = [g['wdown'], g['wuq'], g['wukv'], g['wo']]
    elif k == 2:
        parts = [g['win'], g['wo']]
    elif k == 4:
        parts = [_cols_to_blocks(g['wglu'])]
    elif k == 6:
        parts = [_cols_to_blocks(g['win']), g['wo']]
    else:
        parts = [g['up'], g['down']]
    return parts


_FWD = [mla_fwd, None, hgrn_fwd, None, s5_fwd, None, ret_fwd, None]
_BWD = [mla_bwd, None, hgrn_bwd, None, s5_bwd, None, ret_bwd, None]


def _step(args):
    w = {n: args[n] for n in WEIGHTS}
    x2, tgt = args['x'][0], args['loss_target'][0]

    lp = x2.shape[0] + OFF
    me = _my_index()
    mask = _rowmask(lp)
    rep = {n: w[n] for n in SMALL_REP}

    xs, slots = [], []
    for gi, grp in enumerate(GROUPS):
        items = [w[n][l].astype(BF) for n, l in grp] + ([_flat([w[n] for n in SMALL_SH], LANE)] if gi == 0 else [])
        slots.append((len(xs), len(items)))
        xs += items
    gh = push_start("gather_start", xs, me, scatter=False)

    def fetch(gi, after):
        s, cnt = slots[gi]
        return push_wait("gather_wait_" + SUBS[gi], gh[s:s + cnt], after, scatter=False)

    got = fetch(0, x2)
    sm, o, smp = got[-1].reshape(N_DEV, -1), 0, {}
    for n in SMALL_SH:
        smp[n] = sm[:, o:o + w[n].size].reshape((N_DEV,) + w[n].shape)
        o += w[n].size
    meta = smp['meta_tokens'].transpose(1, 0, 2).reshape(N_META, D)
    lb, lb_vjp = jax.vjp(_lb_of, rep['hgrn_lb_logits'])
    s5p = [rep[n][0] for n in ('s5_lam_re', 's5_lam_im', 's5_log_dt', 's5_b_re', 's5_b_im', 's5_c_re', 's5_c_im')]
    tb, tb_vjp = jax.vjp(s5_tables, *s5p)
    tabs = dict(lb=lb, tb=tb, s5_d=smp['s5_d'].reshape(1, D), ret_gn_g=smp['ret_gn_g'].reshape(1, 2 * D),
                conv_w=smp['ffn_conv_w'])
    h = jnp.concatenate([jnp.zeros((PAD, D), F32), meta, x2], axis=0)
    ws, saved = [], []
    for k in range(8):
        if k > 0:
            got = fetch(k, h)
        ws.append(_sub_weights(k, got, rep, tabs, lp))
        if k % 2:
            h, sv = ffn_fwd(k // 2, h, mask, ws[k])
        else:
            h, sv = _FWD[k](h, mask, ws[k])
        saved.append(sv)
    loss, dh = loss_head(h, tgt)

    gs, sh = [None] * 8, [None] * 8
    early = {}

    def emit(tag, grads, carry):
        blocks = [t.reshape((N_DEV, -1) + t.shape[-1:]) if t.ndim == 2 else t for t in grads]
        early[tag], carry = push_start("scatter_start_mla_" + tag, blocks, me, scatter=True, carry=carry)
        return carry

    for k in reversed(range(1, 8)):
        if k % 2:
            dh, gs[k] = ffn_bwd(k // 2, dh, mask, ws[k], saved[k])
        else:
            dh, gs[k] = _BWD[k](dh, mask, ws[k], saved[k])
        blocks = [b.reshape((N_DEV,) + w[n].shape[1:]) for b, (n, _) in zip(_sub_grad_blocks(k, gs[k]), GROUPS[k])]
        sh[k], dh = push_start("scatter_start_" + SUBS[k], blocks, me, scatter=True, carry=dh)
        if k == 4:
            gs5 = _flat(list(tb_vjp(gs[4]['tb'])), 8 * LANE)
            rh_s5, dh = push_start("small_grads_start_s5", [gs5], me, scatter=False, carry=dh)
    dh, gs[0] = mla_bwd(dh, mask, ws[0], saved[0], emit)
    dmeta = dh[PAD:OFF].reshape(N_META, N_DEV, D // N_DEV).transpose(1, 0, 2)
    dcw = jnp.stack([gs[2 * i + 1]['cw'].reshape(N_DEV, 3, FFN_B) for i in range(4)], axis=1)
    last = push_start("scatter_start_mla", [gs[0]['wdown'].reshape(N_DEV, D // N_DEV, -1),
                                            _pack8([t.reshape(N_DEV, -1) for t in (dmeta, gs[4]['dsk'], gs[6]['gn'], dcw)], LANE)],
                      me, scatter=True)
    sh[0] = [last[0], early['wu'][0], early['wu'][1], early['wo'][0], last[1]]
    grad_x = dh[OFF:]

    g_rep = {
        'norm_mix_g': jnp.concatenate([gs[k]['ng'] for k in (0, 2, 4, 6)], axis=0),
        'norm_ffn_g': jnp.concatenate([gs[k]['ng'] for k in (1, 3, 5, 7)], axis=0),
        'mla_cq_norm_g': gs[0]['gcq'], 'mla_ckv_norm_g': gs[0]['gckv'], 'mla_q_head_g': gs[0]['gq'],
        'mla_k_head_g': gs[0]['gk'], 'hgrn_lb_logits': lb_vjp(gs[2]['lb'])[0], 'hgrn_o_norm_g': gs[2]['go'],
        'ffn_conv_b': jnp.stack([gs[k]['cb'].reshape(-1) for k in (1, 3, 5, 7)], axis=0),
    }
    loss_part = loss[0, 0:1]
    grep = _flat([g_rep[n] for n in REP_REST] + [loss_part], 8 * LANE)
    rh = push_start("small_grads_start", [grep], me, scatter=False)

    lands = {n: [None] * w[n].shape[0] for n in BIG}
    res = {}
    late = [n for n, _ in GROUPS[0]]
    for k in reversed(range(1, 8)):
        got = push_wait("scatter_wait_" + SUBS[k], sh[k], grep, scatter=True)
        for (n, l), t in zip(GROUPS[k], got):
            lands[n][l] = t
    for n in BIG:
        if n not in late:
            res[n] = adamw("adam_" + n, lands[n], w[n], args['m_' + n], args['v_' + n])
    after = res['ffn_w_up'][1]
    got = push_wait("scatter_wait_" + SUBS[0], sh[0], after, scatter=True)
    small_land = got[-1]
    (rep_land,) = push_wait("small_grads_wait", rh, after, scatter=False)
    (s5_land,) = push_wait("small_grads_wait_s5", rh_s5, after, scatter=False)
    for (n, _), t in zip(GROUPS[0], got):
        res[n] = adamw("adam_" + n, [t], w[n], args['m_' + n], args['v_' + n])

    def flat_adam(name, land, names, mult, extra=()):
        like = [w[n] for n in names]
        pad = [jnp.zeros_like(e) for e in extra]
        out = adamw(name, [land], _flat(like + pad, mult)[None], _flat([args['m_' + n] for n in names] + pad, mult)[None],
                    _flat([args['v_' + n] for n in names] + pad, mult)[None])
        for n, parts in zip(names, zip(*[_unflat(t, like) for t in out])):
            res[n] = list(parts)
        return out[0]

    flat_adam("adam_small_sharded", small_land, SMALL_SH, LANE)
    flat_adam("adam_s5_replicated", s5_land, REP_S5, 8 * LANE)
    gsum = flat_adam("adam_small_replicated", rep_land, REP_REST, 8 * LANE, extra=[loss_part])
    total = gsum.reshape(-1)[sum(w[n].size for n in REP_REST)]
    outs = [total, grad_x[None]]
    for k in range(4):
        outs += [res[n][k] for n in WEIGHTS]
    return tuple(outs)


def kernel(x, meta_tokens, norm_mix_g, norm_ffn_g, mla_w_down, mla_cq_norm_g, mla_ckv_norm_g, mla_w_uq, mla_w_ukv, mla_q_head_g, mla_k_head_g, mla_w_o, hgrn_w_in, hgrn_lb_logits, hgrn_o_norm_g, hgrn_w_o, s5_lam_re, s5_lam_im, s5_log_dt, s5_b_re, s5_b_im, s5_c_re, s5_c_im, s5_d, s5_w_glu, ret_w_in, ret_gn_g, ret_w_o, ffn_w_up, ffn_conv_w, ffn_conv_b, ffn_w_down, loss_target, m_meta_tokens, m_norm_mix_g, m_norm_ffn_g, m_mla_w_down, m_mla_cq_norm_g, m_mla_ckv_norm_g, m_mla_w_uq, m_mla_w_ukv, m_mla_q_head_g, m_mla_k_head_g, m_mla_w_o, m_hgrn_w_in, m_hgrn_lb_logits, m_hgrn_o_norm_g, m_hgrn_w_o, m_s5_lam_re, m_s5_lam_im, m_s5_log_dt, m_s5_b_re, m_s5_b_im, m_s5_c_re, m_s5_c_im, m_s5_d, m_s5_w_glu, m_ret_w_in, m_ret_gn_g, m_ret_w_o, m_ffn_w_up, m_ffn_conv_w, m_ffn_conv_b, m_ffn_w_down, v_meta_tokens, v_norm_mix_g, v_norm_ffn_g, v_mla_w_down, v_mla_cq_norm_g, v_mla_ckv_norm_g, v_mla_w_uq, v_mla_w_ukv, v_mla_q_head_g, v_mla_k_head_g, v_mla_w_o, v_hgrn_w_in, v_hgrn_lb_logits, v_hgrn_o_norm_g, v_hgrn_w_o, v_s5_lam_re, v_s5_lam_im, v_s5_log_dt, v_s5_b_re, v_s5_b_im, v_s5_c_re, v_s5_c_im, v_s5_d, v_s5_w_glu, v_ret_w_in, v_ret_gn_g, v_ret_w_o, v_ffn_w_up, v_ffn_conv_w, v_ffn_conv_b, v_ffn_w_down):
    return _step(dict(locals()))
```

```python
import functools
import math

import jax
import jax.numpy as jnp
import numpy as np
from jax import lax
from jax.experimental import pallas as pl
from jax.experimental.pallas import tpu as pltpu

F32 = jnp.float32
BF = jnp.bfloat16
SDS = jax.ShapeDtypeStruct

N_DEV = 8
D = 1024
N_META = 16
PAD = 48
OFF = PAD + N_META
CH = 64
EPS = 1e-6
NEG = -1e30
ROPE_BASE = 10000.0

MLA_H, MLA_NOPE, MLA_ROPE, MLA_V = 8, 128, 64, 128
MLA_QK = MLA_NOPE + MLA_ROPE
MLA_QL, MLA_KVL = 384, 256
HG_H, HG_D, HG_C = 8, 128, 16
S5_G, S5_P, S5_K = 64, 64, 16
S5_SG = 8
RET_H, RET_DK, RET_DV = 4, 256, 512
FFN_F = 2816
FFN_B = 704

ADAM_LR, ADAM_B1, ADAM_B2, ADAM_EPS, ADAM_WD, ADAM_STEP = 0.001, 0.9, 0.999, 1e-08, 0.01, 10

VMEM_LIMIT = 56 * 1024 * 1024
ARB = "arbitrary"


def _cp(n):
    return pltpu.CompilerParams(dimension_semantics=(ARB,) * n, vmem_limit_bytes=VMEM_LIMIT)


def _bdot(a, b, ca, cb):
    return lax.dot_general(a.astype(BF), b.astype(BF), (((ca,), (cb,)), ((), ())), preferred_element_type=F32)


@jax.custom_vjp
def mm(a, b):
    return _bdot(a, b, 1, 0)


@jax.custom_vjp
def mm_nt(a, b):
    return _bdot(a, b, 1, 1)


@jax.custom_vjp
def mm_tn(a, b):
    return _bdot(a, b, 0, 0)


mm.defvjp(lambda a, b: (mm(a, b), (a, b)),
          lambda r, g: (mm_nt(g, r[1]).astype(r[0].dtype), mm_tn(r[0], g).astype(r[1].dtype)))
mm_nt.defvjp(lambda a, b: (mm_nt(a, b), (a, b)),
             lambda r, g: (mm(g, r[1]).astype(r[0].dtype), mm_tn(g, r[0]).astype(r[1].dtype)))
mm_tn.defvjp(lambda a, b: (mm_tn(a, b), (a, b)),
             lambda r, g: (mm_nt(r[1], g).astype(r[0].dtype), mm(r[0], g).astype(r[1].dtype)))


def _xdot(a, b, ca, cb):
    return lax.dot_general(a, b, (((ca,), (cb,)), ((), ())), preferred_element_type=F32,
                           precision=lax.Precision.HIGHEST)


@jax.custom_vjp
def cright(x, r):
    return _xdot(x, r, 1, 0)


cright.defvjp(lambda x, r: (cright(x, r), r), lambda r, g: (_xdot(g, r, 1, 1), jnp.zeros_like(r)))


def _shift_raw(x, s):
    n = x.shape[0]
    r = lax.broadcasted_iota(jnp.int32, x.shape, 0)
    y = pltpu.roll(x, s % n, 0)
    return jnp.where((r >= s) & (r < n + s), y, 0.0)


def _seg_shift_raw(x, s, seg, up):
    n = x.shape[0]
    r = lax.broadcasted_iota(jnp.int32, x.shape, 0) % seg
    if up:
        return jnp.where(r < seg - s, pltpu.roll(x, n - s, 0), 0.0)
    return jnp.where(r >= s, pltpu.roll(x, s, 0), 0.0)


@functools.partial(jax.custom_vjp, nondiff_argnums=(1, 2))
def seg_shift(x, s, seg):
    return _seg_shift_raw(x, s, seg, False)


seg_shift.defvjp(lambda x, s, seg: (_seg_shift_raw(x, s, seg, False), None),
                 lambda s, seg, _, g: (_seg_shift_raw(g, s, seg, True),))


def _seg_cumsum(x, seg):
    s = 1
    while s < seg:
        x = x + seg_shift(x, s, seg)
        s *= 2
    return x


def _rms(x, g):
    return x * lax.rsqrt(jnp.mean(x * x, axis=-1, keepdims=True) + EPS) * g


def _silu(x):
    return x * jax.nn.sigmoid(x)


def _mm_call(name, a, b, *, grid, a_spec, b_spec, o_shape, o_spec, dims, acc_shape, res=None, res_spec=None,
             mask_tm=None):
    nk = grid[2]

    def body(*refs):
        if res is None:
            a_ref, b_ref, o_ref = refs[:3]
        else:
            a_ref, b_ref, r_ref, o_ref = refs[:4]
        k = pl.program_id(2)

        def dot():
            return lax.dot_general(a_ref[...].astype(BF), b_ref[...].astype(BF), dims, preferred_element_type=F32)

        def finish(v):
            if res is not None:
                v = v + r_ref[...].astype(F32)
                rows = pl.program_id(0) * mask_tm + lax.broadcasted_iota(jnp.int32, v.shape, 0)
                v = jnp.where(rows >= PAD, v, 0.0)
            o_ref[...] = v.astype(o_ref.dtype)

        if nk == 1:
            finish(dot())
            return
        acc = refs[-1]

        @pl.when(k == 0)
        def _():
            acc[...] = dot()

        @pl.when((k > 0) & (k < nk - 1))
        def _():
            acc[...] += dot()

        @pl.when(k == nk - 1)
        def _():
            finish(acc[...] + dot())

    ins = [a, b] + ([res] if res is not None else [])
    specs = [a_spec, b_spec] + ([res_spec] if res is not None else [])
    scratch = [pltpu.VMEM(acc_shape, F32)] if nk > 1 else []
    return pl.pallas_call(body, name=name, grid=grid, in_specs=specs, out_specs=o_spec, out_shape=o_shape,
                          scratch_shapes=scratch, compiler_params=_cp(3))(*ins)


NN = (((1,), (0,)), ((), ()))
NT = (((1,), (1,)), ((), ()))
TN = (((0,), (0,)), ((), ()))


def _row_tile(lp):
    for t in (832, 640, 320, 64):
        if lp % t == 0:
            return t
    raise ValueError(lp)


def _col_tile(n):
    for t in (1024, 768, 512, 384, 256, 128):
        if n % t == 0:
            return t
    return n


MM_WHOLE_ROWS = 4160


def _mm_rows(m, whole=False):
    if whole and m <= MM_WHOLE_ROWS:
        return m
    return 2080 if m % 2080 == 0 else _row_tile(m)


def _mm_cols(n):
    for t in (512, 384, 256, 128):
        if n % t == 0:
            return t
    return n


def lin(name, a, w, out_dtype=F32, res=None):
    m, k = a.shape
    n = w.shape[1]
    tm, tn, tc = _mm_rows(m, res is None and a.dtype == BF and k <= 1024), _mm_cols(n), _col_tile(k)
    return _mm_call(name, a, w, grid=(m // tm, n // tn, k // tc),
                    a_spec=pl.BlockSpec((tm, tc), lambda i, j, kk: (i, kk)),
                    b_spec=pl.BlockSpec((tc, tn), lambda i, j, kk: (kk, j)),
                    o_shape=SDS((m, n), out_dtype), o_spec=pl.BlockSpec((tm, tn), lambda i, j, kk: (i, j)),
                    dims=NN, acc_shape=(tm, tn), res=res,
                    res_spec=pl.BlockSpec((tm, tn), lambda i, j, kk: (i, j)), mask_tm=tm)


def lin_bo(name, a, wb, out_dtype=F32):
    m, k = a.shape
    nb, _, n = wb.shape
    tm = _mm_rows(m, a.dtype == BF)
    return _mm_call(name, a, wb, grid=(m // tm, nb, 1),
                    a_spec=pl.BlockSpec((tm, k), lambda i, j, kk: (i, 0)),
                    b_spec=pl.BlockSpec((None, k, n), lambda i, j, kk: (j, 0, 0)),
                    o_shape=SDS((nb, m, n), out_dtype), o_spec=pl.BlockSpec((None, tm, n), lambda i, j, kk: (j, i, 0)),
                    dims=NN, acc_shape=(tm, n))


def lin_bi(name, ab, wb, out_dtype=F32, res=None):
    nb, m, k = ab.shape
    n = wb.shape[2]
    tm, tn = _mm_rows(m), _mm_cols(n)
    return _mm_call(name, ab, wb, grid=(m // tm, n // tn, nb),
                    a_spec=pl.BlockSpec((None, tm, k), lambda i, j, kk: (kk, i, 0)),
                    b_spec=pl.BlockSpec((None, k, tn), lambda i, j, kk: (kk, 0, j)),
                    o_shape=SDS((m, n), out_dtype), o_spec=pl.BlockSpec((tm, tn), lambda i, j, kk: (i, j)),
                    dims=NN, acc_shape=(tm, tn), res=res,
                    res_spec=pl.BlockSpec((tm, tn), lambda i, j, kk: (i, j)), mask_tm=tm)


def lin_t(name, g, w, out_dtype=F32):
    m, n = g.shape
    k = w.shape[0]
    tm, tk, tc = _mm_rows(m), _col_tile(k), _col_tile(n)
    return _mm_call(name, g, w, grid=(m // tm, k // tk, n // tc),
                    a_spec=pl.BlockSpec((tm, tc), lambda i, j, kk: (i, kk)),
                    b_spec=pl.BlockSpec((tk, tc), lambda i, j, kk: (j, kk)),
                    o_shape=SDS((m, k), out_dtype), o_spec=pl.BlockSpec((tm, tk), lambda i, j, kk: (i, j)),
                    dims=NT, acc_shape=(tm, tk))


def lin_t_bi(name, gb, wb, out_dtype=F32):
    nb, m, n = gb.shape
    k = wb.shape[1]
    tm, tk = _mm_rows(m), _col_tile(k)
    return _mm_call(name, gb, wb, grid=(m // tm, k // tk, nb),
                    a_spec=pl.BlockSpec((None, tm, n), lambda i, j, kk: (kk, i, 0)),
                    b_spec=pl.BlockSpec((None, tk, n), lambda i, j, kk: (kk, j, 0)),
                    o_shape=SDS((m, k), out_dtype), o_spec=pl.BlockSpec((tm, tk), lambda i, j, kk: (i, j)),
                    dims=NT, acc_shape=(tm, tk))


def lin_t_bo(name, g, wb, out_dtype=F32):
    m, n = g.shape
    nb, k, _ = wb.shape
    tm = _mm_rows(m)
    return _mm_call(name, g, wb, grid=(m // tm, nb, 1),
                    a_spec=pl.BlockSpec((tm, n), lambda i, j, kk: (i, 0)),
                    b_spec=pl.BlockSpec((None, k, n), lambda i, j, kk: (j, 0, 0)),
                    o_shape=SDS((nb, m, k), out_dtype), o_spec=pl.BlockSpec((None, tm, k), lambda i, j, kk: (j, i, 0)),
                    dims=NT, acc_shape=(tm, k))


def wgrad(name, a, g):
    m, k = a.shape
    n = g.shape[1]
    tm, tn = _mm_rows(m), (_col_tile(n) if k <= 1024 else _mm_cols(n))
    return _mm_call(name, a, g, grid=(1, n // tn, m // tm),
                    a_spec=pl.BlockSpec((tm, k), lambda i, j, kk: (kk, 0)),
                    b_spec=pl.BlockSpec((tm, tn), lambda i, j, kk: (kk, j)),
                    o_shape=SDS((k, n), F32), o_spec=pl.BlockSpec((k, tn), lambda i, j, kk: (0, j)),
                    dims=TN, acc_shape=(k, tn))


def wgrad_bo(name, a, gb):
    m, k = a.shape
    nb, _, n = gb.shape
    tm = _mm_rows(m)
    return _mm_call(name, a, gb, grid=(nb, 1, m // tm),
                    a_spec=pl.BlockSpec((tm, k), lambda i, j, kk: (kk, 0)),
                    b_spec=pl.BlockSpec((None, tm, n), lambda i, j, kk: (i, kk, 0)),
                    o_shape=SDS((nb, k, n), F32), o_spec=pl.BlockSpec((None, k, n), lambda i, j, kk: (i, 0, 0)),
                    dims=TN, acc_shape=(k, n))


def wgrad_bi(name, zb, g):
    nb, m, k = zb.shape
    n = g.shape[1]
    tm, tn = _mm_rows(m), _col_tile(n)
    return _mm_call(name, zb, g, grid=(nb, n // tn, m // tm),
                    a_spec=pl.BlockSpec((None, tm, k), lambda i, j, kk: (i, kk, 0)),
                    b_spec=pl.BlockSpec((tm, tn), lambda i, j, kk: (kk, j)),
                    o_shape=SDS((nb, k, n), F32), o_spec=pl.BlockSpec((None, k, tn), lambda i, j, kk: (i, 0, j)),
                    dims=TN, acc_shape=(k, tn))


class Arg:
    def __init__(self, arr, block, imap, shared=False, acc=False):
        self.arr, self.block, self.imap = arr, block, imap
        self.shared = shared
        self.acc = acc

    @property
    def spec(self):
        return pl.BlockSpec(self.block, self.imap)

    def vshape(self):
        return tuple(b for b in self.block if b is not None)


def _rev(arg, nt):
    return pl.BlockSpec(arg.block, lambda o, t, _f=arg.imap: _f(o, nt - 1 - t))


def seq_fwd(name, fn, grid, params, consts, xs, outs, carries=(), save_dtype=F32):
    no, nt = grid
    n_p, n_c, n_x, n_y, n_k = len(params), len(consts), len(xs), len(outs), len(carries)

    def body(*refs):
        p_refs = refs[:n_p]
        c_refs = refs[n_p:n_p + n_c]
        x_refs = refs[n_p + n_c:n_p + n_c + n_x]
        r = n_p + n_c + n_x
        y_refs = refs[r:r + n_y]
        s_refs = refs[r + n_y:r + n_y + n_k]
        k_refs = refs[r + n_y + n_k:]
        t = pl.program_id(1)

        if n_k:
            @pl.when(t == 0)
            def _():
                for k in k_refs:
                    k[...] = jnp.zeros_like(k)

        carry = tuple(k[...] for k in k_refs)
        for s, c in zip(s_refs, carry):
            s[...] = c.astype(s.dtype)
        new_carry, ys = fn(tuple(p[...] for p in p_refs), tuple(c[...] for c in c_refs), carry,
                           tuple(x[...] for x in x_refs))
        for k, c in zip(k_refs, new_carry):
            k[...] = c
        for y_ref, y in zip(y_refs, ys):
            y_ref[...] = y.astype(y_ref.dtype)

    out_shape = [SDS(s, d) for (s, d, _, _) in outs]
    out_specs = [pl.BlockSpec(b, im) for (_, _, b, im) in outs]
    for cs in carries:
        out_shape.append(SDS((no, nt) + cs, save_dtype))
        out_specs.append(pl.BlockSpec((None, None) + cs, lambda o, t, _n=len(cs): (o, t) + (0,) * _n))
    res = pl.pallas_call(
        body, name=name, grid=grid, in_specs=[a.spec for a in list(params) + list(consts) + list(xs)],
        out_specs=out_specs, out_shape=out_shape, scratch_shapes=[pltpu.VMEM(cs, F32) for cs in carries],
        compiler_params=_cp(2))(*[a.arr for a in list(params) + list(consts) + list(xs)])
    return res[:n_y], res[n_y:]


def seq_bwd(name, fn, grid, params, consts, xs, dys, saved=(), carries=()):
    no, nt = grid
    n_p, n_c, n_x, n_y, n_k = len(params), len(consts), len(xs), len(dys), len(carries)

    def body(*refs):
        p_refs = refs[:n_p]
        c_refs = refs[n_p:n_p + n_c]
        x_refs = refs[n_p + n_c:n_p + n_c + n_x]
        r = n_p + n_c + n_x
        g_refs = refs[r:r + n_y]
        s_refs = refs[r + n_y:r + n_y + n_k]
        r = r + n_y + n_k
        dx_refs = refs[r:r + n_x]
        dp_refs = refs[r + n_x:r + n_x + n_p]
        k_refs = refs[r + n_x + n_p:]
        o = pl.program_id(0)
        t = pl.program_id(1)

        if n_k:
            @pl.when(t == 0)
            def _():
                for k in k_refs:
                    k[...] = jnp.zeros_like(k)

        for a, dp in zip(params, dp_refs):
            @pl.when((t == 0) & (o == 0) if a.shared else (t == 0))
            def _(dp=dp):
                dp[...] = jnp.zeros_like(dp)

        for a, dx in zip(xs, dx_refs):
            if a.acc:
                @pl.when(t == 0)
                def _(dx=dx):
                    dx[...] = jnp.zeros_like(dx)

        consts_v = tuple(c[...] for c in c_refs)

        def f(pv, cv, xv):
            return fn(pv, consts_v, cv, xv)

        pv = tuple(p[...] for p in p_refs)
        cv = tuple(s[...].astype(F32) for s in s_refs)
        xv = tuple(x[...] for x in x_refs)
        (new_carry, ys), vjp = jax.vjp(f, pv, cv, xv)
        cot = (tuple(k[...] for k in k_refs), tuple(g[...].astype(y.dtype) for g, y in zip(g_refs, ys)))
        dpv, dcv, dxv = vjp(cot)
        for k, c in zip(k_refs, dcv):
            k[...] = c
        for dp, v in zip(dp_refs, dpv):
            dp[...] += v
        for a, dx, v in zip(xs, dx_refs, dxv):
            if a.acc:
                dx[...] += v
            else:
                dx[...] = v.astype(dx.dtype)

    in_specs = ([_rev(a, nt) for a in list(params) + list(consts) + list(xs) + list(dys)]
                + [pl.BlockSpec((None, None) + cs, lambda o, t, _n=len(cs): (o, nt - 1 - t) + (0,) * _n) for cs in carries])
    out_shape = [SDS(a.arr.shape, F32) for a in xs] + [SDS(a.arr.shape, F32) for a in params]
    out_specs = [_rev(a, nt) for a in list(xs) + list(params)]
    res = pl.pallas_call(
        body, name=name, grid=grid, in_specs=in_specs, out_specs=out_specs, out_shape=out_shape,
        scratch_shapes=[pltpu.VMEM(cs, F32) for cs in carries], compiler_params=_cp(2))(
            *[a.arr for a in list(params) + list(consts) + list(xs) + list(dys)], *saved)
    return res[:n_x], res[n_x:]


def _rowmask(lp):
    return (jnp.arange(lp) >= PAD).astype(F32)[:, None]


def _norm_fn(p, c, k, x):
    return (), (_rms(x[0] * c[0], p[0]),)


def _norm_b_fn(p, c, k, x):
    h = x[0] * c[0]
    return (), (_rms(h, p[0]), h)


def norm_fwd(name, h, g, mask, out_dtype=BF):
    lp, d = h.shape
    tr = _row_tile(lp)
    row = lambda o, t: (t, 0)
    (a,), _ = seq_fwd(name, _norm_fn, (1, lp // tr), [Arg(g, (1, d), lambda o, t: (0, 0), shared=True)],
                      [Arg(mask, (tr, 1), row)], [Arg(h, (tr, d), row)], [((lp, d), out_dtype, (tr, d), row)])
    return a


def norm_bwd(name, h, g, mask, da, dskip):
    lp, d = h.shape
    tr = _row_tile(lp)
    row = lambda o, t: (t, 0)
    (dh,), (dg,) = seq_bwd(name, _norm_b_fn, (1, lp // tr), [Arg(g, (1, d), lambda o, t: (0, 0), shared=True)],
                           [Arg(mask, (tr, 1), row)], [Arg(h, (tr, d), row)],
                           [Arg(da, (tr, d), row), Arg(dskip, (tr, d), row)])
    return dh, dg


def _ffn_tile(lp):
    return 320 if (lp % 320 == 0 and lp > 320) else 64


def _conv_rows(ext, w, b, n):
    u2 = ext[8:8 + n]
    u1 = pltpu.roll(ext, 1, 0)[8:8 + n]
    u0 = pltpu.roll(ext, 2, 0)[8:8 + n]
    return w[2] * u2 + w[1] * u1 + w[0] * u0 + b, (u0, u1, u2)


def ffn_up_core(name, a, wup, cw, cb):
    lp, kd = a.shape
    _, nj, _, fb = wup.shape
    tr = 416 if (lp % 416 == 0 and lp > 416) else _ffn_tile(lp)
    nt = lp // tr

    def body(a_ref, wu_ref, w_ref, b_ref, u_ref, z_ref, u_s, halo_s):
        i = pl.program_id(1)

        @pl.when(i == 0)
        def _():
            u_s[...] = jnp.zeros_like(u_s)
            halo_s[...] = jnp.zeros_like(halo_s)

        old = (i + 1) % 2
        cs = []
        for s in range(2):
            tile = u_s[old, s]
            ext = jnp.concatenate([halo_s[s], tile], axis=0)
            c, _ = _conv_rows(ext, w_ref[s], b_ref[s], tr)
            cs.append(c)
            halo_s[s] = tile[tr - 8:]
        z_ref[...] = (_silu(cs[0]) * cs[1]).astype(z_ref.dtype)
        for s in range(2):
            un = lax.dot_general(a_ref[...], wu_ref[s], NN, preferred_element_type=F32)
            u_ref[s] = un
            u_s[i % 2, s] = un

    cur = lambda j, i: (0, j, jnp.minimum(i, nt - 1), 0)
    return pl.pallas_call(
        body, name=name, grid=(nj, nt + 1),
        in_specs=[pl.BlockSpec((tr, kd), lambda j, i: (jnp.minimum(i, nt - 1), 0)),
                  pl.BlockSpec((2, None, kd, fb), lambda j, i: (0, j, 0, 0)),
                  pl.BlockSpec((2, None, 3, 1, fb), lambda j, i: (0, j, 0, 0, 0)),
                  pl.BlockSpec((2, None, 1, fb), lambda j, i: (0, j, 0, 0))],
        out_specs=[pl.BlockSpec((2, None, tr, fb), cur),
                   pl.BlockSpec((None, tr, fb), lambda j, i: (j, jnp.maximum(i - 1, 0), 0))],
        out_shape=[SDS((2, nj, lp, fb), F32), SDS((nj, lp, fb), BF)],
        scratch_shapes=[pltpu.VMEM((2, 2, tr, fb), F32), pltpu.VMEM((2, 8, fb), F32)],
        compiler_params=_cp(2))(a, wup, cw, cb)


def ffn_core_bwd(name, u, dz, cw, cb):
    _, nj, lp, fb = u.shape
    tr = _ffn_tile(lp)
    nt = lp // tr
    nb8 = lp // 8

    def body(u_ref, up_ref, un_ref, dz_ref, dzn_ref, w_ref, b_ref, du_ref, dw_ref, db_ref):
        i = pl.program_id(1)

        @pl.when(i == 0)
        def _():
            dw_ref[...] = jnp.zeros_like(dw_ref)
            db_ref[...] = jnp.zeros_like(db_ref)

        prev = jnp.where(i > 0, up_ref[...], 0.0)
        nxt = jnp.where(i < nt - 1, un_ref[...], 0.0)
        dz_e = jnp.concatenate([dz_ref[...], jnp.where(i < nt - 1, dzn_ref[...], 0.0)], axis=0)
        n = tr + 8
        cs, taps = [], []
        for s in range(2):
            ext = jnp.concatenate([prev[s], u_ref[s], nxt[s]], axis=0)
            c, tp = _conv_rows(ext, w_ref[s], b_ref[s], n)
            cs.append(c)
            taps.append(tp)
        sg = jax.nn.sigmoid(cs[0])
        dcs = [dz_e * cs[1] * sg * (1.0 + cs[0] * (1.0 - sg)), dz_e * cs[0] * sg]
        for s in range(2):
            dc = dcs[s]
            w = w_ref[s]
            d1 = pltpu.roll(dc, n - 1, 0)[:tr]
            d2 = pltpu.roll(dc, n - 2, 0)[:tr]
            dcm = dc[:tr]
            du_ref[s] = w[2] * dcm + w[1] * d1 + w[0] * d2
            for k in range(3):
                dw_ref[s, k] += jnp.sum(dcm * taps[s][k][:tr], axis=0, keepdims=True)
            db_ref[s] += jnp.sum(dcm, axis=0, keepdims=True)

    return pl.pallas_call(
        body, name=name, grid=(nj, nt),
        in_specs=[pl.BlockSpec((2, None, tr, fb), lambda j, i: (0, j, i, 0)),
                  pl.BlockSpec((2, None, 8, fb), lambda j, i: (0, j, jnp.maximum(i * (tr // 8) - 1, 0), 0)),
                  pl.BlockSpec((2, None, 8, fb), lambda j, i: (0, j, jnp.minimum((i + 1) * (tr // 8), nb8 - 1), 0)),
                  pl.BlockSpec((None, tr, fb), lambda j, i: (j, i, 0)),
                  pl.BlockSpec((None, 8, fb), lambda j, i: (j, jnp.minimum((i + 1) * (tr // 8), nb8 - 1), 0)),
                  pl.BlockSpec((2, None, 3, 1, fb), lambda j, i: (0, j, 0, 0, 0)),
                  pl.BlockSpec((2, None, 1, fb), lambda j, i: (0, j, 0, 0))],
        out_specs=[pl.BlockSpec((2, None, tr, fb), lambda j, i: (0, j, i, 0)),
                   pl.BlockSpec((2, None, 3, 1, fb), lambda j, i: (0, j, 0, 0, 0)),
                   pl.BlockSpec((2, None, 1, fb), lambda j, i: (0, j, 0, 0))],
        out_shape=[SDS(u.shape, F32), SDS(cw.shape, F32), SDS(cb.shape, F32)],
        compiler_params=_cp(2))(u, u, u, dz, dz, cw, cb)


def ffn_fwd(i, h, mask, w):
    a = norm_fwd(f"ffn{i}_norm", h, w["ng"], mask)
    u, z = ffn_up_core(f"ffn{i}_up_core", a, w["up"].reshape(2, 4, D, FFN_B), w["cw"], w["cb"])
    h2 = lin_bi(f"ffn{i}_down", z, w["down"], res=h)
    return h2, (h, a, u, z)


def ffn_bwd(i, dh2, mask, w, saved):
    h, a, u, z = saved
    lp = h.shape[0]
    g = {}
    g["down"] = wgrad_bi(f"ffn{i}_dwdown", z, dh2)
    dz = lin_t_bo(f"ffn{i}_dz", dh2, w["down"])
    du, g["cw"], g["cb"] = ffn_core_bwd(f"ffn{i}_core_b", u, dz, w["cw"], w["cb"])
    du = du.reshape(8, lp, FFN_B)
    g["up"] = wgrad_bo(f"ffn{i}_dwup", a, du)
    da = lin_t_bi(f"ffn{i}_da", du, w["up"])
    dh, g["ng"] = norm_bwd(f"ffn{i}_norm_b", h, w["ng"], mask, da, dh2)
    return dh, g


HG_HB = 4


def _hgrn_fn(p, c, k, x):
    lb, go = p
    q, f, iv, g = x[0][0], x[0][1], x[0][2], x[0][3]
    (st_all,) = k
    qs = _silu(q)
    forget = lb + (1.0 - lb) * jax.nn.sigmoid(f)
    logf = jnp.log(forget)
    kk = 1.0 - forget
    gc_all = _seg_cumsum(logf, HG_C)
    r = lax.broadcasted_iota(jnp.int32, (HG_C, HG_C), 0)
    cc = lax.broadcasted_iota(jnp.int32, (HG_C, HG_C), 1)
    ns = CH // HG_C
    cells = [(j, s) for j in range(HG_HB) for s in range(ns)]

    def blk(t, j, s):
        return t[HG_C * s:HG_C * (s + 1), HG_D * j:HG_D * (j + 1)]

    gl = {c: jnp.sum(blk(logf, *c), axis=0, keepdims=True) for c in cells}
    qd = {c: blk(qs, *c) * jnp.exp(blk(gc_all, *c)) for c in cells}
    ki = {c: blk(kk, *c) * jnp.exp(-blk(gc_all, *c)) for c in cells}
    up = {c: mm_tn(blk(iv, *c), blk(kk, *c) * jnp.exp(gl[c] - blk(gc_all, *c))) for c in cells}
    st, sts = {}, []
    for j in range(HG_HB):
        cur = st_all[j]
        for s in range(ns):
            st[(j, s)] = cur
            cur = cur * jnp.exp(gl[(j, s)]) + up[(j, s)]
        sts.append(cur)
    both = {c: mm_nt(qd[c], jnp.concatenate([st[c], ki[c]], axis=0)) for c in cells}
    oc = {c: mm(jnp.where(r >= cc, both[c][:, HG_D:], 0.0), blk(iv, *c)) + both[c][:, :HG_D] for c in cells}
    zs = []
    for j in range(HG_HB):
        o = jnp.concatenate([oc[(j, s)] for s in range(ns)], axis=0)
        zs.append(_rms(o, go) * _silu(g[:, HG_D * j:HG_D * (j + 1)]))
    return (jnp.stack(sts, axis=0),), (jnp.concatenate(zs, axis=1),)


def _hgrn_args(u4, lb, go):
    lp = u4.shape[2]
    wb = HG_HB * HG_D
    xs = [Arg(u4, (4, None, CH, wb), lambda o, t: (0, o, t, 0))]
    ps = [Arg(lb, (1, wb), lambda o, t: (0, o)), Arg(go, (1, HG_D), lambda o, t: (0, 0), shared=True)]
    return (HG_H // HG_HB, lp // CH), ps, xs


def hgrn_fwd(h, mask, w):
    lp = h.shape[0]
    a = norm_fwd("hgrn_norm", h, w["ng"], mask)
    u4 = lin_bo("hgrn_in", a, w["win"]).reshape(4, HG_H // HG_HB, lp, HG_HB * HG_D)
    grid, ps, xs = _hgrn_args(u4, w["lb"], w["go"])
    (z,), (st,) = seq_fwd("hgrn_core", _hgrn_fn, grid, ps, [], xs,
                          [((lp, D), BF, (CH, HG_HB * HG_D), lambda o, t: (t, o))], carries=[(HG_HB, HG_D, HG_D)])
    h2 = lin("hgrn_out", z, w["wo"], res=h)
    return h2, (h, a, u4, z, st)


def hgrn_bwd(dh2, mask, w, saved):
    h, a, u4, z, st = saved
    lp = h.shape[0]
    g = {}
    g["wo"] = wgrad("hgrn_dwo", z, dh2)
    dz = lin_t("hgrn_dz", dh2, w["wo"])
    grid, ps, xs = _hgrn_args(u4, w["lb"], w["go"])
    (du4,), (g["lb"], g["go"]) = seq_bwd("hgrn_core_b", _hgrn_fn, grid, ps, [], xs,
                                         [Arg(dz, (CH, HG_HB * HG_D), lambda o, t: (t, o))], saved=[st],
                                         carries=[(HG_HB, HG_D, HG_D)])
    du = du4.reshape(N_DEV, lp, HG_HB * HG_D)
    g["win"] = wgrad_bo("hgrn_dwin", a, du)
    da = lin_t_bi("hgrn_da", du, w["win"])
    dh, g["ng"] = norm_bwd("hgrn_norm_b", h, w["ng"], mask, da, dh2)
    return dh, g


S5_W = S5_SG * S5_P


def s5_tables(lam_re, lam_im, log_dt, b_re, b_im, c_re, c_im):
    dt = jnp.exp(log_dt)[:, None]
    mag = jnp.exp(lam_re * dt)
    abar_re = mag * jnp.cos(lam_im * dt)
    abar_im = mag * jnp.sin(lam_im * dt)
    den = lam_re * lam_re + lam_im * lam_im
    zoh_re = ((abar_re - 1.0) * lam_re + abar_im * lam_im) / den
    zoh_im = (abar_im * lam_re - (abar_re - 1.0) * lam_im) / den
    bbar_re = zoh_re[..., None] * b_re - zoh_im[..., None] * b_im
    bbar_im = zoh_re[..., None] * b_im + zoh_im[..., None] * b_re
    eye = jnp.eye(S5_SG, dtype=F32)

    def blockdiag_in(b):
        t = b.reshape(N_DEV, S5_SG, S5_P, S5_K).transpose(0, 1, 3, 2)
        return jnp.einsum("jakp,ab->jakbp", t, eye).reshape(N_DEV, S5_SG * S5_K, S5_W)

    def blockdiag_out(c):
        t = c.reshape(N_DEV, S5_SG, S5_K, S5_P).transpose(0, 1, 3, 2)
        return jnp.einsum("japk,ab->japbk", t, eye).reshape(N_DEV, S5_W, S5_SG * S5_K)

    wb = jnp.concatenate([blockdiag_in(bbar_re), blockdiag_in(bbar_im)], axis=2)
    wc = jnp.concatenate([blockdiag_out(c_re), -blockdiag_out(c_im)], axis=1)

    abar = jnp.concatenate([abar_re.reshape(N_DEV, 1, S5_W), abar_im.reshape(N_DEV, 1, S5_W)], axis=2)
    return wb, wc, abar


def _cmul(ar, ai, xr, xi):
    return ar * xr - ai * xi, ar * xi + ai * xr


def _scan_rows(ar, ai, xr, xi, reverse):
    for s in range(6):
        sh = -(1 << s) if reverse else (1 << s)
        dr, di = _cmul(ar, ai, _shift_raw(xr, sh), _shift_raw(xi, sh))
        xr, xi = xr + dr, xi + di
        ar, ai = ar * ar - ai * ai, 2.0 * ar * ai
    return xr, xi


@jax.custom_vjp
def cscan(ar, ai, br, bi):
    return _scan_rows(ar, ai, br, bi, False)


def _cscan_fwd(ar, ai, br, bi):
    xr, xi = _scan_rows(ar, ai, br, bi, False)
    return (xr, xi), (ar, ai, xr, xi)


def _cscan_bwd(res, g):
    ar, ai, xr, xi = res
    lr, li = _scan_rows(ar, -ai, g[0], g[1], True)
    pr, pi = _shift_raw(xr, 1), _shift_raw(xi, 1)
    dar = jnp.sum(lr * pr + li * pi, axis=0, keepdims=True)
    dai = jnp.sum(li * pr - lr * pi, axis=0, keepdims=True)
    return dar, dai, lr, li


cscan.defvjp(_cscan_fwd, _cscan_bwd)

S5_BB = 8


def _s5_fn(p, c, k, x):
    wb, wc, abar, dsk = p
    (a,) = x
    (x0,) = k
    blocks = range(S5_BB)
    aj = [a[:, 128 * j:128 * (j + 1)] for j in blocks]
    bu = [mm(aj[j], wb[j]) for j in blocks]
    first = lax.broadcasted_iota(jnp.int32, (CH, S5_W), 0) == 0
    last = lax.broadcasted_iota(jnp.int32, (CH, 2 * S5_W), 0) == CH - 1
    xxs, x0n = [], []
    for j in blocks:
        ar, ai = abar[j][:, :S5_W], abar[j][:, S5_W:]
        cr, ci = _cmul(ar, ai, x0[j][:, :S5_W], x0[j][:, S5_W:])
        xr, xi = cscan(ar, ai, bu[j][:, :S5_W] + jnp.where(first, cr, 0.0), bu[j][:, S5_W:] + jnp.where(first, ci, 0.0))
        xx = jnp.concatenate([xr, xi], axis=1)
        x0n.append(jnp.sum(jnp.where(last, xx, 0.0), axis=0, keepdims=True))
        xxs.append(xx)
    y = jnp.concatenate([mm(xxs[j], wc[j]) for j in blocks], axis=1)
    return (jnp.stack(x0n, axis=0),), (jax.nn.gelu(y + dsk * a),)


def _s5_args(a, tb, dsk):
    lp = a.shape[0]
    wb, wc, abar = tb
    ps = [Arg(wb, (S5_BB, 128, 2 * S5_W), lambda o, t: (o, 0, 0)), Arg(wc, (S5_BB, 2 * S5_W, 128), lambda o, t: (o, 0, 0)),
          Arg(abar, (S5_BB, 1, 2 * S5_W), lambda o, t: (o, 0, 0)), Arg(dsk, (1, 128 * S5_BB), lambda o, t: (0, o))]
    xs = [Arg(a, (CH, 128 * S5_BB), lambda o, t: (t, o))]
    return (N_DEV // S5_BB, lp // CH), ps, xs


def _glu_res_fn(p, c, k, x):
    h, vg = x
    return (), ((h + vg[:, :D] * jax.nn.sigmoid(vg[:, D:])) * c[0],)


def _glu_args(h, vg, mask):
    lp = h.shape[0]
    tr = _row_tile(lp)
    row = lambda o, t: (t, 0)
    return (1, lp // tr), [Arg(mask, (tr, 1), row)], [Arg(h, (tr, D), row), Arg(vg, (tr, 2 * D), row)], tr


def s5_fwd(h, mask, w):
    lp = h.shape[0]
    a = norm_fwd("s5_norm", h, w["ng"], mask, out_dtype=F32)
    grid, ps, xs = _s5_args(a, w["tb"], w["dsk"])
    (z,), (st,) = seq_fwd("s5_core", _s5_fn, grid, ps, [], xs,
                          [((lp, D), BF, (CH, 128 * S5_BB), lambda o, t: (t, o))], carries=[(S5_BB, 1, 2 * S5_W)])
    vg = lin("s5_glu", z, w["wglu"])
    grid2, cs, xs2, tr = _glu_args(h, vg, mask)
    (h2,), _ = seq_fwd("s5_res", _glu_res_fn, grid2, [], cs, xs2, [((lp, D), F32, (tr, D), lambda o, t: (t, 0))])
    return h2, (h, a, z, vg, st)


def s5_bwd(dh2, mask, w, saved):
    h, a, z, vg, st = saved
    g = {}
    grid2, cs, xs2, tr = _glu_args(h, vg, mask)
    (dskip, dvg), _ = seq_bwd("s5_res_b", _glu_res_fn, grid2, [], cs, xs2, [Arg(dh2, (tr, D), lambda o, t: (t, 0))])
    g["wglu"] = wgrad("s5_dwglu", z, dvg)
    dz = lin_t("s5_dz", dvg, w["wglu"])
    grid, ps, xs = _s5_args(a, w["tb"], w["dsk"])
    (da,), dps = seq_bwd("s5_core_b", _s5_fn, grid, ps, [], xs, [Arg(dz, (CH, 128 * S5_BB), lambda o, t: (t, o))],
                         saved=[st], carries=[(S5_BB, 1, 2 * S5_W)])
    g["tb"] = tuple(dps[:3])
    g["dsk"] = dps[3]
    dh, g["ng"] = norm_bwd("s5_norm_b", h, w["ng"], mask, da, dskip)
    return dh, g


def _rope_angles(lp, dim):
    pos = np.maximum(np.arange(lp, dtype=np.float32) - PAD, 0.0).astype(np.float32)
    inv = (1.0 / (ROPE_BASE ** (np.arange(0, dim, 2, dtype=np.float32) / dim))).astype(np.float32)
    return (pos[:, None] * inv[None, :]).astype(np.float32)


def ret_consts(lp):
    f = np.float32
    ang = _rope_angles(lp, RET_DK)
    lg = np.log(1.0 - np.exp2(-5.0 - np.arange(RET_H, dtype=f))).astype(f)
    p = np.arange(CH, dtype=f)
    diff = p[:, None] - p[None, :]
    decay = np.where(diff >= 0, np.exp(diff[None] * lg[:, None, None]), 0.0).astype(f)
    qd = np.exp((p[None, :] + 1.0) * lg[:, None])[..., None].astype(f)
    kd = np.exp((CH - 1.0 - p[None, :]) * lg[:, None])[..., None].astype(f)
    cd = np.exp(CH * lg)[:, None, None].astype(f)
    return np.cos(ang).astype(f), np.sin(ang).astype(f), decay, qd, kd, cd


def _ret_fn(p, c, k, x):
    (gn,) = p
    cos, sin, decay, qd, kd, cd = c
    (st,) = k
    (u,) = x
    hd = RET_DK // 2
    qk_w = RET_H * RET_DK
    heads = range(RET_H)

    def rope(t):
        t1, t2 = t[:, :hd], t[:, hd:]
        return jnp.concatenate([t1 * cos - t2 * sin, t1 * sin + t2 * cos], axis=1)

    qr = [rope(u[:, RET_DK * h:RET_DK * (h + 1)]) for h in heads]
    kr = [rope(u[:, qk_w + RET_DK * h:qk_w + RET_DK * (h + 1)]) * (RET_DK ** -0.5) for h in heads]
    v = [u[:, 2 * qk_w + RET_DV * h:2 * qk_w + RET_DV * (h + 1)] for h in heads]
    scores = [mm_nt(qr[h], kr[h]) for h in heads]
    inter = [mm(qr[h] * qd[h], st[h]) for h in heads]
    st_new = jnp.stack([st[h] * cd[h] + mm_tn(kr[h] * kd[h], v[h]) for h in heads], axis=0)
    o = [mm(scores[h] * decay[h], v[h]) + inter[h] for h in heads]
    zs = []
    for h in heads:
        mu = jnp.mean(o[h], axis=-1, keepdims=True)
        var = jnp.mean(jnp.square(o[h] - mu), axis=-1, keepdims=True)
        gate = u[:, 2 * qk_w + RET_H * RET_DV + RET_DV * h:2 * qk_w + RET_H * RET_DV + RET_DV * (h + 1)]
        zs.append((o[h] - mu) * lax.rsqrt(var + EPS) * gn[:, RET_DV * h:RET_DV * (h + 1)] * _silu(gate))
    return (st_new,), (jnp.concatenate(zs, axis=1),)


def _ret_args(u, gn, rc):
    lp, uw = u.shape
    cos, sin, decay, qd, kd, cd = rc
    full = lambda o, t: (0, 0, 0)
    ps = [Arg(gn, (1, RET_H * RET_DV), lambda o, t: (0, 0))]
    cs = [Arg(cos, (CH, RET_DK // 2), lambda o, t: (t, 0)), Arg(sin, (CH, RET_DK // 2), lambda o, t: (t, 0)),
          Arg(decay, (RET_H, CH, CH), full), Arg(qd, (RET_H, CH, 1), full), Arg(kd, (RET_H, CH, 1), full),
          Arg(cd, (RET_H, 1, 1), full)]
    xs = [Arg(u, (CH, uw), lambda o, t: (t, 0))]
    return (1, lp // CH), ps, cs, xs


def ret_fwd(h, mask, w):
    lp = h.shape[0]
    a = norm_fwd("ret_norm", h, w["ng"], mask)
    u = lin("ret_in", a, w["win"])
    grid, ps, cs, xs = _ret_args(u, w["gn"], w["rc"])
    (z,), (st,) = seq_fwd("ret_core", _ret_fn, grid, ps, cs, xs,
                          [((lp, 2 * D), BF, (CH, RET_H * RET_DV), lambda o, t: (t, 0))],
                          carries=[(RET_H, RET_DK, RET_DV)], save_dtype=BF)
    h2 = lin("ret_out", z, w["wo"], res=h)
    return h2, (h, a, u, z, st)


def ret_bwd(dh2, mask, w, saved):
    h, a, u, z, st = saved
    g = {}
    g["wo"] = wgrad("ret_dwo", z, dh2)
    dz = lin_t("ret_dz", dh2, w["wo"])
    grid, ps, cs, xs = _ret_args(u, w["gn"], w["rc"])
    (du,), (g["gn"],) = seq_bwd("ret_core_b", _ret_fn, grid, ps, cs, xs,
                                [Arg(dz, (CH, RET_H * RET_DV), lambda o, t: (t, 0))], saved=[st],
                                carries=[(RET_H, RET_DK, RET_DV)])
    g["win"] = wgrad("ret_dwin", a, du)
    da = lin_t("ret_da", du, w["win"])
    dh, g["ng"] = norm_bwd("ret_norm_b", h, w["ng"], mask, da, dh2)
    return dh, g


def mla_consts(lp):
    ang = _rope_angles(lp, MLA_ROPE)
    cos = np.concatenate([np.cos(ang), np.cos(ang)], axis=1).astype(np.float32)
    sin = np.concatenate([np.sin(ang), np.sin(ang)], axis=1).astype(np.float32)
    hd = MLA_ROPE // 2
    i = np.arange(hd)
    rot = np.zeros((MLA_ROPE, MLA_ROPE), np.float32)
    rot[hd + i, i] = -1.0
    rot[i, hd + i] = 1.0
    return cos, sin, rot


def _mla_prep1_fn(p, c, k, x):
    gq, gkv = p
    (down,) = x
    return (), (_rms(down[:, :MLA_QL], gq), _rms(down[:, MLA_QL:MLA_QL + MLA_KVL], gkv), down[:, MLA_QL + MLA_KVL:])


def _mla_prep2(p, c, x):
    gq, gk = p
    cos, sin, rot = c
    q, kv, kpe = x
    qn = _rms(q, gq)
    qn_n, qn_r = qn[:, :MLA_NOPE], qn[:, MLA_NOPE:]
    qo = jnp.concatenate([qn_n, qn_r * cos + cright(qn_r, rot) * sin], axis=1)
    kn = kv[:, :MLA_NOPE]
    ms = (jnp.sum(kn * kn, axis=-1, keepdims=True) + jnp.sum(kpe * kpe, axis=-1, keepdims=True)) / MLA_QK
    r = lax.rsqrt(ms + EPS)
    kr = kpe * r * gk[:, MLA_NOPE:]
    ko = jnp.concatenate([kn * r * gk[:, :MLA_NOPE], kr * cos + cright(kr, rot) * sin], axis=1)
    return qo, ko, kv[:, MLA_NOPE:]


def _mla_prep2_fn(p, c, k, x):
    return (), _mla_prep2(p, c, x)[:2]


def _mla_prep2_b_fn(p, c, k, x):
    return (), _mla_prep2(p, c, x)


def _prep1_args(down, gq, gkv):
    lp = down.shape[0]
    tr = _row_tile(lp)
    ps = [Arg(gq, (1, MLA_QL), lambda o, t: (0, 0), shared=True), Arg(gkv, (1, MLA_KVL), lambda o, t: (0, 0), shared=True)]
    return (1, lp // tr), ps, [Arg(down, (tr, down.shape[1]), lambda o, t: (t, 0))], tr


def _prep2_args(qraw, kvraw, kpe, gq, gk, mc):
    lp = kpe.shape[0]
    tr = _row_tile(lp)
    cos, sin, rot = mc
    ps = [Arg(gq, (1, MLA_QK), lambda o, t: (0, 0), shared=True), Arg(gk, (1, MLA_QK), lambda o, t: (0, 0), shared=True)]
    cs = [Arg(cos, (tr, MLA_ROPE), lambda o, t: (o, 0)), Arg(sin, (tr, MLA_ROPE), lambda o, t: (o, 0)),
          Arg(rot, (MLA_ROPE, MLA_ROPE), lambda o, t: (0, 0))]
    xs = [Arg(qraw, (None, tr, MLA_QK), lambda o, t: (t, o, 0)), Arg(kvraw, (None, tr, MLA_NOPE + MLA_V), lambda o, t: (t, o, 0)),
          Arg(kpe, (tr, MLA_ROPE), lambda o, t: (o, 0), acc=True)]
    return (lp // tr, MLA_H), ps, cs, xs, tr


ATT_HB = 2


def _attn_tile(lp, fwd=False):
    if fwd and lp % 1040 == 0 and lp > 1040:
        return 1040
    return 832 if (lp % 832 == 0 and lp > 832) else 64


def _attn_mask(qi, ki, ta):
    rows = qi * ta + lax.broadcasted_iota(jnp.int32, (ta, ta), 0)
    cols = ki * ta + lax.broadcasted_iota(jnp.int32, (ta, ta), 1)
    return (cols >= PAD) & ((cols // CH) <= (rows // CH))


def attn_fwd(q, k, kv):
    nh, lp, dq = q.shape
    ta = _attn_tile(lp, fwd=True)
    nb = lp // ta
    scale = MLA_QK ** -0.5

    hb = ATT_HB

    def body(q_ref, k_ref, v_ref, o_ref, lse_ref, m_s, l_s, acc_s):
        qi, ki = pl.program_id(1), pl.program_id(2)

        @pl.when(ki == 0)
        def _():
            m_s[...] = jnp.full_like(m_s, NEG)
            l_s[...] = jnp.zeros_like(l_s)
            acc_s[...] = jnp.zeros_like(acc_s)

        def step(masked):
            ss = [_bdot(q_ref[j], k_ref[j], 1, 1) * scale for j in range(hb)]
            ps = []
            for j in range(hb):
                s = jnp.where(_attn_mask(qi, ki, ta), ss[j], NEG) if masked else ss[j]
                m_new = jnp.maximum(m_s[j], jnp.max(s, axis=-1, keepdims=True))
                p = jnp.exp(s - m_new)
                alpha = jnp.exp(m_s[j] - m_new)
                l_s[j] = alpha * l_s[j] + jnp.sum(p, axis=-1, keepdims=True)
                m_s[j] = m_new
                ps.append((p, alpha))
            for j in range(hb):
                acc_s[j] = ps[j][1] * acc_s[j] + _bdot(ps[j][0], v_ref[j], 1, 0)

        pl.when((ki == qi) | (ki == 0))(functools.partial(step, True))
        pl.when((ki < qi) & (ki > 0))(functools.partial(step, False))

        @pl.when(ki == nb - 1)
        def _():
            for j in range(hb):
                o_ref[:, MLA_V * j:MLA_V * (j + 1)] = (acc_s[j] / l_s[j]).astype(o_ref.dtype)
                lse_ref[j] = m_s[j] + jnp.log(l_s[j])

    return pl.pallas_call(
        body, name="mla_attn", grid=(nh // hb, nb, nb),
        in_specs=[pl.BlockSpec((hb, ta, dq), lambda h, qi, ki: (h, qi, 0)),
                  pl.BlockSpec((hb, ta, dq), lambda h, qi, ki: (h, jnp.minimum(ki, qi), 0)),
                  pl.BlockSpec((hb, ta, MLA_V), lambda h, qi, ki: (h, jnp.minimum(ki, qi), 1))],
        out_specs=[pl.BlockSpec((ta, hb * MLA_V), lambda h, qi, ki: (qi, h)),
                   pl.BlockSpec((hb, ta, 1), lambda h, qi, ki: (h, qi, 0))],
        out_shape=[SDS((lp, nh * MLA_V), BF), SDS((nh, lp, 1), F32)],
        scratch_shapes=[pltpu.VMEM((hb, ta, 1), F32), pltpu.VMEM((hb, ta, 1), F32), pltpu.VMEM((hb, ta, MLA_V), F32)],
        compiler_params=_cp(3))(q, k, kv)


def attn_bwd(q, k, kv, o, do, lse):
    nh, lp, dq = q.shape
    ta = _attn_tile(lp)
    nb = lp // ta
    scale = MLA_QK ** -0.5

    def body(q_ref, k_ref, v_ref, o_ref, do_ref, lse_ref, dq_ref, dk_ref, dv_ref, dk_s, dv_s):
        ki, qi = pl.program_id(1), pl.program_id(2)

        @pl.when((ki == 0) & (qi == 0))
        def _():
            dq_ref[...] = jnp.zeros_like(dq_ref)

        @pl.when(qi == 0)
        def _():
            dk_s[...] = jnp.zeros_like(dk_s)
            dv_s[...] = jnp.zeros_like(dv_s)

        def step(masked):
            dov = do_ref[...]
            s = _bdot(q_ref[...], k_ref[...], 1, 1) * scale
            dp = _bdot(dov, v_ref[...], 1, 1)
            if masked:
                s = jnp.where(_attn_mask(qi, ki, ta), s, NEG)
            p = jnp.exp(s - lse_ref[...])
            delta = jnp.sum(dov * o_ref[...].astype(F32), axis=-1, keepdims=True)
            dv_s[...] += _bdot(p, dov, 0, 0)
            ds = p * (dp - delta) * scale
            rows = pl.ds(pl.multiple_of(qi * ta, ta), ta)
            dq_ref[rows, :] += _bdot(ds, k_ref[...], 1, 0)
            dk_s[...] += _bdot(ds, q_ref[...], 0, 0)

        pl.when((ki == qi) | (ki == 0))(functools.partial(step, True))
        pl.when((ki < qi) & (ki > 0))(functools.partial(step, False))

        @pl.when(qi == nb - 1)
        def _():
            dk_ref[...] = dk_s[...]
            dv_ref[...] = dv_s[...]

    qmap = lambda h, ki, qi: (h, jnp.maximum(qi, ki), 0)
    return pl.pallas_call(
        body, name="mla_attn_b", grid=(nh, nb, nb),
        in_specs=[pl.BlockSpec((None, ta, dq), qmap),
                  pl.BlockSpec((None, ta, dq), lambda h, ki, qi: (h, ki, 0)),
                  pl.BlockSpec((None, ta, MLA_V), lambda h, ki, qi: (h, ki, 1)),
                  pl.BlockSpec((ta, MLA_V), lambda h, ki, qi: (jnp.maximum(qi, ki), h)),
                  pl.BlockSpec((ta, MLA_V), lambda h, ki, qi: (jnp.maximum(qi, ki), h)),
                  pl.BlockSpec((None, ta, 1), qmap)],
        out_specs=[pl.BlockSpec((None, lp, dq), lambda h, ki, qi: (h, 0, 0)),
                   pl.BlockSpec((None, ta, dq), lambda h, ki, qi: (h, ki, 0)),
                   pl.BlockSpec((None, ta, MLA_V), lambda h, ki, qi: (h, ki, 0))],
        out_shape=[SDS((nh, lp, dq), F32), SDS((nh, lp, dq), F32), SDS((nh, lp, MLA_V), F32)],
        scratch_shapes=[pltpu.VMEM((ta, dq), F32), pltpu.VMEM((ta, MLA_V), F32)],
        compiler_params=_cp(3))(q, k, kv, o, do, lse)


def mla_fwd(h, mask, w):
    lp = h.shape[0]
    a = norm_fwd("mla_norm", h, w["ng"], mask)
    down = lin("mla_down", a, w["wdown"])
    grid, ps, xs, tr = _prep1_args(down, w["gcq"], w["gckv"])
    row = lambda o, t: (t, 0)
    (cq, ckv, kpe), _ = seq_fwd("mla_prep1", _mla_prep1_fn, grid, ps, [], xs,
                                [((lp, MLA_QL), BF, (tr, MLA_QL), row), ((lp, MLA_KVL), BF, (tr, MLA_KVL), row),
                                 ((lp, MLA_ROPE), F32, (tr, MLA_ROPE), row)])
    qraw = lin_bo("mla_uq", cq, w["wuq"])
    kvraw = lin_bo("mla_ukv", ckv, w["wukv"])
    grid, ps, cs, xs, tr = _prep2_args(qraw, kvraw, kpe, w["gq"], w["gk"], w["mc"])
    hm = lambda o, t: (t, o, 0)
    (q, k), _ = seq_fwd("mla_prep2", _mla_prep2_fn, grid, ps, cs, xs,
                        [((MLA_H, lp, MLA_QK), BF, (None, tr, MLA_QK), hm), ((MLA_H, lp, MLA_QK), BF, (None, tr, MLA_QK), hm)])
    o, lse = attn_fwd(q, k, kvraw)
    h2 = lin("mla_out", o, w["wo"], res=h)
    return h2, (h, a, down, cq, ckv, kpe, qraw, kvraw, q, k, o, lse)


def mla_bwd(dh2, mask, w, saved, emit):
    h, a, down, cq, ckv, kpe, qraw, kvraw, q, k, o, lse = saved
    lp = h.shape[0]
    g = {}
    g["wo"] = wgrad("mla_dwo", o, dh2)
    do = lin_t("mla_do", dh2, w["wo"])
    do = emit("wo", [g["wo"]], do)
    dq, dk, dv = attn_bwd(q, k, kvraw, o, do, lse)
    grid, ps, cs, xs, tr = _prep2_args(qraw, kvraw, kpe, w["gq"], w["gk"], w["mc"])
    hm = lambda o, t: (t, o, 0)
    (dqraw, dkvraw, dkpe), (g["gq"], g["gk"]) = seq_bwd(
        "mla_prep2_b", _mla_prep2_b_fn, grid, ps, cs, xs,
        [Arg(dq, (None, tr, MLA_QK), hm), Arg(dk, (None, tr, MLA_QK), hm), Arg(dv, (None, tr, MLA_V), hm)])
    g["wuq"] = wgrad_bo("mla_dwuq", cq, dqraw)
    dcq = lin_t_bi("mla_dcq", dqraw, w["wuq"])
    g["wukv"] = wgrad_bo("mla_dwukv", ckv, dkvraw)
    dckv = lin_t_bi("mla_dckv", dkvraw, w["wukv"])
    dckv = emit("wu", [g["wuq"], g["wukv"]], dckv)
    grid, ps, xs, tr = _prep1_args(down, w["gcq"], w["gckv"])
    row = lambda o, t: (t, 0)
    (ddown,), (g["gcq"], g["gckv"]) = seq_bwd(
        "mla_prep1_b", _mla_prep1_fn, grid, ps, [], xs,
        [Arg(dcq, (tr, MLA_QL), row), Arg(dckv, (tr, MLA_KVL), row), Arg(dkpe, (tr, MLA_ROPE), row)])
    g["wdown"] = wgrad("mla_dwdown", a, ddown)
    da = lin_t("mla_da", ddown, w["wdown"])
    dh, g["ng"] = norm_bwd("mla_norm_b", h, w["ng"], mask, da, dh2)
    return dh, g


def loss_head(h, target):
    lp, d = h.shape
    tr = _row_tile(lp)
    tpad = jnp.concatenate([jnp.zeros((OFF, d), F32), target], axis=0)

    def body(h_ref, t_ref, loss_ref, dh_ref):
        i = pl.program_id(0)

        @pl.when(i == 0)
        def _():
            loss_ref[...] = jnp.zeros_like(loss_ref)

        rows = i * tr + lax.broadcasted_iota(jnp.int32, (tr, d), 0)
        e = jnp.where(rows >= OFF, h_ref[...] - t_ref[...], 0.0)
        loss_ref[...] += jnp.sum(e * e) * (0.5 / d)
        dh_ref[...] = e * (1.0 / d)

    blk = pl.BlockSpec((tr, d), lambda i: (i, 0))
    return pl.pallas_call(
        body, name="loss_head", grid=(lp // tr,), in_specs=[blk, blk],
        out_specs=[pl.BlockSpec((8, 128), lambda i: (0, 0)), blk],
        out_shape=[SDS((8, 128), F32), SDS((lp, d), F32)], compiler_params=_cp(1))(h, tpad)


ADAM_LAND_BYTES = 20 * 1024 * 1024


def _adam_tile(r, c, nl):
    if r % 8:
        return r
    best = 8
    for t in range(8, r + 1, 8):
        if r % t == 0 and N_DEV * t * c * 4 * 2 * nl <= ADAM_LAND_BYTES:
            best = t
    return best


def adamw(name, lands, w, m, v):
    nl, r, c = w.shape
    tr = _adam_tile(r, c, nl)
    c1 = 1.0 / (1.0 - ADAM_B1 ** ADAM_STEP)
    c2 = 1.0 / (1.0 - ADAM_B2 ** ADAM_STEP)

    def body(*refs):
        l_refs = refs[:nl]
        w_ref, m_ref, v_ref, g_ref, d_ref, nm_ref, nv_ref = refs[nl:]
        layer = pl.program_id(0)
        for j in range(nl):
            @pl.when(layer == j)
            def _(j=j):
                g = l_refs[j][0]
                for i in range(1, N_DEV):
                    g = g + l_refs[j][i]
                g_ref[...] = g

        g = g_ref[...]
        nm = ADAM_B1 * m_ref[...] + (1.0 - ADAM_B1) * g
        nv = ADAM_B2 * v_ref[...] + (1.0 - ADAM_B2) * (g * g)
        nm_ref[...] = nm
        nv_ref[...] = nv
        d_ref[...] = -ADAM_LR * ((nm * c1) / (jnp.sqrt(nv * c2) + ADAM_EPS) + ADAM_WD * w_ref[...])

    blk = pl.BlockSpec((None, tr, c), lambda l, i: (l, i, 0))
    land_specs = [pl.BlockSpec((N_DEV, tr, c), lambda l, i, j=j: (0, jnp.where(l == j, i, 0), 0)) for j in range(nl)]
    return pl.pallas_call(
        body, name=name, grid=(nl, r // tr), in_specs=land_specs + [blk, blk, blk],
        out_specs=[blk, blk, blk, blk], out_shape=[SDS((nl, r, c), F32)] * 4, compiler_params=_cp(2))(*lands, w, m, v)


ANY = pl.BlockSpec(memory_space=pl.ANY)
MESH = pl.DeviceIdType.MESH


def _me():
    return lax.axis_index("x"), lax.axis_index("y"), lax.axis_index("c")


def _peers():
    x, y, c = _me()
    out = []
    for k in range(1, N_DEV):
        px = 1 - x if k & 4 else x
        py = 1 - y if k & 2 else y
        pc = 1 - c if k & 1 else c
        out.append(((px, py, pc), 4 * px + 2 * py + pc))
    return out


HBM_SPEC = pl.BlockSpec(memory_space=pltpu.HBM)
SEM_SPEC = pl.BlockSpec(memory_space=pltpu.SEMAPHORE)
DATAFLOW = pltpu.SideEffectType.DATAFLOW_SIDE_EFFECTING


def _my_index():
    return 4 * lax.axis_index("x") + 2 * lax.axis_index("y") + lax.axis_index("c")


def _hbm(a):
    return pltpu.with_memory_space_constraint(a, pltpu.HBM)


NP = N_DEV - 1


def _push_copy(x_ref, land_ref, send, recv, pid, src_idx, dst_idx, scatter):
    src = x_ref.at[src_idx] if scatter else x_ref
    return pltpu.make_async_remote_copy(src_ref=src, dst_ref=land_ref.at[dst_idx], send_sem=send, recv_sem=recv,
                                        device_id=pid, device_id_type=MESH)


def push_start(name, xs, me, scatter, carry=None):
    n = len(xs)
    lands = []
    for a in xs:
        own = lax.dynamic_index_in_dim(a, me, 0, keepdims=True) if scatter else a[None]
        z = lax.empty((N_DEV,) + own.shape[1:], a.dtype)
        lands.append(lax.dynamic_update_slice(z, own, (me,) + (0,) * (own.ndim - 1)))
    ns = 2 * NP * n
    ops = xs + lands + ([carry] if carry is not None else [])
    na = len(ops)

    def body(*refs):
        x_refs, land_refs = refs[:n], refs[n:2 * n]
        sems = refs[na:na + ns]
        token = refs[-1]
        x, y, c = _me()
        mine = 4 * x + 2 * y + c
        for i in range(n):
            for k, (pid, pidx) in enumerate(_peers()):
                s = 2 * (NP * i + k)
                _push_copy(x_refs[i], land_refs[i], sems[s], sems[s + 1], pid, pidx, mine, scatter).start()
        token[...] = jnp.zeros_like(token)

    out_shape = ([pltpu.SemaphoreType.DMA(())] * ns + [pltpu.HBM(a.shape, a.dtype) for a in ops]
                 + [SDS((8, 128), F32)])
    res = pl.pallas_call(
        body, name=name, out_shape=out_shape, in_specs=[HBM_SPEC] * na,
        out_specs=[SEM_SPEC] * ns + [HBM_SPEC] * na + [pl.BlockSpec(memory_space=pltpu.VMEM)],
        input_output_aliases={i: ns + i for i in range(na)},
        compiler_params=pltpu.CompilerParams(has_side_effects=DATAFLOW))(*[_hbm(a) for a in ops])
    sems, thru, token = res[:ns], res[ns:-1], res[-1]
    handles = [dict(x=thru[i], land=thru[n + i], sems=list(sems[2 * NP * i:2 * NP * (i + 1)]), token=token)
               for i in range(n)]
    return (handles, thru[2 * n]) if carry is not None else handles


def push_wait(name, hds, after, scatter):
    n = len(hds)
    ns = 2 * NP

    def body(*refs):
        x_refs, land_refs = refs[:n], refs[n:2 * n]
        sems = refs[2 * n:2 * n + ns * n]
        for i in range(n):
            for k, (pid, pidx) in enumerate(_peers()):
                cp = _push_copy(x_refs[i], land_refs[i], sems[ns * i + 2 * k], sems[ns * i + 2 * k + 1], pid, pidx, pidx,
                                scatter)
                cp.wait_send()
                cp.wait_recv()

    arrs = [hd["x"] for hd in hds] + [hd["land"] for hd in hds]
    sems = [s for hd in hds for s in hd["sems"]]
    res = pl.pallas_call(
        body, name=name, out_shape=[pltpu.HBM(a.shape, a.dtype) for a in arrs],
        in_specs=[HBM_SPEC] * (2 * n) + [SEM_SPEC] * (ns * n) + [ANY], out_specs=[HBM_SPEC] * (2 * n),
        input_output_aliases={i: i for i in range(2 * n)},
        compiler_params=pltpu.CompilerParams(has_side_effects=DATAFLOW))(*arrs, *sems, after)
    return list(res[n:])


WEIGHTS = ['meta_tokens', 'norm_mix_g', 'norm_ffn_g', 'mla_w_down', 'mla_cq_norm_g', 'mla_ckv_norm_g', 'mla_w_uq',
           'mla_w_ukv', 'mla_q_head_g', 'mla_k_head_g', 'mla_w_o', 'hgrn_w_in', 'hgrn_lb_logits', 'hgrn_o_norm_g',
           'hgrn_w_o', 's5_lam_re', 's5_lam_im', 's5_log_dt', 's5_b_re', 's5_b_im', 's5_c_re', 's5_c_im', 's5_d',
           's5_w_glu', 'ret_w_in', 'ret_gn_g', 'ret_w_o', 'ffn_w_up', 'ffn_conv_w', 'ffn_conv_b', 'ffn_w_down']
BIG = ['mla_w_down', 'mla_w_uq', 'mla_w_ukv', 'mla_w_o', 'hgrn_w_in', 'hgrn_w_o', 's5_w_glu', 'ret_w_in', 'ret_w_o',
       'ffn_w_up', 'ffn_w_down']
SMALL_SH = ['meta_tokens', 's5_d', 'ret_gn_g', 'ffn_conv_w']
REP_S5 = ['s5_lam_re', 's5_lam_im', 's5_log_dt', 's5_b_re', 's5_b_im', 's5_c_re', 's5_c_im']
REP_REST = ['norm_mix_g', 'norm_ffn_g', 'mla_cq_norm_g', 'mla_ckv_norm_g', 'mla_q_head_g', 'mla_k_head_g',
            'hgrn_lb_logits', 'hgrn_o_norm_g', 'ffn_conv_b']
SMALL_REP = REP_REST + REP_S5
LANE = 128


def _flat(arrs, mult):
    v = jnp.concatenate([a.reshape(-1) for a in arrs])
    pad = (-v.shape[0]) % mult
    return jnp.pad(v, (0, pad)).reshape(-1, LANE)


def _unflat(flat2d, like):
    v = flat2d.reshape(-1)
    out, o = [], 0
    for a in like:
        out.append(v[o:o + a.size].reshape(a.shape))
        o += a.size
    return out


def _lb_of(logits):
    cum = jnp.cumsum(jax.nn.softmax(logits, axis=0), axis=0)
    return (cum - cum[0:1])[1:2]


def _cols_to_blocks(g):
    k, n = g.shape
    return g.reshape(k, N_DEV, n // N_DEV).transpose(1, 0, 2)


def _blocks_to_cols(wb):
    nb, k, n = wb.shape
    return wb.transpose(1, 0, 2).reshape(k, nb * n)


SUBS = ['mla', 'ffn0', 'hgrn', 'ffn1', 's5', 'ffn2', 'ret', 'ffn3']
GROUPS = [[('mla_w_down', 0), ('mla_w_uq', 0), ('mla_w_ukv', 0), ('mla_w_o', 0)],
          [('ffn_w_up', 0), ('ffn_w_down', 0)],
          [('hgrn_w_in', 0), ('hgrn_w_o', 0)],
          [('ffn_w_up', 1), ('ffn_w_down', 1)],
          [('s5_w_glu', 0)],
          [('ffn_w_up', 2), ('ffn_w_down', 2)],
          [('ret_w_in', 0), ('ret_w_o', 0)],
          [('ffn_w_up', 3), ('ffn_w_down', 3)]]


def _pack8(parts, mult):
    v = jnp.concatenate(parts, axis=1)
    return jnp.pad(v, ((0, 0), (0, (-v.shape[1]) % mult))).reshape(N_DEV, -1, LANE)


def _sub_weights(k, got, rep, tabs, lp):
    ngm, ngf = rep['norm_mix_g'], rep['norm_ffn_g']
    if k == 0:
        return dict(ng=ngm[0:1], wdown=got[0].reshape(D, -1), gcq=rep['mla_cq_norm_g'], gckv=rep['mla_ckv_norm_g'],
                    wuq=got[1], wukv=got[2], gq=rep['mla_q_head_g'], gk=rep['mla_k_head_g'], wo=got[3].reshape(D, D),
                    mc=mla_consts(lp))
    if k == 2:
        return dict(ng=ngm[1:2], win=got[0], lb=tabs['lb'], go=rep['hgrn_o_norm_g'], wo=got[1].reshape(D, D))
    if k == 4:
        return dict(ng=ngm[2:3], tb=tabs['tb'], dsk=tabs['s5_d'], wglu=_blocks_to_cols(got[0]))
    if k == 6:
        return dict(ng=ngm[3:4], win=_blocks_to_cols(got[0]), gn=tabs['ret_gn_g'], wo=got[1].reshape(2 * D, D),
                    rc=ret_consts(lp))
    i = k // 2
    return dict(ng=ngf[i:i + 1], up=got[0], cw=tabs['conv_w'][:, i].reshape(2, 4, 3, 1, FFN_B),
                cb=rep['ffn_conv_b'][i].reshape(2, 4, 1, FFN_B), down=got[1].reshape(4, FFN_B, D))


def _sub_grad_blocks(k, g):
    if k == 0:
        parts = [g['wdown'], g['wuq'], g['wukv'], g['wo']]
    elif k == 2:
        parts = [g['win'], g['wo']]
    elif k == 4:
        parts = [_cols_to_blocks(g['wglu'])]
    elif k == 6:
        parts = [_cols_to_blocks(g['win']), g['wo']]
    else:
        parts = [g['up'], g['down']]
    return parts


_FWD = [mla_fwd, None, hgrn_fwd, None, s5_fwd, None, ret_fwd, None]
_BWD = [mla_bwd, None, hgrn_bwd, None, s5_bwd, None, ret_bwd, None]


def _step(args):
    w = {n: args[n] for n in WEIGHTS}
    x2, tgt = args['x'][0], args['loss_target'][0]

    lp = x2.shape[0] + OFF
    me = _my_index()
    mask = _rowmask(lp)
    rep = {n: w[n] for n in SMALL_REP}

    xs, slots = [], []
    for gi, grp in enumerate(GROUPS):
        items = [w[n][l].astype(BF) for n, l in grp] + ([_flat([w[n] for n in SMALL_SH], LANE)] if gi == 0 else [])
        slots.append((len(xs), len(items)))
        xs += items
    gh = push_start("gather_start", xs, me, scatter=False)

    def fetch(gi, after):
        s, cnt = slots[gi]
        return push_wait("gather_wait_" + SUBS[gi], gh[s:s + cnt], after, scatter=False)

    got = fetch(0, x2)
    sm, o, smp = got[-1].reshape(N_DEV, -1), 0, {}
    for n in SMALL_SH:
        smp[n] = sm[:, o:o + w[n].size].reshape((N_DEV,) + w[n].shape)
        o += w[n].size
    meta = smp['meta_tokens'].transpose(1, 0, 2).reshape(N_META, D)
    lb, lb_vjp = jax.vjp(_lb_of, rep['hgrn_lb_logits'])
    s5p = [rep[n][0] for n in ('s5_lam_re', 's5_lam_im', 's5_log_dt', 's5_b_re', 's5_b_im', 's5_c_re', 's5_c_im')]
    tb, tb_vjp = jax.vjp(s5_tables, *s5p)
    tabs = dict(lb=lb, tb=tb, s5_d=smp['s5_d'].reshape(1, D), ret_gn_g=smp['ret_gn_g'].reshape(1, 2 * D),
                conv_w=smp['ffn_conv_w'])
    h = jnp.concatenate([jnp.zeros((PAD, D), F32), meta, x2], axis=0)
    ws, saved = [], []
    for k in range(8):
        if k > 0:
            got = fetch(k, h)
        ws.append(_sub_weights(k, got, rep, tabs, lp))
        if k % 2:
            h, sv = ffn_fwd(k // 2, h, mask, ws[k])
        else:
            h, sv = _FWD[k](h, mask, ws[k])
        saved.append(sv)
    loss, dh = loss_head(h, tgt)

    gs, sh = [None] * 8, [None] * 8
    early = {}

    def emit(tag, grads, carry):
        blocks = [t.reshape((N_DEV, -1) + t.shape[-1:]) if t.ndim == 2 else t for t in grads]
        early[tag], carry = push_start("scatter_start_mla_" + tag, blocks, me, scatter=True, carry=carry)
        return carry

    for k in reversed(range(1, 8)):
        if k % 2:
            dh, gs[k] = ffn_bwd(k // 2, dh, mask, ws[k], saved[k])
        else:
            dh, gs[k] = _BWD[k](dh, mask, ws[k], saved[k])
        blocks = [b.reshape((N_DEV,) + w[n].shape[1:]) for b, (n, _) in zip(_sub_grad_blocks(k, gs[k]), GROUPS[k])]
        sh[k], dh = push_start("scatter_start_" + SUBS[k], blocks, me, scatter=True, carry=dh)
        if k == 4:
            gs5 = _flat(list(tb_vjp(gs[4]['tb'])), 8 * LANE)
            rh_s5, dh = push_start("small_grads_start_s5", [gs5], me, scatter=False, carry=dh)
    dh, gs[0] = mla_bwd(dh, mask, ws[0], saved[0], emit)
    dmeta = dh[PAD:OFF].reshape(N_META, N_DEV, D // N_DEV).transpose(1, 0, 2)
    dcw = jnp.stack([gs[2 * i + 1]['cw'].reshape(N_DEV, 3, FFN_B) for i in range(4)], axis=1)
    last = push_start("scatter_start_mla", [gs[0]['wdown'].reshape(N_DEV, D // N_DEV, -1),
                                            _pack8([t.reshape(N_DEV, -1) for t in (dmeta, gs[4]['dsk'], gs[6]['gn'], dcw)], LANE)],
                      me, scatter=True)
    sh[0] = [last[0], early['wu'][0], early['wu'][1], early['wo'][0], last[1]]
    grad_x = dh[OFF:]

    g_rep = {
        'norm_mix_g': jnp.concatenate([gs[k]['ng'] for k in (0, 2, 4, 6)], axis=0),
        'norm_ffn_g': jnp.concatenate([gs[k]['ng'] for k in (1, 3, 5, 7)], axis=0),
        'mla_cq_norm_g': gs[0]['gcq'], 'mla_ckv_norm_g': gs[0]['gckv'], 'mla_q_head_g': gs[0]['gq'],
        'mla_k_head_g': gs[0]['gk'], 'hgrn_lb_logits': lb_vjp(gs[2]['lb'])[0], 'hgrn_o_norm_g': gs[2]['go'],
        'ffn_conv_b': jnp.stack([gs[k]['cb'].reshape(-1) for k in (1, 3, 5, 7)], axis=0),
    }
    loss_part = loss[0, 0:1]
    grep = _flat([g_rep[n] for n in REP_REST] + [loss_part], 8 * LANE)
    rh = push_start("small_grads_start", [grep], me, scatter=False)

    lands = {n: [None] * w[n].shape[0] for n in BIG}
    res = {}
    late = [n for n, _ in GROUPS[0]]
    for k in reversed(range(1, 8)):
        got = push_wait("scatter_wait_" + SUBS[k], sh[k], grep, scatter=True)
        for (n, l), t in zip(GROUPS[k], got):
            lands[n][l] = t
    for n in BIG:
        if n not in late:
            res[n] = adamw("adam_" + n, lands[n], w[n], args['m_' + n], args['v_' + n])
    after = res['ffn_w_up'][1]
    got = push_wait("scatter_wait_" + SUBS[0], sh[0], after, scatter=True)
    small_land = got[-1]
    (rep_land,) = push_wait("small_grads_wait", rh, after, scatter=False)
    (s5_land,) = push_wait("small_grads_wait_s5", rh_s5, after, scatter=False)
    for (n, _), t in zip(GROUPS[0], got):
        res[n] = adamw("adam_" + n, [t], w[n], args['m_' + n], args['v_' + n])

    def flat_adam(name, land, names, mult, extra=()):
        like = [w[n] for n in names]
        pad = [jnp.zeros_like(e) for e in extra]
        out = adamw(name, [land], _flat(like + pad, mult)[None], _flat([args['m_' + n] for n in names] + pad, mult)[None],
                    _flat([args['v_' + n] for n in names] + pad, mult)[None])
        for n, parts in zip(names, zip(*[_unflat(t, like) for t in out])):
            res[n] = list(parts)
        return out[0]

    flat_adam("adam_small_sharded", small_land, SMALL_SH, LANE)
    flat_adam("adam_s5_replicated", s5_land, REP_S5, 8 * LANE)
    gsum = flat_adam("adam_small_replicated", rep_land, REP_REST, 8 * LANE, extra=[loss_part])
    total = gsum.reshape(-1)[sum(w[n].size for n in REP_REST)]
    outs = [total, grad_x[None]]
    for k in range(4):
        outs += [res[n][k] for n in WEIGHTS]
    return tuple(outs)


def kernel(x, meta_tokens, norm_mix_g, norm_ffn_g, mla_w_down, mla_cq_norm_g, mla_ckv_norm_g, mla_w_uq, mla_w_ukv, mla_q_head_g, mla_k_head_g, mla_w_o, hgrn_w_in, hgrn_lb_logits, hgrn_o_norm_g, hgrn_w_o, s5_lam_re, s5_lam_im, s5_log_dt, s5_b_re, s5_b_im, s5_c_re, s5_c_im, s5_d, s5_w_glu, ret_w_in, ret_gn_g, ret_w_o, ffn_w_up, ffn_conv_w, ffn_conv_b, ffn_w_down, loss_target, m_meta_tokens, m_norm_mix_g, m_norm_ffn_g, m_mla_w_down, m_mla_cq_norm_g, m_mla_ckv_norm_g, m_mla_w_uq, m_mla_w_ukv, m_mla_q_head_g, m_mla_k_head_g, m_mla_w_o, m_hgrn_w_in, m_hgrn_lb_logits, m_hgrn_o_norm_g, m_hgrn_w_o, m_s5_lam_re, m_s5_lam_im, m_s5_log_dt, m_s5_b_re, m_s5_b_im, m_s5_c_re, m_s5_c_im, m_s5_d, m_s5_w_glu, m_ret_w_in, m_ret_gn_g, m_ret_w_o, m_ffn_w_up, m_ffn_conv_w, m_ffn_conv_b, m_ffn_w_down, v_meta_tokens, v_norm_mix_g, v_norm_ffn_g, v_mla_w_down, v_mla_cq_norm_g, v_mla_ckv_norm_g, v_mla_w_uq, v_mla_w_ukv, v_mla_q_head_g, v_mla_k_head_g, v_mla_w_o, v_hgrn_w_in, v_hgrn_lb_logits, v_hgrn_o_norm_g, v_hgrn_w_o, v_s5_lam_re, v_s5_lam_im, v_s5_log_dt, v_s5_b_re, v_s5_b_im, v_s5_c_re, v_s5_c_im, v_s5_d, v_s5_w_glu, v_ret_w_in, v_ret_gn_g, v_ret_w_o, v_ffn_w_up, v_ffn_conv_w, v_ffn_conv_b, v_ffn_w_down):
    return _step(dict(locals()))
```
